```python
import jax, jax.numpy as jnp
from jax import lax
import numpy as np

D_MODEL = 1024
BATCH = 8
SEQ = 8192
DEPTH = 4

N_A = DEPTH // 2
N_B = DEPTH - N_A
N_HEADS = 16
HEAD_DIM = D_MODEL // N_HEADS
CONV_WIDTH = 3
D_FF = ((8 * D_MODEL // 3 + 255) // 256) * 256
DILATED_GROUPS = ((128, 1), (512, 4), (2048, 16))
BLOCK = 128
EPS = 1e-6

kernel_name = "yoco_shortconv_dilated_alibi_trunk"


def _rms_norm(x, g):
    xf = x.astype(jnp.float32)
    y = xf * lax.rsqrt(jnp.mean(xf * xf, axis=-1, keepdims=True) + EPS)
    return (y * g.astype(jnp.float32)).astype(x.dtype)


def _swiglu(x, w_in, w_out):
    gate, up = jnp.split(x @ w_in, 2, axis=-1)
    return (jax.nn.silu(gate) * up) @ w_out


def _short_conv(x, w_in, w_conv, w_out):
    b_gate, c_gate, h = jnp.split(x @ w_in, 3, axis=-1)
    u = c_gate * h
    u = lax.conv_general_dilated(
        u, w_conv[:, None, :].astype(u.dtype), window_strides=(1,),
        padding=[(CONV_WIDTH - 1, 0)],
        dimension_numbers=("NWC", "WIO", "NWC"),
        feature_group_count=D_MODEL)
    return (b_gate * u) @ w_out


def _alibi_slopes():
    h = np.arange(N_HEADS, dtype=np.float32) + 1.0
    return jnp.asarray(np.power(2.0, -8.0 * h / N_HEADS), dtype=jnp.float32)


def _strided_blocks(t, dilation):
    b, s, h, e = t.shape
    mult = dilation * BLOCK
    sp = -(-s // mult) * mult
    t = jnp.pad(t, ((0, 0), (0, sp - s), (0, 0), (0, 0)))
    return t.reshape(b, sp // mult, BLOCK, dilation, h, e)


def _with_prev_block(tb):
    prev = jnp.concatenate([jnp.zeros_like(tb[:, :1]), tb[:, :-1]], axis=1)
    return jnp.concatenate([prev, tb], axis=2)


def _shared_kv(h, g_kv, w_kv):
    b, s, _ = h.shape
    k, v = jnp.split(_rms_norm(h, g_kv) @ w_kv, 2, axis=-1)
    k = k.reshape(b, s, N_HEADS, HEAD_DIM)
    v = v.reshape(b, s, N_HEADS, HEAD_DIM)
    return [(_with_prev_block(_strided_blocks(k, d)), _with_prev_block(_strided_blocks(v, d)))
            for (_, d) in DILATED_GROUPS]


def _branch_attention(q, k_cat, v_cat, slopes, window, dilation):
    b, s, h, e = q.shape
    qb = _strided_blocks(q, dilation)
    nb = qb.shape[1]
    sc = jnp.einsum("bnqrhe,bnkrhe->bnrhqk", qb, k_cat).astype(jnp.float32)
    u = jnp.arange(BLOCK)[:, None]
    kk = jnp.arange(2 * BLOCK)[None, :]
    delta = u + BLOCK - kk
    band = (delta >= 0) & (delta <= window // dilation)
    first = (jnp.arange(nb) == 0)[:, None, None]
    valid = band[None] & ~(first & (kk < BLOCK)[None])
    bias = -slopes[:, None, None] * (delta * dilation).astype(jnp.float32)[None]
    sc = jnp.where(valid[:, None, None], sc + bias, -jnp.inf)
    m = jnp.max(sc, axis=-1, keepdims=True)
    p = jnp.exp(sc - m)
    den = jnp.sum(p, axis=-1, keepdims=True)
    o = jnp.einsum("bnrhqk,bnkrhe->bnqrhe", p / den, v_cat.astype(jnp.float32))
    lse = (m + jnp.log(den))[..., 0]
    o = o.reshape(b, nb * BLOCK * dilation, h, e)[:, :s]
    lse = lse.transpose(0, 1, 4, 2, 3).reshape(b, nb * BLOCK * dilation, h)[:, :s]
    return o, lse


def _dilated_attention(x, w_q, w_o, shared, slopes):
    b, s, _ = x.shape
    q = (x @ w_q).reshape(b, s, N_HEADS, HEAD_DIM) * (HEAD_DIM ** -0.5)
    outs, lses = [], []
    for (window, dil), (k_cat, v_cat) in zip(DILATED_GROUPS, shared):
        o, l = _branch_attention(q, k_cat, v_cat, slopes, window, dil)
        outs.append(o)
        lses.append(l)
    wts = jax.nn.softmax(jnp.stack(lses), axis=0)
    o = jnp.sum(wts[..., None] * jnp.stack(outs), axis=0)
    return o.reshape(b, s, D_MODEL).astype(x.dtype) @ w_o


def _fwd_setup_inputs(seed: int = 0) -> dict:
    key = jax.random.key(seed)
    ks = jax.random.split(key, 12)
    f32 = jnp.float32
    nrm = lambda k, shape, fan_in: jax.random.normal(k, shape, f32) * (fan_in ** -0.5)
    return {
        "x": jax.random.normal(ks[0], (BATCH, SEQ, D_MODEL), f32),
        "norm_g": 1.0 + 0.05 * jax.random.normal(ks[1], (DEPTH, 4, D_MODEL), f32),
        "conv_in_w": nrm(ks[2], (N_A, D_MODEL, 3 * D_MODEL), D_MODEL),
        "conv_w": nrm(ks[3], (N_A, CONV_WIDTH, D_MODEL), CONV_WIDTH),
        "conv_out_w": nrm(ks[4], (N_A, D_MODEL, D_MODEL), D_MODEL),
        "kv_norm_g": 1.0 + 0.05 * jax.random.normal(ks[5], (D_MODEL,), f32),
        "kv_w": nrm(ks[6], (D_MODEL, 2 * D_MODEL), D_MODEL),
        "q_w": nrm(ks[7], (N_B, D_MODEL, D_MODEL), D_MODEL),
        "o_w": nrm(ks[8], (N_B, D_MODEL, D_MODEL), D_MODEL),
        "ffn_in_w": nrm(ks[9], (DEPTH, D_MODEL, 2 * D_FF), D_MODEL),
        "ffn_out_w": nrm(ks[10], (DEPTH, D_FF, D_MODEL), D_FF),
    }


def _fwd_reference(x, norm_g, conv_in_w, conv_w, conv_out_w, kv_norm_g, kv_w, q_w, o_w,
              ffn_in_w, ffn_out_w):
    slopes = _alibi_slopes()
    shared = None
    for layer in range(DEPTH):
        g = norm_g[layer]
        xn = _rms_norm(x, g[0])
        if layer < N_A:
            mix = _short_conv(xn, conv_in_w[layer], conv_w[layer], conv_out_w[layer])
        else:
            if shared is None:
                shared = _shared_kv(x, kv_norm_g, kv_w)
            j = layer - N_A
            mix = _dilated_attention(xn, q_w[j], o_w[j], shared, slopes)
        x = x + _rms_norm(mix, g[1])
        ff = _swiglu(_rms_norm(x, g[2]), ffn_in_w[layer], ffn_out_w[layer])
        x = x + _rms_norm(ff, g[3])
    return x


import jax as _jax
import jax.numpy as _jnp

TWIN_FORMAT = 'train_step'
FWD_PARAMS = ['x', 'norm_g', 'conv_in_w', 'conv_w', 'conv_out_w', 'kv_norm_g', 'kv_w', 'q_w', 'o_w', 'ffn_in_w', 'ffn_out_w']
TWIN_WEIGHTS = ['norm_g', 'conv_in_w', 'conv_w', 'conv_out_w', 'kv_norm_g', 'kv_w', 'q_w', 'o_w', 'ffn_in_w', 'ffn_out_w']
TWIN_DIFF_INPUT = 'x'
TWIN_INPUTS = ['x', 'norm_g', 'conv_in_w', 'conv_w', 'conv_out_w', 'kv_norm_g', 'kv_w', 'q_w', 'o_w', 'ffn_in_w', 'ffn_out_w', 'loss_target', 'm_norm_g', 'm_conv_in_w', 'm_conv_w', 'm_conv_out_w', 'm_kv_norm_g', 'm_kv_w', 'm_q_w', 'm_o_w', 'm_ffn_in_w', 'm_ffn_out_w', 'v_norm_g', 'v_conv_in_w', 'v_conv_w', 'v_conv_out_w', 'v_kv_norm_g', 'v_kv_w', 'v_q_w', 'v_o_w', 'v_ffn_in_w', 'v_ffn_out_w']
TWIN_OUTPUTS = ['loss', 'grad_x', 'grad_norm_g', 'grad_conv_in_w', 'grad_conv_w', 'grad_conv_out_w', 'grad_kv_norm_g', 'grad_kv_w', 'grad_q_w', 'grad_o_w', 'grad_ffn_in_w', 'grad_ffn_out_w', 'delta_norm_g', 'delta_conv_in_w', 'delta_conv_w', 'delta_conv_out_w', 'delta_kv_norm_g', 'delta_kv_w', 'delta_q_w', 'delta_o_w', 'delta_ffn_in_w', 'delta_ffn_out_w', 'new_m_norm_g', 'new_m_conv_in_w', 'new_m_conv_w', 'new_m_conv_out_w', 'new_m_kv_norm_g', 'new_m_kv_w', 'new_m_q_w', 'new_m_o_w', 'new_m_ffn_in_w', 'new_m_ffn_out_w', 'new_v_norm_g', 'new_v_conv_in_w', 'new_v_conv_w', 'new_v_conv_out_w', 'new_v_kv_norm_g', 'new_v_kv_w', 'new_v_q_w', 'new_v_o_w', 'new_v_ffn_in_w', 'new_v_ffn_out_w']
TWIN_LEAF_KINDS = {'loss': 'loss', 'grad_x': 'grad_x', 'grad_norm_g': 'grad_w', 'grad_conv_in_w': 'grad_w', 'grad_conv_w': 'grad_w', 'grad_conv_out_w': 'grad_w', 'grad_kv_norm_g': 'grad_w', 'grad_kv_w': 'grad_w', 'grad_q_w': 'grad_w', 'grad_o_w': 'grad_w', 'grad_ffn_in_w': 'grad_w', 'grad_ffn_out_w': 'grad_w', 'delta_norm_g': 'delta_w', 'delta_conv_in_w': 'delta_w', 'delta_conv_w': 'delta_w', 'delta_conv_out_w': 'delta_w', 'delta_kv_norm_g': 'delta_w', 'delta_kv_w': 'delta_w', 'delta_q_w': 'delta_w', 'delta_o_w': 'delta_w', 'delta_ffn_in_w': 'delta_w', 'delta_ffn_out_w': 'delta_w', 'new_m_norm_g': 'new_m', 'new_m_conv_in_w': 'new_m', 'new_m_conv_w': 'new_m', 'new_m_conv_out_w': 'new_m', 'new_m_kv_norm_g': 'new_m', 'new_m_kv_w': 'new_m', 'new_m_q_w': 'new_m', 'new_m_o_w': 'new_m', 'new_m_ffn_in_w': 'new_m', 'new_m_ffn_out_w': 'new_m', 'new_v_norm_g': 'new_v', 'new_v_conv_in_w': 'new_v', 'new_v_conv_w': 'new_v', 'new_v_conv_out_w': 'new_v', 'new_v_kv_norm_g': 'new_v', 'new_v_kv_w': 'new_v', 'new_v_q_w': 'new_v', 'new_v_o_w': 'new_v', 'new_v_ffn_in_w': 'new_v', 'new_v_ffn_out_w': 'new_v'}


def _forward(args):
    return _fwd_reference(*[args[k] for k in FWD_PARAMS])


def _output_shape():
    def fwd():
        inp = _fwd_setup_inputs(0)
        return _fwd_reference(*[inp[k] for k in FWD_PARAMS])
    out = _jax.eval_shape(fwd)
    return out.shape, out.dtype

N_MICROBATCH = 1
ADAM_LR = 0.001
ADAM_B1 = 0.9
ADAM_B2 = 0.999
ADAM_EPS = 1e-08
ADAM_WD = 0.01
ADAM_STEP = 10
PER_EXAMPLE_BATCH_AXIS = {'x': 0, 'loss_target': 0}
SHARED_INPUTS = []
_WEIGHT_DTYPES = {'norm_g': _jnp.float32, 'conv_in_w': _jnp.float32, 'conv_w': _jnp.float32, 'conv_out_w': _jnp.float32, 'kv_norm_g': _jnp.float32, 'kv_w': _jnp.float32, 'q_w': _jnp.float32, 'o_w': _jnp.float32, 'ffn_in_w': _jnp.float32, 'ffn_out_w': _jnp.float32}
MOMENT_SCALE = {'norm_g': 4.409308e+01, 'conv_in_w': 2.527615e+00, 'conv_w': 2.650895e+00, 'conv_out_w': 2.755386e+00, 'kv_norm_g': 4.469795e+00, 'kv_w': 2.855421e+00, 'q_w': 6.461136e-01, 'o_w': 3.515071e+00, 'ffn_in_w': 9.817231e-01, 'ffn_out_w': 1.839935e+00}


def _to_microbatches(a, axis):
    t = _jnp.moveaxis(a, axis, 0)
    t = t.reshape((N_MICROBATCH, t.shape[0] // N_MICROBATCH) + t.shape[1:])
    return _jnp.moveaxis(t, 1, axis + 1)


def setup_inputs(seed: int = 0) -> dict:
    inp = _fwd_setup_inputs(seed)
    key = _jax.random.fold_in(_jax.random.key(seed), 7919)
    shape, _ = _output_shape()
    out = dict(inp)
    out["loss_target"] = _jax.random.normal(_jax.random.fold_in(key, 0), shape, _jnp.float32)
    for i, name in enumerate(TWIN_WEIGHTS):
        w = inp[name].astype(_jnp.float32)
        if MOMENT_SCALE is None:
            s = _jnp.sqrt(_jnp.mean(_jnp.square(w)) + 1e-30)
        else:
            s = MOMENT_SCALE[name]
        km, kv = _jax.random.split(_jax.random.fold_in(key, i + 1))
        out[name] = w
        out["m_" + name] = s * _jax.random.normal(km, w.shape, _jnp.float32)
        out["v_" + name] = (s * s) * _jax.random.uniform(kv, w.shape, _jnp.float32, 0.5, 1.5)
    if N_MICROBATCH > 1:
        for name, axis in PER_EXAMPLE_BATCH_AXIS.items():
            out[name] = _to_microbatches(out[name], axis)
    return {'x': out['x'], 'norm_g': out['norm_g'], 'conv_in_w': out['conv_in_w'], 'conv_w': out['conv_w'], 'conv_out_w': out['conv_out_w'], 'kv_norm_g': out['kv_norm_g'], 'kv_w': out['kv_w'], 'q_w': out['q_w'], 'o_w': out['o_w'], 'ffn_in_w': out['ffn_in_w'], 'ffn_out_w': out['ffn_out_w'], 'loss_target': out['loss_target'], 'm_norm_g': out['m_norm_g'], 'm_conv_in_w': out['m_conv_in_w'], 'm_conv_w': out['m_conv_w'], 'm_conv_out_w': out['m_conv_out_w'], 'm_kv_norm_g': out['m_kv_norm_g'], 'm_kv_w': out['m_kv_w'], 'm_q_w': out['m_q_w'], 'm_o_w': out['m_o_w'], 'm_ffn_in_w': out['m_ffn_in_w'], 'm_ffn_out_w': out['m_ffn_out_w'], 'v_norm_g': out['v_norm_g'], 'v_conv_in_w': out['v_conv_in_w'], 'v_conv_w': out['v_conv_w'], 'v_conv_out_w': out['v_conv_out_w'], 'v_kv_norm_g': out['v_kv_norm_g'], 'v_kv_w': out['v_kv_w'], 'v_q_w': out['v_q_w'], 'v_o_w': out['v_o_w'], 'v_ffn_in_w': out['v_ffn_in_w'], 'v_ffn_out_w': out['v_ffn_out_w']}


def _loss(weights, diff, rest, loss_target):
    with _jax.named_scope("forward"):
        args = {**rest, TWIN_DIFF_INPUT: diff, **{k: w.astype(_WEIGHT_DTYPES[k]) for k, w in weights.items()}}
        y = _forward(args)
    with _jax.named_scope("loss_head"):
        err = _jnp.square(y.astype(_jnp.float32) - loss_target)
        return 0.5 * _jnp.sum(_jnp.mean(err, axis=-1)) if err.ndim else 0.5 * err


def _adamw(w, g, m, v):
    m = ADAM_B1 * m + (1.0 - ADAM_B1) * g
    v = ADAM_B2 * v + (1.0 - ADAM_B2) * _jnp.square(g)
    m_hat = m / (1.0 - ADAM_B1 ** ADAM_STEP)
    v_hat = v / (1.0 - ADAM_B2 ** ADAM_STEP)
    delta = -ADAM_LR * (m_hat / (_jnp.sqrt(v_hat) + ADAM_EPS) + ADAM_WD * w)
    return delta, m, v


def reference(x, norm_g, conv_in_w, conv_w, conv_out_w, kv_norm_g, kv_w, q_w, o_w, ffn_in_w, ffn_out_w, loss_target, m_norm_g, m_conv_in_w, m_conv_w, m_conv_out_w, m_kv_norm_g, m_kv_w, m_q_w, m_o_w, m_ffn_in_w, m_ffn_out_w, v_norm_g, v_conv_in_w, v_conv_w, v_conv_out_w, v_kv_norm_g, v_kv_w, v_q_w, v_o_w, v_ffn_in_w, v_ffn_out_w):
    given = dict(x=x, norm_g=norm_g, conv_in_w=conv_in_w, conv_w=conv_w, conv_out_w=conv_out_w, kv_norm_g=kv_norm_g, kv_w=kv_w, q_w=q_w, o_w=o_w, ffn_in_w=ffn_in_w, ffn_out_w=ffn_out_w, loss_target=loss_target, m_norm_g=m_norm_g, m_conv_in_w=m_conv_in_w, m_conv_w=m_conv_w, m_conv_out_w=m_conv_out_w, m_kv_norm_g=m_kv_norm_g, m_kv_w=m_kv_w, m_q_w=m_q_w, m_o_w=m_o_w, m_ffn_in_w=m_ffn_in_w, m_ffn_out_w=m_ffn_out_w, v_norm_g=v_norm_g, v_conv_in_w=v_conv_in_w, v_conv_w=v_conv_w, v_conv_out_w=v_conv_out_w, v_kv_norm_g=v_kv_norm_g, v_kv_w=v_kv_w, v_q_w=v_q_w, v_o_w=v_o_w, v_ffn_in_w=v_ffn_in_w, v_ffn_out_w=v_ffn_out_w)
    weights = {n: given[n] for n in TWIN_WEIGHTS}
    shared = {n: given[n] for n in SHARED_INPUTS}
    per_example = {n: given[n] for n in ['x']}
    grad_fn = _jax.value_and_grad(_loss, argnums=(0, 1))

    def one_microbatch(ex, loss_target):
        ex = dict(ex)
        diff = ex.pop(TWIN_DIFF_INPUT)
        return grad_fn(weights, diff, {**shared, **ex}, loss_target)

    if N_MICROBATCH == 1:
        loss, (grad_w, grad_x) = one_microbatch(per_example, given["loss_target"])
    else:
        def body(carry, xs):
            loss_sum, grad_sum = carry
            l_k, (gw_k, gx_k) = one_microbatch(xs[0], xs[1])
            with _jax.named_scope("update"):
                return (loss_sum + l_k, _jax.tree.map(_jnp.add, grad_sum, gw_k)), gx_k

        init = (_jnp.zeros((), _jnp.float32), _jax.tree.map(_jnp.zeros_like, weights))
        (loss, grad_w), grad_x = _jax.lax.scan(body, init, (per_example, given["loss_target"]))
    with _jax.named_scope("update"):
        delta_w, new_m, new_v = {}, {}, {}
        for n in TWIN_WEIGHTS:
            delta_w[n], new_m[n], new_v[n] = _adamw(weights[n], grad_w[n], given["m_" + n], given["v_" + n])
    return (loss, grad_x, *[grad_w[n] for n in TWIN_WEIGHTS], *[delta_w[n] for n in TWIN_WEIGHTS],
            *[new_m[n] for n in TWIN_WEIGHTS], *[new_v[n] for n in TWIN_WEIGHTS])
```

```python
import functools
import math

import numpy as np
import jax
import jax.numpy as jnp
from jax import lax
from jax.experimental import pallas as pl
from jax.experimental.pallas import tpu as pltpu

F32 = jnp.float32
BF16 = jnp.bfloat16

N_DEV = 8
RMS_EPS = 1e-6
HEAD_DIM = 64
LANES = 128
ATT_BLOCK = 128
DILATIONS = (1, 4, 16)
SUPER = ATT_BLOCK * DILATIONS[-1]
NEG = -1e30

ADAM_LR, ADAM_B1, ADAM_B2, ADAM_EPS, ADAM_WD, ADAM_STEP = 0.001, 0.9, 0.999, 1e-08, 0.01, 10

ROW_TILE = 512
MESH = pl.DeviceIdType.MESH


def _call(body, *, name, grid=None, in_specs=None, out_specs=None, out_shape=None, scratch_shapes=(), prefetch=False,
          **params):
    cp = pltpu.CompilerParams(**params) if params else None
    if prefetch:
        spec = pltpu.PrefetchScalarGridSpec(num_scalar_prefetch=1, grid=grid, in_specs=in_specs, out_specs=out_specs,
                                            scratch_shapes=list(scratch_shapes))
        return pl.pallas_call(body, name=name, grid_spec=spec, out_shape=out_shape, compiler_params=cp)
    kwargs = {k: v for k, v in (("grid", grid), ("in_specs", in_specs), ("out_specs", out_specs)) if v is not None}
    return pl.pallas_call(body, name=name, out_shape=out_shape, scratch_shapes=list(scratch_shapes),
                          compiler_params=cp, **kwargs)


def _sds(shape, dtype):
    return jax.ShapeDtypeStruct(tuple(shape), dtype)


def _rms(x, g):
    r = lax.rsqrt(jnp.mean(x * x, axis=-1, keepdims=True) + RMS_EPS)
    return x * r * g


def _rms_bwd(x, g, dy):
    r = lax.rsqrt(jnp.mean(x * x, axis=-1, keepdims=True) + RMS_EPS)
    xh = x * r
    dxh = dy * g
    dx = r * (dxh - xh * jnp.mean(dxh * xh, axis=-1, keepdims=True))
    return dx, jnp.sum(dy * xh, axis=0, keepdims=True)


def _dot(a, b):
    return jnp.dot(a, b, preferred_element_type=F32)


def _dot_nt(a, b):
    return lax.dot_general(a, b, (((1,), (1,)), ((), ())), preferred_element_type=F32)


def _dot_tn(a, b):
    return lax.dot_general(a, b, (((0,), (0,)), ((), ())), preferred_element_type=F32)


def _mesh_pos():
    return lax.axis_index("x"), lax.axis_index("y"), lax.axis_index("c")


def _all_gather(arrs, name):
    n = len(arrs)

    def body(*refs):
        ins, outs = refs[:n], refs[n:2 * n]
        send_sems, recv_sems, local_sems = refs[2 * n:]
        x, y, c = _mesh_pos()
        me, sibling = (x, y, c), (x, y, 1 - c)
        chips = [(1 - x, y), (x, 1 - y), (1 - x, 1 - y)]

        def copy(a, k, block, to, src=None):
            dst = outs[a].at[4 * block[0] + 2 * block[1] + block[2]]
            return pltpu.make_async_remote_copy(
                src_ref=dst if src is None else src, dst_ref=dst, send_sem=send_sems.at[a, k],
                recv_sem=recv_sems.at[a, k], device_id=to, device_id_type=MESH)

        started = []
        for a in range(n):
            mine = pltpu.make_async_copy(ins[a], outs[a].at[4 * x + 2 * y + c], local_sems.at[a])
            mine.start()
            started.append(mine)
        first = []
        for a in range(n):
            first.append(copy(a, 0, me, sibling, src=ins[a]))
            first += [copy(a, 1 + j, me, (*chip, c), src=ins[a]) for j, chip in enumerate(chips)]
        for cp in first:
            cp.start()
        passed = []
        for a in range(n):
            for j, chip in enumerate(chips):
                copy(a, 1 + j, (*chip, c), me).wait_recv()
                fwd = copy(a, 4 + j, (*chip, c), sibling)
                fwd.start()
                passed.append(fwd)
        for a in range(n):
            copy(a, 0, sibling, me).wait_recv()
            for j, chip in enumerate(chips):
                copy(a, 4 + j, (*chip, 1 - c), me).wait_recv()
        for cp in first + passed:
            cp.wait_send()
        for cp in started:
            cp.wait()

    any_spec = pl.BlockSpec(memory_space=pl.ANY)
    outs = _call(
        body, name=name, in_specs=[any_spec] * n, out_specs=[any_spec] * n,
        out_shape=[_sds((N_DEV,) + a.shape, a.dtype) for a in arrs],
        scratch_shapes=[pltpu.SemaphoreType.DMA((n, 7)), pltpu.SemaphoreType.DMA((n, 7)), pltpu.SemaphoreType.DMA((n,))],
        has_side_effects=True,
    )(*arrs)
    return list(outs)


def _exchange_sibling(grads, name):
    n = len(grads)

    def body(*refs):
        ins, outs = refs[:n], refs[n:2 * n]
        send_sems, recv_sems = refs[2 * n:]
        x, y, c = _mesh_pos()
        sibling = (x, y, 1 - c)
        copies = []
        for a in range(n):
            for q in range(4):
                cp = pltpu.make_async_remote_copy(
                    src_ref=ins[a].at[2 * q + (1 - c)], dst_ref=outs[a].at[q], send_sem=send_sems.at[a, q],
                    recv_sem=recv_sems.at[a, q], device_id=sibling, device_id_type=MESH)
                cp.start()
                copies.append(cp)
        for cp in copies:
            cp.wait_recv()
        for cp in copies:
            cp.wait_send()

    any_spec = pl.BlockSpec(memory_space=pl.ANY)
    outs = _call(
        body, name=name, in_specs=[any_spec] * n, out_specs=[any_spec] * n,
        out_shape=[_sds((4,) + g.shape[1:], g.dtype) for g in grads],
        scratch_shapes=[pltpu.SemaphoreType.DMA((n, 4)), pltpu.SemaphoreType.DMA((n, 4))],
        has_side_effects=True,
    )(*grads)
    return list(outs)


def _exchange_chips(parts, name):
    n = len(parts)

    def body(*refs):
        ins, outs = refs[:n], refs[n:2 * n]
        send_sems, recv_sems = refs[2 * n:]
        x, y, c = _mesh_pos()
        chips = [(1 - x, y), (x, 1 - y), (1 - x, 1 - y)]
        copies = []
        for a in range(n):
            for k, chip in enumerate(chips):
                cp = pltpu.make_async_remote_copy(
                    src_ref=ins[a].at[2 * chip[0] + chip[1]], dst_ref=outs[a].at[k], send_sem=send_sems.at[a, k],
                    recv_sem=recv_sems.at[a, k], device_id=(*chip, c), device_id_type=MESH)
                cp.start()
                copies.append(cp)
        for cp in copies:
            cp.wait_recv()
        for cp in copies:
            cp.wait_send()

    any_spec = pl.BlockSpec(memory_space=pl.ANY)
    outs = _call(
        body, name=name, in_specs=[any_spec] * n, out_specs=[any_spec] * n,
        out_shape=[_sds((3,) + p.shape[1:], p.dtype) for p in parts],
        scratch_shapes=[pltpu.SemaphoreType.DMA((n, 3)), pltpu.SemaphoreType.DMA((n, 3))],
        has_side_effects=True,
    )(*parts)
    return list(outs)


def _row_tile(rows, cap=512):
    t = min(rows, cap)
    while rows % t or (t % 16 and t != rows):
        t -= 1
    return t


def _as2d(a):
    return a.reshape(-1, a.shape[-1])


def _cast_bf16(w, name):
    w2 = _as2d(w)
    rows, cols = w2.shape
    tr = _row_tile(rows)

    def body(w_ref, o_ref):
        o_ref[...] = w_ref[...].astype(BF16)

    spec = pl.BlockSpec((tr, cols), lambda i: (i, 0))
    out = _call(body, name=name, grid=(rows // tr,), in_specs=[spec], out_specs=spec,
                out_shape=_sds(w2.shape, BF16))(w2)
    return out.reshape(w.shape)


def _add_pairs(grad, from_sibling, name):
    blk = grad.shape[1:]
    g2 = grad.reshape(4, 2, -1, blk[-1])
    s2 = from_sibling.reshape(4, -1, blk[-1])
    rows, cols = s2.shape[1:]
    tr = _row_tile(rows)
    core = lax.axis_index("c").astype(jnp.int32).reshape(1)

    def body(c_ref, g_ref, s_ref, o_ref):
        o_ref[...] = (g_ref[...].astype(F32) + s_ref[...].astype(F32)).astype(BF16)

    out = _call(
        body, name=name, grid=(4, rows // tr), prefetch=True,
        in_specs=[pl.BlockSpec((None, None, tr, cols), lambda q, i, c_ref: (q, c_ref[0], i, 0)),
                  pl.BlockSpec((None, tr, cols), lambda q, i, c_ref: (q, i, 0))],
        out_specs=pl.BlockSpec((None, tr, cols), lambda q, i, c_ref: (q, i, 0)),
        out_shape=_sds(s2.shape, BF16))(core, g2, s2)
    return out.reshape((4,) + blk)


def _adamw_math(w, g, m, v):
    m = ADAM_B1 * m + (1.0 - ADAM_B1) * g
    v = ADAM_B2 * v + (1.0 - ADAM_B2) * (g * g)
    m_hat = m / (1.0 - ADAM_B1 ** ADAM_STEP)
    v_hat = v / (1.0 - ADAM_B2 ** ADAM_STEP)
    delta = -ADAM_LR * (m_hat / (jnp.sqrt(v_hat) + ADAM_EPS) + ADAM_WD * w)
    return delta, m, v


def _reduce_adamw(part, from_chips, w, m, v, name):
    shape = w.shape
    cols = shape[-1]
    p2 = part.reshape(4, -1, cols)
    r2 = from_chips.reshape(3, -1, cols)
    w2, m2, v2 = _as2d(w), _as2d(m), _as2d(v)
    rows = w2.shape[0]
    tr = _row_tile(rows, 256)
    chip = (2 * lax.axis_index("x") + lax.axis_index("y")).astype(jnp.int32).reshape(1)

    def body(chip_ref, p_ref, r_ref, w_ref, m_ref, v_ref, g_ref, d_ref, nm_ref, nv_ref):
        g = p_ref[...].astype(F32)
        for k in range(3):
            g = g + r_ref[k].astype(F32)
        delta, nm, nv = _adamw_math(w_ref[...], g, m_ref[...], v_ref[...])
        g_ref[...] = g
        d_ref[...] = delta
        nm_ref[...] = nm
        nv_ref[...] = nv

    spec = pl.BlockSpec((tr, cols), lambda i, chip_ref: (i, 0))
    outs = _call(
        body, name=name, grid=(rows // tr,), prefetch=True,
        in_specs=[pl.BlockSpec((None, tr, cols), lambda i, chip_ref: (chip_ref[0], i, 0)),
                  pl.BlockSpec((3, tr, cols), lambda i, chip_ref: (0, i, 0)), spec, spec, spec],
        out_specs=[spec] * 4, out_shape=[_sds(w2.shape, F32)] * 4)(chip, p2, r2, w2, m2, v2)
    return [o.reshape(shape) for o in outs]


def _small_adamw(gathered, w, m, v, name):
    def body(a_ref, w_ref, m_ref, v_ref, g_ref, d_ref, nm_ref, nv_ref):
        g = a_ref[0]
        for k in range(1, N_DEV):
            g = g + a_ref[k]
        delta, nm, nv = _adamw_math(w_ref[...], g, m_ref[...], v_ref[...])
        g_ref[...] = g
        d_ref[...] = delta
        nm_ref[...] = nm
        nv_ref[...] = nv

    return _call(body, name=name, out_shape=[_sds(w.shape, F32)] * 4)(gathered, w, m, v)


def _norm_matmul_cols(x, g, wg, layer, mode, name):
    t, d = x.shape
    nb = wg.shape[-1]
    tm = ROW_TILE
    pieces = nb // LANES

    def body(x_ref, g_ref, w_ref, y_ref, xn_ref):
        @pl.when(pl.program_id(1) == 0)
        def _():
            xn_ref[...] = _rms(x_ref[...], g_ref[...]).astype(BF16)

        y = _dot(xn_ref[...], w_ref[...])
        if mode == "heads":
            for p in range(pieces):
                y_ref[p] = y[:, p * LANES:(p + 1) * LANES]
        else:
            y_ref[...] = y.astype(BF16)

    if mode == "cols":
        y_shape, y_spec = _sds((t, N_DEV * nb), BF16), pl.BlockSpec((tm, nb), lambda i, j: (i, j))
    elif mode == "blocks":
        y_shape, y_spec = _sds((N_DEV, t, nb), BF16), pl.BlockSpec((None, tm, nb), lambda i, j: (j, i, 0))
    else:
        y_shape = _sds((N_DEV * pieces, t, LANES), F32)
        y_spec = pl.BlockSpec((pieces, tm, LANES), lambda i, j: (j, i, 0))
    return _call(
        body, name=name, grid=(t // tm, N_DEV),
        in_specs=[pl.BlockSpec((tm, d), lambda i, j: (i, 0)), pl.BlockSpec((1, d), lambda i, j: (0, 0)),
                  pl.BlockSpec((None, None, d, nb), lambda i, j: (j, layer, 0, 0))],
        out_specs=[y_spec, pl.BlockSpec((tm, d), lambda i, j: (i, 0))],
        out_shape=[y_shape, _sds((t, d), BF16)])(x, g, wg)


def _norm_matmul_heads(x, g, wg, layer, scale, name):
    t, d = x.shape
    tm = ROW_TILE
    hp = d // LANES

    def body(x_ref, g_ref, w_ref, y_ref, xn_ref):
        xn = _rms(x_ref[...], g_ref[...]).astype(BF16)
        xn_ref[...] = xn
        y = _dot(xn, w_ref[...].reshape(d, d)) * scale
        for p in range(hp):
            y_ref[p] = y[:, p * LANES:(p + 1) * LANES]

    return _call(
        body, name=name, grid=(t // tm,),
        in_specs=[pl.BlockSpec((tm, d), lambda i: (i, 0)), pl.BlockSpec((1, d), lambda i: (0, 0)),
                  pl.BlockSpec((N_DEV, None, d // N_DEV, d), lambda i: (0, layer, 0, 0))],
        out_specs=[pl.BlockSpec((hp, tm, LANES), lambda i: (0, i, 0)), pl.BlockSpec((tm, d), lambda i: (i, 0))],
        out_shape=[_sds((hp, t, LANES), F32), _sds((t, d), BF16)])(x, g, wg)


def _shift_down(u, halo, k, tm):
    row = lax.broadcasted_iota(jnp.int32, u.shape, 0)
    out = pltpu.roll(u, k, 0)
    for j in range(k):
        out = jnp.where(row == j, halo[halo.shape[0] - k + j:halo.shape[0] - k + j + 1, :], out)
    return out


def _shift_up(u, halo, k, tm):
    row = lax.broadcasted_iota(jnp.int32, u.shape, 0)
    out = pltpu.roll(u, tm - k, 0)
    for j in range(k):
        out = jnp.where(row == tm - k + j, halo[j:j + 1, :], out)
    return out


HALO = 16


def _conv_fwd(p, cw, name):
    t, d3 = p.shape
    d = d3 // 3
    tm = ROW_TILE
    hb = tm // HALO

    def body(p_ref, prev_ref, cw_ref, z_ref):
        i = pl.program_id(0)
        b = p_ref[:, 0:d].astype(F32)
        u = p_ref[:, d:2 * d].astype(F32) * p_ref[:, 2 * d:3 * d].astype(F32)
        keep = (i > 0).astype(F32)
        hu = prev_ref[:, d:2 * d].astype(F32) * prev_ref[:, 2 * d:3 * d].astype(F32) * keep
        uc = cw_ref[2:3, :] * u + cw_ref[1:2, :] * _shift_down(u, hu, 1, tm) + cw_ref[0:1, :] * _shift_down(u, hu, 2, tm)
        z_ref[...] = (b * uc).astype(BF16)

    return _call(
        body, name=name, grid=(t // tm,),
        in_specs=[pl.BlockSpec((tm, d3), lambda i: (i, 0)),
                  pl.BlockSpec((HALO, d3), lambda i: (jnp.maximum(i * hb - 1, 0), 0)),
                  pl.BlockSpec((3, d), lambda i: (0, 0))],
        out_specs=pl.BlockSpec((tm, d), lambda i: (i, 0)), out_shape=_sds((t, d), BF16))(p, p, cw)


def _matmul_norm_residual(a3, wg, layer, g, x_res, name):
    kc_n, t, kc = a3.shape
    d = wg.shape[-1]
    per = N_DEV // kc_n
    rows = wg.shape[2]
    tm = ROW_TILE

    def body(a_ref, w_ref, g_ref, x_ref, raw_ref, xo_ref, acc_ref):
        c = pl.program_id(1)
        part = _dot(a_ref[...], w_ref[...].reshape(per * rows, d))

        @pl.when(c == 0)
        def _():
            acc_ref[...] = part

        @pl.when(c > 0)
        def _():
            acc_ref[...] += part

        @pl.when(c == kc_n - 1)
        def _():
            raw = acc_ref[...]
            raw_ref[...] = raw
            xo_ref[...] = x_ref[...] + _rms(raw, g_ref[...])

    row_spec = pl.BlockSpec((tm, d), lambda i, c: (i, 0))
    return _call(
        body, name=name, grid=(t // tm, kc_n),
        in_specs=[pl.BlockSpec((None, tm, kc), lambda i, c: (c, i, 0)),
                  pl.BlockSpec((per, None, rows, d), lambda i, c: (c, layer, 0, 0)),
                  pl.BlockSpec((1, d), lambda i, c: (0, 0)), row_spec],
        out_specs=[row_spec, row_spec], out_shape=[_sds((t, d), F32)] * 2,
        scratch_shapes=[pltpu.VMEM((tm, d), F32)])(a3, wg, g, x_res)


def _swiglu_fwd(h, name):
    _, t, fc = h.shape
    tm = ROW_TILE

    def body(g_ref, u_ref, a_ref):
        gate = g_ref[...].astype(F32)
        a_ref[...] = (gate * jax.nn.sigmoid(gate) * u_ref[...].astype(F32)).astype(BF16)

    return _call(
        body, name=name, grid=(4, t // tm),
        in_specs=[pl.BlockSpec((None, tm, fc), lambda c, i: (c, i, 0)),
                  pl.BlockSpec((None, tm, fc), lambda c, i: (c + 4, i, 0))],
        out_specs=pl.BlockSpec((None, tm, fc), lambda c, i: (c, i, 0)), out_shape=_sds((4, t, fc), BF16))(h, h)


def _alibi_slopes(n_heads):
    hh = np.arange(n_heads, dtype=np.float32) + 1.0
    s = np.power(2.0, -8.0 * hh / n_heads).astype(np.float32)
    return jnp.asarray(np.repeat(s.reshape(n_heads // 2, 2, 1), 2 * ATT_BLOCK, axis=2))


def _band_bias(sl_ref, dil):
    u = lax.broadcasted_iota(jnp.int32, (ATT_BLOCK, 2 * ATT_BLOCK), 0)
    kk = lax.broadcasted_iota(jnp.int32, (ATT_BLOCK, 2 * ATT_BLOCK), 1)
    delta = u + ATT_BLOCK - kk
    valid = (delta >= 0) & (delta <= ATT_BLOCK)
    dist = (delta * dil).astype(F32)
    rows = [jnp.where(valid, -sl_ref[hd:hd + 1, :] * dist, NEG) for hd in range(2)]
    return jnp.concatenate(rows, axis=0)


def _stack_heads(a):
    lane = lax.broadcasted_iota(jnp.int32, a.shape, 1)
    return jnp.concatenate([jnp.where(lane < HEAD_DIM, a, 0.0), jnp.where(lane >= HEAD_DIM, a, 0.0)], axis=0).astype(BF16)


def _unstack_heads(a2):
    top, bot = a2[:ATT_BLOCK], a2[ATT_BLOCK:]
    lane = lax.broadcasted_iota(jnp.int32, top.shape, 1)
    return jnp.where(lane < HEAD_DIM, top, bot)


def _stack_cols(a):
    return jnp.concatenate([a[:, 0:1], a[:, HEAD_DIM:HEAD_DIM + 1]], axis=0)


def _first_mask(first):
    kk = lax.broadcasted_iota(jnp.int32, (2 * ATT_BLOCK, 2 * ATT_BLOCK), 1)
    return jnp.where(jnp.logical_and(first, kk < ATT_BLOCK), NEG, 0.0)


def _attention_fwd(q, kv, slopes, name):
    hp, t, _ = q.shape
    ns = t // SUPER
    nd = len(DILATIONS)

    def body(sl_ref, q_ref, kc_ref, kp_ref, vc_ref, vp_ref, o_ref, lse_ref, kw_ref, vw_ref, og_ref, lg_ref):
        n = pl.program_id(1)
        kw_ref[0:SUPER, :] = kp_ref[...]
        kw_ref[SUPER:, :] = kc_ref[...]
        vw_ref[0:SUPER, :] = vp_ref[...]
        vw_ref[SUPER:, :] = vc_ref[...]
        for gi, dil in enumerate(DILATIONS):
            bias = _band_bias(sl_ref, dil)

            def block(idx, carry, gi=gi, dil=dil, bias=bias):
                r, b = idx % dil, idx // dil
                qs = b * (ATT_BLOCK * dil) + r
                ks = SUPER + (b - 1) * (ATT_BLOCK * dil) + r
                q2 = _stack_heads(q_ref[pl.ds(qs, ATT_BLOCK, stride=dil), :])
                kb = kw_ref[pl.ds(ks, 2 * ATT_BLOCK, stride=dil), :].astype(BF16)
                vb = vw_ref[pl.ds(ks, 2 * ATT_BLOCK, stride=dil), :].astype(BF16)
                s = _dot_nt(q2, kb) + bias + _first_mask(jnp.logical_and(n == 0, b == 0))
                m = jnp.max(s, axis=-1, keepdims=True)
                p = jnp.exp(s - m)
                l = jnp.sum(p, axis=-1, keepdims=True)
                o2 = _dot(p.astype(BF16), vb) / l
                lse2 = jnp.broadcast_to(m + jnp.log(l), (2 * ATT_BLOCK, LANES))
                og_ref[gi, pl.ds(qs, ATT_BLOCK, stride=dil), :] = _unstack_heads(o2)
                lg_ref[gi, pl.ds(qs, ATT_BLOCK, stride=dil), :] = _unstack_heads(lse2)
                return carry

            lax.fori_loop(0, SUPER // ATT_BLOCK, block, 0)
        lg = [lg_ref[gi] for gi in range(nd)]
        top = functools.reduce(jnp.maximum, lg)
        ws = [jnp.exp(x - top) for x in lg]
        tot = functools.reduce(jnp.add, ws)
        lse_ref[...] = top + jnp.log(tot)
        acc = ws[0] * og_ref[0]
        for gi in range(1, nd):
            acc = acc + ws[gi] * og_ref[gi]
        o_ref[...] = (acc / tot).astype(BF16)

    cur = lambda off: pl.BlockSpec((None, SUPER, LANES), lambda h, n: (h + off, n, 0))
    prev = lambda off: pl.BlockSpec((None, SUPER, LANES), lambda h, n: (h + off, jnp.maximum(n - 1, 0), 0))
    return _call(
        body, name=name, grid=(hp, ns),
        in_specs=[pl.BlockSpec((None, 2, 2 * ATT_BLOCK), lambda h, n: (h, 0, 0)), cur(0), cur(0), prev(0), cur(hp), prev(hp)],
        out_specs=[pl.BlockSpec((SUPER, LANES), lambda h, n: (n, h)), cur(0)],
        out_shape=[_sds((t, hp * LANES), BF16), _sds((hp, t, LANES), F32)],
        scratch_shapes=[pltpu.VMEM((2 * SUPER, LANES), F32), pltpu.VMEM((2 * SUPER, LANES), F32),
                        pltpu.VMEM((nd, SUPER, LANES), F32), pltpu.VMEM((nd, SUPER, LANES), F32)],
    )(slopes, q, kv, kv, kv, kv)


def _attention_bwd(q, kv, o, lse, d_o, slopes, q_scale, name):
    hp, t, _ = q.shape
    ns = t // SUPER

    def body(sl_ref, q_ref, kc_ref, kp_ref, vc_ref, vp_ref, o_ref, lse_ref, do_ref,
             dq_ref, dk_ref, dv_ref, kw_ref, vw_ref, dkw_ref, dvw_ref, dd_ref):
        n = pl.program_id(1)

        @pl.when(n == 0)
        def _():
            dkw_ref[...] = jnp.zeros_like(dkw_ref)
            dvw_ref[...] = jnp.zeros_like(dvw_ref)

        @pl.when(n > 0)
        def _():
            dkw_ref[0:SUPER, :] = dkw_ref[SUPER:, :]
            dvw_ref[0:SUPER, :] = dvw_ref[SUPER:, :]
            dkw_ref[SUPER:, :] = jnp.zeros((SUPER, LANES), F32)
            dvw_ref[SUPER:, :] = jnp.zeros((SUPER, LANES), F32)

        @pl.when(n < ns)
        def _():
            kw_ref[0:SUPER, :] = kp_ref[...]
            kw_ref[SUPER:, :] = kc_ref[...]
            vw_ref[0:SUPER, :] = vp_ref[...]
            vw_ref[SUPER:, :] = vc_ref[...]
            prod = do_ref[...] * o_ref[...].astype(F32)
            lane = lax.broadcasted_iota(jnp.int32, prod.shape, 1)
            d0 = jnp.sum(jnp.where(lane < HEAD_DIM, prod, 0.0), axis=-1, keepdims=True)
            d1 = jnp.sum(jnp.where(lane >= HEAD_DIM, prod, 0.0), axis=-1, keepdims=True)
            dd_ref[...] = jnp.where(lane < HEAD_DIM, d0, d1)
            dq_ref[...] = jnp.zeros_like(dq_ref)
            for dil in DILATIONS:
                bias = _band_bias(sl_ref, dil)

                def block(idx, carry, dil=dil, bias=bias):
                    r, b = idx % dil, idx // dil
                    qs = b * (ATT_BLOCK * dil) + r
                    ks = SUPER + (b - 1) * (ATT_BLOCK * dil) + r
                    rows = pl.ds(qs, ATT_BLOCK, stride=dil)
                    keys = pl.ds(ks, 2 * ATT_BLOCK, stride=dil)
                    q2 = _stack_heads(q_ref[rows, :])
                    do2 = _stack_heads(do_ref[rows, :])
                    kb = kw_ref[keys, :].astype(BF16)
                    vb = vw_ref[keys, :].astype(BF16)
                    s = _dot_nt(q2, kb) + bias + _first_mask(jnp.logical_and(n == 0, b == 0))
                    p = jnp.exp(s - _stack_cols(lse_ref[rows, :]))
                    dp = _dot_nt(do2, vb)
                    ds = (p * (dp - _stack_cols(dd_ref[rows, :]))).astype(BF16)
                    dq_ref[rows, :] += _unstack_heads(_dot(ds, kb)) * q_scale
                    dkw_ref[keys, :] += _dot_tn(ds, q2)
                    dvw_ref[keys, :] += _dot_tn(p.astype(BF16), do2)
                    return carry

                lax.fori_loop(0, SUPER // ATT_BLOCK, block, 0)

        dk_ref[...] = dkw_ref[0:SUPER, :]
        dv_ref[...] = dvw_ref[0:SUPER, :]

    last = ns - 1
    cur = lambda off: pl.BlockSpec((None, SUPER, LANES), lambda h, n: (h + off, jnp.minimum(n, last), 0))
    prev = lambda off: pl.BlockSpec((None, SUPER, LANES), lambda h, n: (h + off, jnp.clip(n - 1, 0, last), 0))
    nat = pl.BlockSpec((SUPER, LANES), lambda h, n: (jnp.minimum(n, last), h))
    late = pl.BlockSpec((None, SUPER, LANES), lambda h, n: (h, jnp.maximum(n - 1, 0), 0))
    dq, dk, dv = _call(
        body, name=name, grid=(hp, ns + 1),
        in_specs=[pl.BlockSpec((None, 2, 2 * ATT_BLOCK), lambda h, n: (h, 0, 0)), cur(0), cur(0), prev(0), cur(hp), prev(hp),
                  nat, cur(0), nat],
        out_specs=[cur(0), late, late],
        out_shape=[_sds((hp, t, LANES), F32)] * 3,
        scratch_shapes=[pltpu.VMEM((2 * SUPER, LANES), F32)] * 4 + [pltpu.VMEM((SUPER, LANES), F32)],
    )(slopes, q, kv, kv, kv, kv, o, lse, d_o)
    return dq, dk, dv


def _loss_head(y, target, raw, g, name):
    t, d = y.shape
    tm = ROW_TILE

    def body(y_ref, t_ref, raw_ref, g_ref, sq_ref, dy_ref, draw_ref, dg_ref):
        i = pl.program_id(0)
        err = y_ref[...] - t_ref[...]
        dy = err * (1.0 / d)
        dy_ref[...] = dy
        draw, dg = _rms_bwd(raw_ref[...], g_ref[...], dy)
        draw_ref[...] = draw.astype(BF16)
        sq = jnp.zeros((8, LANES), F32) + jnp.sum(err * err)

        @pl.when(i == 0)
        def _():
            sq_ref[...] = sq
            dg_ref[...] = dg

        @pl.when(i > 0)
        def _():
            sq_ref[...] += sq
            dg_ref[...] += dg

    row = pl.BlockSpec((tm, d), lambda i: (i, 0))
    vec = pl.BlockSpec((1, d), lambda i: (0, 0))
    return _call(
        body, name=name, grid=(t // tm,), in_specs=[row, row, row, vec],
        out_specs=[pl.BlockSpec((8, LANES), lambda i: (0, 0)), row, row, vec],
        out_shape=[_sds((8, LANES), F32), _sds((t, d), F32), _sds((t, d), BF16), _sds((1, d), F32)])(y, target, raw, g)


def _bwd_matmul_norms(a_specs, a_args, a_tile, n_steps, w_spec, w_arg, w_mat, xa, ga, resid, xb, gb, name):
    t, d = xa.shape
    tm = ROW_TILE
    na = len(a_specs)
    second = xb is not None

    def body(*refs):
        a_refs = refs[:na]
        w_ref, xa_ref, ga_ref, res_ref = refs[na:na + 4]
        rest = refs[na + 4:]
        if second:
            xb_ref, gb_ref, dx_ref, d2_ref, dga_ref, dgb_ref, acc_ref = rest
        else:
            dx_ref, dga_ref, acc_ref = rest
        i, j = pl.program_id(0), pl.program_id(1)
        part = _dot_nt(a_tile(*a_refs), w_mat(w_ref))

        @pl.when(j == 0)
        def _():
            acc_ref[...] = part

        @pl.when(j > 0)
        def _():
            acc_ref[...] += part

        @pl.when(j == n_steps - 1)
        def _():
            da, dga = _rms_bwd(xa_ref[...], ga_ref[...], acc_ref[...])
            dx = res_ref[...] + da
            dx_ref[...] = dx
            if second:
                d2, dgb = _rms_bwd(xb_ref[...], gb_ref[...], dx)
                d2_ref[...] = d2.astype(BF16)

            @pl.when(i == 0)
            def _():
                dga_ref[...] = dga
                if second:
                    dgb_ref[...] = dgb

            @pl.when(i > 0)
            def _():
                dga_ref[...] += dga
                if second:
                    dgb_ref[...] += dgb

    row = pl.BlockSpec((tm, d), lambda i, j: (i, 0))
    vec = pl.BlockSpec((1, d), lambda i, j: (0, 0))
    in_specs = list(a_specs) + [w_spec, row, vec, row]
    args = list(a_args) + [w_arg, xa, ga, resid]
    if second:
        in_specs += [row, vec]
        args += [xb, gb]
        out_specs = [row, row, vec, vec]
        out_shape = [_sds((t, d), F32), _sds((t, d), BF16), _sds((1, d), F32), _sds((1, d), F32)]
    else:
        out_specs = [row, vec]
        out_shape = [_sds((t, d), F32), _sds((1, d), F32)]
    return _call(body, name=name, grid=(t // tm, n_steps), in_specs=in_specs, out_specs=out_specs,
                 out_shape=out_shape, scratch_shapes=[pltpu.VMEM((tm, d), F32)])(*args)


def _heads_to_rows(*refs):
    hp = refs[0].shape[0]
    cols = []
    for p in range(hp):
        v = refs[0][p]
        for r in refs[1:]:
            v = v + r[p]
        cols.append(v)
    return jnp.concatenate(cols, axis=-1).astype(BF16)


def _matmul_nt_rows(a, wg, layer, out_dtype, name):
    t, d = a.shape
    tm = ROW_TILE

    def body(a_ref, w_ref, o_ref):
        o_ref[...] = _dot_nt(a_ref[...], w_ref[...].reshape(d, d)).astype(out_dtype)

    row = pl.BlockSpec((tm, d), lambda i: (i, 0))
    return _call(body, name=name, grid=(t // tm,),
                 in_specs=[row, pl.BlockSpec((N_DEV, None, d // N_DEV, d), lambda i: (0, layer, 0, 0))],
                 out_specs=row, out_shape=_sds((t, d), out_dtype))(a, wg)


def _swiglu_bwd(d_ff, wg, layer, h, name):
    t, d = d_ff.shape
    fc = h.shape[-1]
    rows = wg.shape[2]
    tm = ROW_TILE

    def body(df_ref, w_ref, g_ref, u_ref, dh_ref):
        da = _dot_nt(df_ref[...], w_ref[...].reshape(2 * rows, d))
        gate, up = g_ref[...].astype(F32), u_ref[...].astype(F32)
        sig = jax.nn.sigmoid(gate)
        dh_ref[0] = (da * up * (sig * (1.0 + gate * (1.0 - sig)))).astype(BF16)
        dh_ref[1] = (da * (gate * sig)).astype(BF16)

    return _call(
        body, name=name, grid=(t // tm, 4),
        in_specs=[pl.BlockSpec((tm, d), lambda i, c: (i, 0)),
                  pl.BlockSpec((2, None, rows, d), lambda i, c: (c, layer, 0, 0)),
                  pl.BlockSpec((None, tm, fc), lambda i, c: (c, i, 0)),
                  pl.BlockSpec((None, tm, fc), lambda i, c: (c + 4, i, 0))],
        out_specs=pl.BlockSpec((None, 2, tm, fc), lambda i, c: (c, 0, i, 0)),
        out_shape=_sds((4, 2, t, fc), BF16))(d_ff, wg, h, h)


def _conv_bwd(p, d_z, cw, name):
    t, d3 = p.shape
    d = d3 // 3
    tm = ROW_TILE
    hb = tm // HALO
    nt = t // tm

    def body(p_ref, prev_ref, next_ref, dz_ref, dzn_ref, cw_ref, dp_ref, dcw_ref):
        i = pl.program_id(0)
        b = p_ref[:, 0:d].astype(F32)
        c = p_ref[:, d:2 * d].astype(F32)
        h = p_ref[:, 2 * d:3 * d].astype(F32)
        u = c * h
        hu = prev_ref[:, d:2 * d].astype(F32) * prev_ref[:, 2 * d:3 * d].astype(F32) * (i > 0).astype(F32)
        u1, u2 = _shift_down(u, hu, 1, tm), _shift_down(u, hu, 2, tm)
        uc = cw_ref[2:3, :] * u + cw_ref[1:2, :] * u1 + cw_ref[0:1, :] * u2
        dz = dz_ref[...].astype(F32)
        duc = dz * b
        dn = dzn_ref[...].astype(F32) * next_ref[:, 0:d].astype(F32) * (i < nt - 1).astype(F32)
        du = cw_ref[2:3, :] * duc + cw_ref[1:2, :] * _shift_up(duc, dn, 1, tm) + cw_ref[0:1, :] * _shift_up(duc, dn, 2, tm)
        dp_ref[:, 0:d] = (dz * uc).astype(BF16)
        dp_ref[:, d:2 * d] = (du * h).astype(BF16)
        dp_ref[:, 2 * d:3 * d] = (du * c).astype(BF16)
        dcw = jnp.concatenate([jnp.sum(duc * u2, axis=0, keepdims=True), jnp.sum(duc * u1, axis=0, keepdims=True),
                               jnp.sum(duc * u, axis=0, keepdims=True)], axis=0)

        @pl.when(i == 0)
        def _():
            dcw_ref[...] = dcw

        @pl.when(i > 0)
        def _():
            dcw_ref[...] += dcw

    last_halo = t // HALO - 1
    return _call(
        body, name=name, grid=(nt,),
        in_specs=[pl.BlockSpec((tm, d3), lambda i: (i, 0)),
                  pl.BlockSpec((HALO, d3), lambda i: (jnp.maximum(i * hb - 1, 0), 0)),
                  pl.BlockSpec((HALO, d3), lambda i: (jnp.minimum((i + 1) * hb, last_halo), 0)),
                  pl.BlockSpec((tm, d), lambda i: (i, 0)),
                  pl.BlockSpec((HALO, d), lambda i: (jnp.minimum((i + 1) * hb, last_halo), 0)),
                  pl.BlockSpec((3, d), lambda i: (0, 0))],
        out_specs=[pl.BlockSpec((tm, d3), lambda i: (i, 0)), pl.BlockSpec((3, d), lambda i: (0, 0))],
        out_shape=[_sds((t, d3), BF16), _sds((3, d), F32)])(p, p, p, d_z, d_z, cw)


def _grad_weight(a_specs, a_args, a_tile, b_specs, b_args, b_tile, n_out, acc_shape, out_spec, out_shape, t, name):
    tt = ROW_TILE
    na, nb = len(a_specs), len(b_specs)

    def body(*refs):
        a_refs, b_refs = refs[:na], refs[na:na + nb]
        o_ref, acc_ref = refs[na + nb:]
        s = pl.program_id(1)
        part = _dot_tn(a_tile(*a_refs), b_tile(*b_refs))

        @pl.when(s == 0)
        def _():
            acc_ref[...] = part

        @pl.when(s > 0)
        def _():
            acc_ref[...] += part

        @pl.when(s == t // tt - 1)
        def _():
            o_ref[...] = acc_ref[...].astype(BF16).reshape(o_ref.shape)

    return _call(body, name=name, grid=(n_out, t // tt), in_specs=list(a_specs) + list(b_specs), out_specs=out_spec,
                 out_shape=out_shape, scratch_shapes=[pltpu.VMEM(acc_shape, F32)])(*a_args, *b_args)


def _ident(ref):
    return ref[...]


def kernel(x, norm_g, conv_in_w, conv_w, conv_out_w, kv_norm_g, kv_w, q_w, o_w, ffn_in_w, ffn_out_w, loss_target, m_norm_g, m_conv_in_w, m_conv_w, m_conv_out_w, m_kv_norm_g, m_kv_w, m_q_w, m_o_w, m_ffn_in_w, m_ffn_out_w, v_norm_g, v_conv_in_w, v_conv_w, v_conv_out_w, v_kv_norm_g, v_kv_w, v_q_w, v_o_w, v_ffn_in_w, v_ffn_out_w):
    x0 = x[0]
    target = loss_target[0]
    t, d = x0.shape
    depth = norm_g.shape[0]
    n_a = conv_in_w.shape[0]
    n_b = q_w.shape[0]
    hp = d // LANES
    tm = ROW_TILE
    assert t % SUPER == 0 and d % LANES == 0 and depth == n_a + n_b
    dev = 4 * lax.axis_index("x") + 2 * lax.axis_index("y") + lax.axis_index("c")

    n_small = 4 * depth + 3 * n_a
    small_rows = -(-(n_small + 1) // 8) * 8
    small_local = jnp.concatenate([norm_g.reshape(4 * depth, -1), conv_w.reshape(3 * n_a, -1),
                                   jnp.zeros((small_rows - n_small, norm_g.shape[-1]), F32)], axis=0)
    big = {"conv_in_w": conv_in_w, "conv_out_w": conv_out_w, "kv_w": kv_w[None], "q_w": q_w, "o_w": o_w,
           "ffn_in_w": ffn_in_w, "ffn_out_w": ffn_out_w}
    names = list(big)
    gathered = _all_gather([small_local] + [_cast_bf16(big[k], "cast_" + k) for k in names], "gather_weights")
    small_all = gathered[0].transpose(1, 0, 2).reshape(small_rows, d)
    wg = dict(zip(names, gathered[1:]))
    gain = lambda layer, k: small_all[4 * layer + k][None]
    taps = lambda layer: small_all[4 * depth + 3 * layer: 4 * depth + 3 * layer + 3]
    g_kv = kv_norm_g[None]
    slopes = _alibi_slopes(d // HEAD_DIM)
    fc = wg["ffn_in_w"].shape[-1]
    cb = wg["conv_in_w"].shape[-1]
    kvb = wg["kv_w"].shape[-1]
    q_scale = HEAD_DIM ** -0.5

    saved = []
    kv = kvn = None
    xs = x0
    for layer in range(depth):
        tag = f"_l{layer}"
        s = {"x_in": xs}
        if layer < n_a:
            s["p"], s["xn"] = _norm_matmul_cols(xs, gain(layer, 0), wg["conv_in_w"], layer, "cols", "conv_in" + tag)
            s["z"] = _conv_fwd(s["p"], taps(layer), "conv" + tag)
            s["mix"], x_mid = _matmul_norm_residual(s["z"][None], wg["conv_out_w"], layer, gain(layer, 1), xs, "conv_out" + tag)
        else:
            j = layer - n_a
            if kv is None:
                kv, kvn = _norm_matmul_cols(xs, g_kv, wg["kv_w"], 0, "heads", "kv_proj")
            s["q"], s["xn"] = _norm_matmul_heads(xs, gain(layer, 0), wg["q_w"], j, q_scale, "q_proj" + tag)
            s["o"], s["lse"] = _attention_fwd(s["q"], kv, slopes, "attention" + tag)
            s["mix"], x_mid = _matmul_norm_residual(s["o"][None], wg["o_w"], j, gain(layer, 1), xs, "o_proj" + tag)
        s["x_mid"] = x_mid
        s["h"], s["fn"] = _norm_matmul_cols(x_mid, gain(layer, 2), wg["ffn_in_w"], layer, "blocks", "ffn_in" + tag)
        s["a"] = _swiglu_fwd(s["h"], "swiglu" + tag)
        s["ff"], xs = _matmul_norm_residual(s["a"], wg["ffn_out_w"], layer, gain(layer, 3), x_mid, "ffn_out" + tag)
        saved.append(s)

    last = saved[-1]
    sq, dx_out, d_ff, dg3 = _loss_head(xs, target, last["ff"], gain(depth - 1, 3), "loss_head")
    loss = lax.psum(sq[0, 0] * (0.5 / d), ("x", "y", "c"))

    dgain = {(depth - 1, 3): dg3}
    dtaps = {}
    grads = {k: [None] * big[k].shape[0] for k in names}
    row_i = lambda i, j: (i, 0)
    dkv_parts = []
    for layer in reversed(range(depth)):
        tag = f"_l{layer}"
        s = saved[layer]
        dh = _swiglu_bwd(d_ff, wg["ffn_out_w"], layer, s["h"], "swiglu_bwd" + tag)
        rows_out = wg["ffn_out_w"].shape[2]
        grads["ffn_out_w"][layer] = _grad_weight(
            [pl.BlockSpec((None, tm, fc), lambda c, i: (c, i, 0))], [s["a"]], _ident,
            [pl.BlockSpec((tm, d), lambda c, i: (i, 0))], [d_ff], _ident,
            4, (fc, d), pl.BlockSpec((2, rows_out, d), lambda c, i: (c, 0, 0)), _sds((N_DEV, rows_out, d), BF16), t,
            "grad_ffn_out" + tag)
        grads["ffn_in_w"][layer] = _grad_weight(
            [pl.BlockSpec((tm, d), lambda j, i: (i, 0))], [s["fn"]], _ident,
            [pl.BlockSpec((None, None, tm, fc), lambda j, i: (j % 4, j // 4, i, 0))], [dh], _ident,
            N_DEV, (d, fc), pl.BlockSpec((None, d, fc), lambda j, i: (j, 0, 0)), _sds((N_DEV, d, fc), BF16), t,
            "grad_ffn_in" + tag)
        dx_mid, d_mix, dg2, dg1 = _bwd_matmul_norms(
            [pl.BlockSpec((None, None, tm, fc), lambda i, j: (j % 4, j // 4, i, 0))], [dh], _ident, N_DEV,
            pl.BlockSpec((None, None, d, fc), lambda i, j: (j, layer, 0, 0)), wg["ffn_in_w"], _ident,
            s["x_mid"], gain(layer, 2), dx_out, s["mix"], gain(layer, 1), "ffn_in_bwd" + tag)
        dgain[(layer, 2)], dgain[(layer, 1)] = dg2, dg1
        full_rows = pl.BlockSpec((N_DEV, d // N_DEV, d), lambda j, i: (0, 0, 0))
        rows_w = lambda wname, idx: (pl.BlockSpec((N_DEV, None, d // N_DEV, d), lambda i, j: (0, idx, 0, 0)), wg[wname],
                                     lambda w_ref: w_ref[...].reshape(d, d))
        if layer < n_a:
            d_z = _matmul_nt_rows(d_mix, wg["conv_out_w"], layer, BF16, "conv_out_bwd" + tag)
            grads["conv_out_w"][layer] = _grad_weight(
                [pl.BlockSpec((tm, d), lambda j, i: (i, 0))], [s["z"]], _ident,
                [pl.BlockSpec((tm, d), lambda j, i: (i, 0))], [d_mix], _ident,
                1, (d, d), full_rows, _sds((N_DEV, d // N_DEV, d), BF16), t, "grad_conv_out" + tag)
            d_p, dtaps[layer] = _conv_bwd(s["p"], d_z, taps(layer), "conv_bwd" + tag)
            grads["conv_in_w"][layer] = _grad_weight(
                [pl.BlockSpec((tm, d), lambda j, i: (i, 0))], [s["xn"]], _ident,
                [pl.BlockSpec((tm, cb), lambda j, i: (i, j))], [d_p], _ident,
                N_DEV, (d, cb), pl.BlockSpec((None, d, cb), lambda j, i: (j, 0, 0)), _sds((N_DEV, d, cb), BF16), t,
                "grad_conv_in" + tag)
            a_specs, a_args, a_tile, n_steps = [pl.BlockSpec((tm, cb), lambda i, j: (i, j))], [d_p], _ident, N_DEV
            w_spec = pl.BlockSpec((None, None, d, cb), lambda i, j: (j, layer, 0, 0))
            w_arg, w_mat = wg["conv_in_w"], _ident
            resid = dx_mid
        else:
            j_b = layer - n_a
            d_o = _matmul_nt_rows(d_mix, wg["o_w"], j_b, F32, "o_proj_bwd" + tag)
            grads["o_w"][j_b] = _grad_weight(
                [pl.BlockSpec((tm, d), lambda j, i: (i, 0))], [s["o"]], _ident,
                [pl.BlockSpec((tm, d), lambda j, i: (i, 0))], [d_mix], _ident,
                1, (d, d), full_rows, _sds((N_DEV, d // N_DEV, d), BF16), t, "grad_o" + tag)
            dq, dk, dv = _attention_bwd(s["q"], kv, s["o"], s["lse"], d_o, slopes, q_scale, "attention_bwd" + tag)
            dkv_parts.append((dk, dv))
            heads_spec = pl.BlockSpec((hp, tm, LANES), lambda j, i: (0, i, 0))
            grads["q_w"][j_b] = _grad_weight(
                [pl.BlockSpec((tm, d), lambda j, i: (i, 0))], [s["xn"]], _ident,
                [heads_spec], [dq], _heads_to_rows,
                1, (d, d), full_rows, _sds((N_DEV, d // N_DEV, d), BF16), t, "grad_q" + tag)
            a_specs, a_args, a_tile, n_steps = [pl.BlockSpec((hp, tm, LANES), lambda i, j: (0, i, 0))], [dq], _heads_to_rows, 1
            w_spec, w_arg, w_mat = rows_w("q_w", j_b)
            resid = dx_mid
            if layer == n_a:
                pieces = kvb // LANES
                halves = []
                for src in (0, 1):
                    halves.append([part[src] for part in dkv_parts])
                n_half = len(dkv_parts)
                kv_args = [arr for src in (0, 1) for arr in halves[src]]

                def kv_block(src, j):
                    return jnp.where((j // 4) == src, j % 4, 0)

                def kv_tile(axis):
                    def tile(*refs):
                        keys = _heads_to_rows(*refs[:n_half])
                        vals = _heads_to_rows(*refs[n_half:])
                        return jnp.where(pl.program_id(axis) < 4, keys, vals)
                    return tile

                kv_specs = [pl.BlockSpec((pieces, tm, LANES), functools.partial(lambda i, j, src: (kv_block(src, j), i, 0), src=src))
                            for src in (0, 1) for _ in range(n_half)]
                resid, dgain["kv"] = _bwd_matmul_norms(
                    kv_specs, kv_args, kv_tile(1), N_DEV,
                    pl.BlockSpec((None, None, d, kvb), lambda i, j: (j, 0, 0, 0)), wg["kv_w"], _ident,
                    s["x_in"], g_kv, dx_mid, None, None, "kv_proj_bwd")
                kv_b_specs = [pl.BlockSpec((pieces, tm, LANES), functools.partial(lambda j, i, src: (kv_block(src, j), i, 0), src=src))
                              for src in (0, 1) for _ in range(n_half)]
                grads["kv_w"][0] = _grad_weight(
                    [pl.BlockSpec((tm, d), lambda j, i: (i, 0))], [kvn], _ident,
                    kv_b_specs, kv_args, kv_tile(0),
                    N_DEV, (d, kvb), pl.BlockSpec((None, d, kvb), lambda j, i: (j, 0, 0)), _sds((N_DEV, d, kvb), BF16), t,
                    "grad_kv")
        if layer > 0:
            prev = saved[layer - 1]
            dx_out, d_ff, dg0, dg3p = _bwd_matmul_norms(
                a_specs, a_args, a_tile, n_steps, w_spec, w_arg, w_mat,
                s["x_in"], gain(layer, 0), resid, prev["ff"], gain(layer - 1, 3), "mixer_in_bwd" + tag)
            dgain[(layer, 0)], dgain[(layer - 1, 3)] = dg0, dg3p
        else:
            grad_x, dg0 = _bwd_matmul_norms(
                a_specs, a_args, a_tile, n_steps, w_spec, w_arg, w_mat,
                s["x_in"], gain(layer, 0), resid, None, None, "mixer_in_bwd" + tag)
            dgain[(layer, 0)] = dg0

    small_grad = jnp.concatenate(
        [dgain[(layer, k)] for layer in range(depth) for k in range(4)] + [dtaps[layer] for layer in range(n_a)]
        + [dgain["kv"]] + [jnp.zeros((small_rows - n_small - 1, d), F32)], axis=0)
    small_grads_all = _all_gather([small_grad], "gather_small_grads")[0]
    lo = dev * (d // N_DEV)

    def pack(ng, cwp, kvg):
        rows = jnp.concatenate([ng.reshape(4 * depth, -1), cwp.reshape(3 * n_a, -1)], axis=0)
        z = lax.dynamic_update_slice(jnp.zeros((small_rows, d), F32), rows, (0, lo))
        return lax.dynamic_update_slice(z, kvg[None], (n_small, 0))

    w_small = lax.dynamic_update_slice(small_all, g_kv, (n_small, 0))
    m_small, v_small = pack(m_norm_g, m_conv_w, m_kv_norm_g), pack(v_norm_g, v_conv_w, v_kv_norm_g)
    sm = _small_adamw(small_grads_all, w_small, m_small, v_small, "adamw_small")

    def unpack(a):
        mine = lax.dynamic_slice(a, (0, lo), (small_rows, d // N_DEV))
        return (mine[:4 * depth].reshape(norm_g.shape), mine[4 * depth:n_small].reshape(conv_w.shape), a[n_small])

    small_out = [unpack(a) for a in sm]

    order = [(k, i) for k in names for i in range(big[k].shape[0])]
    partials = [grads[k][i] for k, i in order]
    from_sibling = _exchange_sibling(partials, "reduce_sibling")
    chip_parts = [_add_pairs(g, s_, f"pair_sum_{k}_{i}") for (k, i), g, s_ in zip(order, partials, from_sibling)]
    from_chips = _exchange_chips(chip_parts, "reduce_chips")
    moments = {"conv_in_w": (m_conv_in_w, v_conv_in_w), "conv_out_w": (m_conv_out_w, v_conv_out_w),
               "kv_w": (m_kv_w[None], v_kv_w[None]), "q_w": (m_q_w, v_q_w), "o_w": (m_o_w, v_o_w),
               "ffn_in_w": (m_ffn_in_w, v_ffn_in_w), "ffn_out_w": (m_ffn_out_w, v_ffn_out_w)}
    res = {k: [None] * big[k].shape[0] for k in names}
    for (k, i), part, got in zip(order, chip_parts, from_chips):
        res[k][i] = _reduce_adamw(part, got, big[k][i], moments[k][0][i], moments[k][1][i], f"adamw_{k}_{i}")

    def big_out(k, which):
        st = jnp.stack([res[k][i][which] for i in range(big[k].shape[0])], axis=0)
        return st[0] if k == "kv_w" else st

    out_names = ["norm_g", "conv_in_w", "conv_w", "conv_out_w", "kv_norm_g", "kv_w", "q_w", "o_w", "ffn_in_w", "ffn_out_w"]
    small_pos = {"norm_g": 0, "conv_w": 1, "kv_norm_g": 2}
    outs = [loss, grad_x[None]]
    for which in range(4):
        for k in out_names:
            outs.append(small_out[which][small_pos[k]] if k in small_pos else big_out(k, which))
    return tuple(outs)
```

```python
import functools
import math

import numpy as np
import jax
import jax.numpy as jnp
from jax import lax
from jax.experimental import pallas as pl
from jax.experimental.pallas import tpu as pltpu

F32 = jnp.float32
BF16 = jnp.bfloat16

N_DEV = 8
RMS_EPS = 1e-6
HEAD_DIM = 64
LANES = 128
ATT_BLOCK = 128
DILATIONS = (1, 4, 16)
SUPER = ATT_BLOCK * DILATIONS[-1]
NEG = -1e30
ATT_UNROLL_FWD = 8
ATT_UNROLL_BWD = 4

ADAM_LR, ADAM_B1, ADAM_B2, ADAM_EPS, ADAM_WD, ADAM_STEP = 0.001, 0.9, 0.999, 1e-08, 0.01, 10

ROW_TILE = 512
BIG_ROW_TILE = 1024
MESH = pl.DeviceIdType.MESH


def _call(body, *, name, grid=None, in_specs=None, out_specs=None, out_shape=None, scratch_shapes=(), prefetch=False,
          **params):
    cp = pltpu.CompilerParams(**params) if params else None
    if prefetch:
        spec = pltpu.PrefetchScalarGridSpec(num_scalar_prefetch=1, grid=grid, in_specs=in_specs, out_specs=out_specs,
                                            scratch_shapes=list(scratch_shapes))
        return pl.pallas_call(body, name=name, grid_spec=spec, out_shape=out_shape, compiler_params=cp)
    kwargs = {k: v for k, v in (("grid", grid), ("in_specs", in_specs), ("out_specs", out_specs)) if v is not None}
    return pl.pallas_call(body, name=name, out_shape=out_shape, scratch_shapes=list(scratch_shapes),
                          compiler_params=cp, **kwargs)


def _sds(shape, dtype):
    return jax.ShapeDtypeStruct(tuple(shape), dtype)


def _rms(x, g):
    r = lax.rsqrt(jnp.mean(x * x, axis=-1, keepdims=True) + RMS_EPS)
    return x * r * g


def _rms_bwd(x, g, dy):
    r = lax.rsqrt(jnp.mean(x * x, axis=-1, keepdims=True) + RMS_EPS)
    xh = x * r
    dxh = dy * g
    dx = r * (dxh - xh * jnp.mean(dxh * xh, axis=-1, keepdims=True))
    return dx, jnp.sum(dy * xh, axis=0, keepdims=True)


def _dot(a, b):
    return jnp.dot(a, b, preferred_element_type=F32)


def _dot_nt(a, b):
    return lax.dot_general(a, b, (((1,), (1,)), ((), ())), preferred_element_type=F32)


def _dot_tn(a, b):
    return lax.dot_general(a, b, (((0,), (0,)), ((), ())), preferred_element_type=F32)


def _mesh_pos():
    return lax.axis_index("x"), lax.axis_index("y"), lax.axis_index("c")


def _all_gather(arrs, name):
    n = len(arrs)

    def body(*refs):
        ins, outs = refs[:n], refs[n:2 * n]
        send_sems, recv_sems, local_sems = refs[2 * n:]
        x, y, c = _mesh_pos()
        me, sibling = (x, y, c), (x, y, 1 - c)
        chips = [(1 - x, y), (x, 1 - y), (1 - x, 1 - y)]

        def copy(a, k, block, to, src=None):
            dst = outs[a].at[4 * block[0] + 2 * block[1] + block[2]]
            return pltpu.make_async_remote_copy(
                src_ref=dst if src is None else src, dst_ref=dst, send_sem=send_sems.at[a, k],
                recv_sem=recv_sems.at[a, k], device_id=to, device_id_type=MESH)

        started = []
        for a in range(n):
            mine = pltpu.make_async_copy(ins[a], outs[a].at[4 * x + 2 * y + c], local_sems.at[a])
            mine.start()
            started.append(mine)
        first = []
        for a in range(n):
            first.append(copy(a, 0, me, sibling, src=ins[a]))
            first += [copy(a, 1 + j, me, (*chip, c), src=ins[a]) for j, chip in enumerate(chips)]
        for cp in first:
            cp.start()
        passed = []
        for a in range(n):
            for j, chip in enumerate(chips):
                copy(a, 1 + j, (*chip, c), me).wait_recv()
                fwd = copy(a, 4 + j, (*chip, c), sibling)
                fwd.start()
                passed.append(fwd)
        for a in range(n):
            copy(a, 0, sibling, me).wait_recv()
            for j, chip in enumerate(chips):
                copy(a, 4 + j, (*chip, 1 - c), me).wait_recv()
        for cp in first + passed:
            cp.wait_send()
        for cp in started:
            cp.wait()

    any_spec = pl.BlockSpec(memory_space=pl.ANY)
    outs = _call(
        body, name=name, in_specs=[any_spec] * n, out_specs=[any_spec] * n,
        out_shape=[_sds((N_DEV,) + a.shape, a.dtype) for a in arrs],
        scratch_shapes=[pltpu.SemaphoreType.DMA((n, 7)), pltpu.SemaphoreType.DMA((n, 7)), pltpu.SemaphoreType.DMA((n,))],
        has_side_effects=True,
    )(*arrs)
    return list(outs)


def _exchange_sibling(grads, name):
    n = len(grads)

    def body(*refs):
        ins, outs = refs[:n], refs[n:2 * n]
        send_sems, recv_sems = refs[2 * n:]
        x, y, c = _mesh_pos()
        sibling = (x, y, 1 - c)
        copies = []
        for a in range(n):
            for q in range(4):
                cp = pltpu.make_async_remote_copy(
                    src_ref=ins[a].at[2 * q + (1 - c)], dst_ref=outs[a].at[q], send_sem=send_sems.at[a, q],
                    recv_sem=recv_sems.at[a, q], device_id=sibling, device_id_type=MESH)
                cp.start()
                copies.append(cp)
        for cp in copies:
            cp.wait_recv()
        for cp in copies:
            cp.wait_send()

    any_spec = pl.BlockSpec(memory_space=pl.ANY)
    outs = _call(
        body, name=name, in_specs=[any_spec] * n, out_specs=[any_spec] * n,
        out_shape=[_sds((4,) + g.shape[1:], g.dtype) for g in grads],
        scratch_shapes=[pltpu.SemaphoreType.DMA((n, 4)), pltpu.SemaphoreType.DMA((n, 4))],
        has_side_effects=True,
    )(*grads)
    return list(outs)


def _exchange_chips(parts, name):
    n = len(parts)

    def body(*refs):
        ins, outs = refs[:n], refs[n:2 * n]
        send_sems, recv_sems = refs[2 * n:]
        x, y, c = _mesh_pos()
        chips = [(1 - x, y), (x, 1 - y), (1 - x, 1 - y)]
        copies = []
        for a in range(n):
            for k, chip in enumerate(chips):
                cp = pltpu.make_async_remote_copy(
                    src_ref=ins[a].at[2 * chip[0] + chip[1]], dst_ref=outs[a].at[k], send_sem=send_sems.at[a, k],
                    recv_sem=recv_sems.at[a, k], device_id=(*chip, c), device_id_type=MESH)
                cp.start()
                copies.append(cp)
        for cp in copies:
            cp.wait_recv()
        for cp in copies:
            cp.wait_send()

    any_spec = pl.BlockSpec(memory_space=pl.ANY)
    outs = _call(
        body, name=name, in_specs=[any_spec] * n, out_specs=[any_spec] * n,
        out_shape=[_sds((3,) + p.shape[1:], p.dtype) for p in parts],
        scratch_shapes=[pltpu.SemaphoreType.DMA((n, 3)), pltpu.SemaphoreType.DMA((n, 3))],
        has_side_effects=True,
    )(*parts)
    return list(outs)


def _row_tile(rows, cap=512):
    t = min(rows, cap)
    while rows % t or (t % 16 and t != rows):
        t -= 1
    return t


def _as2d(a):
    return a.reshape(-1, a.shape[-1])


def _cast_bf16(w, name):
    w2 = _as2d(w)
    rows, cols = w2.shape
    tr = _row_tile(rows)

    def body(w_ref, o_ref):
        o_ref[...] = w_ref[...].astype(BF16)

    spec = pl.BlockSpec((tr, cols), lambda i: (i, 0))
    out = _call(body, name=name, grid=(rows // tr,), in_specs=[spec], out_specs=spec,
                out_shape=_sds(w2.shape, BF16))(w2)
    return out.reshape(w.shape)


def _add_pairs(grad, from_sibling, name):
    blk = grad.shape[1:]
    g2 = grad.reshape(4, 2, -1, blk[-1])
    s2 = from_sibling.reshape(4, -1, blk[-1])
    rows, cols = s2.shape[1:]
    tr = _row_tile(rows)
    core = lax.axis_index("c").astype(jnp.int32).reshape(1)

    def body(c_ref, g_ref, s_ref, o_ref):
        o_ref[...] = (g_ref[...].astype(F32) + s_ref[...].astype(F32)).astype(BF16)

    out = _call(
        body, name=name, grid=(4, rows // tr), prefetch=True,
        in_specs=[pl.BlockSpec((None, None, tr, cols), lambda q, i, c_ref: (q, c_ref[0], i, 0)),
                  pl.BlockSpec((None, tr, cols), lambda q, i, c_ref: (q, i, 0))],
        out_specs=pl.BlockSpec((None, tr, cols), lambda q, i, c_ref: (q, i, 0)),
        out_shape=_sds(s2.shape, BF16))(core, g2, s2)
    return out.reshape((4,) + blk)


def _adamw_math(w, g, m, v):
    m = ADAM_B1 * m + (1.0 - ADAM_B1) * g
    v = ADAM_B2 * v + (1.0 - ADAM_B2) * (g * g)
    m_hat = m / (1.0 - ADAM_B1 ** ADAM_STEP)
    v_hat = v / (1.0 - ADAM_B2 ** ADAM_STEP)
    delta = -ADAM_LR * (m_hat / (jnp.sqrt(v_hat) + ADAM_EPS) + ADAM_WD * w)
    return delta, m, v


def _reduce_adamw(part, from_chips, w, m, v, name):
    shape = w.shape
    cols = shape[-1]
    p2 = part.reshape(4, -1, cols)
    r2 = from_chips.reshape(3, -1, cols)
    w2, m2, v2 = _as2d(w), _as2d(m), _as2d(v)
    rows = w2.shape[0]
    tr = _row_tile(rows, 256)
    chip = (2 * lax.axis_index("x") + lax.axis_index("y")).astype(jnp.int32).reshape(1)

    def body(chip_ref, p_ref, r_ref, w_ref, m_ref, v_ref, g_ref, d_ref, nm_ref, nv_ref):
        g = p_ref[...].astype(F32)
        for k in range(3):
            g = g + r_ref[k].astype(F32)
        delta, nm, nv = _adamw_math(w_ref[...], g, m_ref[...], v_ref[...])
        g_ref[...] = g
        d_ref[...] = delta
        nm_ref[...] = nm
        nv_ref[...] = nv

    spec = pl.BlockSpec((tr, cols), lambda i, chip_ref: (i, 0))
    outs = _call(
        body, name=name, grid=(rows // tr,), prefetch=True,
        in_specs=[pl.BlockSpec((None, tr, cols), lambda i, chip_ref: (chip_ref[0], i, 0)),
                  pl.BlockSpec((3, tr, cols), lambda i, chip_ref: (0, i, 0)), spec, spec, spec],
        out_specs=[spec] * 4, out_shape=[_sds(w2.shape, F32)] * 4)(chip, p2, r2, w2, m2, v2)
    return [o.reshape(shape) for o in outs]


def _small_adamw(gathered, w, m, v, name):
    def body(a_ref, w_ref, m_ref, v_ref, g_ref, d_ref, nm_ref, nv_ref):
        g = a_ref[0]
        for k in range(1, N_DEV):
            g = g + a_ref[k]
        delta, nm, nv = _adamw_math(w_ref[...], g, m_ref[...], v_ref[...])
        g_ref[...] = g
        d_ref[...] = delta
        nm_ref[...] = nm
        nv_ref[...] = nv

    return _call(body, name=name, out_shape=[_sds(w.shape, F32)] * 4)(gathered, w, m, v)


def _norm_matmul_cols(x, g, wg, layer, mode, name):
    t, d = x.shape
    nb = wg.shape[-1]
    tm = BIG_ROW_TILE
    pieces = nb // LANES

    def body(x_ref, g_ref, w_ref, y_ref, xn_ref):
        @pl.when(pl.program_id(1) == 0)
        def _():
            xn_ref[...] = _rms(x_ref[...], g_ref[...]).astype(BF16)

        y = _dot(xn_ref[...], w_ref[...])
        if mode == "heads":
            for p in range(pieces):
                y_ref[p] = y[:, p * LANES:(p + 1) * LANES]
        else:
            y_ref[...] = y.astype(BF16)

    if mode == "cols":
        y_shape, y_spec = _sds((t, N_DEV * nb), BF16), pl.BlockSpec((tm, nb), lambda i, j: (i, j))
    elif mode == "blocks":
        y_shape, y_spec = _sds((N_DEV, t, nb), BF16), pl.BlockSpec((None, tm, nb), lambda i, j: (j, i, 0))
    else:
        y_shape = _sds((N_DEV * pieces, t, LANES), F32)
        y_spec = pl.BlockSpec((pieces, tm, LANES), lambda i, j: (j, i, 0))
    return _call(
        body, name=name, grid=(t // tm, N_DEV),
        in_specs=[pl.BlockSpec((tm, d), lambda i, j: (i, 0)), pl.BlockSpec((1, d), lambda i, j: (0, 0)),
                  pl.BlockSpec((None, None, d, nb), lambda i, j: (j, layer, 0, 0))],
        out_specs=[y_spec, pl.BlockSpec((tm, d), lambda i, j: (i, 0))],
        out_shape=[y_shape, _sds((t, d), BF16)])(x, g, wg)


def _norm_matmul_heads(x, g, wg, layer, scale, name):
    t, d = x.shape
    tm = ROW_TILE
    hp = d // LANES

    def body(x_ref, g_ref, w_ref, y_ref, xn_ref):
        xn = _rms(x_ref[...], g_ref[...]).astype(BF16)
        xn_ref[...] = xn
        y = _dot(xn, w_ref[...].reshape(d, d)) * scale
        for p in range(hp):
            y_ref[p] = y[:, p * LANES:(p + 1) * LANES]

    return _call(
        body, name=name, grid=(t // tm,),
        in_specs=[pl.BlockSpec((tm, d), lambda i: (i, 0)), pl.BlockSpec((1, d), lambda i: (0, 0)),
                  pl.BlockSpec((N_DEV, None, d // N_DEV, d), lambda i: (0, layer, 0, 0))],
        out_specs=[pl.BlockSpec((hp, tm, LANES), lambda i: (0, i, 0)), pl.BlockSpec((tm, d), lambda i: (i, 0))],
        out_shape=[_sds((hp, t, LANES), F32), _sds((t, d), BF16)])(x, g, wg)


def _shift_down(u, halo, k, tm):
    row = lax.broadcasted_iota(jnp.int32, u.shape, 0)
    out = pltpu.roll(u, k, 0)
    for j in range(k):
        out = jnp.where(row == j, halo[halo.shape[0] - k + j:halo.shape[0] - k + j + 1, :], out)
    return out


def _shift_up(u, halo, k, tm):
    row = lax.broadcasted_iota(jnp.int32, u.shape, 0)
    out = pltpu.roll(u, tm - k, 0)
    for j in range(k):
        out = jnp.where(row == tm - k + j, halo[j:j + 1, :], out)
    return out


HALO = 16


def _conv_fwd(p, cw, name):
    t, d3 = p.shape
    d = d3 // 3
    tm = ROW_TILE
    hb = tm // HALO

    def body(p_ref, prev_ref, cw_ref, z_ref):
        i = pl.program_id(0)
        b = p_ref[:, 0:d].astype(F32)
        u = p_ref[:, d:2 * d].astype(F32) * p_ref[:, 2 * d:3 * d].astype(F32)
        keep = (i > 0).astype(F32)
        hu = prev_ref[:, d:2 * d].astype(F32) * prev_ref[:, 2 * d:3 * d].astype(F32) * keep
        uc = cw_ref[2:3, :] * u + cw_ref[1:2, :] * _shift_down(u, hu, 1, tm) + cw_ref[0:1, :] * _shift_down(u, hu, 2, tm)
        z_ref[...] = (b * uc).astype(BF16)

    return _call(
        body, name=name, grid=(t // tm,),
        in_specs=[pl.BlockSpec((tm, d3), lambda i: (i, 0)),
                  pl.BlockSpec((HALO, d3), lambda i: (jnp.maximum(i * hb - 1, 0), 0)),
                  pl.BlockSpec((3, d), lambda i: (0, 0))],
        out_specs=pl.BlockSpec((tm, d), lambda i: (i, 0)), out_shape=_sds((t, d), BF16))(p, p, cw)


def _matmul_norm_residual(a3, wg, layer, g, x_res, name):
    kc_n, t, kc = a3.shape
    d = wg.shape[-1]
    per = N_DEV // kc_n
    rows = wg.shape[2]
    tm = BIG_ROW_TILE

    def body(a_ref, w_ref, g_ref, x_ref, raw_ref, xo_ref, acc_ref):
        c = pl.program_id(1)
        part = _dot(a_ref[...], w_ref[...].reshape(per * rows, d))

        @pl.when(c == 0)
        def _():
            acc_ref[...] = part

        @pl.when(c > 0)
        def _():
            acc_ref[...] += part

        @pl.when(c == kc_n - 1)
        def _():
            raw = acc_ref[...]
            raw_ref[...] = raw
            xo_ref[...] = x_ref[...] + _rms(raw, g_ref[...])

    row_spec = pl.BlockSpec((tm, d), lambda i, c: (i, 0))
    return _call(
        body, name=name, grid=(t // tm, kc_n),
        in_specs=[pl.BlockSpec((None, tm, kc), lambda i, c: (c, i, 0)),
                  pl.BlockSpec((per, None, rows, d), lambda i, c: (c, layer, 0, 0)),
                  pl.BlockSpec((1, d), lambda i, c: (0, 0)), row_spec],
        out_specs=[row_spec, row_spec], out_shape=[_sds((t, d), F32)] * 2,
        scratch_shapes=[pltpu.VMEM((tm, d), F32)])(a3, wg, g, x_res)


def _swiglu_fwd(h, name):
    _, t, fc = h.shape
    tm = ROW_TILE

    def body(g_ref, u_ref, a_ref):
        gate = g_ref[...].astype(F32)
        a_ref[...] = (gate * jax.nn.sigmoid(gate) * u_ref[...].astype(F32)).astype(BF16)

    return _call(
        body, name=name, grid=(4, t // tm),
        in_specs=[pl.BlockSpec((None, tm, fc), lambda c, i: (c, i, 0)),
                  pl.BlockSpec((None, tm, fc), lambda c, i: (c + 4, i, 0))],
        out_specs=pl.BlockSpec((None, tm, fc), lambda c, i: (c, i, 0)), out_shape=_sds((4, t, fc), BF16))(h, h)


def _alibi_slopes(n_heads):
    hh = np.arange(n_heads, dtype=np.float32) + 1.0
    s = np.power(2.0, -8.0 * hh / n_heads).astype(np.float32)
    return jnp.asarray(np.repeat(s.reshape(n_heads // 2, 2, 1), 2 * ATT_BLOCK, axis=2))


def _band_bias(sl_ref, dil):
    u = lax.broadcasted_iota(jnp.int32, (ATT_BLOCK, 2 * ATT_BLOCK), 0)
    kk = lax.broadcasted_iota(jnp.int32, (ATT_BLOCK, 2 * ATT_BLOCK), 1)
    delta = u + ATT_BLOCK - kk
    valid = (delta >= 0) & (delta <= ATT_BLOCK)
    dist = (delta * dil).astype(F32)
    rows = [jnp.where(valid, -sl_ref[hd:hd + 1, :] * dist, NEG) for hd in range(2)]
    return jnp.concatenate(rows, axis=0)


def _stack_heads(a):
    lane = lax.broadcasted_iota(jnp.int32, a.shape, 1)
    return jnp.concatenate([jnp.where(lane < HEAD_DIM, a, 0.0), jnp.where(lane >= HEAD_DIM, a, 0.0)], axis=0).astype(BF16)


def _unstack_heads(a2):
    top, bot = a2[:ATT_BLOCK], a2[ATT_BLOCK:]
    lane = lax.broadcasted_iota(jnp.int32, top.shape, 1)
    return jnp.where(lane < HEAD_DIM, top, bot)


def _stack_cols(a):
    return jnp.concatenate([a[:, 0:1], a[:, HEAD_DIM:HEAD_DIM + 1]], axis=0)


def _fill_bias(sl_ref, bias_ref):
    kk = lax.broadcasted_iota(jnp.int32, (2 * ATT_BLOCK, 2 * ATT_BLOCK), 1)
    for gi, dil in enumerate(DILATIONS):
        bias = _band_bias(sl_ref, dil)
        bias_ref[2 * gi] = bias
        bias_ref[2 * gi + 1] = jnp.where(kk < ATT_BLOCK, NEG, bias)


def _attention_fwd(q, kv, slopes, name):
    hp, t, _ = q.shape
    ns = t // SUPER
    nd = len(DILATIONS)

    def body(sl_ref, q_ref, kc_ref, kp_ref, vc_ref, vp_ref, o_ref, lse_ref, kw_ref, vw_ref, og_ref, lg_ref, bias_ref):
        n = pl.program_id(1)
        kw_ref[0:SUPER, :] = kp_ref[...]
        kw_ref[SUPER:, :] = kc_ref[...]
        vw_ref[0:SUPER, :] = vp_ref[...]
        vw_ref[SUPER:, :] = vc_ref[...]

        @pl.when(n == 0)
        def _():
            _fill_bias(sl_ref, bias_ref)

        for gi, dil in enumerate(DILATIONS):

            def block(idx, carry, gi=gi, dil=dil):
                r, b = idx % dil, idx // dil
                qs = b * (ATT_BLOCK * dil) + r
                ks = SUPER + (b - 1) * (ATT_BLOCK * dil) + r
                first = jnp.logical_and(n == 0, b == 0).astype(jnp.int32)
                q2 = _stack_heads(q_ref[pl.ds(qs, ATT_BLOCK, stride=dil), :])
                kb = kw_ref[pl.ds(ks, 2 * ATT_BLOCK, stride=dil), :].astype(BF16)
                vb = vw_ref[pl.ds(ks, 2 * ATT_BLOCK, stride=dil), :].astype(BF16)
                s = _dot_nt(q2, kb) + bias_ref[2 * gi + first]
                m = jnp.max(s, axis=-1, keepdims=True)
                p = jnp.exp(s - m)
                l = jnp.sum(p, axis=-1, keepdims=True)
                o2 = _dot(p.astype(BF16), vb) / l
                lse2 = jnp.broadcast_to(m + jnp.log(l), (2 * ATT_BLOCK, LANES))
                og_ref[gi, pl.ds(qs, ATT_BLOCK, stride=dil), :] = _unstack_heads(o2)
                lg_ref[gi, pl.ds(qs, ATT_BLOCK, stride=dil), :] = _unstack_heads(lse2)
                return carry

            lax.fori_loop(0, SUPER // ATT_BLOCK, block, 0, unroll=ATT_UNROLL_FWD)
        lg =[lg_ref[gi] for gi in range(nd)]
        top = functools.reduce(jnp.maximum, lg)
        ws = [jnp.exp(x - top) for x in lg]
        tot = functools.reduce(jnp.add, ws)
        lse_ref[...] = top + jnp.log(tot)
        acc = ws[0] * og_ref[0]
        for gi in range(1, nd):
            acc = acc + ws[gi] * og_ref[gi]
        o_ref[...] = (acc / tot).astype(BF16)

    cur = lambda off: pl.BlockSpec((None, SUPER, LANES), lambda h, n: (h + off, n, 0))
    prev = lambda off: pl.BlockSpec((None, SUPER, LANES), lambda h, n: (h + off, jnp.maximum(n - 1, 0), 0))
    return _call(
        body, name=name, grid=(hp, ns),
        in_specs=[pl.BlockSpec((None, 2, 2 * ATT_BLOCK), lambda h, n: (h, 0, 0)), cur(0), cur(0), prev(0), cur(hp), prev(hp)],
        out_specs=[pl.BlockSpec((SUPER, LANES), lambda h, n: (n, h)), cur(0)],
        out_shape=[_sds((t, hp * LANES), BF16), _sds((hp, t, LANES), F32)],
        scratch_shapes=[pltpu.VMEM((2 * SUPER, LANES), F32), pltpu.VMEM((2 * SUPER, LANES), F32),
                        pltpu.VMEM((nd, SUPER, LANES), F32), pltpu.VMEM((nd, SUPER, LANES), F32),
                        pltpu.VMEM((2 * nd, 2 * ATT_BLOCK, 2 * ATT_BLOCK), F32)],
    )(slopes, q, kv, kv, kv, kv)


def _attention_bwd(q, kv, o, lse, d_o, slopes, q_scale, name):
    hp, t, _ = q.shape
    ns = t // SUPER

    def body(sl_ref, q_ref, kc_ref, kp_ref, vc_ref, vp_ref, o_ref, lse_ref, do_ref,
             dq_ref, dk_ref, dv_ref, kw_ref, vw_ref, dkw_ref, dvw_ref, dd_ref, bias_ref):
        n = pl.program_id(1)

        @pl.when(n == 0)
        def _():
            dkw_ref[...] = jnp.zeros_like(dkw_ref)
            dvw_ref[...] = jnp.zeros_like(dvw_ref)

        @pl.when(n > 0)
        def _():
            dkw_ref[0:SUPER, :] = dkw_ref[SUPER:, :]
            dvw_ref[0:SUPER, :] = dvw_ref[SUPER:, :]
            dkw_ref[SUPER:, :] = jnp.zeros((SUPER, LANES), F32)
            dvw_ref[SUPER:, :] = jnp.zeros((SUPER, LANES), F32)

        @pl.when(n < ns)
        def _():
            kw_ref[0:SUPER, :] = kp_ref[...]
            kw_ref[SUPER:, :] = kc_ref[...]
            vw_ref[0:SUPER, :] = vp_ref[...]
            vw_ref[SUPER:, :] = vc_ref[...]
            prod = do_ref[...] * o_ref[...].astype(F32)
            lane = lax.broadcasted_iota(jnp.int32, prod.shape, 1)
            d0 = jnp.sum(jnp.where(lane < HEAD_DIM, prod, 0.0), axis=-1, keepdims=True)
            d1 = jnp.sum(jnp.where(lane >= HEAD_DIM, prod, 0.0), axis=-1, keepdims=True)
            dd_ref[...] = jnp.where(lane < HEAD_DIM, d0, d1)
            dq_ref[...] = jnp.zeros_like(dq_ref)

            @pl.when(n == 0)
            def _():
                _fill_bias(sl_ref, bias_ref)

            for gi, dil in enumerate(DILATIONS):

                def block(idx, carry, gi=gi, dil=dil):
                    r, b = idx % dil, idx // dil
                    qs = b * (ATT_BLOCK * dil) + r
                    ks = SUPER + (b - 1) * (ATT_BLOCK * dil) + r
                    first = jnp.logical_and(n == 0, b == 0).astype(jnp.int32)
                    rows = pl.ds(qs, ATT_BLOCK, stride=dil)
                    keys = pl.ds(ks, 2 * ATT_BLOCK, stride=dil)
                    q2 = _stack_heads(q_ref[rows, :])
                    do2 = _stack_heads(do_ref[rows, :])
                    kb = kw_ref[keys, :].astype(BF16)
                    vb = vw_ref[keys, :].astype(BF16)
                    s = _dot_nt(q2, kb) + bias_ref[2 * gi + first]
                    p = jnp.exp(s - _stack_cols(lse_ref[rows, :]))
                    dp = _dot_nt(do2, vb)
                    ds = (p * (dp - _stack_cols(dd_ref[rows, :]))).astype(BF16)
                    dq_ref[rows, :] += _unstack_heads(_dot(ds, kb)) * q_scale
                    dkw_ref[keys, :] += _dot_tn(ds, q2)
                    dvw_ref[keys, :] += _dot_tn(p.astype(BF16), do2)
                    return carry

                lax.fori_loop(0, SUPER // ATT_BLOCK, block, 0, unroll=ATT_UNROLL_BWD)

        dk_ref[...] = dkw_ref[0:SUPER, :]
        dv_ref[...] = dvw_ref[0:SUPER, :]

    last = ns - 1
    cur = lambda off: pl.BlockSpec((None, SUPER, LANES), lambda h, n: (h + off, jnp.minimum(n, last), 0))
    prev = lambda off: pl.BlockSpec((None, SUPER, LANES), lambda h, n: (h + off, jnp.clip(n - 1, 0, last), 0))
    nat = pl.BlockSpec((SUPER, LANES), lambda h, n: (jnp.minimum(n, last), h))
    late = pl.BlockSpec((None, SUPER, LANES), lambda h, n: (h, jnp.maximum(n - 1, 0), 0))
    dq, dk, dv = _call(
        body, name=name, grid=(hp, ns + 1),
        in_specs=[pl.BlockSpec((None, 2, 2 * ATT_BLOCK), lambda h, n: (h, 0, 0)), cur(0), cur(0), prev(0), cur(hp), prev(hp),
                  nat, cur(0), nat],
        out_specs=[cur(0), late, late],
        out_shape=[_sds((hp, t, LANES), F32)] * 3,
        scratch_shapes=[pltpu.VMEM((2 * SUPER, LANES), F32)] * 4 + [
            pltpu.VMEM((SUPER, LANES), F32), pltpu.VMEM((2 * len(DILATIONS), 2 * ATT_BLOCK, 2 * ATT_BLOCK), F32)],
    )(slopes, q, kv, kv, kv, kv, o, lse, d_o)
    return dq, dk, dv


def _loss_head(y, target, raw, g, name):
    t, d = y.shape
    tm = ROW_TILE

    def body(y_ref, t_ref, raw_ref, g_ref, sq_ref, dy_ref, draw_ref, dg_ref):
        i = pl.program_id(0)
        err = y_ref[...] - t_ref[...]
        dy = err * (1.0 / d)
        dy_ref[...] = dy
        draw, dg = _rms_bwd(raw_ref[...], g_ref[...], dy)
        draw_ref[...] = draw.astype(BF16)
        sq = jnp.zeros((8, LANES), F32) + jnp.sum(err * err)

        @pl.when(i == 0)
        def _():
            sq_ref[...] = sq
            dg_ref[...] = dg

        @pl.when(i > 0)
        def _():
            sq_ref[...] += sq
            dg_ref[...] += dg

    row = pl.BlockSpec((tm, d), lambda i: (i, 0))
    vec = pl.BlockSpec((1, d), lambda i: (0, 0))
    return _call(
        body, name=name, grid=(t // tm,), in_specs=[row, row, row, vec],
        out_specs=[pl.BlockSpec((8, LANES), lambda i: (0, 0)), row, row, vec],
        out_shape=[_sds((8, LANES), F32), _sds((t, d), F32), _sds((t, d), BF16), _sds((1, d), F32)])(y, target, raw, g)


def _bwd_matmul_norms(a_specs, a_args, a_tile, n_steps, w_spec, w_arg, w_mat, xa, ga, resid, xb, gb, name):
    t, d = xa.shape
    tm = ROW_TILE
    na = len(a_specs)
    second = xb is not None

    def body(*refs):
        a_refs = refs[:na]
        w_ref, xa_ref, ga_ref, res_ref = refs[na:na + 4]
        rest = refs[na + 4:]
        if second:
            xb_ref, gb_ref, dx_ref, d2_ref, dga_ref, dgb_ref, acc_ref = rest
        else:
            dx_ref, dga_ref, acc_ref = rest
        i, j = pl.program_id(0), pl.program_id(1)
        part = _dot_nt(a_tile(*a_refs), w_mat(w_ref))

        @pl.when(j == 0)
        def _():
            acc_ref[...] = part

        @pl.when(j > 0)
        def _():
            acc_ref[...] += part

        @pl.when(j == n_steps - 1)
        def _():
            da, dga = _rms_bwd(xa_ref[...], ga_ref[...], acc_ref[...])
            dx = res_ref[...] + da
            dx_ref[...] = dx
            if second:
                d2, dgb = _rms_bwd(xb_ref[...], gb_ref[...], dx)
                d2_ref[...] = d2.astype(BF16)

            @pl.when(i == 0)
            def _():
                dga_ref[...] = dga
                if second:
                    dgb_ref[...] = dgb

            @pl.when(i > 0)
            def _():
                dga_ref[...] += dga
                if second:
                    dgb_ref[...] += dgb

    row = pl.BlockSpec((tm, d), lambda i, j: (i, 0))
    vec = pl.BlockSpec((1, d), lambda i, j: (0, 0))
    in_specs = list(a_specs) + [w_spec, row, vec, row]
    args = list(a_args) + [w_arg, xa, ga, resid]
    if second:
        in_specs += [row, vec]
        args += [xb, gb]
        out_specs = [row, row, vec, vec]
        out_shape = [_sds((t, d), F32), _sds((t, d), BF16), _sds((1, d), F32), _sds((1, d), F32)]
    else:
        out_specs = [row, vec]
        out_shape = [_sds((t, d), F32), _sds((1, d), F32)]
    return _call(body, name=name, grid=(t // tm, n_steps), in_specs=in_specs, out_specs=out_specs,
                 out_shape=out_shape, scratch_shapes=[pltpu.VMEM((tm, d), F32)])(*args)


def _heads_to_rows(*refs):
    hp = refs[0].shape[0]
    cols = []
    for p in range(hp):
        v = refs[0][p]
        for r in refs[1:]:
            v = v + r[p]
        cols.append(v)
    return jnp.concatenate(cols, axis=-1).astype(BF16)


def _matmul_nt_rows(a, wg, layer, out_dtype, name):
    t, d = a.shape
    tm = ROW_TILE

    def body(a_ref, w_ref, o_ref):
        o_ref[...] = _dot_nt(a_ref[...], w_ref[...].reshape(d, d)).astype(out_dtype)

    row = pl.BlockSpec((tm, d), lambda i: (i, 0))
    return _call(body, name=name, grid=(t // tm,),
                 in_specs=[row, pl.BlockSpec((N_DEV, None, d // N_DEV, d), lambda i: (0, layer, 0, 0))],
                 out_specs=row, out_shape=_sds((t, d), out_dtype))(a, wg)


def _swiglu_bwd(d_ff, wg, layer, h, name):
    t, d = d_ff.shape
    fc = h.shape[-1]
    rows = wg.shape[2]
    tm = BIG_ROW_TILE

    def body(df_ref, w_ref, g_ref, u_ref, dh_ref):
        da = _dot_nt(df_ref[...], w_ref[...].reshape(2 * rows, d))
        gate, up = g_ref[...].astype(F32), u_ref[...].astype(F32)
        sig = jax.nn.sigmoid(gate)
        dh_ref[0] = (da * up * (sig * (1.0 + gate * (1.0 - sig)))).astype(BF16)
        dh_ref[1] = (da * (gate * sig)).astype(BF16)

    return _call(
        body, name=name, grid=(t // tm, 4),
        in_specs=[pl.BlockSpec((tm, d), lambda i, c: (i, 0)),
                  pl.BlockSpec((2, None, rows, d), lambda i, c: (c, layer, 0, 0)),
                  pl.BlockSpec((None, tm, fc), lambda i, c: (c, i, 0)),
                  pl.BlockSpec((None, tm, fc), lambda i, c: (c + 4, i, 0))],
        out_specs=pl.BlockSpec((None, 2, tm, fc), lambda i, c: (c, 0, i, 0)),
        out_shape=_sds((4, 2, t, fc), BF16))(d_ff, wg, h, h)


def _conv_bwd(p, d_z, cw, name):
    t, d3 = p.shape
    d = d3 // 3
    tm = ROW_TILE
    hb = tm // HALO
    nt = t // tm

    def body(p_ref, prev_ref, next_ref, dz_ref, dzn_ref, cw_ref, dp_ref, dcw_ref):
        i = pl.program_id(0)
        b = p_ref[:, 0:d].astype(F32)
        c = p_ref[:, d:2 * d].astype(F32)
        h = p_ref[:, 2 * d:3 * d].astype(F32)
        u = c * h
        hu = prev_ref[:, d:2 * d].astype(F32) * prev_ref[:, 2 * d:3 * d].astype(F32) * (i > 0).astype(F32)
        u1, u2 = _shift_down(u, hu, 1, tm), _shift_down(u, hu, 2, tm)
        uc = cw_ref[2:3, :] * u + cw_ref[1:2, :] * u1 + cw_ref[0:1, :] * u2
        dz = dz_ref[...].astype(F32)
        duc = dz * b
        dn = dzn_ref[...].astype(F32) * next_ref[:, 0:d].astype(F32) * (i < nt - 1).astype(F32)
        du = cw_ref[2:3, :] * duc + cw_ref[1:2, :] * _shift_up(duc, dn, 1, tm) + cw_ref[0:1, :] * _shift_up(duc, dn, 2, tm)
        dp_ref[:, 0:d] = (dz * uc).astype(BF16)
        dp_ref[:, d:2 * d] = (du * h).astype(BF16)
        dp_ref[:, 2 * d:3 * d] = (du * c).astype(BF16)
        dcw = jnp.concatenate([jnp.sum(duc * u2, axis=0, keepdims=True), jnp.sum(duc * u1, axis=0, keepdims=True),
                               jnp.sum(duc * u, axis=0, keepdims=True)], axis=0)

        @pl.when(i == 0)
        def _():
            dcw_ref[...] = dcw

        @pl.when(i > 0)
        def _():
            dcw_ref[...] += dcw

    last_halo = t // HALO - 1
    return _call(
        body, name=name, grid=(nt,),
        in_specs=[pl.BlockSpec((tm, d3), lambda i: (i, 0)),
                  pl.BlockSpec((HALO, d3), lambda i: (jnp.maximum(i * hb - 1, 0), 0)),
                  pl.BlockSpec((HALO, d3), lambda i: (jnp.minimum((i + 1) * hb, last_halo), 0)),
                  pl.BlockSpec((tm, d), lambda i: (i, 0)),
                  pl.BlockSpec((HALO, d), lambda i: (jnp.minimum((i + 1) * hb, last_halo), 0)),
                  pl.BlockSpec((3, d), lambda i: (0, 0))],
        out_specs=[pl.BlockSpec((tm, d3), lambda i: (i, 0)), pl.BlockSpec((3, d), lambda i: (0, 0))],
        out_shape=[_sds((t, d3), BF16), _sds((3, d), F32)])(p, p, p, d_z, d_z, cw)


def _grad_weight(a_specs, a_args, a_tile, b_specs, b_args, b_tile, n_out, acc_shape, out_spec, out_shape, t, name):
    tt = BIG_ROW_TILE
    na, nb = len(a_specs), len(b_specs)

    def body(*refs):
        a_refs, b_refs = refs[:na], refs[na:na + nb]
        o_ref, acc_ref = refs[na + nb:]
        s = pl.program_id(1)
        part = _dot_tn(a_tile(*a_refs), b_tile(*b_refs))

        @pl.when(s == 0)
        def _():
            acc_ref[...] = part

        @pl.when(s > 0)
        def _():
            acc_ref[...] += part

        @pl.when(s == t // tt - 1)
        def _():
            o_ref[...] = acc_ref[...].astype(BF16).reshape(o_ref.shape)

    return _call(body, name=name, grid=(n_out, t // tt), in_specs=list(a_specs) + list(b_specs), out_specs=out_spec,
                 out_shape=out_shape, scratch_shapes=[pltpu.VMEM(acc_shape, F32)])(*a_args, *b_args)


def _ident(ref):
    return ref[...]


def kernel(x, norm_g, conv_in_w, conv_w, conv_out_w, kv_norm_g, kv_w, q_w, o_w, ffn_in_w, ffn_out_w, loss_target, m_norm_g, m_conv_in_w, m_conv_w, m_conv_out_w, m_kv_norm_g, m_kv_w, m_q_w, m_o_w, m_ffn_in_w, m_ffn_out_w, v_norm_g, v_conv_in_w, v_conv_w, v_conv_out_w, v_kv_norm_g, v_kv_w, v_q_w, v_o_w, v_ffn_in_w, v_ffn_out_w):
    x0 = x[0]
    target = loss_target[0]
    t, d = x0.shape
    depth = norm_g.shape[0]
    n_a = conv_in_w.shape[0]
    n_b = q_w.shape[0]
    hp = d // LANES
    tm, tg = ROW_TILE, BIG_ROW_TILE
    assert t % SUPER == 0 and d % LANES == 0 and depth == n_a + n_b
    dev = 4 * lax.axis_index("x") + 2 * lax.axis_index("y") + lax.axis_index("c")

    n_small = 4 * depth + 3 * n_a
    small_rows = -(-(n_small + 1) // 8) * 8
    small_local = jnp.concatenate([norm_g.reshape(4 * depth, -1), conv_w.reshape(3 * n_a, -1),
                                   jnp.zeros((small_rows - n_small, norm_g.shape[-1]), F32)], axis=0)
    big = {"conv_in_w": conv_in_w, "conv_out_w": conv_out_w, "kv_w": kv_w[None], "q_w": q_w, "o_w": o_w,
           "ffn_in_w": ffn_in_w, "ffn_out_w": ffn_out_w}
    names = list(big)
    gathered = _all_gather([small_local] + [_cast_bf16(big[k], "cast_" + k) for k in names], "gather_weights")
    small_all = gathered[0].transpose(1, 0, 2).reshape(small_rows, d)
    wg = dict(zip(names, gathered[1:]))
    gain = lambda layer, k: small_all[4 * layer + k][None]
    taps = lambda layer: small_all[4 * depth + 3 * layer: 4 * depth + 3 * layer + 3]
    g_kv = kv_norm_g[None]
    slopes = _alibi_slopes(d // HEAD_DIM)
    fc = wg["ffn_in_w"].shape[-1]
    cb = wg["conv_in_w"].shape[-1]
    kvb = wg["kv_w"].shape[-1]
    q_scale = HEAD_DIM ** -0.5

    saved = []
    kv = kvn = None
    xs = x0
    for layer in range(depth):
        tag = f"_l{layer}"
        s = {"x_in": xs}
        if layer < n_a:
            s["p"], s["xn"] = _norm_matmul_cols(xs, gain(layer, 0), wg["conv_in_w"], layer, "cols", "conv_in" + tag)
            s["z"] = _conv_fwd(s["p"], taps(layer), "conv" + tag)
            s["mix"], x_mid = _matmul_norm_residual(s["z"][None], wg["conv_out_w"], layer, gain(layer, 1), xs, "conv_out" + tag)
        else:
            j = layer - n_a
            if kv is None:
                kv, kvn = _norm_matmul_cols(xs, g_kv, wg["kv_w"], 0, "heads", "kv_proj")
            s["q"], s["xn"] = _norm_matmul_heads(xs, gain(layer, 0), wg["q_w"], j, q_scale, "q_proj" + tag)
            s["o"], s["lse"] = _attention_fwd(s["q"], kv, slopes, "attention" + tag)
            s["mix"], x_mid = _matmul_norm_residual(s["o"][None], wg["o_w"], j, gain(layer, 1), xs, "o_proj" + tag)
        s["x_mid"] = x_mid
        s["h"], s["fn"] = _norm_matmul_cols(x_mid, gain(layer, 2), wg["ffn_in_w"], layer, "blocks", "ffn_in" + tag)
        s["a"] = _swiglu_fwd(s["h"], "swiglu" + tag)
        s["ff"], xs = _matmul_norm_residual(s["a"], wg["ffn_out_w"], layer, gain(layer, 3), x_mid, "ffn_out" + tag)
        saved.append(s)

    last = saved[-1]
    sq, dx_out, d_ff, dg3 = _loss_head(xs, target, last["ff"], gain(depth - 1, 3), "loss_head")
    loss = lax.psum(sq[0, 0] * (0.5 / d), ("x", "y", "c"))

    dgain = {(depth - 1, 3): dg3}
    dtaps = {}
    grads = {k: [None] * big[k].shape[0] for k in names}
    row_i = lambda i, j: (i, 0)
    dkv_parts = []
    for layer in reversed(range(depth)):
        tag = f"_l{layer}"
        s = saved[layer]
        dh = _swiglu_bwd(d_ff, wg["ffn_out_w"], layer, s["h"], "swiglu_bwd" + tag)
        rows_out = wg["ffn_out_w"].shape[2]
        grads["ffn_out_w"][layer] = _grad_weight(
            [pl.BlockSpec((None, tg, fc), lambda c, i: (c, i, 0))], [s["a"]], _ident,
            [pl.BlockSpec((tg, d), lambda c, i: (i, 0))], [d_ff], _ident,
            4, (fc, d), pl.BlockSpec((2, rows_out, d), lambda c, i: (c, 0, 0)), _sds((N_DEV, rows_out, d), BF16), t,
            "grad_ffn_out" + tag)
        grads["ffn_in_w"][layer] = _grad_weight(
            [pl.BlockSpec((tg, d), lambda j, i: (i, 0))], [s["fn"]], _ident,
            [pl.BlockSpec((None, None, tg, fc), lambda j, i: (j % 4, j // 4, i, 0))], [dh], _ident,
            N_DEV, (d, fc), pl.BlockSpec((None, d, fc), lambda j, i: (j, 0, 0)), _sds((N_DEV, d, fc), BF16), t,
            "grad_ffn_in" + tag)
        dx_mid, d_mix, dg2, dg1 = _bwd_matmul_norms(
            [pl.BlockSpec((None, None, tm, fc), lambda i, j: (j % 4, j // 4, i, 0))], [dh], _ident, N_DEV,
            pl.BlockSpec((None, None, d, fc), lambda i, j: (j, layer, 0, 0)), wg["ffn_in_w"], _ident,
            s["x_mid"], gain(layer, 2), dx_out, s["mix"], gain(layer, 1), "ffn_in_bwd" + tag)
        dgain[(layer, 2)], dgain[(layer, 1)] = dg2, dg1
        full_rows = pl.BlockSpec((N_DEV, d // N_DEV, d), lambda j, i: (0, 0, 0))
        rows_w = lambda wname, idx: (pl.BlockSpec((N_DEV, None, d // N_DEV, d), lambda i, j: (0, idx, 0, 0)), wg[wname],
                                     lambda w_ref: w_ref[...].reshape(d, d))
        if layer < n_a:
            d_z = _matmul_nt_rows(d_mix, wg["conv_out_w"], layer, BF16, "conv_out_bwd" + tag)
            grads["conv_out_w"][layer] = _grad_weight(
                [pl.BlockSpec((tg, d), lambda j, i: (i, 0))], [s["z"]], _ident,
                [pl.BlockSpec((tg, d), lambda j, i: (i, 0))], [d_mix], _ident,
                1, (d, d), full_rows, _sds((N_DEV, d // N_DEV, d), BF16), t, "grad_conv_out" + tag)
            d_p, dtaps[layer] = _conv_bwd(s["p"], d_z, taps(layer), "conv_bwd" + tag)
            grads["conv_in_w"][layer] = _grad_weight(
                [pl.BlockSpec((tg, d), lambda j, i: (i, 0))], [s["xn"]], _ident,
                [pl.BlockSpec((tg, cb), lambda j, i: (i, j))], [d_p], _ident,
                N_DEV, (d, cb), pl.BlockSpec((None, d, cb), lambda j, i: (j, 0, 0)), _sds((N_DEV, d, cb), BF16), t,
                "grad_conv_in" + tag)
            a_specs, a_args, a_tile, n_steps = [pl.BlockSpec((tm, cb), lambda i, j: (i, j))], [d_p], _ident, N_DEV
            w_spec = pl.BlockSpec((None, None, d, cb), lambda i, j: (j, layer, 0, 0))
            w_arg, w_mat = wg["conv_in_w"], _ident
            resid = dx_mid
        else:
            j_b = layer - n_a
            d_o = _matmul_nt_rows(d_mix, wg["o_w"], j_b, F32, "o_proj_bwd" + tag)
            grads["o_w"][j_b] = _grad_weight(
                [pl.BlockSpec((tg, d), lambda j, i: (i, 0))], [s["o"]], _ident,
                [pl.BlockSpec((tg, d), lambda j, i: (i, 0))], [d_mix], _ident,
                1, (d, d), full_rows, _sds((N_DEV, d // N_DEV, d), BF16), t, "grad_o" + tag)
            dq, dk, dv = _attention_bwd(s["q"], kv, s["o"], s["lse"], d_o, slopes, q_scale, "attention_bwd" + tag)
            dkv_parts.append((dk, dv))
            heads_spec = pl.BlockSpec((hp, tg, LANES), lambda j, i: (0, i, 0))
            grads["q_w"][j_b] = _grad_weight(
                [pl.BlockSpec((tg, d), lambda j, i: (i, 0))], [s["xn"]], _ident,
                [heads_spec], [dq], _heads_to_rows,
                1, (d, d), full_rows, _sds((N_DEV, d // N_DEV, d), BF16), t, "grad_q" + tag)
            a_specs, a_args, a_tile, n_steps = [pl.BlockSpec((hp, tm, LANES), lambda i, j: (0, i, 0))], [dq], _heads_to_rows, 1
            w_spec, w_arg, w_mat = rows_w("q_w", j_b)
            resid = dx_mid
            if layer == n_a:
                pieces = kvb // LANES
                halves = []
                for src in (0, 1):
                    halves.append([part[src] for part in dkv_parts])
                n_half = len(dkv_parts)
                kv_args = [arr for src in (0, 1) for arr in halves[src]]

                def kv_block(src, j):
                    return jnp.where((j // 4) == src, j % 4, 0)

                def kv_tile(axis):
                    def tile(*refs):
                        keys = _heads_to_rows(*refs[:n_half])
                        vals = _heads_to_rows(*refs[n_half:])
                        return jnp.where(pl.program_id(axis) < 4, keys, vals)
                    return tile

                kv_specs = [pl.BlockSpec((pieces, tm, LANES), functools.partial(lambda i, j, src: (kv_block(src, j), i, 0), src=src))
                            for src in (0, 1) for _ in range(n_half)]
                resid, dgain["kv"] = _bwd_matmul_norms(
                    kv_specs, kv_args, kv_tile(1), N_DEV,
                    pl.BlockSpec((None, None, d, kvb), lambda i, j: (j, 0, 0, 0)), wg["kv_w"], _ident,
                    s["x_in"], g_kv, dx_mid, None, None, "kv_proj_bwd")
                kv_b_specs = [pl.BlockSpec((pieces, tg, LANES), functools.partial(lambda j, i, src: (kv_block(src, j), i, 0), src=src))
                              for src in (0, 1) for _ in range(n_half)]
                grads["kv_w"][0] = _grad_weight(
                    [pl.BlockSpec((tg, d), lambda j, i: (i, 0))], [kvn], _ident,
                    kv_b_specs, kv_args, kv_tile(0),
                    N_DEV, (d, kvb), pl.BlockSpec((None, d, kvb), lambda j, i: (j, 0, 0)), _sds((N_DEV, d, kvb), BF16), t,
                    "grad_kv")
        if layer > 0:
            prev = saved[layer - 1]
            dx_out, d_ff, dg0, dg3p = _bwd_matmul_norms(
                a_specs, a_args, a_tile, n_steps, w_spec, w_arg, w_mat,
                s["x_in"], gain(layer, 0), resid, prev["ff"], gain(layer - 1, 3), "mixer_in_bwd" + tag)
            dgain[(layer, 0)], dgain[(layer - 1, 3)] = dg0, dg3p
        else:
            grad_x, dg0 = _bwd_matmul_norms(
                a_specs, a_args, a_tile, n_steps, w_spec, w_arg, w_mat,
                s["x_in"], gain(layer, 0), resid, None, None, "mixer_in_bwd" + tag)
            dgain[(layer, 0)] = dg0

    small_grad = jnp.concatenate(
        [dgain[(layer, k)] for layer in range(depth) for k in range(4)] + [dtaps[layer] for layer in range(n_a)]
        + [dgain["kv"]] + [jnp.zeros((small_rows - n_small - 1, d), F32)], axis=0)
    small_grads_all = _all_gather([small_grad], "gather_small_grads")[0]
    lo = dev * (d // N_DEV)

    def pack(ng, cwp, kvg):
        rows = jnp.concatenate([ng.reshape(4 * depth, -1), cwp.reshape(3 * n_a, -1)], axis=0)
        z = lax.dynamic_update_slice(jnp.zeros((small_rows, d), F32), rows, (0, lo))
        return lax.dynamic_update_slice(z, kvg[None], (n_small, 0))

    w_small = lax.dynamic_update_slice(small_all, g_kv, (n_small, 0))
    m_small, v_small = pack(m_norm_g, m_conv_w, m_kv_norm_g), pack(v_norm_g, v_conv_w, v_kv_norm_g)
    sm = _small_adamw(small_grads_all, w_small, m_small, v_small, "adamw_small")

    def unpack(a):
        mine = lax.dynamic_slice(a, (0, lo), (small_rows, d // N_DEV))
        return (mine[:4 * depth].reshape(norm_g.shape), mine[4 * depth:n_small].reshape(conv_w.shape), a[n_small])

    small_out = [unpack(a) for a in sm]

    order = [(k, i) for k in names for i in range(big[k].shape[0])]
    partials = [grads[k][i] for k, i in order]
    from_sibling = _exchange_sibling(partials, "reduce_sibling")
    chip_parts = [_add_pairs(g, s_, f"pair_sum_{k}_{i}") for (k, i), g, s_ in zip(order, partials, from_sibling)]
    from_chips = _exchange_chips(chip_parts, "reduce_chips")
    moments = {"conv_in_w": (m_conv_in_w, v_conv_in_w), "conv_out_w": (m_conv_out_w, v_conv_out_w),
               "kv_w": (m_kv_w[None], v_kv_w[None]), "q_w": (m_q_w, v_q_w), "o_w": (m_o_w, v_o_w),
               "ffn_in_w": (m_ffn_in_w, v_ffn_in_w), "ffn_out_w": (m_ffn_out_w, v_ffn_out_w)}
    res = {k: [None] * big[k].shape[0] for k in names}
    for (k, i), part, got in zip(order, chip_parts, from_chips):
        res[k][i] = _reduce_adamw(part, got, big[k][i], moments[k][0][i], moments[k][1][i], f"adamw_{k}_{i}")

    def big_out(k, which):
        st = jnp.stack([res[k][i][which] for i in range(big[k].shape[0])], axis=0)
        return st[0] if k == "kv_w" else st

    out_names = ["norm_g", "conv_in_w", "conv_w", "conv_out_w", "kv_norm_g", "kv_w", "q_w", "o_w", "ffn_in_w", "ffn_out_w"]
    small_pos = {"norm_g": 0, "conv_w": 1, "kv_norm_g": 2}
    outs = [loss, grad_x[None]]
    for which in range(4):
        for k in out_names:
            outs.append(small_out[which][small_pos[k]] if k in small_pos else big_out(k, which))
    return tuple(outs)
```

```python
import functools
import math

import numpy as np
import jax
import jax.numpy as jnp
from jax import lax
from jax.experimental import pallas as pl
from jax.experimental.pallas import tpu as pltpu

F32 = jnp.float32
BF16 = jnp.bfloat16

N_DEV = 8
RMS_EPS = 1e-6
HEAD_DIM = 64
LANES = 128
ATT_BLOCK = 128
DILATIONS = (1, 4, 16)
SUPER = ATT_BLOCK * DILATIONS[-1]
NEG = -1e30
ATT_UNROLL_FWD = 8
ATT_UNROLL_BWD = 4

ADAM_LR, ADAM_B1, ADAM_B2, ADAM_EPS, ADAM_WD, ADAM_STEP = 0.001, 0.9, 0.999, 1e-08, 0.01, 10

ROW_TILE = 512
BIG_ROW_TILE = 1024
BWD_ROW_TILE = 512
MESH = pl.DeviceIdType.MESH


def _call(body, *, name, grid=None, in_specs=None, out_specs=None, out_shape=None, scratch_shapes=(), prefetch=False,
          **params):
    cp = pltpu.CompilerParams(**params) if params else None
    if prefetch:
        spec = pltpu.PrefetchScalarGridSpec(num_scalar_prefetch=1, grid=grid, in_specs=in_specs, out_specs=out_specs,
                                            scratch_shapes=list(scratch_shapes))
        return pl.pallas_call(body, name=name, grid_spec=spec, out_shape=out_shape, compiler_params=cp)
    kwargs = {k: v for k, v in (("grid", grid), ("in_specs", in_specs), ("out_specs", out_specs)) if v is not None}
    return pl.pallas_call(body, name=name, out_shape=out_shape, scratch_shapes=list(scratch_shapes),
                          compiler_params=cp, **kwargs)


def _sds(shape, dtype):
    return jax.ShapeDtypeStruct(tuple(shape), dtype)


def _rms(x, g):
    r = lax.rsqrt(jnp.mean(x * x, axis=-1, keepdims=True) + RMS_EPS)
    return x * r * g


def _rms_bwd(x, g, dy):
    r = lax.rsqrt(jnp.mean(x * x, axis=-1, keepdims=True) + RMS_EPS)
    xh = x * r
    dxh = dy * g
    dx = r * (dxh - xh * jnp.mean(dxh * xh, axis=-1, keepdims=True))
    return dx, jnp.sum(dy * xh, axis=0, keepdims=True)


def _dot(a, b):
    return jnp.dot(a, b, preferred_element_type=F32)


def _dot_nt(a, b):
    return lax.dot_general(a, b, (((1,), (1,)), ((), ())), preferred_element_type=F32)


def _dot_tn(a, b):
    return lax.dot_general(a, b, (((0,), (0,)), ((), ())), preferred_element_type=F32)


def _mesh_pos():
    return lax.axis_index("x"), lax.axis_index("y"), lax.axis_index("c")


def _all_gather(arrs, name):
    n = len(arrs)

    def body(*refs):
        ins, outs = refs[:n], refs[n:2 * n]
        send_sems, recv_sems, local_sems = refs[2 * n:]
        x, y, c = _mesh_pos()
        me, sibling = (x, y, c), (x, y, 1 - c)
        chips = [(1 - x, y), (x, 1 - y), (1 - x, 1 - y)]

        def copy(a, k, block, to, src=None):
            dst = outs[a].at[4 * block[0] + 2 * block[1] + block[2]]
            return pltpu.make_async_remote_copy(
                src_ref=dst if src is None else src, dst_ref=dst, send_sem=send_sems.at[a, k],
                recv_sem=recv_sems.at[a, k], device_id=to, device_id_type=MESH)

        started = []
        for a in range(n):
            mine = pltpu.make_async_copy(ins[a], outs[a].at[4 * x + 2 * y + c], local_sems.at[a])
            mine.start()
            started.append(mine)
        first = []
        for a in range(n):
            first.append(copy(a, 0, me, sibling, src=ins[a]))
            first += [copy(a, 1 + j, me, (*chip, c), src=ins[a]) for j, chip in enumerate(chips)]
        for cp in first:
            cp.start()
        passed = []
        for a in range(n):
            for j, chip in enumerate(chips):
                copy(a, 1 + j, (*chip, c), me).wait_recv()
                fwd = copy(a, 4 + j, (*chip, c), sibling)
                fwd.start()
                passed.append(fwd)
        for a in range(n):
            copy(a, 0, sibling, me).wait_recv()
            for j, chip in enumerate(chips):
                copy(a, 4 + j, (*chip, 1 - c), me).wait_recv()
        for cp in first + passed:
            cp.wait_send()
        for cp in started:
            cp.wait()

    any_spec = pl.BlockSpec(memory_space=pl.ANY)
    outs = _call(
        body, name=name, in_specs=[any_spec] * n, out_specs=[any_spec] * n,
        out_shape=[_sds((N_DEV,) + a.shape, a.dtype) for a in arrs],
        scratch_shapes=[pltpu.SemaphoreType.DMA((n, 7)), pltpu.SemaphoreType.DMA((n, 7)), pltpu.SemaphoreType.DMA((n,))],
        has_side_effects=True,
    )(*arrs)
    return list(outs)


def _exchange_sibling(grads, name):
    n = len(grads)

    def body(*refs):
        ins, outs = refs[:n], refs[n:2 * n]
        send_sems, recv_sems = refs[2 * n:]
        x, y, c = _mesh_pos()
        sibling = (x, y, 1 - c)
        copies = []
        for a in range(n):
            for q in range(4):
                cp = pltpu.make_async_remote_copy(
                    src_ref=ins[a].at[2 * q + (1 - c)], dst_ref=outs[a].at[q], send_sem=send_sems.at[a, q],
                    recv_sem=recv_sems.at[a, q], device_id=sibling, device_id_type=MESH)
                cp.start()
                copies.append(cp)
        for cp in copies:
            cp.wait_recv()
        for cp in copies:
            cp.wait_send()

    any_spec = pl.BlockSpec(memory_space=pl.ANY)
    outs = _call(
        body, name=name, in_specs=[any_spec] * n, out_specs=[any_spec] * n,
        out_shape=[_sds((4,) + g.shape[1:], g.dtype) for g in grads],
        scratch_shapes=[pltpu.SemaphoreType.DMA((n, 4)), pltpu.SemaphoreType.DMA((n, 4))],
        has_side_effects=True,
    )(*grads)
    return list(outs)


def _exchange_chips(parts, name):
    n = len(parts)

    def body(*refs):
        ins, outs = refs[:n], refs[n:2 * n]
        send_sems, recv_sems = refs[2 * n:]
        x, y, c = _mesh_pos()
        chips = [(1 - x, y), (x, 1 - y), (1 - x, 1 - y)]
        copies = []
        for a in range(n):
            for k, chip in enumerate(chips):
                cp = pltpu.make_async_remote_copy(
                    src_ref=ins[a].at[2 * chip[0] + chip[1]], dst_ref=outs[a].at[k], send_sem=send_sems.at[a, k],
                    recv_sem=recv_sems.at[a, k], device_id=(*chip, c), device_id_type=MESH)
                cp.start()
                copies.append(cp)
        for cp in copies:
            cp.wait_recv()
        for cp in copies:
            cp.wait_send()

    any_spec = pl.BlockSpec(memory_space=pl.ANY)
    outs = _call(
        body, name=name, in_specs=[any_spec] * n, out_specs=[any_spec] * n,
        out_shape=[_sds((3,) + p.shape[1:], p.dtype) for p in parts],
        scratch_shapes=[pltpu.SemaphoreType.DMA((n, 3)), pltpu.SemaphoreType.DMA((n, 3))],
        has_side_effects=True,
    )(*parts)
    return list(outs)


def _row_tile(rows, cap=512):
    t = min(rows, cap)
    while rows % t or (t % 16 and t != rows):
        t -= 1
    return t


def _as2d(a):
    return a.reshape(-1, a.shape[-1])


def _cast_bf16(w, name):
    w2 = _as2d(w)
    rows, cols = w2.shape
    tr = _row_tile(rows)

    def body(w_ref, o_ref):
        o_ref[...] = w_ref[...].astype(BF16)

    spec = pl.BlockSpec((tr, cols), lambda i: (i, 0))
    out = _call(body, name=name, grid=(rows // tr,), in_specs=[spec], out_specs=spec,
                out_shape=_sds(w2.shape, BF16))(w2)
    return out.reshape(w.shape)


def _add_pairs(grad, from_sibling, name):
    blk = grad.shape[1:]
    g2 = grad.reshape(4, 2, -1, blk[-1])
    s2 = from_sibling.reshape(4, -1, blk[-1])
    rows, cols = s2.shape[1:]
    tr = _row_tile(rows)
    core = lax.axis_index("c").astype(jnp.int32).reshape(1)

    def body(c_ref, g_ref, s_ref, o_ref):
        o_ref[...] = (g_ref[...].astype(F32) + s_ref[...].astype(F32)).astype(BF16)

    out = _call(
        body, name=name, grid=(4, rows // tr), prefetch=True,
        in_specs=[pl.BlockSpec((None, None, tr, cols), lambda q, i, c_ref: (q, c_ref[0], i, 0)),
                  pl.BlockSpec((None, tr, cols), lambda q, i, c_ref: (q, i, 0))],
        out_specs=pl.BlockSpec((None, tr, cols), lambda q, i, c_ref: (q, i, 0)),
        out_shape=_sds(s2.shape, BF16))(core, g2, s2)
    return out.reshape((4,) + blk)


def _adamw_math(w, g, m, v):
    m = ADAM_B1 * m + (1.0 - ADAM_B1) * g
    v = ADAM_B2 * v + (1.0 - ADAM_B2) * (g * g)
    m_hat = m / (1.0 - ADAM_B1 ** ADAM_STEP)
    v_hat = v / (1.0 - ADAM_B2 ** ADAM_STEP)
    delta = -ADAM_LR * (m_hat / (jnp.sqrt(v_hat) + ADAM_EPS) + ADAM_WD * w)
    return delta, m, v


def _reduce_adamw(part, from_chips, w, m, v, name):
    shape = w.shape
    cols = shape[-1]
    p2 = part.reshape(4, -1, cols)
    r2 = from_chips.reshape(3, -1, cols)
    w2, m2, v2 = _as2d(w), _as2d(m), _as2d(v)
    rows = w2.shape[0]
    tr = _row_tile(rows, 256)
    chip = (2 * lax.axis_index("x") + lax.axis_index("y")).astype(jnp.int32).reshape(1)

    def body(chip_ref, p_ref, r_ref, w_ref, m_ref, v_ref, g_ref, d_ref, nm_ref, nv_ref):
        g = p_ref[...].astype(F32)
        for k in range(3):
            g = g + r_ref[k].astype(F32)
        delta, nm, nv = _adamw_math(w_ref[...], g, m_ref[...], v_ref[...])
        g_ref[...] = g
        d_ref[...] = delta
        nm_ref[...] = nm
        nv_ref[...] = nv

    spec = pl.BlockSpec((tr, cols), lambda i, chip_ref: (i, 0))
    outs = _call(
        body, name=name, grid=(rows // tr,), prefetch=True,
        in_specs=[pl.BlockSpec((None, tr, cols), lambda i, chip_ref: (chip_ref[0], i, 0)),
                  pl.BlockSpec((3, tr, cols), lambda i, chip_ref: (0, i, 0)), spec, spec, spec],
        out_specs=[spec] * 4, out_shape=[_sds(w2.shape, F32)] * 4)(chip, p2, r2, w2, m2, v2)
    return [o.reshape(shape) for o in outs]


def _small_adamw(gathered, w, m, v, name):
    def body(a_ref, w_ref, m_ref, v_ref, g_ref, d_ref, nm_ref, nv_ref):
        g = a_ref[0]
        for k in range(1, N_DEV):
            g = g + a_ref[k]
        delta, nm, nv = _adamw_math(w_ref[...], g, m_ref[...], v_ref[...])
        g_ref[...] = g
        d_ref[...] = delta
        nm_ref[...] = nm
        nv_ref[...] = nv

    return _call(body, name=name, out_shape=[_sds(w.shape, F32)] * 4)(gathered, w, m, v)


def _norm_matmul_cols(x, g, wg, layer, mode, name):
    t, d = x.shape
    nb = wg.shape[-1]
    tm = BIG_ROW_TILE
    pieces = nb // LANES

    def body(x_ref, g_ref, w_ref, y_ref, xn_ref):
        @pl.when(pl.program_id(1) == 0)
        def _():
            xn_ref[...] = _rms(x_ref[...], g_ref[...]).astype(BF16)

        y = _dot(xn_ref[...], w_ref[...])
        if mode == "heads":
            for p in range(pieces):
                y_ref[p] = y[:, p * LANES:(p + 1) * LANES]
        else:
            y_ref[...] = y.astype(BF16)

    if mode == "cols":
        y_shape, y_spec = _sds((t, N_DEV * nb), BF16), pl.BlockSpec((tm, nb), lambda i, j: (i, j))
    else:
        y_shape = _sds((N_DEV * pieces, t, LANES), F32)
        y_spec = pl.BlockSpec((pieces, tm, LANES), lambda i, j: (j, i, 0))
    return _call(
        body, name=name, grid=(t // tm, N_DEV),
        in_specs=[pl.BlockSpec((tm, d), lambda i, j: (i, 0)), pl.BlockSpec((1, d), lambda i, j: (0, 0)),
                  pl.BlockSpec((None, None, d, nb), lambda i, j: (j, layer, 0, 0))],
        out_specs=[y_spec, pl.BlockSpec((tm, d), lambda i, j: (i, 0))],
        out_shape=[y_shape, _sds((t, d), BF16)])(x, g, wg)


def _ffn_in_swiglu(x, g, wg, layer, name):
    t, d = x.shape
    fc = wg.shape[-1]
    tm = BIG_ROW_TILE

    def body(x_ref, g_ref, wg_ref, wu_ref, gate_ref, up_ref, a_ref, xn_ref):
        @pl.when(pl.program_id(1) == 0)
        def _():
            xn_ref[...] = _rms(x_ref[...], g_ref[...]).astype(BF16)

        xn = xn_ref[...]
        gate, up = _dot(xn, wg_ref[...]), _dot(xn, wu_ref[...])
        gate_ref[...] = gate.astype(BF16)
        up_ref[...] = up.astype(BF16)
        a_ref[...] = (gate * jax.nn.sigmoid(gate) * up).astype(BF16)

    chunk = pl.BlockSpec((None, tm, fc), lambda i, c: (c, i, 0))
    return _call(
        body, name=name, grid=(t // tm, 4),
        in_specs=[pl.BlockSpec((tm, d), lambda i, c: (i, 0)), pl.BlockSpec((1, d), lambda i, c: (0, 0)),
                  pl.BlockSpec((None, None, d, fc), lambda i, c: (c, layer, 0, 0)),
                  pl.BlockSpec((None, None, d, fc), lambda i, c: (c + 4, layer, 0, 0))],
        out_specs=[chunk, chunk, chunk, pl.BlockSpec((tm, d), lambda i, c: (i, 0))],
        out_shape=[_sds((4, t, fc), BF16)] * 3 + [_sds((t, d), BF16)])(x, g, wg, wg)


def _norm_matmul_heads(x, g, wg, layer, scale, name):
    t, d = x.shape
    tm = ROW_TILE
    hp = d // LANES

    def body(x_ref, g_ref, w_ref, y_ref, xn_ref):
        xn = _rms(x_ref[...], g_ref[...]).astype(BF16)
        xn_ref[...] = xn
        y = _dot(xn, w_ref[...].reshape(d, d)) * scale
        for p in range(hp):
            y_ref[p] = y[:, p * LANES:(p + 1) * LANES]

    return _call(
        body, name=name, grid=(t // tm,),
        in_specs=[pl.BlockSpec((tm, d), lambda i: (i, 0)), pl.BlockSpec((1, d), lambda i: (0, 0)),
                  pl.BlockSpec((N_DEV, None, d // N_DEV, d), lambda i: (0, layer, 0, 0))],
        out_specs=[pl.BlockSpec((hp, tm, LANES), lambda i: (0, i, 0)), pl.BlockSpec((tm, d), lambda i: (i, 0))],
        out_shape=[_sds((hp, t, LANES), F32), _sds((t, d), BF16)])(x, g, wg)


def _shift_down(u, halo, k, tm):
    row = lax.broadcasted_iota(jnp.int32, u.shape, 0)
    out = pltpu.roll(u, k, 0)
    for j in range(k):
        out = jnp.where(row == j, halo[halo.shape[0] - k + j:halo.shape[0] - k + j + 1, :], out)
    return out


def _shift_up(u, halo, k, tm):
    row = lax.broadcasted_iota(jnp.int32, u.shape, 0)
    out = pltpu.roll(u, tm - k, 0)
    for j in range(k):
        out = jnp.where(row == tm - k + j, halo[j:j + 1, :], out)
    return out


HALO = 16


def _conv_fwd(p, cw, name):
    t, d3 = p.shape
    d = d3 // 3
    tm = ROW_TILE
    hb = tm // HALO

    def body(p_ref, prev_ref, cw_ref, z_ref):
        i = pl.program_id(0)
        b = p_ref[:, 0:d].astype(F32)
        u = p_ref[:, d:2 * d].astype(F32) * p_ref[:, 2 * d:3 * d].astype(F32)
        keep = (i > 0).astype(F32)
        hu = prev_ref[:, d:2 * d].astype(F32) * prev_ref[:, 2 * d:3 * d].astype(F32) * keep
        uc = cw_ref[2:3, :] * u + cw_ref[1:2, :] * _shift_down(u, hu, 1, tm) + cw_ref[0:1, :] * _shift_down(u, hu, 2, tm)
        z_ref[...] = (b * uc).astype(BF16)

    return _call(
        body, name=name, grid=(t // tm,),
        in_specs=[pl.BlockSpec((tm, d3), lambda i: (i, 0)),
                  pl.BlockSpec((HALO, d3), lambda i: (jnp.maximum(i * hb - 1, 0), 0)),
                  pl.BlockSpec((3, d), lambda i: (0, 0))],
        out_specs=pl.BlockSpec((tm, d), lambda i: (i, 0)), out_shape=_sds((t, d), BF16))(p, p, cw)


def _matmul_norm_residual(a3, wg, layer, g, x_res, name):
    kc_n, t, kc = a3.shape
    d = wg.shape[-1]
    per = N_DEV // kc_n
    rows = wg.shape[2]
    tm = BIG_ROW_TILE

    def body(a_ref, w_ref, g_ref, x_ref, raw_ref, xo_ref, acc_ref):
        c = pl.program_id(1)
        part = _dot(a_ref[...], w_ref[...].reshape(per * rows, d))

        @pl.when(c == 0)
        def _():
            acc_ref[...] = part

        @pl.when(c > 0)
        def _():
            acc_ref[...] += part

        @pl.when(c == kc_n - 1)
        def _():
            raw = acc_ref[...]
            raw_ref[...] = raw
            xo_ref[...] = x_ref[...] + _rms(raw, g_ref[...])

    row_spec = pl.BlockSpec((tm, d), lambda i, c: (i, 0))
    return _call(
        body, name=name, grid=(t // tm, kc_n),
        in_specs=[pl.BlockSpec((None, tm, kc), lambda i, c: (c, i, 0)),
                  pl.BlockSpec((per, None, rows, d), lambda i, c: (c, layer, 0, 0)),
                  pl.BlockSpec((1, d), lambda i, c: (0, 0)), row_spec],
        out_specs=[row_spec, row_spec], out_shape=[_sds((t, d), F32)] * 2,
        scratch_shapes=[pltpu.VMEM((tm, d), F32)])(a3, wg, g, x_res)


def _alibi_slopes(n_heads):
    hh = np.arange(n_heads, dtype=np.float32) + 1.0
    s = np.power(2.0, -8.0 * hh / n_heads).astype(np.float32)
    return jnp.asarray(np.repeat(s.reshape(n_heads // 2, 2, 1), 2 * ATT_BLOCK, axis=2))


def _band_bias(sl_ref, dil):
    u = lax.broadcasted_iota(jnp.int32, (ATT_BLOCK, 2 * ATT_BLOCK), 0)
    kk = lax.broadcasted_iota(jnp.int32, (ATT_BLOCK, 2 * ATT_BLOCK), 1)
    delta = u + ATT_BLOCK - kk
    valid = (delta >= 0) & (delta <= ATT_BLOCK)
    dist = (delta * dil).astype(F32)
    rows = [jnp.where(valid, -sl_ref[hd:hd + 1, :] * dist, NEG) for hd in range(2)]
    return jnp.concatenate(rows, axis=0)


def _stack_heads(a):
    lane = lax.broadcasted_iota(jnp.int32, a.shape, 1)
    return jnp.concatenate([jnp.where(lane < HEAD_DIM, a, 0.0), jnp.where(lane >= HEAD_DIM, a, 0.0)], axis=0).astype(BF16)


def _unstack_heads(a2):
    top, bot = a2[:ATT_BLOCK], a2[ATT_BLOCK:]
    lane = lax.broadcasted_iota(jnp.int32, top.shape, 1)
    return jnp.where(lane < HEAD_DIM, top, bot)


def _stack_cols(a):
    return jnp.concatenate([a[:, 0:1], a[:, HEAD_DIM:HEAD_DIM + 1]], axis=0)


def _fill_bias(sl_ref, bias_ref):
    kk = lax.broadcasted_iota(jnp.int32, (2 * ATT_BLOCK, 2 * ATT_BLOCK), 1)
    for gi, dil in enumerate(DILATIONS):
        bias = _band_bias(sl_ref, dil)
        bias_ref[2 * gi] = bias
        bias_ref[2 * gi + 1] = jnp.where(kk < ATT_BLOCK, NEG, bias)


def _attention_fwd(q, kv, slopes, name):
    hp, t, _ = q.shape
    ns = t // SUPER
    nd = len(DILATIONS)

    def body(sl_ref, q_ref, kc_ref, kp_ref, vc_ref, vp_ref, o_ref, lse_ref, kw_ref, vw_ref, og_ref, lg_ref, bias_ref):
        n = pl.program_id(1)
        kw_ref[0:SUPER, :] = kp_ref[...]
        kw_ref[SUPER:, :] = kc_ref[...]
        vw_ref[0:SUPER, :] = vp_ref[...]
        vw_ref[SUPER:, :] = vc_ref[...]

        @pl.when(n == 0)
        def _():
            _fill_bias(sl_ref, bias_ref)

        for gi, dil in enumerate(DILATIONS):

            def block(idx, carry, gi=gi, dil=dil):
                r, b = idx % dil, idx // dil
                qs = b * (ATT_BLOCK * dil) + r
                ks = SUPER + (b - 1) * (ATT_BLOCK * dil) + r
                first = jnp.logical_and(n == 0, b == 0).astype(jnp.int32)
                q2 = _stack_heads(q_ref[pl.ds(qs, ATT_BLOCK, stride=dil), :])
                kb = kw_ref[pl.ds(ks, 2 * ATT_BLOCK, stride=dil), :].astype(BF16)
                vb = vw_ref[pl.ds(ks, 2 * ATT_BLOCK, stride=dil), :].astype(BF16)
                s = _dot_nt(q2, kb) + bias_ref[2 * gi + first]
                m = jnp.max(s, axis=-1, keepdims=True)
                p = jnp.exp(s - m)
                l = jnp.sum(p, axis=-1, keepdims=True)
                o2 = _dot(p.astype(BF16), vb) / l
                lse2 = jnp.broadcast_to(m + jnp.log(l), (2 * ATT_BLOCK, LANES))
                og_ref[gi, pl.ds(qs, ATT_BLOCK, stride=dil), :] = _unstack_heads(o2)
                lg_ref[gi, pl.ds(qs, ATT_BLOCK, stride=dil), :] = _unstack_heads(lse2)
                return carry

            lax.fori_loop(0, SUPER // ATT_BLOCK, block, 0, unroll=ATT_UNROLL_FWD)
        lg =[lg_ref[gi] for gi in range(nd)]
        top = functools.reduce(jnp.maximum, lg)
        ws = [jnp.exp(x - top) for x in lg]
        tot = functools.reduce(jnp.add, ws)
        lse_ref[...] = top + jnp.log(tot)
        acc = ws[0] * og_ref[0]
        for gi in range(1, nd):
            acc = acc + ws[gi] * og_ref[gi]
        o_ref[...] = (acc / tot).astype(BF16)

    cur = lambda off: pl.BlockSpec((None, SUPER, LANES), lambda h, n: (h + off, n, 0))
    prev = lambda off: pl.BlockSpec((None, SUPER, LANES), lambda h, n: (h + off, jnp.maximum(n - 1, 0), 0))
    return _call(
        body, name=name, grid=(hp, ns),
        in_specs=[pl.BlockSpec((None, 2, 2 * ATT_BLOCK), lambda h, n: (h, 0, 0)), cur(0), cur(0), prev(0), cur(hp), prev(hp)],
        out_specs=[pl.BlockSpec((SUPER, LANES), lambda h, n: (n, h)), cur(0)],
        out_shape=[_sds((t, hp * LANES), BF16), _sds((hp, t, LANES), F32)],
        scratch_shapes=[pltpu.VMEM((2 * SUPER, LANES), F32), pltpu.VMEM((2 * SUPER, LANES), F32),
                        pltpu.VMEM((nd, SUPER, LANES), F32), pltpu.VMEM((nd, SUPER, LANES), F32),
                        pltpu.VMEM((2 * nd, 2 * ATT_BLOCK, 2 * ATT_BLOCK), F32)],
    )(slopes, q, kv, kv, kv, kv)


def _attention_bwd(q, kv, o, lse, d_o, slopes, q_scale, name):
    hp, t, _ = q.shape
    ns = t // SUPER

    def body(sl_ref, q_ref, kc_ref, kp_ref, vc_ref, vp_ref, o_ref, lse_ref, do_ref,
             dq_ref, dk_ref, dv_ref, kw_ref, vw_ref, dkw_ref, dvw_ref, dd_ref, bias_ref):
        n = pl.program_id(1)

        @pl.when(n == 0)
        def _():
            dkw_ref[...] = jnp.zeros_like(dkw_ref)
            dvw_ref[...] = jnp.zeros_like(dvw_ref)

        @pl.when(n > 0)
        def _():
            dkw_ref[0:SUPER, :] = dkw_ref[SUPER:, :]
            dvw_ref[0:SUPER, :] = dvw_ref[SUPER:, :]
            dkw_ref[SUPER:, :] = jnp.zeros((SUPER, LANES), F32)
            dvw_ref[SUPER:, :] = jnp.zeros((SUPER, LANES), F32)

        @pl.when(n < ns)
        def _():
            kw_ref[0:SUPER, :] = kp_ref[...]
            kw_ref[SUPER:, :] = kc_ref[...]
            vw_ref[0:SUPER, :] = vp_ref[...]
            vw_ref[SUPER:, :] = vc_ref[...]
            prod = do_ref[...] * o_ref[...].astype(F32)
            lane = lax.broadcasted_iota(jnp.int32, prod.shape, 1)
            d0 = jnp.sum(jnp.where(lane < HEAD_DIM, prod, 0.0), axis=-1, keepdims=True)
            d1 = jnp.sum(jnp.where(lane >= HEAD_DIM, prod, 0.0), axis=-1, keepdims=True)
            dd_ref[...] = jnp.where(lane < HEAD_DIM, d0, d1)
            dq_ref[...] = jnp.zeros_like(dq_ref)

            @pl.when(n == 0)
            def _():
                _fill_bias(sl_ref, bias_ref)

            for gi, dil in enumerate(DILATIONS):

                def block(idx, carry, gi=gi, dil=dil):
                    r, b = idx % dil, idx // dil
                    qs = b * (ATT_BLOCK * dil) + r
                    ks = SUPER + (b - 1) * (ATT_BLOCK * dil) + r
                    first = jnp.logical_and(n == 0, b == 0).astype(jnp.int32)
                    rows = pl.ds(qs, ATT_BLOCK, stride=dil)
                    keys = pl.ds(ks, 2 * ATT_BLOCK, stride=dil)
                    q2 = _stack_heads(q_ref[rows, :])
                    do2 = _stack_heads(do_ref[rows, :])
                    kb = kw_ref[keys, :].astype(BF16)
                    vb = vw_ref[keys, :].astype(BF16)
                    s = _dot_nt(q2, kb) + bias_ref[2 * gi + first]
                    p = jnp.exp(s - _stack_cols(lse_ref[rows, :]))
                    dp = _dot_nt(do2, vb)
                    ds = (p * (dp - _stack_cols(dd_ref[rows, :]))).astype(BF16)
                    dq_ref[rows, :] += _unstack_heads(_dot(ds, kb)) * q_scale
                    dkw_ref[keys, :] += _dot_tn(ds, q2)
                    dvw_ref[keys, :] += _dot_tn(p.astype(BF16), do2)
                    return carry

                lax.fori_loop(0, SUPER // ATT_BLOCK, block, 0, unroll=ATT_UNROLL_BWD)

        dk_ref[...] = dkw_ref[0:SUPER, :]
        dv_ref[...] = dvw_ref[0:SUPER, :]

    last = ns - 1
    cur = lambda off: pl.BlockSpec((None, SUPER, LANES), lambda h, n: (h + off, jnp.minimum(n, last), 0))
    prev = lambda off: pl.BlockSpec((None, SUPER, LANES), lambda h, n: (h + off, jnp.clip(n - 1, 0, last), 0))
    nat = pl.BlockSpec((SUPER, LANES), lambda h, n: (jnp.minimum(n, last), h))
    late = pl.BlockSpec((None, SUPER, LANES), lambda h, n: (h, jnp.maximum(n - 1, 0), 0))
    dq, dk, dv = _call(
        body, name=name, grid=(hp, ns + 1),
        in_specs=[pl.BlockSpec((None, 2, 2 * ATT_BLOCK), lambda h, n: (h, 0, 0)), cur(0), cur(0), prev(0), cur(hp), prev(hp),
                  nat, cur(0), nat],
        out_specs=[cur(0), late, late],
        out_shape=[_sds((hp, t, LANES), F32)] * 3,
        scratch_shapes=[pltpu.VMEM((2 * SUPER, LANES), F32)] * 4 + [
            pltpu.VMEM((SUPER, LANES), F32), pltpu.VMEM((2 * len(DILATIONS), 2 * ATT_BLOCK, 2 * ATT_BLOCK), F32)],
    )(slopes, q, kv, kv, kv, kv, o, lse, d_o)
    return dq, dk, dv


def _loss_head(y, target, raw, g, name):
    t, d = y.shape
    tm = ROW_TILE

    def body(y_ref, t_ref, raw_ref, g_ref, sq_ref, dy_ref, draw_ref, dg_ref):
        i = pl.program_id(0)
        err = y_ref[...] - t_ref[...]
        dy = err * (1.0 / d)
        dy_ref[...] = dy
        draw, dg = _rms_bwd(raw_ref[...], g_ref[...], dy)
        draw_ref[...] = draw.astype(BF16)
        sq = jnp.zeros((8, LANES), F32) + jnp.sum(err * err)

        @pl.when(i == 0)
        def _():
            sq_ref[...] = sq
            dg_ref[...] = dg

        @pl.when(i > 0)
        def _():
            sq_ref[...] += sq
            dg_ref[...] += dg

    row = pl.BlockSpec((tm, d), lambda i: (i, 0))
    vec = pl.BlockSpec((1, d), lambda i: (0, 0))
    return _call(
        body, name=name, grid=(t // tm,), in_specs=[row, row, row, vec],
        out_specs=[pl.BlockSpec((8, LANES), lambda i: (0, 0)), row, row, vec],
        out_shape=[_sds((8, LANES), F32), _sds((t, d), F32), _sds((t, d), BF16), _sds((1, d), F32)])(y, target, raw, g)


def _bwd_matmul_norms(a_specs, a_args, a_tile, n_steps, w_spec, w_arg, w_mat, xa, ga, resid, xb, gb, name):
    t, d = xa.shape
    tm = BWD_ROW_TILE
    na = len(a_specs)
    second = xb is not None
    per = 2 if n_steps % 2 == 0 else 1
    n_steps //= per

    def blocks_of(spec, k):
        return pl.BlockSpec(spec.block_shape, lambda i, j: spec.index_map(i, per * j + k))

    def body(*refs):
        a_refs, w_refs = refs[:per * na], refs[per * na:per * na + per]
        xa_ref, ga_ref, res_ref = refs[per * na + per:per * na + per + 3]
        rest = refs[per * na + per + 3:]
        if second:
            xb_ref, gb_ref, dx_ref, d2_ref, dga_ref, dgb_ref, acc_ref = rest
        else:
            dx_ref, dga_ref, acc_ref = rest
        i, j = pl.program_id(0), pl.program_id(1)
        part = None
        for k in range(per):
            term = _dot_nt(a_tile(per * j + k, *a_refs[k * na:(k + 1) * na]), w_mat(w_refs[k]))
            part = term if part is None else part + term

        @pl.when(j == 0)
        def _():
            acc_ref[...] = part

        @pl.when(j > 0)
        def _():
            acc_ref[...] += part

        @pl.when(j == n_steps - 1)
        def _():
            da, dga = _rms_bwd(xa_ref[...], ga_ref[...], acc_ref[...])
            dx = res_ref[...] + da
            dx_ref[...] = dx
            if second:
                d2, dgb = _rms_bwd(xb_ref[...], gb_ref[...], dx)
                d2_ref[...] = d2.astype(BF16)

            @pl.when(i == 0)
            def _():
                dga_ref[...] = dga
                if second:
                    dgb_ref[...] = dgb

            @pl.when(i > 0)
            def _():
                dga_ref[...] += dga
                if second:
                    dgb_ref[...] += dgb

    row = pl.BlockSpec((tm, d), lambda i, j: (i, 0))
    vec = pl.BlockSpec((1, d), lambda i, j: (0, 0))
    in_specs = [blocks_of(sp, k) for k in range(per) for sp in a_specs] + [blocks_of(w_spec, k) for k in range(per)]
    in_specs += [row, vec, row]
    args = list(a_args) * per + [w_arg] * per + [xa, ga, resid]
    if second:
        in_specs += [row, vec]
        args += [xb, gb]
        out_specs = [row, row, vec, vec]
        out_shape = [_sds((t, d), F32), _sds((t, d), BF16), _sds((1, d), F32), _sds((1, d), F32)]
    else:
        out_specs = [row, vec]
        out_shape = [_sds((t, d), F32), _sds((1, d), F32)]
    return _call(body, name=name, grid=(t // tm, n_steps), in_specs=in_specs, out_specs=out_specs,
                 out_shape=out_shape, scratch_shapes=[pltpu.VMEM((tm, d), F32)])(*args)


def _heads_to_rows(*refs):
    hp = refs[0].shape[0]
    cols = []
    for p in range(hp):
        v = refs[0][p]
        for r in refs[1:]:
            v = v + r[p]
        cols.append(v)
    return jnp.concatenate(cols, axis=-1).astype(BF16)


def _matmul_nt_rows(a, wg, layer, out_dtype, name):
    t, d = a.shape
    tm = ROW_TILE

    def body(a_ref, w_ref, o_ref):
        o_ref[...] = _dot_nt(a_ref[...], w_ref[...].reshape(d, d)).astype(out_dtype)

    row = pl.BlockSpec((tm, d), lambda i: (i, 0))
    return _call(body, name=name, grid=(t // tm,),
                 in_specs=[row, pl.BlockSpec((N_DEV, None, d // N_DEV, d), lambda i: (0, layer, 0, 0))],
                 out_specs=row, out_shape=_sds((t, d), out_dtype))(a, wg)


def _swiglu_bwd(d_ff, wg, layer, gate, up, name):
    t, d = d_ff.shape
    fc = gate.shape[-1]
    rows = wg.shape[2]
    tm = BIG_ROW_TILE

    def body(df_ref, w_ref, g_ref, u_ref, dh_ref):
        da = _dot_nt(df_ref[...], w_ref[...].reshape(2 * rows, d))
        gate, up = g_ref[...].astype(F32), u_ref[...].astype(F32)
        sig = jax.nn.sigmoid(gate)
        dh_ref[0] = (da * up * (sig * (1.0 + gate * (1.0 - sig)))).astype(BF16)
        dh_ref[1] = (da * (gate * sig)).astype(BF16)

    return _call(
        body, name=name, grid=(t // tm, 4),
        in_specs=[pl.BlockSpec((tm, d), lambda i, c: (i, 0)),
                  pl.BlockSpec((2, None, rows, d), lambda i, c: (c, layer, 0, 0)),
                  pl.BlockSpec((None, tm, fc), lambda i, c: (c, i, 0)),
                  pl.BlockSpec((None, tm, fc), lambda i, c: (c, i, 0))],
        out_specs=pl.BlockSpec((None, 2, tm, fc), lambda i, c: (c, 0, i, 0)),
        out_shape=_sds((4, 2, t, fc), BF16))(d_ff, wg, gate, up)


def _conv_bwd(p, d_z, cw, name):
    t, d3 = p.shape
    d = d3 // 3
    tm = ROW_TILE
    hb = tm // HALO
    nt = t // tm

    def body(p_ref, prev_ref, next_ref, dz_ref, dzn_ref, cw_ref, dp_ref, dcw_ref):
        i = pl.program_id(0)
        b = p_ref[:, 0:d].astype(F32)
        c = p_ref[:, d:2 * d].astype(F32)
        h = p_ref[:, 2 * d:3 * d].astype(F32)
        u = c * h
        hu = prev_ref[:, d:2 * d].astype(F32) * prev_ref[:, 2 * d:3 * d].astype(F32) * (i > 0).astype(F32)
        u1, u2 = _shift_down(u, hu, 1, tm), _shift_down(u, hu, 2, tm)
        uc = cw_ref[2:3, :] * u + cw_ref[1:2, :] * u1 + cw_ref[0:1, :] * u2
        dz = dz_ref[...].astype(F32)
        duc = dz * b
        dn = dzn_ref[...].astype(F32) * next_ref[:, 0:d].astype(F32) * (i < nt - 1).astype(F32)
        du = cw_ref[2:3, :] * duc + cw_ref[1:2, :] * _shift_up(duc, dn, 1, tm) + cw_ref[0:1, :] * _shift_up(duc, dn, 2, tm)
        dp_ref[:, 0:d] = (dz * uc).astype(BF16)
        dp_ref[:, d:2 * d] = (du * h).astype(BF16)
        dp_ref[:, 2 * d:3 * d] = (du * c).astype(BF16)
        dcw = jnp.concatenate([jnp.sum(duc * u2, axis=0, keepdims=True), jnp.sum(duc * u1, axis=0, keepdims=True),
                               jnp.sum(duc * u, axis=0, keepdims=True)], axis=0)

        @pl.when(i == 0)
        def _():
            dcw_ref[...] = dcw

        @pl.when(i > 0)
        def _():
            dcw_ref[...] += dcw

    last_halo = t // HALO - 1
    return _call(
        body, name=name, grid=(nt,),
        in_specs=[pl.BlockSpec((tm, d3), lambda i: (i, 0)),
                  pl.BlockSpec((HALO, d3), lambda i: (jnp.maximum(i * hb - 1, 0), 0)),
                  pl.BlockSpec((HALO, d3), lambda i: (jnp.minimum((i + 1) * hb, last_halo), 0)),
                  pl.BlockSpec((tm, d), lambda i: (i, 0)),
                  pl.BlockSpec((HALO, d), lambda i: (jnp.minimum((i + 1) * hb, last_halo), 0)),
                  pl.BlockSpec((3, d), lambda i: (0, 0))],
        out_specs=[pl.BlockSpec((tm, d3), lambda i: (i, 0)), pl.BlockSpec((3, d), lambda i: (0, 0))],
        out_shape=[_sds((t, d3), BF16), _sds((3, d), F32)])(p, p, p, d_z, d_z, cw)


def _grad_weight(a_specs, a_args, a_tile, b_specs, b_args, b_tile, n_out, acc_shape, out_spec, out_shape, t, name):
    tt = BIG_ROW_TILE
    na, nb = len(a_specs), len(b_specs)

    def body(*refs):
        a_refs, b_refs = refs[:na], refs[na:na + nb]
        o_ref, acc_ref = refs[na + nb:]
        s = pl.program_id(1)
        part = _dot_tn(a_tile(pl.program_id(0), *a_refs), b_tile(pl.program_id(0), *b_refs))

        @pl.when(s == 0)
        def _():
            acc_ref[...] = part

        @pl.when(s > 0)
        def _():
            acc_ref[...] += part

        @pl.when(s == t // tt - 1)
        def _():
            o_ref[...] = acc_ref[...].astype(BF16).reshape(o_ref.shape)

    return _call(body, name=name, grid=(n_out, t // tt), in_specs=list(a_specs) + list(b_specs), out_specs=out_spec,
                 out_shape=out_shape, scratch_shapes=[pltpu.VMEM(acc_shape, F32)])(*a_args, *b_args)


def _ident(*args):
    return args[-1][...]


def _heads_tile(j, *refs):
    return _heads_to_rows(*refs)


def kernel(x, norm_g, conv_in_w, conv_w, conv_out_w, kv_norm_g, kv_w, q_w, o_w, ffn_in_w, ffn_out_w, loss_target, m_norm_g, m_conv_in_w, m_conv_w, m_conv_out_w, m_kv_norm_g, m_kv_w, m_q_w, m_o_w, m_ffn_in_w, m_ffn_out_w, v_norm_g, v_conv_in_w, v_conv_w, v_conv_out_w, v_kv_norm_g, v_kv_w, v_q_w, v_o_w, v_ffn_in_w, v_ffn_out_w):
    x0 = x[0]
    target = loss_target[0]
    t, d = x0.shape
    depth = norm_g.shape[0]
    n_a = conv_in_w.shape[0]
    n_b = q_w.shape[0]
    hp = d // LANES
    tm, tg = BWD_ROW_TILE, BIG_ROW_TILE
    assert t % SUPER == 0 and d % LANES == 0 and depth == n_a + n_b
    dev = 4 * lax.axis_index("x") + 2 * lax.axis_index("y") + lax.axis_index("c")

    n_small = 4 * depth + 3 * n_a
    small_rows = -(-(n_small + 1) // 8) * 8
    small_local = jnp.concatenate([norm_g.reshape(4 * depth, -1), conv_w.reshape(3 * n_a, -1),
                                   jnp.zeros((small_rows - n_small, norm_g.shape[-1]), F32)], axis=0)
    big = {"conv_in_w": conv_in_w, "conv_out_w": conv_out_w, "kv_w": kv_w[None], "q_w": q_w, "o_w": o_w,
           "ffn_in_w": ffn_in_w, "ffn_out_w": ffn_out_w}
    names = list(big)
    gathered = _all_gather([small_local] + [_cast_bf16(big[k], "cast_" + k) for k in names], "gather_weights")
    small_all = gathered[0].transpose(1, 0, 2).reshape(small_rows, d)
    wg = dict(zip(names, gathered[1:]))
    gain = lambda layer, k: small_all[4 * layer + k][None]
    taps = lambda layer: small_all[4 * depth + 3 * layer: 4 * depth + 3 * layer + 3]
    g_kv = kv_norm_g[None]
    slopes = _alibi_slopes(d // HEAD_DIM)
    fc = wg["ffn_in_w"].shape[-1]
    cb = wg["conv_in_w"].shape[-1]
    kvb = wg["kv_w"].shape[-1]
    q_scale = HEAD_DIM ** -0.5

    saved = []
    kv = kvn = None
    xs = x0
    for layer in range(depth):
        tag = f"_l{layer}"
        s = {"x_in": xs}
        if layer < n_a:
            s["p"], s["xn"] = _norm_matmul_cols(xs, gain(layer, 0), wg["conv_in_w"], layer, "cols", "conv_in" + tag)
            s["z"] = _conv_fwd(s["p"], taps(layer), "conv" + tag)
            s["mix"], x_mid = _matmul_norm_residual(s["z"][None], wg["conv_out_w"], layer, gain(layer, 1), xs, "conv_out" + tag)
        else:
            j = layer - n_a
            if kv is None:
                kv, kvn = _norm_matmul_cols(xs, g_kv, wg["kv_w"], 0, "heads", "kv_proj")
            s["q"], s["xn"] = _norm_matmul_heads(xs, gain(layer, 0), wg["q_w"], j, q_scale, "q_proj" + tag)
            s["o"], s["lse"] = _attention_fwd(s["q"], kv, slopes, "attention" + tag)
            s["mix"], x_mid = _matmul_norm_residual(s["o"][None], wg["o_w"], j, gain(layer, 1), xs, "o_proj" + tag)
        s["x_mid"] = x_mid
        s["gate"], s["up"], s["a"], s["fn"] = _ffn_in_swiglu(x_mid, gain(layer, 2), wg["ffn_in_w"], layer, "ffn_in" + tag)
        s["ff"], xs = _matmul_norm_residual(s["a"], wg["ffn_out_w"], layer, gain(layer, 3), x_mid, "ffn_out" + tag)
        saved.append(s)

    last = saved[-1]
    sq, dx_out, d_ff, dg3 = _loss_head(xs, target, last["ff"], gain(depth - 1, 3), "loss_head")
    loss = lax.psum(sq[0, 0] * (0.5 / d), ("x", "y", "c"))

    dgain = {(depth - 1, 3): dg3}
    dtaps = {}
    grads = {k: [None] * big[k].shape[0] for k in names}
    row_i = lambda i, j: (i, 0)
    dkv_parts = []
    for layer in reversed(range(depth)):
        tag = f"_l{layer}"
        s = saved[layer]
        dh = _swiglu_bwd(d_ff, wg["ffn_out_w"], layer, s["gate"], s["up"], "swiglu_bwd" + tag)
        rows_out = wg["ffn_out_w"].shape[2]
        grads["ffn_out_w"][layer] = _grad_weight(
            [pl.BlockSpec((None, tg, fc), lambda c, i: (c, i, 0))], [s["a"]], _ident,
            [pl.BlockSpec((tg, d), lambda c, i: (i, 0))], [d_ff], _ident,
            4, (fc, d), pl.BlockSpec((2, rows_out, d), lambda c, i: (c, 0, 0)), _sds((N_DEV, rows_out, d), BF16), t,
            "grad_ffn_out" + tag)
        grads["ffn_in_w"][layer] = _grad_weight(
            [pl.BlockSpec((tg, d), lambda j, i: (i, 0))], [s["fn"]], _ident,
            [pl.BlockSpec((None, None, tg, fc), lambda j, i: (j % 4, j // 4, i, 0))], [dh], _ident,
            N_DEV, (d, fc), pl.BlockSpec((None, d, fc), lambda j, i: (j, 0, 0)), _sds((N_DEV, d, fc), BF16), t,
            "grad_ffn_in" + tag)
        dx_mid, d_mix, dg2, dg1 = _bwd_matmul_norms(
            [pl.BlockSpec((None, None, tm, fc), lambda i, j: (j % 4, j // 4, i, 0))], [dh], _ident, N_DEV,
            pl.BlockSpec((None, None, d, fc), lambda i, j: (j, layer, 0, 0)), wg["ffn_in_w"], _ident,
            s["x_mid"], gain(layer, 2), dx_out, s["mix"], gain(layer, 1), "ffn_in_bwd" + tag)
        dgain[(layer, 2)], dgain[(layer, 1)] = dg2, dg1
        full_rows = pl.BlockSpec((N_DEV, d // N_DEV, d), lambda j, i: (0, 0, 0))
        rows_w = lambda wname, idx: (pl.BlockSpec((N_DEV, None, d // N_DEV, d), lambda i, j: (0, idx, 0, 0)), wg[wname],
                                     lambda w_ref: w_ref[...].reshape(d, d))
        if layer < n_a:
            d_z = _matmul_nt_rows(d_mix, wg["conv_out_w"], layer, BF16, "conv_out_bwd" + tag)
            grads["conv_out_w"][layer] = _grad_weight(
                [pl.BlockSpec((tg, d), lambda j, i: (i, 0))], [s["z"]], _ident,
                [pl.BlockSpec((tg, d), lambda j, i: (i, 0))], [d_mix], _ident,
                1, (d, d), full_rows, _sds((N_DEV, d // N_DEV, d), BF16), t, "grad_conv_out" + tag)
            d_p, dtaps[layer] = _conv_bwd(s["p"], d_z, taps(layer), "conv_bwd" + tag)
            grads["conv_in_w"][layer] = _grad_weight(
                [pl.BlockSpec((tg, d), lambda j, i: (i, 0))], [s["xn"]], _ident,
                [pl.BlockSpec((tg, cb), lambda j, i: (i, j))], [d_p], _ident,
                N_DEV, (d, cb), pl.BlockSpec((None, d, cb), lambda j, i: (j, 0, 0)), _sds((N_DEV, d, cb), BF16), t,
                "grad_conv_in" + tag)
            a_specs, a_args, a_tile, n_steps = [pl.BlockSpec((tm, cb), lambda i, j: (i, j))], [d_p], _ident, N_DEV
            w_spec = pl.BlockSpec((None, None, d, cb), lambda i, j: (j, layer, 0, 0))
            w_arg, w_mat = wg["conv_in_w"], _ident
            resid = dx_mid
        else:
            j_b = layer - n_a
            d_o = _matmul_nt_rows(d_mix, wg["o_w"], j_b, F32, "o_proj_bwd" + tag)
            grads["o_w"][j_b] = _grad_weight(
                [pl.BlockSpec((tg, d), lambda j, i: (i, 0))], [s["o"]], _ident,
                [pl.BlockSpec((tg, d), lambda j, i: (i, 0))], [d_mix], _ident,
                1, (d, d), full_rows, _sds((N_DEV, d // N_DEV, d), BF16), t, "grad_o" + tag)
            dq, dk, dv = _attention_bwd(s["q"], kv, s["o"], s["lse"], d_o, slopes, q_scale, "attention_bwd" + tag)
            dkv_parts.append((dk, dv))
            heads_spec = pl.BlockSpec((hp, tg, LANES), lambda j, i: (0, i, 0))
            grads["q_w"][j_b] = _grad_weight(
                [pl.BlockSpec((tg, d), lambda j, i: (i, 0))], [s["xn"]], _ident,
                [heads_spec], [dq], _heads_tile,
                1, (d, d), full_rows, _sds((N_DEV, d // N_DEV, d), BF16), t, "grad_q" + tag)
            a_specs, a_args, a_tile, n_steps = [pl.BlockSpec((hp, tm, LANES), lambda i, j: (0, i, 0))], [dq], _heads_tile, 1
            w_spec, w_arg, w_mat = rows_w("q_w", j_b)
            resid = dx_mid
            if layer == n_a:
                pieces = kvb // LANES
                halves = []
                for src in (0, 1):
                    halves.append([part[src] for part in dkv_parts])
                n_half = len(dkv_parts)
                kv_args = [arr for src in (0, 1) for arr in halves[src]]

                def kv_block(src, j):
                    return jnp.where((j // 4) == src, j % 4, 0)

                def kv_tile(j, *refs):
                    keys = _heads_to_rows(*refs[:n_half])
                    vals = _heads_to_rows(*refs[n_half:])
                    return jnp.where(j < 4, keys, vals)

                kv_specs = [pl.BlockSpec((pieces, tm, LANES), functools.partial(lambda i, j, src: (kv_block(src, j), i, 0), src=src))
                            for src in (0, 1) for _ in range(n_half)]
                resid, dgain["kv"] = _bwd_matmul_norms(
                    kv_specs, kv_args, kv_tile, N_DEV,
                    pl.BlockSpec((None, None, d, kvb), lambda i, j: (j, 0, 0, 0)), wg["kv_w"], _ident,
                    s["x_in"], g_kv, dx_mid, None, None, "kv_proj_bwd")
                kv_b_specs = [pl.BlockSpec((pieces, tg, LANES), functools.partial(lambda j, i, src: (kv_block(src, j), i, 0), src=src))
                              for src in (0, 1) for _ in range(n_half)]
                grads["kv_w"][0] = _grad_weight(
                    [pl.BlockSpec((tg, d), lambda j, i: (i, 0))], [kvn], _ident,
                    kv_b_specs, kv_args, kv_tile,
                    N_DEV, (d, kvb), pl.BlockSpec((None, d, kvb), lambda j, i: (j, 0, 0)), _sds((N_DEV, d, kvb), BF16), t,
                    "grad_kv")
        if layer > 0:
            prev = saved[layer - 1]
            dx_out, d_ff, dg0, dg3p = _bwd_matmul_norms(
                a_specs, a_args, a_tile, n_steps, w_spec, w_arg, w_mat,
                s["x_in"], gain(layer, 0), resid, prev["ff"], gain(layer - 1, 3), "mixer_in_bwd" + tag)
            dgain[(layer, 0)], dgain[(layer - 1, 3)] = dg0, dg3p
        else:
            grad_x, dg0 = _bwd_matmul_norms(
                a_specs, a_args, a_tile, n_steps, w_spec, w_arg, w_mat,
                s["x_in"], gain(layer, 0), resid, None, None, "mixer_in_bwd" + tag)
            dgain[(layer, 0)] = dg0

    small_grad = jnp.concatenate(
        [dgain[(layer, k)] for layer in range(depth) for k in range(4)] + [dtaps[layer] for layer in range(n_a)]
        + [dgain["kv"]] + [jnp.zeros((small_rows - n_small - 1, d), F32)], axis=0)
    small_grads_all = _all_gather([small_grad], "gather_small_grads")[0]
    lo = dev * (d // N_DEV)

    def pack(ng, cwp, kvg):
        rows = jnp.concatenate([ng.reshape(4 * depth, -1), cwp.reshape(3 * n_a, -1)], axis=0)
        z = lax.dynamic_update_slice(jnp.zeros((small_rows, d), F32), rows, (0, lo))
        return lax.dynamic_update_slice(z, kvg[None], (n_small, 0))

    w_small = lax.dynamic_update_slice(small_all, g_kv, (n_small, 0))
    m_small, v_small = pack(m_norm_g, m_conv_w, m_kv_norm_g), pack(v_norm_g, v_conv_w, v_kv_norm_g)
    sm = _small_adamw(small_grads_all, w_small, m_small, v_small, "adamw_small")

    def unpack(a):
        mine = lax.dynamic_slice(a, (0, lo), (small_rows, d // N_DEV))
        return (mine[:4 * depth].reshape(norm_g.shape), mine[4 * depth:n_small].reshape(conv_w.shape), a[n_small])

    small_out = [unpack(a) for a in sm]

    order = [(k, i) for k in names for i in range(big[k].shape[0])]
    partials = [grads[k][i] for k, i in order]
    from_sibling = _exchange_sibling(partials, "reduce_sibling")
    chip_parts = [_add_pairs(g, s_, f"pair_sum_{k}_{i}") for (k, i), g, s_ in zip(order, partials, from_sibling)]
    from_chips = _exchange_chips(chip_parts, "reduce_chips")
    moments = {"conv_in_w": (m_conv_in_w, v_conv_in_w), "conv_out_w": (m_conv_out_w, v_conv_out_w),
               "kv_w": (m_kv_w[None], v_kv_w[None]), "q_w": (m_q_w, v_q_w), "o_w": (m_o_w, v_o_w),
               "ffn_in_w": (m_ffn_in_w, v_ffn_in_w), "ffn_out_w": (m_ffn_out_w, v_ffn_out_w)}
    res = {k: [None] * big[k].shape[0] for k in names}
    for (k, i), part, got in zip(order, chip_parts, from_chips):
        res[k][i] = _reduce_adamw(part, got, big[k][i], moments[k][0][i], moments[k][1][i], f"adamw_{k}_{i}")

    def big_out(k, which):
        st = jnp.stack([res[k][i][which] for i in range(big[k].shape[0])], axis=0)
        return st[0] if k == "kv_w" else st

    out_names = ["norm_g", "conv_in_w", "conv_w", "conv_out_w", "kv_norm_g", "kv_w", "q_w", "o_w", "ffn_in_w", "ffn_out_w"]
    small_pos = {"norm_g": 0, "conv_w": 1, "kv_norm_g": 2}
    outs = [loss, grad_x[None]]
    for which in range(4):
        for k in out_names:
            outs.append(small_out[which][small_pos[k]] if k in small_pos else big_out(k, which))
    return tuple(outs)
```

```python
import functools
import math

import numpy as np
import jax
import jax.numpy as jnp
from jax import lax
from jax.experimental import pallas as pl
from jax.experimental.pallas import tpu as pltpu

F32 = jnp.float32
BF16 = jnp.bfloat16

N_DEV = 8
RMS_EPS = 1e-6
HEAD_DIM = 64
LANES = 128
ATT_BLOCK = 128
DILATIONS = (1, 4, 16)
SUPER = ATT_BLOCK * DILATIONS[-1]
NEG = -1e30
ATT_UNROLL_FWD = 8
ATT_UNROLL_BWD = 4

ADAM_LR, ADAM_B1, ADAM_B2, ADAM_EPS, ADAM_WD, ADAM_STEP = 0.001, 0.9, 0.999, 1e-08, 0.01, 10

ROW_TILE = 512
BIG_ROW_TILE = 1024
BWD_ROW_TILE = 512
MESH = pl.DeviceIdType.MESH


def _call(body, *, name, grid=None, in_specs=None, out_specs=None, out_shape=None, scratch_shapes=(), prefetch=False,
          **params):
    cp = pltpu.CompilerParams(**params) if params else None
    if prefetch:
        spec = pltpu.PrefetchScalarGridSpec(num_scalar_prefetch=1, grid=grid, in_specs=in_specs, out_specs=out_specs,
                                            scratch_shapes=list(scratch_shapes))
        return pl.pallas_call(body, name=name, grid_spec=spec, out_shape=out_shape, compiler_params=cp)
    kwargs = {k: v for k, v in (("grid", grid), ("in_specs", in_specs), ("out_specs", out_specs)) if v is not None}
    return pl.pallas_call(body, name=name, out_shape=out_shape, scratch_shapes=list(scratch_shapes),
                          compiler_params=cp, **kwargs)


def _sds(shape, dtype):
    return jax.ShapeDtypeStruct(tuple(shape), dtype)


def _rms(x, g):
    r = lax.rsqrt(jnp.mean(x * x, axis=-1, keepdims=True) + RMS_EPS)
    return x * r * g


def _rms_bwd(x, g, dy):
    r = lax.rsqrt(jnp.mean(x * x, axis=-1, keepdims=True) + RMS_EPS)
    xh = x * r
    dxh = dy * g
    dx = r * (dxh - xh * jnp.mean(dxh * xh, axis=-1, keepdims=True))
    return dx, jnp.sum(dy * xh, axis=0, keepdims=True)


def _dot(a, b):
    return jnp.dot(a, b, preferred_element_type=F32)


def _dot_nt(a, b):
    return lax.dot_general(a, b, (((1,), (1,)), ((), ())), preferred_element_type=F32)


def _dot_tn(a, b):
    return lax.dot_general(a, b, (((0,), (0,)), ((), ())), preferred_element_type=F32)


def _mesh_pos():
    return lax.axis_index("x"), lax.axis_index("y"), lax.axis_index("c")


def _all_gather(arrs, name):
    n = len(arrs)

    def body(*refs):
        ins, outs = refs[:n], refs[n:2 * n]
        send_sems, recv_sems, local_sems = refs[2 * n:]
        x, y, c = _mesh_pos()
        me, sibling = (x, y, c), (x, y, 1 - c)
        chips = [(1 - x, y), (x, 1 - y), (1 - x, 1 - y)]

        def copy(a, k, block, to, src=None):
            dst = outs[a].at[4 * block[0] + 2 * block[1] + block[2]]
            return pltpu.make_async_remote_copy(
                src_ref=dst if src is None else src, dst_ref=dst, send_sem=send_sems.at[a, k],
                recv_sem=recv_sems.at[a, k], device_id=to, device_id_type=MESH)

        started = []
        for a in range(n):
            mine = pltpu.make_async_copy(ins[a], outs[a].at[4 * x + 2 * y + c], local_sems.at[a])
            mine.start()
            started.append(mine)
        first = []
        for a in range(n):
            first.append(copy(a, 0, me, sibling, src=ins[a]))
            first += [copy(a, 1 + j, me, (*chip, c), src=ins[a]) for j, chip in enumerate(chips)]
        for cp in first:
            cp.start()
        passed = []
        for a in range(n):
            for j, chip in enumerate(chips):
                copy(a, 1 + j, (*chip, c), me).wait_recv()
                fwd = copy(a, 4 + j, (*chip, c), sibling)
                fwd.start()
                passed.append(fwd)
        for a in range(n):
            copy(a, 0, sibling, me).wait_recv()
            for j, chip in enumerate(chips):
                copy(a, 4 + j, (*chip, 1 - c), me).wait_recv()
        for cp in first + passed:
            cp.wait_send()
        for cp in started:
            cp.wait()

    any_spec = pl.BlockSpec(memory_space=pl.ANY)
    outs = _call(
        body, name=name, in_specs=[any_spec] * n, out_specs=[any_spec] * n,
        out_shape=[_sds((N_DEV,) + a.shape, a.dtype) for a in arrs],
        scratch_shapes=[pltpu.SemaphoreType.DMA((n, 7)), pltpu.SemaphoreType.DMA((n, 7)), pltpu.SemaphoreType.DMA((n,))],
        has_side_effects=True,
    )(*arrs)
    return list(outs)


def _exchange_sibling(grads, name):
    n = len(grads)

    def body(*refs):
        ins, outs = refs[:n], refs[n:2 * n]
        send_sems, recv_sems = refs[2 * n:]
        x, y, c = _mesh_pos()
        sibling = (x, y, 1 - c)
        copies = []
        for a in range(n):
            for q in range(4):
                cp = pltpu.make_async_remote_copy(
                    src_ref=ins[a].at[2 * q + (1 - c)], dst_ref=outs[a].at[q], send_sem=send_sems.at[a, q],
                    recv_sem=recv_sems.at[a, q], device_id=sibling, device_id_type=MESH)
                cp.start()
                copies.append(cp)
        for cp in copies:
            cp.wait_recv()
        for cp in copies:
            cp.wait_send()

    any_spec = pl.BlockSpec(memory_space=pl.ANY)
    outs = _call(
        body, name=name, in_specs=[any_spec] * n, out_specs=[any_spec] * n,
        out_shape=[_sds((4,) + g.shape[1:], g.dtype) for g in grads],
        scratch_shapes=[pltpu.SemaphoreType.DMA((n, 4)), pltpu.SemaphoreType.DMA((n, 4))],
        has_side_effects=True,
    )(*grads)
    return list(outs)


def _exchange_chips(parts, name):
    n = len(parts)

    def body(*refs):
        ins, outs = refs[:n], refs[n:2 * n]
        send_sems, recv_sems = refs[2 * n:]
        x, y, c = _mesh_pos()
        chips = [(1 - x, y), (x, 1 - y), (1 - x, 1 - y)]
        copies = []
        for a in range(n):
            for k, chip in enumerate(chips):
                cp = pltpu.make_async_remote_copy(
                    src_ref=ins[a].at[2 * chip[0] + chip[1]], dst_ref=outs[a].at[k], send_sem=send_sems.at[a, k],
                    recv_sem=recv_sems.at[a, k], device_id=(*chip, c), device_id_type=MESH)
                cp.start()
                copies.append(cp)
        for cp in copies:
            cp.wait_recv()
        for cp in copies:
            cp.wait_send()

    any_spec = pl.BlockSpec(memory_space=pl.ANY)
    outs = _call(
        body, name=name, in_specs=[any_spec] * n, out_specs=[any_spec] * n,
        out_shape=[_sds((3,) + p.shape[1:], p.dtype) for p in parts],
        scratch_shapes=[pltpu.SemaphoreType.DMA((n, 3)), pltpu.SemaphoreType.DMA((n, 3))],
        has_side_effects=True,
    )(*parts)
    return list(outs)


HBM_SPEC = pl.BlockSpec(memory_space=pltpu.HBM)
SEM_SPEC = pl.BlockSpec(memory_space=pltpu.SEMAPHORE)
DATAFLOW = pltpu.SideEffectType.DATAFLOW_SIDE_EFFECTING
PEERS = [(dx, dy, dc) for dx in (0, 1) for dy in (0, 1) for dc in (0, 1)][1:]


def _peer(flip):
    x, y, c = _mesh_pos()
    return tuple(1 - v if f else v for v, f in zip((x, y, c), flip))


def _slot(pos):
    return 4 * pos[0] + 2 * pos[1] + pos[2]


def _in_hbm(a):
    return pltpu.with_memory_space_constraint(a, pltpu.HBM)


def _direct_copies(srcs, lands, send_sems, recv_sems, scatter):
    me = _slot(_mesh_pos())
    copies = []
    for a in range(len(lands)):
        for k, flip in enumerate(PEERS):
            peer = _peer(flip)
            src = srcs[a].at[_slot(peer)] if scatter else lands[a].at[me]
            idx = a * len(PEERS) + k
            copies.append(pltpu.make_async_remote_copy(
                src_ref=src, dst_ref=lands[a].at[me], send_sem=send_sems.at[idx], recv_sem=recv_sems.at[idx],
                device_id=peer, device_id_type=MESH))
    return copies


def _send_start(srcs, lands, name):
    ns, nl = len(srcs), len(lands)
    scatter = ns > 0

    def body(*refs):
        src_refs, land_refs = refs[:ns], refs[ns:ns + nl]
        send_sems, recv_sems = refs[ns + nl:ns + nl + 2]
        token = refs[-1]
        for cp in _direct_copies(src_refs, land_refs, send_sems, recv_sems, scatter):
            cp.start()
        token[...] = jnp.zeros_like(token)

    sem = pltpu.SemaphoreType.DMA((nl * len(PEERS),))
    outs = pl.pallas_call(
        body, name=name,
        out_shape=(sem, sem) + tuple(pltpu.HBM(a.shape, a.dtype) for a in list(srcs) + list(lands))
        + (_sds((8, LANES), F32),),
        in_specs=[HBM_SPEC] * (ns + nl),
        out_specs=(SEM_SPEC, SEM_SPEC) + (HBM_SPEC,) * (ns + nl) + (pl.BlockSpec(memory_space=pltpu.VMEM),),
        input_output_aliases={i: 2 + i for i in range(ns + nl)},
        compiler_params=pltpu.CompilerParams(has_side_effects=DATAFLOW),
    )(*[_in_hbm(a) for a in list(srcs) + list(lands)])
    send_sems, recv_sems = outs[0], outs[1]
    return send_sems, recv_sems, list(outs[2:2 + ns]), list(outs[2 + ns:2 + ns + nl]), outs[-1]


def _send_wait(send_sems, recv_sems, srcs, lands, after, name):
    ns, nl = len(srcs), len(lands)
    scatter = ns > 0

    def body(*refs):
        src_refs, land_refs = refs[:ns], refs[ns:ns + nl]
        send_sems, recv_sems = refs[ns + nl:ns + nl + 2]
        copies = _direct_copies(src_refs, land_refs, send_sems, recv_sems, scatter)
        for cp in copies:
            cp.wait_send()
        for cp in copies:
            cp.wait_recv()

    outs = pl.pallas_call(
        body, name=name,
        out_shape=tuple(pltpu.HBM(a.shape, a.dtype) for a in list(srcs) + list(lands)),
        in_specs=[HBM_SPEC] * (ns + nl) + [SEM_SPEC, SEM_SPEC, pl.BlockSpec(memory_space=pl.ANY)],
        out_specs=(HBM_SPEC,) * (ns + nl),
        input_output_aliases={i: i for i in range(ns + nl)},
        compiler_params=pltpu.CompilerParams(has_side_effects=DATAFLOW),
    )(*srcs, *lands, send_sems, recv_sems, after)
    return list(outs[ns:])


def _place_own(grads, name):
    n = len(grads)

    def body(*refs):
        ins, outs, sems = refs[:n], refs[n:2 * n], refs[2 * n]
        me = _slot(_mesh_pos())
        copies = [pltpu.make_async_copy(ins[a].at[me], outs[a].at[me], sems.at[a]) for a in range(n)]
        for cp in copies:
            cp.start()
        for cp in copies:
            cp.wait()

    any_spec = pl.BlockSpec(memory_space=pl.ANY)
    return list(_call(body, name=name, in_specs=[any_spec] * n, out_specs=[any_spec] * n,
                      out_shape=[_sds(g.shape, g.dtype) for g in grads],
                      scratch_shapes=[pltpu.SemaphoreType.DMA((n,))])(*grads))


def _row_tile(rows, cap=512):
    t = min(rows, cap)
    while rows % t or (t % 16 and t != rows):
        t -= 1
    return t


def _as2d(a):
    return a.reshape(-1, a.shape[-1])


def _cast_bf16(w, name):
    w2 = _as2d(w)
    rows, cols = w2.shape
    tr = _row_tile(rows)

    def body(w_ref, o_ref):
        o_ref[...] = w_ref[...].astype(BF16)

    spec = pl.BlockSpec((tr, cols), lambda i: (i, 0))
    out = _call(body, name=name, grid=(rows // tr,), in_specs=[spec], out_specs=spec,
                out_shape=_sds(w2.shape, BF16))(w2)
    return out.reshape(w.shape)


def _cast_layer(w, layer, slot, name):
    _, rows, cols = w.shape
    tr = _row_tile(rows)

    def body(*refs):
        refs[-1][...] = refs[-2][...].astype(BF16)

    if slot is None:
        return _call(body, name=name, grid=(rows // tr,),
                     in_specs=[pl.BlockSpec((None, tr, cols), lambda i: (layer, i, 0))],
                     out_specs=pl.BlockSpec((tr, cols), lambda i: (i, 0)), out_shape=_sds((rows, cols), BF16))(w)
    return _call(body, name=name, grid=(rows // tr,), prefetch=True,
                 in_specs=[pl.BlockSpec((None, tr, cols), lambda i, s: (layer, i, 0))],
                 out_specs=pl.BlockSpec((None, tr, cols), lambda i, s: (s[0], i, 0)),
                 out_shape=_sds((N_DEV, rows, cols), BF16))(slot, w)


def _sum_adamw(land, w, m, v, layer, name):
    _, rows, cols = land.shape
    tr = _row_tile(rows, 256)

    def body(l_ref, w_ref, m_ref, v_ref, g_ref, d_ref, nm_ref, nv_ref):
        g = l_ref[0].astype(F32)
        for k in range(1, N_DEV):
            g = g + l_ref[k].astype(F32)
        delta, nm, nv = _adamw_math(w_ref[...], g, m_ref[...], v_ref[...])
        g_ref[...] = g
        d_ref[...] = delta
        nm_ref[...] = nm
        nv_ref[...] = nv

    lay = pl.BlockSpec((None, tr, cols), lambda i: (layer, i, 0))
    out = pl.BlockSpec((tr, cols), lambda i: (i, 0))
    return _call(body, name=name, grid=(rows // tr,),
                 in_specs=[pl.BlockSpec((N_DEV, tr, cols), lambda i: (0, i, 0)), lay, lay, lay],
                 out_specs=[out] * 4, out_shape=[_sds((rows, cols), F32)] * 4)(land, w, m, v)


def _add_pairs(grad, from_sibling, name):
    blk = grad.shape[1:]
    g2 = grad.reshape(4, 2, -1, blk[-1])
    s2 = from_sibling.reshape(4, -1, blk[-1])
    rows, cols = s2.shape[1:]
    tr = _row_tile(rows)
    core = lax.axis_index("c").astype(jnp.int32).reshape(1)

    def body(c_ref, g_ref, s_ref, o_ref):
        o_ref[...] = (g_ref[...].astype(F32) + s_ref[...].astype(F32)).astype(BF16)

    out = _call(
        body, name=name, grid=(4, rows // tr), prefetch=True,
        in_specs=[pl.BlockSpec((None, None, tr, cols), lambda q, i, c_ref: (q, c_ref[0], i, 0)),
                  pl.BlockSpec((None, tr, cols), lambda q, i, c_ref: (q, i, 0))],
        out_specs=pl.BlockSpec((None, tr, cols), lambda q, i, c_ref: (q, i, 0)),
        out_shape=_sds(s2.shape, BF16))(core, g2, s2)
    return out.reshape((4,) + blk)


def _adamw_math(w, g, m, v):
    m = ADAM_B1 * m + (1.0 - ADAM_B1) * g
    v = ADAM_B2 * v + (1.0 - ADAM_B2) * (g * g)
    m_hat = m / (1.0 - ADAM_B1 ** ADAM_STEP)
    v_hat = v / (1.0 - ADAM_B2 ** ADAM_STEP)
    delta = -ADAM_LR * (m_hat / (jnp.sqrt(v_hat) + ADAM_EPS) + ADAM_WD * w)
    return delta, m, v


def _reduce_adamw(part, from_chips, w, m, v, name):
    shape = w.shape
    cols = shape[-1]
    p2 = part.reshape(4, -1, cols)
    r2 = from_chips.reshape(3, -1, cols)
    w2, m2, v2 = _as2d(w), _as2d(m), _as2d(v)
    rows = w2.shape[0]
    tr = _row_tile(rows, 256)
    chip = (2 * lax.axis_index("x") + lax.axis_index("y")).astype(jnp.int32).reshape(1)

    def body(chip_ref, p_ref, r_ref, w_ref, m_ref, v_ref, g_ref, d_ref, nm_ref, nv_ref):
        g = p_ref[...].astype(F32)
        for k in range(3):
            g = g + r_ref[k].astype(F32)
        delta, nm, nv = _adamw_math(w_ref[...], g, m_ref[...], v_ref[...])
        g_ref[...] = g
        d_ref[...] = delta
        nm_ref[...] = nm
        nv_ref[...] = nv

    spec = pl.BlockSpec((tr, cols), lambda i, chip_ref: (i, 0))
    outs = _call(
        body, name=name, grid=(rows // tr,), prefetch=True,
        in_specs=[pl.BlockSpec((None, tr, cols), lambda i, chip_ref: (chip_ref[0], i, 0)),
                  pl.BlockSpec((3, tr, cols), lambda i, chip_ref: (0, i, 0)), spec, spec, spec],
        out_specs=[spec] * 4, out_shape=[_sds(w2.shape, F32)] * 4)(chip, p2, r2, w2, m2, v2)
    return [o.reshape(shape) for o in outs]


def _small_adamw(gathered, w, m, v, name):
    def body(a_ref, w_ref, m_ref, v_ref, g_ref, d_ref, nm_ref, nv_ref):
        g = a_ref[0]
        for k in range(1, N_DEV):
            g = g + a_ref[k]
        delta, nm, nv = _adamw_math(w_ref[...], g, m_ref[...], v_ref[...])
        g_ref[...] = g
        d_ref[...] = delta
        nm_ref[...] = nm
        nv_ref[...] = nv

    return _call(body, name=name, out_shape=[_sds(w.shape, F32)] * 4)(gathered, w, m, v)


def _norm_matmul_cols(x, g, wg, layer, mode, name):
    t, d = x.shape
    nb = wg.shape[-1]
    tm = BIG_ROW_TILE
    pieces = nb // LANES

    def body(x_ref, g_ref, w_ref, y_ref, xn_ref):
        @pl.when(pl.program_id(1) == 0)
        def _():
            xn_ref[...] = _rms(x_ref[...], g_ref[...]).astype(BF16)

        y = _dot(xn_ref[...], w_ref[...])
        if mode == "heads":
            for p in range(pieces):
                y_ref[p] = y[:, p * LANES:(p + 1) * LANES]
        else:
            y_ref[...] = y.astype(BF16)

    if mode == "cols":
        y_shape, y_spec = _sds((t, N_DEV * nb), BF16), pl.BlockSpec((tm, nb), lambda i, j: (i, j))
    else:
        y_shape = _sds((N_DEV * pieces, t, LANES), F32)
        y_spec = pl.BlockSpec((pieces, tm, LANES), lambda i, j: (j, i, 0))
    return _call(
        body, name=name, grid=(t // tm, N_DEV),
        in_specs=[pl.BlockSpec((tm, d), lambda i, j: (i, 0)), pl.BlockSpec((1, d), lambda i, j: (0, 0)),
                  pl.BlockSpec((None, None, d, nb), lambda i, j: (j, layer, 0, 0))],
        out_specs=[y_spec, pl.BlockSpec((tm, d), lambda i, j: (i, 0))],
        out_shape=[y_shape, _sds((t, d), BF16)])(x, g, wg)


def _ffn_in_swiglu(x, g, wg, layer, name):
    t, d = x.shape
    fc = wg.shape[-1]
    tm = BIG_ROW_TILE

    def body(x_ref, g_ref, wg_ref, wu_ref, gate_ref, up_ref, a_ref, xn_ref):
        @pl.when(pl.program_id(1) == 0)
        def _():
            xn_ref[...] = _rms(x_ref[...], g_ref[...]).astype(BF16)

        xn = xn_ref[...]
        gate, up = _dot(xn, wg_ref[...]), _dot(xn, wu_ref[...])
        gate_ref[...] = gate.astype(BF16)
        up_ref[...] = up.astype(BF16)
        a_ref[...] = (gate * jax.nn.sigmoid(gate) * up).astype(BF16)

    chunk = pl.BlockSpec((None, tm, fc), lambda i, c: (c, i, 0))
    return _call(
        body, name=name, grid=(t // tm, 4),
        in_specs=[pl.BlockSpec((tm, d), lambda i, c: (i, 0)), pl.BlockSpec((1, d), lambda i, c: (0, 0)),
                  pl.BlockSpec((None, None, d, fc), lambda i, c: (c, layer, 0, 0)),
                  pl.BlockSpec((None, None, d, fc), lambda i, c: (c + 4, layer, 0, 0))],
        out_specs=[chunk, chunk, chunk, pl.BlockSpec((tm, d), lambda i, c: (i, 0))],
        out_shape=[_sds((4, t, fc), BF16)] * 3 + [_sds((t, d), BF16)])(x, g, wg, wg)


def _norm_matmul_heads(x, g, wg, layer, scale, name):
    t, d = x.shape
    tm = ROW_TILE
    hp = d // LANES

    def body(x_ref, g_ref, w_ref, y_ref, xn_ref):
        xn = _rms(x_ref[...], g_ref[...]).astype(BF16)
        xn_ref[...] = xn
        y = _dot(xn, w_ref[...].reshape(d, d)) * scale
        for p in range(hp):
            y_ref[p] = y[:, p * LANES:(p + 1) * LANES]

    return _call(
        body, name=name, grid=(t // tm,),
        in_specs=[pl.BlockSpec((tm, d), lambda i: (i, 0)), pl.BlockSpec((1, d), lambda i: (0, 0)),
                  pl.BlockSpec((N_DEV, None, d // N_DEV, d), lambda i: (0, layer, 0, 0))],
        out_specs=[pl.BlockSpec((hp, tm, LANES), lambda i: (0, i, 0)), pl.BlockSpec((tm, d), lambda i: (i, 0))],
        out_shape=[_sds((hp, t, LANES), F32), _sds((t, d), BF16)])(x, g, wg)


def _shift_down(u, halo, k, tm):
    row = lax.broadcasted_iota(jnp.int32, u.shape, 0)
    out = pltpu.roll(u, k, 0)
    for j in range(k):
        out = jnp.where(row == j, halo[halo.shape[0] - k + j:halo.shape[0] - k + j + 1, :], out)
    return out


def _shift_up(u, halo, k, tm):
    row = lax.broadcasted_iota(jnp.int32, u.shape, 0)
    out = pltpu.roll(u, tm - k, 0)
    for j in range(k):
        out = jnp.where(row == tm - k + j, halo[j:j + 1, :], out)
    return out


HALO = 16


def _conv_fwd(p, cw, name):
    t, d3 = p.shape
    d = d3 // 3
    tm = ROW_TILE
    hb = tm // HALO

    def body(p_ref, prev_ref, cw_ref, z_ref):
        i = pl.program_id(0)
        b = p_ref[:, 0:d].astype(F32)
        u = p_ref[:, d:2 * d].astype(F32) * p_ref[:, 2 * d:3 * d].astype(F32)
        keep = (i > 0).astype(F32)
        hu = prev_ref[:, d:2 * d].astype(F32) * prev_ref[:, 2 * d:3 * d].astype(F32) * keep
        uc = cw_ref[2:3, :] * u + cw_ref[1:2, :] * _shift_down(u, hu, 1, tm) + cw_ref[0:1, :] * _shift_down(u, hu, 2, tm)
        z_ref[...] = (b * uc).astype(BF16)

    return _call(
        body, name=name, grid=(t // tm,),
        in_specs=[pl.BlockSpec((tm, d3), lambda i: (i, 0)),
                  pl.BlockSpec((HALO, d3), lambda i: (jnp.maximum(i * hb - 1, 0), 0)),
                  pl.BlockSpec((3, d), lambda i: (0, 0))],
        out_specs=pl.BlockSpec((tm, d), lambda i: (i, 0)), out_shape=_sds((t, d), BF16))(p, p, cw)


def _matmul_norm_residual(a3, wg, layer, g, x_res, name):
    kc_n, t, kc = a3.shape
    d = wg.shape[-1]
    per = N_DEV // kc_n
    rows = wg.shape[2]
    tm = BIG_ROW_TILE

    def body(a_ref, w_ref, g_ref, x_ref, raw_ref, xo_ref, acc_ref):
        c = pl.program_id(1)
        part = _dot(a_ref[...], w_ref[...].reshape(per * rows, d))

        @pl.when(c == 0)
        def _():
            acc_ref[...] = part

        @pl.when(c > 0)
        def _():
            acc_ref[...] += part

        @pl.when(c == kc_n - 1)
        def _():
            raw = acc_ref[...]
            raw_ref[...] = raw
            xo_ref[...] = x_ref[...] + _rms(raw, g_ref[...])

    row_spec = pl.BlockSpec((tm, d), lambda i, c: (i, 0))
    return _call(
        body, name=name, grid=(t // tm, kc_n),
        in_specs=[pl.BlockSpec((None, tm, kc), lambda i, c: (c, i, 0)),
                  pl.BlockSpec((per, None, rows, d), lambda i, c: (c, layer, 0, 0)),
                  pl.BlockSpec((1, d), lambda i, c: (0, 0)), row_spec],
        out_specs=[row_spec, row_spec], out_shape=[_sds((t, d), F32)] * 2,
        scratch_shapes=[pltpu.VMEM((tm, d), F32)])(a3, wg, g, x_res)


def _alibi_slopes(n_heads):
    hh = np.arange(n_heads, dtype=np.float32) + 1.0
    s = np.power(2.0, -8.0 * hh / n_heads).astype(np.float32)
    return jnp.asarray(np.repeat(s.reshape(n_heads // 2, 2, 1), 2 * ATT_BLOCK, axis=2))


def _band_bias(sl_ref, dil):
    u = lax.broadcasted_iota(jnp.int32, (ATT_BLOCK, 2 * ATT_BLOCK), 0)
    kk = lax.broadcasted_iota(jnp.int32, (ATT_BLOCK, 2 * ATT_BLOCK), 1)
    delta = u + ATT_BLOCK - kk
    valid = (delta >= 0) & (delta <= ATT_BLOCK)
    dist = (delta * dil).astype(F32)
    rows = [jnp.where(valid, -sl_ref[hd:hd + 1, :] * dist, NEG) for hd in range(2)]
    return jnp.concatenate(rows, axis=0)


def _stack_heads(a):
    lane = lax.broadcasted_iota(jnp.int32, a.shape, 1)
    return jnp.concatenate([jnp.where(lane < HEAD_DIM, a, 0.0), jnp.where(lane >= HEAD_DIM, a, 0.0)], axis=0).astype(BF16)


def _unstack_heads(a2):
    top, bot = a2[:ATT_BLOCK], a2[ATT_BLOCK:]
    lane = lax.broadcasted_iota(jnp.int32, top.shape, 1)
    return jnp.where(lane < HEAD_DIM, top, bot)


def _stack_cols(a):
    return jnp.concatenate([a[:, 0:1], a[:, HEAD_DIM:HEAD_DIM + 1]], axis=0)


def _fill_bias(sl_ref, bias_ref):
    kk = lax.broadcasted_iota(jnp.int32, (2 * ATT_BLOCK, 2 * ATT_BLOCK), 1)
    for gi, dil in enumerate(DILATIONS):
        bias = _band_bias(sl_ref, dil)
        bias_ref[2 * gi] = bias
        bias_ref[2 * gi + 1] = jnp.where(kk < ATT_BLOCK, NEG, bias)


def _attention_fwd(q, kv, slopes, name):
    hp, t, _ = q.shape
    ns = t // SUPER
    nd = len(DILATIONS)

    def body(sl_ref, q_ref, kc_ref, kp_ref, vc_ref, vp_ref, o_ref, lse_ref, kw_ref, vw_ref, og_ref, lg_ref, bias_ref):
        n = pl.program_id(1)
        kw_ref[0:SUPER, :] = kp_ref[...]
        kw_ref[SUPER:, :] = kc_ref[...]
        vw_ref[0:SUPER, :] = vp_ref[...]
        vw_ref[SUPER:, :] = vc_ref[...]

        @pl.when(n == 0)
        def _():
            _fill_bias(sl_ref, bias_ref)

        for gi, dil in enumerate(DILATIONS):

            def block(idx, carry, gi=gi, dil=dil):
                r, b = idx % dil, idx // dil
                qs = b * (ATT_BLOCK * dil) + r
                ks = SUPER + (b - 1) * (ATT_BLOCK * dil) + r
                first = jnp.logical_and(n == 0, b == 0).astype(jnp.int32)
                q2 = _stack_heads(q_ref[pl.ds(qs, ATT_BLOCK, stride=dil), :])
                kb = kw_ref[pl.ds(ks, 2 * ATT_BLOCK, stride=dil), :].astype(BF16)
                vb = vw_ref[pl.ds(ks, 2 * ATT_BLOCK, stride=dil), :].astype(BF16)
                s = _dot_nt(q2, kb) + bias_ref[2 * gi + first]
                m = jnp.max(s, axis=-1, keepdims=True)
                p = jnp.exp(s - m)
                l = jnp.sum(p, axis=-1, keepdims=True)
                o2 = _dot(p.astype(BF16), vb) / l
                lse2 = jnp.broadcast_to(m + jnp.log(l), (2 * ATT_BLOCK, LANES))
                og_ref[gi, pl.ds(qs, ATT_BLOCK, stride=dil), :] = _unstack_heads(o2)
                lg_ref[gi, pl.ds(qs, ATT_BLOCK, stride=dil), :] = _unstack_heads(lse2)
                return carry

            lax.fori_loop(0, SUPER // ATT_BLOCK, block, 0, unroll=ATT_UNROLL_FWD)
        lg =[lg_ref[gi] for gi in range(nd)]
        top = functools.reduce(jnp.maximum, lg)
        ws = [jnp.exp(x - top) for x in lg]
        tot = functools.reduce(jnp.add, ws)
        lse_ref[...] = top + jnp.log(tot)
        acc = ws[0] * og_ref[0]
        for gi in range(1, nd):
            acc = acc + ws[gi] * og_ref[gi]
        o_ref[...] = (acc / tot).astype(BF16)

    cur = lambda off: pl.BlockSpec((None, SUPER, LANES), lambda h, n: (h + off, n, 0))
    prev = lambda off: pl.BlockSpec((None, SUPER, LANES), lambda h, n: (h + off, jnp.maximum(n - 1, 0), 0))
    return _call(
        body, name=name, grid=(hp, ns),
        in_specs=[pl.BlockSpec((None, 2, 2 * ATT_BLOCK), lambda h, n: (h, 0, 0)), cur(0), cur(0), prev(0), cur(hp), prev(hp)],
        out_specs=[pl.BlockSpec((SUPER, LANES), lambda h, n: (n, h)), cur(0)],
        out_shape=[_sds((t, hp * LANES), BF16), _sds((hp, t, LANES), F32)],
        scratch_shapes=[pltpu.VMEM((2 * SUPER, LANES), F32), pltpu.VMEM((2 * SUPER, LANES), F32),
                        pltpu.VMEM((nd, SUPER, LANES), F32), pltpu.VMEM((nd, SUPER, LANES), F32),
                        pltpu.VMEM((2 * nd, 2 * ATT_BLOCK, 2 * ATT_BLOCK), F32)],
    )(slopes, q, kv, kv, kv, kv)


def _attention_bwd(q, kv, o, lse, d_o, slopes, q_scale, name):
    hp, t, _ = q.shape
    ns = t // SUPER

    def body(sl_ref, q_ref, kc_ref, kp_ref, vc_ref, vp_ref, o_ref, lse_ref, do_ref,
             dq_ref, dk_ref, dv_ref, kw_ref, vw_ref, dkw_ref, dvw_ref, dd_ref, bias_ref):
        n = pl.program_id(1)

        @pl.when(n == 0)
        def _():
            dkw_ref[...] = jnp.zeros_like(dkw_ref)
            dvw_ref[...] = jnp.zeros_like(dvw_ref)

        @pl.when(n > 0)
        def _():
            dkw_ref[0:SUPER, :] = dkw_ref[SUPER:, :]
            dvw_ref[0:SUPER, :] = dvw_ref[SUPER:, :]
            dkw_ref[SUPER:, :] = jnp.zeros((SUPER, LANES), F32)
            dvw_ref[SUPER:, :] = jnp.zeros((SUPER, LANES), F32)

        @pl.when(n < ns)
        def _():
            kw_ref[0:SUPER, :] = kp_ref[...]
            kw_ref[SUPER:, :] = kc_ref[...]
            vw_ref[0:SUPER, :] = vp_ref[...]
            vw_ref[SUPER:, :] = vc_ref[...]
            prod = do_ref[...] * o_ref[...].astype(F32)
            lane = lax.broadcasted_iota(jnp.int32, prod.shape, 1)
            d0 = jnp.sum(jnp.where(lane < HEAD_DIM, prod, 0.0), axis=-1, keepdims=True)
            d1 = jnp.sum(jnp.where(lane >= HEAD_DIM, prod, 0.0), axis=-1, keepdims=True)
            dd_ref[...] = jnp.where(lane < HEAD_DIM, d0, d1)
            dq_ref[...] = jnp.zeros_like(dq_ref)

            @pl.when(n == 0)
            def _():
                _fill_bias(sl_ref, bias_ref)

            for gi, dil in enumerate(DILATIONS):

                def block(idx, carry, gi=gi, dil=dil):
                    r, b = idx % dil, idx // dil
                    qs = b * (ATT_BLOCK * dil) + r
                    ks = SUPER + (b - 1) * (ATT_BLOCK * dil) + r
                    first = jnp.logical_and(n == 0, b == 0).astype(jnp.int32)
                    rows = pl.ds(qs, ATT_BLOCK, stride=dil)
                    keys = pl.ds(ks, 2 * ATT_BLOCK, stride=dil)
                    q2 = _stack_heads(q_ref[rows, :])
                    do2 = _stack_heads(do_ref[rows, :])
                    kb = kw_ref[keys, :].astype(BF16)
                    vb = vw_ref[keys, :].astype(BF16)
                    s = _dot_nt(q2, kb) + bias_ref[2 * gi + first]
                    p = jnp.exp(s - _stack_cols(lse_ref[rows, :]))
                    dp = _dot_nt(do2, vb)
                    ds = (p * (dp - _stack_cols(dd_ref[rows, :]))).astype(BF16)
                    dq_ref[rows, :] += _unstack_heads(_dot(ds, kb)) * q_scale
                    dkw_ref[keys, :] += _dot_tn(ds, q2)
                    dvw_ref[keys, :] += _dot_tn(p.astype(BF16), do2)
                    return carry

                lax.fori_loop(0, SUPER // ATT_BLOCK, block, 0, unroll=ATT_UNROLL_BWD)

        dk_ref[...] = dkw_ref[0:SUPER, :]
        dv_ref[...] = dvw_ref[0:SUPER, :]

    last = ns - 1
    cur = lambda off: pl.BlockSpec((None, SUPER, LANES), lambda h, n: (h + off, jnp.minimum(n, last), 0))
    prev = lambda off: pl.BlockSpec((None, SUPER, LANES), lambda h, n: (h + off, jnp.clip(n - 1, 0, last), 0))
    nat = pl.BlockSpec((SUPER, LANES), lambda h, n: (jnp.minimum(n, last), h))
    late = pl.BlockSpec((None, SUPER, LANES), lambda h, n: (h, jnp.maximum(n - 1, 0), 0))
    dq, dk, dv = _call(
        body, name=name, grid=(hp, ns + 1),
        in_specs=[pl.BlockSpec((None, 2, 2 * ATT_BLOCK), lambda h, n: (h, 0, 0)), cur(0), cur(0), prev(0), cur(hp), prev(hp),
                  nat, cur(0), nat],
        out_specs=[cur(0), late, late],
        out_shape=[_sds((hp, t, LANES), F32)] * 3,
        scratch_shapes=[pltpu.VMEM((2 * SUPER, LANES), F32)] * 4 + [
            pltpu.VMEM((SUPER, LANES), F32), pltpu.VMEM((2 * len(DILATIONS), 2 * ATT_BLOCK, 2 * ATT_BLOCK), F32)],
    )(slopes, q, kv, kv, kv, kv, o, lse, d_o)
    return dq, dk, dv


def _loss_head(y, target, raw, g, name):
    t, d = y.shape
    tm = ROW_TILE

    def body(y_ref, t_ref, raw_ref, g_ref, sq_ref, dy_ref, draw_ref, dg_ref):
        i = pl.program_id(0)
        err = y_ref[...] - t_ref[...]
        dy = err * (1.0 / d)
        dy_ref[...] = dy
        draw, dg = _rms_bwd(raw_ref[...], g_ref[...], dy)
        draw_ref[...] = draw.astype(BF16)
        sq = jnp.zeros((8, LANES), F32) + jnp.sum(err * err)

        @pl.when(i == 0)
        def _():
            sq_ref[...] = sq
            dg_ref[...] = dg

        @pl.when(i > 0)
        def _():
            sq_ref[...] += sq
            dg_ref[...] += dg

    row = pl.BlockSpec((tm, d), lambda i: (i, 0))
    vec = pl.BlockSpec((1, d), lambda i: (0, 0))
    return _call(
        body, name=name, grid=(t // tm,), in_specs=[row, row, row, vec],
        out_specs=[pl.BlockSpec((8, LANES), lambda i: (0, 0)), row, row, vec],
        out_shape=[_sds((8, LANES), F32), _sds((t, d), F32), _sds((t, d), BF16), _sds((1, d), F32)])(y, target, raw, g)


def _bwd_matmul_norms(a_specs, a_args, a_tile, n_steps, w_spec, w_arg, w_mat, xa, ga, resid, xb, gb, name):
    t, d = xa.shape
    tm = BWD_ROW_TILE
    na = len(a_specs)
    second = xb is not None
    per = 2 if n_steps % 2 == 0 else 1
    n_steps //= per

    def blocks_of(spec, k):
        return pl.BlockSpec(spec.block_shape, lambda i, j: spec.index_map(i, per * j + k))

    def body(*refs):
        a_refs, w_refs = refs[:per * na], refs[per * na:per * na + per]
        xa_ref, ga_ref, res_ref = refs[per * na + per:per * na + per + 3]
        rest = refs[per * na + per + 3:]
        if second:
            xb_ref, gb_ref, dx_ref, d2_ref, dga_ref, dgb_ref, acc_ref = rest
        else:
            dx_ref, dga_ref, acc_ref = rest
        i, j = pl.program_id(0), pl.program_id(1)
        part = None
        for k in range(per):
            term = _dot_nt(a_tile(per * j + k, *a_refs[k * na:(k + 1) * na]), w_mat(w_refs[k]))
            part = term if part is None else part + term

        @pl.when(j == 0)
        def _():
            acc_ref[...] = part

        @pl.when(j > 0)
        def _():
            acc_ref[...] += part

        @pl.when(j == n_steps - 1)
        def _():
            da, dga = _rms_bwd(xa_ref[...], ga_ref[...], acc_ref[...])
            dx = res_ref[...] + da
            dx_ref[...] = dx
            if second:
                d2, dgb = _rms_bwd(xb_ref[...], gb_ref[...], dx)
                d2_ref[...] = d2.astype(BF16)

            @pl.when(i == 0)
            def _():
                dga_ref[...] = dga
                if second:
                    dgb_ref[...] = dgb

            @pl.when(i > 0)
            def _():
                dga_ref[...] += dga
                if second:
                    dgb_ref[...] += dgb

    row = pl.BlockSpec((tm, d), lambda i, j: (i, 0))
    vec = pl.BlockSpec((1, d), lambda i, j: (0, 0))
    in_specs = [blocks_of(sp, k) for k in range(per) for sp in a_specs] + [blocks_of(w_spec, k) for k in range(per)]
    in_specs += [row, vec, row]
    args = list(a_args) * per + [w_arg] * per + [xa, ga, resid]
    if second:
        in_specs += [row, vec]
        args += [xb, gb]
        out_specs = [row, row, vec, vec]
        out_shape = [_sds((t, d), F32), _sds((t, d), BF16), _sds((1, d), F32), _sds((1, d), F32)]
    else:
        out_specs = [row, vec]
        out_shape = [_sds((t, d), F32), _sds((1, d), F32)]
    return _call(body, name=name, grid=(t // tm, n_steps), in_specs=in_specs, out_specs=out_specs,
                 out_shape=out_shape, scratch_shapes=[pltpu.VMEM((tm, d), F32)])(*args)


def _heads_to_rows(*refs):
    hp = refs[0].shape[0]
    cols = []
    for p in range(hp):
        v = refs[0][p]
        for r in refs[1:]:
            v = v + r[p]
        cols.append(v)
    return jnp.concatenate(cols, axis=-1).astype(BF16)


def _matmul_nt_rows(a, wg, layer, out_dtype, name):
    t, d = a.shape
    tm = ROW_TILE

    def body(a_ref, w_ref, o_ref):
        o_ref[...] = _dot_nt(a_ref[...], w_ref[...].reshape(d, d)).astype(out_dtype)

    row = pl.BlockSpec((tm, d), lambda i: (i, 0))
    return _call(body, name=name, grid=(t // tm,),
                 in_specs=[row, pl.BlockSpec((N_DEV, None, d // N_DEV, d), lambda i: (0, layer, 0, 0))],
                 out_specs=row, out_shape=_sds((t, d), out_dtype))(a, wg)


def _swiglu_bwd(d_ff, wg, layer, gate, up, name):
    t, d = d_ff.shape
    fc = gate.shape[-1]
    rows = wg.shape[2]
    tm = BIG_ROW_TILE

    def body(df_ref, w_ref, g_ref, u_ref, dh_ref):
        da = _dot_nt(df_ref[...], w_ref[...].reshape(2 * rows, d))
        gate, up = g_ref[...].astype(F32), u_ref[...].astype(F32)
        sig = jax.nn.sigmoid(gate)
        dh_ref[0] = (da * up * (sig * (1.0 + gate * (1.0 - sig)))).astype(BF16)
        dh_ref[1] = (da * (gate * sig)).astype(BF16)

    return _call(
        body, name=name, grid=(t // tm, 4),
        in_specs=[pl.BlockSpec((tm, d), lambda i, c: (i, 0)),
                  pl.BlockSpec((2, None, rows, d), lambda i, c: (c, layer, 0, 0)),
                  pl.BlockSpec((None, tm, fc), lambda i, c: (c, i, 0)),
                  pl.BlockSpec((None, tm, fc), lambda i, c: (c, i, 0))],
        out_specs=pl.BlockSpec((None, 2, tm, fc), lambda i, c: (c, 0, i, 0)),
        out_shape=_sds((4, 2, t, fc), BF16))(d_ff, wg, gate, up)


def _conv_bwd(p, d_z, cw, name):
    t, d3 = p.shape
    d = d3 // 3
    tm = ROW_TILE
    hb = tm // HALO
    nt = t // tm

    def body(p_ref, prev_ref, next_ref, dz_ref, dzn_ref, cw_ref, dp_ref, dcw_ref):
        i = pl.program_id(0)
        b = p_ref[:, 0:d].astype(F32)
        c = p_ref[:, d:2 * d].astype(F32)
        h = p_ref[:, 2 * d:3 * d].astype(F32)
        u = c * h
        hu = prev_ref[:, d:2 * d].astype(F32) * prev_ref[:, 2 * d:3 * d].astype(F32) * (i > 0).astype(F32)
        u1, u2 = _shift_down(u, hu, 1, tm), _shift_down(u, hu, 2, tm)
        uc = cw_ref[2:3, :] * u + cw_ref[1:2, :] * u1 + cw_ref[0:1, :] * u2
        dz = dz_ref[...].astype(F32)
        duc = dz * b
        dn = dzn_ref[...].astype(F32) * next_ref[:, 0:d].astype(F32) * (i < nt - 1).astype(F32)
        du = cw_ref[2:3, :] * duc + cw_ref[1:2, :] * _shift_up(duc, dn, 1, tm) + cw_ref[0:1, :] * _shift_up(duc, dn, 2, tm)
        dp_ref[:, 0:d] = (dz * uc).astype(BF16)
        dp_ref[:, d:2 * d] = (du * h).astype(BF16)
        dp_ref[:, 2 * d:3 * d] = (du * c).astype(BF16)
        dcw = jnp.concatenate([jnp.sum(duc * u2, axis=0, keepdims=True), jnp.sum(duc * u1, axis=0, keepdims=True),
                               jnp.sum(duc * u, axis=0, keepdims=True)], axis=0)

        @pl.when(i == 0)
        def _():
            dcw_ref[...] = dcw

        @pl.when(i > 0)
        def _():
            dcw_ref[...] += dcw

    last_halo = t // HALO - 1
    return _call(
        body, name=name, grid=(nt,),
        in_specs=[pl.BlockSpec((tm, d3), lambda i: (i, 0)),
                  pl.BlockSpec((HALO, d3), lambda i: (jnp.maximum(i * hb - 1, 0), 0)),
                  pl.BlockSpec((HALO, d3), lambda i: (jnp.minimum((i + 1) * hb, last_halo), 0)),
                  pl.BlockSpec((tm, d), lambda i: (i, 0)),
                  pl.BlockSpec((HALO, d), lambda i: (jnp.minimum((i + 1) * hb, last_halo), 0)),
                  pl.BlockSpec((3, d), lambda i: (0, 0))],
        out_specs=[pl.BlockSpec((tm, d3), lambda i: (i, 0)), pl.BlockSpec((3, d), lambda i: (0, 0))],
        out_shape=[_sds((t, d3), BF16), _sds((3, d), F32)])(p, p, p, d_z, d_z, cw)


def _grad_weight(a_specs, a_args, a_tile, b_specs, b_args, b_tile, n_out, acc_shape, out_spec, out_shape, t, name):
    tt = BIG_ROW_TILE
    na, nb = len(a_specs), len(b_specs)

    def body(*refs):
        a_refs, b_refs = refs[:na], refs[na:na + nb]
        o_ref, acc_ref = refs[na + nb:]
        s = pl.program_id(1)
        part = _dot_tn(a_tile(pl.program_id(0), *a_refs), b_tile(pl.program_id(0), *b_refs))

        @pl.when(s == 0)
        def _():
            acc_ref[...] = part

        @pl.when(s > 0)
        def _():
            acc_ref[...] += part

        @pl.when(s == t // tt - 1)
        def _():
            o_ref[...] = acc_ref[...].astype(BF16).reshape(o_ref.shape)

    return _call(body, name=name, grid=(n_out, t // tt), in_specs=list(a_specs) + list(b_specs), out_specs=out_spec,
                 out_shape=out_shape, scratch_shapes=[pltpu.VMEM(acc_shape, F32)])(*a_args, *b_args)


def _ident(*args):
    return args[-1][...]


def _heads_tile(j, *refs):
    return _heads_to_rows(*refs)


def kernel(x, norm_g, conv_in_w, conv_w, conv_out_w, kv_norm_g, kv_w, q_w, o_w, ffn_in_w, ffn_out_w, loss_target, m_norm_g, m_conv_in_w, m_conv_w, m_conv_out_w, m_kv_norm_g, m_kv_w, m_q_w, m_o_w, m_ffn_in_w, m_ffn_out_w, v_norm_g, v_conv_in_w, v_conv_w, v_conv_out_w, v_kv_norm_g, v_kv_w, v_q_w, v_o_w, v_ffn_in_w, v_ffn_out_w):
    x0 = x[0]
    target = loss_target[0]
    t, d = x0.shape
    depth = norm_g.shape[0]
    n_a = conv_in_w.shape[0]
    n_b = q_w.shape[0]
    hp = d // LANES
    tm, tg = BWD_ROW_TILE, BIG_ROW_TILE
    assert t % SUPER == 0 and d % LANES == 0 and depth == n_a + n_b
    dev = 4 * lax.axis_index("x") + 2 * lax.axis_index("y") + lax.axis_index("c")

    n_small = 4 * depth + 3 * n_a
    small_rows = -(-(n_small + 1) // 8) * 8
    small_local = jnp.concatenate([norm_g.reshape(4 * depth, -1), conv_w.reshape(3 * n_a, -1),
                                   jnp.zeros((small_rows - n_small, norm_g.shape[-1]), F32)], axis=0)
    big = {"conv_in_w": conv_in_w, "conv_out_w": conv_out_w, "kv_w": kv_w[None], "q_w": q_w, "o_w": o_w,
           "ffn_in_w": ffn_in_w, "ffn_out_w": ffn_out_w}
    names = list(big)

    def group(layer):
        if layer < n_a:
            return [("conv_in_w", layer), ("conv_out_w", layer), ("ffn_in_w", layer), ("ffn_out_w", layer)]
        j = layer - n_a
        return ([("kv_w", 0)] if j == 0 else []) + [("q_w", j), ("o_w", j), ("ffn_in_w", layer), ("ffn_out_w", layer)]

    slot = dev.astype(jnp.int32).reshape(1)
    first = _all_gather([small_local] + [_cast_layer(big[k], i, None, f"cast_{k}_{i}") for k, i in group(0)], "gather_weights")
    small_all = first[0].transpose(1, 0, 2).reshape(small_rows, d)
    wl = {key: a[:, None] for key, a in zip(group(0), first[1:])}
    in_flight, token = {}, 0.0
    for layer in range(1, depth):
        lands = [_cast_layer(big[k], i, slot, f"cast_{k}_{i}") for k, i in group(layer)]
        send_sems, recv_sems, _, lands, tok = _send_start([], lands, f"gather_start_l{layer}")
        in_flight[layer] = (send_sems, recv_sems, lands)
        token = token + tok[0, 0]
    W = lambda k, i: (wl[(k, i)], 0)
    gain = lambda layer, k: small_all[4 * layer + k][None]
    taps = lambda layer: small_all[4 * depth + 3 * layer: 4 * depth + 3 * layer + 3]
    g_kv = kv_norm_g[None]
    slopes = _alibi_slopes(d // HEAD_DIM)
    fc = big["ffn_in_w"].shape[-1]
    cb = big["conv_in_w"].shape[-1]
    kvb = big["kv_w"].shape[-1]
    q_scale = HEAD_DIM ** -0.5

    saved = []
    kv = kvn = None
    xs = x0
    for layer in range(depth):
        tag = f"_l{layer}"
        if layer in in_flight:
            send_sems, recv_sems, lands = in_flight[layer]
            lands = _send_wait(send_sems, recv_sems, [], lands, xs, f"gather_wait_l{layer}")
            wl.update({key: a[:, None] for key, a in zip(group(layer), lands)})
        s = {"x_in": xs}
        g0 = gain(layer, 0) + token if layer == 0 else gain(layer, 0)
        if layer < n_a:
            s["p"], s["xn"] = _norm_matmul_cols(xs, g0, *W("conv_in_w", layer), "cols", "conv_in" + tag)
            s["z"] = _conv_fwd(s["p"], taps(layer), "conv" + tag)
            s["mix"], x_mid = _matmul_norm_residual(s["z"][None], *W("conv_out_w", layer), gain(layer, 1), xs, "conv_out" + tag)
        else:
            j = layer - n_a
            if kv is None:
                kv, kvn = _norm_matmul_cols(xs, g_kv, *W("kv_w", 0), "heads", "kv_proj")
            s["q"], s["xn"] = _norm_matmul_heads(xs, g0, *W("q_w", j), q_scale, "q_proj" + tag)
            s["o"], s["lse"] = _attention_fwd(s["q"], kv, slopes, "attention" + tag)
            s["mix"], x_mid = _matmul_norm_residual(s["o"][None], *W("o_w", j), gain(layer, 1), xs, "o_proj" + tag)
        s["x_mid"] = x_mid
        s["gate"], s["up"], s["a"], s["fn"] = _ffn_in_swiglu(x_mid, gain(layer, 2), *W("ffn_in_w", layer), "ffn_in" + tag)
        s["ff"], xs = _matmul_norm_residual(s["a"], *W("ffn_out_w", layer), gain(layer, 3), x_mid, "ffn_out" + tag)
        saved.append(s)

    last = saved[-1]
    sq, dx_out, d_ff, dg3 = _loss_head(xs, target, last["ff"], gain(depth - 1, 3), "loss_head")
    loss = lax.psum(sq[0, 0] * (0.5 / d), ("x", "y", "c"))

    dgain = {(depth - 1, 3): dg3}
    dtaps = {}
    grads = {k: [None] * big[k].shape[0] for k in names}
    dkv_parts = []
    scattering = {}
    for layer in reversed(range(depth)):
        tag = f"_l{layer}"
        s = saved[layer]
        dh = _swiglu_bwd(d_ff, *W("ffn_out_w", layer), s["gate"], s["up"], "swiglu_bwd" + tag)
        rows_out = big["ffn_out_w"].shape[1]
        grads["ffn_out_w"][layer] = _grad_weight(
            [pl.BlockSpec((None, tg, fc), lambda c, i: (c, i, 0))], [s["a"]], _ident,
            [pl.BlockSpec((tg, d), lambda c, i: (i, 0))], [d_ff], _ident,
            4, (fc, d), pl.BlockSpec((2, rows_out, d), lambda c, i: (c, 0, 0)), _sds((N_DEV, rows_out, d), BF16), t,
            "grad_ffn_out" + tag)
        grads["ffn_in_w"][layer] = _grad_weight(
            [pl.BlockSpec((tg, d), lambda j, i: (i, 0))], [s["fn"]], _ident,
            [pl.BlockSpec((None, None, tg, fc), lambda j, i: (j % 4, j // 4, i, 0))], [dh], _ident,
            N_DEV, (d, fc), pl.BlockSpec((None, d, fc), lambda j, i: (j, 0, 0)), _sds((N_DEV, d, fc), BF16), t,
            "grad_ffn_in" + tag)
        dx_mid, d_mix, dg2, dg1 = _bwd_matmul_norms(
            [pl.BlockSpec((None, None, tm, fc), lambda i, j: (j % 4, j // 4, i, 0))], [dh], _ident, N_DEV,
            pl.BlockSpec((None, None, d, fc), lambda i, j: (j, 0, 0, 0)), W("ffn_in_w", layer)[0], _ident,
            s["x_mid"], gain(layer, 2), dx_out, s["mix"], gain(layer, 1), "ffn_in_bwd" + tag)
        dgain[(layer, 2)], dgain[(layer, 1)] = dg2, dg1
        full_rows = pl.BlockSpec((N_DEV, d // N_DEV, d), lambda j, i: (0, 0, 0))
        rows_w = lambda wname, idx: (pl.BlockSpec((N_DEV, None, d // N_DEV, d), lambda i, j: (0, 0, 0, 0)), W(wname, idx)[0],
                                     lambda w_ref: w_ref[...].reshape(d, d))
        if layer < n_a:
            d_z = _matmul_nt_rows(d_mix, *W("conv_out_w", layer), BF16, "conv_out_bwd" + tag)
            grads["conv_out_w"][layer] = _grad_weight(
                [pl.BlockSpec((tg, d), lambda j, i: (i, 0))], [s["z"]], _ident,
                [pl.BlockSpec((tg, d), lambda j, i: (i, 0))], [d_mix], _ident,
                1, (d, d), full_rows, _sds((N_DEV, d // N_DEV, d), BF16), t, "grad_conv_out" + tag)
            d_p, dtaps[layer] = _conv_bwd(s["p"], d_z, taps(layer), "conv_bwd" + tag)
            grads["conv_in_w"][layer] = _grad_weight(
                [pl.BlockSpec((tg, d), lambda j, i: (i, 0))], [s["xn"]], _ident,
                [pl.BlockSpec((tg, cb), lambda j, i: (i, j))], [d_p], _ident,
                N_DEV, (d, cb), pl.BlockSpec((None, d, cb), lambda j, i: (j, 0, 0)), _sds((N_DEV, d, cb), BF16), t,
                "grad_conv_in" + tag)
            a_specs, a_args, a_tile, n_steps = [pl.BlockSpec((tm, cb), lambda i, j: (i, j))], [d_p], _ident, N_DEV
            w_spec = pl.BlockSpec((None, None, d, cb), lambda i, j: (j, 0, 0, 0))
            w_arg, w_mat = W("conv_in_w", layer)[0], _ident
            resid = dx_mid
        else:
            j_b = layer - n_a
            d_o = _matmul_nt_rows(d_mix, *W("o_w", j_b), F32, "o_proj_bwd" + tag)
            grads["o_w"][j_b] = _grad_weight(
                [pl.BlockSpec((tg, d), lambda j, i: (i, 0))], [s["o"]], _ident,
                [pl.BlockSpec((tg, d), lambda j, i: (i, 0))], [d_mix], _ident,
                1, (d, d), full_rows, _sds((N_DEV, d // N_DEV, d), BF16), t, "grad_o" + tag)
            dq, dk, dv = _attention_bwd(s["q"], kv, s["o"], s["lse"], d_o, slopes, q_scale, "attention_bwd" + tag)
            dkv_parts.append((dk, dv))
            heads_spec = pl.BlockSpec((hp, tg, LANES), lambda j, i: (0, i, 0))
            grads["q_w"][j_b] = _grad_weight(
                [pl.BlockSpec((tg, d), lambda j, i: (i, 0))], [s["xn"]], _ident,
                [heads_spec], [dq], _heads_tile,
                1, (d, d), full_rows, _sds((N_DEV, d // N_DEV, d), BF16), t, "grad_q" + tag)
            a_specs, a_args, a_tile, n_steps = [pl.BlockSpec((hp, tm, LANES), lambda i, j: (0, i, 0))], [dq], _heads_tile, 1
            w_spec, w_arg, w_mat = rows_w("q_w", j_b)
            resid = dx_mid
            if layer == n_a:
                pieces = kvb // LANES
                halves = []
                for src in (0, 1):
                    halves.append([part[src] for part in dkv_parts])
                n_half = len(dkv_parts)
                kv_args = [arr for src in (0, 1) for arr in halves[src]]

                def kv_block(src, j):
                    return jnp.where((j // 4) == src, j % 4, 0)

                def kv_tile(j, *refs):
                    keys = _heads_to_rows(*refs[:n_half])
                    vals = _heads_to_rows(*refs[n_half:])
                    return jnp.where(j < 4, keys, vals)

                kv_specs = [pl.BlockSpec((pieces, tm, LANES), functools.partial(lambda i, j, src: (kv_block(src, j), i, 0), src=src))
                            for src in (0, 1) for _ in range(n_half)]
                resid, dgain["kv"] = _bwd_matmul_norms(
                    kv_specs, kv_args, kv_tile, N_DEV,
                    pl.BlockSpec((None, None, d, kvb), lambda i, j: (j, 0, 0, 0)), W("kv_w", 0)[0], _ident,
                    s["x_in"], g_kv, dx_mid, None, None, "kv_proj_bwd")
                kv_b_specs = [pl.BlockSpec((pieces, tg, LANES), functools.partial(lambda j, i, src: (kv_block(src, j), i, 0), src=src))
                              for src in (0, 1) for _ in range(n_half)]
                grads["kv_w"][0] = _grad_weight(
                    [pl.BlockSpec((tg, d), lambda j, i: (i, 0))], [kvn], _ident,
                    kv_b_specs, kv_args, kv_tile,
                    N_DEV, (d, kvb), pl.BlockSpec((None, d, kvb), lambda j, i: (j, 0, 0)), _sds((N_DEV, d, kvb), BF16), t,
                    "grad_kv")
        parts = [grads[k][i] for k, i in group(layer)]
        zones = _place_own(parts, "scatter_place" + tag)
        send_sems, recv_sems, parts, zones, tok = _send_start(parts, zones, "scatter_start" + tag)
        scattering[layer] = (send_sems, recv_sems, parts, zones)
        if layer > 0:
            prev = saved[layer - 1]
            dx_out, d_ff, dg0, dg3p = _bwd_matmul_norms(
                a_specs, a_args, a_tile, n_steps, w_spec, w_arg, w_mat,
                s["x_in"], gain(layer, 0) + tok[0, 0], resid, prev["ff"], gain(layer - 1, 3), "mixer_in_bwd" + tag)
            dgain[(layer, 0)], dgain[(layer - 1, 3)] = dg0, dg3p
        else:
            grad_x, dg0 = _bwd_matmul_norms(
                a_specs, a_args, a_tile, n_steps, w_spec, w_arg, w_mat,
                s["x_in"], gain(layer, 0), resid, None, None, "mixer_in_bwd" + tag)
            dgain[(layer, 0)] = dg0

    small_grad = jnp.concatenate(
        [dgain[(layer, k)] for layer in range(depth) for k in range(4)] + [dtaps[layer] for layer in range(n_a)]
        + [dgain["kv"]] + [jnp.zeros((small_rows - n_small - 1, d), F32)], axis=0)
    small_grads_all = _all_gather([small_grad], "gather_small_grads")[0]
    lo = dev * (d // N_DEV)

    def pack(ng, cwp, kvg):
        rows = jnp.concatenate([ng.reshape(4 * depth, -1), cwp.reshape(3 * n_a, -1)], axis=0)
        z = lax.dynamic_update_slice(jnp.zeros((small_rows, d), F32), rows, (0, lo))
        return lax.dynamic_update_slice(z, kvg[None], (n_small, 0))

    w_small = lax.dynamic_update_slice(small_all, g_kv, (n_small, 0))
    m_small, v_small = pack(m_norm_g, m_conv_w, m_kv_norm_g), pack(v_norm_g, v_conv_w, v_kv_norm_g)
    sm = _small_adamw(small_grads_all, w_small, m_small, v_small, "adamw_small")

    def unpack(a):
        mine = lax.dynamic_slice(a, (0, lo), (small_rows, d // N_DEV))
        return (mine[:4 * depth].reshape(norm_g.shape), mine[4 * depth:n_small].reshape(conv_w.shape), a[n_small])

    small_out = [unpack(a) for a in sm]

    moments = {"conv_in_w": (m_conv_in_w, v_conv_in_w), "conv_out_w": (m_conv_out_w, v_conv_out_w),
               "kv_w": (m_kv_w[None], v_kv_w[None]), "q_w": (m_q_w, v_q_w), "o_w": (m_o_w, v_o_w),
               "ffn_in_w": (m_ffn_in_w, v_ffn_in_w), "ffn_out_w": (m_ffn_out_w, v_ffn_out_w)}
    res = {k: [None] * big[k].shape[0] for k in names}
    for layer in reversed(range(depth)):
        send_sems, recv_sems, parts, zones = scattering[layer]
        zones = _send_wait(send_sems, recv_sems, parts, zones, grad_x, f"scatter_wait_l{layer}")
        for (k, i), zone in zip(group(layer), zones):
            res[k][i] = _sum_adamw(zone, big[k], moments[k][0], moments[k][1], i, f"adamw_{k}_{i}")

    def big_out(k, which):
        st = jnp.stack([res[k][i][which] for i in range(big[k].shape[0])], axis=0)
        return st[0] if k == "kv_w" else st

    out_names = ["norm_g", "conv_in_w", "conv_w", "conv_out_w", "kv_norm_g", "kv_w", "q_w", "o_w", "ffn_in_w", "ffn_out_w"]
    small_pos = {"norm_g": 0, "conv_w": 1, "kv_norm_g": 2}
    outs = [loss, grad_x[None]]
    for which in range(4):
        for k in out_names:
            outs.append(small_out[which][small_pos[k]] if k in small_pos else big_out(k, which))
    return tuple(outs)
```

```python
import functools
import math

import numpy as np
import jax
import jax.numpy as jnp
from jax import lax
from jax.experimental import pallas as pl
from jax.experimental.pallas import tpu as pltpu

F32 = jnp.float32
BF16 = jnp.bfloat16

N_DEV = 8
RMS_EPS = 1e-6
HEAD_DIM = 64
LANES = 128
ATT_BLOCK = 128
DILATIONS = (1, 4, 16)
SUPER = ATT_BLOCK * DILATIONS[-1]
NEG = -1e30
ATT_UNROLL_FWD = 8
ATT_UNROLL_BWD = 4

ADAM_LR, ADAM_B1, ADAM_B2, ADAM_EPS, ADAM_WD, ADAM_STEP = 0.001, 0.9, 0.999, 1e-08, 0.01, 10

ROW_TILE = 512
BIG_ROW_TILE = 1024
BWD_ROW_TILE = 512
MESH = pl.DeviceIdType.MESH


def _call(body, *, name, grid=None, in_specs=None, out_specs=None, out_shape=None, scratch_shapes=(), prefetch=False,
          **params):
    cp = pltpu.CompilerParams(**params) if params else None
    if prefetch:
        spec = pltpu.PrefetchScalarGridSpec(num_scalar_prefetch=1, grid=grid, in_specs=in_specs, out_specs=out_specs,
                                            scratch_shapes=list(scratch_shapes))
        return pl.pallas_call(body, name=name, grid_spec=spec, out_shape=out_shape, compiler_params=cp)
    kwargs = {k: v for k, v in (("grid", grid), ("in_specs", in_specs), ("out_specs", out_specs)) if v is not None}
    return pl.pallas_call(body, name=name, out_shape=out_shape, scratch_shapes=list(scratch_shapes),
                          compiler_params=cp, **kwargs)


def _sds(shape, dtype):
    return jax.ShapeDtypeStruct(tuple(shape), dtype)


def _rms(x, g):
    r = lax.rsqrt(jnp.mean(x * x, axis=-1, keepdims=True) + RMS_EPS)
    return x * r * g


def _rms_bwd(x, g, dy):
    r = lax.rsqrt(jnp.mean(x * x, axis=-1, keepdims=True) + RMS_EPS)
    xh = x * r
    dxh = dy * g
    dx = r * (dxh - xh * jnp.mean(dxh * xh, axis=-1, keepdims=True))
    return dx, jnp.sum(dy * xh, axis=0, keepdims=True)


def _dot(a, b):
    return jnp.dot(a, b, preferred_element_type=F32)


def _dot_nt(a, b):
    return lax.dot_general(a, b, (((1,), (1,)), ((), ())), preferred_element_type=F32)


def _dot_tn(a, b):
    return lax.dot_general(a, b, (((0,), (0,)), ((), ())), preferred_element_type=F32)


def _mesh_pos():
    return lax.axis_index("x"), lax.axis_index("y"), lax.axis_index("c")


def _all_gather(arrs, name):
    n = len(arrs)

    def body(*refs):
        ins, outs = refs[:n], refs[n:2 * n]
        send_sems, recv_sems, local_sems = refs[2 * n:]
        x, y, c = _mesh_pos()
        me, sibling = (x, y, c), (x, y, 1 - c)
        chips = [(1 - x, y), (x, 1 - y), (1 - x, 1 - y)]

        def copy(a, k, block, to, src=None):
            dst = outs[a].at[4 * block[0] + 2 * block[1] + block[2]]
            return pltpu.make_async_remote_copy(
                src_ref=dst if src is None else src, dst_ref=dst, send_sem=send_sems.at[a, k],
                recv_sem=recv_sems.at[a, k], device_id=to, device_id_type=MESH)

        started = []
        for a in range(n):
            mine = pltpu.make_async_copy(ins[a], outs[a].at[4 * x + 2 * y + c], local_sems.at[a])
            mine.start()
            started.append(mine)
        first = []
        for a in range(n):
            first.append(copy(a, 0, me, sibling, src=ins[a]))
            first += [copy(a, 1 + j, me, (*chip, c), src=ins[a]) for j, chip in enumerate(chips)]
        for cp in first:
            cp.start()
        passed = []
        for a in range(n):
            for j, chip in enumerate(chips):
                copy(a, 1 + j, (*chip, c), me).wait_recv()
                fwd = copy(a, 4 + j, (*chip, c), sibling)
                fwd.start()
                passed.append(fwd)
        for a in range(n):
            copy(a, 0, sibling, me).wait_recv()
            for j, chip in enumerate(chips):
                copy(a, 4 + j, (*chip, 1 - c), me).wait_recv()
        for cp in first + passed:
            cp.wait_send()
        for cp in started:
            cp.wait()

    any_spec = pl.BlockSpec(memory_space=pl.ANY)
    outs = _call(
        body, name=name, in_specs=[any_spec] * n, out_specs=[any_spec] * n,
        out_shape=[_sds((N_DEV,) + a.shape, a.dtype) for a in arrs],
        scratch_shapes=[pltpu.SemaphoreType.DMA((n, 7)), pltpu.SemaphoreType.DMA((n, 7)), pltpu.SemaphoreType.DMA((n,))],
        has_side_effects=True,
    )(*arrs)
    return list(outs)


HBM_SPEC = pl.BlockSpec(memory_space=pltpu.HBM)
SEM_SPEC = pl.BlockSpec(memory_space=pltpu.SEMAPHORE)
DATAFLOW = pltpu.SideEffectType.DATAFLOW_SIDE_EFFECTING
PEERS = [(dx, dy, dc) for dx in (0, 1) for dy in (0, 1) for dc in (0, 1)][1:]


def _peer(flip):
    x, y, c = _mesh_pos()
    return tuple(1 - v if f else v for v, f in zip((x, y, c), flip))


def _slot(pos):
    return 4 * pos[0] + 2 * pos[1] + pos[2]


def _in_hbm(a):
    return pltpu.with_memory_space_constraint(a, pltpu.HBM)


def _direct_copies(srcs, lands, send_sems, recv_sems, scatter):
    me = _slot(_mesh_pos())
    copies = []
    for a in range(len(lands)):
        for k, flip in enumerate(PEERS):
            peer = _peer(flip)
            src = srcs[a].at[_slot(peer)] if scatter else lands[a].at[me]
            idx = a * len(PEERS) + k
            copies.append(pltpu.make_async_remote_copy(
                src_ref=src, dst_ref=lands[a].at[me], send_sem=send_sems.at[idx], recv_sem=recv_sems.at[idx],
                device_id=peer, device_id_type=MESH))
    return copies


def _send_start(srcs, lands, name):
    ns, nl = len(srcs), len(lands)
    scatter = ns > 0

    def body(*refs):
        src_refs, land_refs = refs[:ns], refs[ns:ns + nl]
        send_sems, recv_sems = refs[ns + nl:ns + nl + 2]
        token = refs[-1]
        for cp in _direct_copies(src_refs, land_refs, send_sems, recv_sems, scatter):
            cp.start()
        token[...] = jnp.zeros_like(token)

    sem = pltpu.SemaphoreType.DMA((nl * len(PEERS),))
    outs = pl.pallas_call(
        body, name=name,
        out_shape=(sem, sem) + tuple(pltpu.HBM(a.shape, a.dtype) for a in list(srcs) + list(lands))
        + (_sds((8, LANES), F32),),
        in_specs=[HBM_SPEC] * (ns + nl),
        out_specs=(SEM_SPEC, SEM_SPEC) + (HBM_SPEC,) * (ns + nl) + (pl.BlockSpec(memory_space=pltpu.VMEM),),
        input_output_aliases={i: 2 + i for i in range(ns + nl)},
        compiler_params=pltpu.CompilerParams(has_side_effects=DATAFLOW),
    )(*[_in_hbm(a) for a in list(srcs) + list(lands)])
    send_sems, recv_sems = outs[0], outs[1]
    return send_sems, recv_sems, list(outs[2:2 + ns]), list(outs[2 + ns:2 + ns + nl]), outs[-1]


def _send_wait(send_sems, recv_sems, srcs, lands, after, name):
    ns, nl = len(srcs), len(lands)
    scatter = ns > 0

    def body(*refs):
        src_refs, land_refs = refs[:ns], refs[ns:ns + nl]
        send_sems, recv_sems = refs[ns + nl:ns + nl + 2]
        copies = _direct_copies(src_refs, land_refs, send_sems, recv_sems, scatter)
        for cp in copies:
            cp.wait_send()
        for cp in copies:
            cp.wait_recv()

    outs = pl.pallas_call(
        body, name=name,
        out_shape=tuple(pltpu.HBM(a.shape, a.dtype) for a in list(srcs) + list(lands)),
        in_specs=[HBM_SPEC] * (ns + nl) + [SEM_SPEC, SEM_SPEC, pl.BlockSpec(memory_space=pl.ANY)],
        out_specs=(HBM_SPEC,) * (ns + nl),
        input_output_aliases={i: i for i in range(ns + nl)},
        compiler_params=pltpu.CompilerParams(has_side_effects=DATAFLOW),
    )(*srcs, *lands, send_sems, recv_sems, after)
    return list(outs[:ns]), list(outs[ns:])


def _row_tile(rows, cap=512):
    t = min(rows, cap)
    while rows % t or (t % 16 and t != rows):
        t -= 1
    return t


def _as2d(a):
    return a.reshape(-1, a.shape[-1])


def _cast_layer(w, layer, slot, name):
    _, rows, cols = w.shape
    tr = _row_tile(rows)

    def body(*refs):
        refs[-1][...] = refs[-2][...].astype(BF16)

    if slot is None:
        return _call(body, name=name, grid=(rows // tr,),
                     in_specs=[pl.BlockSpec((None, tr, cols), lambda i: (layer, i, 0))],
                     out_specs=pl.BlockSpec((tr, cols), lambda i: (i, 0)), out_shape=_sds((rows, cols), BF16))(w)
    return _call(body, name=name, grid=(rows // tr,), prefetch=True,
                 in_specs=[pl.BlockSpec((None, tr, cols), lambda i, s: (layer, i, 0))],
                 out_specs=pl.BlockSpec((None, tr, cols), lambda i, s: (s[0], i, 0)),
                 out_shape=_sds((N_DEV, rows, cols), BF16))(slot, w)


def _sum_adamw(slot, part, land, w, m, v, layer, name):
    _, rows, cols = land.shape
    tr = _row_tile(rows, 256)

    def body(s_ref, p_ref, l_ref, w_ref, m_ref, v_ref, g_ref, d_ref, nm_ref, nv_ref):
        own = p_ref[...]
        g = jnp.zeros((tr, cols), F32)
        for k in range(N_DEV):
            g = g + jnp.where(s_ref[0] == k, own, l_ref[k]).astype(F32)
        delta, nm, nv = _adamw_math(w_ref[...], g, m_ref[...], v_ref[...])
        g_ref[...] = g
        d_ref[...] = delta
        nm_ref[...] = nm
        nv_ref[...] = nv

    lay = pl.BlockSpec((None, tr, cols), lambda i, s: (layer, i, 0))
    out = pl.BlockSpec((tr, cols), lambda i, s: (i, 0))
    return _call(body, name=name, grid=(rows // tr,), prefetch=True,
                 in_specs=[pl.BlockSpec((None, tr, cols), lambda i, s: (s[0], i, 0)),
                           pl.BlockSpec((N_DEV, tr, cols), lambda i, s: (0, i, 0)), lay, lay, lay],
                 out_specs=[out] * 4, out_shape=[_sds((rows, cols), F32)] * 4)(slot, part, land, w, m, v)


def _adamw_math(w, g, m, v):
    m = ADAM_B1 * m + (1.0 - ADAM_B1) * g
    v = ADAM_B2 * v + (1.0 - ADAM_B2) * (g * g)
    m_hat = m / (1.0 - ADAM_B1 ** ADAM_STEP)
    v_hat = v / (1.0 - ADAM_B2 ** ADAM_STEP)
    delta = -ADAM_LR * (m_hat / (jnp.sqrt(v_hat) + ADAM_EPS) + ADAM_WD * w)
    return delta, m, v


def _small_adamw(gathered, w, m, v, name):
    def body(a_ref, w_ref, m_ref, v_ref, g_ref, d_ref, nm_ref, nv_ref):
        g = a_ref[0]
        for k in range(1, N_DEV):
            g = g + a_ref[k]
        delta, nm, nv = _adamw_math(w_ref[...], g, m_ref[...], v_ref[...])
        g_ref[...] = g
        d_ref[...] = delta
        nm_ref[...] = nm
        nv_ref[...] = nv

    return _call(body, name=name, out_shape=[_sds(w.shape, F32)] * 4)(gathered, w, m, v)


def _norm_matmul_cols(x, g, wg, layer, mode, name):
    t, d = x.shape
    nb = wg.shape[-1]
    tm = BIG_ROW_TILE
    pieces = nb // LANES

    def body(x_ref, g_ref, w_ref, y_ref, xn_ref):
        @pl.when(pl.program_id(1) == 0)
        def _():
            xn_ref[...] = _rms(x_ref[...], g_ref[...]).astype(BF16)

        y = _dot(xn_ref[...], w_ref[...])
        if mode == "heads":
            for p in range(pieces):
                y_ref[p] = y[:, p * LANES:(p + 1) * LANES]
        else:
            y_ref[...] = y.astype(BF16)

    if mode == "cols":
        y_shape, y_spec = _sds((t, N_DEV * nb), BF16), pl.BlockSpec((tm, nb), lambda i, j: (i, j))
    else:
        y_shape = _sds((N_DEV * pieces, t, LANES), F32)
        y_spec = pl.BlockSpec((pieces, tm, LANES), lambda i, j: (j, i, 0))
    return _call(
        body, name=name, grid=(t // tm, N_DEV),
        in_specs=[pl.BlockSpec((tm, d), lambda i, j: (i, 0)), pl.BlockSpec((1, d), lambda i, j: (0, 0)),
                  pl.BlockSpec((None, None, d, nb), lambda i, j: (j, layer, 0, 0))],
        out_specs=[y_spec, pl.BlockSpec((tm, d), lambda i, j: (i, 0))],
        out_shape=[y_shape, _sds((t, d), BF16)])(x, g, wg)


def _ffn_in_swiglu(x, g, wg, layer, name):
    t, d = x.shape
    fc = wg.shape[-1]
    tm = BIG_ROW_TILE

    def body(x_ref, g_ref, wg_ref, wu_ref, gate_ref, up_ref, a_ref, xn_ref):
        @pl.when(pl.program_id(1) == 0)
        def _():
            xn_ref[...] = _rms(x_ref[...], g_ref[...]).astype(BF16)

        xn = xn_ref[...]
        gate, up = _dot(xn, wg_ref[...]), _dot(xn, wu_ref[...])
        gate_ref[...] = gate.astype(BF16)
        up_ref[...] = up.astype(BF16)
        a_ref[...] = (gate * jax.nn.sigmoid(gate) * up).astype(BF16)

    chunk = pl.BlockSpec((None, tm, fc), lambda i, c: (c, i, 0))
    return _call(
        body, name=name, grid=(t // tm, 4),
        in_specs=[pl.BlockSpec((tm, d), lambda i, c: (i, 0)), pl.BlockSpec((1, d), lambda i, c: (0, 0)),
                  pl.BlockSpec((None, None, d, fc), lambda i, c: (c, layer, 0, 0)),
                  pl.BlockSpec((None, None, d, fc), lambda i, c: (c + 4, layer, 0, 0))],
        out_specs=[chunk, chunk, chunk, pl.BlockSpec((tm, d), lambda i, c: (i, 0))],
        out_shape=[_sds((4, t, fc), BF16)] * 3 + [_sds((t, d), BF16)])(x, g, wg, wg)


def _norm_matmul_heads(x, g, wg, layer, scale, name):
    t, d = x.shape
    tm = ROW_TILE
    hp = d // LANES

    def body(x_ref, g_ref, w_ref, y_ref, xn_ref):
        xn = _rms(x_ref[...], g_ref[...]).astype(BF16)
        xn_ref[...] = xn
        y = _dot(xn, w_ref[...].reshape(d, d)) * scale
        for p in range(hp):
            y_ref[p] = y[:, p * LANES:(p + 1) * LANES]

    return _call(
        body, name=name, grid=(t // tm,),
        in_specs=[pl.BlockSpec((tm, d), lambda i: (i, 0)), pl.BlockSpec((1, d), lambda i: (0, 0)),
                  pl.BlockSpec((N_DEV, None, d // N_DEV, d), lambda i: (0, layer, 0, 0))],
        out_specs=[pl.BlockSpec((hp, tm, LANES), lambda i: (0, i, 0)), pl.BlockSpec((tm, d), lambda i: (i, 0))],
        out_shape=[_sds((hp, t, LANES), F32), _sds((t, d), BF16)])(x, g, wg)


def _shift_down(u, halo, k, tm):
    row = lax.broadcasted_iota(jnp.int32, u.shape, 0)
    out = pltpu.roll(u, k, 0)
    for j in range(k):
        out = jnp.where(row == j, halo[halo.shape[0] - k + j:halo.shape[0] - k + j + 1, :], out)
    return out


def _shift_up(u, halo, k, tm):
    row = lax.broadcasted_iota(jnp.int32, u.shape, 0)
    out = pltpu.roll(u, tm - k, 0)
    for j in range(k):
        out = jnp.where(row == tm - k + j, halo[j:j + 1, :], out)
    return out


HALO = 16


def _conv_fwd(p, cw, name):
    t, d3 = p.shape
    d = d3 // 3
    tm = ROW_TILE
    hb = tm // HALO

    def body(p_ref, prev_ref, cw_ref, z_ref):
        i = pl.program_id(0)
        b = p_ref[:, 0:d].astype(F32)
        u = p_ref[:, d:2 * d].astype(F32) * p_ref[:, 2 * d:3 * d].astype(F32)
        keep = (i > 0).astype(F32)
        hu = prev_ref[:, d:2 * d].astype(F32) * prev_ref[:, 2 * d:3 * d].astype(F32) * keep
        uc = cw_ref[2:3, :] * u + cw_ref[1:2, :] * _shift_down(u, hu, 1, tm) + cw_ref[0:1, :] * _shift_down(u, hu, 2, tm)
        z_ref[...] = (b * uc).astype(BF16)

    return _call(
        body, name=name, grid=(t // tm,),
        in_specs=[pl.BlockSpec((tm, d3), lambda i: (i, 0)),
                  pl.BlockSpec((HALO, d3), lambda i: (jnp.maximum(i * hb - 1, 0), 0)),
                  pl.BlockSpec((3, d), lambda i: (0, 0))],
        out_specs=pl.BlockSpec((tm, d), lambda i: (i, 0)), out_shape=_sds((t, d), BF16))(p, p, cw)


def _matmul_norm_residual(a3, wg, layer, g, x_res, name):
    kc_n, t, kc = a3.shape
    d = wg.shape[-1]
    per = N_DEV // kc_n
    rows = wg.shape[2]
    tm = BIG_ROW_TILE

    def body(a_ref, w_ref, g_ref, x_ref, raw_ref, xo_ref, acc_ref):
        c = pl.program_id(1)
        part = _dot(a_ref[...], w_ref[...].reshape(per * rows, d))

        @pl.when(c == 0)
        def _():
            acc_ref[...] = part

        @pl.when(c > 0)
        def _():
            acc_ref[...] += part

        @pl.when(c == kc_n - 1)
        def _():
            raw = acc_ref[...]
            raw_ref[...] = raw
            xo_ref[...] = x_ref[...] + _rms(raw, g_ref[...])

    row_spec = pl.BlockSpec((tm, d), lambda i, c: (i, 0))
    return _call(
        body, name=name, grid=(t // tm, kc_n),
        in_specs=[pl.BlockSpec((None, tm, kc), lambda i, c: (c, i, 0)),
                  pl.BlockSpec((per, None, rows, d), lambda i, c: (c, layer, 0, 0)),
                  pl.BlockSpec((1, d), lambda i, c: (0, 0)), row_spec],
        out_specs=[row_spec, row_spec], out_shape=[_sds((t, d), F32)] * 2,
        scratch_shapes=[pltpu.VMEM((tm, d), F32)])(a3, wg, g, x_res)


def _alibi_slopes(n_heads):
    hh = np.arange(n_heads, dtype=np.float32) + 1.0
    s = np.power(2.0, -8.0 * hh / n_heads).astype(np.float32)
    return jnp.asarray(np.repeat(s.reshape(n_heads // 2, 2, 1), 2 * ATT_BLOCK, axis=2))


def _band_bias(sl_ref, dil):
    u = lax.broadcasted_iota(jnp.int32, (ATT_BLOCK, 2 * ATT_BLOCK), 0)
    kk = lax.broadcasted_iota(jnp.int32, (ATT_BLOCK, 2 * ATT_BLOCK), 1)
    delta = u + ATT_BLOCK - kk
    valid = (delta >= 0) & (delta <= ATT_BLOCK)
    dist = (delta * dil).astype(F32)
    rows = [jnp.where(valid, -sl_ref[hd:hd + 1, :] * dist, NEG) for hd in range(2)]
    return jnp.concatenate(rows, axis=0)


def _stack_heads(a):
    lane = lax.broadcasted_iota(jnp.int32, a.shape, 1)
    return jnp.concatenate([jnp.where(lane < HEAD_DIM, a, 0.0), jnp.where(lane >= HEAD_DIM, a, 0.0)], axis=0).astype(BF16)


def _unstack_heads(a2):
    top, bot = a2[:ATT_BLOCK], a2[ATT_BLOCK:]
    lane = lax.broadcasted_iota(jnp.int32, top.shape, 1)
    return jnp.where(lane < HEAD_DIM, top, bot)


def _stack_cols(a):
    return jnp.concatenate([a[:, 0:1], a[:, HEAD_DIM:HEAD_DIM + 1]], axis=0)


def _fill_bias(sl_ref, bias_ref):
    kk = lax.broadcasted_iota(jnp.int32, (2 * ATT_BLOCK, 2 * ATT_BLOCK), 1)
    for gi, dil in enumerate(DILATIONS):
        bias = _band_bias(sl_ref, dil)
        bias_ref[2 * gi] = bias
        bias_ref[2 * gi + 1] = jnp.where(kk < ATT_BLOCK, NEG, bias)


def _attention_fwd(q, kv, slopes, name):
    hp, t, _ = q.shape
    ns = t // SUPER
    nd = len(DILATIONS)

    def body(sl_ref, q_ref, kc_ref, kp_ref, vc_ref, vp_ref, o_ref, lse_ref, kw_ref, vw_ref, og_ref, lg_ref, bias_ref):
        n = pl.program_id(1)
        kw_ref[0:SUPER, :] = kp_ref[...]
        kw_ref[SUPER:, :] = kc_ref[...]
        vw_ref[0:SUPER, :] = vp_ref[...]
        vw_ref[SUPER:, :] = vc_ref[...]

        @pl.when(n == 0)
        def _():
            _fill_bias(sl_ref, bias_ref)

        for gi, dil in enumerate(DILATIONS):

            def block(idx, carry, gi=gi, dil=dil):
                r, b = idx % dil, idx // dil
                qs = b * (ATT_BLOCK * dil) + r
                ks = SUPER + (b - 1) * (ATT_BLOCK * dil) + r
                first = jnp.logical_and(n == 0, b == 0).astype(jnp.int32)
                q2 = _stack_heads(q_ref[pl.ds(qs, ATT_BLOCK, stride=dil), :])
                kb = kw_ref[pl.ds(ks, 2 * ATT_BLOCK, stride=dil), :].astype(BF16)
                vb = vw_ref[pl.ds(ks, 2 * ATT_BLOCK, stride=dil), :].astype(BF16)
                s = _dot_nt(q2, kb) + bias_ref[2 * gi + first]
                m = jnp.max(s, axis=-1, keepdims=True)
                p = jnp.exp(s - m)
                l = jnp.sum(p, axis=-1, keepdims=True)
                o2 = _dot(p.astype(BF16), vb) / l
                lse2 = jnp.broadcast_to(m + jnp.log(l), (2 * ATT_BLOCK, LANES))
                og_ref[gi, pl.ds(qs, ATT_BLOCK, stride=dil), :] = _unstack_heads(o2)
                lg_ref[gi, pl.ds(qs, ATT_BLOCK, stride=dil), :] = _unstack_heads(lse2)
                return carry

            lax.fori_loop(0, SUPER // ATT_BLOCK, block, 0, unroll=ATT_UNROLL_FWD)
        lg =[lg_ref[gi] for gi in range(nd)]
        top = functools.reduce(jnp.maximum, lg)
        ws = [jnp.exp(x - top) for x in lg]
        tot = functools.reduce(jnp.add, ws)
        lse_ref[...] = top + jnp.log(tot)
        acc = ws[0] * og_ref[0]
        for gi in range(1, nd):
            acc = acc + ws[gi] * og_ref[gi]
        o_ref[...] = (acc / tot).astype(BF16)

    cur = lambda off: pl.BlockSpec((None, SUPER, LANES), lambda h, n: (h + off, n, 0))
    prev = lambda off: pl.BlockSpec((None, SUPER, LANES), lambda h, n: (h + off, jnp.maximum(n - 1, 0), 0))
    return _call(
        body, name=name, grid=(hp, ns),
        in_specs=[pl.BlockSpec((None, 2, 2 * ATT_BLOCK), lambda h, n: (h, 0, 0)), cur(0), cur(0), prev(0), cur(hp), prev(hp)],
        out_specs=[pl.BlockSpec((SUPER, LANES), lambda h, n: (n, h)), cur(0)],
        out_shape=[_sds((t, hp * LANES), BF16), _sds((hp, t, LANES), F32)],
        scratch_shapes=[pltpu.VMEM((2 * SUPER, LANES), F32), pltpu.VMEM((2 * SUPER, LANES), F32),
                        pltpu.VMEM((nd, SUPER, LANES), F32), pltpu.VMEM((nd, SUPER, LANES), F32),
                        pltpu.VMEM((2 * nd, 2 * ATT_BLOCK, 2 * ATT_BLOCK), F32)],
    )(slopes, q, kv, kv, kv, kv)


def _attention_bwd(q, kv, o, lse, d_o, slopes, q_scale, name):
    hp, t, _ = q.shape
    ns = t // SUPER

    def body(sl_ref, q_ref, kc_ref, kp_ref, vc_ref, vp_ref, o_ref, lse_ref, do_ref,
             dq_ref, dk_ref, dv_ref, kw_ref, vw_ref, dkw_ref, dvw_ref, dd_ref, bias_ref):
        n = pl.program_id(1)

        @pl.when(n == 0)
        def _():
            dkw_ref[...] = jnp.zeros_like(dkw_ref)
            dvw_ref[...] = jnp.zeros_like(dvw_ref)

        @pl.when(n > 0)
        def _():
            dkw_ref[0:SUPER, :] = dkw_ref[SUPER:, :]
            dvw_ref[0:SUPER, :] = dvw_ref[SUPER:, :]
            dkw_ref[SUPER:, :] = jnp.zeros((SUPER, LANES), F32)
            dvw_ref[SUPER:, :] = jnp.zeros((SUPER, LANES), F32)

        @pl.when(n < ns)
        def _():
            kw_ref[0:SUPER, :] = kp_ref[...]
            kw_ref[SUPER:, :] = kc_ref[...]
            vw_ref[0:SUPER, :] = vp_ref[...]
            vw_ref[SUPER:, :] = vc_ref[...]
            prod = do_ref[...] * o_ref[...].astype(F32)
            lane = lax.broadcasted_iota(jnp.int32, prod.shape, 1)
            d0 = jnp.sum(jnp.where(lane < HEAD_DIM, prod, 0.0), axis=-1, keepdims=True)
            d1 = jnp.sum(jnp.where(lane >= HEAD_DIM, prod, 0.0), axis=-1, keepdims=True)
            dd_ref[...] = jnp.where(lane < HEAD_DIM, d0, d1)
            dq_ref[...] = jnp.zeros_like(dq_ref)

            @pl.when(n == 0)
            def _():
                _fill_bias(sl_ref, bias_ref)

            for gi, dil in enumerate(DILATIONS):

                def block(idx, carry, gi=gi, dil=dil):
                    r, b = idx % dil, idx // dil
                    qs = b * (ATT_BLOCK * dil) + r
                    ks = SUPER + (b - 1) * (ATT_BLOCK * dil) + r
                    first = jnp.logical_and(n == 0, b == 0).astype(jnp.int32)
                    rows = pl.ds(qs, ATT_BLOCK, stride=dil)
                    keys = pl.ds(ks, 2 * ATT_BLOCK, stride=dil)
                    q2 = _stack_heads(q_ref[rows, :])
                    do2 = _stack_heads(do_ref[rows, :])
                    kb = kw_ref[keys, :].astype(BF16)
                    vb = vw_ref[keys, :].astype(BF16)
                    s = _dot_nt(q2, kb) + bias_ref[2 * gi + first]
                    p = jnp.exp(s - _stack_cols(lse_ref[rows, :]))
                    dp = _dot_nt(do2, vb)
                    ds = (p * (dp - _stack_cols(dd_ref[rows, :]))).astype(BF16)
                    dq_ref[rows, :] += _unstack_heads(_dot(ds, kb)) * q_scale
                    dkw_ref[keys, :] += _dot_tn(ds, q2)
                    dvw_ref[keys, :] += _dot_tn(p.astype(BF16), do2)
                    return carry

                lax.fori_loop(0, SUPER // ATT_BLOCK, block, 0, unroll=ATT_UNROLL_BWD)

        dk_ref[...] = dkw_ref[0:SUPER, :]
        dv_ref[...] = dvw_ref[0:SUPER, :]

    last = ns - 1
    cur = lambda off: pl.BlockSpec((None, SUPER, LANES), lambda h, n: (h + off, jnp.minimum(n, last), 0))
    prev = lambda off: pl.BlockSpec((None, SUPER, LANES), lambda h, n: (h + off, jnp.clip(n - 1, 0, last), 0))
    nat = pl.BlockSpec((SUPER, LANES), lambda h, n: (jnp.minimum(n, last), h))
    late = pl.BlockSpec((None, SUPER, LANES), lambda h, n: (h, jnp.maximum(n - 1, 0), 0))
    dq, dk, dv = _call(
        body, name=name, grid=(hp, ns + 1),
        in_specs=[pl.BlockSpec((None, 2, 2 * ATT_BLOCK), lambda h, n: (h, 0, 0)), cur(0), cur(0), prev(0), cur(hp), prev(hp),
                  nat, cur(0), nat],
        out_specs=[cur(0), late, late],
        out_shape=[_sds((hp, t, LANES), F32)] * 3,
        scratch_shapes=[pltpu.VMEM((2 * SUPER, LANES), F32)] * 4 + [
            pltpu.VMEM((SUPER, LANES), F32), pltpu.VMEM((2 * len(DILATIONS), 2 * ATT_BLOCK, 2 * ATT_BLOCK), F32)],
    )(slopes, q, kv, kv, kv, kv, o, lse, d_o)
    return dq, dk, dv


def _loss_head(y, target, raw, g, name):
    t, d = y.shape
    tm = ROW_TILE

    def body(y_ref, t_ref, raw_ref, g_ref, sq_ref, dy_ref, draw_ref, dg_ref):
        i = pl.program_id(0)
        err = y_ref[...] - t_ref[...]
        dy = err * (1.0 / d)
        dy_ref[...] = dy
        draw, dg = _rms_bwd(raw_ref[...], g_ref[...], dy)
        draw_ref[...] = draw.astype(BF16)
        sq = jnp.zeros((8, LANES), F32) + jnp.sum(err * err)

        @pl.when(i == 0)
        def _():
            sq_ref[...] = sq
            dg_ref[...] = dg

        @pl.when(i > 0)
        def _():
            sq_ref[...] += sq
            dg_ref[...] += dg

    row = pl.BlockSpec((tm, d), lambda i: (i, 0))
    vec = pl.BlockSpec((1, d), lambda i: (0, 0))
    return _call(
        body, name=name, grid=(t // tm,), in_specs=[row, row, row, vec],
        out_specs=[pl.BlockSpec((8, LANES), lambda i: (0, 0)), row, row, vec],
        out_shape=[_sds((8, LANES), F32), _sds((t, d), F32), _sds((t, d), BF16), _sds((1, d), F32)])(y, target, raw, g)


def _bwd_matmul_norms(a_specs, a_args, a_tile, n_steps, w_spec, w_arg, w_mat, xa, ga, resid, xb, gb, name):
    t, d = xa.shape
    tm = BWD_ROW_TILE
    na = len(a_specs)
    second = xb is not None
    per = 2 if n_steps % 2 == 0 else 1
    n_steps //= per

    def blocks_of(spec, k):
        return pl.BlockSpec(spec.block_shape, lambda i, j: spec.index_map(i, per * j + k))

    def body(*refs):
        a_refs, w_refs = refs[:per * na], refs[per * na:per * na + per]
        xa_ref, ga_ref, res_ref = refs[per * na + per:per * na + per + 3]
        rest = refs[per * na + per + 3:]
        if second:
            xb_ref, gb_ref, dx_ref, d2_ref, dga_ref, dgb_ref, acc_ref = rest
        else:
            dx_ref, dga_ref, acc_ref = rest
        i, j = pl.program_id(0), pl.program_id(1)
        part = None
        for k in range(per):
            term = _dot_nt(a_tile(per * j + k, *a_refs[k * na:(k + 1) * na]), w_mat(w_refs[k]))
            part = term if part is None else part + term

        @pl.when(j == 0)
        def _():
            acc_ref[...] = part

        @pl.when(j > 0)
        def _():
            acc_ref[...] += part

        @pl.when(j == n_steps - 1)
        def _():
            da, dga = _rms_bwd(xa_ref[...], ga_ref[...], acc_ref[...])
            dx = res_ref[...] + da
            dx_ref[...] = dx
            if second:
                d2, dgb = _rms_bwd(xb_ref[...], gb_ref[...], dx)
                d2_ref[...] = d2.astype(BF16)

            @pl.when(i == 0)
            def _():
                dga_ref[...] = dga
                if second:
                    dgb_ref[...] = dgb

            @pl.when(i > 0)
            def _():
                dga_ref[...] += dga
                if second:
                    dgb_ref[...] += dgb

    row = pl.BlockSpec((tm, d), lambda i, j: (i, 0))
    vec = pl.BlockSpec((1, d), lambda i, j: (0, 0))
    in_specs = [blocks_of(sp, k) for k in range(per) for sp in a_specs] + [blocks_of(w_spec, k) for k in range(per)]
    in_specs += [row, vec, row]
    args = list(a_args) * per + [w_arg] * per + [xa, ga, resid]
    if second:
        in_specs += [row, vec]
        args += [xb, gb]
        out_specs = [row, row, vec, vec]
        out_shape = [_sds((t, d), F32), _sds((t, d), BF16), _sds((1, d), F32), _sds((1, d), F32)]
    else:
        out_specs = [row, vec]
        out_shape = [_sds((t, d), F32), _sds((1, d), F32)]
    return _call(body, name=name, grid=(t // tm, n_steps), in_specs=in_specs, out_specs=out_specs,
                 out_shape=out_shape, scratch_shapes=[pltpu.VMEM((tm, d), F32)])(*args)


def _heads_to_rows(*refs):
    hp = refs[0].shape[0]
    cols = []
    for p in range(hp):
        v = refs[0][p]
        for r in refs[1:]:
            v = v + r[p]
        cols.append(v)
    return jnp.concatenate(cols, axis=-1).astype(BF16)


def _matmul_nt_rows(a, wg, layer, out_dtype, name):
    t, d = a.shape
    tm = ROW_TILE

    def body(a_ref, w_ref, o_ref):
        o_ref[...] = _dot_nt(a_ref[...], w_ref[...].reshape(d, d)).astype(out_dtype)

    row = pl.BlockSpec((tm, d), lambda i: (i, 0))
    return _call(body, name=name, grid=(t // tm,),
                 in_specs=[row, pl.BlockSpec((N_DEV, None, d // N_DEV, d), lambda i: (0, layer, 0, 0))],
                 out_specs=row, out_shape=_sds((t, d), out_dtype))(a, wg)


def _swiglu_bwd(d_ff, wg, layer, gate, up, name):
    t, d = d_ff.shape
    fc = gate.shape[-1]
    rows = wg.shape[2]
    tm = BIG_ROW_TILE

    def body(df_ref, w_ref, g_ref, u_ref, dh_ref):
        da = _dot_nt(df_ref[...], w_ref[...].reshape(2 * rows, d))
        gate, up = g_ref[...].astype(F32), u_ref[...].astype(F32)
        sig = jax.nn.sigmoid(gate)
        dh_ref[0] = (da * up * (sig * (1.0 + gate * (1.0 - sig)))).astype(BF16)
        dh_ref[1] = (da * (gate * sig)).astype(BF16)

    return _call(
        body, name=name, grid=(t // tm, 4),
        in_specs=[pl.BlockSpec((tm, d), lambda i, c: (i, 0)),
                  pl.BlockSpec((2, None, rows, d), lambda i, c: (c, layer, 0, 0)),
                  pl.BlockSpec((None, tm, fc), lambda i, c: (c, i, 0)),
                  pl.BlockSpec((None, tm, fc), lambda i, c: (c, i, 0))],
        out_specs=pl.BlockSpec((None, 2, tm, fc), lambda i, c: (c, 0, i, 0)),
        out_shape=_sds((4, 2, t, fc), BF16))(d_ff, wg, gate, up)


def _conv_bwd(p, d_z, cw, name):
    t, d3 = p.shape
    d = d3 // 3
    tm = ROW_TILE
    hb = tm // HALO
    nt = t // tm

    def body(p_ref, prev_ref, next_ref, dz_ref, dzn_ref, cw_ref, dp_ref, dcw_ref):
        i = pl.program_id(0)
        b = p_ref[:, 0:d].astype(F32)
        c = p_ref[:, d:2 * d].astype(F32)
        h = p_ref[:, 2 * d:3 * d].astype(F32)
        u = c * h
        hu = prev_ref[:, d:2 * d].astype(F32) * prev_ref[:, 2 * d:3 * d].astype(F32) * (i > 0).astype(F32)
        u1, u2 = _shift_down(u, hu, 1, tm), _shift_down(u, hu, 2, tm)
        uc = cw_ref[2:3, :] * u + cw_ref[1:2, :] * u1 + cw_ref[0:1, :] * u2
        dz = dz_ref[...].astype(F32)
        duc = dz * b
        dn = dzn_ref[...].astype(F32) * next_ref[:, 0:d].astype(F32) * (i < nt - 1).astype(F32)
        du = cw_ref[2:3, :] * duc + cw_ref[1:2, :] * _shift_up(duc, dn, 1, tm) + cw_ref[0:1, :] * _shift_up(duc, dn, 2, tm)
        dp_ref[:, 0:d] = (dz * uc).astype(BF16)
        dp_ref[:, d:2 * d] = (du * h).astype(BF16)
        dp_ref[:, 2 * d:3 * d] = (du * c).astype(BF16)
        dcw = jnp.concatenate([jnp.sum(duc * u2, axis=0, keepdims=True), jnp.sum(duc * u1, axis=0, keepdims=True),
                               jnp.sum(duc * u, axis=0, keepdims=True)], axis=0)

        @pl.when(i == 0)
        def _():
            dcw_ref[...] = dcw

        @pl.when(i > 0)
        def _():
            dcw_ref[...] += dcw

    last_halo = t // HALO - 1
    return _call(
        body, name=name, grid=(nt,),
        in_specs=[pl.BlockSpec((tm, d3), lambda i: (i, 0)),
                  pl.BlockSpec((HALO, d3), lambda i: (jnp.maximum(i * hb - 1, 0), 0)),
                  pl.BlockSpec((HALO, d3), lambda i: (jnp.minimum((i + 1) * hb, last_halo), 0)),
                  pl.BlockSpec((tm, d), lambda i: (i, 0)),
                  pl.BlockSpec((HALO, d), lambda i: (jnp.minimum((i + 1) * hb, last_halo), 0)),
                  pl.BlockSpec((3, d), lambda i: (0, 0))],
        out_specs=[pl.BlockSpec((tm, d3), lambda i: (i, 0)), pl.BlockSpec((3, d), lambda i: (0, 0))],
        out_shape=[_sds((t, d3), BF16), _sds((3, d), F32)])(p, p, p, d_z, d_z, cw)


def _grad_weight(a_specs, a_args, a_tile, b_specs, b_args, b_tile, n_out, acc_shape, out_spec, out_shape, t, name):
    tt = BIG_ROW_TILE
    na, nb = len(a_specs), len(b_specs)

    def body(*refs):
        a_refs, b_refs = refs[:na], refs[na:na + nb]
        o_ref, acc_ref = refs[na + nb:]
        s = pl.program_id(1)
        part = _dot_tn(a_tile(pl.program_id(0), *a_refs), b_tile(pl.program_id(0), *b_refs))

        @pl.when(s == 0)
        def _():
            acc_ref[...] = part

        @pl.when(s > 0)
        def _():
            acc_ref[...] += part

        @pl.when(s == t // tt - 1)
        def _():
            o_ref[...] = acc_ref[...].astype(BF16).reshape(o_ref.shape)

    return _call(body, name=name, grid=(n_out, t // tt), in_specs=list(a_specs) + list(b_specs), out_specs=out_spec,
                 out_shape=out_shape, scratch_shapes=[pltpu.VMEM(acc_shape, F32)])(*a_args, *b_args)


def _ident(*args):
    return args[-1][...]


def _heads_tile(j, *refs):
    return _heads_to_rows(*refs)


def kernel(x, norm_g, conv_in_w, conv_w, conv_out_w, kv_norm_g, kv_w, q_w, o_w, ffn_in_w, ffn_out_w, loss_target, m_norm_g, m_conv_in_w, m_conv_w, m_conv_out_w, m_kv_norm_g, m_kv_w, m_q_w, m_o_w, m_ffn_in_w, m_ffn_out_w, v_norm_g, v_conv_in_w, v_conv_w, v_conv_out_w, v_kv_norm_g, v_kv_w, v_q_w, v_o_w, v_ffn_in_w, v_ffn_out_w):
    x0 = x[0]
    target = loss_target[0]
    t, d = x0.shape
    depth = norm_g.shape[0]
    n_a = conv_in_w.shape[0]
    n_b = q_w.shape[0]
    hp = d // LANES
    tm, tg = BWD_ROW_TILE, BIG_ROW_TILE
    assert t % SUPER == 0 and d % LANES == 0 and depth == n_a + n_b
    dev = 4 * lax.axis_index("x") + 2 * lax.axis_index("y") + lax.axis_index("c")

    n_small = 4 * depth + 3 * n_a
    small_rows = -(-(n_small + 1) // 8) * 8
    small_local = jnp.concatenate([norm_g.reshape(4 * depth, -1), conv_w.reshape(3 * n_a, -1),
                                   jnp.zeros((small_rows - n_small, norm_g.shape[-1]), F32)], axis=0)
    big = {"conv_in_w": conv_in_w, "conv_out_w": conv_out_w, "kv_w": kv_w[None], "q_w": q_w, "o_w": o_w,
           "ffn_in_w": ffn_in_w, "ffn_out_w": ffn_out_w}
    names = list(big)

    def group(layer):
        if layer < n_a:
            return [("conv_in_w", layer), ("conv_out_w", layer), ("ffn_in_w", layer), ("ffn_out_w", layer)]
        j = layer - n_a
        return ([("kv_w", 0)] if j == 0 else []) + [("q_w", j), ("o_w", j), ("ffn_in_w", layer), ("ffn_out_w", layer)]

    slot = dev.astype(jnp.int32).reshape(1)
    first = _all_gather([small_local] + [_cast_layer(big[k], i, None, f"cast_{k}_{i}") for k, i in group(0)], "gather_weights")
    small_all = first[0].transpose(1, 0, 2).reshape(small_rows, d)
    wl = {key: a[:, None] for key, a in zip(group(0), first[1:])}
    def gather_start(layer):
        lands = [_cast_layer(big[k], i, slot, f"cast_{k}_{i}") for k, i in group(layer)]
        send_sems, recv_sems, _, lands, tok = _send_start([], lands, f"gather_start_l{layer}")
        return (send_sems, recv_sems, lands), tok[0, 0]

    in_flight, token = gather_start(1)
    W = lambda k, i: (wl[(k, i)], 0)
    gain = lambda layer, k: small_all[4 * layer + k][None]
    taps = lambda layer: small_all[4 * depth + 3 * layer: 4 * depth + 3 * layer + 3]
    g_kv = kv_norm_g[None]
    slopes = _alibi_slopes(d // HEAD_DIM)
    fc = big["ffn_in_w"].shape[-1]
    cb = big["conv_in_w"].shape[-1]
    kvb = big["kv_w"].shape[-1]
    q_scale = HEAD_DIM ** -0.5

    saved = []
    kv = kvn = None
    xs = x0
    for layer in range(depth):
        tag = f"_l{layer}"
        if layer > 0:
            send_sems, recv_sems, lands = in_flight
            _, lands = _send_wait(send_sems, recv_sems, [], lands, xs, f"gather_wait_l{layer}")
            wl.update({key: a[:, None] for key, a in zip(group(layer), lands)})
            if layer + 1 < depth:
                in_flight, token = gather_start(layer + 1)
        s = {"x_in": xs}
        g0 = gain(layer, 0) + token if layer + 1 < depth else gain(layer, 0)
        if layer < n_a:
            s["p"], s["xn"] = _norm_matmul_cols(xs, g0, *W("conv_in_w", layer), "cols", "conv_in" + tag)
            s["z"] = _conv_fwd(s["p"], taps(layer), "conv" + tag)
            s["mix"], x_mid = _matmul_norm_residual(s["z"][None], *W("conv_out_w", layer), gain(layer, 1), xs, "conv_out" + tag)
        else:
            j = layer - n_a
            if kv is None:
                kv, kvn = _norm_matmul_cols(xs, g_kv, *W("kv_w", 0), "heads", "kv_proj")
            s["q"], s["xn"] = _norm_matmul_heads(xs, g0, *W("q_w", j), q_scale, "q_proj" + tag)
            s["o"], s["lse"] = _attention_fwd(s["q"], kv, slopes, "attention" + tag)
            s["mix"], x_mid = _matmul_norm_residual(s["o"][None], *W("o_w", j), gain(layer, 1), xs, "o_proj" + tag)
        s["x_mid"] = x_mid
        s["gate"], s["up"], s["a"], s["fn"] = _ffn_in_swiglu(x_mid, gain(layer, 2), *W("ffn_in_w", layer), "ffn_in" + tag)
        s["ff"], xs = _matmul_norm_residual(s["a"], *W("ffn_out_w", layer), gain(layer, 3), x_mid, "ffn_out" + tag)
        saved.append(s)

    last = saved[-1]
    sq, dx_out, d_ff, dg3 = _loss_head(xs, target, last["ff"], gain(depth - 1, 3), "loss_head")
    loss = lax.psum(sq[0, 0] * (0.5 / d), ("x", "y", "c"))

    dgain = {(depth - 1, 3): dg3}
    dtaps = {}
    grads = {k: [None] * big[k].shape[0] for k in names}
    dkv_parts = []
    scattering = []

    def scatter_start(keys, tag):
        parts = [grads[k][i] for k, i in keys]
        zones = [lax.empty(p.shape, p.dtype) for p in parts]
        send_sems, recv_sems, parts, zones, tok = _send_start(parts, zones, "scatter_start" + tag)
        scattering.append((keys, tag, send_sems, recv_sems, parts, zones))
        return tok[0, 0]

    for layer in reversed(range(depth)):
        tag = f"_l{layer}"
        s = saved[layer]
        dh = _swiglu_bwd(d_ff, *W("ffn_out_w", layer), s["gate"], s["up"], "swiglu_bwd" + tag)
        rows_out = big["ffn_out_w"].shape[1]
        grads["ffn_out_w"][layer] = _grad_weight(
            [pl.BlockSpec((None, tg, fc), lambda c, i: (c, i, 0))], [s["a"]], _ident,
            [pl.BlockSpec((tg, d), lambda c, i: (i, 0))], [d_ff], _ident,
            4, (fc, d), pl.BlockSpec((2, rows_out, d), lambda c, i: (c, 0, 0)), _sds((N_DEV, rows_out, d), BF16), t,
            "grad_ffn_out" + tag)
        grads["ffn_in_w"][layer] = _grad_weight(
            [pl.BlockSpec((tg, d), lambda j, i: (i, 0))], [s["fn"]], _ident,
            [pl.BlockSpec((None, None, tg, fc), lambda j, i: (j % 4, j // 4, i, 0))], [dh], _ident,
            N_DEV, (d, fc), pl.BlockSpec((None, d, fc), lambda j, i: (j, 0, 0)), _sds((N_DEV, d, fc), BF16), t,
            "grad_ffn_in" + tag)
        tok = scatter_start([("ffn_in_w", layer), ("ffn_out_w", layer)], "_ffn" + tag)
        dx_mid, d_mix, dg2, dg1 = _bwd_matmul_norms(
            [pl.BlockSpec((None, None, tm, fc), lambda i, j: (j % 4, j // 4, i, 0))], [dh], _ident, N_DEV,
            pl.BlockSpec((None, None, d, fc), lambda i, j: (j, 0, 0, 0)), W("ffn_in_w", layer)[0], _ident,
            s["x_mid"], gain(layer, 2) + tok, dx_out, s["mix"], gain(layer, 1), "ffn_in_bwd" + tag)
        dgain[(layer, 2)], dgain[(layer, 1)] = dg2, dg1
        full_rows = pl.BlockSpec((N_DEV, d // N_DEV, d), lambda j, i: (0, 0, 0))
        rows_w = lambda wname, idx: (pl.BlockSpec((N_DEV, None, d // N_DEV, d), lambda i, j: (0, 0, 0, 0)), W(wname, idx)[0],
                                     lambda w_ref: w_ref[...].reshape(d, d))
        if layer < n_a:
            d_z = _matmul_nt_rows(d_mix, *W("conv_out_w", layer), BF16, "conv_out_bwd" + tag)
            grads["conv_out_w"][layer] = _grad_weight(
                [pl.BlockSpec((tg, d), lambda j, i: (i, 0))], [s["z"]], _ident,
                [pl.BlockSpec((tg, d), lambda j, i: (i, 0))], [d_mix], _ident,
                1, (d, d), full_rows, _sds((N_DEV, d // N_DEV, d), BF16), t, "grad_conv_out" + tag)
            d_p, dtaps[layer] = _conv_bwd(s["p"], d_z, taps(layer), "conv_bwd" + tag)
            grads["conv_in_w"][layer] = _grad_weight(
                [pl.BlockSpec((tg, d), lambda j, i: (i, 0))], [s["xn"]], _ident,
                [pl.BlockSpec((tg, cb), lambda j, i: (i, j))], [d_p], _ident,
                N_DEV, (d, cb), pl.BlockSpec((None, d, cb), lambda j, i: (j, 0, 0)), _sds((N_DEV, d, cb), BF16), t,
                "grad_conv_in" + tag)
            a_specs, a_args, a_tile, n_steps = [pl.BlockSpec((tm, cb), lambda i, j: (i, j))], [d_p], _ident, N_DEV
            w_spec = pl.BlockSpec((None, None, d, cb), lambda i, j: (j, 0, 0, 0))
            w_arg, w_mat = W("conv_in_w", layer)[0], _ident
            resid = dx_mid
        else:
            j_b = layer - n_a
            d_o = _matmul_nt_rows(d_mix, *W("o_w", j_b), F32, "o_proj_bwd" + tag)
            grads["o_w"][j_b] = _grad_weight(
                [pl.BlockSpec((tg, d), lambda j, i: (i, 0))], [s["o"]], _ident,
                [pl.BlockSpec((tg, d), lambda j, i: (i, 0))], [d_mix], _ident,
                1, (d, d), full_rows, _sds((N_DEV, d // N_DEV, d), BF16), t, "grad_o" + tag)
            dq, dk, dv = _attention_bwd(s["q"], kv, s["o"], s["lse"], d_o, slopes, q_scale, "attention_bwd" + tag)
            dkv_parts.append((dk, dv))
            heads_spec = pl.BlockSpec((hp, tg, LANES), lambda j, i: (0, i, 0))
            grads["q_w"][j_b] = _grad_weight(
                [pl.BlockSpec((tg, d), lambda j, i: (i, 0))], [s["xn"]], _ident,
                [heads_spec], [dq], _heads_tile,
                1, (d, d), full_rows, _sds((N_DEV, d // N_DEV, d), BF16), t, "grad_q" + tag)
            a_specs, a_args, a_tile, n_steps = [pl.BlockSpec((hp, tm, LANES), lambda i, j: (0, i, 0))], [dq], _heads_tile, 1
            w_spec, w_arg, w_mat = rows_w("q_w", j_b)
            resid = dx_mid
            if layer == n_a:
                pieces = kvb // LANES
                halves = []
                for src in (0, 1):
                    halves.append([part[src] for part in dkv_parts])
                n_half = len(dkv_parts)
                kv_args = [arr for src in (0, 1) for arr in halves[src]]

                def kv_block(src, j):
                    return jnp.where((j // 4) == src, j % 4, 0)

                def kv_tile(j, *refs):
                    keys = _heads_to_rows(*refs[:n_half])
                    vals = _heads_to_rows(*refs[n_half:])
                    return jnp.where(j < 4, keys, vals)

                kv_specs = [pl.BlockSpec((pieces, tm, LANES), functools.partial(lambda i, j, src: (kv_block(src, j), i, 0), src=src))
                            for src in (0, 1) for _ in range(n_half)]
                resid, dgain["kv"] = _bwd_matmul_norms(
                    kv_specs, kv_args, kv_tile, N_DEV,
                    pl.BlockSpec((None, None, d, kvb), lambda i, j: (j, 0, 0, 0)), W("kv_w", 0)[0], _ident,
                    s["x_in"], g_kv, dx_mid, None, None, "kv_proj_bwd")
                kv_b_specs = [pl.BlockSpec((pieces, tg, LANES), functools.partial(lambda j, i, src: (kv_block(src, j), i, 0), src=src))
                              for src in (0, 1) for _ in range(n_half)]
                grads["kv_w"][0] = _grad_weight(
                    [pl.BlockSpec((tg, d), lambda j, i: (i, 0))], [kvn], _ident,
                    kv_b_specs, kv_args, kv_tile,
                    N_DEV, (d, kvb), pl.BlockSpec((None, d, kvb), lambda j, i: (j, 0, 0)), _sds((N_DEV, d, kvb), BF16), t,
                    "grad_kv")
        tok = scatter_start([key for key in group(layer) if not key[0].startswith("ffn")], "_mix" + tag)
        if layer > 0:
            prev = saved[layer - 1]
            dx_out, d_ff, dg0, dg3p = _bwd_matmul_norms(
                a_specs, a_args, a_tile, n_steps, w_spec, w_arg, w_mat,
                s["x_in"], gain(layer, 0) + tok, resid, prev["ff"], gain(layer - 1, 3), "mixer_in_bwd" + tag)
            dgain[(layer, 0)], dgain[(layer - 1, 3)] = dg0, dg3p
        else:
            grad_x, dg0 = _bwd_matmul_norms(
                a_specs, a_args, a_tile, n_steps, w_spec, w_arg, w_mat,
                s["x_in"], gain(layer, 0), resid, None, None, "mixer_in_bwd" + tag)
            dgain[(layer, 0)] = dg0

    small_grad = jnp.concatenate(
        [dgain[(layer, k)] for layer in range(depth) for k in range(4)] + [dtaps[layer] for layer in range(n_a)]
        + [dgain["kv"]] + [jnp.zeros((small_rows - n_small - 1, d), F32)], axis=0)
    small_grads_all = _all_gather([small_grad], "gather_small_grads")[0]
    lo = dev * (d // N_DEV)

    def pack(ng, cwp, kvg):
        rows = jnp.concatenate([ng.reshape(4 * depth, -1), cwp.reshape(3 * n_a, -1)], axis=0)
        z = lax.dynamic_update_slice(jnp.zeros((small_rows, d), F32), rows, (0, lo))
        return lax.dynamic_update_slice(z, kvg[None], (n_small, 0))

    w_small = lax.dynamic_update_slice(small_all, g_kv, (n_small, 0))
    m_small, v_small = pack(m_norm_g, m_conv_w, m_kv_norm_g), pack(v_norm_g, v_conv_w, v_kv_norm_g)
    sm = _small_adamw(small_grads_all, w_small, m_small, v_small, "adamw_small")

    def unpack(a):
        mine = lax.dynamic_slice(a, (0, lo), (small_rows, d // N_DEV))
        return (mine[:4 * depth].reshape(norm_g.shape), mine[4 * depth:n_small].reshape(conv_w.shape), a[n_small])

    small_out = [unpack(a) for a in sm]

    moments = {"conv_in_w": (m_conv_in_w, v_conv_in_w), "conv_out_w": (m_conv_out_w, v_conv_out_w),
               "kv_w": (m_kv_w[None], v_kv_w[None]), "q_w": (m_q_w, v_q_w), "o_w": (m_o_w, v_o_w),
               "ffn_in_w": (m_ffn_in_w, v_ffn_in_w), "ffn_out_w": (m_ffn_out_w, v_ffn_out_w)}
    res = {k: [None] * big[k].shape[0] for k in names}
    for keys, tag, send_sems, recv_sems, parts, zones in scattering:
        parts, zones = _send_wait(send_sems, recv_sems, parts, zones, grad_x, "scatter_wait" + tag)
        for (k, i), part, zone in zip(keys, parts, zones):
            res[k][i] = _sum_adamw(slot, part, zone, big[k], moments[k][0], moments[k][1], i, f"adamw_{k}_{i}")

    def big_out(k, which):
        st = jnp.stack([res[k][i][which] for i in range(big[k].shape[0])], axis=0)
        return st[0] if k == "kv_w" else st

    out_names = ["norm_g", "conv_in_w", "conv_w", "conv_out_w", "kv_norm_g", "kv_w", "q_w", "o_w", "ffn_in_w", "ffn_out_w"]
    small_pos = {"norm_g": 0, "conv_w": 1, "kv_norm_g": 2}
    outs = [loss, grad_x[None]]
    for which in range(4):
        for k in out_names:
            outs.append(small_out[which][small_pos[k]] if k in small_pos else big_out(k, which))
    return tuple(outs)
```

```python
import functools
import math

import numpy as np
import jax
import jax.numpy as jnp
from jax import lax
from jax.experimental import pallas as pl
from jax.experimental.pallas import tpu as pltpu

F32 = jnp.float32
BF16 = jnp.bfloat16

N_DEV = 8
RMS_EPS = 1e-6
HEAD_DIM = 64
LANES = 128
ATT_BLOCK = 128
DILATIONS = (1, 4, 16)
SUPER = ATT_BLOCK * DILATIONS[-1]
NEG = -1e30
ATT_UNROLL_FWD = 8
ATT_UNROLL_BWD = 4

ADAM_LR, ADAM_B1, ADAM_B2, ADAM_EPS, ADAM_WD, ADAM_STEP = 0.001, 0.9, 0.999, 1e-08, 0.01, 10

ROW_TILE = 512
BIG_ROW_TILE = 1024
BWD_ROW_TILE = 512
MESH = pl.DeviceIdType.MESH


def _call(body, *, name, grid=None, in_specs=None, out_specs=None, out_shape=None, scratch_shapes=(), prefetch=False,
          **params):
    cp = pltpu.CompilerParams(**params) if params else None
    if prefetch:
        spec = pltpu.PrefetchScalarGridSpec(num_scalar_prefetch=1, grid=grid, in_specs=in_specs, out_specs=out_specs,
                                            scratch_shapes=list(scratch_shapes))
        return pl.pallas_call(body, name=name, grid_spec=spec, out_shape=out_shape, compiler_params=cp)
    kwargs = {k: v for k, v in (("grid", grid), ("in_specs", in_specs), ("out_specs", out_specs)) if v is not None}
    return pl.pallas_call(body, name=name, out_shape=out_shape, scratch_shapes=list(scratch_shapes),
                          compiler_params=cp, **kwargs)


def _sds(shape, dtype):
    return jax.ShapeDtypeStruct(tuple(shape), dtype)


def _rms(x, g):
    r = lax.rsqrt(jnp.mean(x * x, axis=-1, keepdims=True) + RMS_EPS)
    return x * r * g


def _rms_bwd(x, g, dy):
    r = lax.rsqrt(jnp.mean(x * x, axis=-1, keepdims=True) + RMS_EPS)
    xh = x * r
    dxh = dy * g
    dx = r * (dxh - xh * jnp.mean(dxh * xh, axis=-1, keepdims=True))
    return dx, jnp.sum(dy * xh, axis=0, keepdims=True)


def _dot(a, b):
    return jnp.dot(a, b, preferred_element_type=F32)


def _dot_nt(a, b):
    return lax.dot_general(a, b, (((1,), (1,)), ((), ())), preferred_element_type=F32)


def _dot_tn(a, b):
    return lax.dot_general(a, b, (((0,), (0,)), ((), ())), preferred_element_type=F32)


def _mesh_pos():
    return lax.axis_index("x"), lax.axis_index("y"), lax.axis_index("c")


def _all_gather(arrs, name):
    n = len(arrs)

    def body(*refs):
        ins, outs = refs[:n], refs[n:2 * n]
        send_sems, recv_sems, local_sems = refs[2 * n:]
        x, y, c = _mesh_pos()
        me, sibling = (x, y, c), (x, y, 1 - c)
        chips = [(1 - x, y), (x, 1 - y), (1 - x, 1 - y)]

        def copy(a, k, block, to, src=None):
            dst = outs[a].at[4 * block[0] + 2 * block[1] + block[2]]
            return pltpu.make_async_remote_copy(
                src_ref=dst if src is None else src, dst_ref=dst, send_sem=send_sems.at[a, k],
                recv_sem=recv_sems.at[a, k], device_id=to, device_id_type=MESH)

        started = []
        for a in range(n):
            mine = pltpu.make_async_copy(ins[a], outs[a].at[4 * x + 2 * y + c], local_sems.at[a])
            mine.start()
            started.append(mine)
        first = []
        for a in range(n):
            first.append(copy(a, 0, me, sibling, src=ins[a]))
            first += [copy(a, 1 + j, me, (*chip, c), src=ins[a]) for j, chip in enumerate(chips)]
        for cp in first:
            cp.start()
        passed = []
        for a in range(n):
            for j, chip in enumerate(chips):
                copy(a, 1 + j, (*chip, c), me).wait_recv()
                fwd = copy(a, 4 + j, (*chip, c), sibling)
                fwd.start()
                passed.append(fwd)
        for a in range(n):
            copy(a, 0, sibling, me).wait_recv()
            for j, chip in enumerate(chips):
                copy(a, 4 + j, (*chip, 1 - c), me).wait_recv()
        for cp in first + passed:
            cp.wait_send()
        for cp in started:
            cp.wait()

    any_spec = pl.BlockSpec(memory_space=pl.ANY)
    outs = _call(
        body, name=name, in_specs=[any_spec] * n, out_specs=[any_spec] * n,
        out_shape=[_sds((N_DEV,) + a.shape, a.dtype) for a in arrs],
        scratch_shapes=[pltpu.SemaphoreType.DMA((n, 7)), pltpu.SemaphoreType.DMA((n, 7)), pltpu.SemaphoreType.DMA((n,))],
        has_side_effects=True,
    )(*arrs)
    return list(outs)


HBM_SPEC = pl.BlockSpec(memory_space=pltpu.HBM)
SEM_SPEC = pl.BlockSpec(memory_space=pltpu.SEMAPHORE)
DATAFLOW = pltpu.SideEffectType.DATAFLOW_SIDE_EFFECTING
PEERS = [(dx, dy, dc) for dx in (0, 1) for dy in (0, 1) for dc in (0, 1)][1:]


def _peer(flip):
    x, y, c = _mesh_pos()
    return tuple(1 - v if f else v for v, f in zip((x, y, c), flip))


def _slot(pos):
    return 4 * pos[0] + 2 * pos[1] + pos[2]


def _in_hbm(a):
    return pltpu.with_memory_space_constraint(a, pltpu.HBM)


def _direct_copies(srcs, lands, send_sems, recv_sems, scatter):
    me = _slot(_mesh_pos())
    copies = []
    for a in range(len(lands)):
        for k, flip in enumerate(PEERS):
            peer = _peer(flip)
            src = srcs[a].at[_slot(peer)] if scatter else lands[a].at[me]
            idx = a * len(PEERS) + k
            copies.append(pltpu.make_async_remote_copy(
                src_ref=src, dst_ref=lands[a].at[me], send_sem=send_sems.at[idx], recv_sem=recv_sems.at[idx],
                device_id=peer, device_id_type=MESH))
    return copies


def _send_start(srcs, lands, after, name):
    ns, nl = len(srcs), len(lands)
    scatter = ns > 0

    def body(*refs):
        src_refs, land_refs = refs[:ns], refs[ns:ns + nl]
        send_sems, recv_sems = refs[ns + nl + 1:ns + nl + 3]
        token = refs[-1]
        for cp in _direct_copies(src_refs, land_refs, send_sems, recv_sems, scatter):
            cp.start()
        token[...] = jnp.zeros_like(token)

    sem = pltpu.SemaphoreType.DMA((nl * len(PEERS),))
    outs = pl.pallas_call(
        body, name=name,
        out_shape=(sem, sem) + tuple(pltpu.HBM(a.shape, a.dtype) for a in list(srcs) + list(lands))
        + (_sds((8, LANES), F32),),
        in_specs=[HBM_SPEC] * (ns + nl) + [pl.BlockSpec(memory_space=pl.ANY)],
        out_specs=(SEM_SPEC, SEM_SPEC) + (HBM_SPEC,) * (ns + nl) + (pl.BlockSpec(memory_space=pltpu.VMEM),),
        input_output_aliases={i: 2 + i for i in range(ns + nl)},
        compiler_params=pltpu.CompilerParams(has_side_effects=DATAFLOW),
    )(*[_in_hbm(a) for a in list(srcs) + list(lands)], after)
    send_sems, recv_sems = outs[0], outs[1]
    return send_sems, recv_sems, list(outs[2:2 + ns]), list(outs[2 + ns:2 + ns + nl]), outs[-1]


def _send_wait(send_sems, recv_sems, srcs, lands, after, name):
    ns, nl = len(srcs), len(lands)
    scatter = ns > 0

    def body(*refs):
        src_refs, land_refs = refs[:ns], refs[ns:ns + nl]
        send_sems, recv_sems = refs[ns + nl:ns + nl + 2]
        copies = _direct_copies(src_refs, land_refs, send_sems, recv_sems, scatter)
        for cp in copies:
            cp.wait_send()
        for cp in copies:
            cp.wait_recv()

    outs = pl.pallas_call(
        body, name=name,
        out_shape=tuple(pltpu.HBM(a.shape, a.dtype) for a in list(srcs) + list(lands)),
        in_specs=[HBM_SPEC] * (ns + nl) + [SEM_SPEC, SEM_SPEC, pl.BlockSpec(memory_space=pl.ANY)],
        out_specs=(HBM_SPEC,) * (ns + nl),
        input_output_aliases={i: i for i in range(ns + nl)},
        compiler_params=pltpu.CompilerParams(has_side_effects=DATAFLOW),
    )(*srcs, *lands, send_sems, recv_sems, after)
    return list(outs[:ns]), list(outs[ns:])


def _row_tile(rows, cap=512):
    t = min(rows, cap)
    while rows % t or (t % 16 and t != rows):
        t -= 1
    return t


def _as2d(a):
    return a.reshape(-1, a.shape[-1])


def _cast_layer(w, layer, slot, name):
    _, rows, cols = w.shape
    tr = _row_tile(rows)

    def body(*refs):
        refs[-1][...] = refs[-2][...].astype(BF16)

    if slot is None:
        return _call(body, name=name, grid=(rows // tr,),
                     in_specs=[pl.BlockSpec((None, tr, cols), lambda i: (layer, i, 0))],
                     out_specs=pl.BlockSpec((tr, cols), lambda i: (i, 0)), out_shape=_sds((rows, cols), BF16))(w)
    return _call(body, name=name, grid=(rows // tr,), prefetch=True,
                 in_specs=[pl.BlockSpec((None, tr, cols), lambda i, s: (layer, i, 0))],
                 out_specs=pl.BlockSpec((None, tr, cols), lambda i, s: (s[0], i, 0)),
                 out_shape=_sds((N_DEV, rows, cols), BF16))(slot, w)


def _sum_adamw(slot, part, land, w, m, v, layer, name):
    _, rows, cols = land.shape
    tr = _row_tile(rows, 256)

    def body(s_ref, p_ref, l_ref, w_ref, m_ref, v_ref, g_ref, d_ref, nm_ref, nv_ref):
        own = p_ref[...]
        g = jnp.zeros((tr, cols), F32)
        for k in range(N_DEV):
            g = g + jnp.where(s_ref[0] == k, own, l_ref[k]).astype(F32)
        delta, nm, nv = _adamw_math(w_ref[...], g, m_ref[...], v_ref[...])
        g_ref[...] = g
        d_ref[...] = delta
        nm_ref[...] = nm
        nv_ref[...] = nv

    lay = pl.BlockSpec((None, tr, cols), lambda i, s: (layer, i, 0))
    out = pl.BlockSpec((tr, cols), lambda i, s: (i, 0))
    return _call(body, name=name, grid=(rows // tr,), prefetch=True,
                 in_specs=[pl.BlockSpec((None, tr, cols), lambda i, s: (s[0], i, 0)),
                           pl.BlockSpec((N_DEV, tr, cols), lambda i, s: (0, i, 0)), lay, lay, lay],
                 out_specs=[out] * 4, out_shape=[_sds((rows, cols), F32)] * 4)(slot, part, land, w, m, v)


def _adamw_math(w, g, m, v):
    m = ADAM_B1 * m + (1.0 - ADAM_B1) * g
    v = ADAM_B2 * v + (1.0 - ADAM_B2) * (g * g)
    m_hat = m / (1.0 - ADAM_B1 ** ADAM_STEP)
    v_hat = v / (1.0 - ADAM_B2 ** ADAM_STEP)
    delta = -ADAM_LR * (m_hat / (jnp.sqrt(v_hat) + ADAM_EPS) + ADAM_WD * w)
    return delta, m, v


def _small_adamw(gathered, w, m, v, name):
    def body(a_ref, w_ref, m_ref, v_ref, g_ref, d_ref, nm_ref, nv_ref):
        g = a_ref[0]
        for k in range(1, N_DEV):
            g = g + a_ref[k]
        delta, nm, nv = _adamw_math(w_ref[...], g, m_ref[...], v_ref[...])
        g_ref[...] = g
        d_ref[...] = delta
        nm_ref[...] = nm
        nv_ref[...] = nv

    return _call(body, name=name, out_shape=[_sds(w.shape, F32)] * 4)(gathered, w, m, v)


def _norm_matmul_cols(x, g, wg, layer, mode, name):
    t, d = x.shape
    nb = wg.shape[-1]
    tm = BIG_ROW_TILE
    pieces = nb // LANES

    def body(x_ref, g_ref, w_ref, y_ref, xn_ref):
        @pl.when(pl.program_id(1) == 0)
        def _():
            xn_ref[...] = _rms(x_ref[...], g_ref[...]).astype(BF16)

        y = _dot(xn_ref[...], w_ref[...])
        if mode == "heads":
            for p in range(pieces):
                y_ref[p] = y[:, p * LANES:(p + 1) * LANES]
        else:
            y_ref[...] = y.astype(BF16)

    if mode == "cols":
        y_shape, y_spec = _sds((t, N_DEV * nb), BF16), pl.BlockSpec((tm, nb), lambda i, j: (i, j))
    else:
        y_shape = _sds((N_DEV * pieces, t, LANES), F32)
        y_spec = pl.BlockSpec((pieces, tm, LANES), lambda i, j: (j, i, 0))
    return _call(
        body, name=name, grid=(t // tm, N_DEV),
        in_specs=[pl.BlockSpec((tm, d), lambda i, j: (i, 0)), pl.BlockSpec((1, d), lambda i, j: (0, 0)),
                  pl.BlockSpec((None, None, d, nb), lambda i, j: (j, layer, 0, 0))],
        out_specs=[y_spec, pl.BlockSpec((tm, d), lambda i, j: (i, 0))],
        out_shape=[y_shape, _sds((t, d), BF16)])(x, g, wg)


def _ffn_in_swiglu(x, g, wg, layer, name):
    t, d = x.shape
    fc = wg.shape[-1]
    tm = BIG_ROW_TILE

    def body(x_ref, g_ref, wg_ref, wu_ref, gate_ref, up_ref, a_ref, xn_ref):
        @pl.when(pl.program_id(1) == 0)
        def _():
            xn_ref[...] = _rms(x_ref[...], g_ref[...]).astype(BF16)

        xn = xn_ref[...]
        gate, up = _dot(xn, wg_ref[...]), _dot(xn, wu_ref[...])
        gate_ref[...] = gate.astype(BF16)
        up_ref[...] = up.astype(BF16)
        a_ref[...] = (gate * jax.nn.sigmoid(gate) * up).astype(BF16)

    chunk = pl.BlockSpec((None, tm, fc), lambda i, c: (c, i, 0))
    return _call(
        body, name=name, grid=(t // tm, 4),
        in_specs=[pl.BlockSpec((tm, d), lambda i, c: (i, 0)), pl.BlockSpec((1, d), lambda i, c: (0, 0)),
                  pl.BlockSpec((None, None, d, fc), lambda i, c: (c, layer, 0, 0)),
                  pl.BlockSpec((None, None, d, fc), lambda i, c: (c + 4, layer, 0, 0))],
        out_specs=[chunk, chunk, chunk, pl.BlockSpec((tm, d), lambda i, c: (i, 0))],
        out_shape=[_sds((4, t, fc), BF16)] * 3 + [_sds((t, d), BF16)])(x, g, wg, wg)


def _norm_matmul_heads(x, g, wg, layer, scale, name):
    t, d = x.shape
    tm = ROW_TILE
    hp = d // LANES

    def body(x_ref, g_ref, w_ref, y_ref, xn_ref):
        xn = _rms(x_ref[...], g_ref[...]).astype(BF16)
        xn_ref[...] = xn
        y = _dot(xn, w_ref[...].reshape(d, d)) * scale
        for p in range(hp):
            y_ref[p] = y[:, p * LANES:(p + 1) * LANES]

    return _call(
        body, name=name, grid=(t // tm,),
        in_specs=[pl.BlockSpec((tm, d), lambda i: (i, 0)), pl.BlockSpec((1, d), lambda i: (0, 0)),
                  pl.BlockSpec((N_DEV, None, d // N_DEV, d), lambda i: (0, layer, 0, 0))],
        out_specs=[pl.BlockSpec((hp, tm, LANES), lambda i: (0, i, 0)), pl.BlockSpec((tm, d), lambda i: (i, 0))],
        out_shape=[_sds((hp, t, LANES), F32), _sds((t, d), BF16)])(x, g, wg)


def _shift_down(u, halo, k, tm):
    row = lax.broadcasted_iota(jnp.int32, u.shape, 0)
    out = pltpu.roll(u, k, 0)
    for j in range(k):
        out = jnp.where(row == j, halo[halo.shape[0] - k + j:halo.shape[0] - k + j + 1, :], out)
    return out


def _shift_up(u, halo, k, tm):
    row = lax.broadcasted_iota(jnp.int32, u.shape, 0)
    out = pltpu.roll(u, tm - k, 0)
    for j in range(k):
        out = jnp.where(row == tm - k + j, halo[j:j + 1, :], out)
    return out


HALO = 16


def _conv_fwd(p, cw, name):
    t, d3 = p.shape
    d = d3 // 3
    tm = ROW_TILE
    hb = tm // HALO

    def body(p_ref, prev_ref, cw_ref, z_ref):
        i = pl.program_id(0)
        b = p_ref[:, 0:d].astype(F32)
        u = p_ref[:, d:2 * d].astype(F32) * p_ref[:, 2 * d:3 * d].astype(F32)
        keep = (i > 0).astype(F32)
        hu = prev_ref[:, d:2 * d].astype(F32) * prev_ref[:, 2 * d:3 * d].astype(F32) * keep
        uc = cw_ref[2:3, :] * u + cw_ref[1:2, :] * _shift_down(u, hu, 1, tm) + cw_ref[0:1, :] * _shift_down(u, hu, 2, tm)
        z_ref[...] = (b * uc).astype(BF16)

    return _call(
        body, name=name, grid=(t // tm,),
        in_specs=[pl.BlockSpec((tm, d3), lambda i: (i, 0)),
                  pl.BlockSpec((HALO, d3), lambda i: (jnp.maximum(i * hb - 1, 0), 0)),
                  pl.BlockSpec((3, d), lambda i: (0, 0))],
        out_specs=pl.BlockSpec((tm, d), lambda i: (i, 0)), out_shape=_sds((t, d), BF16))(p, p, cw)


def _matmul_norm_residual(a3, wg, layer, g, x_res, name):
    kc_n, t, kc = a3.shape
    d = wg.shape[-1]
    per = N_DEV // kc_n
    rows = wg.shape[2]
    tm = BIG_ROW_TILE

    def body(a_ref, w_ref, g_ref, x_ref, raw_ref, xo_ref, acc_ref):
        c = pl.program_id(1)
        part = _dot(a_ref[...], w_ref[...].reshape(per * rows, d))

        @pl.when(c == 0)
        def _():
            acc_ref[...] = part

        @pl.when(c > 0)
        def _():
            acc_ref[...] += part

        @pl.when(c == kc_n - 1)
        def _():
            raw = acc_ref[...]
            raw_ref[...] = raw
            xo_ref[...] = x_ref[...] + _rms(raw, g_ref[...])

    row_spec = pl.BlockSpec((tm, d), lambda i, c: (i, 0))
    return _call(
        body, name=name, grid=(t // tm, kc_n),
        in_specs=[pl.BlockSpec((None, tm, kc), lambda i, c: (c, i, 0)),
                  pl.BlockSpec((per, None, rows, d), lambda i, c: (c, layer, 0, 0)),
                  pl.BlockSpec((1, d), lambda i, c: (0, 0)), row_spec],
        out_specs=[row_spec, row_spec], out_shape=[_sds((t, d), F32)] * 2,
        scratch_shapes=[pltpu.VMEM((tm, d), F32)])(a3, wg, g, x_res)


def _alibi_slopes(n_heads):
    hh = np.arange(n_heads, dtype=np.float32) + 1.0
    s = np.power(2.0, -8.0 * hh / n_heads).astype(np.float32)
    return jnp.asarray(np.repeat(s.reshape(n_heads // 2, 2, 1), 2 * ATT_BLOCK, axis=2))


def _band_bias(sl_ref, dil):
    u = lax.broadcasted_iota(jnp.int32, (ATT_BLOCK, 2 * ATT_BLOCK), 0)
    kk = lax.broadcasted_iota(jnp.int32, (ATT_BLOCK, 2 * ATT_BLOCK), 1)
    delta = u + ATT_BLOCK - kk
    valid = (delta >= 0) & (delta <= ATT_BLOCK)
    dist = (delta * dil).astype(F32)
    rows = [jnp.where(valid, -sl_ref[hd:hd + 1, :] * dist, NEG) for hd in range(2)]
    return jnp.concatenate(rows, axis=0)


def _stack_heads(a):
    lane = lax.broadcasted_iota(jnp.int32, a.shape, 1)
    return jnp.concatenate([jnp.where(lane < HEAD_DIM, a, 0.0), jnp.where(lane >= HEAD_DIM, a, 0.0)], axis=0).astype(BF16)


def _unstack_heads(a2):
    top, bot = a2[:ATT_BLOCK], a2[ATT_BLOCK:]
    lane = lax.broadcasted_iota(jnp.int32, top.shape, 1)
    return jnp.where(lane < HEAD_DIM, top, bot)


def _stack_cols(a):
    return jnp.concatenate([a[:, 0:1], a[:, HEAD_DIM:HEAD_DIM + 1]], axis=0)


def _fill_bias(sl_ref, bias_ref):
    kk = lax.broadcasted_iota(jnp.int32, (2 * ATT_BLOCK, 2 * ATT_BLOCK), 1)
    for gi, dil in enumerate(DILATIONS):
        bias = _band_bias(sl_ref, dil)
        bias_ref[2 * gi] = bias
        bias_ref[2 * gi + 1] = jnp.where(kk < ATT_BLOCK, NEG, bias)


def _attention_fwd(q, kv, slopes, name):
    hp, t, _ = q.shape
    ns = t // SUPER
    nd = len(DILATIONS)

    def body(sl_ref, q_ref, kc_ref, kp_ref, vc_ref, vp_ref, o_ref, lse_ref, kw_ref, vw_ref, og_ref, lg_ref, bias_ref):
        n = pl.program_id(1)
        kw_ref[0:SUPER, :] = kp_ref[...]
        kw_ref[SUPER:, :] = kc_ref[...]
        vw_ref[0:SUPER, :] = vp_ref[...]
        vw_ref[SUPER:, :] = vc_ref[...]

        @pl.when(n == 0)
        def _():
            _fill_bias(sl_ref, bias_ref)

        for gi, dil in enumerate(DILATIONS):

            def block(idx, carry, gi=gi, dil=dil):
                r, b = idx % dil, idx // dil
                qs = b * (ATT_BLOCK * dil) + r
                ks = SUPER + (b - 1) * (ATT_BLOCK * dil) + r
                first = jnp.logical_and(n == 0, b == 0).astype(jnp.int32)
                q2 = _stack_heads(q_ref[pl.ds(qs, ATT_BLOCK, stride=dil), :])
                kb = kw_ref[pl.ds(ks, 2 * ATT_BLOCK, stride=dil), :].astype(BF16)
                vb = vw_ref[pl.ds(ks, 2 * ATT_BLOCK, stride=dil), :].astype(BF16)
                s = _dot_nt(q2, kb) + bias_ref[2 * gi + first]
                m = jnp.max(s, axis=-1, keepdims=True)
                p = jnp.exp(s - m)
                l = jnp.sum(p, axis=-1, keepdims=True)
                o2 = _dot(p.astype(BF16), vb) / l
                lse2 = jnp.broadcast_to(m + jnp.log(l), (2 * ATT_BLOCK, LANES))
                og_ref[gi, pl.ds(qs, ATT_BLOCK, stride=dil), :] = _unstack_heads(o2)
                lg_ref[gi, pl.ds(qs, ATT_BLOCK, stride=dil), :] = _unstack_heads(lse2)
                return carry

            lax.fori_loop(0, SUPER // ATT_BLOCK, block, 0, unroll=ATT_UNROLL_FWD)
        lg =[lg_ref[gi] for gi in range(nd)]
        top = functools.reduce(jnp.maximum, lg)
        ws = [jnp.exp(x - top) for x in lg]
        tot = functools.reduce(jnp.add, ws)
        lse_ref[...] = top + jnp.log(tot)
        acc = ws[0] * og_ref[0]
        for gi in range(1, nd):
            acc = acc + ws[gi] * og_ref[gi]
        o_ref[...] = (acc / tot).astype(BF16)

    cur = lambda off: pl.BlockSpec((None, SUPER, LANES), lambda h, n: (h + off, n, 0))
    prev = lambda off: pl.BlockSpec((None, SUPER, LANES), lambda h, n: (h + off, jnp.maximum(n - 1, 0), 0))
    return _call(
        body, name=name, grid=(hp, ns),
        in_specs=[pl.BlockSpec((None, 2, 2 * ATT_BLOCK), lambda h, n: (h, 0, 0)), cur(0), cur(0), prev(0), cur(hp), prev(hp)],
        out_specs=[pl.BlockSpec((SUPER, LANES), lambda h, n: (n, h)), cur(0)],
        out_shape=[_sds((t, hp * LANES), BF16), _sds((hp, t, LANES), F32)],
        scratch_shapes=[pltpu.VMEM((2 * SUPER, LANES), F32), pltpu.VMEM((2 * SUPER, LANES), F32),
                        pltpu.VMEM((nd, SUPER, LANES), F32), pltpu.VMEM((nd, SUPER, LANES), F32),
                        pltpu.VMEM((2 * nd, 2 * ATT_BLOCK, 2 * ATT_BLOCK), F32)],
    )(slopes, q, kv, kv, kv, kv)


def _attention_bwd(q, kv, o, lse, d_o, slopes, q_scale, name):
    hp, t, _ = q.shape
    ns = t // SUPER

    def body(sl_ref, q_ref, kc_ref, kp_ref, vc_ref, vp_ref, o_ref, lse_ref, do_ref,
             dq_ref, dk_ref, dv_ref, kw_ref, vw_ref, dkw_ref, dvw_ref, dd_ref, bias_ref):
        n = pl.program_id(1)

        @pl.when(n == 0)
        def _():
            dkw_ref[...] = jnp.zeros_like(dkw_ref)
            dvw_ref[...] = jnp.zeros_like(dvw_ref)

        @pl.when(n > 0)
        def _():
            dkw_ref[0:SUPER, :] = dkw_ref[SUPER:, :]
            dvw_ref[0:SUPER, :] = dvw_ref[SUPER:, :]
            dkw_ref[SUPER:, :] = jnp.zeros((SUPER, LANES), F32)
            dvw_ref[SUPER:, :] = jnp.zeros((SUPER, LANES), F32)

        @pl.when(n < ns)
        def _():
            kw_ref[0:SUPER, :] = kp_ref[...]
            kw_ref[SUPER:, :] = kc_ref[...]
            vw_ref[0:SUPER, :] = vp_ref[...]
            vw_ref[SUPER:, :] = vc_ref[...]
            prod = do_ref[...] * o_ref[...].astype(F32)
            lane = lax.broadcasted_iota(jnp.int32, prod.shape, 1)
            d0 = jnp.sum(jnp.where(lane < HEAD_DIM, prod, 0.0), axis=-1, keepdims=True)
            d1 = jnp.sum(jnp.where(lane >= HEAD_DIM, prod, 0.0), axis=-1, keepdims=True)
            dd_ref[...] = jnp.where(lane < HEAD_DIM, d0, d1)
            dq_ref[...] = jnp.zeros_like(dq_ref)

            @pl.when(n == 0)
            def _():
                _fill_bias(sl_ref, bias_ref)

            for gi, dil in enumerate(DILATIONS):

                def block(idx, carry, gi=gi, dil=dil):
                    r, b = idx % dil, idx // dil
                    qs = b * (ATT_BLOCK * dil) + r
                    ks = SUPER + (b - 1) * (ATT_BLOCK * dil) + r
                    first = jnp.logical_and(n == 0, b == 0).astype(jnp.int32)
                    rows = pl.ds(qs, ATT_BLOCK, stride=dil)
                    keys = pl.ds(ks, 2 * ATT_BLOCK, stride=dil)
                    q2 = _stack_heads(q_ref[rows, :])
                    do2 = _stack_heads(do_ref[rows, :])
                    kb = kw_ref[keys, :].astype(BF16)
                    vb = vw_ref[keys, :].astype(BF16)
                    s = _dot_nt(q2, kb) + bias_ref[2 * gi + first]
                    p = jnp.exp(s - _stack_cols(lse_ref[rows, :]))
                    dp = _dot_nt(do2, vb)
                    ds = (p * (dp - _stack_cols(dd_ref[rows, :]))).astype(BF16)
                    dq_ref[rows, :] += _unstack_heads(_dot(ds, kb)) * q_scale
                    dkw_ref[keys, :] += _dot_tn(ds, q2)
                    dvw_ref[keys, :] += _dot_tn(p.astype(BF16), do2)
                    return carry

                lax.fori_loop(0, SUPER // ATT_BLOCK, block, 0, unroll=ATT_UNROLL_BWD)

        dk_ref[...] = dkw_ref[0:SUPER, :]
        dv_ref[...] = dvw_ref[0:SUPER, :]

    last = ns - 1
    cur = lambda off: pl.BlockSpec((None, SUPER, LANES), lambda h, n: (h + off, jnp.minimum(n, last), 0))
    prev = lambda off: pl.BlockSpec((None, SUPER, LANES), lambda h, n: (h + off, jnp.clip(n - 1, 0, last), 0))
    nat = pl.BlockSpec((SUPER, LANES), lambda h, n: (jnp.minimum(n, last), h))
    late = pl.BlockSpec((None, SUPER, LANES), lambda h, n: (h, jnp.maximum(n - 1, 0), 0))
    dq, dk, dv = _call(
        body, name=name, grid=(hp, ns + 1),
        in_specs=[pl.BlockSpec((None, 2, 2 * ATT_BLOCK), lambda h, n: (h, 0, 0)), cur(0), cur(0), prev(0), cur(hp), prev(hp),
                  nat, cur(0), nat],
        out_specs=[cur(0), late, late],
        out_shape=[_sds((hp, t, LANES), F32)] * 3,
        scratch_shapes=[pltpu.VMEM((2 * SUPER, LANES), F32)] * 4 + [
            pltpu.VMEM((SUPER, LANES), F32), pltpu.VMEM((2 * len(DILATIONS), 2 * ATT_BLOCK, 2 * ATT_BLOCK), F32)],
    )(slopes, q, kv, kv, kv, kv, o, lse, d_o)
    return dq, dk, dv


def _loss_head(y, target, raw, g, name):
    t, d = y.shape
    tm = ROW_TILE

    def body(y_ref, t_ref, raw_ref, g_ref, sq_ref, dy_ref, draw_ref, dg_ref):
        i = pl.program_id(0)
        err = y_ref[...] - t_ref[...]
        dy = err * (1.0 / d)
        dy_ref[...] = dy
        draw, dg = _rms_bwd(raw_ref[...], g_ref[...], dy)
        draw_ref[...] = draw.astype(BF16)
        sq = jnp.zeros((8, LANES), F32) + jnp.sum(err * err)

        @pl.when(i == 0)
        def _():
            sq_ref[...] = sq
            dg_ref[...] = dg

        @pl.when(i > 0)
        def _():
            sq_ref[...] += sq
            dg_ref[...] += dg

    row = pl.BlockSpec((tm, d), lambda i: (i, 0))
    vec = pl.BlockSpec((1, d), lambda i: (0, 0))
    return _call(
        body, name=name, grid=(t // tm,), in_specs=[row, row, row, vec],
        out_specs=[pl.BlockSpec((8, LANES), lambda i: (0, 0)), row, row, vec],
        out_shape=[_sds((8, LANES), F32), _sds((t, d), F32), _sds((t, d), BF16), _sds((1, d), F32)])(y, target, raw, g)


def _bwd_matmul_norms(a_specs, a_args, a_tile, n_steps, w_spec, w_arg, w_mat, xa, ga, resid, xb, gb, name):
    t, d = xa.shape
    tm = BWD_ROW_TILE
    na = len(a_specs)
    second = xb is not None
    per = 2 if n_steps % 2 == 0 else 1
    n_steps //= per

    def blocks_of(spec, k):
        return pl.BlockSpec(spec.block_shape, lambda i, j: spec.index_map(i, per * j + k))

    def body(*refs):
        a_refs, w_refs = refs[:per * na], refs[per * na:per * na + per]
        xa_ref, ga_ref, res_ref = refs[per * na + per:per * na + per + 3]
        rest = refs[per * na + per + 3:]
        if second:
            xb_ref, gb_ref, dx_ref, d2_ref, dga_ref, dgb_ref, acc_ref = rest
        else:
            dx_ref, dga_ref, acc_ref = rest
        i, j = pl.program_id(0), pl.program_id(1)
        part = None
        for k in range(per):
            term = _dot_nt(a_tile(per * j + k, *a_refs[k * na:(k + 1) * na]), w_mat(w_refs[k]))
            part = term if part is None else part + term

        @pl.when(j == 0)
        def _():
            acc_ref[...] = part

        @pl.when(j > 0)
        def _():
            acc_ref[...] += part

        @pl.when(j == n_steps - 1)
        def _():
            da, dga = _rms_bwd(xa_ref[...], ga_ref[...], acc_ref[...])
            dx = res_ref[...] + da
            dx_ref[...] = dx
            if second:
                d2, dgb = _rms_bwd(xb_ref[...], gb_ref[...], dx)
                d2_ref[...] = d2.astype(BF16)

            @pl.when(i == 0)
            def _():
                dga_ref[...] = dga
                if second:
                    dgb_ref[...] = dgb

            @pl.when(i > 0)
            def _():
                dga_ref[...] += dga
                if second:
                    dgb_ref[...] += dgb

    row = pl.BlockSpec((tm, d), lambda i, j: (i, 0))
    vec = pl.BlockSpec((1, d), lambda i, j: (0, 0))
    in_specs = [blocks_of(sp, k) for k in range(per) for sp in a_specs] + [blocks_of(w_spec, k) for k in range(per)]
    in_specs += [row, vec, row]
    args = list(a_args) * per + [w_arg] * per + [xa, ga, resid]
    if second:
        in_specs += [row, vec]
        args += [xb, gb]
        out_specs = [row, row, vec, vec]
        out_shape = [_sds((t, d), F32), _sds((t, d), BF16), _sds((1, d), F32), _sds((1, d), F32)]
    else:
        out_specs = [row, vec]
        out_shape = [_sds((t, d), F32), _sds((1, d), F32)]
    return _call(body, name=name, grid=(t // tm, n_steps), in_specs=in_specs, out_specs=out_specs,
                 out_shape=out_shape, scratch_shapes=[pltpu.VMEM((tm, d), F32)])(*args)


def _heads_to_rows(*refs):
    hp = refs[0].shape[0]
    cols = []
    for p in range(hp):
        v = refs[0][p]
        for r in refs[1:]:
            v = v + r[p]
        cols.append(v)
    return jnp.concatenate(cols, axis=-1).astype(BF16)


def _matmul_nt_rows(a, wg, layer, out_dtype, name):
    t, d = a.shape
    tm = ROW_TILE

    def body(a_ref, w_ref, o_ref):
        o_ref[...] = _dot_nt(a_ref[...], w_ref[...].reshape(d, d)).astype(out_dtype)

    row = pl.BlockSpec((tm, d), lambda i: (i, 0))
    return _call(body, name=name, grid=(t // tm,),
                 in_specs=[row, pl.BlockSpec((N_DEV, None, d // N_DEV, d), lambda i: (0, layer, 0, 0))],
                 out_specs=row, out_shape=_sds((t, d), out_dtype))(a, wg)


def _swiglu_bwd(d_ff, wg, layer, gate, up, name):
    t, d = d_ff.shape
    fc = gate.shape[-1]
    rows = wg.shape[2]
    tm = BIG_ROW_TILE

    def body(df_ref, w_ref, g_ref, u_ref, dh_ref):
        da = _dot_nt(df_ref[...], w_ref[...].reshape(2 * rows, d))
        gate, up = g_ref[...].astype(F32), u_ref[...].astype(F32)
        sig = jax.nn.sigmoid(gate)
        dh_ref[0] = (da * up * (sig * (1.0 + gate * (1.0 - sig)))).astype(BF16)
        dh_ref[1] = (da * (gate * sig)).astype(BF16)

    return _call(
        body, name=name, grid=(t // tm, 4),
        in_specs=[pl.BlockSpec((tm, d), lambda i, c: (i, 0)),
                  pl.BlockSpec((2, None, rows, d), lambda i, c: (c, layer, 0, 0)),
                  pl.BlockSpec((None, tm, fc), lambda i, c: (c, i, 0)),
                  pl.BlockSpec((None, tm, fc), lambda i, c: (c, i, 0))],
        out_specs=pl.BlockSpec((None, 2, tm, fc), lambda i, c: (c, 0, i, 0)),
        out_shape=_sds((4, 2, t, fc), BF16))(d_ff, wg, gate, up)


def _conv_bwd(p, d_z, cw, name):
    t, d3 = p.shape
    d = d3 // 3
    tm = ROW_TILE
    hb = tm // HALO
    nt = t // tm

    def body(p_ref, prev_ref, next_ref, dz_ref, dzn_ref, cw_ref, dp_ref, dcw_ref):
        i = pl.program_id(0)
        b = p_ref[:, 0:d].astype(F32)
        c = p_ref[:, d:2 * d].astype(F32)
        h = p_ref[:, 2 * d:3 * d].astype(F32)
        u = c * h
        hu = prev_ref[:, d:2 * d].astype(F32) * prev_ref[:, 2 * d:3 * d].astype(F32) * (i > 0).astype(F32)
        u1, u2 = _shift_down(u, hu, 1, tm), _shift_down(u, hu, 2, tm)
        uc = cw_ref[2:3, :] * u + cw_ref[1:2, :] * u1 + cw_ref[0:1, :] * u2
        dz = dz_ref[...].astype(F32)
        duc = dz * b
        dn = dzn_ref[...].astype(F32) * next_ref[:, 0:d].astype(F32) * (i < nt - 1).astype(F32)
        du = cw_ref[2:3, :] * duc + cw_ref[1:2, :] * _shift_up(duc, dn, 1, tm) + cw_ref[0:1, :] * _shift_up(duc, dn, 2, tm)
        dp_ref[:, 0:d] = (dz * uc).astype(BF16)
        dp_ref[:, d:2 * d] = (du * h).astype(BF16)
        dp_ref[:, 2 * d:3 * d] = (du * c).astype(BF16)
        dcw = jnp.concatenate([jnp.sum(duc * u2, axis=0, keepdims=True), jnp.sum(duc * u1, axis=0, keepdims=True),
                               jnp.sum(duc * u, axis=0, keepdims=True)], axis=0)

        @pl.when(i == 0)
        def _():
            dcw_ref[...] = dcw

        @pl.when(i > 0)
        def _():
            dcw_ref[...] += dcw

    last_halo = t // HALO - 1
    return _call(
        body, name=name, grid=(nt,),
        in_specs=[pl.BlockSpec((tm, d3), lambda i: (i, 0)),
                  pl.BlockSpec((HALO, d3), lambda i: (jnp.maximum(i * hb - 1, 0), 0)),
                  pl.BlockSpec((HALO, d3), lambda i: (jnp.minimum((i + 1) * hb, last_halo), 0)),
                  pl.BlockSpec((tm, d), lambda i: (i, 0)),
                  pl.BlockSpec((HALO, d), lambda i: (jnp.minimum((i + 1) * hb, last_halo), 0)),
                  pl.BlockSpec((3, d), lambda i: (0, 0))],
        out_specs=[pl.BlockSpec((tm, d3), lambda i: (i, 0)), pl.BlockSpec((3, d), lambda i: (0, 0))],
        out_shape=[_sds((t, d3), BF16), _sds((3, d), F32)])(p, p, p, d_z, d_z, cw)


def _grad_weight(a_specs, a_args, a_tile, b_specs, b_args, b_tile, n_out, acc_shape, out_spec, out_shape, t, name):
    tt = BIG_ROW_TILE
    na, nb = len(a_specs), len(b_specs)

    def body(*refs):
        a_refs, b_refs = refs[:na], refs[na:na + nb]
        o_ref, acc_ref = refs[na + nb:]
        s = pl.program_id(1)
        part = _dot_tn(a_tile(pl.program_id(0), *a_refs), b_tile(pl.program_id(0), *b_refs))

        @pl.when(s == 0)
        def _():
            acc_ref[...] = part

        @pl.when(s > 0)
        def _():
            acc_ref[...] += part

        @pl.when(s == t // tt - 1)
        def _():
            o_ref[...] = acc_ref[...].astype(BF16).reshape(o_ref.shape)

    return _call(body, name=name, grid=(n_out, t // tt), in_specs=list(a_specs) + list(b_specs), out_specs=out_spec,
                 out_shape=out_shape, scratch_shapes=[pltpu.VMEM(acc_shape, F32)])(*a_args, *b_args)


def _ident(*args):
    return args[-1][...]


def _heads_tile(j, *refs):
    return _heads_to_rows(*refs)


def kernel(x, norm_g, conv_in_w, conv_w, conv_out_w, kv_norm_g, kv_w, q_w, o_w, ffn_in_w, ffn_out_w, loss_target, m_norm_g, m_conv_in_w, m_conv_w, m_conv_out_w, m_kv_norm_g, m_kv_w, m_q_w, m_o_w, m_ffn_in_w, m_ffn_out_w, v_norm_g, v_conv_in_w, v_conv_w, v_conv_out_w, v_kv_norm_g, v_kv_w, v_q_w, v_o_w, v_ffn_in_w, v_ffn_out_w):
    x0 = x[0]
    target = loss_target[0]
    t, d = x0.shape
    depth = norm_g.shape[0]
    n_a = conv_in_w.shape[0]
    n_b = q_w.shape[0]
    hp = d // LANES
    tm, tg = BWD_ROW_TILE, BIG_ROW_TILE
    assert t % SUPER == 0 and d % LANES == 0 and depth == n_a + n_b
    dev = 4 * lax.axis_index("x") + 2 * lax.axis_index("y") + lax.axis_index("c")

    n_small = 4 * depth + 3 * n_a
    small_rows = -(-(n_small + 1) // 8) * 8
    small_local = jnp.concatenate([norm_g.reshape(4 * depth, -1), conv_w.reshape(3 * n_a, -1),
                                   jnp.zeros((small_rows - n_small, norm_g.shape[-1]), F32)], axis=0)
    big = {"conv_in_w": conv_in_w, "conv_out_w": conv_out_w, "kv_w": kv_w[None], "q_w": q_w, "o_w": o_w,
           "ffn_in_w": ffn_in_w, "ffn_out_w": ffn_out_w}
    names = list(big)

    def group(layer):
        if layer < n_a:
            return [("conv_in_w", layer), ("conv_out_w", layer), ("ffn_in_w", layer), ("ffn_out_w", layer)]
        j = layer - n_a
        return ([("kv_w", 0)] if j == 0 else []) + [("q_w", j), ("o_w", j), ("ffn_in_w", layer), ("ffn_out_w", layer)]

    slot = dev.astype(jnp.int32).reshape(1)
    first = _all_gather([small_local] + [_cast_layer(big[k], i, None, f"cast_{k}_{i}") for k, i in group(0)], "gather_weights")
    small_all = first[0].transpose(1, 0, 2).reshape(small_rows, d)
    wl = {key: a[:, None] for key, a in zip(group(0), first[1:])}
    def gather_start(layer, after):
        lands = [_cast_layer(big[k], i, slot, f"cast_{k}_{i}") for k, i in group(layer)]
        send_sems, recv_sems, _, lands, tok = _send_start([], lands, after, f"gather_start_l{layer}")
        return (send_sems, recv_sems, lands), tok[0, 0]

    in_flight, token = gather_start(1, small_all)
    W = lambda k, i: (wl[(k, i)], 0)
    gain = lambda layer, k: small_all[4 * layer + k][None]
    taps = lambda layer: small_all[4 * depth + 3 * layer: 4 * depth + 3 * layer + 3]
    g_kv = kv_norm_g[None]
    slopes = _alibi_slopes(d // HEAD_DIM)
    fc = big["ffn_in_w"].shape[-1]
    cb = big["conv_in_w"].shape[-1]
    kvb = big["kv_w"].shape[-1]
    q_scale = HEAD_DIM ** -0.5

    saved = []
    kv = kvn = None
    xs = x0
    for layer in range(depth):
        tag = f"_l{layer}"
        if layer > 0:
            send_sems, recv_sems, lands = in_flight
            _, lands = _send_wait(send_sems, recv_sems, [], lands, xs, f"gather_wait_l{layer}")
            wl.update({key: a[:, None] for key, a in zip(group(layer), lands)})
            if layer + 1 < depth:
                in_flight, token = gather_start(layer + 1, xs)
        s = {"x_in": xs}
        g0 = gain(layer, 0) + token if layer + 1 < depth else gain(layer, 0)
        if layer < n_a:
            s["p"], s["xn"] = _norm_matmul_cols(xs, g0, *W("conv_in_w", layer), "cols", "conv_in" + tag)
            s["z"] = _conv_fwd(s["p"], taps(layer), "conv" + tag)
            s["mix"], x_mid = _matmul_norm_residual(s["z"][None], *W("conv_out_w", layer), gain(layer, 1), xs, "conv_out" + tag)
        else:
            j = layer - n_a
            if kv is None:
                kv, kvn = _norm_matmul_cols(xs, g_kv, *W("kv_w", 0), "heads", "kv_proj")
            s["q"], s["xn"] = _norm_matmul_heads(xs, g0, *W("q_w", j), q_scale, "q_proj" + tag)
            s["o"], s["lse"] = _attention_fwd(s["q"], kv, slopes, "attention" + tag)
            s["mix"], x_mid = _matmul_norm_residual(s["o"][None], *W("o_w", j), gain(layer, 1), xs, "o_proj" + tag)
        s["x_mid"] = x_mid
        s["gate"], s["up"], s["a"], s["fn"] = _ffn_in_swiglu(x_mid, gain(layer, 2), *W("ffn_in_w", layer), "ffn_in" + tag)
        s["ff"], xs = _matmul_norm_residual(s["a"], *W("ffn_out_w", layer), gain(layer, 3), x_mid, "ffn_out" + tag)
        saved.append(s)

    last = saved[-1]
    sq, dx_out, d_ff, dg3 = _loss_head(xs, target, last["ff"], gain(depth - 1, 3), "loss_head")
    loss = lax.psum(sq[0, 0] * (0.5 / d), ("x", "y", "c"))

    dgain = {(depth - 1, 3): dg3}
    dtaps = {}
    grads = {k: [None] * big[k].shape[0] for k in names}
    dkv_parts = []
    scattering = []

    def scatter_start(keys, tag):
        parts = [grads[k][i] for k, i in keys]
        zones = [lax.empty(p.shape, p.dtype) for p in parts]
        send_sems, recv_sems, parts, zones, tok = _send_start(parts, zones, small_all, "scatter_start" + tag)
        scattering.append((keys, tag, send_sems, recv_sems, parts, zones))
        return tok[0, 0]

    for layer in reversed(range(depth)):
        tag = f"_l{layer}"
        s = saved[layer]
        dh = _swiglu_bwd(d_ff, *W("ffn_out_w", layer), s["gate"], s["up"], "swiglu_bwd" + tag)
        rows_out = big["ffn_out_w"].shape[1]
        grads["ffn_out_w"][layer] = _grad_weight(
            [pl.BlockSpec((None, tg, fc), lambda c, i: (c, i, 0))], [s["a"]], _ident,
            [pl.BlockSpec((tg, d), lambda c, i: (i, 0))], [d_ff], _ident,
            4, (fc, d), pl.BlockSpec((2, rows_out, d), lambda c, i: (c, 0, 0)), _sds((N_DEV, rows_out, d), BF16), t,
            "grad_ffn_out" + tag)
        grads["ffn_in_w"][layer] = _grad_weight(
            [pl.BlockSpec((tg, d), lambda j, i: (i, 0))], [s["fn"]], _ident,
            [pl.BlockSpec((None, None, tg, fc), lambda j, i: (j % 4, j // 4, i, 0))], [dh], _ident,
            N_DEV, (d, fc), pl.BlockSpec((None, d, fc), lambda j, i: (j, 0, 0)), _sds((N_DEV, d, fc), BF16), t,
            "grad_ffn_in" + tag)
        tok = scatter_start([("ffn_in_w", layer), ("ffn_out_w", layer)], "_ffn" + tag)
        dx_mid, d_mix, dg2, dg1 = _bwd_matmul_norms(
            [pl.BlockSpec((None, None, tm, fc), lambda i, j: (j % 4, j // 4, i, 0))], [dh], _ident, N_DEV,
            pl.BlockSpec((None, None, d, fc), lambda i, j: (j, 0, 0, 0)), W("ffn_in_w", layer)[0], _ident,
            s["x_mid"], gain(layer, 2) + tok, dx_out, s["mix"], gain(layer, 1), "ffn_in_bwd" + tag)
        dgain[(layer, 2)], dgain[(layer, 1)] = dg2, dg1
        full_rows = pl.BlockSpec((N_DEV, d // N_DEV, d), lambda j, i: (0, 0, 0))
        rows_w = lambda wname, idx: (pl.BlockSpec((N_DEV, None, d // N_DEV, d), lambda i, j: (0, 0, 0, 0)), W(wname, idx)[0],
                                     lambda w_ref: w_ref[...].reshape(d, d))
        if layer < n_a:
            d_z = _matmul_nt_rows(d_mix, *W("conv_out_w", layer), BF16, "conv_out_bwd" + tag)
            grads["conv_out_w"][layer] = _grad_weight(
                [pl.BlockSpec((tg, d), lambda j, i: (i, 0))], [s["z"]], _ident,
                [pl.BlockSpec((tg, d), lambda j, i: (i, 0))], [d_mix], _ident,
                1, (d, d), full_rows, _sds((N_DEV, d // N_DEV, d), BF16), t, "grad_conv_out" + tag)
            d_p, dtaps[layer] = _conv_bwd(s["p"], d_z, taps(layer), "conv_bwd" + tag)
            grads["conv_in_w"][layer] = _grad_weight(
                [pl.BlockSpec((tg, d), lambda j, i: (i, 0))], [s["xn"]], _ident,
                [pl.BlockSpec((tg, cb), lambda j, i: (i, j))], [d_p], _ident,
                N_DEV, (d, cb), pl.BlockSpec((None, d, cb), lambda j, i: (j, 0, 0)), _sds((N_DEV, d, cb), BF16), t,
                "grad_conv_in" + tag)
            a_specs, a_args, a_tile, n_steps = [pl.BlockSpec((tm, cb), lambda i, j: (i, j))], [d_p], _ident, N_DEV
            w_spec = pl.BlockSpec((None, None, d, cb), lambda i, j: (j, 0, 0, 0))
            w_arg, w_mat = W("conv_in_w", layer)[0], _ident
            resid = dx_mid
        else:
            j_b = layer - n_a
            d_o = _matmul_nt_rows(d_mix, *W("o_w", j_b), F32, "o_proj_bwd" + tag)
            grads["o_w"][j_b] = _grad_weight(
                [pl.BlockSpec((tg, d), lambda j, i: (i, 0))], [s["o"]], _ident,
                [pl.BlockSpec((tg, d), lambda j, i: (i, 0))], [d_mix], _ident,
                1, (d, d), full_rows, _sds((N_DEV, d // N_DEV, d), BF16), t, "grad_o" + tag)
            dq, dk, dv = _attention_bwd(s["q"], kv, s["o"], s["lse"], d_o, slopes, q_scale, "attention_bwd" + tag)
            dkv_parts.append((dk, dv))
            heads_spec = pl.BlockSpec((hp, tg, LANES), lambda j, i: (0, i, 0))
            grads["q_w"][j_b] = _grad_weight(
                [pl.BlockSpec((tg, d), lambda j, i: (i, 0))], [s["xn"]], _ident,
                [heads_spec], [dq], _heads_tile,
                1, (d, d), full_rows, _sds((N_DEV, d // N_DEV, d), BF16), t, "grad_q" + tag)
            a_specs, a_args, a_tile, n_steps = [pl.BlockSpec((hp, tm, LANES), lambda i, j: (0, i, 0))], [dq], _heads_tile, 1
            w_spec, w_arg, w_mat = rows_w("q_w", j_b)
            resid = dx_mid
            if layer == n_a:
                pieces = kvb // LANES
                halves = []
                for src in (0, 1):
                    halves.append([part[src] for part in dkv_parts])
                n_half = len(dkv_parts)
                kv_args = [arr for src in (0, 1) for arr in halves[src]]

                def kv_block(src, j):
                    return jnp.where((j // 4) == src, j % 4, 0)

                def kv_tile(j, *refs):
                    keys = _heads_to_rows(*refs[:n_half])
                    vals = _heads_to_rows(*refs[n_half:])
                    return jnp.where(j < 4, keys, vals)

                kv_specs = [pl.BlockSpec((pieces, tm, LANES), functools.partial(lambda i, j, src: (kv_block(src, j), i, 0), src=src))
                            for src in (0, 1) for _ in range(n_half)]
                resid, dgain["kv"] = _bwd_matmul_norms(
                    kv_specs, kv_args, kv_tile, N_DEV,
                    pl.BlockSpec((None, None, d, kvb), lambda i, j: (j, 0, 0, 0)), W("kv_w", 0)[0], _ident,
                    s["x_in"], g_kv, dx_mid, None, None, "kv_proj_bwd")
                kv_b_specs = [pl.BlockSpec((pieces, tg, LANES), functools.partial(lambda j, i, src: (kv_block(src, j), i, 0), src=src))
                              for src in (0, 1) for _ in range(n_half)]
                grads["kv_w"][0] = _grad_weight(
                    [pl.BlockSpec((tg, d), lambda j, i: (i, 0))], [kvn], _ident,
                    kv_b_specs, kv_args, kv_tile,
                    N_DEV, (d, kvb), pl.BlockSpec((None, d, kvb), lambda j, i: (j, 0, 0)), _sds((N_DEV, d, kvb), BF16), t,
                    "grad_kv")
        tok = scatter_start([key for key in group(layer) if not key[0].startswith("ffn")], "_mix" + tag)
        if layer > 0:
            prev = saved[layer - 1]
            dx_out, d_ff, dg0, dg3p = _bwd_matmul_norms(
                a_specs, a_args, a_tile, n_steps, w_spec, w_arg, w_mat,
                s["x_in"], gain(layer, 0) + tok, resid, prev["ff"], gain(layer - 1, 3), "mixer_in_bwd" + tag)
            dgain[(layer, 0)], dgain[(layer - 1, 3)] = dg0, dg3p
        else:
            grad_x, dg0 = _bwd_matmul_norms(
                a_specs, a_args, a_tile, n_steps, w_spec, w_arg, w_mat,
                s["x_in"], gain(layer, 0), resid, None, None, "mixer_in_bwd" + tag)
            dgain[(layer, 0)] = dg0

    small_grad = jnp.concatenate(
        [dgain[(layer, k)] for layer in range(depth) for k in range(4)] + [dtaps[layer] for layer in range(n_a)]
        + [dgain["kv"]] + [jnp.zeros((small_rows - n_small - 1, d), F32)], axis=0)
    small_grads_all = _all_gather([small_grad], "gather_small_grads")[0]
    lo = dev * (d // N_DEV)

    def pack(ng, cwp, kvg):
        rows = jnp.concatenate([ng.reshape(4 * depth, -1), cwp.reshape(3 * n_a, -1)], axis=0)
        z = lax.dynamic_update_slice(jnp.zeros((small_rows, d), F32), rows, (0, lo))
        return lax.dynamic_update_slice(z, kvg[None], (n_small, 0))

    w_small = lax.dynamic_update_slice(small_all, g_kv, (n_small, 0))
    m_small, v_small = pack(m_norm_g, m_conv_w, m_kv_norm_g), pack(v_norm_g, v_conv_w, v_kv_norm_g)
    sm = _small_adamw(small_grads_all, w_small, m_small, v_small, "adamw_small")

    def unpack(a):
        mine = lax.dynamic_slice(a, (0, lo), (small_rows, d // N_DEV))
        return (mine[:4 * depth].reshape(norm_g.shape), mine[4 * depth:n_small].reshape(conv_w.shape), a[n_small])

    small_out = [unpack(a) for a in sm]

    moments = {"conv_in_w": (m_conv_in_w, v_conv_in_w), "conv_out_w": (m_conv_out_w, v_conv_out_w),
               "kv_w": (m_kv_w[None], v_kv_w[None]), "q_w": (m_q_w, v_q_w), "o_w": (m_o_w, v_o_w),
               "ffn_in_w": (m_ffn_in_w, v_ffn_in_w), "ffn_out_w": (m_ffn_out_w, v_ffn_out_w)}
    res = {k: [None] * big[k].shape[0] for k in names}
    for keys, tag, send_sems, recv_sems, parts, zones in scattering:
        parts, zones = _send_wait(send_sems, recv_sems, parts, zones, grad_x, "scatter_wait" + tag)
        for (k, i), part, zone in zip(keys, parts, zones):
            res[k][i] = _sum_adamw(slot, part, zone, big[k], moments[k][0], moments[k][1], i, f"adamw_{k}_{i}")

    def big_out(k, which):
        st = jnp.stack([res[k][i][which] for i in range(big[k].shape[0])], axis=0)
        return st[0] if k == "kv_w" else st

    out_names = ["norm_g", "conv_in_w", "conv_w", "conv_out_w", "kv_norm_g", "kv_w", "q_w", "o_w", "ffn_in_w", "ffn_out_w"]
    small_pos = {"norm_g": 0, "conv_w": 1, "kv_norm_g": 2}
    outs = [loss, grad_x[None]]
    for which in range(4):
        for k in out_names:
            outs.append(small_out[which][small_pos[k]] if k in small_pos else big_out(k, which))
    return tuple(outs)
```

```python
import functools
import math

import numpy as np
import jax
import jax.numpy as jnp
from jax import lax
from jax.experimental import pallas as pl
from jax.experimental.pallas import tpu as pltpu

F32 = jnp.float32
BF16 = jnp.bfloat16

N_DEV = 8
RMS_EPS = 1e-6
HEAD_DIM = 64
LANES = 128
ATT_BLOCK = 128
DILATIONS = (1, 4, 16)
SUPER = ATT_BLOCK * DILATIONS[-1]
NEG = -1e30
ATT_UNROLL_FWD = 8
ATT_UNROLL_BWD = 8

ADAM_LR, ADAM_B1, ADAM_B2, ADAM_EPS, ADAM_WD, ADAM_STEP = 0.001, 0.9, 0.999, 1e-08, 0.01, 10

ROW_TILE = 512
BIG_ROW_TILE = 1024
BWD_ROW_TILE = 512
MESH = pl.DeviceIdType.MESH


def _call(body, *, name, grid=None, in_specs=None, out_specs=None, out_shape=None, scratch_shapes=(), prefetch=False,
          **params):
    cp = pltpu.CompilerParams(**params) if params else None
    if prefetch:
        spec = pltpu.PrefetchScalarGridSpec(num_scalar_prefetch=1, grid=grid, in_specs=in_specs, out_specs=out_specs,
                                            scratch_shapes=list(scratch_shapes))
        return pl.pallas_call(body, name=name, grid_spec=spec, out_shape=out_shape, compiler_params=cp)
    kwargs = {k: v for k, v in (("grid", grid), ("in_specs", in_specs), ("out_specs", out_specs)) if v is not None}
    return pl.pallas_call(body, name=name, out_shape=out_shape, scratch_shapes=list(scratch_shapes),
                          compiler_params=cp, **kwargs)


def _sds(shape, dtype):
    return jax.ShapeDtypeStruct(tuple(shape), dtype)


def _rms(x, g):
    r = lax.rsqrt(jnp.mean(x * x, axis=-1, keepdims=True) + RMS_EPS)
    return x * r * g


def _rms_bwd(x, g, dy):
    r = lax.rsqrt(jnp.mean(x * x, axis=-1, keepdims=True) + RMS_EPS)
    xh = x * r
    dxh = dy * g
    dx = r * (dxh - xh * jnp.mean(dxh * xh, axis=-1, keepdims=True))
    return dx, jnp.sum(dy * xh, axis=0, keepdims=True)


def _dot(a, b):
    return jnp.dot(a, b, preferred_element_type=F32)


def _dot_nt(a, b):
    return lax.dot_general(a, b, (((1,), (1,)), ((), ())), preferred_element_type=F32)


def _dot_tn(a, b):
    return lax.dot_general(a, b, (((0,), (0,)), ((), ())), preferred_element_type=F32)


def _mesh_pos():
    return lax.axis_index("x"), lax.axis_index("y"), lax.axis_index("c")


def _all_gather(arrs, name):
    n = len(arrs)

    def body(*refs):
        ins, outs = refs[:n], refs[n:2 * n]
        send_sems, recv_sems, local_sems = refs[2 * n:]
        x, y, c = _mesh_pos()
        me, sibling = (x, y, c), (x, y, 1 - c)
        chips = [(1 - x, y), (x, 1 - y), (1 - x, 1 - y)]

        def copy(a, k, block, to, src=None):
            dst = outs[a].at[4 * block[0] + 2 * block[1] + block[2]]
            return pltpu.make_async_remote_copy(
                src_ref=dst if src is None else src, dst_ref=dst, send_sem=send_sems.at[a, k],
                recv_sem=recv_sems.at[a, k], device_id=to, device_id_type=MESH)

        started = []
        for a in range(n):
            mine = pltpu.make_async_copy(ins[a], outs[a].at[4 * x + 2 * y + c], local_sems.at[a])
            mine.start()
            started.append(mine)
        first = []
        for a in range(n):
            first.append(copy(a, 0, me, sibling, src=ins[a]))
            first += [copy(a, 1 + j, me, (*chip, c), src=ins[a]) for j, chip in enumerate(chips)]
        for cp in first:
            cp.start()
        passed = []
        for a in range(n):
            for j, chip in enumerate(chips):
                copy(a, 1 + j, (*chip, c), me).wait_recv()
                fwd = copy(a, 4 + j, (*chip, c), sibling)
                fwd.start()
                passed.append(fwd)
        for a in range(n):
            copy(a, 0, sibling, me).wait_recv()
            for j, chip in enumerate(chips):
                copy(a, 4 + j, (*chip, 1 - c), me).wait_recv()
        for cp in first + passed:
            cp.wait_send()
        for cp in started:
            cp.wait()

    any_spec = pl.BlockSpec(memory_space=pl.ANY)
    outs = _call(
        body, name=name, in_specs=[any_spec] * n, out_specs=[any_spec] * n,
        out_shape=[_sds((N_DEV,) + a.shape, a.dtype) for a in arrs],
        scratch_shapes=[pltpu.SemaphoreType.DMA((n, 7)), pltpu.SemaphoreType.DMA((n, 7)), pltpu.SemaphoreType.DMA((n,))],
        has_side_effects=True,
    )(*arrs)
    return list(outs)


HBM_SPEC = pl.BlockSpec(memory_space=pltpu.HBM)
SEM_SPEC = pl.BlockSpec(memory_space=pltpu.SEMAPHORE)
DATAFLOW = pltpu.SideEffectType.DATAFLOW_SIDE_EFFECTING
PEERS = [(dx, dy, dc) for dx in (0, 1) for dy in (0, 1) for dc in (0, 1)][1:]


def _peer(flip):
    x, y, c = _mesh_pos()
    return tuple(1 - v if f else v for v, f in zip((x, y, c), flip))


def _slot(pos):
    return 4 * pos[0] + 2 * pos[1] + pos[2]


def _in_hbm(a):
    return pltpu.with_memory_space_constraint(a, pltpu.HBM)


def _direct_copies(srcs, lands, send_sems, recv_sems, scatter):
    me = _slot(_mesh_pos())
    copies = []
    for a in range(len(lands)):
        for k, flip in enumerate(PEERS):
            peer = _peer(flip)
            src = srcs[a].at[_slot(peer)] if scatter else lands[a].at[me]
            idx = a * len(PEERS) + k
            copies.append(pltpu.make_async_remote_copy(
                src_ref=src, dst_ref=lands[a].at[me], send_sem=send_sems.at[idx], recv_sem=recv_sems.at[idx],
                device_id=peer, device_id_type=MESH))
    return copies


def _send_start(srcs, lands, after, name):
    ns, nl = len(srcs), len(lands)
    scatter = ns > 0

    def body(*refs):
        src_refs, land_refs = refs[:ns], refs[ns:ns + nl]
        send_sems, recv_sems = refs[ns + nl + 1:ns + nl + 3]
        token = refs[-1]
        for cp in _direct_copies(src_refs, land_refs, send_sems, recv_sems, scatter):
            cp.start()
        token[...] = jnp.zeros_like(token)

    sem = pltpu.SemaphoreType.DMA((nl * len(PEERS),))
    outs = pl.pallas_call(
        body, name=name,
        out_shape=(sem, sem) + tuple(pltpu.HBM(a.shape, a.dtype) for a in list(srcs) + list(lands))
        + (_sds((8, LANES), F32),),
        in_specs=[HBM_SPEC] * (ns + nl) + [pl.BlockSpec(memory_space=pl.ANY)],
        out_specs=(SEM_SPEC, SEM_SPEC) + (HBM_SPEC,) * (ns + nl) + (pl.BlockSpec(memory_space=pltpu.VMEM),),
        input_output_aliases={i: 2 + i for i in range(ns + nl)},
        compiler_params=pltpu.CompilerParams(has_side_effects=DATAFLOW),
    )(*[_in_hbm(a) for a in list(srcs) + list(lands)], after)
    send_sems, recv_sems = outs[0], outs[1]
    return send_sems, recv_sems, list(outs[2:2 + ns]), list(outs[2 + ns:2 + ns + nl]), outs[-1]


def _send_wait(send_sems, recv_sems, srcs, lands, after, name):
    ns, nl = len(srcs), len(lands)
    scatter = ns > 0

    def body(*refs):
        src_refs, land_refs = refs[:ns], refs[ns:ns + nl]
        send_sems, recv_sems = refs[ns + nl:ns + nl + 2]
        copies = _direct_copies(src_refs, land_refs, send_sems, recv_sems, scatter)
        for cp in copies:
            cp.wait_send()
        for cp in copies:
            cp.wait_recv()

    outs = pl.pallas_call(
        body, name=name,
        out_shape=tuple(pltpu.HBM(a.shape, a.dtype) for a in list(srcs) + list(lands)),
        in_specs=[HBM_SPEC] * (ns + nl) + [SEM_SPEC, SEM_SPEC, pl.BlockSpec(memory_space=pl.ANY)],
        out_specs=(HBM_SPEC,) * (ns + nl),
        input_output_aliases={i: i for i in range(ns + nl)},
        compiler_params=pltpu.CompilerParams(has_side_effects=DATAFLOW),
    )(*srcs, *lands, send_sems, recv_sems, after)
    return list(outs[:ns]), list(outs[ns:])


def _row_tile(rows, cap=512):
    t = min(rows, cap)
    while rows % t or (t % 16 and t != rows):
        t -= 1
    return t


def _as2d(a):
    return a.reshape(-1, a.shape[-1])


def _cast_layer(w, layer, slot, name):
    _, rows, cols = w.shape
    tr = _row_tile(rows)

    def body(*refs):
        refs[-1][...] = refs[-2][...].astype(BF16)

    if slot is None:
        return _call(body, name=name, grid=(rows // tr,),
                     in_specs=[pl.BlockSpec((None, tr, cols), lambda i: (layer, i, 0))],
                     out_specs=pl.BlockSpec((tr, cols), lambda i: (i, 0)), out_shape=_sds((rows, cols), BF16))(w)
    return _call(body, name=name, grid=(rows // tr,), prefetch=True,
                 in_specs=[pl.BlockSpec((None, tr, cols), lambda i, s: (layer, i, 0))],
                 out_specs=pl.BlockSpec((None, tr, cols), lambda i, s: (s[0], i, 0)),
                 out_shape=_sds((N_DEV, rows, cols), BF16))(slot, w)


def _sum_adamw(slot, part, land, w, m, v, layer, name):
    _, rows, cols = land.shape
    tr = _row_tile(rows, 256)

    def body(s_ref, p_ref, l_ref, w_ref, m_ref, v_ref, g_ref, d_ref, nm_ref, nv_ref):
        own = p_ref[...]
        g = jnp.zeros((tr, cols), F32)
        for k in range(N_DEV):
            g = g + jnp.where(s_ref[0] == k, own, l_ref[k]).astype(F32)
        delta, nm, nv = _adamw_math(w_ref[...], g, m_ref[...], v_ref[...])
        g_ref[...] = g
        d_ref[...] = delta
        nm_ref[...] = nm
        nv_ref[...] = nv

    lay = pl.BlockSpec((None, tr, cols), lambda i, s: (layer, i, 0))
    out = pl.BlockSpec((tr, cols), lambda i, s: (i, 0))
    return _call(body, name=name, grid=(rows // tr,), prefetch=True,
                 in_specs=[pl.BlockSpec((None, tr, cols), lambda i, s: (s[0], i, 0)),
                           pl.BlockSpec((N_DEV, tr, cols), lambda i, s: (0, i, 0)), lay, lay, lay],
                 out_specs=[out] * 4, out_shape=[_sds((rows, cols), F32)] * 4)(slot, part, land, w, m, v)


def _adamw_math(w, g, m, v):
    m = ADAM_B1 * m + (1.0 - ADAM_B1) * g
    v = ADAM_B2 * v + (1.0 - ADAM_B2) * (g * g)
    m_hat = m / (1.0 - ADAM_B1 ** ADAM_STEP)
    v_hat = v / (1.0 - ADAM_B2 ** ADAM_STEP)
    delta = -ADAM_LR * (m_hat / (jnp.sqrt(v_hat) + ADAM_EPS) + ADAM_WD * w)
    return delta, m, v


def _small_adamw(gathered, w, m, v, name):
    def body(a_ref, w_ref, m_ref, v_ref, g_ref, d_ref, nm_ref, nv_ref):
        g = a_ref[0]
        for k in range(1, N_DEV):
            g = g + a_ref[k]
        delta, nm, nv = _adamw_math(w_ref[...], g, m_ref[...], v_ref[...])
        g_ref[...] = g
        d_ref[...] = delta
        nm_ref[...] = nm
        nv_ref[...] = nv

    return _call(body, name=name, out_shape=[_sds(w.shape, F32)] * 4)(gathered, w, m, v)


def _norm_matmul_cols(x, g, wg, layer, mode, name):
    t, d = x.shape
    nb = wg.shape[-1]
    tm = BIG_ROW_TILE
    pieces = nb // LANES

    def body(x_ref, g_ref, w_ref, y_ref, xn_ref):
        @pl.when(pl.program_id(1) == 0)
        def _():
            xn_ref[...] = _rms(x_ref[...], g_ref[...]).astype(BF16)

        y = _dot(xn_ref[...], w_ref[...])
        if mode == "heads":
            for p in range(pieces):
                y_ref[p] = y[:, p * LANES:(p + 1) * LANES]
        else:
            y_ref[...] = y.astype(BF16)

    if mode == "cols":
        y_shape, y_spec = _sds((t, N_DEV * nb), BF16), pl.BlockSpec((tm, nb), lambda i, j: (i, j))
    else:
        y_shape = _sds((N_DEV * pieces, t, LANES), F32)
        y_spec = pl.BlockSpec((pieces, tm, LANES), lambda i, j: (j, i, 0))
    return _call(
        body, name=name, grid=(t // tm, N_DEV),
        in_specs=[pl.BlockSpec((tm, d), lambda i, j: (i, 0)), pl.BlockSpec((1, d), lambda i, j: (0, 0)),
                  pl.BlockSpec((None, None, d, nb), lambda i, j: (j, layer, 0, 0))],
        out_specs=[y_spec, pl.BlockSpec((tm, d), lambda i, j: (i, 0))],
        out_shape=[y_shape, _sds((t, d), BF16)])(x, g, wg)


def _ffn_in_swiglu(x, g, wg, layer, name):
    t, d = x.shape
    fc = wg.shape[-1]
    tm = BIG_ROW_TILE

    def body(x_ref, g_ref, wg_ref, wu_ref, gate_ref, up_ref, a_ref, xn_ref):
        @pl.when(pl.program_id(1) == 0)
        def _():
            xn_ref[...] = _rms(x_ref[...], g_ref[...]).astype(BF16)

        xn = xn_ref[...]
        gate, up = _dot(xn, wg_ref[...]), _dot(xn, wu_ref[...])
        gate_ref[...] = gate.astype(BF16)
        up_ref[...] = up.astype(BF16)
        a_ref[...] = (gate * jax.nn.sigmoid(gate) * up).astype(BF16)

    chunk = pl.BlockSpec((None, tm, fc), lambda i, c: (c, i, 0))
    return _call(
        body, name=name, grid=(t // tm, 4),
        in_specs=[pl.BlockSpec((tm, d), lambda i, c: (i, 0)), pl.BlockSpec((1, d), lambda i, c: (0, 0)),
                  pl.BlockSpec((None, None, d, fc), lambda i, c: (c, layer, 0, 0)),
                  pl.BlockSpec((None, None, d, fc), lambda i, c: (c + 4, layer, 0, 0))],
        out_specs=[chunk, chunk, chunk, pl.BlockSpec((tm, d), lambda i, c: (i, 0))],
        out_shape=[_sds((4, t, fc), BF16)] * 3 + [_sds((t, d), BF16)])(x, g, wg, wg)


def _norm_matmul_heads(x, g, wg, layer, scale, name):
    t, d = x.shape
    tm = ROW_TILE
    hp = d // LANES

    def body(x_ref, g_ref, w_ref, y_ref, xn_ref):
        xn = _rms(x_ref[...], g_ref[...]).astype(BF16)
        xn_ref[...] = xn
        y = _dot(xn, w_ref[...].reshape(d, d)) * scale
        for p in range(hp):
            y_ref[p] = y[:, p * LANES:(p + 1) * LANES]

    return _call(
        body, name=name, grid=(t // tm,),
        in_specs=[pl.BlockSpec((tm, d), lambda i: (i, 0)), pl.BlockSpec((1, d), lambda i: (0, 0)),
                  pl.BlockSpec((N_DEV, None, d // N_DEV, d), lambda i: (0, layer, 0, 0))],
        out_specs=[pl.BlockSpec((hp, tm, LANES), lambda i: (0, i, 0)), pl.BlockSpec((tm, d), lambda i: (i, 0))],
        out_shape=[_sds((hp, t, LANES), F32), _sds((t, d), BF16)])(x, g, wg)


def _shift_down(u, halo, k, tm):
    row = lax.broadcasted_iota(jnp.int32, u.shape, 0)
    out = pltpu.roll(u, k, 0)
    for j in range(k):
        out = jnp.where(row == j, halo[halo.shape[0] - k + j:halo.shape[0] - k + j + 1, :], out)
    return out


def _shift_up(u, halo, k, tm):
    row = lax.broadcasted_iota(jnp.int32, u.shape, 0)
    out = pltpu.roll(u, tm - k, 0)
    for j in range(k):
        out = jnp.where(row == tm - k + j, halo[j:j + 1, :], out)
    return out


HALO = 16


def _conv_fwd(p, cw, name):
    t, d3 = p.shape
    d = d3 // 3
    tm = ROW_TILE
    hb = tm // HALO

    def body(p_ref, prev_ref, cw_ref, z_ref):
        i = pl.program_id(0)
        b = p_ref[:, 0:d].astype(F32)
        u = p_ref[:, d:2 * d].astype(F32) * p_ref[:, 2 * d:3 * d].astype(F32)
        keep = (i > 0).astype(F32)
        hu = prev_ref[:, d:2 * d].astype(F32) * prev_ref[:, 2 * d:3 * d].astype(F32) * keep
        uc = cw_ref[2:3, :] * u + cw_ref[1:2, :] * _shift_down(u, hu, 1, tm) + cw_ref[0:1, :] * _shift_down(u, hu, 2, tm)
        z_ref[...] = (b * uc).astype(BF16)

    return _call(
        body, name=name, grid=(t // tm,),
        in_specs=[pl.BlockSpec((tm, d3), lambda i: (i, 0)),
                  pl.BlockSpec((HALO, d3), lambda i: (jnp.maximum(i * hb - 1, 0), 0)),
                  pl.BlockSpec((3, d), lambda i: (0, 0))],
        out_specs=pl.BlockSpec((tm, d), lambda i: (i, 0)), out_shape=_sds((t, d), BF16))(p, p, cw)


def _matmul_norm_residual(a3, wg, layer, g, x_res, name):
    kc_n, t, kc = a3.shape
    d = wg.shape[-1]
    per = N_DEV // kc_n
    rows = wg.shape[2]
    tm = BIG_ROW_TILE

    def body(a_ref, w_ref, g_ref, x_ref, raw_ref, xo_ref, acc_ref):
        c = pl.program_id(1)
        part = _dot(a_ref[...], w_ref[...].reshape(per * rows, d))

        @pl.when(c == 0)
        def _():
            acc_ref[...] = part

        @pl.when(c > 0)
        def _():
            acc_ref[...] += part

        @pl.when(c == kc_n - 1)
        def _():
            raw = acc_ref[...]
            raw_ref[...] = raw
            xo_ref[...] = x_ref[...] + _rms(raw, g_ref[...])

    row_spec = pl.BlockSpec((tm, d), lambda i, c: (i, 0))
    return _call(
        body, name=name, grid=(t // tm, kc_n),
        in_specs=[pl.BlockSpec((None, tm, kc), lambda i, c: (c, i, 0)),
                  pl.BlockSpec((per, None, rows, d), lambda i, c: (c, layer, 0, 0)),
                  pl.BlockSpec((1, d), lambda i, c: (0, 0)), row_spec],
        out_specs=[row_spec, row_spec], out_shape=[_sds((t, d), F32)] * 2,
        scratch_shapes=[pltpu.VMEM((tm, d), F32)])(a3, wg, g, x_res)


def _alibi_slopes(n_heads):
    hh = np.arange(n_heads, dtype=np.float32) + 1.0
    s = np.power(2.0, -8.0 * hh / n_heads).astype(np.float32)
    return jnp.asarray(np.repeat(s.reshape(n_heads // 2, 2, 1), 2 * ATT_BLOCK, axis=2))


def _band_bias(sl_ref, dil):
    u = lax.broadcasted_iota(jnp.int32, (ATT_BLOCK, 2 * ATT_BLOCK), 0)
    kk = lax.broadcasted_iota(jnp.int32, (ATT_BLOCK, 2 * ATT_BLOCK), 1)
    delta = u + ATT_BLOCK - kk
    valid = (delta >= 0) & (delta <= ATT_BLOCK)
    dist = (delta * dil).astype(F32)
    rows = [jnp.where(valid, -sl_ref[hd:hd + 1, :] * dist, NEG) for hd in range(2)]
    return jnp.concatenate(rows, axis=0)


def _stack_heads(a):
    lane = lax.broadcasted_iota(jnp.int32, a.shape, 1)
    return jnp.concatenate([jnp.where(lane < HEAD_DIM, a, 0.0), jnp.where(lane >= HEAD_DIM, a, 0.0)], axis=0).astype(BF16)


def _unstack_heads(a2):
    top, bot = a2[:ATT_BLOCK], a2[ATT_BLOCK:]
    lane = lax.broadcasted_iota(jnp.int32, top.shape, 1)
    return jnp.where(lane < HEAD_DIM, top, bot)


def _rows_to_lanes(a0, a1):
    eye = lax.broadcasted_iota(jnp.int32, a0.shape, 0) == lax.broadcasted_iota(jnp.int32, a0.shape, 1)
    return jnp.concatenate([jnp.sum(jnp.where(eye, a, 0.0), axis=0, keepdims=True) for a in (a0, a1)], axis=1)


def _fill_bias_t(sl_ref, bias_ref):
    kk = lax.broadcasted_iota(jnp.int32, (2 * ATT_BLOCK, 2 * ATT_BLOCK), 0)
    lane = lax.broadcasted_iota(jnp.int32, (2 * ATT_BLOCK, 2 * ATT_BLOCK), 1)
    delta = lane % ATT_BLOCK + ATT_BLOCK - kk
    valid = (delta >= 0) & (delta <= ATT_BLOCK)
    slope = jnp.concatenate([sl_ref[0:1, :ATT_BLOCK], sl_ref[1:2, :ATT_BLOCK]], axis=1)
    for gi, dil in enumerate(DILATIONS):
        bias = jnp.where(valid, -slope * (delta * dil).astype(F32), NEG)
        bias_ref[2 * gi] = bias
        bias_ref[2 * gi + 1] = jnp.where(kk < ATT_BLOCK, NEG, bias)


def _fill_bias(sl_ref, bias_ref):
    kk = lax.broadcasted_iota(jnp.int32, (2 * ATT_BLOCK, 2 * ATT_BLOCK), 1)
    for gi, dil in enumerate(DILATIONS):
        bias = _band_bias(sl_ref, dil)
        bias_ref[2 * gi] = bias
        bias_ref[2 * gi + 1] = jnp.where(kk < ATT_BLOCK, NEG, bias)


def _attention_fwd(q, kv, slopes, name):
    hp, t, _ = q.shape
    ns = t // SUPER
    nd = len(DILATIONS)

    def body(sl_ref, q_ref, kc_ref, kp_ref, vc_ref, vp_ref, o_ref, lse_ref, kw_ref, vw_ref, og_ref, lg_ref, bias_ref):
        n = pl.program_id(1)
        kw_ref[0:SUPER, :] = kp_ref[...]
        kw_ref[SUPER:, :] = kc_ref[...]
        vw_ref[0:SUPER, :] = vp_ref[...]
        vw_ref[SUPER:, :] = vc_ref[...]

        @pl.when(n == 0)
        def _():
            _fill_bias(sl_ref, bias_ref)

        for gi, dil in enumerate(DILATIONS):

            def block(idx, carry, gi=gi, dil=dil):
                r, b = idx % dil, idx // dil
                qs = b * (ATT_BLOCK * dil) + r
                ks = SUPER + (b - 1) * (ATT_BLOCK * dil) + r
                first = jnp.logical_and(n == 0, b == 0).astype(jnp.int32)
                q2 = _stack_heads(q_ref[pl.ds(qs, ATT_BLOCK, stride=dil), :])
                kb = kw_ref[pl.ds(ks, 2 * ATT_BLOCK, stride=dil), :].astype(BF16)
                vb = vw_ref[pl.ds(ks, 2 * ATT_BLOCK, stride=dil), :].astype(BF16)
                s = _dot_nt(q2, kb) + bias_ref[2 * gi + first]
                m = jnp.max(s, axis=-1, keepdims=True)
                p = jnp.exp(s - m).astype(BF16)
                ol = _dot(p, jnp.concatenate([vb, jnp.ones_like(vb)], axis=1))
                l = ol[:, LANES:]
                o2 = ol[:, :LANES] / l
                lse2 = m + jnp.log(l)
                og_ref[gi, pl.ds(qs, ATT_BLOCK, stride=dil), :] = _unstack_heads(o2)
                lg_ref[gi, pl.ds(qs, ATT_BLOCK, stride=dil), :] = _unstack_heads(lse2)
                return carry

            lax.fori_loop(0, SUPER // ATT_BLOCK, block, 0, unroll=ATT_UNROLL_FWD)
        lg =[lg_ref[gi] for gi in range(nd)]
        top = functools.reduce(jnp.maximum, lg)
        ws = [jnp.exp(x - top) for x in lg]
        tot = functools.reduce(jnp.add, ws)
        lse_ref[...] = top + jnp.log(tot)
        acc = ws[0] * og_ref[0]
        for gi in range(1, nd):
            acc = acc + ws[gi] * og_ref[gi]
        o_ref[...] = (acc / tot).astype(BF16)

    cur = lambda off: pl.BlockSpec((None, SUPER, LANES), lambda h, n: (h + off, n, 0))
    prev = lambda off: pl.BlockSpec((None, SUPER, LANES), lambda h, n: (h + off, jnp.maximum(n - 1, 0), 0))
    return _call(
        body, name=name, grid=(hp, ns),
        in_specs=[pl.BlockSpec((None, 2, 2 * ATT_BLOCK), lambda h, n: (h, 0, 0)), cur(0), cur(0), prev(0), cur(hp), prev(hp)],
        out_specs=[pl.BlockSpec((SUPER, LANES), lambda h, n: (n, h)), cur(0)],
        out_shape=[_sds((t, hp * LANES), BF16), _sds((hp, t, LANES), F32)],
        scratch_shapes=[pltpu.VMEM((2 * SUPER, LANES), F32), pltpu.VMEM((2 * SUPER, LANES), F32),
                        pltpu.VMEM((nd, SUPER, LANES), F32), pltpu.VMEM((nd, SUPER, LANES), F32),
                        pltpu.VMEM((2 * nd, 2 * ATT_BLOCK, 2 * ATT_BLOCK), F32)],
    )(slopes, q, kv, kv, kv, kv)


def _attention_bwd(q, kv, o, lse, d_o, slopes, q_scale, name):
    hp, t, _ = q.shape
    ns = t // SUPER

    def body(sl_ref, q_ref, kc_ref, kp_ref, vc_ref, vp_ref, o_ref, lse_ref, do_ref,
             dq_ref, dk_ref, dv_ref, kw_ref, vw_ref, dkw_ref, dvw_ref, st_ref, bias_ref):
        n = pl.program_id(1)

        @pl.when(n == 0)
        def _():
            dkw_ref[...] = jnp.zeros_like(dkw_ref)
            dvw_ref[...] = jnp.zeros_like(dvw_ref)

        @pl.when(n > 0)
        def _():
            dkw_ref[0:SUPER, :] = dkw_ref[SUPER:, :]
            dvw_ref[0:SUPER, :] = dvw_ref[SUPER:, :]
            dkw_ref[SUPER:, :] = jnp.zeros((SUPER, LANES), F32)
            dvw_ref[SUPER:, :] = jnp.zeros((SUPER, LANES), F32)

        @pl.when(n < ns)
        def _():
            kw_ref[0:SUPER, :] = kp_ref[...]
            kw_ref[SUPER:, :] = kc_ref[...]
            vw_ref[0:SUPER, :] = vp_ref[...]
            vw_ref[SUPER:, :] = vc_ref[...]
            prod = do_ref[...] * o_ref[...].astype(F32)
            lane = lax.broadcasted_iota(jnp.int32, prod.shape, 1)
            zero = jnp.zeros((SUPER, LANES), F32)
            st_ref[0] = zero + jnp.sum(jnp.where(lane < HEAD_DIM, prod, 0.0), axis=-1, keepdims=True)
            st_ref[1] = zero + jnp.sum(jnp.where(lane >= HEAD_DIM, prod, 0.0), axis=-1, keepdims=True)
            lse = lse_ref[...]
            swapped = pltpu.roll(lse, HEAD_DIM, 1)
            st_ref[2] = jnp.where(lane < HEAD_DIM, lse, swapped)
            st_ref[3] = jnp.where(lane >= HEAD_DIM, lse, swapped)
            dq_ref[...] = jnp.zeros_like(dq_ref)

            @pl.when(n == 0)
            def _():
                _fill_bias_t(sl_ref, bias_ref)

            for gi, dil in enumerate(DILATIONS):

                def block(idx, carry, gi=gi, dil=dil):
                    r, b = idx % dil, idx // dil
                    qs = b * (ATT_BLOCK * dil) + r
                    ks = SUPER + (b - 1) * (ATT_BLOCK * dil) + r
                    first = jnp.logical_and(n == 0, b == 0).astype(jnp.int32)
                    rows = pl.ds(qs, ATT_BLOCK, stride=dil)
                    keys = pl.ds(ks, 2 * ATT_BLOCK, stride=dil)
                    q2 = _stack_heads(q_ref[rows, :])
                    do2 = _stack_heads(do_ref[rows, :])
                    kb = kw_ref[keys, :].astype(BF16)
                    vb = vw_ref[keys, :].astype(BF16)
                    dd = _rows_to_lanes(st_ref[0, rows, :], st_ref[1, rows, :])
                    lse_b = _rows_to_lanes(st_ref[2, rows, :], st_ref[3, rows, :])
                    p = jnp.exp(_dot_nt(kb, q2) + bias_ref[2 * gi + first] - lse_b)
                    ds = (p * (_dot_nt(vb, do2) - dd)).astype(BF16)
                    dvw_ref[keys, :] += _dot(p.astype(BF16), do2)
                    dkw_ref[keys, :] += _dot(ds, q2)
                    dq_ref[rows, :] += _unstack_heads(_dot_tn(ds, kb)) * q_scale
                    return carry

                lax.fori_loop(0, SUPER // ATT_BLOCK, block, 0, unroll=ATT_UNROLL_BWD)

        dk_ref[...] = dkw_ref[0:SUPER, :]
        dv_ref[...] = dvw_ref[0:SUPER, :]

    last = ns - 1
    cur = lambda off: pl.BlockSpec((None, SUPER, LANES), lambda h, n: (h + off, jnp.minimum(n, last), 0))
    prev = lambda off: pl.BlockSpec((None, SUPER, LANES), lambda h, n: (h + off, jnp.clip(n - 1, 0, last), 0))
    nat = pl.BlockSpec((SUPER, LANES), lambda h, n: (jnp.minimum(n, last), h))
    late = pl.BlockSpec((None, SUPER, LANES), lambda h, n: (h, jnp.maximum(n - 1, 0), 0))
    dq, dk, dv = _call(
        body, name=name, grid=(hp, ns + 1),
        in_specs=[pl.BlockSpec((None, 2, 2 * ATT_BLOCK), lambda h, n: (h, 0, 0)), cur(0), cur(0), prev(0), cur(hp), prev(hp),
                  nat, cur(0), nat],
        out_specs=[cur(0), late, late],
        out_shape=[_sds((hp, t, LANES), F32)] * 3,
        scratch_shapes=[pltpu.VMEM((2 * SUPER, LANES), F32)] * 4 + [
            pltpu.VMEM((4, SUPER, LANES), F32), pltpu.VMEM((2 * len(DILATIONS), 2 * ATT_BLOCK, 2 * ATT_BLOCK), F32)],
    )(slopes, q, kv, kv, kv, kv, o, lse, d_o)
    return dq, dk, dv


def _loss_head(y, target, raw, g, name):
    t, d = y.shape
    tm = ROW_TILE

    def body(y_ref, t_ref, raw_ref, g_ref, sq_ref, dy_ref, draw_ref, dg_ref):
        i = pl.program_id(0)
        err = y_ref[...] - t_ref[...]
        dy = err * (1.0 / d)
        dy_ref[...] = dy
        draw, dg = _rms_bwd(raw_ref[...], g_ref[...], dy)
        draw_ref[...] = draw.astype(BF16)
        sq = jnp.zeros((8, LANES), F32) + jnp.sum(err * err)

        @pl.when(i == 0)
        def _():
            sq_ref[...] = sq
            dg_ref[...] = dg

        @pl.when(i > 0)
        def _():
            sq_ref[...] += sq
            dg_ref[...] += dg

    row = pl.BlockSpec((tm, d), lambda i: (i, 0))
    vec = pl.BlockSpec((1, d), lambda i: (0, 0))
    return _call(
        body, name=name, grid=(t // tm,), in_specs=[row, row, row, vec],
        out_specs=[pl.BlockSpec((8, LANES), lambda i: (0, 0)), row, row, vec],
        out_shape=[_sds((8, LANES), F32), _sds((t, d), F32), _sds((t, d), BF16), _sds((1, d), F32)])(y, target, raw, g)


def _bwd_matmul_norms(a_specs, a_args, a_tile, n_steps, w_spec, w_arg, w_mat, xa, ga, resid, xb, gb, name):
    t, d = xa.shape
    tm = BWD_ROW_TILE
    na = len(a_specs)
    second = xb is not None
    per = 2 if n_steps % 2 == 0 else 1
    n_steps //= per

    def blocks_of(spec, k):
        return pl.BlockSpec(spec.block_shape, lambda i, j: spec.index_map(i, per * j + k))

    def body(*refs):
        a_refs, w_refs = refs[:per * na], refs[per * na:per * na + per]
        xa_ref, ga_ref, res_ref = refs[per * na + per:per * na + per + 3]
        rest = refs[per * na + per + 3:]
        if second:
            xb_ref, gb_ref, dx_ref, d2_ref, dga_ref, dgb_ref, acc_ref = rest
        else:
            dx_ref, dga_ref, acc_ref = rest
        i, j = pl.program_id(0), pl.program_id(1)
        part = None
        for k in range(per):
            term = _dot_nt(a_tile(per * j + k, *a_refs[k * na:(k + 1) * na]), w_mat(w_refs[k]))
            part = term if part is None else part + term

        @pl.when(j == 0)
        def _():
            acc_ref[...] = part

        @pl.when(j > 0)
        def _():
            acc_ref[...] += part

        @pl.when(j == n_steps - 1)
        def _():
            da, dga = _rms_bwd(xa_ref[...], ga_ref[...], acc_ref[...])
            dx = res_ref[...] + da
            dx_ref[...] = dx
            if second:
                d2, dgb = _rms_bwd(xb_ref[...], gb_ref[...], dx)
                d2_ref[...] = d2.astype(BF16)

            @pl.when(i == 0)
            def _():
                dga_ref[...] = dga
                if second:
                    dgb_ref[...] = dgb

            @pl.when(i > 0)
            def _():
                dga_ref[...] += dga
                if second:
                    dgb_ref[...] += dgb

    row = pl.BlockSpec((tm, d), lambda i, j: (i, 0))
    vec = pl.BlockSpec((1, d), lambda i, j: (0, 0))
    in_specs = [blocks_of(sp, k) for k in range(per) for sp in a_specs] + [blocks_of(w_spec, k) for k in range(per)]
    in_specs += [row, vec, row]
    args = list(a_args) * per + [w_arg] * per + [xa, ga, resid]
    if second:
        in_specs += [row, vec]
        args += [xb, gb]
        out_specs = [row, row, vec, vec]
        out_shape = [_sds((t, d), F32), _sds((t, d), BF16), _sds((1, d), F32), _sds((1, d), F32)]
    else:
        out_specs = [row, vec]
        out_shape = [_sds((t, d), F32), _sds((1, d), F32)]
    return _call(body, name=name, grid=(t // tm, n_steps), in_specs=in_specs, out_specs=out_specs,
                 out_shape=out_shape, scratch_shapes=[pltpu.VMEM((tm, d), F32)])(*args)


def _heads_to_rows(*refs):
    hp = refs[0].shape[0]
    cols = []
    for p in range(hp):
        v = refs[0][p]
        for r in refs[1:]:
            v = v + r[p]
        cols.append(v)
    return jnp.concatenate(cols, axis=-1).astype(BF16)


def _matmul_nt_rows(a, wg, layer, out_dtype, name):
    t, d = a.shape
    tm = ROW_TILE

    def body(a_ref, w_ref, o_ref):
        o_ref[...] = _dot_nt(a_ref[...], w_ref[...].reshape(d, d)).astype(out_dtype)

    row = pl.BlockSpec((tm, d), lambda i: (i, 0))
    return _call(body, name=name, grid=(t // tm,),
                 in_specs=[row, pl.BlockSpec((N_DEV, None, d // N_DEV, d), lambda i: (0, layer, 0, 0))],
                 out_specs=row, out_shape=_sds((t, d), out_dtype))(a, wg)


def _swiglu_bwd(d_ff, wg, layer, gate, up, name):
    t, d = d_ff.shape
    fc = gate.shape[-1]
    rows = wg.shape[2]
    tm = BIG_ROW_TILE

    def body(df_ref, w_ref, g_ref, u_ref, dh_ref):
        da = _dot_nt(df_ref[...], w_ref[...].reshape(2 * rows, d))
        gate, up = g_ref[...].astype(F32), u_ref[...].astype(F32)
        sig = jax.nn.sigmoid(gate)
        dh_ref[0] = (da * up * (sig * (1.0 + gate * (1.0 - sig)))).astype(BF16)
        dh_ref[1] = (da * (gate * sig)).astype(BF16)

    return _call(
        body, name=name, grid=(t // tm, 4),
        in_specs=[pl.BlockSpec((tm, d), lambda i, c: (i, 0)),
                  pl.BlockSpec((2, None, rows, d), lambda i, c: (c, layer, 0, 0)),
                  pl.BlockSpec((None, tm, fc), lambda i, c: (c, i, 0)),
                  pl.BlockSpec((None, tm, fc), lambda i, c: (c, i, 0))],
        out_specs=pl.BlockSpec((None, 2, tm, fc), lambda i, c: (c, 0, i, 0)),
        out_shape=_sds((4, 2, t, fc), BF16))(d_ff, wg, gate, up)


def _conv_bwd(p, d_z, cw, name):
    t, d3 = p.shape
    d = d3 // 3
    tm = ROW_TILE
    hb = tm // HALO
    nt = t // tm

    def body(p_ref, prev_ref, next_ref, dz_ref, dzn_ref, cw_ref, dp_ref, dcw_ref):
        i = pl.program_id(0)
        b = p_ref[:, 0:d].astype(F32)
        c = p_ref[:, d:2 * d].astype(F32)
        h = p_ref[:, 2 * d:3 * d].astype(F32)
        u = c * h
        hu = prev_ref[:, d:2 * d].astype(F32) * prev_ref[:, 2 * d:3 * d].astype(F32) * (i > 0).astype(F32)
        u1, u2 = _shift_down(u, hu, 1, tm), _shift_down(u, hu, 2, tm)
        uc = cw_ref[2:3, :] * u + cw_ref[1:2, :] * u1 + cw_ref[0:1, :] * u2
        dz = dz_ref[...].astype(F32)
        duc = dz * b
        dn = dzn_ref[...].astype(F32) * next_ref[:, 0:d].astype(F32) * (i < nt - 1).astype(F32)
        du = cw_ref[2:3, :] * duc + cw_ref[1:2, :] * _shift_up(duc, dn, 1, tm) + cw_ref[0:1, :] * _shift_up(duc, dn, 2, tm)
        dp_ref[:, 0:d] = (dz * uc).astype(BF16)
        dp_ref[:, d:2 * d] = (du * h).astype(BF16)
        dp_ref[:, 2 * d:3 * d] = (du * c).astype(BF16)
        dcw = jnp.concatenate([jnp.sum(duc * u2, axis=0, keepdims=True), jnp.sum(duc * u1, axis=0, keepdims=True),
                               jnp.sum(duc * u, axis=0, keepdims=True)], axis=0)

        @pl.when(i == 0)
        def _():
            dcw_ref[...] = dcw

        @pl.when(i > 0)
        def _():
            dcw_ref[...] += dcw

    last_halo = t // HALO - 1
    return _call(
        body, name=name, grid=(nt,),
        in_specs=[pl.BlockSpec((tm, d3), lambda i: (i, 0)),
                  pl.BlockSpec((HALO, d3), lambda i: (jnp.maximum(i * hb - 1, 0), 0)),
                  pl.BlockSpec((HALO, d3), lambda i: (jnp.minimum((i + 1) * hb, last_halo), 0)),
                  pl.BlockSpec((tm, d), lambda i: (i, 0)),
                  pl.BlockSpec((HALO, d), lambda i: (jnp.minimum((i + 1) * hb, last_halo), 0)),
                  pl.BlockSpec((3, d), lambda i: (0, 0))],
        out_specs=[pl.BlockSpec((tm, d3), lambda i: (i, 0)), pl.BlockSpec((3, d), lambda i: (0, 0))],
        out_shape=[_sds((t, d3), BF16), _sds((3, d), F32)])(p, p, p, d_z, d_z, cw)


def _grad_weight(a_specs, a_args, a_tile, b_specs, b_args, b_tile, n_out, acc_shape, out_spec, out_shape, t, name):
    tt = BIG_ROW_TILE
    na, nb = len(a_specs), len(b_specs)

    def body(*refs):
        a_refs, b_refs = refs[:na], refs[na:na + nb]
        o_ref, acc_ref = refs[na + nb:]
        s = pl.program_id(1)
        part = _dot_tn(a_tile(pl.program_id(0), *a_refs), b_tile(pl.program_id(0), *b_refs))

        @pl.when(s == 0)
        def _():
            acc_ref[...] = part

        @pl.when(s > 0)
        def _():
            acc_ref[...] += part

        @pl.when(s == t // tt - 1)
        def _():
            o_ref[...] = acc_ref[...].astype(BF16).reshape(o_ref.shape)

    return _call(body, name=name, grid=(n_out, t // tt), in_specs=list(a_specs) + list(b_specs), out_specs=out_spec,
                 out_shape=out_shape, scratch_shapes=[pltpu.VMEM(acc_shape, F32)])(*a_args, *b_args)


def _ident(*args):
    return args[-1][...]


def _heads_tile(j, *refs):
    return _heads_to_rows(*refs)


def kernel(x, norm_g, conv_in_w, conv_w, conv_out_w, kv_norm_g, kv_w, q_w, o_w, ffn_in_w, ffn_out_w, loss_target, m_norm_g, m_conv_in_w, m_conv_w, m_conv_out_w, m_kv_norm_g, m_kv_w, m_q_w, m_o_w, m_ffn_in_w, m_ffn_out_w, v_norm_g, v_conv_in_w, v_conv_w, v_conv_out_w, v_kv_norm_g, v_kv_w, v_q_w, v_o_w, v_ffn_in_w, v_ffn_out_w):
    x0 = x[0]
    target = loss_target[0]
    t, d = x0.shape
    depth = norm_g.shape[0]
    n_a = conv_in_w.shape[0]
    n_b = q_w.shape[0]
    hp = d // LANES
    tm, tg = BWD_ROW_TILE, BIG_ROW_TILE
    assert t % SUPER == 0 and d % LANES == 0 and depth == n_a + n_b
    dev = 4 * lax.axis_index("x") + 2 * lax.axis_index("y") + lax.axis_index("c")

    n_small = 4 * depth + 3 * n_a
    small_rows = -(-(n_small + 1) // 8) * 8
    small_local = jnp.concatenate([norm_g.reshape(4 * depth, -1), conv_w.reshape(3 * n_a, -1),
                                   jnp.zeros((small_rows - n_small, norm_g.shape[-1]), F32)], axis=0)
    big = {"conv_in_w": conv_in_w, "conv_out_w": conv_out_w, "kv_w": kv_w[None], "q_w": q_w, "o_w": o_w,
           "ffn_in_w": ffn_in_w, "ffn_out_w": ffn_out_w}
    names = list(big)

    def group(layer):
        if layer < n_a:
            return [("conv_in_w", layer), ("conv_out_w", layer), ("ffn_in_w", layer), ("ffn_out_w", layer)]
        j = layer - n_a
        return ([("kv_w", 0)] if j == 0 else []) + [("q_w", j), ("o_w", j), ("ffn_in_w", layer), ("ffn_out_w", layer)]

    slot = dev.astype(jnp.int32).reshape(1)
    first = _all_gather([small_local] + [_cast_layer(big[k], i, None, f"cast_{k}_{i}") for k, i in group(0)], "gather_weights")
    small_all = first[0].transpose(1, 0, 2).reshape(small_rows, d)
    wl = {key: a[:, None] for key, a in zip(group(0), first[1:])}
    def gather_start(layer, after):
        lands = [_cast_layer(big[k], i, slot, f"cast_{k}_{i}") for k, i in group(layer)]
        send_sems, recv_sems, _, lands, tok = _send_start([], lands, after, f"gather_start_l{layer}")
        return (send_sems, recv_sems, lands), tok[0, 0]

    in_flight, token = gather_start(1, small_all)
    W = lambda k, i: (wl[(k, i)], 0)
    gain = lambda layer, k: small_all[4 * layer + k][None]
    taps = lambda layer: small_all[4 * depth + 3 * layer: 4 * depth + 3 * layer + 3]
    g_kv = kv_norm_g[None]
    slopes = _alibi_slopes(d // HEAD_DIM)
    fc = big["ffn_in_w"].shape[-1]
    cb = big["conv_in_w"].shape[-1]
    kvb = big["kv_w"].shape[-1]
    q_scale = HEAD_DIM ** -0.5

    saved = []
    kv = kvn = None
    xs = x0
    for layer in range(depth):
        tag = f"_l{layer}"
        if layer > 0:
            send_sems, recv_sems, lands = in_flight
            _, lands = _send_wait(send_sems, recv_sems, [], lands, xs, f"gather_wait_l{layer}")
            wl.update({key: a[:, None] for key, a in zip(group(layer), lands)})
            if layer + 1 < depth:
                in_flight, token = gather_start(layer + 1, xs)
        s = {"x_in": xs}
        g0 = gain(layer, 0) + token if layer + 1 < depth else gain(layer, 0)
        if layer < n_a:
            s["p"], s["xn"] = _norm_matmul_cols(xs, g0, *W("conv_in_w", layer), "cols", "conv_in" + tag)
            s["z"] = _conv_fwd(s["p"], taps(layer), "conv" + tag)
            s["mix"], x_mid = _matmul_norm_residual(s["z"][None], *W("conv_out_w", layer), gain(layer, 1), xs, "conv_out" + tag)
        else:
            j = layer - n_a
            if kv is None:
                kv, kvn = _norm_matmul_cols(xs, g_kv, *W("kv_w", 0), "heads", "kv_proj")
            s["q"], s["xn"] = _norm_matmul_heads(xs, g0, *W("q_w", j), q_scale, "q_proj" + tag)
            s["o"], s["lse"] = _attention_fwd(s["q"], kv, slopes, "attention" + tag)
            s["mix"], x_mid = _matmul_norm_residual(s["o"][None], *W("o_w", j), gain(layer, 1), xs, "o_proj" + tag)
        s["x_mid"] = x_mid
        s["gate"], s["up"], s["a"], s["fn"] = _ffn_in_swiglu(x_mid, gain(layer, 2), *W("ffn_in_w", layer), "ffn_in" + tag)
        s["ff"], xs = _matmul_norm_residual(s["a"], *W("ffn_out_w", layer), gain(layer, 3), x_mid, "ffn_out" + tag)
        saved.append(s)

    last = saved[-1]
    sq, dx_out, d_ff, dg3 = _loss_head(xs, target, last["ff"], gain(depth - 1, 3), "loss_head")
    loss = lax.psum(sq[0, 0] * (0.5 / d), ("x", "y", "c"))

    dgain = {(depth - 1, 3): dg3}
    dtaps = {}
    grads = {k: [None] * big[k].shape[0] for k in names}
    dkv_parts = []
    scattering = []

    def scatter_start(keys, tag):
        parts = [grads[k][i] for k, i in keys]
        zones = [lax.empty(p.shape, p.dtype) for p in parts]
        send_sems, recv_sems, parts, zones, tok = _send_start(parts, zones, small_all, "scatter_start" + tag)
        scattering.append((keys, tag, send_sems, recv_sems, parts, zones))
        return tok[0, 0]

    for layer in reversed(range(depth)):
        tag = f"_l{layer}"
        s = saved[layer]
        dh = _swiglu_bwd(d_ff, *W("ffn_out_w", layer), s["gate"], s["up"], "swiglu_bwd" + tag)
        rows_out = big["ffn_out_w"].shape[1]
        grads["ffn_out_w"][layer] = _grad_weight(
            [pl.BlockSpec((None, tg, fc), lambda c, i: (c, i, 0))], [s["a"]], _ident,
            [pl.BlockSpec((tg, d), lambda c, i: (i, 0))], [d_ff], _ident,
            4, (fc, d), pl.BlockSpec((2, rows_out, d), lambda c, i: (c, 0, 0)), _sds((N_DEV, rows_out, d), BF16), t,
            "grad_ffn_out" + tag)
        grads["ffn_in_w"][layer] = _grad_weight(
            [pl.BlockSpec((tg, d), lambda j, i: (i, 0))], [s["fn"]], _ident,
            [pl.BlockSpec((None, None, tg, fc), lambda j, i: (j % 4, j // 4, i, 0))], [dh], _ident,
            N_DEV, (d, fc), pl.BlockSpec((None, d, fc), lambda j, i: (j, 0, 0)), _sds((N_DEV, d, fc), BF16), t,
            "grad_ffn_in" + tag)
        tok = scatter_start([("ffn_in_w", layer), ("ffn_out_w", layer)], "_ffn" + tag)
        dx_mid, d_mix, dg2, dg1 = _bwd_matmul_norms(
            [pl.BlockSpec((None, None, tm, fc), lambda i, j: (j % 4, j // 4, i, 0))], [dh], _ident, N_DEV,
            pl.BlockSpec((None, None, d, fc), lambda i, j: (j, 0, 0, 0)), W("ffn_in_w", layer)[0], _ident,
            s["x_mid"], gain(layer, 2) + tok, dx_out, s["mix"], gain(layer, 1), "ffn_in_bwd" + tag)
        dgain[(layer, 2)], dgain[(layer, 1)] = dg2, dg1
        full_rows = pl.BlockSpec((N_DEV, d // N_DEV, d), lambda j, i: (0, 0, 0))
        rows_w = lambda wname, idx: (pl.BlockSpec((N_DEV, None, d // N_DEV, d), lambda i, j: (0, 0, 0, 0)), W(wname, idx)[0],
                                     lambda w_ref: w_ref[...].reshape(d, d))
        if layer < n_a:
            d_z = _matmul_nt_rows(d_mix, *W("conv_out_w", layer), BF16, "conv_out_bwd" + tag)
            grads["conv_out_w"][layer] = _grad_weight(
                [pl.BlockSpec((tg, d), lambda j, i: (i, 0))], [s["z"]], _ident,
                [pl.BlockSpec((tg, d), lambda j, i: (i, 0))], [d_mix], _ident,
                1, (d, d), full_rows, _sds((N_DEV, d // N_DEV, d), BF16), t, "grad_conv_out" + tag)
            d_p, dtaps[layer] = _conv_bwd(s["p"], d_z, taps(layer), "conv_bwd" + tag)
            grads["conv_in_w"][layer] = _grad_weight(
                [pl.BlockSpec((tg, d), lambda j, i: (i, 0))], [s["xn"]], _ident,
                [pl.BlockSpec((tg, cb), lambda j, i: (i, j))], [d_p], _ident,
                N_DEV, (d, cb), pl.BlockSpec((None, d, cb), lambda j, i: (j, 0, 0)), _sds((N_DEV, d, cb), BF16), t,
                "grad_conv_in" + tag)
            a_specs, a_args, a_tile, n_steps = [pl.BlockSpec((tm, cb), lambda i, j: (i, j))], [d_p], _ident, N_DEV
            w_spec = pl.BlockSpec((None, None, d, cb), lambda i, j: (j, 0, 0, 0))
            w_arg, w_mat = W("conv_in_w", layer)[0], _ident
            resid = dx_mid
        else:
            j_b = layer - n_a
            d_o = _matmul_nt_rows(d_mix, *W("o_w", j_b), F32, "o_proj_bwd" + tag)
            grads["o_w"][j_b] = _grad_weight(
                [pl.BlockSpec((tg, d), lambda j, i: (i, 0))], [s["o"]], _ident,
                [pl.BlockSpec((tg, d), lambda j, i: (i, 0))], [d_mix], _ident,
                1, (d, d), full_rows, _sds((N_DEV, d // N_DEV, d), BF16), t, "grad_o" + tag)
            dq, dk, dv = _attention_bwd(s["q"], kv, s["o"], s["lse"], d_o, slopes, q_scale, "attention_bwd" + tag)
            dkv_parts.append((dk, dv))
            heads_spec = pl.BlockSpec((hp, tg, LANES), lambda j, i: (0, i, 0))
            grads["q_w"][j_b] = _grad_weight(
                [pl.BlockSpec((tg, d), lambda j, i: (i, 0))], [s["xn"]], _ident,
                [heads_spec], [dq], _heads_tile,
                1, (d, d), full_rows, _sds((N_DEV, d // N_DEV, d), BF16), t, "grad_q" + tag)
            a_specs, a_args, a_tile, n_steps = [pl.BlockSpec((hp, tm, LANES), lambda i, j: (0, i, 0))], [dq], _heads_tile, 1
            w_spec, w_arg, w_mat = rows_w("q_w", j_b)
            resid = dx_mid
            if layer == n_a:
                pieces = kvb // LANES
                halves = []
                for src in (0, 1):
                    halves.append([part[src] for part in dkv_parts])
                n_half = len(dkv_parts)
                kv_args = [arr for src in (0, 1) for arr in halves[src]]

                def kv_block(src, j):
                    return jnp.where((j // 4) == src, j % 4, 0)

                def kv_tile(j, *refs):
                    keys = _heads_to_rows(*refs[:n_half])
                    vals = _heads_to_rows(*refs[n_half:])
                    return jnp.where(j < 4, keys, vals)

                kv_specs = [pl.BlockSpec((pieces, tm, LANES), functools.partial(lambda i, j, src: (kv_block(src, j), i, 0), src=src))
                            for src in (0, 1) for _ in range(n_half)]
                resid, dgain["kv"] = _bwd_matmul_norms(
                    kv_specs, kv_args, kv_tile, N_DEV,
                    pl.BlockSpec((None, None, d, kvb), lambda i, j: (j, 0, 0, 0)), W("kv_w", 0)[0], _ident,
                    s["x_in"], g_kv, dx_mid, None, None, "kv_proj_bwd")
                kv_b_specs = [pl.BlockSpec((pieces, tg, LANES), functools.partial(lambda j, i, src: (kv_block(src, j), i, 0), src=src))
                              for src in (0, 1) for _ in range(n_half)]
                grads["kv_w"][0] = _grad_weight(
                    [pl.BlockSpec((tg, d), lambda j, i: (i, 0))], [kvn], _ident,
                    kv_b_specs, kv_args, kv_tile,
                    N_DEV, (d, kvb), pl.BlockSpec((None, d, kvb), lambda j, i: (j, 0, 0)), _sds((N_DEV, d, kvb), BF16), t,
                    "grad_kv")
        tok = scatter_start([key for key in group(layer) if not key[0].startswith("ffn")], "_mix" + tag)
        if layer > 0:
            prev = saved[layer - 1]
            dx_out, d_ff, dg0, dg3p = _bwd_matmul_norms(
                a_specs, a_args, a_tile, n_steps, w_spec, w_arg, w_mat,
                s["x_in"], gain(layer, 0) + tok, resid, prev["ff"], gain(layer - 1, 3), "mixer_in_bwd" + tag)
            dgain[(layer, 0)], dgain[(layer - 1, 3)] = dg0, dg3p
        else:
            grad_x, dg0 = _bwd_matmul_norms(
                a_specs, a_args, a_tile, n_steps, w_spec, w_arg, w_mat,
                s["x_in"], gain(layer, 0), resid, None, None, "mixer_in_bwd" + tag)
            dgain[(layer, 0)] = dg0

    small_grad = jnp.concatenate(
        [dgain[(layer, k)] for layer in range(depth) for k in range(4)] + [dtaps[layer] for layer in range(n_a)]
        + [dgain["kv"]] + [jnp.zeros((small_rows - n_small - 1, d), F32)], axis=0)
    small_grads_all = _all_gather([small_grad], "gather_small_grads")[0]
    lo = dev * (d // N_DEV)

    def pack(ng, cwp, kvg):
        rows = jnp.concatenate([ng.reshape(4 * depth, -1), cwp.reshape(3 * n_a, -1)], axis=0)
        z = lax.dynamic_update_slice(jnp.zeros((small_rows, d), F32), rows, (0, lo))
        return lax.dynamic_update_slice(z, kvg[None], (n_small, 0))

    w_small = lax.dynamic_update_slice(small_all, g_kv, (n_small, 0))
    m_small, v_small = pack(m_norm_g, m_conv_w, m_kv_norm_g), pack(v_norm_g, v_conv_w, v_kv_norm_g)
    sm = _small_adamw(small_grads_all, w_small, m_small, v_small, "adamw_small")

    def unpack(a):
        mine = lax.dynamic_slice(a, (0, lo), (small_rows, d // N_DEV))
        return (mine[:4 * depth].reshape(norm_g.shape), mine[4 * depth:n_small].reshape(conv_w.shape), a[n_small])

    small_out = [unpack(a) for a in sm]

    moments = {"conv_in_w": (m_conv_in_w, v_conv_in_w), "conv_out_w": (m_conv_out_w, v_conv_out_w),
               "kv_w": (m_kv_w[None], v_kv_w[None]), "q_w": (m_q_w, v_q_w), "o_w": (m_o_w, v_o_w),
               "ffn_in_w": (m_ffn_in_w, v_ffn_in_w), "ffn_out_w": (m_ffn_out_w, v_ffn_out_w)}
    res = {k: [None] * big[k].shape[0] for k in names}
    for keys, tag, send_sems, recv_sems, parts, zones in scattering:
        parts, zones = _send_wait(send_sems, recv_sems, parts, zones, grad_x, "scatter_wait" + tag)
        for (k, i), part, zone in zip(keys, parts, zones):
            res[k][i] = _sum_adamw(slot, part, zone, big[k], moments[k][0], moments[k][1], i, f"adamw_{k}_{i}")

    def big_out(k, which):
        st = jnp.stack([res[k][i][which] for i in range(big[k].shape[0])], axis=0)
        return st[0] if k == "kv_w" else st

    out_names = ["norm_g", "conv_in_w", "conv_w", "conv_out_w", "kv_norm_g", "kv_w", "q_w", "o_w", "ffn_in_w", "ffn_out_w"]
    small_pos = {"norm_g": 0, "conv_w": 1, "kv_norm_g": 2}
    outs = [loss, grad_x[None]]
    for which in range(4):
        for k in out_names:
            outs.append(small_out[which][small_pos[k]] if k in small_pos else big_out(k, which))
    return tuple(outs)
```

```python
import functools
import math

import numpy as np
import jax
import jax.numpy as jnp
from jax import lax
from jax.experimental import pallas as pl
from jax.experimental.pallas import tpu as pltpu

F32 = jnp.float32
BF16 = jnp.bfloat16

N_DEV = 8
RMS_EPS = 1e-6
HEAD_DIM = 64
LANES = 128
ATT_BLOCK = 128
DILATIONS = (1, 4, 16)
SUPER = ATT_BLOCK * DILATIONS[-1]
NEG = -1e30
ATT_UNROLL_FWD = 8
ATT_UNROLL_BWD = 8

ADAM_LR, ADAM_B1, ADAM_B2, ADAM_EPS, ADAM_WD, ADAM_STEP = 0.001, 0.9, 0.999, 1e-08, 0.01, 10

ROW_TILE = 512
BIG_ROW_TILE = 1024
GRAD_ROW_TILE = 2048
BWD_ROW_TILE = 512
MESH = pl.DeviceIdType.MESH


def _call(body, *, name, grid=None, in_specs=None, out_specs=None, out_shape=None, scratch_shapes=(), prefetch=False,
          **params):
    cp = pltpu.CompilerParams(**params) if params else None
    if prefetch:
        spec = pltpu.PrefetchScalarGridSpec(num_scalar_prefetch=1, grid=grid, in_specs=in_specs, out_specs=out_specs,
                                            scratch_shapes=list(scratch_shapes))
        return pl.pallas_call(body, name=name, grid_spec=spec, out_shape=out_shape, compiler_params=cp)
    kwargs = {k: v for k, v in (("grid", grid), ("in_specs", in_specs), ("out_specs", out_specs)) if v is not None}
    return pl.pallas_call(body, name=name, out_shape=out_shape, scratch_shapes=list(scratch_shapes),
                          compiler_params=cp, **kwargs)


def _sds(shape, dtype):
    return jax.ShapeDtypeStruct(tuple(shape), dtype)


def _rms(x, g):
    r = lax.rsqrt(jnp.mean(x * x, axis=-1, keepdims=True) + RMS_EPS)
    return x * r * g


def _rms_bwd(x, g, dy):
    r = lax.rsqrt(jnp.mean(x * x, axis=-1, keepdims=True) + RMS_EPS)
    xh = x * r
    dxh = dy * g
    dx = r * (dxh - xh * jnp.mean(dxh * xh, axis=-1, keepdims=True))
    return dx, jnp.sum(dy * xh, axis=0, keepdims=True)


def _dot(a, b):
    return jnp.dot(a, b, preferred_element_type=F32)


def _dot_nt(a, b):
    return lax.dot_general(a, b, (((1,), (1,)), ((), ())), preferred_element_type=F32)


def _dot_tn(a, b):
    return lax.dot_general(a, b, (((0,), (0,)), ((), ())), preferred_element_type=F32)


def _mesh_pos():
    return lax.axis_index("x"), lax.axis_index("y"), lax.axis_index("c")


def _all_gather(arrs, name):
    n = len(arrs)

    def body(*refs):
        ins, outs = refs[:n], refs[n:2 * n]
        send_sems, recv_sems, local_sems = refs[2 * n:]
        x, y, c = _mesh_pos()
        me, sibling = (x, y, c), (x, y, 1 - c)
        chips = [(1 - x, y), (x, 1 - y), (1 - x, 1 - y)]

        def copy(a, k, block, to, src=None):
            dst = outs[a].at[4 * block[0] + 2 * block[1] + block[2]]
            return pltpu.make_async_remote_copy(
                src_ref=dst if src is None else src, dst_ref=dst, send_sem=send_sems.at[a, k],
                recv_sem=recv_sems.at[a, k], device_id=to, device_id_type=MESH)

        started = []
        for a in range(n):
            mine = pltpu.make_async_copy(ins[a], outs[a].at[4 * x + 2 * y + c], local_sems.at[a])
            mine.start()
            started.append(mine)
        first = []
        for a in range(n):
            first.append(copy(a, 0, me, sibling, src=ins[a]))
            first += [copy(a, 1 + j, me, (*chip, c), src=ins[a]) for j, chip in enumerate(chips)]
        for cp in first:
            cp.start()
        passed = []
        for a in range(n):
            for j, chip in enumerate(chips):
                copy(a, 1 + j, (*chip, c), me).wait_recv()
                fwd = copy(a, 4 + j, (*chip, c), sibling)
                fwd.start()
                passed.append(fwd)
        for a in range(n):
            copy(a, 0, sibling, me).wait_recv()
            for j, chip in enumerate(chips):
                copy(a, 4 + j, (*chip, 1 - c), me).wait_recv()
        for cp in first + passed:
            cp.wait_send()
        for cp in started:
            cp.wait()

    any_spec = pl.BlockSpec(memory_space=pl.ANY)
    outs = _call(
        body, name=name, in_specs=[any_spec] * n, out_specs=[any_spec] * n,
        out_shape=[_sds((N_DEV,) + a.shape, a.dtype) for a in arrs],
        scratch_shapes=[pltpu.SemaphoreType.DMA((n, 7)), pltpu.SemaphoreType.DMA((n, 7)), pltpu.SemaphoreType.DMA((n,))],
        has_side_effects=True,
    )(*arrs)
    return list(outs)


HBM_SPEC = pl.BlockSpec(memory_space=pltpu.HBM)
SEM_SPEC = pl.BlockSpec(memory_space=pltpu.SEMAPHORE)
DATAFLOW = pltpu.SideEffectType.DATAFLOW_SIDE_EFFECTING
PEERS = [(dx, dy, dc) for dx in (0, 1) for dy in (0, 1) for dc in (0, 1)][1:]


def _peer(flip):
    x, y, c = _mesh_pos()
    return tuple(1 - v if f else v for v, f in zip((x, y, c), flip))


def _slot(pos):
    return 4 * pos[0] + 2 * pos[1] + pos[2]


def _in_hbm(a):
    return pltpu.with_memory_space_constraint(a, pltpu.HBM)


def _direct_copies(srcs, lands, send_sems, recv_sems, scatter):
    me = _slot(_mesh_pos())
    copies = []
    for a in range(len(lands)):
        for k, flip in enumerate(PEERS):
            peer = _peer(flip)
            src = srcs[a].at[_slot(peer)] if scatter else lands[a].at[me]
            idx = a * len(PEERS) + k
            copies.append(pltpu.make_async_remote_copy(
                src_ref=src, dst_ref=lands[a].at[me], send_sem=send_sems.at[idx], recv_sem=recv_sems.at[idx],
                device_id=peer, device_id_type=MESH))
    return copies


def _send_start(srcs, lands, after, name):
    ns, nl = len(srcs), len(lands)
    scatter = ns > 0

    def body(*refs):
        src_refs, land_refs = refs[:ns], refs[ns:ns + nl]
        send_sems, recv_sems = refs[ns + nl + 1:ns + nl + 3]
        token = refs[-1]
        for cp in _direct_copies(src_refs, land_refs, send_sems, recv_sems, scatter):
            cp.start()
        token[...] = jnp.zeros_like(token)

    sem = pltpu.SemaphoreType.DMA((nl * len(PEERS),))
    outs = pl.pallas_call(
        body, name=name,
        out_shape=(sem, sem) + tuple(pltpu.HBM(a.shape, a.dtype) for a in list(srcs) + list(lands))
        + (_sds((8, LANES), F32),),
        in_specs=[HBM_SPEC] * (ns + nl) + [pl.BlockSpec(memory_space=pl.ANY)],
        out_specs=(SEM_SPEC, SEM_SPEC) + (HBM_SPEC,) * (ns + nl) + (pl.BlockSpec(memory_space=pltpu.VMEM),),
        input_output_aliases={i: 2 + i for i in range(ns + nl)},
        compiler_params=pltpu.CompilerParams(has_side_effects=DATAFLOW),
    )(*[_in_hbm(a) for a in list(srcs) + list(lands)], after)
    send_sems, recv_sems = outs[0], outs[1]
    return send_sems, recv_sems, list(outs[2:2 + ns]), list(outs[2 + ns:2 + ns + nl]), outs[-1]


def _send_wait(send_sems, recv_sems, srcs, lands, after, name):
    ns, nl = len(srcs), len(lands)
    scatter = ns > 0

    def body(*refs):
        src_refs, land_refs = refs[:ns], refs[ns:ns + nl]
        send_sems, recv_sems = refs[ns + nl:ns + nl + 2]
        copies = _direct_copies(src_refs, land_refs, send_sems, recv_sems, scatter)
        for cp in copies:
            cp.wait_send()
        for cp in copies:
            cp.wait_recv()

    outs = pl.pallas_call(
        body, name=name,
        out_shape=tuple(pltpu.HBM(a.shape, a.dtype) for a in list(srcs) + list(lands)),
        in_specs=[HBM_SPEC] * (ns + nl) + [SEM_SPEC, SEM_SPEC, pl.BlockSpec(memory_space=pl.ANY)],
        out_specs=(HBM_SPEC,) * (ns + nl),
        input_output_aliases={i: i for i in range(ns + nl)},
        compiler_params=pltpu.CompilerParams(has_side_effects=DATAFLOW),
    )(*srcs, *lands, send_sems, recv_sems, after)
    return list(outs[:ns]), list(outs[ns:])


def _row_tile(rows, cap=512):
    t = min(rows, cap)
    while rows % t or (t % 16 and t != rows):
        t -= 1
    return t


def _as2d(a):
    return a.reshape(-1, a.shape[-1])


def _cast_layer(w, layer, slot, name):
    _, rows, cols = w.shape
    tr = _row_tile(rows)

    def body(*refs):
        refs[-1][...] = refs[-2][...].astype(BF16)

    if slot is None:
        return _call(body, name=name, grid=(rows // tr,),
                     in_specs=[pl.BlockSpec((None, tr, cols), lambda i: (layer, i, 0))],
                     out_specs=pl.BlockSpec((tr, cols), lambda i: (i, 0)), out_shape=_sds((rows, cols), BF16))(w)
    return _call(body, name=name, grid=(rows // tr,), prefetch=True,
                 in_specs=[pl.BlockSpec((None, tr, cols), lambda i, s: (layer, i, 0))],
                 out_specs=pl.BlockSpec((None, tr, cols), lambda i, s: (s[0], i, 0)),
                 out_shape=_sds((N_DEV, rows, cols), BF16))(slot, w)


def _sum_adamw(slot, parts, lands, w, m, v, name):
    n_l = len(parts)
    _, rows, cols = lands[0].shape
    tr = _row_tile(rows, 128)

    def body(s_ref, *refs):
        p_refs, l_refs = refs[:n_l], refs[n_l:2 * n_l]
        w_ref, m_ref, v_ref, g_ref, d_ref, nm_ref, nv_ref = refs[2 * n_l:]
        for k in range(n_l):
            @pl.when(pl.program_id(0) == k)
            def _(k=k):
                own = p_refs[k][...]
                g = jnp.zeros((tr, cols), F32)
                for j in range(N_DEV):
                    g = g + jnp.where(s_ref[0] == j, own, l_refs[k][j]).astype(F32)
                delta, nm, nv = _adamw_math(w_ref[...], g, m_ref[...], v_ref[...])
                g_ref[...] = g
                d_ref[...] = delta
                nm_ref[...] = nm
                nv_ref[...] = nv

    def own_block(k):
        return pl.BlockSpec((None, tr, cols), lambda l, i, s: (s[0], jnp.where(l == k, i, 0), 0))

    def zone_block(k):
        return pl.BlockSpec((N_DEV, tr, cols), lambda l, i, s: (0, jnp.where(l == k, i, 0), 0))

    lay = pl.BlockSpec((None, tr, cols), lambda l, i, s: (l, i, 0))
    return _call(body, name=name, grid=(n_l, rows // tr), prefetch=True,
                 in_specs=[own_block(k) for k in range(n_l)] + [zone_block(k) for k in range(n_l)] + [lay, lay, lay],
                 out_specs=[lay] * 4, out_shape=[_sds((n_l, rows, cols), F32)] * 4)(slot, *parts, *lands, w, m, v)


def _adamw_math(w, g, m, v):
    m = ADAM_B1 * m + (1.0 - ADAM_B1) * g
    v = ADAM_B2 * v + (1.0 - ADAM_B2) * (g * g)
    m_hat = m / (1.0 - ADAM_B1 ** ADAM_STEP)
    v_hat = v / (1.0 - ADAM_B2 ** ADAM_STEP)
    delta = -ADAM_LR * (m_hat / (jnp.sqrt(v_hat) + ADAM_EPS) + ADAM_WD * w)
    return delta, m, v


def _small_adamw(gathered, w, m, v, name):
    def body(a_ref, w_ref, m_ref, v_ref, g_ref, d_ref, nm_ref, nv_ref):
        g = a_ref[0]
        for k in range(1, N_DEV):
            g = g + a_ref[k]
        delta, nm, nv = _adamw_math(w_ref[...], g, m_ref[...], v_ref[...])
        g_ref[...] = g
        d_ref[...] = delta
        nm_ref[...] = nm
        nv_ref[...] = nv

    return _call(body, name=name, out_shape=[_sds(w.shape, F32)] * 4)(gathered, w, m, v)


def _norm_matmul_cols(x, g, wg, layer, mode, name):
    t, d = x.shape
    nb = wg.shape[-1]
    tm = BIG_ROW_TILE
    pieces = nb // LANES

    def body(x_ref, g_ref, w_ref, y_ref, xnt_ref, xn_ref):
        @pl.when(pl.program_id(1) == 0)
        def _():
            xn = _rms(x_ref[...], g_ref[...])
            xn_ref[...] = xn.astype(BF16)
            xnt_ref[...] = xn.T.astype(BF16)

        y = _dot(xn_ref[...], w_ref[...])
        if mode == "heads":
            for p in range(pieces):
                y_ref[p] = y[:, p * LANES:(p + 1) * LANES]
        else:
            y_ref[...] = y.astype(BF16)

    if mode == "cols":
        y_shape, y_spec = _sds((t, N_DEV * nb), BF16), pl.BlockSpec((tm, nb), lambda i, j: (i, j))
    else:
        y_shape = _sds((N_DEV * pieces, t, LANES), F32)
        y_spec = pl.BlockSpec((pieces, tm, LANES), lambda i, j: (j, i, 0))
    return _call(
        body, name=name, grid=(t // tm, N_DEV),
        in_specs=[pl.BlockSpec((tm, d), lambda i, j: (i, 0)), pl.BlockSpec((1, d), lambda i, j: (0, 0)),
                  pl.BlockSpec((None, None, d, nb), lambda i, j: (j, layer, 0, 0))],
        out_specs=[y_spec, pl.BlockSpec((d, tm), lambda i, j: (0, i))],
        out_shape=[y_shape, _sds((d, t), BF16)], scratch_shapes=[pltpu.VMEM((tm, d), BF16)])(x, g, wg)


def _ffn_in_swiglu(x, g, wg, layer, name):
    t, d = x.shape
    fc = wg.shape[-1]
    tm = BIG_ROW_TILE

    def body(x_ref, g_ref, wg_ref, wu_ref, gate_ref, up_ref, a_ref, xnt_ref, xn_ref):
        @pl.when(pl.program_id(1) == 0)
        def _():
            xn = _rms(x_ref[...], g_ref[...])
            xn_ref[...] = xn.astype(BF16)
            xnt_ref[...] = xn.T.astype(BF16)

        xn = xn_ref[...]
        gate, up = _dot(xn, wg_ref[...]), _dot(xn, wu_ref[...])
        gate_ref[...] = gate.astype(BF16)
        up_ref[...] = up.astype(BF16)
        a_ref[...] = (gate * jax.nn.sigmoid(gate) * up).astype(BF16)

    chunk = pl.BlockSpec((None, tm, fc), lambda i, c: (c, i, 0))
    return _call(
        body, name=name, grid=(t // tm, 4),
        in_specs=[pl.BlockSpec((tm, d), lambda i, c: (i, 0)), pl.BlockSpec((1, d), lambda i, c: (0, 0)),
                  pl.BlockSpec((None, None, d, fc), lambda i, c: (c, layer, 0, 0)),
                  pl.BlockSpec((None, None, d, fc), lambda i, c: (c + 4, layer, 0, 0))],
        out_specs=[chunk, chunk, chunk, pl.BlockSpec((d, tm), lambda i, c: (0, i))],
        out_shape=[_sds((4, t, fc), BF16)] * 3 + [_sds((d, t), BF16)],
        scratch_shapes=[pltpu.VMEM((tm, d), BF16)])(x, g, wg, wg)


def _norm_matmul_heads(x, g, wg, layer, scale, name):
    t, d = x.shape
    tm = ROW_TILE
    hp = d // LANES

    def body(x_ref, g_ref, w_ref, y_ref, xn_ref):
        xn = _rms(x_ref[...], g_ref[...]).astype(BF16)
        xn_ref[...] = xn
        y = _dot(xn, w_ref[...].reshape(d, d)) * scale
        for p in range(hp):
            y_ref[p] = y[:, p * LANES:(p + 1) * LANES]

    return _call(
        body, name=name, grid=(t // tm,),
        in_specs=[pl.BlockSpec((tm, d), lambda i: (i, 0)), pl.BlockSpec((1, d), lambda i: (0, 0)),
                  pl.BlockSpec((N_DEV, None, d // N_DEV, d), lambda i: (0, layer, 0, 0))],
        out_specs=[pl.BlockSpec((hp, tm, LANES), lambda i: (0, i, 0)), pl.BlockSpec((tm, d), lambda i: (i, 0))],
        out_shape=[_sds((hp, t, LANES), F32), _sds((t, d), BF16)])(x, g, wg)


def _shift_down(u, halo, k, tm):
    row = lax.broadcasted_iota(jnp.int32, u.shape, 0)
    out = pltpu.roll(u, k, 0)
    for j in range(k):
        out = jnp.where(row == j, halo[halo.shape[0] - k + j:halo.shape[0] - k + j + 1, :], out)
    return out


def _shift_up(u, halo, k, tm):
    row = lax.broadcasted_iota(jnp.int32, u.shape, 0)
    out = pltpu.roll(u, tm - k, 0)
    for j in range(k):
        out = jnp.where(row == tm - k + j, halo[j:j + 1, :], out)
    return out


HALO = 16


def _conv_fwd(p, cw, name):
    t, d3 = p.shape
    d = d3 // 3
    tm = ROW_TILE
    hb = tm // HALO

    def body(p_ref, prev_ref, cw_ref, z_ref):
        i = pl.program_id(0)
        b = p_ref[:, 0:d].astype(F32)
        u = p_ref[:, d:2 * d].astype(F32) * p_ref[:, 2 * d:3 * d].astype(F32)
        keep = (i > 0).astype(F32)
        hu = prev_ref[:, d:2 * d].astype(F32) * prev_ref[:, 2 * d:3 * d].astype(F32) * keep
        uc = cw_ref[2:3, :] * u + cw_ref[1:2, :] * _shift_down(u, hu, 1, tm) + cw_ref[0:1, :] * _shift_down(u, hu, 2, tm)
        z_ref[...] = (b * uc).astype(BF16)

    return _call(
        body, name=name, grid=(t // tm,),
        in_specs=[pl.BlockSpec((tm, d3), lambda i: (i, 0)),
                  pl.BlockSpec((HALO, d3), lambda i: (jnp.maximum(i * hb - 1, 0), 0)),
                  pl.BlockSpec((3, d), lambda i: (0, 0))],
        out_specs=pl.BlockSpec((tm, d), lambda i: (i, 0)), out_shape=_sds((t, d), BF16))(p, p, cw)


def _matmul_norm_residual(a3, wg, layer, g, x_res, name):
    kc_n, t, kc = a3.shape
    d = wg.shape[-1]
    per = N_DEV // kc_n
    rows = wg.shape[2]
    tm = ROW_TILE

    def body(a_ref, w_ref, g_ref, x_ref, raw_ref, xo_ref):
        raw = None
        for c in range(kc_n):
            term = _dot(a_ref[c], w_ref[c * per:(c + 1) * per].reshape(per * rows, d))
            raw = term if raw is None else raw + term
        raw_ref[...] = raw
        xo_ref[...] = x_ref[...] + _rms(raw, g_ref[...])

    row_spec = pl.BlockSpec((tm, d), lambda i: (i, 0))
    return _call(
        body, name=name, grid=(t // tm,),
        in_specs=[pl.BlockSpec((kc_n, tm, kc), lambda i: (0, i, 0)),
                  pl.BlockSpec((N_DEV, None, rows, d), lambda i: (0, layer, 0, 0)),
                  pl.BlockSpec((1, d), lambda i: (0, 0)), row_spec],
        out_specs=[row_spec, row_spec], out_shape=[_sds((t, d), F32)] * 2)(a3, wg, g, x_res)


def _alibi_slopes(n_heads):
    hh = np.arange(n_heads, dtype=np.float32) + 1.0
    s = np.power(2.0, -8.0 * hh / n_heads).astype(np.float32)
    return jnp.asarray(np.repeat(s.reshape(n_heads // 2, 2, 1), 2 * ATT_BLOCK, axis=2))


def _band_bias(sl_ref, dil):
    u = lax.broadcasted_iota(jnp.int32, (ATT_BLOCK, 2 * ATT_BLOCK), 0)
    kk = lax.broadcasted_iota(jnp.int32, (ATT_BLOCK, 2 * ATT_BLOCK), 1)
    delta = u + ATT_BLOCK - kk
    valid = (delta >= 0) & (delta <= ATT_BLOCK)
    dist = (delta * dil).astype(F32)
    rows = [jnp.where(valid, -sl_ref[hd:hd + 1, :] * dist, NEG) for hd in range(2)]
    return jnp.concatenate(rows, axis=0)


def _stack_heads(a):
    lane = lax.broadcasted_iota(jnp.int32, a.shape, 1)
    return jnp.concatenate([jnp.where(lane < HEAD_DIM, a, 0.0), jnp.where(lane >= HEAD_DIM, a, 0.0)], axis=0).astype(BF16)


def _unstack_heads(a2):
    top, bot = a2[:ATT_BLOCK], a2[ATT_BLOCK:]
    lane = lax.broadcasted_iota(jnp.int32, top.shape, 1)
    return jnp.where(lane < HEAD_DIM, top, bot)


def _rows_to_lanes(a0, a1):
    eye = lax.broadcasted_iota(jnp.int32, a0.shape, 0) == lax.broadcasted_iota(jnp.int32, a0.shape, 1)
    return jnp.concatenate([jnp.sum(jnp.where(eye, a, 0.0), axis=0, keepdims=True) for a in (a0, a1)], axis=1)


def _fill_bias_t(sl_ref, bias_ref):
    kk = lax.broadcasted_iota(jnp.int32, (2 * ATT_BLOCK, 2 * ATT_BLOCK), 0)
    lane = lax.broadcasted_iota(jnp.int32, (2 * ATT_BLOCK, 2 * ATT_BLOCK), 1)
    delta = lane % ATT_BLOCK + ATT_BLOCK - kk
    valid = (delta >= 0) & (delta <= ATT_BLOCK)
    slope = jnp.concatenate([sl_ref[0:1, :ATT_BLOCK], sl_ref[1:2, :ATT_BLOCK]], axis=1)
    for gi, dil in enumerate(DILATIONS):
        bias = jnp.where(valid, -slope * (delta * dil).astype(F32), NEG)
        bias_ref[2 * gi] = bias
        bias_ref[2 * gi + 1] = jnp.where(kk < ATT_BLOCK, NEG, bias)


def _fill_bias(sl_ref, bias_ref):
    kk = lax.broadcasted_iota(jnp.int32, (2 * ATT_BLOCK, 2 * ATT_BLOCK), 1)
    for gi, dil in enumerate(DILATIONS):
        bias = _band_bias(sl_ref, dil)
        bias_ref[2 * gi] = bias
        bias_ref[2 * gi + 1] = jnp.where(kk < ATT_BLOCK, NEG, bias)


def _attention_fwd(q, kv, slopes, name):
    hp, t, _ = q.shape
    ns = t // SUPER
    nd = len(DILATIONS)

    def body(sl_ref, q_ref, kc_ref, kp_ref, vc_ref, vp_ref, o_ref, lse_ref, kw_ref, vw_ref, og_ref, lg_ref, bias_ref):
        n = pl.program_id(1)
        kw_ref[0:SUPER, :] = kp_ref[...]
        kw_ref[SUPER:, :] = kc_ref[...]
        vw_ref[0:SUPER, :] = vp_ref[...]
        vw_ref[SUPER:, :] = vc_ref[...]

        @pl.when(n == 0)
        def _():
            _fill_bias(sl_ref, bias_ref)

        for gi, dil in enumerate(DILATIONS):

            def block(idx, carry, gi=gi, dil=dil):
                r, b = idx % dil, idx // dil
                qs = b * (ATT_BLOCK * dil) + r
                ks = SUPER + (b - 1) * (ATT_BLOCK * dil) + r
                first = jnp.logical_and(n == 0, b == 0).astype(jnp.int32)
                q2 = _stack_heads(q_ref[pl.ds(qs, ATT_BLOCK, stride=dil), :])
                kb = kw_ref[pl.ds(ks, 2 * ATT_BLOCK, stride=dil), :].astype(BF16)
                vb = vw_ref[pl.ds(ks, 2 * ATT_BLOCK, stride=dil), :].astype(BF16)
                s = _dot_nt(q2, kb) + bias_ref[2 * gi + first]
                m = jnp.max(s, axis=-1, keepdims=True)
                p = jnp.exp(s - m).astype(BF16)
                ol = _dot(p, jnp.concatenate([vb, jnp.ones_like(vb)], axis=1))
                l = ol[:, LANES:]
                o2 = ol[:, :LANES] / l
                lse2 = m + jnp.log(l)
                og_ref[gi, pl.ds(qs, ATT_BLOCK, stride=dil), :] = _unstack_heads(o2)
                lg_ref[gi, pl.ds(qs, ATT_BLOCK, stride=dil), :] = _unstack_heads(lse2)
                return carry

            lax.fori_loop(0, SUPER // ATT_BLOCK, block, 0, unroll=ATT_UNROLL_FWD)
        lg =[lg_ref[gi] for gi in range(nd)]
        top = functools.reduce(jnp.maximum, lg)
        ws = [jnp.exp(x - top) for x in lg]
        tot = functools.reduce(jnp.add, ws)
        lse_ref[...] = top + jnp.log(tot)
        acc = ws[0] * og_ref[0]
        for gi in range(1, nd):
            acc = acc + ws[gi] * og_ref[gi]
        o_ref[...] = (acc / tot).astype(BF16)

    cur = lambda off: pl.BlockSpec((None, SUPER, LANES), lambda h, n: (h + off, n, 0))
    prev = lambda off: pl.BlockSpec((None, SUPER, LANES), lambda h, n: (h + off, jnp.maximum(n - 1, 0), 0))
    return _call(
        body, name=name, grid=(hp, ns),
        in_specs=[pl.BlockSpec((None, 2, 2 * ATT_BLOCK), lambda h, n: (h, 0, 0)), cur(0), cur(0), prev(0), cur(hp), prev(hp)],
        out_specs=[pl.BlockSpec((SUPER, LANES), lambda h, n: (n, h)), cur(0)],
        out_shape=[_sds((t, hp * LANES), BF16), _sds((hp, t, LANES), F32)],
        scratch_shapes=[pltpu.VMEM((2 * SUPER, LANES), F32), pltpu.VMEM((2 * SUPER, LANES), F32),
                        pltpu.VMEM((nd, SUPER, LANES), F32), pltpu.VMEM((nd, SUPER, LANES), F32),
                        pltpu.VMEM((2 * nd, 2 * ATT_BLOCK, 2 * ATT_BLOCK), F32)],
    )(slopes, q, kv, kv, kv, kv)


def _attention_bwd(q, kv, o, lse, d_o, slopes, q_scale, name):
    hp, t, _ = q.shape
    ns = t // SUPER

    def body(sl_ref, q_ref, kc_ref, kp_ref, vc_ref, vp_ref, o_ref, lse_ref, do_ref,
             dq_ref, dk_ref, dv_ref, kw_ref, vw_ref, dkw_ref, dvw_ref, st_ref, bias_ref):
        n = pl.program_id(1)

        @pl.when(n == 0)
        def _():
            dkw_ref[...] = jnp.zeros_like(dkw_ref)
            dvw_ref[...] = jnp.zeros_like(dvw_ref)

        @pl.when(n > 0)
        def _():
            dkw_ref[0:SUPER, :] = dkw_ref[SUPER:, :]
            dvw_ref[0:SUPER, :] = dvw_ref[SUPER:, :]
            dkw_ref[SUPER:, :] = jnp.zeros((SUPER, LANES), F32)
            dvw_ref[SUPER:, :] = jnp.zeros((SUPER, LANES), F32)

        @pl.when(n < ns)
        def _():
            kw_ref[0:SUPER, :] = kp_ref[...]
            kw_ref[SUPER:, :] = kc_ref[...]
            vw_ref[0:SUPER, :] = vp_ref[...]
            vw_ref[SUPER:, :] = vc_ref[...]
            prod = do_ref[...] * o_ref[...].astype(F32)
            lane = lax.broadcasted_iota(jnp.int32, prod.shape, 1)
            zero = jnp.zeros((SUPER, LANES), F32)
            st_ref[0] = zero + jnp.sum(jnp.where(lane < HEAD_DIM, prod, 0.0), axis=-1, keepdims=True)
            st_ref[1] = zero + jnp.sum(jnp.where(lane >= HEAD_DIM, prod, 0.0), axis=-1, keepdims=True)
            lse = lse_ref[...]
            swapped = pltpu.roll(lse, HEAD_DIM, 1)
            st_ref[2] = jnp.where(lane < HEAD_DIM, lse, swapped)
            st_ref[3] = jnp.where(lane >= HEAD_DIM, lse, swapped)
            dq_ref[...] = jnp.zeros_like(dq_ref)

            @pl.when(n == 0)
            def _():
                _fill_bias_t(sl_ref, bias_ref)

            for gi, dil in enumerate(DILATIONS):

                def block(idx, carry, gi=gi, dil=dil):
                    r, b = idx % dil, idx // dil
                    qs = b * (ATT_BLOCK * dil) + r
                    ks = SUPER + (b - 1) * (ATT_BLOCK * dil) + r
                    first = jnp.logical_and(n == 0, b == 0).astype(jnp.int32)
                    rows = pl.ds(qs, ATT_BLOCK, stride=dil)
                    keys = pl.ds(ks, 2 * ATT_BLOCK, stride=dil)
                    q2 = _stack_heads(q_ref[rows, :])
                    do2 = _stack_heads(do_ref[rows, :])
                    kb = kw_ref[keys, :].astype(BF16)
                    vb = vw_ref[keys, :].astype(BF16)
                    dd = _rows_to_lanes(st_ref[0, rows, :], st_ref[1, rows, :])
                    lse_b = _rows_to_lanes(st_ref[2, rows, :], st_ref[3, rows, :])
                    p = jnp.exp(_dot_nt(kb, q2) + bias_ref[2 * gi + first] - lse_b)
                    ds = (p * (_dot_nt(vb, do2) - dd)).astype(BF16)
                    dvw_ref[keys, :] += _dot(p.astype(BF16), do2)
                    dkw_ref[keys, :] += _dot(ds, q2)
                    dq_ref[rows, :] += _unstack_heads(_dot_tn(ds, kb)) * q_scale
                    return carry

                lax.fori_loop(0, SUPER // ATT_BLOCK, block, 0, unroll=ATT_UNROLL_BWD)

        dk_ref[...] = dkw_ref[0:SUPER, :]
        dv_ref[...] = dvw_ref[0:SUPER, :]

    last = ns - 1
    cur = lambda off: pl.BlockSpec((None, SUPER, LANES), lambda h, n: (h + off, jnp.minimum(n, last), 0))
    prev = lambda off: pl.BlockSpec((None, SUPER, LANES), lambda h, n: (h + off, jnp.clip(n - 1, 0, last), 0))
    nat = pl.BlockSpec((SUPER, LANES), lambda h, n: (jnp.minimum(n, last), h))
    late = pl.BlockSpec((None, SUPER, LANES), lambda h, n: (h, jnp.maximum(n - 1, 0), 0))
    dq, dk, dv = _call(
        body, name=name, grid=(hp, ns + 1),
        in_specs=[pl.BlockSpec((None, 2, 2 * ATT_BLOCK), lambda h, n: (h, 0, 0)), cur(0), cur(0), prev(0), cur(hp), prev(hp),
                  nat, cur(0), nat],
        out_specs=[cur(0), late, late],
        out_shape=[_sds((hp, t, LANES), F32)] * 3,
        scratch_shapes=[pltpu.VMEM((2 * SUPER, LANES), F32)] * 4 + [
            pltpu.VMEM((4, SUPER, LANES), F32), pltpu.VMEM((2 * len(DILATIONS), 2 * ATT_BLOCK, 2 * ATT_BLOCK), F32)],
    )(slopes, q, kv, kv, kv, kv, o, lse, d_o)
    return dq, dk, dv


def _loss_head(y, target, raw, g, name):
    t, d = y.shape
    tm = ROW_TILE

    def body(y_ref, t_ref, raw_ref, g_ref, sq_ref, dy_ref, draw_ref, dg_ref):
        i = pl.program_id(0)
        err = y_ref[...] - t_ref[...]
        dy = err * (1.0 / d)
        dy_ref[...] = dy
        draw, dg = _rms_bwd(raw_ref[...], g_ref[...], dy)
        draw_ref[...] = draw.astype(BF16)
        sq = jnp.zeros((8, LANES), F32) + jnp.sum(err * err)

        @pl.when(i == 0)
        def _():
            sq_ref[...] = sq
            dg_ref[...] = dg

        @pl.when(i > 0)
        def _():
            sq_ref[...] += sq
            dg_ref[...] += dg

    row = pl.BlockSpec((tm, d), lambda i: (i, 0))
    vec = pl.BlockSpec((1, d), lambda i: (0, 0))
    return _call(
        body, name=name, grid=(t // tm,), in_specs=[row, row, row, vec],
        out_specs=[pl.BlockSpec((8, LANES), lambda i: (0, 0)), row, row, vec],
        out_shape=[_sds((8, LANES), F32), _sds((t, d), F32), _sds((t, d), BF16), _sds((1, d), F32)])(y, target, raw, g)


def _bwd_matmul_norms(a_specs, a_args, a_tile, n_steps, w_spec, w_arg, w_mat, xa, ga, resid, xb, gb, name):
    t, d = xa.shape
    tm = BWD_ROW_TILE
    na = len(a_specs)
    second = xb is not None
    per = 4 if n_steps % 4 == 0 else 1
    n_steps //= per

    def blocks_of(spec, k):
        return pl.BlockSpec(spec.block_shape, lambda i, j: spec.index_map(i, per * j + k))

    def body(*refs):
        a_refs, w_refs = refs[:per * na], refs[per * na:per * na + per]
        xa_ref, ga_ref, res_ref = refs[per * na + per:per * na + per + 3]
        rest = refs[per * na + per + 3:]
        if second:
            xb_ref, gb_ref, dx_ref, d2_ref, dga_ref, dgb_ref, acc_ref = rest
        else:
            dx_ref, dga_ref, acc_ref = rest
        i, j = pl.program_id(0), pl.program_id(1)
        part = None
        for k in range(per):
            term = _dot_nt(a_tile(per * j + k, *a_refs[k * na:(k + 1) * na]), w_mat(w_refs[k]))
            part = term if part is None else part + term

        @pl.when(j == 0)
        def _():
            acc_ref[...] = part

        @pl.when(j > 0)
        def _():
            acc_ref[...] += part

        @pl.when(j == n_steps - 1)
        def _():
            da, dga = _rms_bwd(xa_ref[...], ga_ref[...], acc_ref[...])
            dx = res_ref[...] + da
            dx_ref[...] = dx
            if second:
                d2, dgb = _rms_bwd(xb_ref[...], gb_ref[...], dx)
                d2_ref[...] = d2.astype(BF16)

            @pl.when(i == 0)
            def _():
                dga_ref[...] = dga
                if second:
                    dgb_ref[...] = dgb

            @pl.when(i > 0)
            def _():
                dga_ref[...] += dga
                if second:
                    dgb_ref[...] += dgb

    row = pl.BlockSpec((tm, d), lambda i, j: (i, 0))
    vec = pl.BlockSpec((1, d), lambda i, j: (0, 0))
    in_specs = [blocks_of(sp, k) for k in range(per) for sp in a_specs] + [blocks_of(w_spec, k) for k in range(per)]
    in_specs += [row, vec, row]
    args = list(a_args) * per + [w_arg] * per + [xa, ga, resid]
    if second:
        in_specs += [row, vec]
        args += [xb, gb]
        out_specs = [row, row, vec, vec]
        out_shape = [_sds((t, d), F32), _sds((t, d), BF16), _sds((1, d), F32), _sds((1, d), F32)]
    else:
        out_specs = [row, vec]
        out_shape = [_sds((t, d), F32), _sds((1, d), F32)]
    return _call(body, name=name, grid=(t // tm, n_steps), in_specs=in_specs, out_specs=out_specs,
                 out_shape=out_shape, scratch_shapes=[pltpu.VMEM((tm, d), F32)])(*args)


def _heads_to_rows(*refs):
    hp = refs[0].shape[0]
    cols = []
    for p in range(hp):
        v = refs[0][p]
        for r in refs[1:]:
            v = v + r[p]
        cols.append(v)
    return jnp.concatenate(cols, axis=-1).astype(BF16)


def _matmul_nt_rows(a, wg, layer, out_dtype, name):
    t, d = a.shape
    tm = ROW_TILE

    def body(a_ref, w_ref, o_ref):
        o_ref[...] = _dot_nt(a_ref[...], w_ref[...].reshape(d, d)).astype(out_dtype)

    row = pl.BlockSpec((tm, d), lambda i: (i, 0))
    return _call(body, name=name, grid=(t // tm,),
                 in_specs=[row, pl.BlockSpec((N_DEV, None, d // N_DEV, d), lambda i: (0, layer, 0, 0))],
                 out_specs=row, out_shape=_sds((t, d), out_dtype))(a, wg)


def _swiglu_bwd(d_ff, wg, layer, gate, up, name):
    t, d = d_ff.shape
    fc = gate.shape[-1]
    rows = wg.shape[2]
    tm = BIG_ROW_TILE

    def body(df_ref, w_ref, g_ref, u_ref, dh_ref):
        da = _dot_nt(df_ref[...], w_ref[...].reshape(2 * rows, d))
        gate, up = g_ref[...].astype(F32), u_ref[...].astype(F32)
        sig = jax.nn.sigmoid(gate)
        dh_ref[0] = (da * up * (sig * (1.0 + gate * (1.0 - sig)))).astype(BF16)
        dh_ref[1] = (da * (gate * sig)).astype(BF16)

    return _call(
        body, name=name, grid=(t // tm, 4),
        in_specs=[pl.BlockSpec((tm, d), lambda i, c: (i, 0)),
                  pl.BlockSpec((2, None, rows, d), lambda i, c: (c, layer, 0, 0)),
                  pl.BlockSpec((None, tm, fc), lambda i, c: (c, i, 0)),
                  pl.BlockSpec((None, tm, fc), lambda i, c: (c, i, 0))],
        out_specs=pl.BlockSpec((None, 2, tm, fc), lambda i, c: (c, 0, i, 0)),
        out_shape=_sds((4, 2, t, fc), BF16))(d_ff, wg, gate, up)


def _conv_bwd(p, d_z, cw, name):
    t, d3 = p.shape
    d = d3 // 3
    tm = ROW_TILE
    hb = tm // HALO
    nt = t // tm

    def body(p_ref, prev_ref, next_ref, dz_ref, dzn_ref, cw_ref, dp_ref, dcw_ref):
        i = pl.program_id(0)
        b = p_ref[:, 0:d].astype(F32)
        c = p_ref[:, d:2 * d].astype(F32)
        h = p_ref[:, 2 * d:3 * d].astype(F32)
        u = c * h
        hu = prev_ref[:, d:2 * d].astype(F32) * prev_ref[:, 2 * d:3 * d].astype(F32) * (i > 0).astype(F32)
        u1, u2 = _shift_down(u, hu, 1, tm), _shift_down(u, hu, 2, tm)
        uc = cw_ref[2:3, :] * u + cw_ref[1:2, :] * u1 + cw_ref[0:1, :] * u2
        dz = dz_ref[...].astype(F32)
        duc = dz * b
        dn = dzn_ref[...].astype(F32) * next_ref[:, 0:d].astype(F32) * (i < nt - 1).astype(F32)
        du = cw_ref[2:3, :] * duc + cw_ref[1:2, :] * _shift_up(duc, dn, 1, tm) + cw_ref[0:1, :] * _shift_up(duc, dn, 2, tm)
        dp_ref[:, 0:d] = (dz * uc).astype(BF16)
        dp_ref[:, d:2 * d] = (du * h).astype(BF16)
        dp_ref[:, 2 * d:3 * d] = (du * c).astype(BF16)
        dcw = jnp.concatenate([jnp.sum(duc * u2, axis=0, keepdims=True), jnp.sum(duc * u1, axis=0, keepdims=True),
                               jnp.sum(duc * u, axis=0, keepdims=True)], axis=0)

        @pl.when(i == 0)
        def _():
            dcw_ref[...] = dcw

        @pl.when(i > 0)
        def _():
            dcw_ref[...] += dcw

    last_halo = t // HALO - 1
    return _call(
        body, name=name, grid=(nt,),
        in_specs=[pl.BlockSpec((tm, d3), lambda i: (i, 0)),
                  pl.BlockSpec((HALO, d3), lambda i: (jnp.maximum(i * hb - 1, 0), 0)),
                  pl.BlockSpec((HALO, d3), lambda i: (jnp.minimum((i + 1) * hb, last_halo), 0)),
                  pl.BlockSpec((tm, d), lambda i: (i, 0)),
                  pl.BlockSpec((HALO, d), lambda i: (jnp.minimum((i + 1) * hb, last_halo), 0)),
                  pl.BlockSpec((3, d), lambda i: (0, 0))],
        out_specs=[pl.BlockSpec((tm, d3), lambda i: (i, 0)), pl.BlockSpec((3, d), lambda i: (0, 0))],
        out_shape=[_sds((t, d3), BF16), _sds((3, d), F32)])(p, p, p, d_z, d_z, cw)


def _grad_weight(a_specs, a_args, a_tile, b_specs, b_args, b_tile, n_out, acc_shape, out_spec, out_shape, t, name,
                 a_transposed=False):
    tt = GRAD_ROW_TILE
    na, nb = len(a_specs), len(b_specs)

    def body(*refs):
        a_refs, b_refs = refs[:na], refs[na:na + nb]
        o_ref, acc_ref = refs[na + nb:]
        s = pl.program_id(1)
        a, b = a_tile(pl.program_id(0), *a_refs), b_tile(pl.program_id(0), *b_refs)
        part = _dot(a, b) if a_transposed else _dot_tn(a, b)

        @pl.when(s == 0)
        def _():
            acc_ref[...] = part

        @pl.when(s > 0)
        def _():
            acc_ref[...] += part

        @pl.when(s == t // tt - 1)
        def _():
            o_ref[...] = acc_ref[...].astype(BF16).reshape(o_ref.shape)

    return _call(body, name=name, grid=(n_out, t // tt), in_specs=list(a_specs) + list(b_specs), out_specs=out_spec,
                 out_shape=out_shape, scratch_shapes=[pltpu.VMEM(acc_shape, F32)])(*a_args, *b_args)


def _ident(*args):
    return args[-1][...]


def _heads_tile(j, *refs):
    return _heads_to_rows(*refs)


def kernel(x, norm_g, conv_in_w, conv_w, conv_out_w, kv_norm_g, kv_w, q_w, o_w, ffn_in_w, ffn_out_w, loss_target, m_norm_g, m_conv_in_w, m_conv_w, m_conv_out_w, m_kv_norm_g, m_kv_w, m_q_w, m_o_w, m_ffn_in_w, m_ffn_out_w, v_norm_g, v_conv_in_w, v_conv_w, v_conv_out_w, v_kv_norm_g, v_kv_w, v_q_w, v_o_w, v_ffn_in_w, v_ffn_out_w):
    x0 = x[0]
    target = loss_target[0]
    t, d = x0.shape
    depth = norm_g.shape[0]
    n_a = conv_in_w.shape[0]
    n_b = q_w.shape[0]
    hp = d // LANES
    tm, tg = BWD_ROW_TILE, GRAD_ROW_TILE
    assert t % SUPER == 0 and d % LANES == 0 and depth == n_a + n_b
    dev = 4 * lax.axis_index("x") + 2 * lax.axis_index("y") + lax.axis_index("c")

    n_small = 4 * depth + 3 * n_a
    small_rows = -(-(n_small + 1) // 8) * 8
    small_local = jnp.concatenate([norm_g.reshape(4 * depth, -1), conv_w.reshape(3 * n_a, -1),
                                   jnp.zeros((small_rows - n_small, norm_g.shape[-1]), F32)], axis=0)
    big = {"conv_in_w": conv_in_w, "conv_out_w": conv_out_w, "kv_w": kv_w[None], "q_w": q_w, "o_w": o_w,
           "ffn_in_w": ffn_in_w, "ffn_out_w": ffn_out_w}
    names = list(big)

    def group(layer):
        if layer < n_a:
            return [("conv_in_w", layer), ("conv_out_w", layer), ("ffn_in_w", layer), ("ffn_out_w", layer)]
        j = layer - n_a
        return ([("kv_w", 0)] if j == 0 else []) + [("q_w", j), ("o_w", j), ("ffn_in_w", layer), ("ffn_out_w", layer)]

    slot = dev.astype(jnp.int32).reshape(1)
    first = _all_gather([small_local] + [_cast_layer(big[k], i, None, f"cast_{k}_{i}") for k, i in group(0)], "gather_weights")
    small_all = first[0].transpose(1, 0, 2).reshape(small_rows, d)
    wl = {key: a[:, None] for key, a in zip(group(0), first[1:])}
    def gather_start(layer, after):
        lands = [_cast_layer(big[k], i, slot, f"cast_{k}_{i}") for k, i in group(layer)]
        send_sems, recv_sems, _, lands, tok = _send_start([], lands, after, f"gather_start_l{layer}")
        return (send_sems, recv_sems, lands), tok[0, 0]

    in_flight, token = gather_start(1, small_all)
    W = lambda k, i: (wl[(k, i)], 0)
    gain = lambda layer, k: small_all[4 * layer + k][None]
    taps = lambda layer: small_all[4 * depth + 3 * layer: 4 * depth + 3 * layer + 3]
    g_kv = kv_norm_g[None]
    slopes = _alibi_slopes(d // HEAD_DIM)
    fc = big["ffn_in_w"].shape[-1]
    cb = big["conv_in_w"].shape[-1]
    kvb = big["kv_w"].shape[-1]
    q_scale = HEAD_DIM ** -0.5

    saved = []
    kv = kvn_t = None
    xs = x0
    for layer in range(depth):
        tag = f"_l{layer}"
        if layer > 0:
            send_sems, recv_sems, lands = in_flight
            _, lands = _send_wait(send_sems, recv_sems, [], lands, xs, f"gather_wait_l{layer}")
            wl.update({key: a[:, None] for key, a in zip(group(layer), lands)})
            if layer + 1 < depth:
                in_flight, token = gather_start(layer + 1, xs)
        s = {"x_in": xs}
        g0 = gain(layer, 0) + token if layer + 1 < depth else gain(layer, 0)
        if layer < n_a:
            s["p"], s["xn_t"] = _norm_matmul_cols(xs, g0, *W("conv_in_w", layer), "cols", "conv_in" + tag)
            s["z"] = _conv_fwd(s["p"], taps(layer), "conv" + tag)
            s["mix"], x_mid = _matmul_norm_residual(s["z"][None], *W("conv_out_w", layer), gain(layer, 1), xs, "conv_out" + tag)
        else:
            j = layer - n_a
            if kv is None:
                kv, kvn_t = _norm_matmul_cols(xs, g_kv, *W("kv_w", 0), "heads", "kv_proj")
            s["q"], s["xn"] = _norm_matmul_heads(xs, g0, *W("q_w", j), q_scale, "q_proj" + tag)
            s["o"], s["lse"] = _attention_fwd(s["q"], kv, slopes, "attention" + tag)
            s["mix"], x_mid = _matmul_norm_residual(s["o"][None], *W("o_w", j), gain(layer, 1), xs, "o_proj" + tag)
        s["x_mid"] = x_mid
        s["gate"], s["up"], s["a"], s["fn_t"] = _ffn_in_swiglu(x_mid, gain(layer, 2), *W("ffn_in_w", layer), "ffn_in" + tag)
        s["ff"], xs = _matmul_norm_residual(s["a"], *W("ffn_out_w", layer), gain(layer, 3), x_mid, "ffn_out" + tag)
        saved.append(s)

    last = saved[-1]
    sq, dx_out, d_ff, dg3 = _loss_head(xs, target, last["ff"], gain(depth - 1, 3), "loss_head")
    loss = lax.psum(sq[0, 0] * (0.5 / d), ("x", "y", "c"))

    dgain = {(depth - 1, 3): dg3}
    dtaps = {}
    grads = {k: [None] * big[k].shape[0] for k in names}
    dkv_parts = []
    scattering = []

    def scatter_start(keys, tag):
        parts = [grads[k][i] for k, i in keys]
        zones = [lax.empty(p.shape, p.dtype) for p in parts]
        send_sems, recv_sems, parts, zones, tok = _send_start(parts, zones, small_all, "scatter_start" + tag)
        scattering.append((keys, tag, send_sems, recv_sems, parts, zones))
        return tok[0, 0]

    for layer in reversed(range(depth)):
        tag = f"_l{layer}"
        s = saved[layer]
        dh = _swiglu_bwd(d_ff, *W("ffn_out_w", layer), s["gate"], s["up"], "swiglu_bwd" + tag)
        rows_out = big["ffn_out_w"].shape[1]
        grads["ffn_out_w"][layer] = _grad_weight(
            [pl.BlockSpec((None, tg, fc), lambda c, i: (c, i, 0))], [s["a"]], _ident,
            [pl.BlockSpec((tg, d), lambda c, i: (i, 0))], [d_ff], _ident,
            4, (fc, d), pl.BlockSpec((2, rows_out, d), lambda c, i: (c, 0, 0)), _sds((N_DEV, rows_out, d), BF16), t,
            "grad_ffn_out" + tag)
        grads["ffn_in_w"][layer] = _grad_weight(
            [pl.BlockSpec((d, tg), lambda j, i: (0, i))], [s["fn_t"]], _ident,
            [pl.BlockSpec((None, None, tg, fc), lambda j, i: (j % 4, j // 4, i, 0))], [dh], _ident,
            N_DEV, (d, fc), pl.BlockSpec((None, d, fc), lambda j, i: (j, 0, 0)), _sds((N_DEV, d, fc), BF16), t,
            "grad_ffn_in" + tag, a_transposed=True)
        tok = scatter_start([("ffn_in_w", layer), ("ffn_out_w", layer)], "_ffn" + tag)
        dx_mid, d_mix, dg2, dg1 = _bwd_matmul_norms(
            [pl.BlockSpec((None, None, tm, fc), lambda i, j: (j % 4, j // 4, i, 0))], [dh], _ident, N_DEV,
            pl.BlockSpec((None, None, d, fc), lambda i, j: (j, 0, 0, 0)), W("ffn_in_w", layer)[0], _ident,
            s["x_mid"], gain(layer, 2) + tok, dx_out, s["mix"], gain(layer, 1), "ffn_in_bwd" + tag)
        dgain[(layer, 2)], dgain[(layer, 1)] = dg2, dg1
        full_rows = pl.BlockSpec((N_DEV, d // N_DEV, d), lambda j, i: (0, 0, 0))
        rows_w = lambda wname, idx: (pl.BlockSpec((N_DEV, None, d // N_DEV, d), lambda i, j: (0, 0, 0, 0)), W(wname, idx)[0],
                                     lambda w_ref: w_ref[...].reshape(d, d))
        if layer < n_a:
            d_z = _matmul_nt_rows(d_mix, *W("conv_out_w", layer), BF16, "conv_out_bwd" + tag)
            grads["conv_out_w"][layer] = _grad_weight(
                [pl.BlockSpec((tg, d), lambda j, i: (i, 0))], [s["z"]], _ident,
                [pl.BlockSpec((tg, d), lambda j, i: (i, 0))], [d_mix], _ident,
                1, (d, d), full_rows, _sds((N_DEV, d // N_DEV, d), BF16), t, "grad_conv_out" + tag)
            d_p, dtaps[layer] = _conv_bwd(s["p"], d_z, taps(layer), "conv_bwd" + tag)
            grads["conv_in_w"][layer] = _grad_weight(
                [pl.BlockSpec((d, tg), lambda j, i: (0, i))], [s["xn_t"]], _ident,
                [pl.BlockSpec((tg, cb), lambda j, i: (i, j))], [d_p], _ident,
                N_DEV, (d, cb), pl.BlockSpec((None, d, cb), lambda j, i: (j, 0, 0)), _sds((N_DEV, d, cb), BF16), t,
                "grad_conv_in" + tag, a_transposed=True)
            a_specs, a_args, a_tile, n_steps = [pl.BlockSpec((tm, cb), lambda i, j: (i, j))], [d_p], _ident, N_DEV
            w_spec = pl.BlockSpec((None, None, d, cb), lambda i, j: (j, 0, 0, 0))
            w_arg, w_mat = W("conv_in_w", layer)[0], _ident
            resid = dx_mid
        else:
            j_b = layer - n_a
            d_o = _matmul_nt_rows(d_mix, *W("o_w", j_b), F32, "o_proj_bwd" + tag)
            grads["o_w"][j_b] = _grad_weight(
                [pl.BlockSpec((tg, d), lambda j, i: (i, 0))], [s["o"]], _ident,
                [pl.BlockSpec((tg, d), lambda j, i: (i, 0))], [d_mix], _ident,
                1, (d, d), full_rows, _sds((N_DEV, d // N_DEV, d), BF16), t, "grad_o" + tag)
            dq, dk, dv = _attention_bwd(s["q"], kv, s["o"], s["lse"], d_o, slopes, q_scale, "attention_bwd" + tag)
            dkv_parts.append((dk, dv))
            heads_spec = pl.BlockSpec((hp, tg, LANES), lambda j, i: (0, i, 0))
            grads["q_w"][j_b] = _grad_weight(
                [pl.BlockSpec((tg, d), lambda j, i: (i, 0))], [s["xn"]], _ident,
                [heads_spec], [dq], _heads_tile,
                1, (d, d), full_rows, _sds((N_DEV, d // N_DEV, d), BF16), t, "grad_q" + tag)
            a_specs, a_args, a_tile, n_steps = [pl.BlockSpec((hp, tm, LANES), lambda i, j: (0, i, 0))], [dq], _heads_tile, 1
            w_spec, w_arg, w_mat = rows_w("q_w", j_b)
            resid = dx_mid
            if layer == n_a:
                pieces = kvb // LANES
                halves = []
                for src in (0, 1):
                    halves.append([part[src] for part in dkv_parts])
                n_half = len(dkv_parts)
                kv_args = [arr for src in (0, 1) for arr in halves[src]]

                def kv_block(src, j):
                    return jnp.where((j // 4) == src, j % 4, 0)

                def kv_tile(j, *refs):
                    keys = _heads_to_rows(*refs[:n_half])
                    vals = _heads_to_rows(*refs[n_half:])
                    return jnp.where(j < 4, keys, vals)

                kv_specs = [pl.BlockSpec((pieces, tm, LANES), functools.partial(lambda i, j, src: (kv_block(src, j), i, 0), src=src))
                            for src in (0, 1) for _ in range(n_half)]
                resid, dgain["kv"] = _bwd_matmul_norms(
                    kv_specs, kv_args, kv_tile, N_DEV,
                    pl.BlockSpec((None, None, d, kvb), lambda i, j: (j, 0, 0, 0)), W("kv_w", 0)[0], _ident,
                    s["x_in"], g_kv, dx_mid, None, None, "kv_proj_bwd")
                kv_b_specs = [pl.BlockSpec((pieces, tg, LANES), functools.partial(lambda j, i, src: (kv_block(src, j), i, 0), src=src))
                              for src in (0, 1) for _ in range(n_half)]
                grads["kv_w"][0] = _grad_weight(
                    [pl.BlockSpec((d, tg), lambda j, i: (0, i))], [kvn_t], _ident,
                    kv_b_specs, kv_args, kv_tile,
                    N_DEV, (d, kvb), pl.BlockSpec((None, d, kvb), lambda j, i: (j, 0, 0)), _sds((N_DEV, d, kvb), BF16), t,
                    "grad_kv", a_transposed=True)
        tok = scatter_start([key for key in group(layer) if not key[0].startswith("ffn")], "_mix" + tag)
        if layer > 0:
            prev = saved[layer - 1]
            dx_out, d_ff, dg0, dg3p = _bwd_matmul_norms(
                a_specs, a_args, a_tile, n_steps, w_spec, w_arg, w_mat,
                s["x_in"], gain(layer, 0) + tok, resid, prev["ff"], gain(layer - 1, 3), "mixer_in_bwd" + tag)
            dgain[(layer, 0)], dgain[(layer - 1, 3)] = dg0, dg3p
        else:
            grad_x, dg0 = _bwd_matmul_norms(
                a_specs, a_args, a_tile, n_steps, w_spec, w_arg, w_mat,
                s["x_in"], gain(layer, 0), resid, None, None, "mixer_in_bwd" + tag)
            dgain[(layer, 0)] = dg0

    small_grad = jnp.concatenate(
        [dgain[(layer, k)] for layer in range(depth) for k in range(4)] + [dtaps[layer] for layer in range(n_a)]
        + [dgain["kv"]] + [jnp.zeros((small_rows - n_small - 1, d), F32)], axis=0)
    small_grads_all = _all_gather([small_grad], "gather_small_grads")[0]
    lo = dev * (d // N_DEV)

    def pack(ng, cwp, kvg):
        rows = jnp.concatenate([ng.reshape(4 * depth, -1), cwp.reshape(3 * n_a, -1)], axis=0)
        z = lax.dynamic_update_slice(jnp.zeros((small_rows, d), F32), rows, (0, lo))
        return lax.dynamic_update_slice(z, kvg[None], (n_small, 0))

    w_small = lax.dynamic_update_slice(small_all, g_kv, (n_small, 0))
    m_small, v_small = pack(m_norm_g, m_conv_w, m_kv_norm_g), pack(v_norm_g, v_conv_w, v_kv_norm_g)
    sm = _small_adamw(small_grads_all, w_small, m_small, v_small, "adamw_small")

    def unpack(a):
        mine = lax.dynamic_slice(a, (0, lo), (small_rows, d // N_DEV))
        return (mine[:4 * depth].reshape(norm_g.shape), mine[4 * depth:n_small].reshape(conv_w.shape), a[n_small])

    small_out = [unpack(a) for a in sm]

    moments = {"conv_in_w": (m_conv_in_w, v_conv_in_w), "conv_out_w": (m_conv_out_w, v_conv_out_w),
               "kv_w": (m_kv_w[None], v_kv_w[None]), "q_w": (m_q_w, v_q_w), "o_w": (m_o_w, v_o_w),
               "ffn_in_w": (m_ffn_in_w, v_ffn_in_w), "ffn_out_w": (m_ffn_out_w, v_ffn_out_w)}
    landed = {k: [None] * big[k].shape[0] for k in names}
    for keys, tag, send_sems, recv_sems, parts, zones in scattering:
        parts, zones = _send_wait(send_sems, recv_sems, parts, zones, grad_x, "scatter_wait" + tag)
        for (k, i), part, zone in zip(keys, parts, zones):
            landed[k][i] = (part, zone)
    res = {k: _sum_adamw(slot, [p for p, _ in landed[k]], [z for _, z in landed[k]], big[k], moments[k][0], moments[k][1],
                         "adamw_" + k) for k in names}

    def big_out(k, which):
        return res[k][which][0] if k == "kv_w" else res[k][which]

    out_names = ["norm_g", "conv_in_w", "conv_w", "conv_out_w", "kv_norm_g", "kv_w", "q_w", "o_w", "ffn_in_w", "ffn_out_w"]
    small_pos = {"norm_g": 0, "conv_w": 1, "kv_norm_g": 2}
    outs = [loss, grad_x[None]]
    for which in range(4):
        for k in out_names:
            outs.append(small_out[which][small_pos[k]] if k in small_pos else big_out(k, which))
    return tuple(outs)
```

```python
import functools
import math

import numpy as np
import jax
import jax.numpy as jnp
from jax import lax
from jax.experimental import pallas as pl
from jax.experimental.pallas import tpu as pltpu

F32 = jnp.float32
BF16 = jnp.bfloat16

N_DEV = 8
RMS_EPS = 1e-6
HEAD_DIM = 64
LANES = 128
ATT_BLOCK = 128
DILATIONS = (1, 4, 16)
SUPER = ATT_BLOCK * DILATIONS[-1]
NEG = -1e30
ATT_UNROLL_FWD = 8
ATT_UNROLL_BWD = 8

ADAM_LR, ADAM_B1, ADAM_B2, ADAM_EPS, ADAM_WD, ADAM_STEP = 0.001, 0.9, 0.999, 1e-08, 0.01, 10

ROW_TILE = 512
BIG_ROW_TILE = 1024
GRAD_ROW_TILE = 2048
BWD_ROW_TILE = 512
MESH = pl.DeviceIdType.MESH


def _call(body, *, name, grid=None, in_specs=None, out_specs=None, out_shape=None, scratch_shapes=(), prefetch=False,
          **params):
    cp = pltpu.CompilerParams(**params) if params else None
    if prefetch:
        spec = pltpu.PrefetchScalarGridSpec(num_scalar_prefetch=1, grid=grid, in_specs=in_specs, out_specs=out_specs,
                                            scratch_shapes=list(scratch_shapes))
        return pl.pallas_call(body, name=name, grid_spec=spec, out_shape=out_shape, compiler_params=cp)
    kwargs = {k: v for k, v in (("grid", grid), ("in_specs", in_specs), ("out_specs", out_specs)) if v is not None}
    return pl.pallas_call(body, name=name, out_shape=out_shape, scratch_shapes=list(scratch_shapes),
                          compiler_params=cp, **kwargs)


def _sds(shape, dtype):
    return jax.ShapeDtypeStruct(tuple(shape), dtype)


def _rms(x, g):
    r = lax.rsqrt(jnp.mean(x * x, axis=-1, keepdims=True) + RMS_EPS)
    return x * r * g


def _rms_bwd(x, g, dy):
    r = lax.rsqrt(jnp.mean(x * x, axis=-1, keepdims=True) + RMS_EPS)
    xh = x * r
    dxh = dy * g
    dx = r * (dxh - xh * jnp.mean(dxh * xh, axis=-1, keepdims=True))
    return dx, jnp.sum(dy * xh, axis=0, keepdims=True)


def _dot(a, b):
    return jnp.dot(a, b, preferred_element_type=F32)


def _dot_nt(a, b):
    return lax.dot_general(a, b, (((1,), (1,)), ((), ())), preferred_element_type=F32)


def _dot_tn(a, b):
    return lax.dot_general(a, b, (((0,), (0,)), ((), ())), preferred_element_type=F32)


def _mesh_pos():
    return lax.axis_index("x"), lax.axis_index("y"), lax.axis_index("c")


def _all_gather(arrs, name):
    n = len(arrs)

    def body(*refs):
        ins, outs = refs[:n], refs[n:2 * n]
        send_sems, recv_sems, local_sems = refs[2 * n:]
        x, y, c = _mesh_pos()
        me, sibling = (x, y, c), (x, y, 1 - c)
        chips = [(1 - x, y), (x, 1 - y), (1 - x, 1 - y)]

        def copy(a, k, block, to, src=None):
            dst = outs[a].at[4 * block[0] + 2 * block[1] + block[2]]
            return pltpu.make_async_remote_copy(
                src_ref=dst if src is None else src, dst_ref=dst, send_sem=send_sems.at[a, k],
                recv_sem=recv_sems.at[a, k], device_id=to, device_id_type=MESH)

        started = []
        for a in range(n):
            mine = pltpu.make_async_copy(ins[a], outs[a].at[4 * x + 2 * y + c], local_sems.at[a])
            mine.start()
            started.append(mine)
        first = []
        for a in range(n):
            first.append(copy(a, 0, me, sibling, src=ins[a]))
            first += [copy(a, 1 + j, me, (*chip, c), src=ins[a]) for j, chip in enumerate(chips)]
        for cp in first:
            cp.start()
        passed = []
        for a in range(n):
            for j, chip in enumerate(chips):
                copy(a, 1 + j, (*chip, c), me).wait_recv()
                fwd = copy(a, 4 + j, (*chip, c), sibling)
                fwd.start()
                passed.append(fwd)
        for a in range(n):
            copy(a, 0, sibling, me).wait_recv()
            for j, chip in enumerate(chips):
                copy(a, 4 + j, (*chip, 1 - c), me).wait_recv()
        for cp in first + passed:
            cp.wait_send()
        for cp in started:
            cp.wait()

    any_spec = pl.BlockSpec(memory_space=pl.ANY)
    outs = _call(
        body, name=name, in_specs=[any_spec] * n, out_specs=[any_spec] * n,
        out_shape=[_sds((N_DEV,) + a.shape, a.dtype) for a in arrs],
        scratch_shapes=[pltpu.SemaphoreType.DMA((n, 7)), pltpu.SemaphoreType.DMA((n, 7)), pltpu.SemaphoreType.DMA((n,))],
        has_side_effects=True,
    )(*arrs)
    return list(outs)


HBM_SPEC = pl.BlockSpec(memory_space=pltpu.HBM)
SEM_SPEC = pl.BlockSpec(memory_space=pltpu.SEMAPHORE)
DATAFLOW = pltpu.SideEffectType.DATAFLOW_SIDE_EFFECTING
PEERS = [(dx, dy, dc) for dx in (0, 1) for dy in (0, 1) for dc in (0, 1)][1:]


def _peer(flip):
    x, y, c = _mesh_pos()
    return tuple(1 - v if f else v for v, f in zip((x, y, c), flip))


def _slot(pos):
    return 4 * pos[0] + 2 * pos[1] + pos[2]


def _in_hbm(a):
    return pltpu.with_memory_space_constraint(a, pltpu.HBM)


def _direct_copies(srcs, lands, send_sems, recv_sems, scatter):
    me = _slot(_mesh_pos())
    copies = []
    for a in range(len(lands)):
        for k, flip in enumerate(PEERS):
            peer = _peer(flip)
            src = srcs[a].at[_slot(peer)] if scatter else lands[a].at[me]
            idx = a * len(PEERS) + k
            copies.append(pltpu.make_async_remote_copy(
                src_ref=src, dst_ref=lands[a].at[me], send_sem=send_sems.at[idx], recv_sem=recv_sems.at[idx],
                device_id=peer, device_id_type=MESH))
    return copies


def _send_start(srcs, lands, after, name):
    ns, nl = len(srcs), len(lands)
    scatter = ns > 0

    def body(*refs):
        src_refs, land_refs = refs[:ns], refs[ns:ns + nl]
        send_sems, recv_sems = refs[ns + nl + 1:ns + nl + 3]
        token = refs[-1]
        for cp in _direct_copies(src_refs, land_refs, send_sems, recv_sems, scatter):
            cp.start()
        token[...] = jnp.zeros_like(token)

    sem = pltpu.SemaphoreType.DMA((nl * len(PEERS),))
    outs = pl.pallas_call(
        body, name=name,
        out_shape=(sem, sem) + tuple(pltpu.HBM(a.shape, a.dtype) for a in list(srcs) + list(lands))
        + (_sds((8, LANES), F32),),
        in_specs=[HBM_SPEC] * (ns + nl) + [pl.BlockSpec(memory_space=pl.ANY)],
        out_specs=(SEM_SPEC, SEM_SPEC) + (HBM_SPEC,) * (ns + nl) + (pl.BlockSpec(memory_space=pltpu.VMEM),),
        input_output_aliases={i: 2 + i for i in range(ns + nl)},
        compiler_params=pltpu.CompilerParams(has_side_effects=DATAFLOW),
    )(*[_in_hbm(a) for a in list(srcs) + list(lands)], after)
    send_sems, recv_sems = outs[0], outs[1]
    return send_sems, recv_sems, list(outs[2:2 + ns]), list(outs[2 + ns:2 + ns + nl]), outs[-1]


def _send_wait(send_sems, recv_sems, srcs, lands, after, name):
    ns, nl = len(srcs), len(lands)
    scatter = ns > 0

    def body(*refs):
        src_refs, land_refs = refs[:ns], refs[ns:ns + nl]
        send_sems, recv_sems = refs[ns + nl:ns + nl + 2]
        copies = _direct_copies(src_refs, land_refs, send_sems, recv_sems, scatter)
        for cp in copies:
            cp.wait_send()
        for cp in copies:
            cp.wait_recv()

    outs = pl.pallas_call(
        body, name=name,
        out_shape=tuple(pltpu.HBM(a.shape, a.dtype) for a in list(srcs) + list(lands)),
        in_specs=[HBM_SPEC] * (ns + nl) + [SEM_SPEC, SEM_SPEC, pl.BlockSpec(memory_space=pl.ANY)],
        out_specs=(HBM_SPEC,) * (ns + nl),
        input_output_aliases={i: i for i in range(ns + nl)},
        compiler_params=pltpu.CompilerParams(has_side_effects=DATAFLOW),
    )(*srcs, *lands, send_sems, recv_sems, after)
    return list(outs[:ns]), list(outs[ns:])


def _row_tile(rows, cap=512):
    t = min(rows, cap)
    while rows % t or (t % 16 and t != rows):
        t -= 1
    return t


def _as2d(a):
    return a.reshape(-1, a.shape[-1])


def _cast_layer(w, layer, slot, name):
    _, rows, cols = w.shape
    tr = _row_tile(rows)

    def body(*refs):
        refs[-1][...] = refs[-2][...].astype(BF16)

    if slot is None:
        return _call(body, name=name, grid=(rows // tr,),
                     in_specs=[pl.BlockSpec((None, tr, cols), lambda i: (layer, i, 0))],
                     out_specs=pl.BlockSpec((tr, cols), lambda i: (i, 0)), out_shape=_sds((rows, cols), BF16))(w)
    return _call(body, name=name, grid=(rows // tr,), prefetch=True,
                 in_specs=[pl.BlockSpec((None, tr, cols), lambda i, s: (layer, i, 0))],
                 out_specs=pl.BlockSpec((None, tr, cols), lambda i, s: (s[0], i, 0)),
                 out_shape=_sds((N_DEV, rows, cols), BF16))(slot, w)


def _sum_adamw(slot, parts, lands, w, m, v, name):
    n_l = len(parts)
    _, rows, cols = lands[0].shape
    tr = _row_tile(rows, 128)

    def body(s_ref, *refs):
        p_refs, l_refs = refs[:n_l], refs[n_l:2 * n_l]
        w_ref, m_ref, v_ref, g_ref, d_ref, nm_ref, nv_ref = refs[2 * n_l:]
        for k in range(n_l):
            @pl.when(pl.program_id(0) == k)
            def _(k=k):
                own = p_refs[k][...]
                g = jnp.zeros((tr, cols), F32)
                for j in range(N_DEV):
                    g = g + jnp.where(s_ref[0] == j, own, l_refs[k][j]).astype(F32)
                delta, nm, nv = _adamw_math(w_ref[...], g, m_ref[...], v_ref[...])
                g_ref[...] = g
                d_ref[...] = delta
                nm_ref[...] = nm
                nv_ref[...] = nv

    def own_block(k):
        return pl.BlockSpec((None, tr, cols), lambda l, i, s: (s[0], jnp.where(l == k, i, 0), 0))

    def zone_block(k):
        return pl.BlockSpec((N_DEV, tr, cols), lambda l, i, s: (0, jnp.where(l == k, i, 0), 0))

    lay = pl.BlockSpec((None, tr, cols), lambda l, i, s: (l, i, 0))
    return _call(body, name=name, grid=(n_l, rows // tr), prefetch=True,
                 in_specs=[own_block(k) for k in range(n_l)] + [zone_block(k) for k in range(n_l)] + [lay, lay, lay],
                 out_specs=[lay] * 4, out_shape=[_sds((n_l, rows, cols), F32)] * 4)(slot, *parts, *lands, w, m, v)


def _adamw_math(w, g, m, v):
    m = ADAM_B1 * m + (1.0 - ADAM_B1) * g
    v = ADAM_B2 * v + (1.0 - ADAM_B2) * (g * g)
    m_hat = m / (1.0 - ADAM_B1 ** ADAM_STEP)
    v_hat = v / (1.0 - ADAM_B2 ** ADAM_STEP)
    delta = -ADAM_LR * (m_hat / (jnp.sqrt(v_hat) + ADAM_EPS) + ADAM_WD * w)
    return delta, m, v


def _small_adamw(gathered, w, m, v, name):
    def body(a_ref, w_ref, m_ref, v_ref, g_ref, d_ref, nm_ref, nv_ref):
        g = a_ref[0]
        for k in range(1, N_DEV):
            g = g + a_ref[k]
        delta, nm, nv = _adamw_math(w_ref[...], g, m_ref[...], v_ref[...])
        g_ref[...] = g
        d_ref[...] = delta
        nm_ref[...] = nm
        nv_ref[...] = nv

    return _call(body, name=name, out_shape=[_sds(w.shape, F32)] * 4)(gathered, w, m, v)


def _norm_matmul_cols(x, g, wg, layer, mode, name):
    t, d = x.shape
    nb = wg.shape[-1]
    tm = BIG_ROW_TILE
    pieces = nb // LANES
    per = 2 if mode == "cols" else 1

    def body(x_ref, g_ref, w_ref, y_ref, xnt_ref, xn_ref):
        @pl.when(pl.program_id(1) == 0)
        def _():
            xn = _rms(x_ref[...], g_ref[...])
            xn_ref[...] = xn.astype(BF16)
            xnt_ref[...] = xn.T.astype(BF16)

        y = _dot(xn_ref[...], jnp.concatenate([w_ref[k] for k in range(per)], axis=1))
        if mode == "heads":
            for p in range(pieces):
                y_ref[p] = y[:, p * LANES:(p + 1) * LANES]
        else:
            y_ref[...] = y.astype(BF16)

    if mode == "cols":
        y_shape, y_spec = _sds((t, N_DEV * nb), BF16), pl.BlockSpec((tm, per * nb), lambda i, j: (i, j))
    else:
        y_shape = _sds((N_DEV * pieces, t, LANES), F32)
        y_spec = pl.BlockSpec((pieces, tm, LANES), lambda i, j: (j, i, 0))
    return _call(
        body, name=name, grid=(t // tm, N_DEV // per),
        in_specs=[pl.BlockSpec((tm, d), lambda i, j: (i, 0)), pl.BlockSpec((1, d), lambda i, j: (0, 0)),
                  pl.BlockSpec((per, None, d, nb), lambda i, j: (j, layer, 0, 0))],
        out_specs=[y_spec, pl.BlockSpec((d, tm), lambda i, j: (0, i))],
        out_shape=[y_shape, _sds((d, t), BF16)], scratch_shapes=[pltpu.VMEM((tm, d), BF16)])(x, g, wg)


def _ffn_in_swiglu(x, g, wg, layer, name):
    t, d = x.shape
    fc = wg.shape[-1]
    tm = BIG_ROW_TILE

    def body(x_ref, g_ref, wg_ref, wu_ref, gate_ref, up_ref, a_ref, xnt_ref, xn_ref):
        @pl.when(pl.program_id(1) == 0)
        def _():
            xn = _rms(x_ref[...], g_ref[...])
            xn_ref[...] = xn.astype(BF16)
            xnt_ref[...] = xn.T.astype(BF16)

        xn = xn_ref[...]
        gate, up = _dot(xn, wg_ref[...]), _dot(xn, wu_ref[...])
        gate_ref[...] = gate.astype(BF16)
        up_ref[...] = up.astype(BF16)
        a_ref[...] = (gate * jax.nn.sigmoid(gate) * up).astype(BF16)

    chunk = pl.BlockSpec((None, tm, fc), lambda i, c: (c, i, 0))
    return _call(
        body, name=name, grid=(t // tm, 4),
        in_specs=[pl.BlockSpec((tm, d), lambda i, c: (i, 0)), pl.BlockSpec((1, d), lambda i, c: (0, 0)),
                  pl.BlockSpec((None, None, d, fc), lambda i, c: (c, layer, 0, 0)),
                  pl.BlockSpec((None, None, d, fc), lambda i, c: (c + 4, layer, 0, 0))],
        out_specs=[chunk, chunk, chunk, pl.BlockSpec((d, tm), lambda i, c: (0, i))],
        out_shape=[_sds((4, t, fc), BF16)] * 3 + [_sds((d, t), BF16)],
        scratch_shapes=[pltpu.VMEM((tm, d), BF16)])(x, g, wg, wg)


def _norm_matmul_heads(x, g, wg, layer, scale, name):
    t, d = x.shape
    tm = ROW_TILE
    hp = d // LANES

    def body(x_ref, g_ref, w_ref, y_ref, xn_ref):
        xn = _rms(x_ref[...], g_ref[...]).astype(BF16)
        xn_ref[...] = xn
        y = _dot(xn, w_ref[...].reshape(d, d)) * scale
        for p in range(hp):
            y_ref[p] = y[:, p * LANES:(p + 1) * LANES]

    return _call(
        body, name=name, grid=(t // tm,),
        in_specs=[pl.BlockSpec((tm, d), lambda i: (i, 0)), pl.BlockSpec((1, d), lambda i: (0, 0)),
                  pl.BlockSpec((N_DEV, None, d // N_DEV, d), lambda i: (0, layer, 0, 0))],
        out_specs=[pl.BlockSpec((hp, tm, LANES), lambda i: (0, i, 0)), pl.BlockSpec((tm, d), lambda i: (i, 0))],
        out_shape=[_sds((hp, t, LANES), F32), _sds((t, d), BF16)])(x, g, wg)


def _shift_down(u, halo, k, tm):
    row = lax.broadcasted_iota(jnp.int32, u.shape, 0)
    out = pltpu.roll(u, k, 0)
    for j in range(k):
        out = jnp.where(row == j, halo[halo.shape[0] - k + j:halo.shape[0] - k + j + 1, :], out)
    return out


def _shift_up(u, halo, k, tm):
    row = lax.broadcasted_iota(jnp.int32, u.shape, 0)
    out = pltpu.roll(u, tm - k, 0)
    for j in range(k):
        out = jnp.where(row == tm - k + j, halo[j:j + 1, :], out)
    return out


HALO = 16


def _conv_fwd(p, cw, name):
    t, d3 = p.shape
    d = d3 // 3
    tm = ROW_TILE
    hb = tm // HALO

    def body(p_ref, prev_ref, cw_ref, z_ref):
        i = pl.program_id(0)
        b = p_ref[:, 0:d].astype(F32)
        u = p_ref[:, d:2 * d].astype(F32) * p_ref[:, 2 * d:3 * d].astype(F32)
        keep = (i > 0).astype(F32)
        hu = prev_ref[:, d:2 * d].astype(F32) * prev_ref[:, 2 * d:3 * d].astype(F32) * keep
        uc = cw_ref[2:3, :] * u + cw_ref[1:2, :] * _shift_down(u, hu, 1, tm) + cw_ref[0:1, :] * _shift_down(u, hu, 2, tm)
        z_ref[...] = (b * uc).astype(BF16)

    return _call(
        body, name=name, grid=(t // tm,),
        in_specs=[pl.BlockSpec((tm, d3), lambda i: (i, 0)),
                  pl.BlockSpec((HALO, d3), lambda i: (jnp.maximum(i * hb - 1, 0), 0)),
                  pl.BlockSpec((3, d), lambda i: (0, 0))],
        out_specs=pl.BlockSpec((tm, d), lambda i: (i, 0)), out_shape=_sds((t, d), BF16))(p, p, cw)


def _matmul_norm_residual(a3, wg, layer, g, x_res, name):
    kc_n, t, kc = a3.shape
    d = wg.shape[-1]
    per = N_DEV // kc_n
    rows = wg.shape[2]
    tm = ROW_TILE

    def body(a_ref, w_ref, g_ref, x_ref, raw_ref, xo_ref):
        raw = None
        for c in range(kc_n):
            term = _dot(a_ref[c], w_ref[c * per:(c + 1) * per].reshape(per * rows, d))
            raw = term if raw is None else raw + term
        raw_ref[...] = raw
        xo_ref[...] = x_ref[...] + _rms(raw, g_ref[...])

    row_spec = pl.BlockSpec((tm, d), lambda i: (i, 0))
    return _call(
        body, name=name, grid=(t // tm,),
        in_specs=[pl.BlockSpec((kc_n, tm, kc), lambda i: (0, i, 0)),
                  pl.BlockSpec((N_DEV, None, rows, d), lambda i: (0, layer, 0, 0)),
                  pl.BlockSpec((1, d), lambda i: (0, 0)), row_spec],
        out_specs=[row_spec, row_spec], out_shape=[_sds((t, d), F32)] * 2)(a3, wg, g, x_res)


def _alibi_slopes(n_heads):
    hh = np.arange(n_heads, dtype=np.float32) + 1.0
    s = np.power(2.0, -8.0 * hh / n_heads).astype(np.float32)
    return jnp.asarray(np.repeat(s.reshape(n_heads // 2, 2, 1), 2 * ATT_BLOCK, axis=2))


def _band_bias(sl_ref, dil):
    u = lax.broadcasted_iota(jnp.int32, (ATT_BLOCK, 2 * ATT_BLOCK), 0)
    kk = lax.broadcasted_iota(jnp.int32, (ATT_BLOCK, 2 * ATT_BLOCK), 1)
    delta = u + ATT_BLOCK - kk
    valid = (delta >= 0) & (delta <= ATT_BLOCK)
    dist = (delta * dil).astype(F32)
    rows = [jnp.where(valid, -sl_ref[hd:hd + 1, :] * dist, NEG) for hd in range(2)]
    return jnp.concatenate(rows, axis=0)


def _stack_heads(a):
    lane = lax.broadcasted_iota(jnp.int32, a.shape, 1)
    return jnp.concatenate([jnp.where(lane < HEAD_DIM, a, 0.0), jnp.where(lane >= HEAD_DIM, a, 0.0)], axis=0).astype(BF16)


def _unstack_heads(a2):
    top, bot = a2[:ATT_BLOCK], a2[ATT_BLOCK:]
    lane = lax.broadcasted_iota(jnp.int32, top.shape, 1)
    return jnp.where(lane < HEAD_DIM, top, bot)


def _rows_to_lanes(a0, a1):
    eye = lax.broadcasted_iota(jnp.int32, a0.shape, 0) == lax.broadcasted_iota(jnp.int32, a0.shape, 1)
    return jnp.concatenate([jnp.sum(jnp.where(eye, a, 0.0), axis=0, keepdims=True) for a in (a0, a1)], axis=1)


def _fill_bias_t(sl_ref, bias_ref):
    kk = lax.broadcasted_iota(jnp.int32, (2 * ATT_BLOCK, 2 * ATT_BLOCK), 0)
    lane = lax.broadcasted_iota(jnp.int32, (2 * ATT_BLOCK, 2 * ATT_BLOCK), 1)
    delta = lane % ATT_BLOCK + ATT_BLOCK - kk
    valid = (delta >= 0) & (delta <= ATT_BLOCK)
    slope = jnp.concatenate([sl_ref[0:1, :ATT_BLOCK], sl_ref[1:2, :ATT_BLOCK]], axis=1)
    for gi, dil in enumerate(DILATIONS):
        bias = jnp.where(valid, -slope * (delta * dil).astype(F32), NEG)
        bias_ref[2 * gi] = bias
        bias_ref[2 * gi + 1] = jnp.where(kk < ATT_BLOCK, NEG, bias)


def _fill_bias(sl_ref, bias_ref):
    kk = lax.broadcasted_iota(jnp.int32, (2 * ATT_BLOCK, 2 * ATT_BLOCK), 1)
    for gi, dil in enumerate(DILATIONS):
        bias = _band_bias(sl_ref, dil)
        bias_ref[2 * gi] = bias
        bias_ref[2 * gi + 1] = jnp.where(kk < ATT_BLOCK, NEG, bias)


def _attention_fwd(q, kv, slopes, name):
    hp, t, _ = q.shape
    ns = t // SUPER
    nd = len(DILATIONS)

    def body(sl_ref, q_ref, kc_ref, kp_ref, vc_ref, vp_ref, o_ref, lse_ref, kw_ref, vw_ref, og_ref, lg_ref, bias_ref):
        n = pl.program_id(1)
        kw_ref[0:SUPER, :] = kp_ref[...]
        kw_ref[SUPER:, :] = kc_ref[...]
        vw_ref[0:SUPER, :] = vp_ref[...]
        vw_ref[SUPER:, :] = vc_ref[...]

        @pl.when(n == 0)
        def _():
            _fill_bias(sl_ref, bias_ref)

        for gi, dil in enumerate(DILATIONS):

            def block(idx, carry, gi=gi, dil=dil):
                r, b = idx % dil, idx // dil
                qs = b * (ATT_BLOCK * dil) + r
                ks = SUPER + (b - 1) * (ATT_BLOCK * dil) + r
                first = jnp.logical_and(n == 0, b == 0).astype(jnp.int32)
                q2 = _stack_heads(q_ref[pl.ds(qs, ATT_BLOCK, stride=dil), :])
                kb = kw_ref[pl.ds(ks, 2 * ATT_BLOCK, stride=dil), :].astype(BF16)
                vb = vw_ref[pl.ds(ks, 2 * ATT_BLOCK, stride=dil), :].astype(BF16)
                s = _dot_nt(q2, kb) + bias_ref[2 * gi + first]
                m = jnp.max(s, axis=-1, keepdims=True)
                p = jnp.exp(s - m).astype(BF16)
                ol = _dot(p, jnp.concatenate([vb, jnp.ones_like(vb)], axis=1))
                l = ol[:, LANES:]
                o2 = ol[:, :LANES] / l
                lse2 = m + jnp.log(l)
                og_ref[gi, pl.ds(qs, ATT_BLOCK, stride=dil), :] = _unstack_heads(o2)
                lg_ref[gi, pl.ds(qs, ATT_BLOCK, stride=dil), :] = _unstack_heads(lse2)
                return carry

            lax.fori_loop(0, SUPER // ATT_BLOCK, block, 0, unroll=ATT_UNROLL_FWD)
        lg =[lg_ref[gi] for gi in range(nd)]
        top = functools.reduce(jnp.maximum, lg)
        ws = [jnp.exp(x - top) for x in lg]
        tot = functools.reduce(jnp.add, ws)
        lse_ref[...] = top + jnp.log(tot)
        acc = ws[0] * og_ref[0]
        for gi in range(1, nd):
            acc = acc + ws[gi] * og_ref[gi]
        o_ref[...] = (acc / tot).astype(BF16)

    cur = lambda off: pl.BlockSpec((None, SUPER, LANES), lambda h, n: (h + off, n, 0))
    prev = lambda off: pl.BlockSpec((None, SUPER, LANES), lambda h, n: (h + off, jnp.maximum(n - 1, 0), 0))
    return _call(
        body, name=name, grid=(hp, ns),
        in_specs=[pl.BlockSpec((None, 2, 2 * ATT_BLOCK), lambda h, n: (h, 0, 0)), cur(0), cur(0), prev(0), cur(hp), prev(hp)],
        out_specs=[pl.BlockSpec((SUPER, LANES), lambda h, n: (n, h)), cur(0)],
        out_shape=[_sds((t, hp * LANES), BF16), _sds((hp, t, LANES), F32)],
        scratch_shapes=[pltpu.VMEM((2 * SUPER, LANES), F32), pltpu.VMEM((2 * SUPER, LANES), F32),
                        pltpu.VMEM((nd, SUPER, LANES), F32), pltpu.VMEM((nd, SUPER, LANES), F32),
                        pltpu.VMEM((2 * nd, 2 * ATT_BLOCK, 2 * ATT_BLOCK), F32)],
    )(slopes, q, kv, kv, kv, kv)


def _attention_bwd(q, kv, o, lse, d_o, slopes, q_scale, name):
    hp, t, _ = q.shape
    ns = t // SUPER

    def body(sl_ref, q_ref, kc_ref, kp_ref, vc_ref, vp_ref, o_ref, lse_ref, do_ref,
             dq_ref, dk_ref, dv_ref, kw_ref, vw_ref, dkw_ref, dvw_ref, st_ref, bias_ref):
        n = pl.program_id(1)

        @pl.when(n == 0)
        def _():
            dkw_ref[...] = jnp.zeros_like(dkw_ref)
            dvw_ref[...] = jnp.zeros_like(dvw_ref)

        @pl.when(n > 0)
        def _():
            dkw_ref[0:SUPER, :] = dkw_ref[SUPER:, :]
            dvw_ref[0:SUPER, :] = dvw_ref[SUPER:, :]
            dkw_ref[SUPER:, :] = jnp.zeros((SUPER, LANES), F32)
            dvw_ref[SUPER:, :] = jnp.zeros((SUPER, LANES), F32)

        @pl.when(n < ns)
        def _():
            kw_ref[0:SUPER, :] = kp_ref[...]
            kw_ref[SUPER:, :] = kc_ref[...]
            vw_ref[0:SUPER, :] = vp_ref[...]
            vw_ref[SUPER:, :] = vc_ref[...]
            prod = do_ref[...] * o_ref[...].astype(F32)
            lane = lax.broadcasted_iota(jnp.int32, prod.shape, 1)
            zero = jnp.zeros((SUPER, LANES), F32)
            st_ref[0] = zero + jnp.sum(jnp.where(lane < HEAD_DIM, prod, 0.0), axis=-1, keepdims=True)
            st_ref[1] = zero + jnp.sum(jnp.where(lane >= HEAD_DIM, prod, 0.0), axis=-1, keepdims=True)
            lse = lse_ref[...]
            swapped = pltpu.roll(lse, HEAD_DIM, 1)
            st_ref[2] = jnp.where(lane < HEAD_DIM, lse, swapped)
            st_ref[3] = jnp.where(lane >= HEAD_DIM, lse, swapped)
            dq_ref[...] = jnp.zeros_like(dq_ref)

            @pl.when(n == 0)
            def _():
                _fill_bias_t(sl_ref, bias_ref)

            for gi, dil in enumerate(DILATIONS):

                def block(idx, carry, gi=gi, dil=dil):
                    r, b = idx % dil, idx // dil
                    qs = b * (ATT_BLOCK * dil) + r
                    ks = SUPER + (b - 1) * (ATT_BLOCK * dil) + r
                    first = jnp.logical_and(n == 0, b == 0).astype(jnp.int32)
                    rows = pl.ds(qs, ATT_BLOCK, stride=dil)
                    keys = pl.ds(ks, 2 * ATT_BLOCK, stride=dil)
                    q2 = _stack_heads(q_ref[rows, :])
                    do2 = _stack_heads(do_ref[rows, :])
                    kb = kw_ref[keys, :].astype(BF16)
                    vb = vw_ref[keys, :].astype(BF16)
                    dd = _rows_to_lanes(st_ref[0, rows, :], st_ref[1, rows, :])
                    lse_b = _rows_to_lanes(st_ref[2, rows, :], st_ref[3, rows, :])
                    ps, dss = [], []
                    for half in range(2):
                        hk = slice(half * ATT_BLOCK, (half + 1) * ATT_BLOCK)
                        p = jnp.exp(_dot_nt(kb[hk], q2) + bias_ref[2 * gi + first, hk, :] - lse_b)
                        dss.append((p * (_dot_nt(vb[hk], do2) - dd)).astype(BF16))
                        ps.append(p.astype(BF16))
                    p, ds = jnp.concatenate(ps, axis=0), jnp.concatenate(dss, axis=0)
                    dvw_ref[keys, :] += _dot(p, do2)
                    dkw_ref[keys, :] += _dot(ds, q2)
                    dq_ref[rows, :] += _unstack_heads(_dot_tn(ds, kb)) * q_scale
                    return carry

                lax.fori_loop(0, SUPER // ATT_BLOCK, block, 0, unroll=ATT_UNROLL_BWD)

        dk_ref[...] = dkw_ref[0:SUPER, :]
        dv_ref[...] = dvw_ref[0:SUPER, :]

    last = ns - 1
    cur = lambda off: pl.BlockSpec((None, SUPER, LANES), lambda h, n: (h + off, jnp.minimum(n, last), 0))
    prev = lambda off: pl.BlockSpec((None, SUPER, LANES), lambda h, n: (h + off, jnp.clip(n - 1, 0, last), 0))
    nat = pl.BlockSpec((SUPER, LANES), lambda h, n: (jnp.minimum(n, last), h))
    late = pl.BlockSpec((None, SUPER, LANES), lambda h, n: (h, jnp.maximum(n - 1, 0), 0))
    dq, dk, dv = _call(
        body, name=name, grid=(hp, ns + 1),
        in_specs=[pl.BlockSpec((None, 2, 2 * ATT_BLOCK), lambda h, n: (h, 0, 0)), cur(0), cur(0), prev(0), cur(hp), prev(hp),
                  nat, cur(0), nat],
        out_specs=[cur(0), late, late],
        out_shape=[_sds((hp, t, LANES), F32)] * 3,
        scratch_shapes=[pltpu.VMEM((2 * SUPER, LANES), F32)] * 4 + [
            pltpu.VMEM((4, SUPER, LANES), F32), pltpu.VMEM((2 * len(DILATIONS), 2 * ATT_BLOCK, 2 * ATT_BLOCK), F32)],
    )(slopes, q, kv, kv, kv, kv, o, lse, d_o)
    return dq, dk, dv


def _loss_head(y, target, raw, g, name):
    t, d = y.shape
    tm = ROW_TILE

    def body(y_ref, t_ref, raw_ref, g_ref, sq_ref, dy_ref, draw_ref, dg_ref):
        i = pl.program_id(0)
        err = y_ref[...] - t_ref[...]
        dy = err * (1.0 / d)
        dy_ref[...] = dy
        draw, dg = _rms_bwd(raw_ref[...], g_ref[...], dy)
        draw_ref[...] = draw.astype(BF16)
        sq = jnp.zeros((8, LANES), F32) + jnp.sum(err * err)

        @pl.when(i == 0)
        def _():
            sq_ref[...] = sq
            dg_ref[...] = dg

        @pl.when(i > 0)
        def _():
            sq_ref[...] += sq
            dg_ref[...] += dg

    row = pl.BlockSpec((tm, d), lambda i: (i, 0))
    vec = pl.BlockSpec((1, d), lambda i: (0, 0))
    return _call(
        body, name=name, grid=(t // tm,), in_specs=[row, row, row, vec],
        out_specs=[pl.BlockSpec((8, LANES), lambda i: (0, 0)), row, row, vec],
        out_shape=[_sds((8, LANES), F32), _sds((t, d), F32), _sds((t, d), BF16), _sds((1, d), F32)])(y, target, raw, g)


def _bwd_matmul_norms(a_specs, a_args, a_tile, n_steps, w_spec, w_arg, w_mat, xa, ga, resid, xb, gb, name):
    t, d = xa.shape
    tm = BWD_ROW_TILE
    na = len(a_specs)
    second = xb is not None
    per = 4 if n_steps % 4 == 0 else 1
    n_steps //= per

    def blocks_of(spec, k):
        return pl.BlockSpec(spec.block_shape, lambda i, j: spec.index_map(i, per * j + k))

    def body(*refs):
        a_refs, w_refs = refs[:per * na], refs[per * na:per * na + per]
        xa_ref, ga_ref, res_ref = refs[per * na + per:per * na + per + 3]
        rest = refs[per * na + per + 3:]
        if second:
            xb_ref, gb_ref, dx_ref, d2_ref, dga_ref, dgb_ref, acc_ref = rest
        else:
            dx_ref, dga_ref, acc_ref = rest
        i, j = pl.program_id(0), pl.program_id(1)
        part = None
        for k in range(per):
            term = _dot_nt(a_tile(per * j + k, *a_refs[k * na:(k + 1) * na]), w_mat(w_refs[k]))
            part = term if part is None else part + term

        @pl.when(j == 0)
        def _():
            acc_ref[...] = part

        @pl.when(j > 0)
        def _():
            acc_ref[...] += part

        @pl.when(j == n_steps - 1)
        def _():
            da, dga = _rms_bwd(xa_ref[...], ga_ref[...], acc_ref[...])
            dx = res_ref[...] + da
            dx_ref[...] = dx
            if second:
                d2, dgb = _rms_bwd(xb_ref[...], gb_ref[...], dx)
                d2_ref[...] = d2.astype(BF16)

            @pl.when(i == 0)
            def _():
                dga_ref[...] = dga
                if second:
                    dgb_ref[...] = dgb

            @pl.when(i > 0)
            def _():
                dga_ref[...] += dga
                if second:
                    dgb_ref[...] += dgb

    row = pl.BlockSpec((tm, d), lambda i, j: (i, 0))
    vec = pl.BlockSpec((1, d), lambda i, j: (0, 0))
    in_specs = [blocks_of(sp, k) for k in range(per) for sp in a_specs] + [blocks_of(w_spec, k) for k in range(per)]
    in_specs += [row, vec, row]
    args = list(a_args) * per + [w_arg] * per + [xa, ga, resid]
    if second:
        in_specs += [row, vec]
        args += [xb, gb]
        out_specs = [row, row, vec, vec]
        out_shape = [_sds((t, d), F32), _sds((t, d), BF16), _sds((1, d), F32), _sds((1, d), F32)]
    else:
        out_specs = [row, vec]
        out_shape = [_sds((t, d), F32), _sds((1, d), F32)]
    return _call(body, name=name, grid=(t // tm, n_steps), in_specs=in_specs, out_specs=out_specs,
                 out_shape=out_shape, scratch_shapes=[pltpu.VMEM((tm, d), F32)])(*args)


def _heads_to_rows(*refs):
    hp = refs[0].shape[0]
    cols = []
    for p in range(hp):
        v = refs[0][p]
        for r in refs[1:]:
            v = v + r[p]
        cols.append(v)
    return jnp.concatenate(cols, axis=-1).astype(BF16)


def _matmul_nt_rows(a, wg, layer, out_dtype, name):
    t, d = a.shape
    tm = ROW_TILE

    def body(a_ref, w_ref, o_ref):
        o_ref[...] = _dot_nt(a_ref[...], w_ref[...].reshape(d, d)).astype(out_dtype)

    row = pl.BlockSpec((tm, d), lambda i: (i, 0))
    return _call(body, name=name, grid=(t // tm,),
                 in_specs=[row, pl.BlockSpec((N_DEV, None, d // N_DEV, d), lambda i: (0, layer, 0, 0))],
                 out_specs=row, out_shape=_sds((t, d), out_dtype))(a, wg)


def _swiglu_bwd(d_ff, wg, layer, gate, up, name):
    t, d = d_ff.shape
    fc = gate.shape[-1]
    rows = wg.shape[2]
    tm = BIG_ROW_TILE

    def body(df_ref, w_ref, g_ref, u_ref, dh_ref):
        da = _dot_nt(df_ref[...], w_ref[...].reshape(2 * rows, d))
        gate, up = g_ref[...].astype(F32), u_ref[...].astype(F32)
        sig = jax.nn.sigmoid(gate)
        dh_ref[0] = (da * up * (sig * (1.0 + gate * (1.0 - sig)))).astype(BF16)
        dh_ref[1] = (da * (gate * sig)).astype(BF16)

    return _call(
        body, name=name, grid=(t // tm, 4),
        in_specs=[pl.BlockSpec((tm, d), lambda i, c: (i, 0)),
                  pl.BlockSpec((2, None, rows, d), lambda i, c: (c, layer, 0, 0)),
                  pl.BlockSpec((None, tm, fc), lambda i, c: (c, i, 0)),
                  pl.BlockSpec((None, tm, fc), lambda i, c: (c, i, 0))],
        out_specs=pl.BlockSpec((None, 2, tm, fc), lambda i, c: (c, 0, i, 0)),
        out_shape=_sds((4, 2, t, fc), BF16))(d_ff, wg, gate, up)


def _conv_bwd(p, d_z, cw, name):
    t, d3 = p.shape
    d = d3 // 3
    tm = ROW_TILE
    hb = tm // HALO
    nt = t // tm

    def body(p_ref, prev_ref, next_ref, dz_ref, dzn_ref, cw_ref, dp_ref, dcw_ref):
        i = pl.program_id(0)
        b = p_ref[:, 0:d].astype(F32)
        c = p_ref[:, d:2 * d].astype(F32)
        h = p_ref[:, 2 * d:3 * d].astype(F32)
        u = c * h
        hu = prev_ref[:, d:2 * d].astype(F32) * prev_ref[:, 2 * d:3 * d].astype(F32) * (i > 0).astype(F32)
        u1, u2 = _shift_down(u, hu, 1, tm), _shift_down(u, hu, 2, tm)
        uc = cw_ref[2:3, :] * u + cw_ref[1:2, :] * u1 + cw_ref[0:1, :] * u2
        dz = dz_ref[...].astype(F32)
        duc = dz * b
        dn = dzn_ref[...].astype(F32) * next_ref[:, 0:d].astype(F32) * (i < nt - 1).astype(F32)
        du = cw_ref[2:3, :] * duc + cw_ref[1:2, :] * _shift_up(duc, dn, 1, tm) + cw_ref[0:1, :] * _shift_up(duc, dn, 2, tm)
        dp_ref[:, 0:d] = (dz * uc).astype(BF16)
        dp_ref[:, d:2 * d] = (du * h).astype(BF16)
        dp_ref[:, 2 * d:3 * d] = (du * c).astype(BF16)
        dcw = jnp.concatenate([jnp.sum(duc * u2, axis=0, keepdims=True), jnp.sum(duc * u1, axis=0, keepdims=True),
                               jnp.sum(duc * u, axis=0, keepdims=True)], axis=0)

        @pl.when(i == 0)
        def _():
            dcw_ref[...] = dcw

        @pl.when(i > 0)
        def _():
            dcw_ref[...] += dcw

    last_halo = t // HALO - 1
    return _call(
        body, name=name, grid=(nt,),
        in_specs=[pl.BlockSpec((tm, d3), lambda i: (i, 0)),
                  pl.BlockSpec((HALO, d3), lambda i: (jnp.maximum(i * hb - 1, 0), 0)),
                  pl.BlockSpec((HALO, d3), lambda i: (jnp.minimum((i + 1) * hb, last_halo), 0)),
                  pl.BlockSpec((tm, d), lambda i: (i, 0)),
                  pl.BlockSpec((HALO, d), lambda i: (jnp.minimum((i + 1) * hb, last_halo), 0)),
                  pl.BlockSpec((3, d), lambda i: (0, 0))],
        out_specs=[pl.BlockSpec((tm, d3), lambda i: (i, 0)), pl.BlockSpec((3, d), lambda i: (0, 0))],
        out_shape=[_sds((t, d3), BF16), _sds((3, d), F32)])(p, p, p, d_z, d_z, cw)


def _grad_weight(a_specs, a_args, a_tile, b_specs, b_args, b_tile, n_out, acc_shape, out_spec, out_shape, t, name,
                 a_transposed=False):
    tt = GRAD_ROW_TILE
    na, nb = len(a_specs), len(b_specs)

    def body(*refs):
        a_refs, b_refs = refs[:na], refs[na:na + nb]
        o_ref, acc_ref = refs[na + nb:]
        s = pl.program_id(1)
        a, b = a_tile(pl.program_id(0), *a_refs), b_tile(pl.program_id(0), *b_refs)
        part = _dot(a, b) if a_transposed else _dot_tn(a, b)

        @pl.when(s == 0)
        def _():
            acc_ref[...] = part

        @pl.when(s > 0)
        def _():
            acc_ref[...] += part

        @pl.when(s == t // tt - 1)
        def _():
            acc = acc_ref[...].astype(BF16)
            if o_ref.shape[-1] == acc.shape[-1]:
                o_ref[...] = acc.reshape(o_ref.shape)
            else:
                for k in range(o_ref.shape[0]):
                    o_ref[k] = acc[:, k * o_ref.shape[-1]:(k + 1) * o_ref.shape[-1]]

    return _call(body, name=name, grid=(n_out, t // tt), in_specs=list(a_specs) + list(b_specs), out_specs=out_spec,
                 out_shape=out_shape, scratch_shapes=[pltpu.VMEM(acc_shape, F32)])(*a_args, *b_args)


def _ident(*args):
    return args[-1][...]


def _heads_tile(j, *refs):
    return _heads_to_rows(*refs)


def kernel(x, norm_g, conv_in_w, conv_w, conv_out_w, kv_norm_g, kv_w, q_w, o_w, ffn_in_w, ffn_out_w, loss_target, m_norm_g, m_conv_in_w, m_conv_w, m_conv_out_w, m_kv_norm_g, m_kv_w, m_q_w, m_o_w, m_ffn_in_w, m_ffn_out_w, v_norm_g, v_conv_in_w, v_conv_w, v_conv_out_w, v_kv_norm_g, v_kv_w, v_q_w, v_o_w, v_ffn_in_w, v_ffn_out_w):
    x0 = x[0]
    target = loss_target[0]
    t, d = x0.shape
    depth = norm_g.shape[0]
    n_a = conv_in_w.shape[0]
    n_b = q_w.shape[0]
    hp = d // LANES
    tm, tg = BWD_ROW_TILE, GRAD_ROW_TILE
    assert t % SUPER == 0 and d % LANES == 0 and depth == n_a + n_b
    dev = 4 * lax.axis_index("x") + 2 * lax.axis_index("y") + lax.axis_index("c")

    n_small = 4 * depth + 3 * n_a
    small_rows = -(-(n_small + 1) // 8) * 8
    small_local = jnp.concatenate([norm_g.reshape(4 * depth, -1), conv_w.reshape(3 * n_a, -1),
                                   jnp.zeros((small_rows - n_small, norm_g.shape[-1]), F32)], axis=0)
    big = {"conv_in_w": conv_in_w, "conv_out_w": conv_out_w, "kv_w": kv_w[None], "q_w": q_w, "o_w": o_w,
           "ffn_in_w": ffn_in_w, "ffn_out_w": ffn_out_w}
    names = list(big)

    def group(layer):
        if layer < n_a:
            return [("conv_in_w", layer), ("conv_out_w", layer), ("ffn_in_w", layer), ("ffn_out_w", layer)]
        j = layer - n_a
        return ([("kv_w", 0)] if j == 0 else []) + [("q_w", j), ("o_w", j), ("ffn_in_w", layer), ("ffn_out_w", layer)]

    slot = dev.astype(jnp.int32).reshape(1)
    first = _all_gather([small_local] + [_cast_layer(big[k], i, None, f"cast_{k}_{i}") for k, i in group(0)], "gather_weights")
    small_all = first[0].transpose(1, 0, 2).reshape(small_rows, d)
    wl = {key: a[:, None] for key, a in zip(group(0), first[1:])}
    def gather_start(layer, after):
        lands = [_cast_layer(big[k], i, slot, f"cast_{k}_{i}") for k, i in group(layer)]
        send_sems, recv_sems, _, lands, tok = _send_start([], lands, after, f"gather_start_l{layer}")
        return (send_sems, recv_sems, lands), tok[0, 0]

    in_flight, token = gather_start(1, small_all)
    W = lambda k, i: (wl[(k, i)], 0)
    gain = lambda layer, k: small_all[4 * layer + k][None]
    taps = lambda layer: small_all[4 * depth + 3 * layer: 4 * depth + 3 * layer + 3]
    g_kv = kv_norm_g[None]
    slopes = _alibi_slopes(d // HEAD_DIM)
    fc = big["ffn_in_w"].shape[-1]
    cb = big["conv_in_w"].shape[-1]
    kvb = big["kv_w"].shape[-1]
    q_scale = HEAD_DIM ** -0.5

    saved = []
    kv = kvn_t = None
    xs = x0
    for layer in range(depth):
        tag = f"_l{layer}"
        if layer > 0:
            send_sems, recv_sems, lands = in_flight
            _, lands = _send_wait(send_sems, recv_sems, [], lands, xs, f"gather_wait_l{layer}")
            wl.update({key: a[:, None] for key, a in zip(group(layer), lands)})
            if layer + 1 < depth:
                in_flight, token = gather_start(layer + 1, xs)
        s = {"x_in": xs}
        g0 = gain(layer, 0) + token if layer + 1 < depth else gain(layer, 0)
        if layer < n_a:
            s["p"], s["xn_t"] = _norm_matmul_cols(xs, g0, *W("conv_in_w", layer), "cols", "conv_in" + tag)
            s["z"] = _conv_fwd(s["p"], taps(layer), "conv" + tag)
            s["mix"], x_mid = _matmul_norm_residual(s["z"][None], *W("conv_out_w", layer), gain(layer, 1), xs, "conv_out" + tag)
        else:
            j = layer - n_a
            if kv is None:
                kv, kvn_t = _norm_matmul_cols(xs, g_kv, *W("kv_w", 0), "heads", "kv_proj")
            s["q"], s["xn"] = _norm_matmul_heads(xs, g0, *W("q_w", j), q_scale, "q_proj" + tag)
            s["o"], s["lse"] = _attention_fwd(s["q"], kv, slopes, "attention" + tag)
            s["mix"], x_mid = _matmul_norm_residual(s["o"][None], *W("o_w", j), gain(layer, 1), xs, "o_proj" + tag)
        s["x_mid"] = x_mid
        s["gate"], s["up"], s["a"], s["fn_t"] = _ffn_in_swiglu(x_mid, gain(layer, 2), *W("ffn_in_w", layer), "ffn_in" + tag)
        s["ff"], xs = _matmul_norm_residual(s["a"], *W("ffn_out_w", layer), gain(layer, 3), x_mid, "ffn_out" + tag)
        saved.append(s)

    last = saved[-1]
    sq, dx_out, d_ff, dg3 = _loss_head(xs, target, last["ff"], gain(depth - 1, 3), "loss_head")
    loss = lax.psum(sq[0, 0] * (0.5 / d), ("x", "y", "c"))

    dgain = {(depth - 1, 3): dg3}
    dtaps = {}
    grads = {k: [None] * big[k].shape[0] for k in names}
    dkv_parts = []
    scattering = []

    def scatter_start(keys, tag):
        parts = [grads[k][i] for k, i in keys]
        zones = [lax.empty(p.shape, p.dtype) for p in parts]
        send_sems, recv_sems, parts, zones, tok = _send_start(parts, zones, small_all, "scatter_start" + tag)
        scattering.append((keys, tag, send_sems, recv_sems, parts, zones))
        return tok[0, 0]

    for layer in reversed(range(depth)):
        tag = f"_l{layer}"
        s = saved[layer]
        dh = _swiglu_bwd(d_ff, *W("ffn_out_w", layer), s["gate"], s["up"], "swiglu_bwd" + tag)
        rows_out = big["ffn_out_w"].shape[1]
        grads["ffn_out_w"][layer] = _grad_weight(
            [pl.BlockSpec((None, tg, fc), lambda c, i: (c, i, 0))], [s["a"]], _ident,
            [pl.BlockSpec((tg, d), lambda c, i: (i, 0))], [d_ff], _ident,
            4, (fc, d), pl.BlockSpec((2, rows_out, d), lambda c, i: (c, 0, 0)), _sds((N_DEV, rows_out, d), BF16), t,
            "grad_ffn_out" + tag)
        grads["ffn_in_w"][layer] = _grad_weight(
            [pl.BlockSpec((d, tg), lambda j, i: (0, i))], [s["fn_t"]], _ident,
            [pl.BlockSpec((None, None, tg, fc), lambda j, i: (j % 4, j // 4, i, 0))], [dh], _ident,
            N_DEV, (d, fc), pl.BlockSpec((None, d, fc), lambda j, i: (j, 0, 0)), _sds((N_DEV, d, fc), BF16), t,
            "grad_ffn_in" + tag, a_transposed=True)
        tok = scatter_start([("ffn_in_w", layer), ("ffn_out_w", layer)], "_ffn" + tag)
        dx_mid, d_mix, dg2, dg1 = _bwd_matmul_norms(
            [pl.BlockSpec((None, None, tm, fc), lambda i, j: (j % 4, j // 4, i, 0))], [dh], _ident, N_DEV,
            pl.BlockSpec((None, None, d, fc), lambda i, j: (j, 0, 0, 0)), W("ffn_in_w", layer)[0], _ident,
            s["x_mid"], gain(layer, 2) + tok, dx_out, s["mix"], gain(layer, 1), "ffn_in_bwd" + tag)
        dgain[(layer, 2)], dgain[(layer, 1)] = dg2, dg1
        full_rows = pl.BlockSpec((N_DEV, d // N_DEV, d), lambda j, i: (0, 0, 0))
        rows_w = lambda wname, idx: (pl.BlockSpec((N_DEV, None, d // N_DEV, d), lambda i, j: (0, 0, 0, 0)), W(wname, idx)[0],
                                     lambda w_ref: w_ref[...].reshape(d, d))
        if layer < n_a:
            d_z = _matmul_nt_rows(d_mix, *W("conv_out_w", layer), BF16, "conv_out_bwd" + tag)
            grads["conv_out_w"][layer] = _grad_weight(
                [pl.BlockSpec((tg, d), lambda j, i: (i, 0))], [s["z"]], _ident,
                [pl.BlockSpec((tg, d), lambda j, i: (i, 0))], [d_mix], _ident,
                1, (d, d), full_rows, _sds((N_DEV, d // N_DEV, d), BF16), t, "grad_conv_out" + tag)
            d_p, dtaps[layer] = _conv_bwd(s["p"], d_z, taps(layer), "conv_bwd" + tag)
            grads["conv_in_w"][layer] = _grad_weight(
                [pl.BlockSpec((d, tg), lambda j, i: (0, i))], [s["xn_t"]], _ident,
                [pl.BlockSpec((tg, 2 * cb), lambda j, i: (i, j))], [d_p], _ident,
                N_DEV // 2, (d, 2 * cb), pl.BlockSpec((2, d, cb), lambda j, i: (j, 0, 0)), _sds((N_DEV, d, cb), BF16), t,
                "grad_conv_in" + tag, a_transposed=True)
            a_specs, a_args, a_tile, n_steps = [pl.BlockSpec((tm, 4 * cb), lambda i, j: (i, j))], [d_p], _ident, N_DEV // 4
            w_spec = pl.BlockSpec((4, None, d, cb), lambda i, j: (j, 0, 0, 0))
            w_arg = W("conv_in_w", layer)[0]
            w_mat = lambda w_ref: jnp.concatenate([w_ref[k] for k in range(4)], axis=1)
            resid = dx_mid
        else:
            j_b = layer - n_a
            d_o = _matmul_nt_rows(d_mix, *W("o_w", j_b), F32, "o_proj_bwd" + tag)
            grads["o_w"][j_b] = _grad_weight(
                [pl.BlockSpec((tg, d), lambda j, i: (i, 0))], [s["o"]], _ident,
                [pl.BlockSpec((tg, d), lambda j, i: (i, 0))], [d_mix], _ident,
                1, (d, d), full_rows, _sds((N_DEV, d // N_DEV, d), BF16), t, "grad_o" + tag)
            dq, dk, dv = _attention_bwd(s["q"], kv, s["o"], s["lse"], d_o, slopes, q_scale, "attention_bwd" + tag)
            dkv_parts.append((dk, dv))
            heads_spec = pl.BlockSpec((hp, tg, LANES), lambda j, i: (0, i, 0))
            grads["q_w"][j_b] = _grad_weight(
                [pl.BlockSpec((tg, d), lambda j, i: (i, 0))], [s["xn"]], _ident,
                [heads_spec], [dq], _heads_tile,
                1, (d, d), full_rows, _sds((N_DEV, d // N_DEV, d), BF16), t, "grad_q" + tag)
            a_specs, a_args, a_tile, n_steps = [pl.BlockSpec((hp, tm, LANES), lambda i, j: (0, i, 0))], [dq], _heads_tile, 1
            w_spec, w_arg, w_mat = rows_w("q_w", j_b)
            resid = dx_mid
            if layer == n_a:
                pieces = kvb // LANES
                halves = []
                for src in (0, 1):
                    halves.append([part[src] for part in dkv_parts])
                n_half = len(dkv_parts)
                kv_args = [arr for src in (0, 1) for arr in halves[src]]

                def kv_block(src, j):
                    return jnp.where((j // 4) == src, j % 4, 0)

                def kv_tile(j, *refs):
                    keys = _heads_to_rows(*refs[:n_half])
                    vals = _heads_to_rows(*refs[n_half:])
                    return jnp.where(j < 4, keys, vals)

                kv_specs = [pl.BlockSpec((pieces, tm, LANES), functools.partial(lambda i, j, src: (kv_block(src, j), i, 0), src=src))
                            for src in (0, 1) for _ in range(n_half)]
                resid, dgain["kv"] = _bwd_matmul_norms(
                    kv_specs, kv_args, kv_tile, N_DEV,
                    pl.BlockSpec((None, None, d, kvb), lambda i, j: (j, 0, 0, 0)), W("kv_w", 0)[0], _ident,
                    s["x_in"], g_kv, dx_mid, None, None, "kv_proj_bwd")
                kv_b_specs = [pl.BlockSpec((pieces, tg, LANES), functools.partial(lambda j, i, src: (kv_block(src, j), i, 0), src=src))
                              for src in (0, 1) for _ in range(n_half)]
                grads["kv_w"][0] = _grad_weight(
                    [pl.BlockSpec((d, tg), lambda j, i: (0, i))], [kvn_t], _ident,
                    kv_b_specs, kv_args, kv_tile,
                    N_DEV, (d, kvb), pl.BlockSpec((None, d, kvb), lambda j, i: (j, 0, 0)), _sds((N_DEV, d, kvb), BF16), t,
                    "grad_kv", a_transposed=True)
        tok = scatter_start([key for key in group(layer) if not key[0].startswith("ffn")], "_mix" + tag)
        if layer > 0:
            prev = saved[layer - 1]
            dx_out, d_ff, dg0, dg3p = _bwd_matmul_norms(
                a_specs, a_args, a_tile, n_steps, w_spec, w_arg, w_mat,
                s["x_in"], gain(layer, 0) + tok, resid, prev["ff"], gain(layer - 1, 3), "mixer_in_bwd" + tag)
            dgain[(layer, 0)], dgain[(layer - 1, 3)] = dg0, dg3p
        else:
            grad_x, dg0 = _bwd_matmul_norms(
                a_specs, a_args, a_tile, n_steps, w_spec, w_arg, w_mat,
                s["x_in"], gain(layer, 0), resid, None, None, "mixer_in_bwd" + tag)
            dgain[(layer, 0)] = dg0

    small_grad = jnp.concatenate(
        [dgain[(layer, k)] for layer in range(depth) for k in range(4)] + [dtaps[layer] for layer in range(n_a)]
        + [dgain["kv"]] + [jnp.zeros((small_rows - n_small - 1, d), F32)], axis=0)
    small_grads_all = _all_gather([small_grad], "gather_small_grads")[0]
    lo = dev * (d // N_DEV)

    def pack(ng, cwp, kvg):
        rows = jnp.concatenate([ng.reshape(4 * depth, -1), cwp.reshape(3 * n_a, -1)], axis=0)
        z = lax.dynamic_update_slice(jnp.zeros((small_rows, d), F32), rows, (0, lo))
        return lax.dynamic_update_slice(z, kvg[None], (n_small, 0))

    w_small = lax.dynamic_update_slice(small_all, g_kv, (n_small, 0))
    m_small, v_small = pack(m_norm_g, m_conv_w, m_kv_norm_g), pack(v_norm_g, v_conv_w, v_kv_norm_g)
    sm = _small_adamw(small_grads_all, w_small, m_small, v_small, "adamw_small")

    def unpack(a):
        mine = lax.dynamic_slice(a, (0, lo), (small_rows, d // N_DEV))
        return (mine[:4 * depth].reshape(norm_g.shape), mine[4 * depth:n_small].reshape(conv_w.shape), a[n_small])

    small_out = [unpack(a) for a in sm]

    moments = {"conv_in_w": (m_conv_in_w, v_conv_in_w), "conv_out_w": (m_conv_out_w, v_conv_out_w),
               "kv_w": (m_kv_w[None], v_kv_w[None]), "q_w": (m_q_w, v_q_w), "o_w": (m_o_w, v_o_w),
               "ffn_in_w": (m_ffn_in_w, v_ffn_in_w), "ffn_out_w": (m_ffn_out_w, v_ffn_out_w)}
    landed = {k: [None] * big[k].shape[0] for k in names}
    for keys, tag, send_sems, recv_sems, parts, zones in scattering:
        parts, zones = _send_wait(send_sems, recv_sems, parts, zones, grad_x, "scatter_wait" + tag)
        for (k, i), part, zone in zip(keys, parts, zones):
            landed[k][i] = (part, zone)
    res = {k: _sum_adamw(slot, [p for p, _ in landed[k]], [z for _, z in landed[k]], big[k], moments[k][0], moments[k][1],
                         "adamw_" + k) for k in names}

    def big_out(k, which):
        return res[k][which][0] if k == "kv_w" else res[k][which]

    out_names = ["norm_g", "conv_in_w", "conv_w", "conv_out_w", "kv_norm_g", "kv_w", "q_w", "o_w", "ffn_in_w", "ffn_out_w"]
    small_pos = {"norm_g": 0, "conv_w": 1, "kv_norm_g": 2}
    outs = [loss, grad_x[None]]
    for which in range(4):
        for k in out_names:
            outs.append(small_out[which][small_pos[k]] if k in small_pos else big_out(k, which))
    return tuple(outs)
```

```python
import functools
import math

import numpy as np
import jax
import jax.numpy as jnp
from jax import lax
from jax.experimental import pallas as pl
from jax.experimental.pallas import tpu as pltpu

F32 = jnp.float32
BF16 = jnp.bfloat16

N_DEV = 8
RMS_EPS = 1e-6
HEAD_DIM = 64
LANES = 128
ATT_BLOCK = 128
DILATIONS = (1, 4, 16)
SUPER = ATT_BLOCK * DILATIONS[-1]
NEG = -1e30
ATT_UNROLL_FWD = 8
ATT_UNROLL_BWD = 8

ADAM_LR, ADAM_B1, ADAM_B2, ADAM_EPS, ADAM_WD, ADAM_STEP = 0.001, 0.9, 0.999, 1e-08, 0.01, 10

ROW_TILE = 512
BIG_ROW_TILE = 1024
SWIGLU_ROWS = 256
GRAD_ROW_TILE = 2048
BWD_ROW_TILE = 512
MESH = pl.DeviceIdType.MESH


def _call(body, *, name, grid=None, in_specs=None, out_specs=None, out_shape=None, scratch_shapes=(), prefetch=False,
          **params):
    cp = pltpu.CompilerParams(**params) if params else None
    if prefetch:
        spec = pltpu.PrefetchScalarGridSpec(num_scalar_prefetch=1, grid=grid, in_specs=in_specs, out_specs=out_specs,
                                            scratch_shapes=list(scratch_shapes))
        return pl.pallas_call(body, name=name, grid_spec=spec, out_shape=out_shape, compiler_params=cp)
    kwargs = {k: v for k, v in (("grid", grid), ("in_specs", in_specs), ("out_specs", out_specs)) if v is not None}
    return pl.pallas_call(body, name=name, out_shape=out_shape, scratch_shapes=list(scratch_shapes),
                          compiler_params=cp, **kwargs)


def _sds(shape, dtype):
    return jax.ShapeDtypeStruct(tuple(shape), dtype)


def _rms(x, g):
    r = lax.rsqrt(jnp.mean(x * x, axis=-1, keepdims=True) + RMS_EPS)
    return x * r * g


def _rms_bwd(x, g, dy):
    r = lax.rsqrt(jnp.mean(x * x, axis=-1, keepdims=True) + RMS_EPS)
    xh = x * r
    dxh = dy * g
    dx = r * (dxh - xh * jnp.mean(dxh * xh, axis=-1, keepdims=True))
    return dx, jnp.sum(dy * xh, axis=0, keepdims=True)


def _dot(a, b):
    return jnp.dot(a, b, preferred_element_type=F32)


def _dot_nt(a, b):
    return lax.dot_general(a, b, (((1,), (1,)), ((), ())), preferred_element_type=F32)


def _dot_tn(a, b):
    return lax.dot_general(a, b, (((0,), (0,)), ((), ())), preferred_element_type=F32)


def _mesh_pos():
    return lax.axis_index("x"), lax.axis_index("y"), lax.axis_index("c")


def _all_gather(arrs, name):
    n = len(arrs)

    def body(*refs):
        ins, outs = refs[:n], refs[n:2 * n]
        send_sems, recv_sems, local_sems = refs[2 * n:]
        x, y, c = _mesh_pos()
        me, sibling = (x, y, c), (x, y, 1 - c)
        chips = [(1 - x, y), (x, 1 - y), (1 - x, 1 - y)]

        def copy(a, k, block, to, src=None):
            dst = outs[a].at[4 * block[0] + 2 * block[1] + block[2]]
            return pltpu.make_async_remote_copy(
                src_ref=dst if src is None else src, dst_ref=dst, send_sem=send_sems.at[a, k],
                recv_sem=recv_sems.at[a, k], device_id=to, device_id_type=MESH)

        started = []
        for a in range(n):
            mine = pltpu.make_async_copy(ins[a], outs[a].at[4 * x + 2 * y + c], local_sems.at[a])
            mine.start()
            started.append(mine)
        first = []
        for a in range(n):
            first.append(copy(a, 0, me, sibling, src=ins[a]))
            first += [copy(a, 1 + j, me, (*chip, c), src=ins[a]) for j, chip in enumerate(chips)]
        for cp in first:
            cp.start()
        passed = []
        for a in range(n):
            for j, chip in enumerate(chips):
                copy(a, 1 + j, (*chip, c), me).wait_recv()
                fwd = copy(a, 4 + j, (*chip, c), sibling)
                fwd.start()
                passed.append(fwd)
        for a in range(n):
            copy(a, 0, sibling, me).wait_recv()
            for j, chip in enumerate(chips):
                copy(a, 4 + j, (*chip, 1 - c), me).wait_recv()
        for cp in first + passed:
            cp.wait_send()
        for cp in started:
            cp.wait()

    any_spec = pl.BlockSpec(memory_space=pl.ANY)
    outs = _call(
        body, name=name, in_specs=[any_spec] * n, out_specs=[any_spec] * n,
        out_shape=[_sds((N_DEV,) + a.shape, a.dtype) for a in arrs],
        scratch_shapes=[pltpu.SemaphoreType.DMA((n, 7)), pltpu.SemaphoreType.DMA((n, 7)), pltpu.SemaphoreType.DMA((n,))],
        has_side_effects=True,
    )(*arrs)
    return list(outs)


HBM_SPEC = pl.BlockSpec(memory_space=pltpu.HBM)
SEM_SPEC = pl.BlockSpec(memory_space=pltpu.SEMAPHORE)
DATAFLOW = pltpu.SideEffectType.DATAFLOW_SIDE_EFFECTING
PEERS = [(dx, dy, dc) for dx in (0, 1) for dy in (0, 1) for dc in (0, 1)][1:]


def _peer(flip):
    x, y, c = _mesh_pos()
    return tuple(1 - v if f else v for v, f in zip((x, y, c), flip))


def _slot(pos):
    return 4 * pos[0] + 2 * pos[1] + pos[2]


def _in_hbm(a):
    return pltpu.with_memory_space_constraint(a, pltpu.HBM)


def _direct_copies(srcs, lands, send_sems, recv_sems, scatter):
    me = _slot(_mesh_pos())
    copies = []
    for a in range(len(lands)):
        for k, flip in enumerate(PEERS):
            peer = _peer(flip)
            src = srcs[a].at[_slot(peer)] if scatter else lands[a].at[me]
            idx = a * len(PEERS) + k
            copies.append(pltpu.make_async_remote_copy(
                src_ref=src, dst_ref=lands[a].at[me], send_sem=send_sems.at[idx], recv_sem=recv_sems.at[idx],
                device_id=peer, device_id_type=MESH))
    return copies


def _send_start(srcs, lands, after, name):
    ns, nl = len(srcs), len(lands)
    scatter = ns > 0

    def body(*refs):
        src_refs, land_refs = refs[:ns], refs[ns:ns + nl]
        send_sems, recv_sems = refs[ns + nl + 1:ns + nl + 3]
        token = refs[-1]
        for cp in _direct_copies(src_refs, land_refs, send_sems, recv_sems, scatter):
            cp.start()
        token[...] = jnp.zeros_like(token)

    sem = pltpu.SemaphoreType.DMA((nl * len(PEERS),))
    outs = pl.pallas_call(
        body, name=name,
        out_shape=(sem, sem) + tuple(pltpu.HBM(a.shape, a.dtype) for a in list(srcs) + list(lands))
        + (_sds((8, LANES), F32),),
        in_specs=[HBM_SPEC] * (ns + nl) + [pl.BlockSpec(memory_space=pl.ANY)],
        out_specs=(SEM_SPEC, SEM_SPEC) + (HBM_SPEC,) * (ns + nl) + (pl.BlockSpec(memory_space=pltpu.VMEM),),
        input_output_aliases={i: 2 + i for i in range(ns + nl)},
        compiler_params=pltpu.CompilerParams(has_side_effects=DATAFLOW),
    )(*[_in_hbm(a) for a in list(srcs) + list(lands)], after)
    send_sems, recv_sems = outs[0], outs[1]
    return send_sems, recv_sems, list(outs[2:2 + ns]), list(outs[2 + ns:2 + ns + nl]), outs[-1]


def _send_wait(send_sems, recv_sems, srcs, lands, after, name):
    ns, nl = len(srcs), len(lands)
    scatter = ns > 0

    def body(*refs):
        src_refs, land_refs = refs[:ns], refs[ns:ns + nl]
        send_sems, recv_sems = refs[ns + nl:ns + nl + 2]
        copies = _direct_copies(src_refs, land_refs, send_sems, recv_sems, scatter)
        for cp in copies:
            cp.wait_send()
        for cp in copies:
            cp.wait_recv()

    outs = pl.pallas_call(
        body, name=name,
        out_shape=tuple(pltpu.HBM(a.shape, a.dtype) for a in list(srcs) + list(lands)),
        in_specs=[HBM_SPEC] * (ns + nl) + [SEM_SPEC, SEM_SPEC, pl.BlockSpec(memory_space=pl.ANY)],
        out_specs=(HBM_SPEC,) * (ns + nl),
        input_output_aliases={i: i for i in range(ns + nl)},
        compiler_params=pltpu.CompilerParams(has_side_effects=DATAFLOW),
    )(*srcs, *lands, send_sems, recv_sems, after)
    return list(outs[:ns]), list(outs[ns:])


def _row_tile(rows, cap=512):
    t = min(rows, cap)
    while rows % t or (t % 16 and t != rows):
        t -= 1
    return t


def _as2d(a):
    return a.reshape(-1, a.shape[-1])


def _cast_layer(w, layer, slot, name):
    _, rows, cols = w.shape
    tr = _row_tile(rows)

    def body(*refs):
        refs[-1][...] = refs[-2][...].astype(BF16)

    if slot is None:
        return _call(body, name=name, grid=(rows // tr,),
                     in_specs=[pl.BlockSpec((None, tr, cols), lambda i: (layer, i, 0))],
                     out_specs=pl.BlockSpec((tr, cols), lambda i: (i, 0)), out_shape=_sds((rows, cols), BF16))(w)
    return _call(body, name=name, grid=(rows // tr,), prefetch=True,
                 in_specs=[pl.BlockSpec((None, tr, cols), lambda i, s: (layer, i, 0))],
                 out_specs=pl.BlockSpec((None, tr, cols), lambda i, s: (s[0], i, 0)),
                 out_shape=_sds((N_DEV, rows, cols), BF16))(slot, w)


def _sum_adamw(slot, parts, lands, w, m, v, name):
    n_l = len(parts)
    _, rows, cols = lands[0].shape
    tr = _row_tile(rows, 128)

    def body(s_ref, *refs):
        p_refs, l_refs = refs[:n_l], refs[n_l:2 * n_l]
        w_ref, m_ref, v_ref, g_ref, d_ref, nm_ref, nv_ref = refs[2 * n_l:]
        for k in range(n_l):
            @pl.when(pl.program_id(0) == k)
            def _(k=k):
                own = p_refs[k][...]
                g = jnp.zeros((tr, cols), F32)
                for j in range(N_DEV):
                    g = g + jnp.where(s_ref[0] == j, own, l_refs[k][j]).astype(F32)
                delta, nm, nv = _adamw_math(w_ref[...], g, m_ref[...], v_ref[...])
                g_ref[...] = g
                d_ref[...] = delta
                nm_ref[...] = nm
                nv_ref[...] = nv

    def own_block(k):
        return pl.BlockSpec((None, tr, cols), lambda l, i, s: (s[0], jnp.where(l == k, i, 0), 0))

    def zone_block(k):
        return pl.BlockSpec((N_DEV, tr, cols), lambda l, i, s: (0, jnp.where(l == k, i, 0), 0))

    lay = pl.BlockSpec((None, tr, cols), lambda l, i, s: (l, i, 0))
    return _call(body, name=name, grid=(n_l, rows // tr), prefetch=True,
                 in_specs=[own_block(k) for k in range(n_l)] + [zone_block(k) for k in range(n_l)] + [lay, lay, lay],
                 out_specs=[lay] * 4, out_shape=[_sds((n_l, rows, cols), F32)] * 4)(slot, *parts, *lands, w, m, v)


def _adamw_math(w, g, m, v):
    m = ADAM_B1 * m + (1.0 - ADAM_B1) * g
    v = ADAM_B2 * v + (1.0 - ADAM_B2) * (g * g)
    m_hat = m / (1.0 - ADAM_B1 ** ADAM_STEP)
    v_hat = v / (1.0 - ADAM_B2 ** ADAM_STEP)
    delta = -ADAM_LR * (m_hat / (jnp.sqrt(v_hat) + ADAM_EPS) + ADAM_WD * w)
    return delta, m, v


def _small_adamw(gathered, w, m, v, name):
    def body(a_ref, w_ref, m_ref, v_ref, g_ref, d_ref, nm_ref, nv_ref):
        g = a_ref[0]
        for k in range(1, N_DEV):
            g = g + a_ref[k]
        delta, nm, nv = _adamw_math(w_ref[...], g, m_ref[...], v_ref[...])
        g_ref[...] = g
        d_ref[...] = delta
        nm_ref[...] = nm
        nv_ref[...] = nv

    return _call(body, name=name, out_shape=[_sds(w.shape, F32)] * 4)(gathered, w, m, v)


def _norm_matmul_cols(x, g, wg, layer, mode, name):
    t, d = x.shape
    nb = wg.shape[-1]
    tm = BIG_ROW_TILE
    pieces = nb // LANES
    per = 2 if mode == "cols" else 1

    def body(x_ref, g_ref, w_ref, y_ref, xnt_ref, xn_ref):
        @pl.when(pl.program_id(1) == 0)
        def _():
            xn = _rms(x_ref[...], g_ref[...])
            xn_ref[...] = xn.astype(BF16)
            xnt_ref[...] = xn.T.astype(BF16)

        y = _dot(xn_ref[...], jnp.concatenate([w_ref[k] for k in range(per)], axis=1))
        if mode == "heads":
            for p in range(pieces):
                y_ref[p] = y[:, p * LANES:(p + 1) * LANES]
        else:
            y_ref[...] = y.astype(BF16)

    if mode == "cols":
        y_shape, y_spec = _sds((t, N_DEV * nb), BF16), pl.BlockSpec((tm, per * nb), lambda i, j: (i, j))
    else:
        y_shape = _sds((N_DEV * pieces, t, LANES), F32)
        y_spec = pl.BlockSpec((pieces, tm, LANES), lambda i, j: (j, i, 0))
    return _call(
        body, name=name, grid=(t // tm, N_DEV // per),
        in_specs=[pl.BlockSpec((tm, d), lambda i, j: (i, 0)), pl.BlockSpec((1, d), lambda i, j: (0, 0)),
                  pl.BlockSpec((per, None, d, nb), lambda i, j: (j, layer, 0, 0))],
        out_specs=[y_spec, pl.BlockSpec((d, tm), lambda i, j: (0, i))],
        out_shape=[y_shape, _sds((d, t), BF16)], scratch_shapes=[pltpu.VMEM((tm, d), BF16)])(x, g, wg)


def _ffn_in_swiglu(x, g, wg, layer, name):
    t, d = x.shape
    fc = wg.shape[-1]
    tm = BIG_ROW_TILE

    def body(x_ref, g_ref, wg_ref, wu_ref, gate_ref, up_ref, a_ref, xnt_ref, xn_ref):
        @pl.when(pl.program_id(1) == 0)
        def _():
            xn = _rms(x_ref[...], g_ref[...])
            xn_ref[...] = xn.astype(BF16)
            xnt_ref[...] = xn.T.astype(BF16)

        xn = xn_ref[...]
        gate, up = _dot(xn, wg_ref[...]), _dot(xn, wu_ref[...])
        gate_ref[...] = gate.astype(BF16)
        up_ref[...] = up.astype(BF16)
        a_ref[...] = (gate * jax.nn.sigmoid(gate) * up).astype(BF16)

    chunk = pl.BlockSpec((None, tm, fc), lambda i, c: (c, i, 0))
    return _call(
        body, name=name, grid=(t // tm, 4),
        in_specs=[pl.BlockSpec((tm, d), lambda i, c: (i, 0)), pl.BlockSpec((1, d), lambda i, c: (0, 0)),
                  pl.BlockSpec((None, None, d, fc), lambda i, c: (c, layer, 0, 0)),
                  pl.BlockSpec((None, None, d, fc), lambda i, c: (c + 4, layer, 0, 0))],
        out_specs=[chunk, chunk, chunk, pl.BlockSpec((d, tm), lambda i, c: (0, i))],
        out_shape=[_sds((4, t, fc), BF16)] * 3 + [_sds((d, t), BF16)],
        scratch_shapes=[pltpu.VMEM((tm, d), BF16)])(x, g, wg, wg)


def _norm_matmul_heads(x, g, wg, layer, scale, name):
    t, d = x.shape
    tm = ROW_TILE
    hp = d // LANES

    def body(x_ref, g_ref, w_ref, y_ref, xn_ref):
        xn = _rms(x_ref[...], g_ref[...]).astype(BF16)
        xn_ref[...] = xn
        y = _dot(xn, w_ref[...].reshape(d, d)) * scale
        for p in range(hp):
            y_ref[p] = y[:, p * LANES:(p + 1) * LANES]

    return _call(
        body, name=name, grid=(t // tm,),
        in_specs=[pl.BlockSpec((tm, d), lambda i: (i, 0)), pl.BlockSpec((1, d), lambda i: (0, 0)),
                  pl.BlockSpec((N_DEV, None, d // N_DEV, d), lambda i: (0, layer, 0, 0))],
        out_specs=[pl.BlockSpec((hp, tm, LANES), lambda i: (0, i, 0)), pl.BlockSpec((tm, d), lambda i: (i, 0))],
        out_shape=[_sds((hp, t, LANES), F32), _sds((t, d), BF16)])(x, g, wg)


def _shift_down(u, halo, k, tm):
    row = lax.broadcasted_iota(jnp.int32, u.shape, 0)
    out = pltpu.roll(u, k, 0)
    for j in range(k):
        out = jnp.where(row == j, halo[halo.shape[0] - k + j:halo.shape[0] - k + j + 1, :], out)
    return out


def _shift_up(u, halo, k, tm):
    row = lax.broadcasted_iota(jnp.int32, u.shape, 0)
    out = pltpu.roll(u, tm - k, 0)
    for j in range(k):
        out = jnp.where(row == tm - k + j, halo[j:j + 1, :], out)
    return out


HALO = 16


def _conv_fwd(p, cw, name):
    t, d3 = p.shape
    d = d3 // 3
    tm = ROW_TILE
    hb = tm // HALO

    def body(p_ref, prev_ref, cw_ref, z_ref):
        i = pl.program_id(0)
        b = p_ref[:, 0:d].astype(F32)
        u = p_ref[:, d:2 * d].astype(F32) * p_ref[:, 2 * d:3 * d].astype(F32)
        keep = (i > 0).astype(F32)
        hu = prev_ref[:, d:2 * d].astype(F32) * prev_ref[:, 2 * d:3 * d].astype(F32) * keep
        uc = cw_ref[2:3, :] * u + cw_ref[1:2, :] * _shift_down(u, hu, 1, tm) + cw_ref[0:1, :] * _shift_down(u, hu, 2, tm)
        z_ref[...] = (b * uc).astype(BF16)

    return _call(
        body, name=name, grid=(t // tm,),
        in_specs=[pl.BlockSpec((tm, d3), lambda i: (i, 0)),
                  pl.BlockSpec((HALO, d3), lambda i: (jnp.maximum(i * hb - 1, 0), 0)),
                  pl.BlockSpec((3, d), lambda i: (0, 0))],
        out_specs=pl.BlockSpec((tm, d), lambda i: (i, 0)), out_shape=_sds((t, d), BF16))(p, p, cw)


def _matmul_norm_residual(a3, wg, layer, g, x_res, name):
    kc_n, t, kc = a3.shape
    d = wg.shape[-1]
    per = N_DEV // kc_n
    rows = wg.shape[2]
    tm = ROW_TILE

    def body(a_ref, w_ref, g_ref, x_ref, raw_ref, xo_ref):
        raw = None
        for c in range(kc_n):
            term = _dot(a_ref[c], w_ref[c * per:(c + 1) * per].reshape(per * rows, d))
            raw = term if raw is None else raw + term
        raw_ref[...] = raw
        xo_ref[...] = x_ref[...] + _rms(raw, g_ref[...])

    row_spec = pl.BlockSpec((tm, d), lambda i: (i, 0))
    return _call(
        body, name=name, grid=(t // tm,),
        in_specs=[pl.BlockSpec((kc_n, tm, kc), lambda i: (0, i, 0)),
                  pl.BlockSpec((N_DEV, None, rows, d), lambda i: (0, layer, 0, 0)),
                  pl.BlockSpec((1, d), lambda i: (0, 0)), row_spec],
        out_specs=[row_spec, row_spec], out_shape=[_sds((t, d), F32)] * 2)(a3, wg, g, x_res)


def _alibi_slopes(n_heads):
    hh = np.arange(n_heads, dtype=np.float32) + 1.0
    s = np.power(2.0, -8.0 * hh / n_heads).astype(np.float32)
    return jnp.asarray(np.repeat(s.reshape(n_heads // 2, 2, 1), 2 * ATT_BLOCK, axis=2))


def _band_bias(sl_ref, dil):
    u = lax.broadcasted_iota(jnp.int32, (ATT_BLOCK, 2 * ATT_BLOCK), 0)
    kk = lax.broadcasted_iota(jnp.int32, (ATT_BLOCK, 2 * ATT_BLOCK), 1)
    delta = u + ATT_BLOCK - kk
    valid = (delta >= 0) & (delta <= ATT_BLOCK)
    dist = (delta * dil).astype(F32)
    rows = [jnp.where(valid, -sl_ref[hd:hd + 1, :] * dist, NEG) for hd in range(2)]
    return jnp.concatenate(rows, axis=0)


def _stack_heads(a):
    lane = lax.broadcasted_iota(jnp.int32, a.shape, 1)
    return jnp.concatenate([jnp.where(lane < HEAD_DIM, a, 0.0), jnp.where(lane >= HEAD_DIM, a, 0.0)], axis=0).astype(BF16)


def _unstack_heads(a2):
    top, bot = a2[:ATT_BLOCK], a2[ATT_BLOCK:]
    lane = lax.broadcasted_iota(jnp.int32, top.shape, 1)
    return jnp.where(lane < HEAD_DIM, top, bot)


def _rows_to_lanes(a0, a1):
    eye = lax.broadcasted_iota(jnp.int32, a0.shape, 0) == lax.broadcasted_iota(jnp.int32, a0.shape, 1)
    return jnp.concatenate([jnp.sum(jnp.where(eye, a, 0.0), axis=0, keepdims=True) for a in (a0, a1)], axis=1)


def _fill_bias_t(sl_ref, bias_ref):
    kk = lax.broadcasted_iota(jnp.int32, (2 * ATT_BLOCK, 2 * ATT_BLOCK), 0)
    lane = lax.broadcasted_iota(jnp.int32, (2 * ATT_BLOCK, 2 * ATT_BLOCK), 1)
    delta = lane % ATT_BLOCK + ATT_BLOCK - kk
    valid = (delta >= 0) & (delta <= ATT_BLOCK)
    slope = jnp.concatenate([sl_ref[0:1, :ATT_BLOCK], sl_ref[1:2, :ATT_BLOCK]], axis=1)
    for gi, dil in enumerate(DILATIONS):
        bias = jnp.where(valid, -slope * (delta * dil).astype(F32), NEG)
        bias_ref[2 * gi] = bias
        bias_ref[2 * gi + 1] = jnp.where(kk < ATT_BLOCK, NEG, bias)


def _fill_bias(sl_ref, bias_ref):
    kk = lax.broadcasted_iota(jnp.int32, (2 * ATT_BLOCK, 2 * ATT_BLOCK), 1)
    for gi, dil in enumerate(DILATIONS):
        bias = _band_bias(sl_ref, dil)
        bias_ref[2 * gi] = bias
        bias_ref[2 * gi + 1] = jnp.where(kk < ATT_BLOCK, NEG, bias)


def _attention_fwd(q, kv, slopes, name):
    hp, t, _ = q.shape
    ns = t // SUPER
    nd = len(DILATIONS)

    def body(sl_ref, q_ref, kc_ref, kp_ref, vc_ref, vp_ref, o_ref, lse_ref, kw_ref, vw_ref, og_ref, lg_ref, bias_ref):
        n = pl.program_id(1)
        kw_ref[0:SUPER, :] = kp_ref[...]
        kw_ref[SUPER:, :] = kc_ref[...]
        vw_ref[0:SUPER, :] = vp_ref[...]
        vw_ref[SUPER:, :] = vc_ref[...]

        @pl.when(n == 0)
        def _():
            _fill_bias(sl_ref, bias_ref)

        for gi, dil in enumerate(DILATIONS):

            def block(idx, carry, gi=gi, dil=dil):
                r, b = idx % dil, idx // dil
                qs = b * (ATT_BLOCK * dil) + r
                ks = SUPER + (b - 1) * (ATT_BLOCK * dil) + r
                first = jnp.logical_and(n == 0, b == 0).astype(jnp.int32)
                q2 = _stack_heads(q_ref[pl.ds(qs, ATT_BLOCK, stride=dil), :])
                kb = kw_ref[pl.ds(ks, 2 * ATT_BLOCK, stride=dil), :].astype(BF16)
                vb = vw_ref[pl.ds(ks, 2 * ATT_BLOCK, stride=dil), :].astype(BF16)
                s = _dot_nt(q2, kb) + bias_ref[2 * gi + first]
                m = jnp.max(s, axis=-1, keepdims=True)
                p = jnp.exp(s - m).astype(BF16)
                ol = _dot(p, jnp.concatenate([vb, jnp.ones_like(vb)], axis=1))
                l = ol[:, LANES:]
                o2 = ol[:, :LANES] / l
                lse2 = m + jnp.log(l)
                og_ref[gi, pl.ds(qs, ATT_BLOCK, stride=dil), :] = _unstack_heads(o2)
                lg_ref[gi, pl.ds(qs, ATT_BLOCK, stride=dil), :] = _unstack_heads(lse2)
                return carry

            lax.fori_loop(0, SUPER // ATT_BLOCK, block, 0, unroll=ATT_UNROLL_FWD)
        lg =[lg_ref[gi] for gi in range(nd)]
        top = functools.reduce(jnp.maximum, lg)
        ws = [jnp.exp(x - top) for x in lg]
        tot = functools.reduce(jnp.add, ws)
        lse_ref[...] = top + jnp.log(tot)
        acc = ws[0] * og_ref[0]
        for gi in range(1, nd):
            acc = acc + ws[gi] * og_ref[gi]
        o_ref[...] = (acc / tot).astype(BF16)

    cur = lambda off: pl.BlockSpec((None, SUPER, LANES), lambda h, n: (h + off, n, 0))
    prev = lambda off: pl.BlockSpec((None, SUPER, LANES), lambda h, n: (h + off, jnp.maximum(n - 1, 0), 0))
    return _call(
        body, name=name, grid=(hp, ns),
        in_specs=[pl.BlockSpec((None, 2, 2 * ATT_BLOCK), lambda h, n: (h, 0, 0)), cur(0), cur(0), prev(0), cur(hp), prev(hp)],
        out_specs=[pl.BlockSpec((SUPER, LANES), lambda h, n: (n, h)), cur(0)],
        out_shape=[_sds((t, hp * LANES), BF16), _sds((hp, t, LANES), F32)],
        scratch_shapes=[pltpu.VMEM((2 * SUPER, LANES), F32), pltpu.VMEM((2 * SUPER, LANES), F32),
                        pltpu.VMEM((nd, SUPER, LANES), F32), pltpu.VMEM((nd, SUPER, LANES), F32),
                        pltpu.VMEM((2 * nd, 2 * ATT_BLOCK, 2 * ATT_BLOCK), F32)],
    )(slopes, q, kv, kv, kv, kv)


def _attention_bwd(q, kv, o, lse, d_o, dk_in, dv_in, slopes, q_scale, name):
    hp, t, _ = q.shape
    ns = t // SUPER
    shared = dk_in is not None

    def body(sl_ref, q_ref, kc_ref, kp_ref, vc_ref, vp_ref, o_ref, lse_ref, do_ref, *rest):
        dki_ref, dvi_ref = rest[:2] if shared else (None, None)
        dq_ref, dk_ref, dv_ref, kw_ref, vw_ref, dkw_ref, dvw_ref, st_ref, bias_ref = rest[2 if shared else 0:]
        n = pl.program_id(1)

        @pl.when(n == 0)
        def _():
            dkw_ref[...] = jnp.zeros_like(dkw_ref)
            dvw_ref[...] = jnp.zeros_like(dvw_ref)

        @pl.when(n > 0)
        def _():
            dkw_ref[0:SUPER, :] = dkw_ref[SUPER:, :]
            dvw_ref[0:SUPER, :] = dvw_ref[SUPER:, :]
            dkw_ref[SUPER:, :] = jnp.zeros((SUPER, LANES), F32)
            dvw_ref[SUPER:, :] = jnp.zeros((SUPER, LANES), F32)

        @pl.when(n < ns)
        def _():
            kw_ref[0:SUPER, :] = kp_ref[...]
            kw_ref[SUPER:, :] = kc_ref[...]
            vw_ref[0:SUPER, :] = vp_ref[...]
            vw_ref[SUPER:, :] = vc_ref[...]
            prod = do_ref[...] * o_ref[...].astype(F32)
            lane = lax.broadcasted_iota(jnp.int32, prod.shape, 1)
            zero = jnp.zeros((SUPER, LANES), F32)
            st_ref[0] = zero + jnp.sum(jnp.where(lane < HEAD_DIM, prod, 0.0), axis=-1, keepdims=True)
            st_ref[1] = zero + jnp.sum(jnp.where(lane >= HEAD_DIM, prod, 0.0), axis=-1, keepdims=True)
            lse = lse_ref[...]
            swapped = pltpu.roll(lse, HEAD_DIM, 1)
            st_ref[2] = jnp.where(lane < HEAD_DIM, lse, swapped)
            st_ref[3] = jnp.where(lane >= HEAD_DIM, lse, swapped)
            dq_ref[...] = jnp.zeros_like(dq_ref)

            @pl.when(n == 0)
            def _():
                _fill_bias_t(sl_ref, bias_ref)

            for gi, dil in enumerate(DILATIONS):

                def block(idx, carry, gi=gi, dil=dil):
                    r, b = idx % dil, idx // dil
                    qs = b * (ATT_BLOCK * dil) + r
                    ks = SUPER + (b - 1) * (ATT_BLOCK * dil) + r
                    first = jnp.logical_and(n == 0, b == 0).astype(jnp.int32)
                    rows = pl.ds(qs, ATT_BLOCK, stride=dil)
                    keys = pl.ds(ks, 2 * ATT_BLOCK, stride=dil)
                    q2 = _stack_heads(q_ref[rows, :])
                    do2 = _stack_heads(do_ref[rows, :])
                    kb = kw_ref[keys, :].astype(BF16)
                    vb = vw_ref[keys, :].astype(BF16)
                    dd = _rows_to_lanes(st_ref[0, rows, :], st_ref[1, rows, :])
                    lse_b = _rows_to_lanes(st_ref[2, rows, :], st_ref[3, rows, :])
                    ps, dss = [], []
                    for half in range(2):
                        hk = slice(half * ATT_BLOCK, (half + 1) * ATT_BLOCK)
                        p = jnp.exp(_dot_nt(kb[hk], q2) + bias_ref[2 * gi + first, hk, :] - lse_b)
                        dss.append((p * (_dot_nt(vb[hk], do2) - dd)).astype(BF16))
                        ps.append(p.astype(BF16))
                    p, ds = jnp.concatenate(ps, axis=0), jnp.concatenate(dss, axis=0)
                    dvw_ref[keys, :] += _dot(p, do2)
                    dkw_ref[keys, :] += _dot(ds, q2)
                    dq_ref[rows, :] += _unstack_heads(_dot_tn(ds, kb)) * q_scale
                    return carry

                lax.fori_loop(0, SUPER // ATT_BLOCK, block, 0, unroll=ATT_UNROLL_BWD)

        dk_ref[...] = dkw_ref[0:SUPER, :] + dki_ref[...] if shared else dkw_ref[0:SUPER, :]
        dv_ref[...] = dvw_ref[0:SUPER, :] + dvi_ref[...] if shared else dvw_ref[0:SUPER, :]

    last = ns - 1
    cur = lambda off: pl.BlockSpec((None, SUPER, LANES), lambda h, n: (h + off, jnp.minimum(n, last), 0))
    prev = lambda off: pl.BlockSpec((None, SUPER, LANES), lambda h, n: (h + off, jnp.clip(n - 1, 0, last), 0))
    nat = pl.BlockSpec((SUPER, LANES), lambda h, n: (jnp.minimum(n, last), h))
    late = pl.BlockSpec((None, SUPER, LANES), lambda h, n: (h, jnp.maximum(n - 1, 0), 0))
    dq, dk, dv = _call(
        body, name=name, grid=(hp, ns + 1),
        in_specs=[pl.BlockSpec((None, 2, 2 * ATT_BLOCK), lambda h, n: (h, 0, 0)), cur(0), cur(0), prev(0), cur(hp), prev(hp),
                  nat, cur(0), nat] + ([late, late] if shared else []),
        out_specs=[cur(0), late, late],
        out_shape=[_sds((hp, t, LANES), F32)] * 3,
        scratch_shapes=[pltpu.VMEM((2 * SUPER, LANES), F32)] * 4 + [
            pltpu.VMEM((4, SUPER, LANES), F32), pltpu.VMEM((2 * len(DILATIONS), 2 * ATT_BLOCK, 2 * ATT_BLOCK), F32)],
    )(slopes, q, kv, kv, kv, kv, o, lse, d_o, *((dk_in, dv_in) if shared else ()))
    return dq, dk, dv


def _loss_head(y, target, raw, g, name):
    t, d = y.shape
    tm = ROW_TILE

    def body(y_ref, t_ref, raw_ref, g_ref, sq_ref, dy_ref, draw_ref, dg_ref):
        i = pl.program_id(0)
        err = y_ref[...] - t_ref[...]
        dy = err * (1.0 / d)
        dy_ref[...] = dy
        draw, dg = _rms_bwd(raw_ref[...], g_ref[...], dy)
        draw_ref[...] = draw.astype(BF16)
        sq = jnp.zeros((8, LANES), F32) + jnp.sum(err * err)

        @pl.when(i == 0)
        def _():
            sq_ref[...] = sq
            dg_ref[...] = dg

        @pl.when(i > 0)
        def _():
            sq_ref[...] += sq
            dg_ref[...] += dg

    row = pl.BlockSpec((tm, d), lambda i: (i, 0))
    vec = pl.BlockSpec((1, d), lambda i: (0, 0))
    return _call(
        body, name=name, grid=(t // tm,), in_specs=[row, row, row, vec],
        out_specs=[pl.BlockSpec((8, LANES), lambda i: (0, 0)), row, row, vec],
        out_shape=[_sds((8, LANES), F32), _sds((t, d), F32), _sds((t, d), BF16), _sds((1, d), F32)])(y, target, raw, g)


def _bwd_matmul_norms(a_specs, a_args, a_tile, n_steps, w_spec, w_arg, w_mat, xa, ga, resid, xb, gb, name):
    t, d = xa.shape
    tm = BWD_ROW_TILE
    na = len(a_specs)
    second = xb is not None
    per = 4 if n_steps % 4 == 0 else 1
    n_steps //= per

    def blocks_of(spec, k):
        return pl.BlockSpec(spec.block_shape, lambda i, j: spec.index_map(i, per * j + k))

    def body(*refs):
        a_refs, w_refs = refs[:per * na], refs[per * na:per * na + per]
        xa_ref, ga_ref, res_ref = refs[per * na + per:per * na + per + 3]
        rest = refs[per * na + per + 3:]
        if second:
            xb_ref, gb_ref, dx_ref, d2_ref, dga_ref, dgb_ref, acc_ref = rest
        else:
            dx_ref, dga_ref, acc_ref = rest
        i, j = pl.program_id(0), pl.program_id(1)
        part = None
        for k in range(per):
            term = _dot_nt(a_tile(per * j + k, *a_refs[k * na:(k + 1) * na]), w_mat(w_refs[k]))
            part = term if part is None else part + term

        @pl.when(j == 0)
        def _():
            acc_ref[...] = part

        @pl.when(j > 0)
        def _():
            acc_ref[...] += part

        @pl.when(j == n_steps - 1)
        def _():
            da, dga = _rms_bwd(xa_ref[...], ga_ref[...], acc_ref[...])
            dx = res_ref[...] + da
            dx_ref[...] = dx
            if second:
                d2, dgb = _rms_bwd(xb_ref[...], gb_ref[...], dx)
                d2_ref[...] = d2.astype(BF16)

            @pl.when(i == 0)
            def _():
                dga_ref[...] = dga
                if second:
                    dgb_ref[...] = dgb

            @pl.when(i > 0)
            def _():
                dga_ref[...] += dga
                if second:
                    dgb_ref[...] += dgb

    row = pl.BlockSpec((tm, d), lambda i, j: (i, 0))
    vec = pl.BlockSpec((1, d), lambda i, j: (0, 0))
    in_specs = [blocks_of(sp, k) for k in range(per) for sp in a_specs] + [blocks_of(w_spec, k) for k in range(per)]
    in_specs += [row, vec, row]
    args = list(a_args) * per + [w_arg] * per + [xa, ga, resid]
    if second:
        in_specs += [row, vec]
        args += [xb, gb]
        out_specs = [row, row, vec, vec]
        out_shape = [_sds((t, d), F32), _sds((t, d), BF16), _sds((1, d), F32), _sds((1, d), F32)]
    else:
        out_specs = [row, vec]
        out_shape = [_sds((t, d), F32), _sds((1, d), F32)]
    return _call(body, name=name, grid=(t // tm, n_steps), in_specs=in_specs, out_specs=out_specs,
                 out_shape=out_shape, scratch_shapes=[pltpu.VMEM((tm, d), F32)])(*args)


def _heads_to_rows(*refs):
    hp = refs[0].shape[0]
    cols = []
    for p in range(hp):
        v = refs[0][p]
        for r in refs[1:]:
            v = v + r[p]
        cols.append(v)
    return jnp.concatenate(cols, axis=-1).astype(BF16)


def _matmul_nt_rows(a, wg, layer, out_dtype, name):
    t, d = a.shape
    tm = ROW_TILE

    def body(a_ref, w_ref, o_ref):
        o_ref[...] = _dot_nt(a_ref[...], w_ref[...].reshape(d, d)).astype(out_dtype)

    row = pl.BlockSpec((tm, d), lambda i: (i, 0))
    return _call(body, name=name, grid=(t // tm,),
                 in_specs=[row, pl.BlockSpec((N_DEV, None, d // N_DEV, d), lambda i: (0, layer, 0, 0))],
                 out_specs=row, out_shape=_sds((t, d), out_dtype))(a, wg)


def _swiglu_bwd(d_ff, wg, layer, gate, up, name):
    t, d = d_ff.shape
    fc = gate.shape[-1]
    rows = wg.shape[2]
    tm = BIG_ROW_TILE

    def body(df_ref, w_ref, g_ref, u_ref, dh_ref):
        w = w_ref[...].reshape(2 * rows, d)
        for r0 in range(0, tm, SWIGLU_ROWS):
            rs = slice(r0, r0 + SWIGLU_ROWS)
            da = _dot_nt(df_ref[rs, :], w)
            gate, up = g_ref[rs, :].astype(F32), u_ref[rs, :].astype(F32)
            sig = jax.nn.sigmoid(gate)
            dh_ref[0, rs, :] = (da * up * (sig * (1.0 + gate * (1.0 - sig)))).astype(BF16)
            dh_ref[1, rs, :] = (da * (gate * sig)).astype(BF16)

    return _call(
        body, name=name, grid=(t // tm, 4),
        in_specs=[pl.BlockSpec((tm, d), lambda i, c: (i, 0)),
                  pl.BlockSpec((2, None, rows, d), lambda i, c: (c, layer, 0, 0)),
                  pl.BlockSpec((None, tm, fc), lambda i, c: (c, i, 0)),
                  pl.BlockSpec((None, tm, fc), lambda i, c: (c, i, 0))],
        out_specs=pl.BlockSpec((None, 2, tm, fc), lambda i, c: (c, 0, i, 0)),
        out_shape=_sds((4, 2, t, fc), BF16))(d_ff, wg, gate, up)


def _conv_bwd(p, d_z, cw, name):
    t, d3 = p.shape
    d = d3 // 3
    tm = ROW_TILE
    hb = tm // HALO
    nt = t // tm

    def body(p_ref, prev_ref, next_ref, dz_ref, dzn_ref, cw_ref, dp_ref, dcw_ref):
        i = pl.program_id(0)
        b = p_ref[:, 0:d].astype(F32)
        c = p_ref[:, d:2 * d].astype(F32)
        h = p_ref[:, 2 * d:3 * d].astype(F32)
        u = c * h
        hu = prev_ref[:, d:2 * d].astype(F32) * prev_ref[:, 2 * d:3 * d].astype(F32) * (i > 0).astype(F32)
        u1, u2 = _shift_down(u, hu, 1, tm), _shift_down(u, hu, 2, tm)
        uc = cw_ref[2:3, :] * u + cw_ref[1:2, :] * u1 + cw_ref[0:1, :] * u2
        dz = dz_ref[...].astype(F32)
        duc = dz * b
        dn = dzn_ref[...].astype(F32) * next_ref[:, 0:d].astype(F32) * (i < nt - 1).astype(F32)
        du = cw_ref[2:3, :] * duc + cw_ref[1:2, :] * _shift_up(duc, dn, 1, tm) + cw_ref[0:1, :] * _shift_up(duc, dn, 2, tm)
        dp_ref[:, 0:d] = (dz * uc).astype(BF16)
        dp_ref[:, d:2 * d] = (du * h).astype(BF16)
        dp_ref[:, 2 * d:3 * d] = (du * c).astype(BF16)
        dcw = jnp.concatenate([jnp.sum(duc * u2, axis=0, keepdims=True), jnp.sum(duc * u1, axis=0, keepdims=True),
                               jnp.sum(duc * u, axis=0, keepdims=True)], axis=0)

        @pl.when(i == 0)
        def _():
            dcw_ref[...] = dcw

        @pl.when(i > 0)
        def _():
            dcw_ref[...] += dcw

    last_halo = t // HALO - 1
    return _call(
        body, name=name, grid=(nt,),
        in_specs=[pl.BlockSpec((tm, d3), lambda i: (i, 0)),
                  pl.BlockSpec((HALO, d3), lambda i: (jnp.maximum(i * hb - 1, 0), 0)),
                  pl.BlockSpec((HALO, d3), lambda i: (jnp.minimum((i + 1) * hb, last_halo), 0)),
                  pl.BlockSpec((tm, d), lambda i: (i, 0)),
                  pl.BlockSpec((HALO, d), lambda i: (jnp.minimum((i + 1) * hb, last_halo), 0)),
                  pl.BlockSpec((3, d), lambda i: (0, 0))],
        out_specs=[pl.BlockSpec((tm, d3), lambda i: (i, 0)), pl.BlockSpec((3, d), lambda i: (0, 0))],
        out_shape=[_sds((t, d3), BF16), _sds((3, d), F32)])(p, p, p, d_z, d_z, cw)


def _grad_weight(a_specs, a_args, a_tile, b_specs, b_args, b_tile, n_out, acc_shape, out_spec, out_shape, t, name,
                 a_transposed=False):
    tt = GRAD_ROW_TILE
    na, nb = len(a_specs), len(b_specs)

    def body(*refs):
        a_refs, b_refs = refs[:na], refs[na:na + nb]
        o_ref, acc_ref = refs[na + nb:]
        s = pl.program_id(1)
        a, b = a_tile(pl.program_id(0), *a_refs), b_tile(pl.program_id(0), *b_refs)
        part = _dot(a, b) if a_transposed else _dot_tn(a, b)

        @pl.when(s == 0)
        def _():
            acc_ref[...] = part

        @pl.when(s > 0)
        def _():
            acc_ref[...] += part

        @pl.when(s == t // tt - 1)
        def _():
            acc = acc_ref[...].astype(BF16)
            if o_ref.shape[-1] == acc.shape[-1]:
                o_ref[...] = acc.reshape(o_ref.shape)
            else:
                for k in range(o_ref.shape[0]):
                    o_ref[k] = acc[:, k * o_ref.shape[-1]:(k + 1) * o_ref.shape[-1]]

    return _call(body, name=name, grid=(n_out, t // tt), in_specs=list(a_specs) + list(b_specs), out_specs=out_spec,
                 out_shape=out_shape, scratch_shapes=[pltpu.VMEM(acc_shape, F32)])(*a_args, *b_args)


def _ident(*args):
    return args[-1][...]


def _heads_tile(j, *refs):
    return _heads_to_rows(*refs)


def kernel(x, norm_g, conv_in_w, conv_w, conv_out_w, kv_norm_g, kv_w, q_w, o_w, ffn_in_w, ffn_out_w, loss_target, m_norm_g, m_conv_in_w, m_conv_w, m_conv_out_w, m_kv_norm_g, m_kv_w, m_q_w, m_o_w, m_ffn_in_w, m_ffn_out_w, v_norm_g, v_conv_in_w, v_conv_w, v_conv_out_w, v_kv_norm_g, v_kv_w, v_q_w, v_o_w, v_ffn_in_w, v_ffn_out_w):
    x0 = x[0]
    target = loss_target[0]
    t, d = x0.shape
    depth = norm_g.shape[0]
    n_a = conv_in_w.shape[0]
    n_b = q_w.shape[0]
    hp = d // LANES
    tm, tg = BWD_ROW_TILE, GRAD_ROW_TILE
    assert t % SUPER == 0 and d % LANES == 0 and depth == n_a + n_b
    dev = 4 * lax.axis_index("x") + 2 * lax.axis_index("y") + lax.axis_index("c")

    n_small = 4 * depth + 3 * n_a
    small_rows = -(-(n_small + 1) // 8) * 8
    small_local = jnp.concatenate([norm_g.reshape(4 * depth, -1), conv_w.reshape(3 * n_a, -1),
                                   jnp.zeros((small_rows - n_small, norm_g.shape[-1]), F32)], axis=0)
    big = {"conv_in_w": conv_in_w, "conv_out_w": conv_out_w, "kv_w": kv_w[None], "q_w": q_w, "o_w": o_w,
           "ffn_in_w": ffn_in_w, "ffn_out_w": ffn_out_w}
    names = list(big)

    def group(layer):
        if layer < n_a:
            return [("conv_in_w", layer), ("conv_out_w", layer), ("ffn_in_w", layer), ("ffn_out_w", layer)]
        j = layer - n_a
        return ([("kv_w", 0)] if j == 0 else []) + [("q_w", j), ("o_w", j), ("ffn_in_w", layer), ("ffn_out_w", layer)]

    slot = dev.astype(jnp.int32).reshape(1)
    is_ffn = lambda key: key[0].startswith("ffn")
    first_keys = [key for key in group(0) if not is_ffn(key)]
    first = _all_gather([small_local] + [_cast_layer(big[k], i, None, f"cast_{k}_{i}") for k, i in first_keys], "gather_weights")
    small_all = first[0].transpose(1, 0, 2).reshape(small_rows, d)
    wl = {key: a[:, None] for key, a in zip(first_keys, first[1:])}

    def gather_start(keys, after, tag):
        lands = [_cast_layer(big[k], i, slot, f"cast_{k}_{i}") for k, i in keys]
        send_sems, recv_sems, _, lands, tok = _send_start([], lands, after, "gather_start" + tag)
        return (keys, tag, send_sems, recv_sems, lands), tok[0, 0]

    def gather_wait(flight, after):
        keys, tag, send_sems, recv_sems, lands = flight
        _, lands = _send_wait(send_sems, recv_sems, [], lands, after, "gather_wait" + tag)
        wl.update({key: a[:, None] for key, a in zip(keys, lands)})

    in_flight, token = gather_start([key for key in group(0) if is_ffn(key)], small_all, "_l0")
    W = lambda k, i: (wl[(k, i)], 0)
    gain = lambda layer, k: small_all[4 * layer + k][None]
    taps = lambda layer: small_all[4 * depth + 3 * layer: 4 * depth + 3 * layer + 3]
    g_kv = kv_norm_g[None]
    slopes = _alibi_slopes(d // HEAD_DIM)
    fc = big["ffn_in_w"].shape[-1]
    cb = big["conv_in_w"].shape[-1]
    kvb = big["kv_w"].shape[-1]
    q_scale = HEAD_DIM ** -0.5

    saved = []
    kv = kvn_t = None
    xs = x0
    for layer in range(depth):
        tag = f"_l{layer}"
        g0 = g2 = 0.0
        if layer == 0:
            g0 = token
        else:
            gather_wait(in_flight, xs)
            if layer + 1 < depth:
                in_flight, g0 = gather_start(group(layer + 1), xs, f"_l{layer + 1}")
        s = {"x_in": xs}
        g0 = gain(layer, 0) + g0
        if layer < n_a:
            s["p"], s["xn_t"] = _norm_matmul_cols(xs, g0, *W("conv_in_w", layer), "cols", "conv_in" + tag)
            s["z"] = _conv_fwd(s["p"], taps(layer), "conv" + tag)
            s["mix"], x_mid = _matmul_norm_residual(s["z"][None], *W("conv_out_w", layer), gain(layer, 1), xs, "conv_out" + tag)
        else:
            j = layer - n_a
            if kv is None:
                kv, kvn_t = _norm_matmul_cols(xs, g_kv, *W("kv_w", 0), "heads", "kv_proj")
            s["q"], s["xn"] = _norm_matmul_heads(xs, g0, *W("q_w", j), q_scale, "q_proj" + tag)
            s["o"], s["lse"] = _attention_fwd(s["q"], kv, slopes, "attention" + tag)
            s["mix"], x_mid = _matmul_norm_residual(s["o"][None], *W("o_w", j), gain(layer, 1), xs, "o_proj" + tag)
        s["x_mid"] = x_mid
        if layer == 0:
            gather_wait(in_flight, x_mid)
            in_flight, g2 = gather_start(group(1), x_mid, "_l1")
        s["gate"], s["up"], s["a"], s["fn_t"] = _ffn_in_swiglu(x_mid, gain(layer, 2) + g2, *W("ffn_in_w", layer), "ffn_in" + tag)
        s["ff"], xs = _matmul_norm_residual(s["a"], *W("ffn_out_w", layer), gain(layer, 3), x_mid, "ffn_out" + tag)
        saved.append(s)

    last = saved[-1]
    sq, dx_out, d_ff, dg3 = _loss_head(xs, target, last["ff"], gain(depth - 1, 3), "loss_head")
    loss = lax.psum(sq[0, 0] * (0.5 / d), ("x", "y", "c"))

    dgain = {(depth - 1, 3): dg3}
    dtaps = {}
    grads = {k: [None] * big[k].shape[0] for k in names}
    dkv_parts = []
    scattering = []

    def scatter_start(keys, tag):
        parts = [grads[k][i] for k, i in keys]
        zones = [lax.empty(p.shape, p.dtype) for p in parts]
        send_sems, recv_sems, parts, zones, tok = _send_start(parts, zones, small_all, "scatter_start" + tag)
        scattering.append((keys, tag, send_sems, recv_sems, parts, zones))
        return tok[0, 0]

    for layer in reversed(range(depth)):
        tag = f"_l{layer}"
        s = saved[layer]
        dh = _swiglu_bwd(d_ff, *W("ffn_out_w", layer), s["gate"], s["up"], "swiglu_bwd" + tag)
        rows_out = big["ffn_out_w"].shape[1]
        grads["ffn_out_w"][layer] = _grad_weight(
            [pl.BlockSpec((None, tg, fc), lambda c, i: (c, i, 0))], [s["a"]], _ident,
            [pl.BlockSpec((tg, d), lambda c, i: (i, 0))], [d_ff], _ident,
            4, (fc, d), pl.BlockSpec((2, rows_out, d), lambda c, i: (c, 0, 0)), _sds((N_DEV, rows_out, d), BF16), t,
            "grad_ffn_out" + tag)
        grads["ffn_in_w"][layer] = _grad_weight(
            [pl.BlockSpec((d, tg), lambda j, i: (0, i))], [s["fn_t"]], _ident,
            [pl.BlockSpec((None, None, tg, fc), lambda j, i: (j % 4, j // 4, i, 0))], [dh], _ident,
            N_DEV, (d, fc), pl.BlockSpec((None, d, fc), lambda j, i: (j, 0, 0)), _sds((N_DEV, d, fc), BF16), t,
            "grad_ffn_in" + tag, a_transposed=True)
        tok = scatter_start([("ffn_in_w", layer), ("ffn_out_w", layer)], "_ffn" + tag)
        dx_mid, d_mix, dg2, dg1 = _bwd_matmul_norms(
            [pl.BlockSpec((None, None, tm, fc), lambda i, j: (j % 4, j // 4, i, 0))], [dh], _ident, N_DEV,
            pl.BlockSpec((None, None, d, fc), lambda i, j: (j, 0, 0, 0)), W("ffn_in_w", layer)[0], _ident,
            s["x_mid"], gain(layer, 2) + tok, dx_out, s["mix"], gain(layer, 1), "ffn_in_bwd" + tag)
        dgain[(layer, 2)], dgain[(layer, 1)] = dg2, dg1
        full_rows = pl.BlockSpec((N_DEV, d // N_DEV, d), lambda j, i: (0, 0, 0))
        rows_w = lambda wname, idx: (pl.BlockSpec((N_DEV, None, d // N_DEV, d), lambda i, j: (0, 0, 0, 0)), W(wname, idx)[0],
                                     lambda w_ref: w_ref[...].reshape(d, d))
        if layer < n_a:
            d_z = _matmul_nt_rows(d_mix, *W("conv_out_w", layer), BF16, "conv_out_bwd" + tag)
            grads["conv_out_w"][layer] = _grad_weight(
                [pl.BlockSpec((tg, d), lambda j, i: (i, 0))], [s["z"]], _ident,
                [pl.BlockSpec((tg, d), lambda j, i: (i, 0))], [d_mix], _ident,
                1, (d, d), full_rows, _sds((N_DEV, d // N_DEV, d), BF16), t, "grad_conv_out" + tag)
            d_p, dtaps[layer] = _conv_bwd(s["p"], d_z, taps(layer), "conv_bwd" + tag)
            grads["conv_in_w"][layer] = _grad_weight(
                [pl.BlockSpec((d, tg), lambda j, i: (0, i))], [s["xn_t"]], _ident,
                [pl.BlockSpec((tg, 2 * cb), lambda j, i: (i, j))], [d_p], _ident,
                N_DEV // 2, (d, 2 * cb), pl.BlockSpec((2, d, cb), lambda j, i: (j, 0, 0)), _sds((N_DEV, d, cb), BF16), t,
                "grad_conv_in" + tag, a_transposed=True)
            a_specs, a_args, a_tile, n_steps = [pl.BlockSpec((tm, 4 * cb), lambda i, j: (i, j))], [d_p], _ident, N_DEV // 4
            w_spec = pl.BlockSpec((4, None, d, cb), lambda i, j: (j, 0, 0, 0))
            w_arg = W("conv_in_w", layer)[0]
            w_mat = lambda w_ref: jnp.concatenate([w_ref[k] for k in range(4)], axis=1)
            resid = dx_mid
        else:
            j_b = layer - n_a
            d_o = _matmul_nt_rows(d_mix, *W("o_w", j_b), F32, "o_proj_bwd" + tag)
            grads["o_w"][j_b] = _grad_weight(
                [pl.BlockSpec((tg, d), lambda j, i: (i, 0))], [s["o"]], _ident,
                [pl.BlockSpec((tg, d), lambda j, i: (i, 0))], [d_mix], _ident,
                1, (d, d), full_rows, _sds((N_DEV, d // N_DEV, d), BF16), t, "grad_o" + tag)
            dk_in, dv_in = dkv_parts[0] if dkv_parts else (None, None)
            dq, dk, dv = _attention_bwd(s["q"], kv, s["o"], s["lse"], d_o, dk_in, dv_in, slopes, q_scale, "attention_bwd" + tag)
            dkv_parts = [(dk, dv)]
            heads_spec = pl.BlockSpec((hp, tg, LANES), lambda j, i: (0, i, 0))
            grads["q_w"][j_b] = _grad_weight(
                [pl.BlockSpec((tg, d), lambda j, i: (i, 0))], [s["xn"]], _ident,
                [heads_spec], [dq], _heads_tile,
                1, (d, d), full_rows, _sds((N_DEV, d // N_DEV, d), BF16), t, "grad_q" + tag)
            a_specs, a_args, a_tile, n_steps = [pl.BlockSpec((hp, tm, LANES), lambda i, j: (0, i, 0))], [dq], _heads_tile, 1
            w_spec, w_arg, w_mat = rows_w("q_w", j_b)
            resid = dx_mid
            if layer == n_a:
                pieces = kvb // LANES
                halves = []
                for src in (0, 1):
                    halves.append([part[src] for part in dkv_parts])
                n_half = len(dkv_parts)
                kv_args = [arr for src in (0, 1) for arr in halves[src]]

                def kv_block(src, j):
                    return jnp.where((j // 4) == src, j % 4, 0)

                def kv_tile(j, *refs):
                    keys = _heads_to_rows(*refs[:n_half])
                    vals = _heads_to_rows(*refs[n_half:])
                    return jnp.where(j < 4, keys, vals)

                kv_specs = [pl.BlockSpec((pieces, tm, LANES), functools.partial(lambda i, j, src: (kv_block(src, j), i, 0), src=src))
                            for src in (0, 1) for _ in range(n_half)]
                resid, dgain["kv"] = _bwd_matmul_norms(
                    kv_specs, kv_args, kv_tile, N_DEV,
                    pl.BlockSpec((None, None, d, kvb), lambda i, j: (j, 0, 0, 0)), W("kv_w", 0)[0], _ident,
                    s["x_in"], g_kv, dx_mid, None, None, "kv_proj_bwd")
                kv_b_specs = [pl.BlockSpec((pieces, tg, LANES), functools.partial(lambda j, i, src: (kv_block(src, j), i, 0), src=src))
                              for src in (0, 1) for _ in range(n_half)]
                grads["kv_w"][0] = _grad_weight(
                    [pl.BlockSpec((d, tg), lambda j, i: (0, i))], [kvn_t], _ident,
                    kv_b_specs, kv_args, kv_tile,
                    N_DEV, (d, kvb), pl.BlockSpec((None, d, kvb), lambda j, i: (j, 0, 0)), _sds((N_DEV, d, kvb), BF16), t,
                    "grad_kv", a_transposed=True)
        tok = scatter_start([key for key in group(layer) if not key[0].startswith("ffn")], "_mix" + tag)
        if layer > 0:
            prev = saved[layer - 1]
            dx_out, d_ff, dg0, dg3p = _bwd_matmul_norms(
                a_specs, a_args, a_tile, n_steps, w_spec, w_arg, w_mat,
                s["x_in"], gain(layer, 0) + tok, resid, prev["ff"], gain(layer - 1, 3), "mixer_in_bwd" + tag)
            dgain[(layer, 0)], dgain[(layer - 1, 3)] = dg0, dg3p
        else:
            grad_x, dg0 = _bwd_matmul_norms(
                a_specs, a_args, a_tile, n_steps, w_spec, w_arg, w_mat,
                s["x_in"], gain(layer, 0), resid, None, None, "mixer_in_bwd" + tag)
            dgain[(layer, 0)] = dg0

    small_grad = jnp.concatenate(
        [dgain[(layer, k)] for layer in range(depth) for k in range(4)] + [dtaps[layer] for layer in range(n_a)]
        + [dgain["kv"]] + [jnp.zeros((small_rows - n_small - 1, d), F32)], axis=0)
    small_grads_all = _all_gather([small_grad], "gather_small_grads")[0]
    lo = dev * (d // N_DEV)

    def pack(ng, cwp, kvg):
        rows = jnp.concatenate([ng.reshape(4 * depth, -1), cwp.reshape(3 * n_a, -1)], axis=0)
        z = lax.dynamic_update_slice(jnp.zeros((small_rows, d), F32), rows, (0, lo))
        return lax.dynamic_update_slice(z, kvg[None], (n_small, 0))

    w_small = lax.dynamic_update_slice(small_all, g_kv, (n_small, 0))
    m_small, v_small = pack(m_norm_g, m_conv_w, m_kv_norm_g), pack(v_norm_g, v_conv_w, v_kv_norm_g)
    sm = _small_adamw(small_grads_all, w_small, m_small, v_small, "adamw_small")

    def unpack(a):
        mine = lax.dynamic_slice(a, (0, lo), (small_rows, d // N_DEV))
        return (mine[:4 * depth].reshape(norm_g.shape), mine[4 * depth:n_small].reshape(conv_w.shape), a[n_small])

    small_out = [unpack(a) for a in sm]

    moments = {"conv_in_w": (m_conv_in_w, v_conv_in_w), "conv_out_w": (m_conv_out_w, v_conv_out_w),
               "kv_w": (m_kv_w[None], v_kv_w[None]), "q_w": (m_q_w, v_q_w), "o_w": (m_o_w, v_o_w),
               "ffn_in_w": (m_ffn_in_w, v_ffn_in_w), "ffn_out_w": (m_ffn_out_w, v_ffn_out_w)}
    landed = {k: [None] * big[k].shape[0] for k in names}
    for keys, tag, send_sems, recv_sems, parts, zones in scattering:
        parts, zones = _send_wait(send_sems, recv_sems, parts, zones, grad_x, "scatter_wait" + tag)
        for (k, i), part, zone in zip(keys, parts, zones):
            landed[k][i] = (part, zone)
    res = {k: _sum_adamw(slot, [p for p, _ in landed[k]], [z for _, z in landed[k]], big[k], moments[k][0], moments[k][1],
                         "adamw_" + k) for k in names}

    def big_out(k, which):
        return res[k][which][0] if k == "kv_w" else res[k][which]

    out_names = ["norm_g", "conv_in_w", "conv_w", "conv_out_w", "kv_norm_g", "kv_w", "q_w", "o_w", "ffn_in_w", "ffn_out_w"]
    small_pos = {"norm_g": 0, "conv_w": 1, "kv_norm_g": 2}
    outs = [loss, grad_x[None]]
    for which in range(4):
        for k in out_names:
            outs.append(small_out[which][small_pos[k]] if k in small_pos else big_out(k, which))
    return tuple(outs)
```

```python
import functools
import math

import numpy as np
import jax
import jax.numpy as jnp
from jax import lax
from jax.experimental import pallas as pl
from jax.experimental.pallas import tpu as pltpu

F32 = jnp.float32
BF16 = jnp.bfloat16

N_DEV = 8
RMS_EPS = 1e-6
HEAD_DIM = 64
LANES = 128
ATT_BLOCK = 128
DILATIONS = (1, 4, 16)
SUPER = ATT_BLOCK * DILATIONS[-1]
NEG = -1e30
ATT_UNROLL_FWD = 8
ATT_UNROLL_BWD = 8

ADAM_LR, ADAM_B1, ADAM_B2, ADAM_EPS, ADAM_WD, ADAM_STEP = 0.001, 0.9, 0.999, 1e-08, 0.01, 10

ROW_TILE = 512
BIG_ROW_TILE = 1024
SWIGLU_ROWS = 256
GRAD_ROW_TILE = 2048
BWD_ROW_TILE = 512
MESH = pl.DeviceIdType.MESH


def _call(body, *, name, grid=None, in_specs=None, out_specs=None, out_shape=None, scratch_shapes=(), prefetch=False,
          **params):
    cp = pltpu.CompilerParams(**params) if params else None
    if prefetch:
        spec = pltpu.PrefetchScalarGridSpec(num_scalar_prefetch=1, grid=grid, in_specs=in_specs, out_specs=out_specs,
                                            scratch_shapes=list(scratch_shapes))
        return pl.pallas_call(body, name=name, grid_spec=spec, out_shape=out_shape, compiler_params=cp)
    kwargs = {k: v for k, v in (("grid", grid), ("in_specs", in_specs), ("out_specs", out_specs)) if v is not None}
    return pl.pallas_call(body, name=name, out_shape=out_shape, scratch_shapes=list(scratch_shapes),
                          compiler_params=cp, **kwargs)


def _sds(shape, dtype):
    return jax.ShapeDtypeStruct(tuple(shape), dtype)


def _rms(x, g):
    r = lax.rsqrt(jnp.mean(x * x, axis=-1, keepdims=True) + RMS_EPS)
    return x * r * g


def _rms_bwd(x, g, dy):
    r = lax.rsqrt(jnp.mean(x * x, axis=-1, keepdims=True) + RMS_EPS)
    xh = x * r
    dxh = dy * g
    dx = r * (dxh - xh * jnp.mean(dxh * xh, axis=-1, keepdims=True))
    return dx, jnp.sum(dy * xh, axis=0, keepdims=True)


def _dot(a, b):
    return jnp.dot(a, b, preferred_element_type=F32)


def _dot_nt(a, b):
    return lax.dot_general(a, b, (((1,), (1,)), ((), ())), preferred_element_type=F32)


def _dot_tn(a, b):
    return lax.dot_general(a, b, (((0,), (0,)), ((), ())), preferred_element_type=F32)


def _mesh_pos():
    return lax.axis_index("x"), lax.axis_index("y"), lax.axis_index("c")


def _all_gather(arrs, name):
    n = len(arrs)

    def body(*refs):
        ins, outs = refs[:n], refs[n:2 * n]
        send_sems, recv_sems, local_sems = refs[2 * n:]
        x, y, c = _mesh_pos()
        me, sibling = (x, y, c), (x, y, 1 - c)
        chips = [(1 - x, y), (x, 1 - y), (1 - x, 1 - y)]

        def copy(a, k, block, to, src=None):
            dst = outs[a].at[4 * block[0] + 2 * block[1] + block[2]]
            return pltpu.make_async_remote_copy(
                src_ref=dst if src is None else src, dst_ref=dst, send_sem=send_sems.at[a, k],
                recv_sem=recv_sems.at[a, k], device_id=to, device_id_type=MESH)

        started = []
        for a in range(n):
            mine = pltpu.make_async_copy(ins[a], outs[a].at[4 * x + 2 * y + c], local_sems.at[a])
            mine.start()
            started.append(mine)
        first = []
        for a in range(n):
            first.append(copy(a, 0, me, sibling, src=ins[a]))
            first += [copy(a, 1 + j, me, (*chip, c), src=ins[a]) for j, chip in enumerate(chips)]
        for cp in first:
            cp.start()
        passed = []
        for a in range(n):
            for j, chip in enumerate(chips):
                copy(a, 1 + j, (*chip, c), me).wait_recv()
                fwd = copy(a, 4 + j, (*chip, c), sibling)
                fwd.start()
                passed.append(fwd)
        for a in range(n):
            copy(a, 0, sibling, me).wait_recv()
            for j, chip in enumerate(chips):
                copy(a, 4 + j, (*chip, 1 - c), me).wait_recv()
        for cp in first + passed:
            cp.wait_send()
        for cp in started:
            cp.wait()

    any_spec = pl.BlockSpec(memory_space=pl.ANY)
    outs = _call(
        body, name=name, in_specs=[any_spec] * n, out_specs=[any_spec] * n,
        out_shape=[_sds((N_DEV,) + a.shape, a.dtype) for a in arrs],
        scratch_shapes=[pltpu.SemaphoreType.DMA((n, 7)), pltpu.SemaphoreType.DMA((n, 7)), pltpu.SemaphoreType.DMA((n,))],
        has_side_effects=True,
    )(*arrs)
    return list(outs)


HBM_SPEC = pl.BlockSpec(memory_space=pltpu.HBM)
SEM_SPEC = pl.BlockSpec(memory_space=pltpu.SEMAPHORE)
DATAFLOW = pltpu.SideEffectType.DATAFLOW_SIDE_EFFECTING
PEERS = [(dx, dy, dc) for dx in (0, 1) for dy in (0, 1) for dc in (0, 1)][1:]


def _peer(flip):
    x, y, c = _mesh_pos()
    return tuple(1 - v if f else v for v, f in zip((x, y, c), flip))


def _slot(pos):
    return 4 * pos[0] + 2 * pos[1] + pos[2]


def _in_hbm(a):
    return pltpu.with_memory_space_constraint(a, pltpu.HBM)


def _direct_copies(srcs, lands, send_sems, recv_sems, scatter):
    me = _slot(_mesh_pos())
    copies = []
    for a in range(len(lands)):
        for k, flip in enumerate(PEERS):
            peer = _peer(flip)
            src = srcs[a].at[_slot(peer)] if scatter else lands[a].at[me]
            idx = a * len(PEERS) + k
            copies.append(pltpu.make_async_remote_copy(
                src_ref=src, dst_ref=lands[a].at[me], send_sem=send_sems.at[idx], recv_sem=recv_sems.at[idx],
                device_id=peer, device_id_type=MESH))
    return copies


def _send_start(srcs, lands, after, name):
    ns, nl = len(srcs), len(lands)
    scatter = ns > 0

    def body(*refs):
        src_refs, land_refs = refs[:ns], refs[ns:ns + nl]
        send_sems, recv_sems = refs[ns + nl + 1:ns + nl + 3]
        token = refs[-1]
        for cp in _direct_copies(src_refs, land_refs, send_sems, recv_sems, scatter):
            cp.start()
        token[...] = jnp.zeros_like(token)

    sem = pltpu.SemaphoreType.DMA((nl * len(PEERS),))
    outs = pl.pallas_call(
        body, name=name,
        out_shape=(sem, sem) + tuple(pltpu.HBM(a.shape, a.dtype) for a in list(srcs) + list(lands))
        + (_sds((8, LANES), F32),),
        in_specs=[HBM_SPEC] * (ns + nl) + [pl.BlockSpec(memory_space=pl.ANY)],
        out_specs=(SEM_SPEC, SEM_SPEC) + (HBM_SPEC,) * (ns + nl) + (pl.BlockSpec(memory_space=pltpu.VMEM),),
        input_output_aliases={i: 2 + i for i in range(ns + nl)},
        compiler_params=pltpu.CompilerParams(has_side_effects=DATAFLOW),
    )(*[_in_hbm(a) for a in list(srcs) + list(lands)], after)
    send_sems, recv_sems = outs[0], outs[1]
    return send_sems, recv_sems, list(outs[2:2 + ns]), list(outs[2 + ns:2 + ns + nl]), outs[-1]


def _send_wait(send_sems, recv_sems, srcs, lands, after, name):
    ns, nl = len(srcs), len(lands)
    scatter = ns > 0

    def body(*refs):
        src_refs, land_refs = refs[:ns], refs[ns:ns + nl]
        send_sems, recv_sems = refs[ns + nl:ns + nl + 2]
        copies = _direct_copies(src_refs, land_refs, send_sems, recv_sems, scatter)
        for cp in copies:
            cp.wait_send()
        for cp in copies:
            cp.wait_recv()

    outs = pl.pallas_call(
        body, name=name,
        out_shape=tuple(pltpu.HBM(a.shape, a.dtype) for a in list(srcs) + list(lands)),
        in_specs=[HBM_SPEC] * (ns + nl) + [SEM_SPEC, SEM_SPEC, pl.BlockSpec(memory_space=pl.ANY)],
        out_specs=(HBM_SPEC,) * (ns + nl),
        input_output_aliases={i: i for i in range(ns + nl)},
        compiler_params=pltpu.CompilerParams(has_side_effects=DATAFLOW),
    )(*srcs, *lands, send_sems, recv_sems, after)
    return list(outs[:ns]), list(outs[ns:])


def _row_tile(rows, cap=512):
    t = min(rows, cap)
    while rows % t or (t % 16 and t != rows):
        t -= 1
    return t


def _as2d(a):
    return a.reshape(-1, a.shape[-1])


def _cast_layer(w, layer, slot, name):
    _, rows, cols = w.shape
    tr = _row_tile(rows)

    def body(*refs):
        refs[-1][...] = refs[-2][...].astype(BF16)

    if slot is None:
        return _call(body, name=name, grid=(rows // tr,),
                     in_specs=[pl.BlockSpec((None, tr, cols), lambda i: (layer, i, 0))],
                     out_specs=pl.BlockSpec((tr, cols), lambda i: (i, 0)), out_shape=_sds((rows, cols), BF16))(w)
    return _call(body, name=name, grid=(rows // tr,), prefetch=True,
                 in_specs=[pl.BlockSpec((None, tr, cols), lambda i, s: (layer, i, 0))],
                 out_specs=pl.BlockSpec((None, tr, cols), lambda i, s: (s[0], i, 0)),
                 out_shape=_sds((N_DEV, rows, cols), BF16))(slot, w)


def _sum_adamw(slot, parts, lands, w, m, v, name):
    n_l = len(parts)
    _, rows, cols = lands[0].shape
    tr = _row_tile(rows, 128)

    def body(s_ref, *refs):
        p_refs, l_refs = refs[:n_l], refs[n_l:2 * n_l]
        w_ref, m_ref, v_ref, g_ref, d_ref, nm_ref, nv_ref = refs[2 * n_l:]
        for k in range(n_l):
            @pl.when(pl.program_id(0) == k)
            def _(k=k):
                own = p_refs[k][...]
                g = jnp.zeros((tr, cols), F32)
                for j in range(N_DEV):
                    g = g + jnp.where(s_ref[0] == j, own, l_refs[k][j]).astype(F32)
                delta, nm, nv = _adamw_math(w_ref[...], g, m_ref[...], v_ref[...])
                g_ref[...] = g
                d_ref[...] = delta
                nm_ref[...] = nm
                nv_ref[...] = nv

    def own_block(k):
        return pl.BlockSpec((None, tr, cols), lambda l, i, s: (s[0], jnp.where(l == k, i, 0), 0))

    def zone_block(k):
        return pl.BlockSpec((N_DEV, tr, cols), lambda l, i, s: (0, jnp.where(l == k, i, 0), 0))

    lay = pl.BlockSpec((None, tr, cols), lambda l, i, s: (l, i, 0))
    return _call(body, name=name, grid=(n_l, rows // tr), prefetch=True,
                 in_specs=[own_block(k) for k in range(n_l)] + [zone_block(k) for k in range(n_l)] + [lay, lay, lay],
                 out_specs=[lay] * 4, out_shape=[_sds((n_l, rows, cols), F32)] * 4)(slot, *parts, *lands, w, m, v)


def _adamw_math(w, g, m, v):
    m = ADAM_B1 * m + (1.0 - ADAM_B1) * g
    v = ADAM_B2 * v + (1.0 - ADAM_B2) * (g * g)
    m_hat = m / (1.0 - ADAM_B1 ** ADAM_STEP)
    v_hat = v / (1.0 - ADAM_B2 ** ADAM_STEP)
    delta = -ADAM_LR * (m_hat / (jnp.sqrt(v_hat) + ADAM_EPS) + ADAM_WD * w)
    return delta, m, v


def _small_adamw(gathered, w, m, v, name):
    def body(a_ref, w_ref, m_ref, v_ref, g_ref, d_ref, nm_ref, nv_ref):
        g = a_ref[0]
        for k in range(1, N_DEV):
            g = g + a_ref[k]
        delta, nm, nv = _adamw_math(w_ref[...], g, m_ref[...], v_ref[...])
        g_ref[...] = g
        d_ref[...] = delta
        nm_ref[...] = nm
        nv_ref[...] = nv

    return _call(body, name=name, out_shape=[_sds(w.shape, F32)] * 4)(gathered, w, m, v)


def _norm_matmul_cols(x, g, wg, layer, mode, name):
    t, d = x.shape
    nb = wg.shape[-1]
    tm = BIG_ROW_TILE
    pieces = nb // LANES
    per = 2 if mode == "cols" else 1

    def body(x_ref, g_ref, w_ref, y_ref, xnt_ref, xn_ref):
        @pl.when(pl.program_id(1) == 0)
        def _():
            xn = _rms(x_ref[...], g_ref[...])
            xn_ref[...] = xn.astype(BF16)
            xnt_ref[...] = xn.T.astype(BF16)

        y = _dot(xn_ref[...], jnp.concatenate([w_ref[k] for k in range(per)], axis=1))
        if mode == "heads":
            for p in range(pieces):
                y_ref[p] = y[:, p * LANES:(p + 1) * LANES]
        else:
            y_ref[...] = y.astype(BF16)

    if mode == "cols":
        y_shape, y_spec = _sds((t, N_DEV * nb), BF16), pl.BlockSpec((tm, per * nb), lambda i, j: (i, j))
    else:
        y_shape = _sds((N_DEV * pieces, t, LANES), F32)
        y_spec = pl.BlockSpec((pieces, tm, LANES), lambda i, j: (j, i, 0))
    return _call(
        body, name=name, grid=(t // tm, N_DEV // per),
        in_specs=[pl.BlockSpec((tm, d), lambda i, j: (i, 0)), pl.BlockSpec((1, d), lambda i, j: (0, 0)),
                  pl.BlockSpec((per, None, d, nb), lambda i, j: (j, layer, 0, 0))],
        out_specs=[y_spec, pl.BlockSpec((d, tm), lambda i, j: (0, i))],
        out_shape=[y_shape, _sds((d, t), BF16)], scratch_shapes=[pltpu.VMEM((tm, d), BF16)])(x, g, wg)


def _ffn_in_swiglu(x, g, wg, layer, name):
    t, d = x.shape
    fc = wg.shape[-2]
    tm = BIG_ROW_TILE

    def body(x_ref, g_ref, wg_ref, wu_ref, gate_ref, up_ref, a_ref, xn_ref):
        @pl.when(pl.program_id(1) == 0)
        def _():
            xn_ref[...] = _rms(x_ref[...], g_ref[...]).astype(BF16)

        xn = xn_ref[...]
        gate, up = _dot_nt(xn, wg_ref[...]), _dot_nt(xn, wu_ref[...])
        gate_ref[...] = gate.astype(BF16)
        up_ref[...] = up.astype(BF16)
        a_ref[...] = (gate * jax.nn.sigmoid(gate) * up).astype(BF16)

    chunk = pl.BlockSpec((None, tm, fc), lambda i, c: (c, i, 0))
    return _call(
        body, name=name, grid=(t // tm, 4),
        in_specs=[pl.BlockSpec((tm, d), lambda i, c: (i, 0)), pl.BlockSpec((1, d), lambda i, c: (0, 0)),
                  pl.BlockSpec((None, None, fc, d), lambda i, c: (c, layer, 0, 0)),
                  pl.BlockSpec((None, None, fc, d), lambda i, c: (c + 4, layer, 0, 0))],
        out_specs=[chunk, chunk, chunk, pl.BlockSpec((tm, d), lambda i, c: (i, 0))],
        out_shape=[_sds((4, t, fc), BF16)] * 3 + [_sds((t, d), BF16)])(x, g, wg, wg)


def _norm_matmul_heads(x, g, wg, layer, scale, name):
    t, d = x.shape
    tm = ROW_TILE
    hp = d // LANES

    def body(x_ref, g_ref, w_ref, y_ref, xn_ref):
        xn = _rms(x_ref[...], g_ref[...]).astype(BF16)
        xn_ref[...] = xn
        y = _dot(xn, w_ref[...].reshape(d, d)) * scale
        for p in range(hp):
            y_ref[p] = y[:, p * LANES:(p + 1) * LANES]

    return _call(
        body, name=name, grid=(t // tm,),
        in_specs=[pl.BlockSpec((tm, d), lambda i: (i, 0)), pl.BlockSpec((1, d), lambda i: (0, 0)),
                  pl.BlockSpec((N_DEV, None, d // N_DEV, d), lambda i: (0, layer, 0, 0))],
        out_specs=[pl.BlockSpec((hp, tm, LANES), lambda i: (0, i, 0)), pl.BlockSpec((tm, d), lambda i: (i, 0))],
        out_shape=[_sds((hp, t, LANES), F32), _sds((t, d), BF16)])(x, g, wg)


def _shift_down(u, halo, k, tm):
    row = lax.broadcasted_iota(jnp.int32, u.shape, 0)
    out = pltpu.roll(u, k, 0)
    for j in range(k):
        out = jnp.where(row == j, halo[halo.shape[0] - k + j:halo.shape[0] - k + j + 1, :], out)
    return out


def _shift_up(u, halo, k, tm):
    row = lax.broadcasted_iota(jnp.int32, u.shape, 0)
    out = pltpu.roll(u, tm - k, 0)
    for j in range(k):
        out = jnp.where(row == tm - k + j, halo[j:j + 1, :], out)
    return out


HALO = 16


def _conv_fwd(p, cw, name):
    t, d3 = p.shape
    d = d3 // 3
    tm = ROW_TILE
    hb = tm // HALO

    def body(p_ref, prev_ref, cw_ref, z_ref):
        i = pl.program_id(0)
        b = p_ref[:, 0:d].astype(F32)
        u = p_ref[:, d:2 * d].astype(F32) * p_ref[:, 2 * d:3 * d].astype(F32)
        keep = (i > 0).astype(F32)
        hu = prev_ref[:, d:2 * d].astype(F32) * prev_ref[:, 2 * d:3 * d].astype(F32) * keep
        uc = cw_ref[2:3, :] * u + cw_ref[1:2, :] * _shift_down(u, hu, 1, tm) + cw_ref[0:1, :] * _shift_down(u, hu, 2, tm)
        z_ref[...] = (b * uc).astype(BF16)

    return _call(
        body, name=name, grid=(t // tm,),
        in_specs=[pl.BlockSpec((tm, d3), lambda i: (i, 0)),
                  pl.BlockSpec((HALO, d3), lambda i: (jnp.maximum(i * hb - 1, 0), 0)),
                  pl.BlockSpec((3, d), lambda i: (0, 0))],
        out_specs=pl.BlockSpec((tm, d), lambda i: (i, 0)), out_shape=_sds((t, d), BF16))(p, p, cw)


def _matmul_norm_residual(a3, wg, layer, g, x_res, name):
    kc_n, t, kc = a3.shape
    d = wg.shape[-1]
    per = N_DEV // kc_n
    rows = wg.shape[2]
    tm = ROW_TILE

    def body(a_ref, w_ref, g_ref, x_ref, raw_ref, xo_ref):
        raw = None
        for c in range(kc_n):
            term = _dot(a_ref[c], w_ref[c * per:(c + 1) * per].reshape(per * rows, d))
            raw = term if raw is None else raw + term
        raw_ref[...] = raw
        xo_ref[...] = x_ref[...] + _rms(raw, g_ref[...])

    row_spec = pl.BlockSpec((tm, d), lambda i: (i, 0))
    return _call(
        body, name=name, grid=(t // tm,),
        in_specs=[pl.BlockSpec((kc_n, tm, kc), lambda i: (0, i, 0)),
                  pl.BlockSpec((N_DEV, None, rows, d), lambda i: (0, layer, 0, 0)),
                  pl.BlockSpec((1, d), lambda i: (0, 0)), row_spec],
        out_specs=[row_spec, row_spec], out_shape=[_sds((t, d), F32)] * 2)(a3, wg, g, x_res)


def _alibi_slopes(n_heads):
    hh = np.arange(n_heads, dtype=np.float32) + 1.0
    s = np.power(2.0, -8.0 * hh / n_heads).astype(np.float32)
    return jnp.asarray(np.repeat(s.reshape(n_heads // 2, 2, 1), 2 * ATT_BLOCK, axis=2))


def _band_bias(sl_ref, dil):
    u = lax.broadcasted_iota(jnp.int32, (ATT_BLOCK, 2 * ATT_BLOCK), 0)
    kk = lax.broadcasted_iota(jnp.int32, (ATT_BLOCK, 2 * ATT_BLOCK), 1)
    delta = u + ATT_BLOCK - kk
    valid = (delta >= 0) & (delta <= ATT_BLOCK)
    dist = (delta * dil).astype(F32)
    rows = [jnp.where(valid, -sl_ref[hd:hd + 1, :] * dist, NEG) for hd in range(2)]
    return jnp.concatenate(rows, axis=0)


def _stack_heads(a):
    lane = lax.broadcasted_iota(jnp.int32, a.shape, 1)
    return jnp.concatenate([jnp.where(lane < HEAD_DIM, a, 0.0), jnp.where(lane >= HEAD_DIM, a, 0.0)], axis=0).astype(BF16)


def _unstack_heads(a2):
    top, bot = a2[:ATT_BLOCK], a2[ATT_BLOCK:]
    lane = lax.broadcasted_iota(jnp.int32, top.shape, 1)
    return jnp.where(lane < HEAD_DIM, top, bot)


def _rows_to_lanes(a0, a1):
    eye = lax.broadcasted_iota(jnp.int32, a0.shape, 0) == lax.broadcasted_iota(jnp.int32, a0.shape, 1)
    return jnp.concatenate([jnp.sum(jnp.where(eye, a, 0.0), axis=0, keepdims=True) for a in (a0, a1)], axis=1)


def _fill_bias_t(sl_ref, bias_ref):
    kk = lax.broadcasted_iota(jnp.int32, (2 * ATT_BLOCK, 2 * ATT_BLOCK), 0)
    lane = lax.broadcasted_iota(jnp.int32, (2 * ATT_BLOCK, 2 * ATT_BLOCK), 1)
    delta = lane % ATT_BLOCK + ATT_BLOCK - kk
    valid = (delta >= 0) & (delta <= ATT_BLOCK)
    slope = jnp.concatenate([sl_ref[0:1, :ATT_BLOCK], sl_ref[1:2, :ATT_BLOCK]], axis=1)
    for gi, dil in enumerate(DILATIONS):
        bias = jnp.where(valid, -slope * (delta * dil).astype(F32), NEG)
        bias_ref[2 * gi] = bias
        bias_ref[2 * gi + 1] = jnp.where(kk < ATT_BLOCK, NEG, bias)


def _fill_bias(sl_ref, bias_ref):
    kk = lax.broadcasted_iota(jnp.int32, (2 * ATT_BLOCK, 2 * ATT_BLOCK), 1)
    for gi, dil in enumerate(DILATIONS):
        bias = _band_bias(sl_ref, dil)
        bias_ref[2 * gi] = bias
        bias_ref[2 * gi + 1] = jnp.where(kk < ATT_BLOCK, NEG, bias)


def _attention_fwd(q, kv, slopes, name):
    hp, t, _ = q.shape
    ns = t // SUPER
    nd = len(DILATIONS)

    def body(sl_ref, q_ref, kc_ref, kp_ref, vc_ref, vp_ref, o_ref, lse_ref, kw_ref, vw_ref, og_ref, lg_ref, bias_ref):
        n = pl.program_id(1)
        kw_ref[0:SUPER, :] = kp_ref[...]
        kw_ref[SUPER:, :] = kc_ref[...]
        vw_ref[0:SUPER, :] = vp_ref[...]
        vw_ref[SUPER:, :] = vc_ref[...]

        @pl.when(n == 0)
        def _():
            _fill_bias(sl_ref, bias_ref)

        for gi, dil in enumerate(DILATIONS):

            def block(idx, carry, gi=gi, dil=dil):
                r, b = idx % dil, idx // dil
                qs = b * (ATT_BLOCK * dil) + r
                ks = SUPER + (b - 1) * (ATT_BLOCK * dil) + r
                first = jnp.logical_and(n == 0, b == 0).astype(jnp.int32)
                q2 = _stack_heads(q_ref[pl.ds(qs, ATT_BLOCK, stride=dil), :])
                kb = kw_ref[pl.ds(ks, 2 * ATT_BLOCK, stride=dil), :].astype(BF16)
                vb = vw_ref[pl.ds(ks, 2 * ATT_BLOCK, stride=dil), :].astype(BF16)
                s = _dot_nt(q2, kb) + bias_ref[2 * gi + first]
                m = jnp.max(s, axis=-1, keepdims=True)
                p = jnp.exp(s - m).astype(BF16)
                ol = _dot(p, jnp.concatenate([vb, jnp.ones_like(vb)], axis=1))
                l = ol[:, LANES:]
                o2 = ol[:, :LANES] / l
                lse2 = m + jnp.log(l)
                og_ref[gi, pl.ds(qs, ATT_BLOCK, stride=dil), :] = _unstack_heads(o2)
                lg_ref[gi, pl.ds(qs, ATT_BLOCK, stride=dil), :] = _unstack_heads(lse2)
                return carry

            lax.fori_loop(0, SUPER // ATT_BLOCK, block, 0, unroll=ATT_UNROLL_FWD)
        lg =[lg_ref[gi] for gi in range(nd)]
        top = functools.reduce(jnp.maximum, lg)
        ws = [jnp.exp(x - top) for x in lg]
        tot = functools.reduce(jnp.add, ws)
        lse_ref[...] = top + jnp.log(tot)
        acc = ws[0] * og_ref[0]
        for gi in range(1, nd):
            acc = acc + ws[gi] * og_ref[gi]
        o_ref[...] = (acc / tot).astype(BF16)

    cur = lambda off: pl.BlockSpec((None, SUPER, LANES), lambda h, n: (h + off, n, 0))
    prev = lambda off: pl.BlockSpec((None, SUPER, LANES), lambda h, n: (h + off, jnp.maximum(n - 1, 0), 0))
    return _call(
        body, name=name, grid=(hp, ns),
        in_specs=[pl.BlockSpec((None, 2, 2 * ATT_BLOCK), lambda h, n: (h, 0, 0)), cur(0), cur(0), prev(0), cur(hp), prev(hp)],
        out_specs=[pl.BlockSpec((SUPER, LANES), lambda h, n: (n, h)), cur(0)],
        out_shape=[_sds((t, hp * LANES), BF16), _sds((hp, t, LANES), F32)],
        scratch_shapes=[pltpu.VMEM((2 * SUPER, LANES), F32), pltpu.VMEM((2 * SUPER, LANES), F32),
                        pltpu.VMEM((nd, SUPER, LANES), F32), pltpu.VMEM((nd, SUPER, LANES), F32),
                        pltpu.VMEM((2 * nd, 2 * ATT_BLOCK, 2 * ATT_BLOCK), F32)],
    )(slopes, q, kv, kv, kv, kv)


def _attention_bwd(q, kv, o, lse, d_o, dk_in, dv_in, slopes, q_scale, name):
    hp, t, _ = q.shape
    ns = t // SUPER
    shared = dk_in is not None

    def body(sl_ref, q_ref, kc_ref, kp_ref, vc_ref, vp_ref, o_ref, lse_ref, do_ref, *rest):
        dki_ref, dvi_ref = rest[:2] if shared else (None, None)
        dq_ref, dk_ref, dv_ref, kw_ref, vw_ref, dkw_ref, dvw_ref, st_ref, bias_ref = rest[2 if shared else 0:]
        n = pl.program_id(1)

        @pl.when(n == 0)
        def _():
            dkw_ref[...] = jnp.zeros_like(dkw_ref)
            dvw_ref[...] = jnp.zeros_like(dvw_ref)

        @pl.when(n > 0)
        def _():
            dkw_ref[0:SUPER, :] = dkw_ref[SUPER:, :]
            dvw_ref[0:SUPER, :] = dvw_ref[SUPER:, :]
            dkw_ref[SUPER:, :] = jnp.zeros((SUPER, LANES), F32)
            dvw_ref[SUPER:, :] = jnp.zeros((SUPER, LANES), F32)

        @pl.when(n < ns)
        def _():
            kw_ref[0:SUPER, :] = kp_ref[...]
            kw_ref[SUPER:, :] = kc_ref[...]
            vw_ref[0:SUPER, :] = vp_ref[...]
            vw_ref[SUPER:, :] = vc_ref[...]
            prod = do_ref[...] * o_ref[...].astype(F32)
            lane = lax.broadcasted_iota(jnp.int32, prod.shape, 1)
            zero = jnp.zeros((SUPER, LANES), F32)
            st_ref[0] = zero + jnp.sum(jnp.where(lane < HEAD_DIM, prod, 0.0), axis=-1, keepdims=True)
            st_ref[1] = zero + jnp.sum(jnp.where(lane >= HEAD_DIM, prod, 0.0), axis=-1, keepdims=True)
            lse = lse_ref[...]
            swapped = pltpu.roll(lse, HEAD_DIM, 1)
            st_ref[2] = jnp.where(lane < HEAD_DIM, lse, swapped)
            st_ref[3] = jnp.where(lane >= HEAD_DIM, lse, swapped)
            dq_ref[...] = jnp.zeros_like(dq_ref)

            @pl.when(n == 0)
            def _():
                _fill_bias_t(sl_ref, bias_ref)

            for gi, dil in enumerate(DILATIONS):

                def block(idx, carry, gi=gi, dil=dil):
                    r, b = idx % dil, idx // dil
                    qs = b * (ATT_BLOCK * dil) + r
                    ks = SUPER + (b - 1) * (ATT_BLOCK * dil) + r
                    first = jnp.logical_and(n == 0, b == 0).astype(jnp.int32)
                    rows = pl.ds(qs, ATT_BLOCK, stride=dil)
                    keys = pl.ds(ks, 2 * ATT_BLOCK, stride=dil)
                    q2 = _stack_heads(q_ref[rows, :])
                    do2 = _stack_heads(do_ref[rows, :])
                    kb = kw_ref[keys, :].astype(BF16)
                    vb = vw_ref[keys, :].astype(BF16)
                    dd = _rows_to_lanes(st_ref[0, rows, :], st_ref[1, rows, :])
                    lse_b = _rows_to_lanes(st_ref[2, rows, :], st_ref[3, rows, :])
                    ps, dss = [], []
                    for half in range(2):
                        hk = slice(half * ATT_BLOCK, (half + 1) * ATT_BLOCK)
                        p = jnp.exp(_dot_nt(kb[hk], q2) + bias_ref[2 * gi + first, hk, :] - lse_b)
                        dss.append((p * (_dot_nt(vb[hk], do2) - dd)).astype(BF16))
                        ps.append(p.astype(BF16))
                    p, ds = jnp.concatenate(ps, axis=0), jnp.concatenate(dss, axis=0)
                    dvw_ref[keys, :] += _dot(p, do2)
                    dkw_ref[keys, :] += _dot(ds, q2)
                    dq_ref[rows, :] += _unstack_heads(_dot_tn(ds, kb)) * q_scale
                    return carry

                lax.fori_loop(0, SUPER // ATT_BLOCK, block, 0, unroll=ATT_UNROLL_BWD)

        dk_ref[...] = dkw_ref[0:SUPER, :] + dki_ref[...] if shared else dkw_ref[0:SUPER, :]
        dv_ref[...] = dvw_ref[0:SUPER, :] + dvi_ref[...] if shared else dvw_ref[0:SUPER, :]

    last = ns - 1
    cur = lambda off: pl.BlockSpec((None, SUPER, LANES), lambda h, n: (h + off, jnp.minimum(n, last), 0))
    prev = lambda off: pl.BlockSpec((None, SUPER, LANES), lambda h, n: (h + off, jnp.clip(n - 1, 0, last), 0))
    nat = pl.BlockSpec((SUPER, LANES), lambda h, n: (jnp.minimum(n, last), h))
    late = pl.BlockSpec((None, SUPER, LANES), lambda h, n: (h, jnp.maximum(n - 1, 0), 0))
    dq, dk, dv = _call(
        body, name=name, grid=(hp, ns + 1),
        in_specs=[pl.BlockSpec((None, 2, 2 * ATT_BLOCK), lambda h, n: (h, 0, 0)), cur(0), cur(0), prev(0), cur(hp), prev(hp),
                  nat, cur(0), nat] + ([late, late] if shared else []),
        out_specs=[cur(0), late, late],
        out_shape=[_sds((hp, t, LANES), F32)] * 3,
        scratch_shapes=[pltpu.VMEM((2 * SUPER, LANES), F32)] * 4 + [
            pltpu.VMEM((4, SUPER, LANES), F32), pltpu.VMEM((2 * len(DILATIONS), 2 * ATT_BLOCK, 2 * ATT_BLOCK), F32)],
    )(slopes, q, kv, kv, kv, kv, o, lse, d_o, *((dk_in, dv_in) if shared else ()))
    return dq, dk, dv


def _loss_head(y, target, raw, g, name):
    t, d = y.shape
    tm = ROW_TILE

    def body(y_ref, t_ref, raw_ref, g_ref, sq_ref, dy_ref, draw_ref, dg_ref):
        i = pl.program_id(0)
        err = y_ref[...] - t_ref[...]
        dy = err * (1.0 / d)
        dy_ref[...] = dy
        draw, dg = _rms_bwd(raw_ref[...], g_ref[...], dy)
        draw_ref[...] = draw.astype(BF16)
        sq = jnp.zeros((8, LANES), F32) + jnp.sum(err * err)

        @pl.when(i == 0)
        def _():
            sq_ref[...] = sq
            dg_ref[...] = dg

        @pl.when(i > 0)
        def _():
            sq_ref[...] += sq
            dg_ref[...] += dg

    row = pl.BlockSpec((tm, d), lambda i: (i, 0))
    vec = pl.BlockSpec((1, d), lambda i: (0, 0))
    return _call(
        body, name=name, grid=(t // tm,), in_specs=[row, row, row, vec],
        out_specs=[pl.BlockSpec((8, LANES), lambda i: (0, 0)), row, row, vec],
        out_shape=[_sds((8, LANES), F32), _sds((t, d), F32), _sds((t, d), BF16), _sds((1, d), F32)])(y, target, raw, g)


def _bwd_matmul_norms(a_specs, a_args, a_tile, n_steps, w_spec, w_arg, w_mat, xa, ga, resid, xb, gb, name,
                      w_transposed=False):
    t, d = xa.shape
    tm = BWD_ROW_TILE
    na = len(a_specs)
    second = xb is not None
    per = 4 if n_steps % 4 == 0 else 1
    n_steps //= per

    def blocks_of(spec, k):
        return pl.BlockSpec(spec.block_shape, lambda i, j: spec.index_map(i, per * j + k))

    def body(*refs):
        a_refs, w_refs = refs[:per * na], refs[per * na:per * na + per]
        xa_ref, ga_ref, res_ref = refs[per * na + per:per * na + per + 3]
        rest = refs[per * na + per + 3:]
        if second:
            xb_ref, gb_ref, dx_ref, d2_ref, dga_ref, dgb_ref, acc_ref = rest
        else:
            dx_ref, dga_ref, acc_ref = rest
        i, j = pl.program_id(0), pl.program_id(1)
        part = None
        for k in range(per):
            term = (_dot if w_transposed else _dot_nt)(a_tile(per * j + k, *a_refs[k * na:(k + 1) * na]), w_mat(w_refs[k]))
            part = term if part is None else part + term

        @pl.when(j == 0)
        def _():
            acc_ref[...] = part

        @pl.when(j > 0)
        def _():
            acc_ref[...] += part

        @pl.when(j == n_steps - 1)
        def _():
            da, dga = _rms_bwd(xa_ref[...], ga_ref[...], acc_ref[...])
            dx = res_ref[...] + da
            dx_ref[...] = dx
            if second:
                d2, dgb = _rms_bwd(xb_ref[...], gb_ref[...], dx)
                d2_ref[...] = d2.astype(BF16)

            @pl.when(i == 0)
            def _():
                dga_ref[...] = dga
                if second:
                    dgb_ref[...] = dgb

            @pl.when(i > 0)
            def _():
                dga_ref[...] += dga
                if second:
                    dgb_ref[...] += dgb

    row = pl.BlockSpec((tm, d), lambda i, j: (i, 0))
    vec = pl.BlockSpec((1, d), lambda i, j: (0, 0))
    in_specs = [blocks_of(sp, k) for k in range(per) for sp in a_specs] + [blocks_of(w_spec, k) for k in range(per)]
    in_specs += [row, vec, row]
    args = list(a_args) * per + [w_arg] * per + [xa, ga, resid]
    if second:
        in_specs += [row, vec]
        args += [xb, gb]
        out_specs = [row, row, vec, vec]
        out_shape = [_sds((t, d), F32), _sds((t, d), BF16), _sds((1, d), F32), _sds((1, d), F32)]
    else:
        out_specs = [row, vec]
        out_shape = [_sds((t, d), F32), _sds((1, d), F32)]
    return _call(body, name=name, grid=(t // tm, n_steps), in_specs=in_specs, out_specs=out_specs,
                 out_shape=out_shape, scratch_shapes=[pltpu.VMEM((tm, d), F32)])(*args)


def _heads_to_rows(*refs):
    hp = refs[0].shape[0]
    cols = []
    for p in range(hp):
        v = refs[0][p]
        for r in refs[1:]:
            v = v + r[p]
        cols.append(v)
    return jnp.concatenate(cols, axis=-1).astype(BF16)


def _matmul_nt_rows(a, wg, layer, out_dtype, name):
    t, d = a.shape
    tm = ROW_TILE

    def body(a_ref, w_ref, o_ref):
        o_ref[...] = _dot_nt(a_ref[...], w_ref[...].reshape(d, d)).astype(out_dtype)

    row = pl.BlockSpec((tm, d), lambda i: (i, 0))
    return _call(body, name=name, grid=(t // tm,),
                 in_specs=[row, pl.BlockSpec((N_DEV, None, d // N_DEV, d), lambda i: (0, layer, 0, 0))],
                 out_specs=row, out_shape=_sds((t, d), out_dtype))(a, wg)


def _swiglu_bwd(d_ff, wg, layer, gate, up, name):
    t, d = d_ff.shape
    fc = gate.shape[-1]
    rows = wg.shape[2]
    tm = BIG_ROW_TILE

    def body(df_ref, w_ref, g_ref, u_ref, dh_ref):
        w = w_ref[...].reshape(2 * rows, d)
        for r0 in range(0, tm, SWIGLU_ROWS):
            rs = slice(r0, r0 + SWIGLU_ROWS)
            da = _dot_nt(df_ref[rs, :], w)
            gate, up = g_ref[rs, :].astype(F32), u_ref[rs, :].astype(F32)
            sig = jax.nn.sigmoid(gate)
            dh_ref[0, rs, :] = (da * up * (sig * (1.0 + gate * (1.0 - sig)))).astype(BF16)
            dh_ref[1, rs, :] = (da * (gate * sig)).astype(BF16)

    return _call(
        body, name=name, grid=(t // tm, 4),
        in_specs=[pl.BlockSpec((tm, d), lambda i, c: (i, 0)),
                  pl.BlockSpec((2, None, rows, d), lambda i, c: (c, layer, 0, 0)),
                  pl.BlockSpec((None, tm, fc), lambda i, c: (c, i, 0)),
                  pl.BlockSpec((None, tm, fc), lambda i, c: (c, i, 0))],
        out_specs=pl.BlockSpec((None, 2, tm, fc), lambda i, c: (c, 0, i, 0)),
        out_shape=_sds((4, 2, t, fc), BF16))(d_ff, wg, gate, up)


def _conv_bwd(p, d_z, cw, name):
    t, d3 = p.shape
    d = d3 // 3
    tm = ROW_TILE
    hb = tm // HALO
    nt = t // tm

    def body(p_ref, prev_ref, next_ref, dz_ref, dzn_ref, cw_ref, dp_ref, dcw_ref):
        i = pl.program_id(0)
        b = p_ref[:, 0:d].astype(F32)
        c = p_ref[:, d:2 * d].astype(F32)
        h = p_ref[:, 2 * d:3 * d].astype(F32)
        u = c * h
        hu = prev_ref[:, d:2 * d].astype(F32) * prev_ref[:, 2 * d:3 * d].astype(F32) * (i > 0).astype(F32)
        u1, u2 = _shift_down(u, hu, 1, tm), _shift_down(u, hu, 2, tm)
        uc = cw_ref[2:3, :] * u + cw_ref[1:2, :] * u1 + cw_ref[0:1, :] * u2
        dz = dz_ref[...].astype(F32)
        duc = dz * b
        dn = dzn_ref[...].astype(F32) * next_ref[:, 0:d].astype(F32) * (i < nt - 1).astype(F32)
        du = cw_ref[2:3, :] * duc + cw_ref[1:2, :] * _shift_up(duc, dn, 1, tm) + cw_ref[0:1, :] * _shift_up(duc, dn, 2, tm)
        dp_ref[:, 0:d] = (dz * uc).astype(BF16)
        dp_ref[:, d:2 * d] = (du * h).astype(BF16)
        dp_ref[:, 2 * d:3 * d] = (du * c).astype(BF16)
        dcw = jnp.concatenate([jnp.sum(duc * u2, axis=0, keepdims=True), jnp.sum(duc * u1, axis=0, keepdims=True),
                               jnp.sum(duc * u, axis=0, keepdims=True)], axis=0)

        @pl.when(i == 0)
        def _():
            dcw_ref[...] = dcw

        @pl.when(i > 0)
        def _():
            dcw_ref[...] += dcw

    last_halo = t // HALO - 1
    return _call(
        body, name=name, grid=(nt,),
        in_specs=[pl.BlockSpec((tm, d3), lambda i: (i, 0)),
                  pl.BlockSpec((HALO, d3), lambda i: (jnp.maximum(i * hb - 1, 0), 0)),
                  pl.BlockSpec((HALO, d3), lambda i: (jnp.minimum((i + 1) * hb, last_halo), 0)),
                  pl.BlockSpec((tm, d), lambda i: (i, 0)),
                  pl.BlockSpec((HALO, d), lambda i: (jnp.minimum((i + 1) * hb, last_halo), 0)),
                  pl.BlockSpec((3, d), lambda i: (0, 0))],
        out_specs=[pl.BlockSpec((tm, d3), lambda i: (i, 0)), pl.BlockSpec((3, d), lambda i: (0, 0))],
        out_shape=[_sds((t, d3), BF16), _sds((3, d), F32)])(p, p, p, d_z, d_z, cw)


def _grad_weight(a_specs, a_args, a_tile, b_specs, b_args, b_tile, n_out, acc_shape, out_spec, out_shape, t, name,
                 a_transposed=False):
    tt = GRAD_ROW_TILE
    na, nb = len(a_specs), len(b_specs)

    def body(*refs):
        a_refs, b_refs = refs[:na], refs[na:na + nb]
        o_ref, acc_ref = refs[na + nb:]
        s = pl.program_id(1)
        a, b = a_tile(pl.program_id(0), *a_refs), b_tile(pl.program_id(0), *b_refs)
        part = _dot(a, b) if a_transposed else _dot_tn(a, b)

        @pl.when(s == 0)
        def _():
            acc_ref[...] = part

        @pl.when(s > 0)
        def _():
            acc_ref[...] += part

        @pl.when(s == t // tt - 1)
        def _():
            acc = acc_ref[...].astype(BF16)
            if o_ref.shape[-1] == acc.shape[-1]:
                o_ref[...] = acc.reshape(o_ref.shape)
            else:
                for k in range(o_ref.shape[0]):
                    o_ref[k] = acc[:, k * o_ref.shape[-1]:(k + 1) * o_ref.shape[-1]]

    return _call(body, name=name, grid=(n_out, t // tt), in_specs=list(a_specs) + list(b_specs), out_specs=out_spec,
                 out_shape=out_shape, scratch_shapes=[pltpu.VMEM(acc_shape, F32)])(*a_args, *b_args)


def _ident(*args):
    return args[-1][...]


def _heads_tile(j, *refs):
    return _heads_to_rows(*refs)


def kernel(x, norm_g, conv_in_w, conv_w, conv_out_w, kv_norm_g, kv_w, q_w, o_w, ffn_in_w, ffn_out_w, loss_target, m_norm_g, m_conv_in_w, m_conv_w, m_conv_out_w, m_kv_norm_g, m_kv_w, m_q_w, m_o_w, m_ffn_in_w, m_ffn_out_w, v_norm_g, v_conv_in_w, v_conv_w, v_conv_out_w, v_kv_norm_g, v_kv_w, v_q_w, v_o_w, v_ffn_in_w, v_ffn_out_w):
    x0 = x[0]
    target = loss_target[0]
    t, d = x0.shape
    depth = norm_g.shape[0]
    n_a = conv_in_w.shape[0]
    n_b = q_w.shape[0]
    hp = d // LANES
    tm, tg = BWD_ROW_TILE, GRAD_ROW_TILE
    assert t % SUPER == 0 and d % LANES == 0 and depth == n_a + n_b
    dev = 4 * lax.axis_index("x") + 2 * lax.axis_index("y") + lax.axis_index("c")

    n_small = 4 * depth + 3 * n_a
    small_rows = -(-(n_small + 1) // 8) * 8
    small_local = jnp.concatenate([norm_g.reshape(4 * depth, -1), conv_w.reshape(3 * n_a, -1),
                                   jnp.zeros((small_rows - n_small, norm_g.shape[-1]), F32)], axis=0)
    swap = lambda a: jnp.swapaxes(a, 1, 2)
    big = {"conv_in_w": conv_in_w, "conv_out_w": conv_out_w, "kv_w": kv_w[None], "q_w": q_w, "o_w": o_w,
           "ffn_in_w": swap(ffn_in_w), "ffn_out_w": ffn_out_w}
    names = list(big)

    def group(layer):
        if layer < n_a:
            return [("conv_in_w", layer), ("conv_out_w", layer), ("ffn_in_w", layer), ("ffn_out_w", layer)]
        j = layer - n_a
        return ([("kv_w", 0)] if j == 0 else []) + [("q_w", j), ("o_w", j), ("ffn_in_w", layer), ("ffn_out_w", layer)]

    slot = dev.astype(jnp.int32).reshape(1)
    is_ffn = lambda key: key[0].startswith("ffn")
    first_keys = [key for key in group(0) if not is_ffn(key)]
    first = _all_gather([small_local] + [_cast_layer(big[k], i, None, f"cast_{k}_{i}") for k, i in first_keys], "gather_weights")
    small_all = first[0].transpose(1, 0, 2).reshape(small_rows, d)
    wl = {key: a[:, None] for key, a in zip(first_keys, first[1:])}

    def gather_start(keys, after, tag):
        lands = [_cast_layer(big[k], i, slot, f"cast_{k}_{i}") for k, i in keys]
        send_sems, recv_sems, _, lands, tok = _send_start([], lands, after, "gather_start" + tag)
        return (keys, tag, send_sems, recv_sems, lands), tok[0, 0]

    def gather_wait(flight, after):
        keys, tag, send_sems, recv_sems, lands = flight
        _, lands = _send_wait(send_sems, recv_sems, [], lands, after, "gather_wait" + tag)
        wl.update({key: a[:, None] for key, a in zip(keys, lands)})

    in_flight, token = gather_start([key for key in group(0) if is_ffn(key)], small_all, "_l0")
    W = lambda k, i: (wl[(k, i)], 0)
    gain = lambda layer, k: small_all[4 * layer + k][None]
    taps = lambda layer: small_all[4 * depth + 3 * layer: 4 * depth + 3 * layer + 3]
    g_kv = kv_norm_g[None]
    slopes = _alibi_slopes(d // HEAD_DIM)
    fc = big["ffn_in_w"].shape[-2]
    cb = big["conv_in_w"].shape[-1]
    kvb = big["kv_w"].shape[-1]
    q_scale = HEAD_DIM ** -0.5

    saved = []
    kv = kvn_t = None
    xs = x0
    for layer in range(depth):
        tag = f"_l{layer}"
        g0 = g2 = 0.0
        if layer == 0:
            g0 = token
        else:
            gather_wait(in_flight, xs)
            if layer + 1 < depth:
                in_flight, g0 = gather_start(group(layer + 1), xs, f"_l{layer + 1}")
        s = {"x_in": xs}
        g0 = gain(layer, 0) + g0
        if layer < n_a:
            s["p"], s["xn_t"] = _norm_matmul_cols(xs, g0, *W("conv_in_w", layer), "cols", "conv_in" + tag)
            s["z"] = _conv_fwd(s["p"], taps(layer), "conv" + tag)
            s["mix"], x_mid = _matmul_norm_residual(s["z"][None], *W("conv_out_w", layer), gain(layer, 1), xs, "conv_out" + tag)
        else:
            j = layer - n_a
            if kv is None:
                kv, kvn_t = _norm_matmul_cols(xs, g_kv, *W("kv_w", 0), "heads", "kv_proj")
            s["q"], s["xn"] = _norm_matmul_heads(xs, g0, *W("q_w", j), q_scale, "q_proj" + tag)
            s["o"], s["lse"] = _attention_fwd(s["q"], kv, slopes, "attention" + tag)
            s["mix"], x_mid = _matmul_norm_residual(s["o"][None], *W("o_w", j), gain(layer, 1), xs, "o_proj" + tag)
        s["x_mid"] = x_mid
        if layer == 0:
            gather_wait(in_flight, x_mid)
            in_flight, g2 = gather_start(group(1), x_mid, "_l1")
        s["gate"], s["up"], s["a"], s["fn"] = _ffn_in_swiglu(x_mid, gain(layer, 2) + g2, *W("ffn_in_w", layer), "ffn_in" + tag)
        s["ff"], xs = _matmul_norm_residual(s["a"], *W("ffn_out_w", layer), gain(layer, 3), x_mid, "ffn_out" + tag)
        saved.append(s)

    last = saved[-1]
    sq, dx_out, d_ff, dg3 = _loss_head(xs, target, last["ff"], gain(depth - 1, 3), "loss_head")
    loss = lax.psum(sq[0, 0] * (0.5 / d), ("x", "y", "c"))

    dgain = {(depth - 1, 3): dg3}
    dtaps = {}
    grads = {k: [None] * big[k].shape[0] for k in names}
    dkv_parts = []
    scattering = []

    def scatter_start(keys, tag):
        parts = [grads[k][i] for k, i in keys]
        zones = [lax.empty(p.shape, p.dtype) for p in parts]
        send_sems, recv_sems, parts, zones, tok = _send_start(parts, zones, small_all, "scatter_start" + tag)
        scattering.append((keys, tag, send_sems, recv_sems, parts, zones))
        return tok[0, 0]

    for layer in reversed(range(depth)):
        tag = f"_l{layer}"
        s = saved[layer]
        dh = _swiglu_bwd(d_ff, *W("ffn_out_w", layer), s["gate"], s["up"], "swiglu_bwd" + tag)
        rows_out = big["ffn_out_w"].shape[1]
        grads["ffn_out_w"][layer] = _grad_weight(
            [pl.BlockSpec((None, tg, fc), lambda c, i: (c, i, 0))], [s["a"]], _ident,
            [pl.BlockSpec((tg, d), lambda c, i: (i, 0))], [d_ff], _ident,
            4, (fc, d), pl.BlockSpec((2, rows_out, d), lambda c, i: (c, 0, 0)), _sds((N_DEV, rows_out, d), BF16), t,
            "grad_ffn_out" + tag)
        grads["ffn_in_w"][layer] = _grad_weight(
            [pl.BlockSpec((None, None, tg, fc), lambda j, i: (j % 4, j // 4, i, 0))], [dh], _ident,
            [pl.BlockSpec((tg, d), lambda j, i: (i, 0))], [s["fn"]], _ident,
            N_DEV, (fc, d), pl.BlockSpec((None, fc, d), lambda j, i: (j, 0, 0)), _sds((N_DEV, fc, d), BF16), t,
            "grad_ffn_in" + tag)
        tok = scatter_start([("ffn_in_w", layer), ("ffn_out_w", layer)], "_ffn" + tag)
        dx_mid, d_mix, dg2, dg1 = _bwd_matmul_norms(
            [pl.BlockSpec((None, None, tm, fc), lambda i, j: (j % 4, j // 4, i, 0))], [dh], _ident, N_DEV,
            pl.BlockSpec((None, None, fc, d), lambda i, j: (j, 0, 0, 0)), W("ffn_in_w", layer)[0], _ident,
            s["x_mid"], gain(layer, 2) + tok, dx_out, s["mix"], gain(layer, 1), "ffn_in_bwd" + tag, w_transposed=True)
        dgain[(layer, 2)], dgain[(layer, 1)] = dg2, dg1
        full_rows = pl.BlockSpec((N_DEV, d // N_DEV, d), lambda j, i: (0, 0, 0))
        rows_w = lambda wname, idx: (pl.BlockSpec((N_DEV, None, d // N_DEV, d), lambda i, j: (0, 0, 0, 0)), W(wname, idx)[0],
                                     lambda w_ref: w_ref[...].reshape(d, d))
        if layer < n_a:
            d_z = _matmul_nt_rows(d_mix, *W("conv_out_w", layer), BF16, "conv_out_bwd" + tag)
            grads["conv_out_w"][layer] = _grad_weight(
                [pl.BlockSpec((tg, d), lambda j, i: (i, 0))], [s["z"]], _ident,
                [pl.BlockSpec((tg, d), lambda j, i: (i, 0))], [d_mix], _ident,
                1, (d, d), full_rows, _sds((N_DEV, d // N_DEV, d), BF16), t, "grad_conv_out" + tag)
            d_p, dtaps[layer] = _conv_bwd(s["p"], d_z, taps(layer), "conv_bwd" + tag)
            grads["conv_in_w"][layer] = _grad_weight(
                [pl.BlockSpec((d, tg), lambda j, i: (0, i))], [s["xn_t"]], _ident,
                [pl.BlockSpec((tg, 2 * cb), lambda j, i: (i, j))], [d_p], _ident,
                N_DEV // 2, (d, 2 * cb), pl.BlockSpec((2, d, cb), lambda j, i: (j, 0, 0)), _sds((N_DEV, d, cb), BF16), t,
                "grad_conv_in" + tag, a_transposed=True)
            a_specs, a_args, a_tile, n_steps = [pl.BlockSpec((tm, 4 * cb), lambda i, j: (i, j))], [d_p], _ident, N_DEV // 4
            w_spec = pl.BlockSpec((4, None, d, cb), lambda i, j: (j, 0, 0, 0))
            w_arg = W("conv_in_w", layer)[0]
            w_mat = lambda w_ref: jnp.concatenate([w_ref[k] for k in range(4)], axis=1)
            resid = dx_mid
        else:
            j_b = layer - n_a
            d_o = _matmul_nt_rows(d_mix, *W("o_w", j_b), F32, "o_proj_bwd" + tag)
            grads["o_w"][j_b] = _grad_weight(
                [pl.BlockSpec((tg, d), lambda j, i: (i, 0))], [s["o"]], _ident,
                [pl.BlockSpec((tg, d), lambda j, i: (i, 0))], [d_mix], _ident,
                1, (d, d), full_rows, _sds((N_DEV, d // N_DEV, d), BF16), t, "grad_o" + tag)
            dk_in, dv_in = dkv_parts[0] if dkv_parts else (None, None)
            dq, dk, dv = _attention_bwd(s["q"], kv, s["o"], s["lse"], d_o, dk_in, dv_in, slopes, q_scale, "attention_bwd" + tag)
            dkv_parts = [(dk, dv)]
            heads_spec = pl.BlockSpec((hp, tg, LANES), lambda j, i: (0, i, 0))
            grads["q_w"][j_b] = _grad_weight(
                [pl.BlockSpec((tg, d), lambda j, i: (i, 0))], [s["xn"]], _ident,
                [heads_spec], [dq], _heads_tile,
                1, (d, d), full_rows, _sds((N_DEV, d // N_DEV, d), BF16), t, "grad_q" + tag)
            a_specs, a_args, a_tile, n_steps = [pl.BlockSpec((hp, tm, LANES), lambda i, j: (0, i, 0))], [dq], _heads_tile, 1
            w_spec, w_arg, w_mat = rows_w("q_w", j_b)
            resid = dx_mid
            if layer == n_a:
                pieces = kvb // LANES
                halves = []
                for src in (0, 1):
                    halves.append([part[src] for part in dkv_parts])
                n_half = len(dkv_parts)
                kv_args = [arr for src in (0, 1) for arr in halves[src]]

                def kv_block(src, j):
                    return jnp.where((j // 4) == src, j % 4, 0)

                def kv_tile(j, *refs):
                    keys = _heads_to_rows(*refs[:n_half])
                    vals = _heads_to_rows(*refs[n_half:])
                    return jnp.where(j < 4, keys, vals)

                kv_specs = [pl.BlockSpec((pieces, tm, LANES), functools.partial(lambda i, j, src: (kv_block(src, j), i, 0), src=src))
                            for src in (0, 1) for _ in range(n_half)]
                resid, dgain["kv"] = _bwd_matmul_norms(
                    kv_specs, kv_args, kv_tile, N_DEV,
                    pl.BlockSpec((None, None, d, kvb), lambda i, j: (j, 0, 0, 0)), W("kv_w", 0)[0], _ident,
                    s["x_in"], g_kv, dx_mid, None, None, "kv_proj_bwd")
                kv_b_specs = [pl.BlockSpec((pieces, tg, LANES), functools.partial(lambda j, i, src: (kv_block(src, j), i, 0), src=src))
                              for src in (0, 1) for _ in range(n_half)]
                grads["kv_w"][0] = _grad_weight(
                    [pl.BlockSpec((d, tg), lambda j, i: (0, i))], [kvn_t], _ident,
                    kv_b_specs, kv_args, kv_tile,
                    N_DEV, (d, kvb), pl.BlockSpec((None, d, kvb), lambda j, i: (j, 0, 0)), _sds((N_DEV, d, kvb), BF16), t,
                    "grad_kv", a_transposed=True)
        tok = scatter_start([key for key in group(layer) if not key[0].startswith("ffn")], "_mix" + tag)
        if layer > 0:
            prev = saved[layer - 1]
            dx_out, d_ff, dg0, dg3p = _bwd_matmul_norms(
                a_specs, a_args, a_tile, n_steps, w_spec, w_arg, w_mat,
                s["x_in"], gain(layer, 0) + tok, resid, prev["ff"], gain(layer - 1, 3), "mixer_in_bwd" + tag)
            dgain[(layer, 0)], dgain[(layer - 1, 3)] = dg0, dg3p
        else:
            grad_x, dg0 = _bwd_matmul_norms(
                a_specs, a_args, a_tile, n_steps, w_spec, w_arg, w_mat,
                s["x_in"], gain(layer, 0), resid, None, None, "mixer_in_bwd" + tag)
            dgain[(layer, 0)] = dg0

    small_grad = jnp.concatenate(
        [dgain[(layer, k)] for layer in range(depth) for k in range(4)] + [dtaps[layer] for layer in range(n_a)]
        + [dgain["kv"]] + [jnp.zeros((small_rows - n_small - 1, d), F32)], axis=0)
    small_grads_all = _all_gather([small_grad], "gather_small_grads")[0]
    lo = dev * (d // N_DEV)

    def pack(ng, cwp, kvg):
        rows = jnp.concatenate([ng.reshape(4 * depth, -1), cwp.reshape(3 * n_a, -1)], axis=0)
        z = lax.dynamic_update_slice(jnp.zeros((small_rows, d), F32), rows, (0, lo))
        return lax.dynamic_update_slice(z, kvg[None], (n_small, 0))

    w_small = lax.dynamic_update_slice(small_all, g_kv, (n_small, 0))
    m_small, v_small = pack(m_norm_g, m_conv_w, m_kv_norm_g), pack(v_norm_g, v_conv_w, v_kv_norm_g)
    sm = _small_adamw(small_grads_all, w_small, m_small, v_small, "adamw_small")

    def unpack(a):
        mine = lax.dynamic_slice(a, (0, lo), (small_rows, d // N_DEV))
        return (mine[:4 * depth].reshape(norm_g.shape), mine[4 * depth:n_small].reshape(conv_w.shape), a[n_small])

    small_out = [unpack(a) for a in sm]

    moments = {"conv_in_w": (m_conv_in_w, v_conv_in_w), "conv_out_w": (m_conv_out_w, v_conv_out_w),
               "kv_w": (m_kv_w[None], v_kv_w[None]), "q_w": (m_q_w, v_q_w), "o_w": (m_o_w, v_o_w),
               "ffn_in_w": (swap(m_ffn_in_w), swap(v_ffn_in_w)), "ffn_out_w": (m_ffn_out_w, v_ffn_out_w)}
    landed = {k: [None] * big[k].shape[0] for k in names}
    for keys, tag, send_sems, recv_sems, parts, zones in scattering:
        parts, zones = _send_wait(send_sems, recv_sems, parts, zones, grad_x, "scatter_wait" + tag)
        for (k, i), part, zone in zip(keys, parts, zones):
            landed[k][i] = (part, zone)
    res = {k: _sum_adamw(slot, [p for p, _ in landed[k]], [z for _, z in landed[k]], big[k], moments[k][0], moments[k][1],
                         "adamw_" + k) for k in names}

    def big_out(k, which):
        out = res[k][which]
        return out[0] if k == "kv_w" else swap(out) if k == "ffn_in_w" else out

    out_names = ["norm_g", "conv_in_w", "conv_w", "conv_out_w", "kv_norm_g", "kv_w", "q_w", "o_w", "ffn_in_w", "ffn_out_w"]
    small_pos = {"norm_g": 0, "conv_w": 1, "kv_norm_g": 2}
    outs = [loss, grad_x[None]]
    for which in range(4):
        for k in out_names:
            outs.append(small_out[which][small_pos[k]] if k in small_pos else big_out(k, which))
    return tuple(outs)
```

```python
import functools
import math

import numpy as np
import jax
import jax.numpy as jnp
from jax import lax
from jax.experimental import pallas as pl
from jax.experimental.pallas import tpu as pltpu

F32 = jnp.float32
BF16 = jnp.bfloat16

N_DEV = 8
RMS_EPS = 1e-6
HEAD_DIM = 64
LANES = 128
ATT_BLOCK = 128
DILATIONS = (1, 4, 16)
SUPER = ATT_BLOCK * DILATIONS[-1]
NEG = -1e30
ATT_UNROLL_FWD = 8
ATT_UNROLL_BWD = 8

ADAM_LR, ADAM_B1, ADAM_B2, ADAM_EPS, ADAM_WD, ADAM_STEP = 0.001, 0.9, 0.999, 1e-08, 0.01, 10

ROW_TILE = 512
BIG_ROW_TILE = 1024
SWIGLU_ROWS = 256
GRAD_ROW_TILE = 2048
BWD_ROW_TILE = 512
MESH = pl.DeviceIdType.MESH


def _call(body, *, name, grid=None, in_specs=None, out_specs=None, out_shape=None, scratch_shapes=(), prefetch=False,
          **params):
    cp = pltpu.CompilerParams(**params) if params else None
    if prefetch:
        spec = pltpu.PrefetchScalarGridSpec(num_scalar_prefetch=1, grid=grid, in_specs=in_specs, out_specs=out_specs,
                                            scratch_shapes=list(scratch_shapes))
        return pl.pallas_call(body, name=name, grid_spec=spec, out_shape=out_shape, compiler_params=cp)
    kwargs = {k: v for k, v in (("grid", grid), ("in_specs", in_specs), ("out_specs", out_specs)) if v is not None}
    return pl.pallas_call(body, name=name, out_shape=out_shape, scratch_shapes=list(scratch_shapes),
                          compiler_params=cp, **kwargs)


def _sds(shape, dtype):
    return jax.ShapeDtypeStruct(tuple(shape), dtype)


def _rms(x, g):
    r = lax.rsqrt(jnp.mean(x * x, axis=-1, keepdims=True) + RMS_EPS)
    return x * r * g


def _rms_bwd(x, g, dy):
    r = lax.rsqrt(jnp.mean(x * x, axis=-1, keepdims=True) + RMS_EPS)
    xh = x * r
    dxh = dy * g
    dx = r * (dxh - xh * jnp.mean(dxh * xh, axis=-1, keepdims=True))
    return dx, jnp.sum(dy * xh, axis=0, keepdims=True)


def _dot(a, b):
    return jnp.dot(a, b, preferred_element_type=F32)


def _dot_nt(a, b):
    return lax.dot_general(a, b, (((1,), (1,)), ((), ())), preferred_element_type=F32)


def _dot_tn(a, b):
    return lax.dot_general(a, b, (((0,), (0,)), ((), ())), preferred_element_type=F32)


def _mesh_pos():
    return lax.axis_index("x"), lax.axis_index("y"), lax.axis_index("c")


def _all_gather(arrs, name):
    n = len(arrs)

    def body(*refs):
        ins, outs = refs[:n], refs[n:2 * n]
        send_sems, recv_sems, local_sems = refs[2 * n:]
        x, y, c = _mesh_pos()
        me, sibling = (x, y, c), (x, y, 1 - c)
        chips = [(1 - x, y), (x, 1 - y), (1 - x, 1 - y)]

        def copy(a, k, block, to, src=None):
            dst = outs[a].at[4 * block[0] + 2 * block[1] + block[2]]
            return pltpu.make_async_remote_copy(
                src_ref=dst if src is None else src, dst_ref=dst, send_sem=send_sems.at[a, k],
                recv_sem=recv_sems.at[a, k], device_id=to, device_id_type=MESH)

        started = []
        for a in range(n):
            mine = pltpu.make_async_copy(ins[a], outs[a].at[4 * x + 2 * y + c], local_sems.at[a])
            mine.start()
            started.append(mine)
        first = []
        for a in range(n):
            first.append(copy(a, 0, me, sibling, src=ins[a]))
            first += [copy(a, 1 + j, me, (*chip, c), src=ins[a]) for j, chip in enumerate(chips)]
        for cp in first:
            cp.start()
        passed = []
        for a in range(n):
            for j, chip in enumerate(chips):
                copy(a, 1 + j, (*chip, c), me).wait_recv()
                fwd = copy(a, 4 + j, (*chip, c), sibling)
                fwd.start()
                passed.append(fwd)
        for a in range(n):
            copy(a, 0, sibling, me).wait_recv()
            for j, chip in enumerate(chips):
                copy(a, 4 + j, (*chip, 1 - c), me).wait_recv()
        for cp in first + passed:
            cp.wait_send()
        for cp in started:
            cp.wait()

    any_spec = pl.BlockSpec(memory_space=pl.ANY)
    outs = _call(
        body, name=name, in_specs=[any_spec] * n, out_specs=[any_spec] * n,
        out_shape=[_sds((N_DEV,) + a.shape, a.dtype) for a in arrs],
        scratch_shapes=[pltpu.SemaphoreType.DMA((n, 7)), pltpu.SemaphoreType.DMA((n, 7)), pltpu.SemaphoreType.DMA((n,))],
        has_side_effects=True,
    )(*arrs)
    return list(outs)


HBM_SPEC = pl.BlockSpec(memory_space=pltpu.HBM)
SEM_SPEC = pl.BlockSpec(memory_space=pltpu.SEMAPHORE)
DATAFLOW = pltpu.SideEffectType.DATAFLOW_SIDE_EFFECTING
PEERS = [(dx, dy, dc) for dx in (0, 1) for dy in (0, 1) for dc in (0, 1)][1:]


def _peer(flip):
    x, y, c = _mesh_pos()
    return tuple(1 - v if f else v for v, f in zip((x, y, c), flip))


def _slot(pos):
    return 4 * pos[0] + 2 * pos[1] + pos[2]


def _in_hbm(a):
    return pltpu.with_memory_space_constraint(a, pltpu.HBM)


def _direct_copies(srcs, lands, send_sems, recv_sems, scatter):
    me = _slot(_mesh_pos())
    copies = []
    for a in range(len(lands)):
        for k, flip in enumerate(PEERS):
            peer = _peer(flip)
            src = srcs[a].at[_slot(peer)] if scatter else lands[a].at[me]
            idx = a * len(PEERS) + k
            copies.append(pltpu.make_async_remote_copy(
                src_ref=src, dst_ref=lands[a].at[me], send_sem=send_sems.at[idx], recv_sem=recv_sems.at[idx],
                device_id=peer, device_id_type=MESH))
    return copies


def _send_start(srcs, lands, after, name):
    ns, nl = len(srcs), len(lands)
    scatter = ns > 0

    def body(*refs):
        src_refs, land_refs = refs[:ns], refs[ns:ns + nl]
        send_sems, recv_sems = refs[ns + nl + 1:ns + nl + 3]
        token = refs[-1]
        for cp in _direct_copies(src_refs, land_refs, send_sems, recv_sems, scatter):
            cp.start()
        token[...] = jnp.zeros_like(token)

    sem = pltpu.SemaphoreType.DMA((nl * len(PEERS),))
    outs = pl.pallas_call(
        body, name=name,
        out_shape=(sem, sem) + tuple(pltpu.HBM(a.shape, a.dtype) for a in list(srcs) + list(lands))
        + (_sds((8, LANES), F32),),
        in_specs=[HBM_SPEC] * (ns + nl) + [pl.BlockSpec(memory_space=pl.ANY)],
        out_specs=(SEM_SPEC, SEM_SPEC) + (HBM_SPEC,) * (ns + nl) + (pl.BlockSpec(memory_space=pltpu.VMEM),),
        input_output_aliases={i: 2 + i for i in range(ns + nl)},
        compiler_params=pltpu.CompilerParams(has_side_effects=DATAFLOW),
    )(*[_in_hbm(a) for a in list(srcs) + list(lands)], after)
    send_sems, recv_sems = outs[0], outs[1]
    return send_sems, recv_sems, list(outs[2:2 + ns]), list(outs[2 + ns:2 + ns + nl]), outs[-1]


def _send_wait(send_sems, recv_sems, srcs, lands, after, name):
    ns, nl = len(srcs), len(lands)
    scatter = ns > 0

    def body(*refs):
        src_refs, land_refs = refs[:ns], refs[ns:ns + nl]
        send_sems, recv_sems = refs[ns + nl:ns + nl + 2]
        copies = _direct_copies(src_refs, land_refs, send_sems, recv_sems, scatter)
        for cp in copies:
            cp.wait_send()
        for cp in copies:
            cp.wait_recv()

    outs = pl.pallas_call(
        body, name=name,
        out_shape=tuple(pltpu.HBM(a.shape, a.dtype) for a in list(srcs) + list(lands)),
        in_specs=[HBM_SPEC] * (ns + nl) + [SEM_SPEC, SEM_SPEC, pl.BlockSpec(memory_space=pl.ANY)],
        out_specs=(HBM_SPEC,) * (ns + nl),
        input_output_aliases={i: i for i in range(ns + nl)},
        compiler_params=pltpu.CompilerParams(has_side_effects=DATAFLOW),
    )(*srcs, *lands, send_sems, recv_sems, after)
    return list(outs[:ns]), list(outs[ns:])


def _row_tile(rows, cap=512):
    t = min(rows, cap)
    while rows % t or (t % 16 and t != rows):
        t -= 1
    return t


def _as2d(a):
    return a.reshape(-1, a.shape[-1])


def _cast_layer(w, layer, slot, name):
    _, rows, cols = w.shape
    tr = _row_tile(rows)

    def body(*refs):
        refs[-1][...] = refs[-2][...].astype(BF16)

    if slot is None:
        return _call(body, name=name, grid=(rows // tr,),
                     in_specs=[pl.BlockSpec((None, tr, cols), lambda i: (layer, i, 0))],
                     out_specs=pl.BlockSpec((tr, cols), lambda i: (i, 0)), out_shape=_sds((rows, cols), BF16))(w)
    return _call(body, name=name, grid=(rows // tr,), prefetch=True,
                 in_specs=[pl.BlockSpec((None, tr, cols), lambda i, s: (layer, i, 0))],
                 out_specs=pl.BlockSpec((None, tr, cols), lambda i, s: (s[0], i, 0)),
                 out_shape=_sds((N_DEV, rows, cols), BF16))(slot, w)


def _sum_adamw(slot, parts, lands, w, m, v, name):
    n_l = len(parts)
    _, rows, cols = lands[0].shape
    tr = _row_tile(rows, 128)

    def body(s_ref, *refs):
        p_refs, l_refs = refs[:n_l], refs[n_l:2 * n_l]
        w_ref, m_ref, v_ref, g_ref, d_ref, nm_ref, nv_ref = refs[2 * n_l:]
        for k in range(n_l):
            @pl.when(pl.program_id(0) == k)
            def _(k=k):
                own = p_refs[k][...]
                g = jnp.zeros((tr, cols), F32)
                for j in range(N_DEV):
                    g = g + jnp.where(s_ref[0] == j, own, l_refs[k][j]).astype(F32)
                delta, nm, nv = _adamw_math(w_ref[...], g, m_ref[...], v_ref[...])
                g_ref[...] = g
                d_ref[...] = delta
                nm_ref[...] = nm
                nv_ref[...] = nv

    def own_block(k):
        return pl.BlockSpec((None, tr, cols), lambda l, i, s: (s[0], jnp.where(l == k, i, 0), 0))

    def zone_block(k):
        return pl.BlockSpec((N_DEV, tr, cols), lambda l, i, s: (0, jnp.where(l == k, i, 0), 0))

    lay = pl.BlockSpec((None, tr, cols), lambda l, i, s: (l, i, 0))
    return _call(body, name=name, grid=(n_l, rows // tr), prefetch=True,
                 in_specs=[own_block(k) for k in range(n_l)] + [zone_block(k) for k in range(n_l)] + [lay, lay, lay],
                 out_specs=[lay] * 4, out_shape=[_sds((n_l, rows, cols), F32)] * 4)(slot, *parts, *lands, w, m, v)


def _adamw_math(w, g, m, v):
    m = ADAM_B1 * m + (1.0 - ADAM_B1) * g
    v = ADAM_B2 * v + (1.0 - ADAM_B2) * (g * g)
    m_hat = m / (1.0 - ADAM_B1 ** ADAM_STEP)
    v_hat = v / (1.0 - ADAM_B2 ** ADAM_STEP)
    delta = -ADAM_LR * (m_hat / (jnp.sqrt(v_hat) + ADAM_EPS) + ADAM_WD * w)
    return delta, m, v


def _small_adamw(gathered, w, m, v, name):
    def body(a_ref, w_ref, m_ref, v_ref, g_ref, d_ref, nm_ref, nv_ref):
        g = a_ref[0]
        for k in range(1, N_DEV):
            g = g + a_ref[k]
        delta, nm, nv = _adamw_math(w_ref[...], g, m_ref[...], v_ref[...])
        g_ref[...] = g
        d_ref[...] = delta
        nm_ref[...] = nm
        nv_ref[...] = nv

    return _call(body, name=name, out_shape=[_sds(w.shape, F32)] * 4)(gathered, w, m, v)


def _norm_matmul_cols(x, g, wg, layer, mode, name):
    t, d = x.shape
    nb = wg.shape[-1]
    tm = BIG_ROW_TILE
    per = 2
    pieces = per * nb // LANES

    def body(x_ref, g_ref, w_ref, y_ref, xnt_ref, xn_ref):
        @pl.when(pl.program_id(1) == 0)
        def _():
            xn = _rms(x_ref[...], g_ref[...])
            xn_ref[...] = xn.astype(BF16)
            xnt_ref[...] = xn.T.astype(BF16)

        y = _dot(xn_ref[...], jnp.concatenate([w_ref[k] for k in range(per)], axis=1))
        if mode == "heads":
            for p in range(pieces):
                y_ref[p] = y[:, p * LANES:(p + 1) * LANES]
        else:
            y_ref[...] = y.astype(BF16)

    if mode == "cols":
        y_shape, y_spec = _sds((t, N_DEV * nb), BF16), pl.BlockSpec((tm, per * nb), lambda i, j: (i, j))
    else:
        y_shape = _sds((N_DEV // per * pieces, t, LANES), F32)
        y_spec = pl.BlockSpec((pieces, tm, LANES), lambda i, j: (j, i, 0))
    return _call(
        body, name=name, grid=(t // tm, N_DEV // per),
        in_specs=[pl.BlockSpec((tm, d), lambda i, j: (i, 0)), pl.BlockSpec((1, d), lambda i, j: (0, 0)),
                  pl.BlockSpec((per, None, d, nb), lambda i, j: (j, layer, 0, 0))],
        out_specs=[y_spec, pl.BlockSpec((d, tm), lambda i, j: (0, i))],
        out_shape=[y_shape, _sds((d, t), BF16)], scratch_shapes=[pltpu.VMEM((tm, d), BF16)])(x, g, wg)


def _ffn_in_swiglu(x, g, wg, layer, name):
    t, d = x.shape
    fc = wg.shape[-2]
    tm = BIG_ROW_TILE

    def body(x_ref, g_ref, wg_ref, wu_ref, gate_ref, up_ref, a_ref, xn_ref):
        @pl.when(pl.program_id(1) == 0)
        def _():
            xn_ref[...] = _rms(x_ref[...], g_ref[...]).astype(BF16)

        xn = xn_ref[...]
        gate, up = _dot_nt(xn, wg_ref[...]), _dot_nt(xn, wu_ref[...])
        gate_ref[...] = gate.astype(BF16)
        up_ref[...] = up.astype(BF16)
        a_ref[...] = (gate * jax.nn.sigmoid(gate) * up).astype(BF16)

    chunk = pl.BlockSpec((None, tm, fc), lambda i, c: (c, i, 0))
    return _call(
        body, name=name, grid=(t // tm, 4),
        in_specs=[pl.BlockSpec((tm, d), lambda i, c: (i, 0)), pl.BlockSpec((1, d), lambda i, c: (0, 0)),
                  pl.BlockSpec((None, None, fc, d), lambda i, c: (c, layer, 0, 0)),
                  pl.BlockSpec((None, None, fc, d), lambda i, c: (c + 4, layer, 0, 0))],
        out_specs=[chunk, chunk, chunk, pl.BlockSpec((tm, d), lambda i, c: (i, 0))],
        out_shape=[_sds((4, t, fc), BF16)] * 3 + [_sds((t, d), BF16)])(x, g, wg, wg)


def _norm_matmul_heads(x, g, wg, layer, scale, name):
    t, d = x.shape
    tm = ROW_TILE
    hp = d // LANES

    def body(x_ref, g_ref, w_ref, y_ref, xn_ref):
        xn = _rms(x_ref[...], g_ref[...]).astype(BF16)
        xn_ref[...] = xn
        y = _dot(xn, w_ref[...].reshape(d, d)) * scale
        for p in range(hp):
            y_ref[p] = y[:, p * LANES:(p + 1) * LANES]

    return _call(
        body, name=name, grid=(t // tm,),
        in_specs=[pl.BlockSpec((tm, d), lambda i: (i, 0)), pl.BlockSpec((1, d), lambda i: (0, 0)),
                  pl.BlockSpec((N_DEV, None, d // N_DEV, d), lambda i: (0, layer, 0, 0))],
        out_specs=[pl.BlockSpec((hp, tm, LANES), lambda i: (0, i, 0)), pl.BlockSpec((tm, d), lambda i: (i, 0))],
        out_shape=[_sds((hp, t, LANES), F32), _sds((t, d), BF16)])(x, g, wg)


def _shift_down(u, halo, k, tm):
    row = lax.broadcasted_iota(jnp.int32, u.shape, 0)
    out = pltpu.roll(u, k, 0)
    for j in range(k):
        out = jnp.where(row == j, halo[halo.shape[0] - k + j:halo.shape[0] - k + j + 1, :], out)
    return out


def _shift_up(u, halo, k, tm):
    row = lax.broadcasted_iota(jnp.int32, u.shape, 0)
    out = pltpu.roll(u, tm - k, 0)
    for j in range(k):
        out = jnp.where(row == tm - k + j, halo[j:j + 1, :], out)
    return out


HALO = 16


def _conv_fwd(p, cw, name):
    t, d3 = p.shape
    d = d3 // 3
    tm = ROW_TILE
    hb = tm // HALO

    def body(p_ref, prev_ref, cw_ref, z_ref):
        i = pl.program_id(0)
        b = p_ref[:, 0:d].astype(F32)
        u = p_ref[:, d:2 * d].astype(F32) * p_ref[:, 2 * d:3 * d].astype(F32)
        keep = (i > 0).astype(F32)
        hu = prev_ref[:, d:2 * d].astype(F32) * prev_ref[:, 2 * d:3 * d].astype(F32) * keep
        uc = cw_ref[2:3, :] * u + cw_ref[1:2, :] * _shift_down(u, hu, 1, tm) + cw_ref[0:1, :] * _shift_down(u, hu, 2, tm)
        z_ref[...] = (b * uc).astype(BF16)

    return _call(
        body, name=name, grid=(t // tm,),
        in_specs=[pl.BlockSpec((tm, d3), lambda i: (i, 0)),
                  pl.BlockSpec((HALO, d3), lambda i: (jnp.maximum(i * hb - 1, 0), 0)),
                  pl.BlockSpec((3, d), lambda i: (0, 0))],
        out_specs=pl.BlockSpec((tm, d), lambda i: (i, 0)), out_shape=_sds((t, d), BF16))(p, p, cw)


def _matmul_norm_residual(a3, wg, layer, g, x_res, name):
    kc_n, t, kc = a3.shape
    d = wg.shape[-1]
    per = N_DEV // kc_n
    rows = wg.shape[2]
    tm = ROW_TILE

    def body(a_ref, w_ref, g_ref, x_ref, raw_ref, xo_ref):
        raw = None
        for c in range(kc_n):
            term = _dot(a_ref[c], w_ref[c * per:(c + 1) * per].reshape(per * rows, d))
            raw = term if raw is None else raw + term
        raw_ref[...] = raw
        xo_ref[...] = x_ref[...] + _rms(raw, g_ref[...])

    row_spec = pl.BlockSpec((tm, d), lambda i: (i, 0))
    return _call(
        body, name=name, grid=(t // tm,),
        in_specs=[pl.BlockSpec((kc_n, tm, kc), lambda i: (0, i, 0)),
                  pl.BlockSpec((N_DEV, None, rows, d), lambda i: (0, layer, 0, 0)),
                  pl.BlockSpec((1, d), lambda i: (0, 0)), row_spec],
        out_specs=[row_spec, row_spec], out_shape=[_sds((t, d), F32)] * 2)(a3, wg, g, x_res)


def _alibi_slopes(n_heads):
    hh = np.arange(n_heads, dtype=np.float32) + 1.0
    s = np.power(2.0, -8.0 * hh / n_heads).astype(np.float32)
    return jnp.asarray(np.repeat(s.reshape(n_heads // 2, 2, 1), 2 * ATT_BLOCK, axis=2))


def _band_bias(sl_ref, dil):
    u = lax.broadcasted_iota(jnp.int32, (ATT_BLOCK, 2 * ATT_BLOCK), 0)
    kk = lax.broadcasted_iota(jnp.int32, (ATT_BLOCK, 2 * ATT_BLOCK), 1)
    delta = u + ATT_BLOCK - kk
    valid = (delta >= 0) & (delta <= ATT_BLOCK)
    dist = (delta * dil).astype(F32)
    rows = [jnp.where(valid, -sl_ref[hd:hd + 1, :] * dist, NEG) for hd in range(2)]
    return jnp.concatenate(rows, axis=0)


def _stack_heads(a):
    lane = lax.broadcasted_iota(jnp.int32, a.shape, 1)
    return jnp.concatenate([jnp.where(lane < HEAD_DIM, a, 0.0), jnp.where(lane >= HEAD_DIM, a, 0.0)], axis=0).astype(BF16)


def _unstack_heads(a2):
    top, bot = a2[:ATT_BLOCK], a2[ATT_BLOCK:]
    lane = lax.broadcasted_iota(jnp.int32, top.shape, 1)
    return jnp.where(lane < HEAD_DIM, top, bot)


def _rows_to_lanes(a0, a1):
    eye = lax.broadcasted_iota(jnp.int32, a0.shape, 0) == lax.broadcasted_iota(jnp.int32, a0.shape, 1)
    return jnp.concatenate([jnp.sum(jnp.where(eye, a, 0.0), axis=0, keepdims=True) for a in (a0, a1)], axis=1)


def _fill_bias_t(sl_ref, bias_ref):
    kk = lax.broadcasted_iota(jnp.int32, (2 * ATT_BLOCK, 2 * ATT_BLOCK), 0)
    lane = lax.broadcasted_iota(jnp.int32, (2 * ATT_BLOCK, 2 * ATT_BLOCK), 1)
    delta = lane % ATT_BLOCK + ATT_BLOCK - kk
    valid = (delta >= 0) & (delta <= ATT_BLOCK)
    slope = jnp.concatenate([sl_ref[0:1, :ATT_BLOCK], sl_ref[1:2, :ATT_BLOCK]], axis=1)
    for gi, dil in enumerate(DILATIONS):
        bias = jnp.where(valid, -slope * (delta * dil).astype(F32), NEG)
        bias_ref[2 * gi] = bias
        bias_ref[2 * gi + 1] = jnp.where(kk < ATT_BLOCK, NEG, bias)


def _fill_bias(sl_ref, bias_ref):
    kk = lax.broadcasted_iota(jnp.int32, (2 * ATT_BLOCK, 2 * ATT_BLOCK), 1)
    for gi, dil in enumerate(DILATIONS):
        bias = _band_bias(sl_ref, dil)
        bias_ref[2 * gi] = bias
        bias_ref[2 * gi + 1] = jnp.where(kk < ATT_BLOCK, NEG, bias)


def _strided_keys(dil, r, ks, kc_ref, vc_ref, kw_ref, vw_ref, kcar_ref, vcar_ref):
    if dil * ATT_BLOCK != SUPER:
        keys = pl.ds(ks, 2 * ATT_BLOCK, stride=dil)
        return kw_ref[keys, :].astype(BF16), vw_ref[keys, :].astype(BF16)
    own = pl.ds(r, ATT_BLOCK, stride=dil)
    k_own, v_own = kc_ref[own, :].astype(BF16), vc_ref[own, :].astype(BF16)
    kb = jnp.concatenate([kcar_ref[r], k_own], axis=0)
    vb = jnp.concatenate([vcar_ref[r], v_own], axis=0)
    kcar_ref[r] = k_own
    vcar_ref[r] = v_own
    return kb, vb


def _attention_fwd(q, kv, slopes, name):
    hp, t, _ = q.shape
    ns = t // SUPER
    nd = len(DILATIONS)

    def body(sl_ref, q_ref, kc_ref, kp_ref, vc_ref, vp_ref, o_ref, lse_ref, kw_ref, vw_ref, og_ref, lg_ref, bias_ref,
             kcar_ref, vcar_ref):
        n = pl.program_id(1)
        kw_ref[0:SUPER, :] = kp_ref[...]
        kw_ref[SUPER:, :] = kc_ref[...]
        vw_ref[0:SUPER, :] = vp_ref[...]
        vw_ref[SUPER:, :] = vc_ref[...]

        @pl.when(n == 0)
        def _():
            _fill_bias(sl_ref, bias_ref)
            kcar_ref[...] = jnp.zeros_like(kcar_ref)
            vcar_ref[...] = jnp.zeros_like(vcar_ref)

        for gi, dil in enumerate(DILATIONS):

            def block(idx, carry, gi=gi, dil=dil):
                r, b = idx % dil, idx // dil
                qs = b * (ATT_BLOCK * dil) + r
                ks = SUPER + (b - 1) * (ATT_BLOCK * dil) + r
                first = jnp.logical_and(n == 0, b == 0).astype(jnp.int32)
                q2 = _stack_heads(q_ref[pl.ds(qs, ATT_BLOCK, stride=dil), :])
                kb, vb = _strided_keys(dil, r, ks, kc_ref, vc_ref, kw_ref, vw_ref, kcar_ref, vcar_ref)
                s = _dot_nt(q2, kb) + bias_ref[2 * gi + first]
                m = jnp.max(s, axis=-1, keepdims=True)
                p = jnp.exp(s - m).astype(BF16)
                ol = _dot(p, jnp.concatenate([vb, jnp.ones_like(vb)], axis=1))
                l = ol[:, LANES:]
                o2 = ol[:, :LANES] / l
                lse2 = m + jnp.log(l)
                og_ref[gi, pl.ds(qs, ATT_BLOCK, stride=dil), :] = _unstack_heads(o2)
                lg_ref[gi, pl.ds(qs, ATT_BLOCK, stride=dil), :] = _unstack_heads(lse2)
                return carry

            lax.fori_loop(0, SUPER // ATT_BLOCK, block, 0, unroll=ATT_UNROLL_FWD)
        lg =[lg_ref[gi] for gi in range(nd)]
        top = functools.reduce(jnp.maximum, lg)
        ws = [jnp.exp(x - top) for x in lg]
        tot = functools.reduce(jnp.add, ws)
        lse_ref[...] = top + jnp.log(tot)
        acc = ws[0] * og_ref[0]
        for gi in range(1, nd):
            acc = acc + ws[gi] * og_ref[gi]
        o_ref[...] = (acc / tot).astype(BF16)

    cur = lambda off: pl.BlockSpec((None, SUPER, LANES), lambda h, n: (h + off, n, 0))
    prev = lambda off: pl.BlockSpec((None, SUPER, LANES), lambda h, n: (h + off, jnp.maximum(n - 1, 0), 0))
    return _call(
        body, name=name, grid=(hp, ns),
        in_specs=[pl.BlockSpec((None, 2, 2 * ATT_BLOCK), lambda h, n: (h, 0, 0)), cur(0), cur(0), prev(0), cur(hp), prev(hp)],
        out_specs=[pl.BlockSpec((SUPER, LANES), lambda h, n: (n, h)), cur(0)],
        out_shape=[_sds((t, hp * LANES), BF16), _sds((hp, t, LANES), F32)],
        scratch_shapes=[pltpu.VMEM((2 * SUPER, LANES), F32), pltpu.VMEM((2 * SUPER, LANES), F32),
                        pltpu.VMEM((nd, SUPER, LANES), F32), pltpu.VMEM((nd, SUPER, LANES), F32),
                        pltpu.VMEM((2 * nd, 2 * ATT_BLOCK, 2 * ATT_BLOCK), F32)] + [
                            pltpu.VMEM((DILATIONS[-1], ATT_BLOCK, LANES), BF16)] * 2,
    )(slopes, q, kv, kv, kv, kv)


def _attention_bwd(q, kv, o, lse, d_o, dk_in, dv_in, slopes, q_scale, name):
    hp, t, _ = q.shape
    ns = t // SUPER
    shared = dk_in is not None

    def body(sl_ref, q_ref, kc_ref, kp_ref, vc_ref, vp_ref, o_ref, lse_ref, do_ref, *rest):
        dki_ref, dvi_ref = rest[:2] if shared else (None, None)
        dq_ref, dk_ref, dv_ref, kw_ref, vw_ref, dkw_ref, dvw_ref, st_ref, bias_ref, kcar_ref, vcar_ref = rest[2 if shared else 0:]
        n = pl.program_id(1)

        @pl.when(n == 0)
        def _():
            dkw_ref[...] = jnp.zeros_like(dkw_ref)
            dvw_ref[...] = jnp.zeros_like(dvw_ref)

        @pl.when(n > 0)
        def _():
            dkw_ref[0:SUPER, :] = dkw_ref[SUPER:, :]
            dvw_ref[0:SUPER, :] = dvw_ref[SUPER:, :]
            dkw_ref[SUPER:, :] = jnp.zeros((SUPER, LANES), F32)
            dvw_ref[SUPER:, :] = jnp.zeros((SUPER, LANES), F32)

        @pl.when(n < ns)
        def _():
            kw_ref[0:SUPER, :] = kp_ref[...]
            kw_ref[SUPER:, :] = kc_ref[...]
            vw_ref[0:SUPER, :] = vp_ref[...]
            vw_ref[SUPER:, :] = vc_ref[...]
            prod = do_ref[...] * o_ref[...].astype(F32)
            lane = lax.broadcasted_iota(jnp.int32, prod.shape, 1)
            zero = jnp.zeros((SUPER, LANES), F32)
            st_ref[0] = zero + jnp.sum(jnp.where(lane < HEAD_DIM, prod, 0.0), axis=-1, keepdims=True)
            st_ref[1] = zero + jnp.sum(jnp.where(lane >= HEAD_DIM, prod, 0.0), axis=-1, keepdims=True)
            lse = lse_ref[...]
            swapped = pltpu.roll(lse, HEAD_DIM, 1)
            st_ref[2] = jnp.where(lane < HEAD_DIM, lse, swapped)
            st_ref[3] = jnp.where(lane >= HEAD_DIM, lse, swapped)
            dq_ref[...] = jnp.zeros_like(dq_ref)

            @pl.when(n == 0)
            def _():
                _fill_bias_t(sl_ref, bias_ref)
                kcar_ref[...] = jnp.zeros_like(kcar_ref)
                vcar_ref[...] = jnp.zeros_like(vcar_ref)

            for gi, dil in enumerate(DILATIONS):

                def block(idx, carry, gi=gi, dil=dil):
                    r, b = idx % dil, idx // dil
                    qs = b * (ATT_BLOCK * dil) + r
                    ks = SUPER + (b - 1) * (ATT_BLOCK * dil) + r
                    first = jnp.logical_and(n == 0, b == 0).astype(jnp.int32)
                    rows = pl.ds(qs, ATT_BLOCK, stride=dil)
                    keys = pl.ds(ks, 2 * ATT_BLOCK, stride=dil)
                    q2 = _stack_heads(q_ref[rows, :])
                    do2 = _stack_heads(do_ref[rows, :])
                    kb, vb = _strided_keys(dil, r, ks, kc_ref, vc_ref, kw_ref, vw_ref, kcar_ref, vcar_ref)
                    dd = _rows_to_lanes(st_ref[0, rows, :], st_ref[1, rows, :])
                    lse_b = _rows_to_lanes(st_ref[2, rows, :], st_ref[3, rows, :])
                    ps, dss = [], []
                    for half in range(2):
                        hk = slice(half * ATT_BLOCK, (half + 1) * ATT_BLOCK)
                        p = jnp.exp(_dot_nt(kb[hk], q2) + bias_ref[2 * gi + first, hk, :] - lse_b)
                        dss.append((p * (_dot_nt(vb[hk], do2) - dd)).astype(BF16))
                        ps.append(p.astype(BF16))
                    p, ds = jnp.concatenate(ps, axis=0), jnp.concatenate(dss, axis=0)
                    dvw_ref[keys, :] += _dot(p, do2)
                    dkw_ref[keys, :] += _dot(ds, q2)
                    dq_ref[rows, :] += _unstack_heads(_dot_tn(ds, kb)) * q_scale
                    return carry

                lax.fori_loop(0, SUPER // ATT_BLOCK, block, 0, unroll=ATT_UNROLL_BWD)

        dk_ref[...] = dkw_ref[0:SUPER, :] + dki_ref[...] if shared else dkw_ref[0:SUPER, :]
        dv_ref[...] = dvw_ref[0:SUPER, :] + dvi_ref[...] if shared else dvw_ref[0:SUPER, :]

    last = ns - 1
    cur = lambda off: pl.BlockSpec((None, SUPER, LANES), lambda h, n: (h + off, jnp.minimum(n, last), 0))
    prev = lambda off: pl.BlockSpec((None, SUPER, LANES), lambda h, n: (h + off, jnp.clip(n - 1, 0, last), 0))
    nat = pl.BlockSpec((SUPER, LANES), lambda h, n: (jnp.minimum(n, last), h))
    late = pl.BlockSpec((None, SUPER, LANES), lambda h, n: (h, jnp.maximum(n - 1, 0), 0))
    dq, dk, dv = _call(
        body, name=name, grid=(hp, ns + 1),
        in_specs=[pl.BlockSpec((None, 2, 2 * ATT_BLOCK), lambda h, n: (h, 0, 0)), cur(0), cur(0), prev(0), cur(hp), prev(hp),
                  nat, cur(0), nat] + ([late, late] if shared else []),
        out_specs=[cur(0), late, late],
        out_shape=[_sds((hp, t, LANES), F32)] * 3,
        scratch_shapes=[pltpu.VMEM((2 * SUPER, LANES), F32)] * 4 + [
            pltpu.VMEM((4, SUPER, LANES), F32), pltpu.VMEM((2 * len(DILATIONS), 2 * ATT_BLOCK, 2 * ATT_BLOCK), F32)] + [
                pltpu.VMEM((DILATIONS[-1], ATT_BLOCK, LANES), BF16)] * 2,
    )(slopes, q, kv, kv, kv, kv, o, lse, d_o, *((dk_in, dv_in) if shared else ()))
    return dq, dk, dv


def _loss_head(y, target, raw, g, name):
    t, d = y.shape
    tm = ROW_TILE

    def body(y_ref, t_ref, raw_ref, g_ref, sq_ref, dy_ref, draw_ref, dg_ref):
        i = pl.program_id(0)
        err = y_ref[...] - t_ref[...]
        dy = err * (1.0 / d)
        dy_ref[...] = dy
        draw, dg = _rms_bwd(raw_ref[...], g_ref[...], dy)
        draw_ref[...] = draw.astype(BF16)
        sq = jnp.zeros((8, LANES), F32) + jnp.sum(err * err)

        @pl.when(i == 0)
        def _():
            sq_ref[...] = sq
            dg_ref[...] = dg

        @pl.when(i > 0)
        def _():
            sq_ref[...] += sq
            dg_ref[...] += dg

    row = pl.BlockSpec((tm, d), lambda i: (i, 0))
    vec = pl.BlockSpec((1, d), lambda i: (0, 0))
    return _call(
        body, name=name, grid=(t // tm,), in_specs=[row, row, row, vec],
        out_specs=[pl.BlockSpec((8, LANES), lambda i: (0, 0)), row, row, vec],
        out_shape=[_sds((8, LANES), F32), _sds((t, d), F32), _sds((t, d), BF16), _sds((1, d), F32)])(y, target, raw, g)


def _bwd_matmul_norms(a_specs, a_args, a_tile, n_steps, w_spec, w_arg, w_mat, xa, ga, resid, xb, gb, name,
                      w_transposed=False):
    t, d = xa.shape
    tm = BWD_ROW_TILE
    na = len(a_specs)
    second = xb is not None
    per = 4 if n_steps % 4 == 0 else 1
    n_steps //= per

    def blocks_of(spec, k):
        return pl.BlockSpec(spec.block_shape, lambda i, j: spec.index_map(i, per * j + k))

    def body(*refs):
        a_refs, w_refs = refs[:per * na], refs[per * na:per * na + per]
        xa_ref, ga_ref, res_ref = refs[per * na + per:per * na + per + 3]
        rest = refs[per * na + per + 3:]
        if second:
            xb_ref, gb_ref, dx_ref, d2_ref, dga_ref, dgb_ref, acc_ref = rest
        else:
            dx_ref, dga_ref, acc_ref = rest
        i, j = pl.program_id(0), pl.program_id(1)
        part = None
        for k in range(per):
            term = (_dot if w_transposed else _dot_nt)(a_tile(per * j + k, *a_refs[k * na:(k + 1) * na]), w_mat(w_refs[k]))
            part = term if part is None else part + term

        @pl.when(j == 0)
        def _():
            acc_ref[...] = part

        @pl.when(j > 0)
        def _():
            acc_ref[...] += part

        @pl.when(j == n_steps - 1)
        def _():
            da, dga = _rms_bwd(xa_ref[...], ga_ref[...], acc_ref[...])
            dx = res_ref[...] + da
            dx_ref[...] = dx
            if second:
                d2, dgb = _rms_bwd(xb_ref[...], gb_ref[...], dx)
                d2_ref[...] = d2.astype(BF16)

            @pl.when(i == 0)
            def _():
                dga_ref[...] = dga
                if second:
                    dgb_ref[...] = dgb

            @pl.when(i > 0)
            def _():
                dga_ref[...] += dga
                if second:
                    dgb_ref[...] += dgb

    row = pl.BlockSpec((tm, d), lambda i, j: (i, 0))
    vec = pl.BlockSpec((1, d), lambda i, j: (0, 0))
    in_specs = [blocks_of(sp, k) for k in range(per) for sp in a_specs] + [blocks_of(w_spec, k) for k in range(per)]
    in_specs += [row, vec, row]
    args = list(a_args) * per + [w_arg] * per + [xa, ga, resid]
    if second:
        in_specs += [row, vec]
        args += [xb, gb]
        out_specs = [row, row, vec, vec]
        out_shape = [_sds((t, d), F32), _sds((t, d), BF16), _sds((1, d), F32), _sds((1, d), F32)]
    else:
        out_specs = [row, vec]
        out_shape = [_sds((t, d), F32), _sds((1, d), F32)]
    return _call(body, name=name, grid=(t // tm, n_steps), in_specs=in_specs, out_specs=out_specs,
                 out_shape=out_shape, scratch_shapes=[pltpu.VMEM((tm, d), F32)])(*args)


def _heads_to_rows(*refs):
    hp = refs[0].shape[0]
    cols = []
    for p in range(hp):
        v = refs[0][p]
        for r in refs[1:]:
            v = v + r[p]
        cols.append(v)
    return jnp.concatenate(cols, axis=-1).astype(BF16)


def _matmul_nt_rows(a, wg, layer, out_dtype, name):
    t, d = a.shape
    tm = ROW_TILE

    def body(a_ref, w_ref, o_ref):
        o_ref[...] = _dot_nt(a_ref[...], w_ref[...].reshape(d, d)).astype(out_dtype)

    row = pl.BlockSpec((tm, d), lambda i: (i, 0))
    return _call(body, name=name, grid=(t // tm,),
                 in_specs=[row, pl.BlockSpec((N_DEV, None, d // N_DEV, d), lambda i: (0, layer, 0, 0))],
                 out_specs=row, out_shape=_sds((t, d), out_dtype))(a, wg)


def _swiglu_bwd(d_ff, wg, layer, gate, up, name):
    t, d = d_ff.shape
    fc = gate.shape[-1]
    rows = wg.shape[2]
    tm = BIG_ROW_TILE

    def body(df_ref, w_ref, g_ref, u_ref, dh_ref):
        w = w_ref[...].reshape(2 * rows, d)
        for r0 in range(0, tm, SWIGLU_ROWS):
            rs = slice(r0, r0 + SWIGLU_ROWS)
            da = _dot_nt(df_ref[rs, :], w)
            gate, up = g_ref[rs, :].astype(F32), u_ref[rs, :].astype(F32)
            sig = jax.nn.sigmoid(gate)
            dh_ref[0, rs, :] = (da * up * (sig * (1.0 + gate * (1.0 - sig)))).astype(BF16)
            dh_ref[1, rs, :] = (da * (gate * sig)).astype(BF16)

    return _call(
        body, name=name, grid=(t // tm, 4),
        in_specs=[pl.BlockSpec((tm, d), lambda i, c: (i, 0)),
                  pl.BlockSpec((2, None, rows, d), lambda i, c: (c, layer, 0, 0)),
                  pl.BlockSpec((None, tm, fc), lambda i, c: (c, i, 0)),
                  pl.BlockSpec((None, tm, fc), lambda i, c: (c, i, 0))],
        out_specs=pl.BlockSpec((None, 2, tm, fc), lambda i, c: (c, 0, i, 0)),
        out_shape=_sds((4, 2, t, fc), BF16))(d_ff, wg, gate, up)


def _conv_bwd(p, d_z, cw, name):
    t, d3 = p.shape
    d = d3 // 3
    tm = ROW_TILE
    hb = tm // HALO
    nt = t // tm

    def body(p_ref, prev_ref, next_ref, dz_ref, dzn_ref, cw_ref, dp_ref, dcw_ref):
        i = pl.program_id(0)
        b = p_ref[:, 0:d].astype(F32)
        c = p_ref[:, d:2 * d].astype(F32)
        h = p_ref[:, 2 * d:3 * d].astype(F32)
        u = c * h
        hu = prev_ref[:, d:2 * d].astype(F32) * prev_ref[:, 2 * d:3 * d].astype(F32) * (i > 0).astype(F32)
        u1, u2 = _shift_down(u, hu, 1, tm), _shift_down(u, hu, 2, tm)
        uc = cw_ref[2:3, :] * u + cw_ref[1:2, :] * u1 + cw_ref[0:1, :] * u2
        dz = dz_ref[...].astype(F32)
        duc = dz * b
        dn = dzn_ref[...].astype(F32) * next_ref[:, 0:d].astype(F32) * (i < nt - 1).astype(F32)
        du = cw_ref[2:3, :] * duc + cw_ref[1:2, :] * _shift_up(duc, dn, 1, tm) + cw_ref[0:1, :] * _shift_up(duc, dn, 2, tm)
        dp_ref[:, 0:d] = (dz * uc).astype(BF16)
        dp_ref[:, d:2 * d] = (du * h).astype(BF16)
        dp_ref[:, 2 * d:3 * d] = (du * c).astype(BF16)
        dcw = jnp.concatenate([jnp.sum(duc * u2, axis=0, keepdims=True), jnp.sum(duc * u1, axis=0, keepdims=True),
                               jnp.sum(duc * u, axis=0, keepdims=True)], axis=0)

        @pl.when(i == 0)
        def _():
            dcw_ref[...] = dcw

        @pl.when(i > 0)
        def _():
            dcw_ref[...] += dcw

    last_halo = t // HALO - 1
    return _call(
        body, name=name, grid=(nt,),
        in_specs=[pl.BlockSpec((tm, d3), lambda i: (i, 0)),
                  pl.BlockSpec((HALO, d3), lambda i: (jnp.maximum(i * hb - 1, 0), 0)),
                  pl.BlockSpec((HALO, d3), lambda i: (jnp.minimum((i + 1) * hb, last_halo), 0)),
                  pl.BlockSpec((tm, d), lambda i: (i, 0)),
                  pl.BlockSpec((HALO, d), lambda i: (jnp.minimum((i + 1) * hb, last_halo), 0)),
                  pl.BlockSpec((3, d), lambda i: (0, 0))],
        out_specs=[pl.BlockSpec((tm, d3), lambda i: (i, 0)), pl.BlockSpec((3, d), lambda i: (0, 0))],
        out_shape=[_sds((t, d3), BF16), _sds((3, d), F32)])(p, p, p, d_z, d_z, cw)


def _grad_weight(a_specs, a_args, a_tile, b_specs, b_args, b_tile, n_out, acc_shape, out_spec, out_shape, t, name,
                 a_transposed=False):
    tt = GRAD_ROW_TILE
    na, nb = len(a_specs), len(b_specs)

    def body(*refs):
        a_refs, b_refs = refs[:na], refs[na:na + nb]
        o_ref, acc_ref = refs[na + nb:]
        s = pl.program_id(1)
        a, b = a_tile(pl.program_id(0), *a_refs), b_tile(pl.program_id(0), *b_refs)
        part = _dot(a, b) if a_transposed else _dot_tn(a, b)

        @pl.when(s == 0)
        def _():
            acc_ref[...] = part

        @pl.when(s > 0)
        def _():
            acc_ref[...] += part

        @pl.when(s == t // tt - 1)
        def _():
            acc = acc_ref[...].astype(BF16)
            if o_ref.shape[-1] == acc.shape[-1]:
                o_ref[...] = acc.reshape(o_ref.shape)
            else:
                for k in range(o_ref.shape[0]):
                    o_ref[k] = acc[:, k * o_ref.shape[-1]:(k + 1) * o_ref.shape[-1]]

    return _call(body, name=name, grid=(n_out, t // tt), in_specs=list(a_specs) + list(b_specs), out_specs=out_spec,
                 out_shape=out_shape, scratch_shapes=[pltpu.VMEM(acc_shape, F32)])(*a_args, *b_args)


def _ident(*args):
    return args[-1][...]


def _heads_tile(j, *refs):
    return _heads_to_rows(*refs)


def kernel(x, norm_g, conv_in_w, conv_w, conv_out_w, kv_norm_g, kv_w, q_w, o_w, ffn_in_w, ffn_out_w, loss_target, m_norm_g, m_conv_in_w, m_conv_w, m_conv_out_w, m_kv_norm_g, m_kv_w, m_q_w, m_o_w, m_ffn_in_w, m_ffn_out_w, v_norm_g, v_conv_in_w, v_conv_w, v_conv_out_w, v_kv_norm_g, v_kv_w, v_q_w, v_o_w, v_ffn_in_w, v_ffn_out_w):
    x0 = x[0]
    target = loss_target[0]
    t, d = x0.shape
    depth = norm_g.shape[0]
    n_a = conv_in_w.shape[0]
    n_b = q_w.shape[0]
    hp = d // LANES
    tm, tg = BWD_ROW_TILE, GRAD_ROW_TILE
    assert t % SUPER == 0 and d % LANES == 0 and depth == n_a + n_b
    dev = 4 * lax.axis_index("x") + 2 * lax.axis_index("y") + lax.axis_index("c")

    n_small = 4 * depth + 3 * n_a
    small_rows = -(-(n_small + 1) // 8) * 8
    small_local = jnp.concatenate([norm_g.reshape(4 * depth, -1), conv_w.reshape(3 * n_a, -1),
                                   jnp.zeros((small_rows - n_small, norm_g.shape[-1]), F32)], axis=0)
    swap = lambda a: jnp.swapaxes(a, 1, 2)
    big = {"conv_in_w": conv_in_w, "conv_out_w": conv_out_w, "kv_w": kv_w[None], "q_w": q_w, "o_w": o_w,
           "ffn_in_w": swap(ffn_in_w), "ffn_out_w": ffn_out_w}
    names = list(big)

    def group(layer):
        if layer < n_a:
            return [("conv_in_w", layer), ("conv_out_w", layer), ("ffn_in_w", layer), ("ffn_out_w", layer)]
        j = layer - n_a
        return ([("kv_w", 0)] if j == 0 else []) + [("q_w", j), ("o_w", j), ("ffn_in_w", layer), ("ffn_out_w", layer)]

    slot = dev.astype(jnp.int32).reshape(1)
    is_ffn = lambda key: key[0].startswith("ffn")
    first_keys = [key for key in group(0) if not is_ffn(key)]
    first = _all_gather([small_local] + [_cast_layer(big[k], i, None, f"cast_{k}_{i}") for k, i in first_keys], "gather_weights")
    small_all = first[0].transpose(1, 0, 2).reshape(small_rows, d)
    wl = {key: a[:, None] for key, a in zip(first_keys, first[1:])}

    def gather_start(keys, after, tag):
        lands = [_cast_layer(big[k], i, slot, f"cast_{k}_{i}") for k, i in keys]
        send_sems, recv_sems, _, lands, tok = _send_start([], lands, after, "gather_start" + tag)
        return (keys, tag, send_sems, recv_sems, lands), tok[0, 0]

    def gather_wait(flight, after):
        keys, tag, send_sems, recv_sems, lands = flight
        _, lands = _send_wait(send_sems, recv_sems, [], lands, after, "gather_wait" + tag)
        wl.update({key: a[:, None] for key, a in zip(keys, lands)})

    in_flight, token = gather_start([key for key in group(0) if is_ffn(key)], small_all, "_l0")
    W = lambda k, i: (wl[(k, i)], 0)
    gain = lambda layer, k: small_all[4 * layer + k][None]
    taps = lambda layer: small_all[4 * depth + 3 * layer: 4 * depth + 3 * layer + 3]
    g_kv = kv_norm_g[None]
    slopes = _alibi_slopes(d // HEAD_DIM)
    fc = big["ffn_in_w"].shape[-2]
    cb = big["conv_in_w"].shape[-1]
    kvb = big["kv_w"].shape[-1]
    q_scale = HEAD_DIM ** -0.5

    saved = []
    kv = kvn_t = None
    xs = x0
    for layer in range(depth):
        tag = f"_l{layer}"
        g0 = g2 = 0.0
        if layer == 0:
            g0 = token
        else:
            gather_wait(in_flight, xs)
            if layer + 1 < depth:
                in_flight, g0 = gather_start(group(layer + 1), xs, f"_l{layer + 1}")
        s = {"x_in": xs}
        g0 = gain(layer, 0) + g0
        if layer < n_a:
            s["p"], s["xn_t"] = _norm_matmul_cols(xs, g0, *W("conv_in_w", layer), "cols", "conv_in" + tag)
            s["z"] = _conv_fwd(s["p"], taps(layer), "conv" + tag)
            s["mix"], x_mid = _matmul_norm_residual(s["z"][None], *W("conv_out_w", layer), gain(layer, 1), xs, "conv_out" + tag)
        else:
            j = layer - n_a
            if kv is None:
                kv, kvn_t = _norm_matmul_cols(xs, g_kv, *W("kv_w", 0), "heads", "kv_proj")
            s["q"], s["xn"] = _norm_matmul_heads(xs, g0, *W("q_w", j), q_scale, "q_proj" + tag)
            s["o"], s["lse"] = _attention_fwd(s["q"], kv, slopes, "attention" + tag)
            s["mix"], x_mid = _matmul_norm_residual(s["o"][None], *W("o_w", j), gain(layer, 1), xs, "o_proj" + tag)
        s["x_mid"] = x_mid
        if layer == 0:
            gather_wait(in_flight, x_mid)
            in_flight, g2 = gather_start(group(1), x_mid, "_l1")
        s["gate"], s["up"], s["a"], s["fn"] = _ffn_in_swiglu(x_mid, gain(layer, 2) + g2, *W("ffn_in_w", layer), "ffn_in" + tag)
        s["ff"], xs = _matmul_norm_residual(s["a"], *W("ffn_out_w", layer), gain(layer, 3), x_mid, "ffn_out" + tag)
        saved.append(s)

    last = saved[-1]
    sq, dx_out, d_ff, dg3 = _loss_head(xs, target, last["ff"], gain(depth - 1, 3), "loss_head")
    loss = lax.psum(sq[0, 0] * (0.5 / d), ("x", "y", "c"))

    dgain = {(depth - 1, 3): dg3}
    dtaps = {}
    grads = {k: [None] * big[k].shape[0] for k in names}
    dkv_parts = []
    scattering = []

    def scatter_start(keys, tag):
        parts = [grads[k][i] for k, i in keys]
        zones = [lax.empty(p.shape, p.dtype) for p in parts]
        send_sems, recv_sems, parts, zones, tok = _send_start(parts, zones, small_all, "scatter_start" + tag)
        scattering.append((keys, tag, send_sems, recv_sems, parts, zones))
        return tok[0, 0]

    for layer in reversed(range(depth)):
        tag = f"_l{layer}"
        s = saved[layer]
        dh = _swiglu_bwd(d_ff, *W("ffn_out_w", layer), s["gate"], s["up"], "swiglu_bwd" + tag)
        rows_out = big["ffn_out_w"].shape[1]
        grads["ffn_out_w"][layer] = _grad_weight(
            [pl.BlockSpec((None, tg, fc), lambda c, i: (c, i, 0))], [s["a"]], _ident,
            [pl.BlockSpec((tg, d), lambda c, i: (i, 0))], [d_ff], _ident,
            4, (fc, d), pl.BlockSpec((2, rows_out, d), lambda c, i: (c, 0, 0)), _sds((N_DEV, rows_out, d), BF16), t,
            "grad_ffn_out" + tag)
        grads["ffn_in_w"][layer] = _grad_weight(
            [pl.BlockSpec((None, None, tg, fc), lambda j, i: (j % 4, j // 4, i, 0))], [dh], _ident,
            [pl.BlockSpec((tg, d), lambda j, i: (i, 0))], [s["fn"]], _ident,
            N_DEV, (fc, d), pl.BlockSpec((None, fc, d), lambda j, i: (j, 0, 0)), _sds((N_DEV, fc, d), BF16), t,
            "grad_ffn_in" + tag)
        tok = scatter_start([("ffn_in_w", layer), ("ffn_out_w", layer)], "_ffn" + tag)
        dx_mid, d_mix, dg2, dg1 = _bwd_matmul_norms(
            [pl.BlockSpec((None, None, tm, fc), lambda i, j: (j % 4, j // 4, i, 0))], [dh], _ident, N_DEV,
            pl.BlockSpec((None, None, fc, d), lambda i, j: (j, 0, 0, 0)), W("ffn_in_w", layer)[0], _ident,
            s["x_mid"], gain(layer, 2) + tok, dx_out, s["mix"], gain(layer, 1), "ffn_in_bwd" + tag, w_transposed=True)
        dgain[(layer, 2)], dgain[(layer, 1)] = dg2, dg1
        full_rows = pl.BlockSpec((N_DEV, d // N_DEV, d), lambda j, i: (0, 0, 0))
        rows_w = lambda wname, idx: (pl.BlockSpec((N_DEV, None, d // N_DEV, d), lambda i, j: (0, 0, 0, 0)), W(wname, idx)[0],
                                     lambda w_ref: w_ref[...].reshape(d, d))
        if layer < n_a:
            d_z = _matmul_nt_rows(d_mix, *W("conv_out_w", layer), BF16, "conv_out_bwd" + tag)
            grads["conv_out_w"][layer] = _grad_weight(
                [pl.BlockSpec((tg, d), lambda j, i: (i, 0))], [s["z"]], _ident,
                [pl.BlockSpec((tg, d), lambda j, i: (i, 0))], [d_mix], _ident,
                1, (d, d), full_rows, _sds((N_DEV, d // N_DEV, d), BF16), t, "grad_conv_out" + tag)
            d_p, dtaps[layer] = _conv_bwd(s["p"], d_z, taps(layer), "conv_bwd" + tag)
            grads["conv_in_w"][layer] = _grad_weight(
                [pl.BlockSpec((d, tg), lambda j, i: (0, i))], [s["xn_t"]], _ident,
                [pl.BlockSpec((tg, 2 * cb), lambda j, i: (i, j))], [d_p], _ident,
                N_DEV // 2, (d, 2 * cb), pl.BlockSpec((2, d, cb), lambda j, i: (j, 0, 0)), _sds((N_DEV, d, cb), BF16), t,
                "grad_conv_in" + tag, a_transposed=True)
            a_specs, a_args, a_tile, n_steps = [pl.BlockSpec((tm, 4 * cb), lambda i, j: (i, j))], [d_p], _ident, N_DEV // 4
            w_spec = pl.BlockSpec((4, None, d, cb), lambda i, j: (j, 0, 0, 0))
            w_arg = W("conv_in_w", layer)[0]
            w_mat = lambda w_ref: jnp.concatenate([w_ref[k] for k in range(4)], axis=1)
            resid = dx_mid
        else:
            j_b = layer - n_a
            d_o = _matmul_nt_rows(d_mix, *W("o_w", j_b), F32, "o_proj_bwd" + tag)
            grads["o_w"][j_b] = _grad_weight(
                [pl.BlockSpec((tg, d), lambda j, i: (i, 0))], [s["o"]], _ident,
                [pl.BlockSpec((tg, d), lambda j, i: (i, 0))], [d_mix], _ident,
                1, (d, d), full_rows, _sds((N_DEV, d // N_DEV, d), BF16), t, "grad_o" + tag)
            dk_in, dv_in = dkv_parts[0] if dkv_parts else (None, None)
            dq, dk, dv = _attention_bwd(s["q"], kv, s["o"], s["lse"], d_o, dk_in, dv_in, slopes, q_scale, "attention_bwd" + tag)
            dkv_parts = [(dk, dv)]
            heads_spec = pl.BlockSpec((hp, tg, LANES), lambda j, i: (0, i, 0))
            grads["q_w"][j_b] = _grad_weight(
                [pl.BlockSpec((tg, d), lambda j, i: (i, 0))], [s["xn"]], _ident,
                [heads_spec], [dq], _heads_tile,
                1, (d, d), full_rows, _sds((N_DEV, d // N_DEV, d), BF16), t, "grad_q" + tag)
            a_specs, a_args, a_tile, n_steps = [pl.BlockSpec((hp, tm, LANES), lambda i, j: (0, i, 0))], [dq], _heads_tile, 1
            w_spec, w_arg, w_mat = rows_w("q_w", j_b)
            resid = dx_mid
            if layer == n_a:
                pieces = kvb // LANES
                halves = []
                for src in (0, 1):
                    halves.append([part[src] for part in dkv_parts])
                n_half = len(dkv_parts)
                kv_args = [arr for src in (0, 1) for arr in halves[src]]

                def kv_block(src, j):
                    return jnp.where((j // 4) == src, j % 4, 0)

                def kv_tile(j, *refs):
                    keys = _heads_to_rows(*refs[:n_half])
                    vals = _heads_to_rows(*refs[n_half:])
                    return jnp.where(j < 4, keys, vals)

                kv_specs = [pl.BlockSpec((pieces, tm, LANES), functools.partial(lambda i, j, src: (kv_block(src, j), i, 0), src=src))
                            for src in (0, 1) for _ in range(n_half)]
                resid, dgain["kv"] = _bwd_matmul_norms(
                    kv_specs, kv_args, kv_tile, N_DEV,
                    pl.BlockSpec((None, None, d, kvb), lambda i, j: (j, 0, 0, 0)), W("kv_w", 0)[0], _ident,
                    s["x_in"], g_kv, dx_mid, None, None, "kv_proj_bwd")
                kv_b_specs = [pl.BlockSpec((pieces, tg, LANES), functools.partial(lambda j, i, src: (kv_block(src, j), i, 0), src=src))
                              for src in (0, 1) for _ in range(n_half)]
                grads["kv_w"][0] = _grad_weight(
                    [pl.BlockSpec((d, tg), lambda j, i: (0, i))], [kvn_t], _ident,
                    kv_b_specs, kv_args, kv_tile,
                    N_DEV, (d, kvb), pl.BlockSpec((None, d, kvb), lambda j, i: (j, 0, 0)), _sds((N_DEV, d, kvb), BF16), t,
                    "grad_kv", a_transposed=True)
        tok = scatter_start([key for key in group(layer) if not key[0].startswith("ffn")], "_mix" + tag)
        if layer > 0:
            prev = saved[layer - 1]
            dx_out, d_ff, dg0, dg3p = _bwd_matmul_norms(
                a_specs, a_args, a_tile, n_steps, w_spec, w_arg, w_mat,
                s["x_in"], gain(layer, 0) + tok, resid, prev["ff"], gain(layer - 1, 3), "mixer_in_bwd" + tag)
            dgain[(layer, 0)], dgain[(layer - 1, 3)] = dg0, dg3p
        else:
            grad_x, dg0 = _bwd_matmul_norms(
                a_specs, a_args, a_tile, n_steps, w_spec, w_arg, w_mat,
                s["x_in"], gain(layer, 0), resid, None, None, "mixer_in_bwd" + tag)
            dgain[(layer, 0)] = dg0

    small_grad = jnp.concatenate(
        [dgain[(layer, k)] for layer in range(depth) for k in range(4)] + [dtaps[layer] for layer in range(n_a)]
        + [dgain["kv"]] + [jnp.zeros((small_rows - n_small - 1, d), F32)], axis=0)
    small_grads_all = _all_gather([small_grad], "gather_small_grads")[0]
    lo = dev * (d // N_DEV)

    def pack(ng, cwp, kvg):
        rows = jnp.concatenate([ng.reshape(4 * depth, -1), cwp.reshape(3 * n_a, -1)], axis=0)
        z = lax.dynamic_update_slice(jnp.zeros((small_rows, d), F32), rows, (0, lo))
        return lax.dynamic_update_slice(z, kvg[None], (n_small, 0))

    w_small = lax.dynamic_update_slice(small_all, g_kv, (n_small, 0))
    m_small, v_small = pack(m_norm_g, m_conv_w, m_kv_norm_g), pack(v_norm_g, v_conv_w, v_kv_norm_g)
    sm = _small_adamw(small_grads_all, w_small, m_small, v_small, "adamw_small")

    def unpack(a):
        mine = lax.dynamic_slice(a, (0, lo), (small_rows, d // N_DEV))
        return (mine[:4 * depth].reshape(norm_g.shape), mine[4 * depth:n_small].reshape(conv_w.shape), a[n_small])

    small_out = [unpack(a) for a in sm]

    moments = {"conv_in_w": (m_conv_in_w, v_conv_in_w), "conv_out_w": (m_conv_out_w, v_conv_out_w),
               "kv_w": (m_kv_w[None], v_kv_w[None]), "q_w": (m_q_w, v_q_w), "o_w": (m_o_w, v_o_w),
               "ffn_in_w": (swap(m_ffn_in_w), swap(v_ffn_in_w)), "ffn_out_w": (m_ffn_out_w, v_ffn_out_w)}
    landed = {k: [None] * big[k].shape[0] for k in names}
    for keys, tag, send_sems, recv_sems, parts, zones in scattering:
        parts, zones = _send_wait(send_sems, recv_sems, parts, zones, grad_x, "scatter_wait" + tag)
        for (k, i), part, zone in zip(keys, parts, zones):
            landed[k][i] = (part, zone)
    res = {k: _sum_adamw(slot, [p for p, _ in landed[k]], [z for _, z in landed[k]], big[k], moments[k][0], moments[k][1],
                         "adamw_" + k) for k in names}

    def big_out(k, which):
        out = res[k][which]
        return out[0] if k == "kv_w" else swap(out) if k == "ffn_in_w" else out

    out_names = ["norm_g", "conv_in_w", "conv_w", "conv_out_w", "kv_norm_g", "kv_w", "q_w", "o_w", "ffn_in_w", "ffn_out_w"]
    small_pos = {"norm_g": 0, "conv_w": 1, "kv_norm_g": 2}
    outs = [loss, grad_x[None]]
    for which in range(4):
        for k in out_names:
            outs.append(small_out[which][small_pos[k]] if k in small_pos else big_out(k, which))
    return tuple(outs)
```

```python
import functools
import math

import numpy as np
import jax
import jax.numpy as jnp
from jax import lax
from jax.experimental import pallas as pl
from jax.experimental.pallas import tpu as pltpu

F32 = jnp.float32
BF16 = jnp.bfloat16

N_DEV = 8
RMS_EPS = 1e-6
HEAD_DIM = 64
LANES = 128
ATT_BLOCK = 128
DILATIONS = (1, 4, 16)
SUPER = ATT_BLOCK * DILATIONS[-1]
NEG = -1e30
ATT_UNROLL_FWD = 16
ATT_UNROLL_BWD = 16

ADAM_LR, ADAM_B1, ADAM_B2, ADAM_EPS, ADAM_WD, ADAM_STEP = 0.001, 0.9, 0.999, 1e-08, 0.01, 10

ROW_TILE = 512
BIG_ROW_TILE = 1024
SWIGLU_ROWS = 256
GRAD_ROW_TILE = 2048
BWD_ROW_TILE = 512
MESH = pl.DeviceIdType.MESH


def _call(body, *, name, grid=None, in_specs=None, out_specs=None, out_shape=None, scratch_shapes=(), prefetch=False,
          **params):
    cp = pltpu.CompilerParams(**params) if params else None
    if prefetch:
        spec = pltpu.PrefetchScalarGridSpec(num_scalar_prefetch=1, grid=grid, in_specs=in_specs, out_specs=out_specs,
                                            scratch_shapes=list(scratch_shapes))
        return pl.pallas_call(body, name=name, grid_spec=spec, out_shape=out_shape, compiler_params=cp)
    kwargs = {k: v for k, v in (("grid", grid), ("in_specs", in_specs), ("out_specs", out_specs)) if v is not None}
    return pl.pallas_call(body, name=name, out_shape=out_shape, scratch_shapes=list(scratch_shapes),
                          compiler_params=cp, **kwargs)


def _sds(shape, dtype):
    return jax.ShapeDtypeStruct(tuple(shape), dtype)


def _rms(x, g):
    r = lax.rsqrt(jnp.mean(x * x, axis=-1, keepdims=True) + RMS_EPS)
    return x * r * g


def _rms_bwd(x, g, dy):
    r = lax.rsqrt(jnp.mean(x * x, axis=-1, keepdims=True) + RMS_EPS)
    xh = x * r
    dxh = dy * g
    dx = r * (dxh - xh * jnp.mean(dxh * xh, axis=-1, keepdims=True))
    return dx, jnp.sum(dy * xh, axis=0, keepdims=True)


def _dot(a, b):
    return jnp.dot(a, b, preferred_element_type=F32)


def _dot_nt(a, b):
    return lax.dot_general(a, b, (((1,), (1,)), ((), ())), preferred_element_type=F32)


def _dot_tn(a, b):
    return lax.dot_general(a, b, (((0,), (0,)), ((), ())), preferred_element_type=F32)


def _mesh_pos():
    return lax.axis_index("x"), lax.axis_index("y"), lax.axis_index("c")


def _all_gather(arrs, name):
    n = len(arrs)

    def body(*refs):
        ins, outs = refs[:n], refs[n:2 * n]
        send_sems, recv_sems, local_sems = refs[2 * n:]
        x, y, c = _mesh_pos()
        me, sibling = (x, y, c), (x, y, 1 - c)
        chips = [(1 - x, y), (x, 1 - y), (1 - x, 1 - y)]

        def copy(a, k, block, to, src=None):
            dst = outs[a].at[4 * block[0] + 2 * block[1] + block[2]]
            return pltpu.make_async_remote_copy(
                src_ref=dst if src is None else src, dst_ref=dst, send_sem=send_sems.at[a, k],
                recv_sem=recv_sems.at[a, k], device_id=to, device_id_type=MESH)

        started = []
        for a in range(n):
            mine = pltpu.make_async_copy(ins[a], outs[a].at[4 * x + 2 * y + c], local_sems.at[a])
            mine.start()
            started.append(mine)
        first = []
        for a in range(n):
            first.append(copy(a, 0, me, sibling, src=ins[a]))
            first += [copy(a, 1 + j, me, (*chip, c), src=ins[a]) for j, chip in enumerate(chips)]
        for cp in first:
            cp.start()
        passed = []
        for a in range(n):
            for j, chip in enumerate(chips):
                copy(a, 1 + j, (*chip, c), me).wait_recv()
                fwd = copy(a, 4 + j, (*chip, c), sibling)
                fwd.start()
                passed.append(fwd)
        for a in range(n):
            copy(a, 0, sibling, me).wait_recv()
            for j, chip in enumerate(chips):
                copy(a, 4 + j, (*chip, 1 - c), me).wait_recv()
        for cp in first + passed:
            cp.wait_send()
        for cp in started:
            cp.wait()

    any_spec = pl.BlockSpec(memory_space=pl.ANY)
    outs = _call(
        body, name=name, in_specs=[any_spec] * n, out_specs=[any_spec] * n,
        out_shape=[_sds((N_DEV,) + a.shape, a.dtype) for a in arrs],
        scratch_shapes=[pltpu.SemaphoreType.DMA((n, 7)), pltpu.SemaphoreType.DMA((n, 7)), pltpu.SemaphoreType.DMA((n,))],
        has_side_effects=True,
    )(*arrs)
    return list(outs)


HBM_SPEC = pl.BlockSpec(memory_space=pltpu.HBM)
SEM_SPEC = pl.BlockSpec(memory_space=pltpu.SEMAPHORE)
DATAFLOW = pltpu.SideEffectType.DATAFLOW_SIDE_EFFECTING
PEERS = [(dx, dy, dc) for dx in (0, 1) for dy in (0, 1) for dc in (0, 1)][1:]


def _peer(flip):
    x, y, c = _mesh_pos()
    return tuple(1 - v if f else v for v, f in zip((x, y, c), flip))


def _slot(pos):
    return 4 * pos[0] + 2 * pos[1] + pos[2]


def _in_hbm(a):
    return pltpu.with_memory_space_constraint(a, pltpu.HBM)


def _direct_copies(srcs, lands, send_sems, recv_sems, scatter):
    me = _slot(_mesh_pos())
    copies = []
    for a in range(len(lands)):
        for k, flip in enumerate(PEERS):
            peer = _peer(flip)
            src = srcs[a].at[_slot(peer)] if scatter else lands[a].at[me]
            idx = a * len(PEERS) + k
            copies.append(pltpu.make_async_remote_copy(
                src_ref=src, dst_ref=lands[a].at[me], send_sem=send_sems.at[idx], recv_sem=recv_sems.at[idx],
                device_id=peer, device_id_type=MESH))
    return copies


def _send_start(srcs, lands, after, name):
    ns, nl = len(srcs), len(lands)
    scatter = ns > 0

    def body(*refs):
        src_refs, land_refs = refs[:ns], refs[ns:ns + nl]
        send_sems, recv_sems = refs[ns + nl + 1:ns + nl + 3]
        token = refs[-1]
        for cp in _direct_copies(src_refs, land_refs, send_sems, recv_sems, scatter):
            cp.start()
        token[...] = jnp.zeros_like(token)

    sem = pltpu.SemaphoreType.DMA((nl * len(PEERS),))
    outs = pl.pallas_call(
        body, name=name,
        out_shape=(sem, sem) + tuple(pltpu.HBM(a.shape, a.dtype) for a in list(srcs) + list(lands))
        + (_sds((8, LANES), F32),),
        in_specs=[HBM_SPEC] * (ns + nl) + [pl.BlockSpec(memory_space=pl.ANY)],
        out_specs=(SEM_SPEC, SEM_SPEC) + (HBM_SPEC,) * (ns + nl) + (pl.BlockSpec(memory_space=pltpu.VMEM),),
        input_output_aliases={i: 2 + i for i in range(ns + nl)},
        compiler_params=pltpu.CompilerParams(has_side_effects=DATAFLOW),
    )(*[_in_hbm(a) for a in list(srcs) + list(lands)], after)
    send_sems, recv_sems = outs[0], outs[1]
    return send_sems, recv_sems, list(outs[2:2 + ns]), list(outs[2 + ns:2 + ns + nl]), outs[-1]


def _send_wait(send_sems, recv_sems, srcs, lands, after, name):
    ns, nl = len(srcs), len(lands)
    scatter = ns > 0

    def body(*refs):
        src_refs, land_refs = refs[:ns], refs[ns:ns + nl]
        send_sems, recv_sems = refs[ns + nl:ns + nl + 2]
        copies = _direct_copies(src_refs, land_refs, send_sems, recv_sems, scatter)
        for cp in copies:
            cp.wait_send()
        for cp in copies:
            cp.wait_recv()

    outs = pl.pallas_call(
        body, name=name,
        out_shape=tuple(pltpu.HBM(a.shape, a.dtype) for a in list(srcs) + list(lands)),
        in_specs=[HBM_SPEC] * (ns + nl) + [SEM_SPEC, SEM_SPEC, pl.BlockSpec(memory_space=pl.ANY)],
        out_specs=(HBM_SPEC,) * (ns + nl),
        input_output_aliases={i: i for i in range(ns + nl)},
        compiler_params=pltpu.CompilerParams(has_side_effects=DATAFLOW),
    )(*srcs, *lands, send_sems, recv_sems, after)
    return list(outs[:ns]), list(outs[ns:])


def _row_tile(rows, cap=512):
    t = min(rows, cap)
    while rows % t or (t % 16 and t != rows):
        t -= 1
    return t


def _as2d(a):
    return a.reshape(-1, a.shape[-1])


def _cast_layer(w, layer, slot, name):
    _, rows, cols = w.shape
    tr = _row_tile(rows)

    def body(*refs):
        refs[-1][...] = refs[-2][...].astype(BF16)

    if slot is None:
        return _call(body, name=name, grid=(rows // tr,),
                     in_specs=[pl.BlockSpec((None, tr, cols), lambda i: (layer, i, 0))],
                     out_specs=pl.BlockSpec((tr, cols), lambda i: (i, 0)), out_shape=_sds((rows, cols), BF16))(w)
    return _call(body, name=name, grid=(rows // tr,), prefetch=True,
                 in_specs=[pl.BlockSpec((None, tr, cols), lambda i, s: (layer, i, 0))],
                 out_specs=pl.BlockSpec((None, tr, cols), lambda i, s: (s[0], i, 0)),
                 out_shape=_sds((N_DEV, rows, cols), BF16))(slot, w)


def _sum_adamw(slot, parts, lands, w, m, v, name):
    n_l = len(parts)
    _, rows, cols = lands[0].shape
    tr = _row_tile(rows, 128)

    def body(s_ref, *refs):
        p_refs, l_refs = refs[:n_l], refs[n_l:2 * n_l]
        w_ref, m_ref, v_ref, g_ref, d_ref, nm_ref, nv_ref = refs[2 * n_l:]
        for k in range(n_l):
            @pl.when(pl.program_id(0) == k)
            def _(k=k):
                own = p_refs[k][...]
                g = jnp.zeros((tr, cols), F32)
                for j in range(N_DEV):
                    g = g + jnp.where(s_ref[0] == j, own, l_refs[k][j]).astype(F32)
                delta, nm, nv = _adamw_math(w_ref[...], g, m_ref[...], v_ref[...])
                g_ref[...] = g
                d_ref[...] = delta
                nm_ref[...] = nm
                nv_ref[...] = nv

    def own_block(k):
        return pl.BlockSpec((None, tr, cols), lambda l, i, s: (s[0], jnp.where(l == k, i, 0), 0))

    def zone_block(k):
        return pl.BlockSpec((N_DEV, tr, cols), lambda l, i, s: (0, jnp.where(l == k, i, 0), 0))

    lay = pl.BlockSpec((None, tr, cols), lambda l, i, s: (l, i, 0))
    return _call(body, name=name, grid=(n_l, rows // tr), prefetch=True,
                 in_specs=[own_block(k) for k in range(n_l)] + [zone_block(k) for k in range(n_l)] + [lay, lay, lay],
                 out_specs=[lay] * 4, out_shape=[_sds((n_l, rows, cols), F32)] * 4)(slot, *parts, *lands, w, m, v)


def _adamw_math(w, g, m, v):
    m = ADAM_B1 * m + (1.0 - ADAM_B1) * g
    v = ADAM_B2 * v + (1.0 - ADAM_B2) * (g * g)
    m_hat = m / (1.0 - ADAM_B1 ** ADAM_STEP)
    v_hat = v / (1.0 - ADAM_B2 ** ADAM_STEP)
    delta = -ADAM_LR * (m_hat / (jnp.sqrt(v_hat) + ADAM_EPS) + ADAM_WD * w)
    return delta, m, v


def _small_adamw(gathered, w, m, v, name):
    def body(a_ref, w_ref, m_ref, v_ref, g_ref, d_ref, nm_ref, nv_ref):
        g = a_ref[0]
        for k in range(1, N_DEV):
            g = g + a_ref[k]
        delta, nm, nv = _adamw_math(w_ref[...], g, m_ref[...], v_ref[...])
        g_ref[...] = g
        d_ref[...] = delta
        nm_ref[...] = nm
        nv_ref[...] = nv

    return _call(body, name=name, out_shape=[_sds(w.shape, F32)] * 4)(gathered, w, m, v)


def _norm_matmul_cols(x, g, wg, layer, mode, name):
    t, d = x.shape
    nb = wg.shape[-1]
    tm = BIG_ROW_TILE
    per = 2
    pieces = per * nb // LANES

    def body(x_ref, g_ref, w_ref, y_ref, xnt_ref, xn_ref):
        @pl.when(pl.program_id(1) == 0)
        def _():
            xn = _rms(x_ref[...], g_ref[...])
            xn_ref[...] = xn.astype(BF16)
            xnt_ref[...] = xn.T.astype(BF16)

        y = _dot(xn_ref[...], jnp.concatenate([w_ref[k] for k in range(per)], axis=1))
        if mode == "heads":
            for p in range(pieces):
                y_ref[p] = y[:, p * LANES:(p + 1) * LANES]
        else:
            y_ref[...] = y.astype(BF16)

    if mode == "cols":
        y_shape, y_spec = _sds((t, N_DEV * nb), BF16), pl.BlockSpec((tm, per * nb), lambda i, j: (i, j))
    else:
        y_shape = _sds((N_DEV // per * pieces, t, LANES), F32)
        y_spec = pl.BlockSpec((pieces, tm, LANES), lambda i, j: (j, i, 0))
    return _call(
        body, name=name, grid=(t // tm, N_DEV // per),
        in_specs=[pl.BlockSpec((tm, d), lambda i, j: (i, 0)), pl.BlockSpec((1, d), lambda i, j: (0, 0)),
                  pl.BlockSpec((per, None, d, nb), lambda i, j: (j, layer, 0, 0))],
        out_specs=[y_spec, pl.BlockSpec((d, tm), lambda i, j: (0, i))],
        out_shape=[y_shape, _sds((d, t), BF16)], scratch_shapes=[pltpu.VMEM((tm, d), BF16)])(x, g, wg)


def _ffn_in_swiglu(x, g, wg, layer, name):
    t, d = x.shape
    fc = wg.shape[-2]
    tm = BIG_ROW_TILE

    def body(x_ref, g_ref, wg_ref, wu_ref, gate_ref, up_ref, a_ref, xn_ref):
        @pl.when(pl.program_id(1) == 0)
        def _():
            xn_ref[...] = _rms(x_ref[...], g_ref[...]).astype(BF16)

        xn = xn_ref[...]
        gate, up = _dot_nt(xn, wg_ref[...]), _dot_nt(xn, wu_ref[...])
        gate_ref[...] = gate.astype(BF16)
        up_ref[...] = up.astype(BF16)
        a_ref[...] = (gate * jax.nn.sigmoid(gate) * up).astype(BF16)

    chunk = pl.BlockSpec((None, tm, fc), lambda i, c: (c, i, 0))
    return _call(
        body, name=name, grid=(t // tm, 4),
        in_specs=[pl.BlockSpec((tm, d), lambda i, c: (i, 0)), pl.BlockSpec((1, d), lambda i, c: (0, 0)),
                  pl.BlockSpec((None, None, fc, d), lambda i, c: (c, layer, 0, 0)),
                  pl.BlockSpec((None, None, fc, d), lambda i, c: (c + 4, layer, 0, 0))],
        out_specs=[chunk, chunk, chunk, pl.BlockSpec((tm, d), lambda i, c: (i, 0))],
        out_shape=[_sds((4, t, fc), BF16)] * 3 + [_sds((t, d), BF16)])(x, g, wg, wg)


def _norm_matmul_heads(x, g, wg, layer, scale, name):
    t, d = x.shape
    tm = ROW_TILE
    hp = d // LANES

    def body(x_ref, g_ref, w_ref, y_ref, xn_ref):
        xn = _rms(x_ref[...], g_ref[...]).astype(BF16)
        xn_ref[...] = xn
        y = _dot(xn, w_ref[...].reshape(d, d)) * scale
        for p in range(hp):
            y_ref[p] = y[:, p * LANES:(p + 1) * LANES]

    return _call(
        body, name=name, grid=(t // tm,),
        in_specs=[pl.BlockSpec((tm, d), lambda i: (i, 0)), pl.BlockSpec((1, d), lambda i: (0, 0)),
                  pl.BlockSpec((N_DEV, None, d // N_DEV, d), lambda i: (0, layer, 0, 0))],
        out_specs=[pl.BlockSpec((hp, tm, LANES), lambda i: (0, i, 0)), pl.BlockSpec((tm, d), lambda i: (i, 0))],
        out_shape=[_sds((hp, t, LANES), F32), _sds((t, d), BF16)])(x, g, wg)


def _shift_down(u, halo, k, tm):
    row = lax.broadcasted_iota(jnp.int32, u.shape, 0)
    out = pltpu.roll(u, k, 0)
    for j in range(k):
        out = jnp.where(row == j, halo[halo.shape[0] - k + j:halo.shape[0] - k + j + 1, :], out)
    return out


def _shift_up(u, halo, k, tm):
    row = lax.broadcasted_iota(jnp.int32, u.shape, 0)
    out = pltpu.roll(u, tm - k, 0)
    for j in range(k):
        out = jnp.where(row == tm - k + j, halo[j:j + 1, :], out)
    return out


HALO = 16


def _conv_fwd(p, cw, name):
    t, d3 = p.shape
    d = d3 // 3
    tm = ROW_TILE
    hb = tm // HALO

    def body(p_ref, prev_ref, cw_ref, z_ref):
        i = pl.program_id(0)
        b = p_ref[:, 0:d].astype(F32)
        u = p_ref[:, d:2 * d].astype(F32) * p_ref[:, 2 * d:3 * d].astype(F32)
        keep = (i > 0).astype(F32)
        hu = prev_ref[:, d:2 * d].astype(F32) * prev_ref[:, 2 * d:3 * d].astype(F32) * keep
        uc = cw_ref[2:3, :] * u + cw_ref[1:2, :] * _shift_down(u, hu, 1, tm) + cw_ref[0:1, :] * _shift_down(u, hu, 2, tm)
        z_ref[...] = (b * uc).astype(BF16)

    return _call(
        body, name=name, grid=(t // tm,),
        in_specs=[pl.BlockSpec((tm, d3), lambda i: (i, 0)),
                  pl.BlockSpec((HALO, d3), lambda i: (jnp.maximum(i * hb - 1, 0), 0)),
                  pl.BlockSpec((3, d), lambda i: (0, 0))],
        out_specs=pl.BlockSpec((tm, d), lambda i: (i, 0)), out_shape=_sds((t, d), BF16))(p, p, cw)


def _matmul_norm_residual(a3, wg, layer, g, x_res, name):
    kc_n, t, kc = a3.shape
    d = wg.shape[-1]
    per = N_DEV // kc_n
    rows = wg.shape[2]
    tm = ROW_TILE

    def body(a_ref, w_ref, g_ref, x_ref, raw_ref, xo_ref):
        raw = None
        for c in range(kc_n):
            term = _dot(a_ref[c], w_ref[c * per:(c + 1) * per].reshape(per * rows, d))
            raw = term if raw is None else raw + term
        raw_ref[...] = raw
        xo_ref[...] = x_ref[...] + _rms(raw, g_ref[...])

    row_spec = pl.BlockSpec((tm, d), lambda i: (i, 0))
    return _call(
        body, name=name, grid=(t // tm,),
        in_specs=[pl.BlockSpec((kc_n, tm, kc), lambda i: (0, i, 0)),
                  pl.BlockSpec((N_DEV, None, rows, d), lambda i: (0, layer, 0, 0)),
                  pl.BlockSpec((1, d), lambda i: (0, 0)), row_spec],
        out_specs=[row_spec, row_spec], out_shape=[_sds((t, d), F32)] * 2)(a3, wg, g, x_res)


def _alibi_slopes(n_heads):
    hh = np.arange(n_heads, dtype=np.float32) + 1.0
    s = np.power(2.0, -8.0 * hh / n_heads).astype(np.float32)
    return jnp.asarray(np.repeat(s.reshape(n_heads // 2, 2, 1), 2 * ATT_BLOCK, axis=2))


def _band_bias(sl_ref, dil):
    u = lax.broadcasted_iota(jnp.int32, (ATT_BLOCK, 2 * ATT_BLOCK), 0)
    kk = lax.broadcasted_iota(jnp.int32, (ATT_BLOCK, 2 * ATT_BLOCK), 1)
    delta = u + ATT_BLOCK - kk
    valid = (delta >= 0) & (delta <= ATT_BLOCK)
    dist = (delta * dil).astype(F32)
    rows = [jnp.where(valid, -sl_ref[hd:hd + 1, :] * dist, NEG) for hd in range(2)]
    return jnp.concatenate(rows, axis=0)


def _stack_heads(a):
    lane = lax.broadcasted_iota(jnp.int32, a.shape, 1)
    return jnp.concatenate([jnp.where(lane < HEAD_DIM, a, 0.0), jnp.where(lane >= HEAD_DIM, a, 0.0)], axis=0).astype(BF16)


def _unstack_heads(a2):
    top, bot = a2[:ATT_BLOCK], a2[ATT_BLOCK:]
    lane = lax.broadcasted_iota(jnp.int32, top.shape, 1)
    return jnp.where(lane < HEAD_DIM, top, bot)


def _rows_to_lanes(a0, a1):
    eye = lax.broadcasted_iota(jnp.int32, a0.shape, 0) == lax.broadcasted_iota(jnp.int32, a0.shape, 1)
    return jnp.concatenate([jnp.sum(jnp.where(eye, a, 0.0), axis=0, keepdims=True) for a in (a0, a1)], axis=1)


def _fill_bias_t(sl_ref, bias_ref):
    kk = lax.broadcasted_iota(jnp.int32, (2 * ATT_BLOCK, 2 * ATT_BLOCK), 0)
    lane = lax.broadcasted_iota(jnp.int32, (2 * ATT_BLOCK, 2 * ATT_BLOCK), 1)
    delta = lane % ATT_BLOCK + ATT_BLOCK - kk
    valid = (delta >= 0) & (delta <= ATT_BLOCK)
    slope = jnp.concatenate([sl_ref[0:1, :ATT_BLOCK], sl_ref[1:2, :ATT_BLOCK]], axis=1)
    for gi, dil in enumerate(DILATIONS):
        bias = jnp.where(valid, -slope * (delta * dil).astype(F32), NEG)
        bias_ref[2 * gi] = bias
        bias_ref[2 * gi + 1] = jnp.where(kk < ATT_BLOCK, NEG, bias)


def _fill_bias(sl_ref, bias_ref):
    kk = lax.broadcasted_iota(jnp.int32, (2 * ATT_BLOCK, 2 * ATT_BLOCK), 1)
    for gi, dil in enumerate(DILATIONS):
        bias = _band_bias(sl_ref, dil)
        bias_ref[2 * gi] = bias
        bias_ref[2 * gi + 1] = jnp.where(kk < ATT_BLOCK, NEG, bias)


def _strided_keys(dil, r, ks, kc_ref, vc_ref, kw_ref, vw_ref, kcar_ref, vcar_ref):
    if dil * ATT_BLOCK != SUPER:
        keys = pl.ds(ks, 2 * ATT_BLOCK, stride=dil)
        return kw_ref[keys, :].astype(BF16), vw_ref[keys, :].astype(BF16)
    own = pl.ds(r, ATT_BLOCK, stride=dil)
    k_own, v_own = kc_ref[own, :].astype(BF16), vc_ref[own, :].astype(BF16)
    kb = jnp.concatenate([kcar_ref[r], k_own], axis=0)
    vb = jnp.concatenate([vcar_ref[r], v_own], axis=0)
    kcar_ref[r] = k_own
    vcar_ref[r] = v_own
    return kb, vb


def _attention_fwd(q, kv, slopes, name):
    hp, t, _ = q.shape
    ns = t // SUPER
    nd = len(DILATIONS)

    def body(sl_ref, q_ref, kc_ref, kp_ref, vc_ref, vp_ref, o_ref, lse_ref, kw_ref, vw_ref, og_ref, lg_ref, bias_ref,
             kcar_ref, vcar_ref):
        n = pl.program_id(1)
        kw_ref[0:SUPER, :] = kp_ref[...]
        kw_ref[SUPER:, :] = kc_ref[...]
        vw_ref[0:SUPER, :] = vp_ref[...]
        vw_ref[SUPER:, :] = vc_ref[...]

        @pl.when(n == 0)
        def _():
            _fill_bias(sl_ref, bias_ref)
            kcar_ref[...] = jnp.zeros_like(kcar_ref)
            vcar_ref[...] = jnp.zeros_like(vcar_ref)

        for gi, dil in enumerate(DILATIONS):

            def block(idx, carry, gi=gi, dil=dil):
                r, b = idx % dil, idx // dil
                qs = b * (ATT_BLOCK * dil) + r
                ks = SUPER + (b - 1) * (ATT_BLOCK * dil) + r
                first = jnp.logical_and(n == 0, b == 0).astype(jnp.int32)
                q2 = _stack_heads(q_ref[pl.ds(qs, ATT_BLOCK, stride=dil), :])
                kb, vb = _strided_keys(dil, r, ks, kc_ref, vc_ref, kw_ref, vw_ref, kcar_ref, vcar_ref)
                s = _dot_nt(q2, kb) + bias_ref[2 * gi + first]
                m = jnp.max(s, axis=-1, keepdims=True)
                p = jnp.exp(s - m).astype(BF16)
                ol = _dot(p, jnp.concatenate([vb, jnp.ones_like(vb)], axis=1))
                l = ol[:, LANES:]
                o2 = ol[:, :LANES] / l
                lse2 = m + jnp.log(l)
                og_ref[gi, pl.ds(qs, ATT_BLOCK, stride=dil), :] = _unstack_heads(o2)
                lg_ref[gi, pl.ds(qs, ATT_BLOCK, stride=dil), :] = _unstack_heads(lse2)
                return carry

            lax.fori_loop(0, SUPER // ATT_BLOCK, block, 0, unroll=ATT_UNROLL_FWD)
        lg = [lg_ref[gi] for gi in range(nd)]
        top = functools.reduce(jnp.maximum, lg)
        ws = [jnp.exp(x - top) for x in lg]
        tot = functools.reduce(jnp.add, ws)
        lse_ref[...] = top + jnp.log(tot)
        acc = ws[0] * og_ref[0]
        for gi in range(1, nd):
            acc = acc + ws[gi] * og_ref[gi]
        o_ref[...] = (acc / tot).astype(BF16)

    cur = lambda off: pl.BlockSpec((None, SUPER, LANES), lambda h, n: (h + off, n, 0))
    prev = lambda off: pl.BlockSpec((None, SUPER, LANES), lambda h, n: (h + off, jnp.maximum(n - 1, 0), 0))
    return _call(
        body, name=name, grid=(hp, ns),
        in_specs=[pl.BlockSpec((None, 2, 2 * ATT_BLOCK), lambda h, n: (h, 0, 0)), cur(0), cur(0), prev(0), cur(hp), prev(hp)],
        out_specs=[pl.BlockSpec((SUPER, LANES), lambda h, n: (n, h)), cur(0)],
        out_shape=[_sds((t, hp * LANES), BF16), _sds((hp, t, LANES), F32)],
        scratch_shapes=[pltpu.VMEM((2 * SUPER, LANES), F32), pltpu.VMEM((2 * SUPER, LANES), F32),
                        pltpu.VMEM((nd, SUPER, LANES), F32), pltpu.VMEM((nd, SUPER, LANES), F32),
                        pltpu.VMEM((2 * nd, 2 * ATT_BLOCK, 2 * ATT_BLOCK), F32)] + [
                            pltpu.VMEM((DILATIONS[-1], ATT_BLOCK, LANES), BF16)] * 2,
    )(slopes, q, kv, kv, kv, kv)


def _attention_bwd(q, kv, o, lse, d_o, dk_in, dv_in, slopes, q_scale, name):
    hp, t, _ = q.shape
    ns = t // SUPER
    shared = dk_in is not None

    def body(sl_ref, q_ref, kc_ref, kp_ref, vc_ref, vp_ref, o_ref, lse_ref, do_ref, *rest):
        dki_ref, dvi_ref = rest[:2] if shared else (None, None)
        dq_ref, dk_ref, dv_ref, kw_ref, vw_ref, dkw_ref, dvw_ref, st_ref, bias_ref, kcar_ref, vcar_ref = rest[2 if shared else 0:]
        n = pl.program_id(1)

        @pl.when(n == 0)
        def _():
            dkw_ref[...] = jnp.zeros_like(dkw_ref)
            dvw_ref[...] = jnp.zeros_like(dvw_ref)

        @pl.when(n > 0)
        def _():
            dkw_ref[0:SUPER, :] = dkw_ref[SUPER:, :]
            dvw_ref[0:SUPER, :] = dvw_ref[SUPER:, :]
            dkw_ref[SUPER:, :] = jnp.zeros((SUPER, LANES), F32)
            dvw_ref[SUPER:, :] = jnp.zeros((SUPER, LANES), F32)

        @pl.when(n < ns)
        def _():
            kw_ref[0:SUPER, :] = kp_ref[...]
            kw_ref[SUPER:, :] = kc_ref[...]
            vw_ref[0:SUPER, :] = vp_ref[...]
            vw_ref[SUPER:, :] = vc_ref[...]
            prod = do_ref[...] * o_ref[...].astype(F32)
            lane = lax.broadcasted_iota(jnp.int32, prod.shape, 1)
            zero = jnp.zeros((SUPER, LANES), F32)
            st_ref[0] = zero + jnp.sum(jnp.where(lane < HEAD_DIM, prod, 0.0), axis=-1, keepdims=True)
            st_ref[1] = zero + jnp.sum(jnp.where(lane >= HEAD_DIM, prod, 0.0), axis=-1, keepdims=True)
            lse = lse_ref[...]
            swapped = pltpu.roll(lse, HEAD_DIM, 1)
            st_ref[2] = jnp.where(lane < HEAD_DIM, lse, swapped)
            st_ref[3] = jnp.where(lane >= HEAD_DIM, lse, swapped)
            dq_ref[...] = jnp.zeros_like(dq_ref)

            @pl.when(n == 0)
            def _():
                _fill_bias_t(sl_ref, bias_ref)
                kcar_ref[...] = jnp.zeros_like(kcar_ref)
                vcar_ref[...] = jnp.zeros_like(vcar_ref)

            for gi, dil in enumerate(DILATIONS):

                def block(idx, carry, gi=gi, dil=dil):
                    r, b = idx % dil, idx // dil
                    qs = b * (ATT_BLOCK * dil) + r
                    ks = SUPER + (b - 1) * (ATT_BLOCK * dil) + r
                    first = jnp.logical_and(n == 0, b == 0).astype(jnp.int32)
                    rows = pl.ds(qs, ATT_BLOCK, stride=dil)
                    keys = pl.ds(ks, 2 * ATT_BLOCK, stride=dil)
                    q2 = _stack_heads(q_ref[rows, :])
                    do2 = _stack_heads(do_ref[rows, :])
                    kb, vb = _strided_keys(dil, r, ks, kc_ref, vc_ref, kw_ref, vw_ref, kcar_ref, vcar_ref)
                    dd = _rows_to_lanes(st_ref[0, rows, :], st_ref[1, rows, :])
                    lse_b = _rows_to_lanes(st_ref[2, rows, :], st_ref[3, rows, :])
                    ps, dss = [], []
                    for half in range(2):
                        hk = slice(half * ATT_BLOCK, (half + 1) * ATT_BLOCK)
                        p = jnp.exp(_dot_nt(kb[hk], q2) + bias_ref[2 * gi + first, hk, :] - lse_b)
                        dss.append((p * (_dot_nt(vb[hk], do2) - dd)).astype(BF16))
                        ps.append(p.astype(BF16))
                    p, ds = jnp.concatenate(ps, axis=0), jnp.concatenate(dss, axis=0)
                    dvw_ref[keys, :] += _dot(p, do2)
                    dkw_ref[keys, :] += _dot(ds, q2)
                    dq_ref[rows, :] += _unstack_heads(_dot_tn(ds, kb)) * q_scale
                    return carry

                lax.fori_loop(0, SUPER // ATT_BLOCK, block, 0, unroll=ATT_UNROLL_BWD)

        dk_ref[...] = dkw_ref[0:SUPER, :] + dki_ref[...] if shared else dkw_ref[0:SUPER, :]
        dv_ref[...] = dvw_ref[0:SUPER, :] + dvi_ref[...] if shared else dvw_ref[0:SUPER, :]

    last = ns - 1
    cur = lambda off: pl.BlockSpec((None, SUPER, LANES), lambda h, n: (h + off, jnp.minimum(n, last), 0))
    prev = lambda off: pl.BlockSpec((None, SUPER, LANES), lambda h, n: (h + off, jnp.clip(n - 1, 0, last), 0))
    nat = pl.BlockSpec((SUPER, LANES), lambda h, n: (jnp.minimum(n, last), h))
    late = pl.BlockSpec((None, SUPER, LANES), lambda h, n: (h, jnp.maximum(n - 1, 0), 0))
    dq, dk, dv = _call(
        body, name=name, grid=(hp, ns + 1),
        in_specs=[pl.BlockSpec((None, 2, 2 * ATT_BLOCK), lambda h, n: (h, 0, 0)), cur(0), cur(0), prev(0), cur(hp), prev(hp),
                  nat, cur(0), nat] + ([late, late] if shared else []),
        out_specs=[cur(0), late, late],
        out_shape=[_sds((hp, t, LANES), F32)] * 3,
        scratch_shapes=[pltpu.VMEM((2 * SUPER, LANES), F32)] * 4 + [
            pltpu.VMEM((4, SUPER, LANES), F32), pltpu.VMEM((2 * len(DILATIONS), 2 * ATT_BLOCK, 2 * ATT_BLOCK), F32)] + [
                pltpu.VMEM((DILATIONS[-1], ATT_BLOCK, LANES), BF16)] * 2,
    )(slopes, q, kv, kv, kv, kv, o, lse, d_o, *((dk_in, dv_in) if shared else ()))
    return dq, dk, dv


def _loss_head(y, target, raw, g, name):
    t, d = y.shape
    tm = ROW_TILE

    def body(y_ref, t_ref, raw_ref, g_ref, sq_ref, dy_ref, draw_ref, dg_ref):
        i = pl.program_id(0)
        err = y_ref[...] - t_ref[...]
        dy = err * (1.0 / d)
        dy_ref[...] = dy
        draw, dg = _rms_bwd(raw_ref[...], g_ref[...], dy)
        draw_ref[...] = draw.astype(BF16)
        sq = jnp.zeros((8, LANES), F32) + jnp.sum(err * err)

        @pl.when(i == 0)
        def _():
            sq_ref[...] = sq
            dg_ref[...] = dg

        @pl.when(i > 0)
        def _():
            sq_ref[...] += sq
            dg_ref[...] += dg

    row = pl.BlockSpec((tm, d), lambda i: (i, 0))
    vec = pl.BlockSpec((1, d), lambda i: (0, 0))
    return _call(
        body, name=name, grid=(t // tm,), in_specs=[row, row, row, vec],
        out_specs=[pl.BlockSpec((8, LANES), lambda i: (0, 0)), row, row, vec],
        out_shape=[_sds((8, LANES), F32), _sds((t, d), F32), _sds((t, d), BF16), _sds((1, d), F32)])(y, target, raw, g)


def _bwd_matmul_norms(a_specs, a_args, a_tile, n_steps, w_spec, w_arg, w_mat, xa, ga, resid, xb, gb, name,
                      w_transposed=False):
    t, d = xa.shape
    tm = BWD_ROW_TILE
    na = len(a_specs)
    second = xb is not None
    per = 4 if n_steps % 4 == 0 else 1
    n_steps //= per

    def blocks_of(spec, k):
        return pl.BlockSpec(spec.block_shape, lambda i, j: spec.index_map(i, per * j + k))

    def body(*refs):
        a_refs, w_refs = refs[:per * na], refs[per * na:per * na + per]
        xa_ref, ga_ref, res_ref = refs[per * na + per:per * na + per + 3]
        rest = refs[per * na + per + 3:]
        if second:
            xb_ref, gb_ref, dx_ref, d2_ref, dga_ref, dgb_ref, acc_ref = rest
        else:
            dx_ref, dga_ref, acc_ref = rest
        i, j = pl.program_id(0), pl.program_id(1)
        part = None
        for k in range(per):
            term = (_dot if w_transposed else _dot_nt)(a_tile(per * j + k, *a_refs[k * na:(k + 1) * na]), w_mat(w_refs[k]))
            part = term if part is None else part + term

        @pl.when(j == 0)
        def _():
            acc_ref[...] = part

        @pl.when(j > 0)
        def _():
            acc_ref[...] += part

        @pl.when(j == n_steps - 1)
        def _():
            da, dga = _rms_bwd(xa_ref[...], ga_ref[...], acc_ref[...])
            dx = res_ref[...] + da
            dx_ref[...] = dx
            if second:
                d2, dgb = _rms_bwd(xb_ref[...], gb_ref[...], dx)
                d2_ref[...] = d2.astype(BF16)

            @pl.when(i == 0)
            def _():
                dga_ref[...] = dga
                if second:
                    dgb_ref[...] = dgb

            @pl.when(i > 0)
            def _():
                dga_ref[...] += dga
                if second:
                    dgb_ref[...] += dgb

    row = pl.BlockSpec((tm, d), lambda i, j: (i, 0))
    vec = pl.BlockSpec((1, d), lambda i, j: (0, 0))
    in_specs = [blocks_of(sp, k) for k in range(per) for sp in a_specs] + [blocks_of(w_spec, k) for k in range(per)]
    in_specs += [row, vec, row]
    args = list(a_args) * per + [w_arg] * per + [xa, ga, resid]
    if second:
        in_specs += [row, vec]
        args += [xb, gb]
        out_specs = [row, row, vec, vec]
        out_shape = [_sds((t, d), F32), _sds((t, d), BF16), _sds((1, d), F32), _sds((1, d), F32)]
    else:
        out_specs = [row, vec]
        out_shape = [_sds((t, d), F32), _sds((1, d), F32)]
    return _call(body, name=name, grid=(t // tm, n_steps), in_specs=in_specs, out_specs=out_specs,
                 out_shape=out_shape, scratch_shapes=[pltpu.VMEM((tm, d), F32)])(*args)


def _heads_to_rows(*refs):
    hp = refs[0].shape[0]
    cols = []
    for p in range(hp):
        v = refs[0][p]
        for r in refs[1:]:
            v = v + r[p]
        cols.append(v)
    return jnp.concatenate(cols, axis=-1).astype(BF16)


def _matmul_nt_rows(a, wg, layer, out_dtype, name):
    t, d = a.shape
    tm = ROW_TILE

    def body(a_ref, w_ref, o_ref):
        o_ref[...] = _dot_nt(a_ref[...], w_ref[...].reshape(d, d)).astype(out_dtype)

    row = pl.BlockSpec((tm, d), lambda i: (i, 0))
    return _call(body, name=name, grid=(t // tm,),
                 in_specs=[row, pl.BlockSpec((N_DEV, None, d // N_DEV, d), lambda i: (0, layer, 0, 0))],
                 out_specs=row, out_shape=_sds((t, d), out_dtype))(a, wg)


def _swiglu_bwd(d_ff, wg, layer, gate, up, name):
    t, d = d_ff.shape
    fc = gate.shape[-1]
    rows = wg.shape[2]
    tm = BIG_ROW_TILE

    def body(df_ref, w_ref, g_ref, u_ref, dh_ref):
        w = w_ref[...].reshape(2 * rows, d)
        for r0 in range(0, tm, SWIGLU_ROWS):
            rs = slice(r0, r0 + SWIGLU_ROWS)
            da = _dot_nt(df_ref[rs, :], w)
            gate, up = g_ref[rs, :].astype(F32), u_ref[rs, :].astype(F32)
            sig = jax.nn.sigmoid(gate)
            dh_ref[0, rs, :] = (da * up * (sig * (1.0 + gate * (1.0 - sig)))).astype(BF16)
            dh_ref[1, rs, :] = (da * (gate * sig)).astype(BF16)

    return _call(
        body, name=name, grid=(t // tm, 4),
        in_specs=[pl.BlockSpec((tm, d), lambda i, c: (i, 0)),
                  pl.BlockSpec((2, None, rows, d), lambda i, c: (c, layer, 0, 0)),
                  pl.BlockSpec((None, tm, fc), lambda i, c: (c, i, 0)),
                  pl.BlockSpec((None, tm, fc), lambda i, c: (c, i, 0))],
        out_specs=pl.BlockSpec((None, 2, tm, fc), lambda i, c: (c, 0, i, 0)),
        out_shape=_sds((4, 2, t, fc), BF16))(d_ff, wg, gate, up)


def _conv_bwd(p, d_z, cw, name):
    t, d3 = p.shape
    d = d3 // 3
    tm = ROW_TILE
    hb = tm // HALO
    nt = t // tm

    def body(p_ref, prev_ref, next_ref, dz_ref, dzn_ref, cw_ref, dp_ref, dcw_ref):
        i = pl.program_id(0)
        b = p_ref[:, 0:d].astype(F32)
        c = p_ref[:, d:2 * d].astype(F32)
        h = p_ref[:, 2 * d:3 * d].astype(F32)
        u = c * h
        hu = prev_ref[:, d:2 * d].astype(F32) * prev_ref[:, 2 * d:3 * d].astype(F32) * (i > 0).astype(F32)
        u1, u2 = _shift_down(u, hu, 1, tm), _shift_down(u, hu, 2, tm)
        uc = cw_ref[2:3, :] * u + cw_ref[1:2, :] * u1 + cw_ref[0:1, :] * u2
        dz = dz_ref[...].astype(F32)
        duc = dz * b
        dn = dzn_ref[...].astype(F32) * next_ref[:, 0:d].astype(F32) * (i < nt - 1).astype(F32)
        du = cw_ref[2:3, :] * duc + cw_ref[1:2, :] * _shift_up(duc, dn, 1, tm) + cw_ref[0:1, :] * _shift_up(duc, dn, 2, tm)
        dp_ref[:, 0:d] = (dz * uc).astype(BF16)
        dp_ref[:, d:2 * d] = (du * h).astype(BF16)
        dp_ref[:, 2 * d:3 * d] = (du * c).astype(BF16)
        dcw = jnp.concatenate([jnp.sum(duc * u2, axis=0, keepdims=True), jnp.sum(duc * u1, axis=0, keepdims=True),
                               jnp.sum(duc * u, axis=0, keepdims=True)], axis=0)

        @pl.when(i == 0)
        def _():
            dcw_ref[...] = dcw

        @pl.when(i > 0)
        def _():
            dcw_ref[...] += dcw

    last_halo = t // HALO - 1
    return _call(
        body, name=name, grid=(nt,),
        in_specs=[pl.BlockSpec((tm, d3), lambda i: (i, 0)),
                  pl.BlockSpec((HALO, d3), lambda i: (jnp.maximum(i * hb - 1, 0), 0)),
                  pl.BlockSpec((HALO, d3), lambda i: (jnp.minimum((i + 1) * hb, last_halo), 0)),
                  pl.BlockSpec((tm, d), lambda i: (i, 0)),
                  pl.BlockSpec((HALO, d), lambda i: (jnp.minimum((i + 1) * hb, last_halo), 0)),
                  pl.BlockSpec((3, d), lambda i: (0, 0))],
        out_specs=[pl.BlockSpec((tm, d3), lambda i: (i, 0)), pl.BlockSpec((3, d), lambda i: (0, 0))],
        out_shape=[_sds((t, d3), BF16), _sds((3, d), F32)])(p, p, p, d_z, d_z, cw)


def _grad_weight(a_specs, a_args, a_tile, b_specs, b_args, b_tile, n_out, acc_shape, out_spec, out_shape, t, name,
                 a_transposed=False):
    tt = GRAD_ROW_TILE
    na, nb = len(a_specs), len(b_specs)

    def body(*refs):
        a_refs, b_refs = refs[:na], refs[na:na + nb]
        o_ref, acc_ref = refs[na + nb:]
        s = pl.program_id(1)
        a, b = a_tile(pl.program_id(0), *a_refs), b_tile(pl.program_id(0), *b_refs)
        part = _dot(a, b) if a_transposed else _dot_tn(a, b)

        @pl.when(s == 0)
        def _():
            acc_ref[...] = part

        @pl.when(s > 0)
        def _():
            acc_ref[...] += part

        @pl.when(s == t // tt - 1)
        def _():
            acc = acc_ref[...].astype(BF16)
            if o_ref.shape[-1] == acc.shape[-1]:
                o_ref[...] = acc.reshape(o_ref.shape)
            else:
                for k in range(o_ref.shape[0]):
                    o_ref[k] = acc[:, k * o_ref.shape[-1]:(k + 1) * o_ref.shape[-1]]

    return _call(body, name=name, grid=(n_out, t // tt), in_specs=list(a_specs) + list(b_specs), out_specs=out_spec,
                 out_shape=out_shape, scratch_shapes=[pltpu.VMEM(acc_shape, F32)])(*a_args, *b_args)


def _ident(*args):
    return args[-1][...]


def _heads_tile(j, *refs):
    return _heads_to_rows(*refs)


def kernel(x, norm_g, conv_in_w, conv_w, conv_out_w, kv_norm_g, kv_w, q_w, o_w, ffn_in_w, ffn_out_w, loss_target, m_norm_g, m_conv_in_w, m_conv_w, m_conv_out_w, m_kv_norm_g, m_kv_w, m_q_w, m_o_w, m_ffn_in_w, m_ffn_out_w, v_norm_g, v_conv_in_w, v_conv_w, v_conv_out_w, v_kv_norm_g, v_kv_w, v_q_w, v_o_w, v_ffn_in_w, v_ffn_out_w):
    x0 = x[0]
    target = loss_target[0]
    t, d = x0.shape
    depth = norm_g.shape[0]
    n_a = conv_in_w.shape[0]
    n_b = q_w.shape[0]
    hp = d // LANES
    tm, tg = BWD_ROW_TILE, GRAD_ROW_TILE
    assert t % SUPER == 0 and d % LANES == 0 and depth == n_a + n_b
    dev = 4 * lax.axis_index("x") + 2 * lax.axis_index("y") + lax.axis_index("c")

    n_small = 4 * depth + 3 * n_a
    small_rows = -(-(n_small + 1) // 8) * 8
    small_local = jnp.concatenate([norm_g.reshape(4 * depth, -1), conv_w.reshape(3 * n_a, -1),
                                   jnp.zeros((small_rows - n_small, norm_g.shape[-1]), F32)], axis=0)
    swap = lambda a: jnp.swapaxes(a, 1, 2)
    big = {"conv_in_w": conv_in_w, "conv_out_w": conv_out_w, "kv_w": kv_w[None], "q_w": q_w, "o_w": o_w,
           "ffn_in_w": swap(ffn_in_w), "ffn_out_w": ffn_out_w}
    names = list(big)

    def group(layer):
        if layer < n_a:
            return [("conv_in_w", layer), ("conv_out_w", layer), ("ffn_in_w", layer), ("ffn_out_w", layer)]
        j = layer - n_a
        return ([("kv_w", 0)] if j == 0 else []) + [("q_w", j), ("o_w", j), ("ffn_in_w", layer), ("ffn_out_w", layer)]

    slot = dev.astype(jnp.int32).reshape(1)
    is_ffn = lambda key: key[0].startswith("ffn")
    first_keys = [key for key in group(0) if not is_ffn(key)]
    first = _all_gather([small_local] + [_cast_layer(big[k], i, None, f"cast_{k}_{i}") for k, i in first_keys], "gather_weights")
    small_all = first[0].transpose(1, 0, 2).reshape(small_rows, d)
    wl = {key: a[:, None] for key, a in zip(first_keys, first[1:])}

    def gather_start(keys, after, tag):
        lands = [_cast_layer(big[k], i, slot, f"cast_{k}_{i}") for k, i in keys]
        send_sems, recv_sems, _, lands, tok = _send_start([], lands, after, "gather_start" + tag)
        return (keys, tag, send_sems, recv_sems, lands), tok[0, 0]

    def gather_wait(flight, after):
        keys, tag, send_sems, recv_sems, lands = flight
        _, lands = _send_wait(send_sems, recv_sems, [], lands, after, "gather_wait" + tag)
        wl.update({key: a[:, None] for key, a in zip(keys, lands)})

    in_flight, token = gather_start([key for key in group(0) if is_ffn(key)], small_all, "_l0")
    W = lambda k, i: (wl[(k, i)], 0)
    gain = lambda layer, k: small_all[4 * layer + k][None]
    taps = lambda layer: small_all[4 * depth + 3 * layer: 4 * depth + 3 * layer + 3]
    g_kv = kv_norm_g[None]
    slopes = _alibi_slopes(d // HEAD_DIM)
    fc = big["ffn_in_w"].shape[-2]
    cb = big["conv_in_w"].shape[-1]
    kvb = big["kv_w"].shape[-1]
    q_scale = HEAD_DIM ** -0.5

    saved = []
    kv = kvn_t = None
    xs = x0
    for layer in range(depth):
        tag = f"_l{layer}"
        g0 = g2 = 0.0
        if layer == 0:
            g0 = token
        else:
            gather_wait(in_flight, xs)
            if layer + 1 < depth:
                in_flight, g0 = gather_start(group(layer + 1), xs, f"_l{layer + 1}")
        s = {"x_in": xs}
        g0 = gain(layer, 0) + g0
        if layer < n_a:
            s["p"], s["xn_t"] = _norm_matmul_cols(xs, g0, *W("conv_in_w", layer), "cols", "conv_in" + tag)
            s["z"] = _conv_fwd(s["p"], taps(layer), "conv" + tag)
            s["mix"], x_mid = _matmul_norm_residual(s["z"][None], *W("conv_out_w", layer), gain(layer, 1), xs, "conv_out" + tag)
        else:
            j = layer - n_a
            if kv is None:
                kv, kvn_t = _norm_matmul_cols(xs, g_kv, *W("kv_w", 0), "heads", "kv_proj")
            s["q"], s["xn"] = _norm_matmul_heads(xs, g0, *W("q_w", j), q_scale, "q_proj" + tag)
            s["o"], s["lse"] = _attention_fwd(s["q"], kv, slopes, "attention" + tag)
            s["mix"], x_mid = _matmul_norm_residual(s["o"][None], *W("o_w", j), gain(layer, 1), xs, "o_proj" + tag)
        s["x_mid"] = x_mid
        if layer == 0:
            gather_wait(in_flight, x_mid)
            in_flight, g2 = gather_start(group(1), x_mid, "_l1")
        s["gate"], s["up"], s["a"], s["fn"] = _ffn_in_swiglu(x_mid, gain(layer, 2) + g2, *W("ffn_in_w", layer), "ffn_in" + tag)
        s["ff"], xs = _matmul_norm_residual(s["a"], *W("ffn_out_w", layer), gain(layer, 3), x_mid, "ffn_out" + tag)
        saved.append(s)

    last = saved[-1]
    sq, dx_out, d_ff, dg3 = _loss_head(xs, target, last["ff"], gain(depth - 1, 3), "loss_head")
    loss = lax.psum(sq[0, 0] * (0.5 / d), ("x", "y", "c"))

    dgain = {(depth - 1, 3): dg3}
    dtaps = {}
    grads = {k: [None] * big[k].shape[0] for k in names}
    dkv_parts = []
    scattering = []

    def scatter_start(keys, tag):
        parts = [grads[k][i] for k, i in keys]
        zones = [lax.empty(p.shape, p.dtype) for p in parts]
        send_sems, recv_sems, parts, zones, tok = _send_start(parts, zones, small_all, "scatter_start" + tag)
        scattering.append((keys, tag, send_sems, recv_sems, parts, zones))
        return tok[0, 0]

    for layer in reversed(range(depth)):
        tag = f"_l{layer}"
        s = saved[layer]
        dh = _swiglu_bwd(d_ff, *W("ffn_out_w", layer), s["gate"], s["up"], "swiglu_bwd" + tag)
        rows_out = big["ffn_out_w"].shape[1]
        grads["ffn_out_w"][layer] = _grad_weight(
            [pl.BlockSpec((None, tg, fc), lambda c, i: (c, i, 0))], [s["a"]], _ident,
            [pl.BlockSpec((tg, d), lambda c, i: (i, 0))], [d_ff], _ident,
            4, (fc, d), pl.BlockSpec((2, rows_out, d), lambda c, i: (c, 0, 0)), _sds((N_DEV, rows_out, d), BF16), t,
            "grad_ffn_out" + tag)
        grads["ffn_in_w"][layer] = _grad_weight(
            [pl.BlockSpec((None, None, tg, fc), lambda j, i: (j % 4, j // 4, i, 0))], [dh], _ident,
            [pl.BlockSpec((tg, d), lambda j, i: (i, 0))], [s["fn"]], _ident,
            N_DEV, (fc, d), pl.BlockSpec((None, fc, d), lambda j, i: (j, 0, 0)), _sds((N_DEV, fc, d), BF16), t,
            "grad_ffn_in" + tag)
        tok = scatter_start([("ffn_in_w", layer), ("ffn_out_w", layer)], "_ffn" + tag)
        dx_mid, d_mix, dg2, dg1 = _bwd_matmul_norms(
            [pl.BlockSpec((None, None, tm, fc), lambda i, j: (j % 4, j // 4, i, 0))], [dh], _ident, N_DEV,
            pl.BlockSpec((None, None, fc, d), lambda i, j: (j, 0, 0, 0)), W("ffn_in_w", layer)[0], _ident,
            s["x_mid"], gain(layer, 2) + tok, dx_out, s["mix"], gain(layer, 1), "ffn_in_bwd" + tag, w_transposed=True)
        dgain[(layer, 2)], dgain[(layer, 1)] = dg2, dg1
        full_rows = pl.BlockSpec((N_DEV, d // N_DEV, d), lambda j, i: (0, 0, 0))
        rows_w = lambda wname, idx: (pl.BlockSpec((N_DEV, None, d // N_DEV, d), lambda i, j: (0, 0, 0, 0)), W(wname, idx)[0],
                                     lambda w_ref: w_ref[...].reshape(d, d))
        if layer < n_a:
            d_z = _matmul_nt_rows(d_mix, *W("conv_out_w", layer), BF16, "conv_out_bwd" + tag)
            grads["conv_out_w"][layer] = _grad_weight(
                [pl.BlockSpec((tg, d), lambda j, i: (i, 0))], [s["z"]], _ident,
                [pl.BlockSpec((tg, d), lambda j, i: (i, 0))], [d_mix], _ident,
                1, (d, d), full_rows, _sds((N_DEV, d // N_DEV, d), BF16), t, "grad_conv_out" + tag)
            d_p, dtaps[layer] = _conv_bwd(s["p"], d_z, taps(layer), "conv_bwd" + tag)
            grads["conv_in_w"][layer] = _grad_weight(
                [pl.BlockSpec((d, tg), lambda j, i: (0, i))], [s["xn_t"]], _ident,
                [pl.BlockSpec((tg, 2 * cb), lambda j, i: (i, j))], [d_p], _ident,
                N_DEV // 2, (d, 2 * cb), pl.BlockSpec((2, d, cb), lambda j, i: (j, 0, 0)), _sds((N_DEV, d, cb), BF16), t,
                "grad_conv_in" + tag, a_transposed=True)
            a_specs, a_args, a_tile, n_steps = [pl.BlockSpec((tm, 4 * cb), lambda i, j: (i, j))], [d_p], _ident, N_DEV // 4
            w_spec = pl.BlockSpec((4, None, d, cb), lambda i, j: (j, 0, 0, 0))
            w_arg = W("conv_in_w", layer)[0]
            w_mat = lambda w_ref: jnp.concatenate([w_ref[k] for k in range(4)], axis=1)
            resid = dx_mid
        else:
            j_b = layer - n_a
            d_o = _matmul_nt_rows(d_mix, *W("o_w", j_b), F32, "o_proj_bwd" + tag)
            grads["o_w"][j_b] = _grad_weight(
                [pl.BlockSpec((tg, d), lambda j, i: (i, 0))], [s["o"]], _ident,
                [pl.BlockSpec((tg, d), lambda j, i: (i, 0))], [d_mix], _ident,
                1, (d, d), full_rows, _sds((N_DEV, d // N_DEV, d), BF16), t, "grad_o" + tag)
            dk_in, dv_in = dkv_parts[0] if dkv_parts else (None, None)
            dq, dk, dv = _attention_bwd(s["q"], kv, s["o"], s["lse"], d_o, dk_in, dv_in, slopes, q_scale, "attention_bwd" + tag)
            dkv_parts = [(dk, dv)]
            heads_spec = pl.BlockSpec((hp, tg, LANES), lambda j, i: (0, i, 0))
            grads["q_w"][j_b] = _grad_weight(
                [pl.BlockSpec((tg, d), lambda j, i: (i, 0))], [s["xn"]], _ident,
                [heads_spec], [dq], _heads_tile,
                1, (d, d), full_rows, _sds((N_DEV, d // N_DEV, d), BF16), t, "grad_q" + tag)
            a_specs, a_args, a_tile, n_steps = [pl.BlockSpec((hp, tm, LANES), lambda i, j: (0, i, 0))], [dq], _heads_tile, 1
            w_spec, w_arg, w_mat = rows_w("q_w", j_b)
            resid = dx_mid
            if layer == n_a:
                pieces = kvb // LANES
                halves = []
                for src in (0, 1):
                    halves.append([part[src] for part in dkv_parts])
                n_half = len(dkv_parts)
                kv_args = [arr for src in (0, 1) for arr in halves[src]]

                def kv_block(src, j):
                    return jnp.where((j // 4) == src, j % 4, 0)

                def kv_tile(j, *refs):
                    keys = _heads_to_rows(*refs[:n_half])
                    vals = _heads_to_rows(*refs[n_half:])
                    return jnp.where(j < 4, keys, vals)

                kv_specs = [pl.BlockSpec((pieces, tm, LANES), functools.partial(lambda i, j, src: (kv_block(src, j), i, 0), src=src))
                            for src in (0, 1) for _ in range(n_half)]
                resid, dgain["kv"] = _bwd_matmul_norms(
                    kv_specs, kv_args, kv_tile, N_DEV,
                    pl.BlockSpec((None, None, d, kvb), lambda i, j: (j, 0, 0, 0)), W("kv_w", 0)[0], _ident,
                    s["x_in"], g_kv, dx_mid, None, None, "kv_proj_bwd")
                kv_b_specs = [pl.BlockSpec((pieces, tg, LANES), functools.partial(lambda j, i, src: (kv_block(src, j), i, 0), src=src))
                              for src in (0, 1) for _ in range(n_half)]
                grads["kv_w"][0] = _grad_weight(
                    [pl.BlockSpec((d, tg), lambda j, i: (0, i))], [kvn_t], _ident,
                    kv_b_specs, kv_args, kv_tile,
                    N_DEV, (d, kvb), pl.BlockSpec((None, d, kvb), lambda j, i: (j, 0, 0)), _sds((N_DEV, d, kvb), BF16), t,
                    "grad_kv", a_transposed=True)
        tok = scatter_start([key for key in group(layer) if not key[0].startswith("ffn")], "_mix" + tag)
        if layer > 0:
            prev = saved[layer - 1]
            dx_out, d_ff, dg0, dg3p = _bwd_matmul_norms(
                a_specs, a_args, a_tile, n_steps, w_spec, w_arg, w_mat,
                s["x_in"], gain(layer, 0) + tok, resid, prev["ff"], gain(layer - 1, 3), "mixer_in_bwd" + tag)
            dgain[(layer, 0)], dgain[(layer - 1, 3)] = dg0, dg3p
        else:
            grad_x, dg0 = _bwd_matmul_norms(
                a_specs, a_args, a_tile, n_steps, w_spec, w_arg, w_mat,
                s["x_in"], gain(layer, 0), resid, None, None, "mixer_in_bwd" + tag)
            dgain[(layer, 0)] = dg0

    small_grad = jnp.concatenate(
        [dgain[(layer, k)] for layer in range(depth) for k in range(4)] + [dtaps[layer] for layer in range(n_a)]
        + [dgain["kv"]] + [jnp.zeros((small_rows - n_small - 1, d), F32)], axis=0)
    small_grads_all = _all_gather([small_grad], "gather_small_grads")[0]
    lo = dev * (d // N_DEV)

    def pack(ng, cwp, kvg):
        rows = jnp.concatenate([ng.reshape(4 * depth, -1), cwp.reshape(3 * n_a, -1)], axis=0)
        z = lax.dynamic_update_slice(jnp.zeros((small_rows, d), F32), rows, (0, lo))
        return lax.dynamic_update_slice(z, kvg[None], (n_small, 0))

    w_small = lax.dynamic_update_slice(small_all, g_kv, (n_small, 0))
    m_small, v_small = pack(m_norm_g, m_conv_w, m_kv_norm_g), pack(v_norm_g, v_conv_w, v_kv_norm_g)
    sm = _small_adamw(small_grads_all, w_small, m_small, v_small, "adamw_small")

    def unpack(a):
        mine = lax.dynamic_slice(a, (0, lo), (small_rows, d // N_DEV))
        return (mine[:4 * depth].reshape(norm_g.shape), mine[4 * depth:n_small].reshape(conv_w.shape), a[n_small])

    small_out = [unpack(a) for a in sm]

    moments = {"conv_in_w": (m_conv_in_w, v_conv_in_w), "conv_out_w": (m_conv_out_w, v_conv_out_w),
               "kv_w": (m_kv_w[None], v_kv_w[None]), "q_w": (m_q_w, v_q_w), "o_w": (m_o_w, v_o_w),
               "ffn_in_w": (swap(m_ffn_in_w), swap(v_ffn_in_w)), "ffn_out_w": (m_ffn_out_w, v_ffn_out_w)}
    landed = {k: [None] * big[k].shape[0] for k in names}
    for keys, tag, send_sems, recv_sems, parts, zones in scattering:
        parts, zones = _send_wait(send_sems, recv_sems, parts, zones, grad_x, "scatter_wait" + tag)
        for (k, i), part, zone in zip(keys, parts, zones):
            landed[k][i] = (part, zone)
    res = {k: _sum_adamw(slot, [p for p, _ in landed[k]], [z for _, z in landed[k]], big[k], moments[k][0], moments[k][1],
                         "adamw_" + k) for k in names}

    def big_out(k, which):
        out = res[k][which]
        return out[0] if k == "kv_w" else swap(out) if k == "ffn_in_w" else out

    out_names = ["norm_g", "conv_in_w", "conv_w", "conv_out_w", "kv_norm_g", "kv_w", "q_w", "o_w", "ffn_in_w", "ffn_out_w"]
    small_pos = {"norm_g": 0, "conv_w": 1, "kv_norm_g": 2}
    outs = [loss, grad_x[None]]
    for which in range(4):
        for k in out_names:
            outs.append(small_out[which][small_pos[k]] if k in small_pos else big_out(k, which))
    return tuple(outs)
```

```python
import functools
import math

import numpy as np
import jax
import jax.numpy as jnp
from jax import lax
from jax.experimental import pallas as pl
from jax.experimental.pallas import tpu as pltpu

F32 = jnp.float32
BF16 = jnp.bfloat16

N_DEV = 8
RMS_EPS = 1e-6
HEAD_DIM = 64
LANES = 128
ATT_BLOCK = 128
DILATIONS = (1, 4, 16)
SUPER = ATT_BLOCK * DILATIONS[-1]
NEG = -1e30

ADAM_LR, ADAM_B1, ADAM_B2, ADAM_EPS, ADAM_WD, ADAM_STEP = 0.001, 0.9, 0.999, 1e-08, 0.01, 10

ROW_TILE = 512
BIG_ROW_TILE = 1024
SWIGLU_ROWS = 256
GRAD_ROW_TILE = 2048
BWD_ROW_TILE = 512
MESH = pl.DeviceIdType.MESH


def _call(body, *, name, grid=None, in_specs=None, out_specs=None, out_shape=None, scratch_shapes=(), prefetch=False,
          **params):
    cp = pltpu.CompilerParams(**params) if params else None
    if prefetch:
        spec = pltpu.PrefetchScalarGridSpec(num_scalar_prefetch=1, grid=grid, in_specs=in_specs, out_specs=out_specs,
                                            scratch_shapes=list(scratch_shapes))
        return pl.pallas_call(body, name=name, grid_spec=spec, out_shape=out_shape, compiler_params=cp)
    kwargs = {k: v for k, v in (("grid", grid), ("in_specs", in_specs), ("out_specs", out_specs)) if v is not None}
    return pl.pallas_call(body, name=name, out_shape=out_shape, scratch_shapes=list(scratch_shapes),
                          compiler_params=cp, **kwargs)


def _sds(shape, dtype):
    return jax.ShapeDtypeStruct(tuple(shape), dtype)


def _rms(x, g):
    r = lax.rsqrt(jnp.mean(x * x, axis=-1, keepdims=True) + RMS_EPS)
    return x * r * g


def _rms_bwd(x, g, dy):
    r = lax.rsqrt(jnp.mean(x * x, axis=-1, keepdims=True) + RMS_EPS)
    xh = x * r
    dxh = dy * g
    dx = r * (dxh - xh * jnp.mean(dxh * xh, axis=-1, keepdims=True))
    return dx, jnp.sum(dy * xh, axis=0, keepdims=True)


def _dot(a, b):
    return jnp.dot(a, b, preferred_element_type=F32)


def _dot_nt(a, b):
    return lax.dot_general(a, b, (((1,), (1,)), ((), ())), preferred_element_type=F32)


def _dot_tn(a, b):
    return lax.dot_general(a, b, (((0,), (0,)), ((), ())), preferred_element_type=F32)


def _mesh_pos():
    return lax.axis_index("x"), lax.axis_index("y"), lax.axis_index("c")


def _all_gather(arrs, name):
    n = len(arrs)

    def body(*refs):
        ins, outs = refs[:n], refs[n:2 * n]
        send_sems, recv_sems, local_sems = refs[2 * n:]
        x, y, c = _mesh_pos()
        me, sibling = (x, y, c), (x, y, 1 - c)
        chips = [(1 - x, y), (x, 1 - y), (1 - x, 1 - y)]

        def copy(a, k, block, to, src=None):
            dst = outs[a].at[4 * block[0] + 2 * block[1] + block[2]]
            return pltpu.make_async_remote_copy(
                src_ref=dst if src is None else src, dst_ref=dst, send_sem=send_sems.at[a, k],
                recv_sem=recv_sems.at[a, k], device_id=to, device_id_type=MESH)

        started = []
        for a in range(n):
            mine = pltpu.make_async_copy(ins[a], outs[a].at[4 * x + 2 * y + c], local_sems.at[a])
            mine.start()
            started.append(mine)
        first = []
        for a in range(n):
            first.append(copy(a, 0, me, sibling, src=ins[a]))
            first += [copy(a, 1 + j, me, (*chip, c), src=ins[a]) for j, chip in enumerate(chips)]
        for cp in first:
            cp.start()
        passed = []
        for a in range(n):
            for j, chip in enumerate(chips):
                copy(a, 1 + j, (*chip, c), me).wait_recv()
                fwd = copy(a, 4 + j, (*chip, c), sibling)
                fwd.start()
                passed.append(fwd)
        for a in range(n):
            copy(a, 0, sibling, me).wait_recv()
            for j, chip in enumerate(chips):
                copy(a, 4 + j, (*chip, 1 - c), me).wait_recv()
        for cp in first + passed:
            cp.wait_send()
        for cp in started:
            cp.wait()

    any_spec = pl.BlockSpec(memory_space=pl.ANY)
    outs = _call(
        body, name=name, in_specs=[any_spec] * n, out_specs=[any_spec] * n,
        out_shape=[_sds((N_DEV,) + a.shape, a.dtype) for a in arrs],
        scratch_shapes=[pltpu.SemaphoreType.DMA((n, 7)), pltpu.SemaphoreType.DMA((n, 7)), pltpu.SemaphoreType.DMA((n,))],
        has_side_effects=True,
    )(*arrs)
    return list(outs)


HBM_SPEC = pl.BlockSpec(memory_space=pltpu.HBM)
SEM_SPEC = pl.BlockSpec(memory_space=pltpu.SEMAPHORE)
DATAFLOW = pltpu.SideEffectType.DATAFLOW_SIDE_EFFECTING
PEERS = [(dx, dy, dc) for dx in (0, 1) for dy in (0, 1) for dc in (0, 1)][1:]


def _peer(flip):
    x, y, c = _mesh_pos()
    return tuple(1 - v if f else v for v, f in zip((x, y, c), flip))


def _slot(pos):
    return 4 * pos[0] + 2 * pos[1] + pos[2]


def _in_hbm(a):
    return pltpu.with_memory_space_constraint(a, pltpu.HBM)


def _direct_copies(srcs, lands, send_sems, recv_sems, scatter):
    me = _slot(_mesh_pos())
    copies = []
    for a in range(len(lands)):
        for k, flip in enumerate(PEERS):
            peer = _peer(flip)
            src = srcs[a].at[_slot(peer)] if scatter else lands[a].at[me]
            idx = a * len(PEERS) + k
            copies.append(pltpu.make_async_remote_copy(
                src_ref=src, dst_ref=lands[a].at[me], send_sem=send_sems.at[idx], recv_sem=recv_sems.at[idx],
                device_id=peer, device_id_type=MESH))
    return copies


def _send_start(srcs, lands, after, name):
    ns, nl = len(srcs), len(lands)
    scatter = ns > 0

    def body(*refs):
        src_refs, land_refs = refs[:ns], refs[ns:ns + nl]
        send_sems, recv_sems = refs[ns + nl + 1:ns + nl + 3]
        token = refs[-1]
        for cp in _direct_copies(src_refs, land_refs, send_sems, recv_sems, scatter):
            cp.start()
        token[...] = jnp.zeros_like(token)

    sem = pltpu.SemaphoreType.DMA((nl * len(PEERS),))
    outs = pl.pallas_call(
        body, name=name,
        out_shape=(sem, sem) + tuple(pltpu.HBM(a.shape, a.dtype) for a in list(srcs) + list(lands))
        + (_sds((8, LANES), F32),),
        in_specs=[HBM_SPEC] * (ns + nl) + [pl.BlockSpec(memory_space=pl.ANY)],
        out_specs=(SEM_SPEC, SEM_SPEC) + (HBM_SPEC,) * (ns + nl) + (pl.BlockSpec(memory_space=pltpu.VMEM),),
        input_output_aliases={i: 2 + i for i in range(ns + nl)},
        compiler_params=pltpu.CompilerParams(has_side_effects=DATAFLOW),
    )(*[_in_hbm(a) for a in list(srcs) + list(lands)], after)
    send_sems, recv_sems = outs[0], outs[1]
    return send_sems, recv_sems, list(outs[2:2 + ns]), list(outs[2 + ns:2 + ns + nl]), outs[-1]


def _send_wait(send_sems, recv_sems, srcs, lands, after, name):
    ns, nl = len(srcs), len(lands)
    scatter = ns > 0

    def body(*refs):
        src_refs, land_refs = refs[:ns], refs[ns:ns + nl]
        send_sems, recv_sems = refs[ns + nl:ns + nl + 2]
        copies = _direct_copies(src_refs, land_refs, send_sems, recv_sems, scatter)
        for cp in copies:
            cp.wait_send()
        for cp in copies:
            cp.wait_recv()

    outs = pl.pallas_call(
        body, name=name,
        out_shape=tuple(pltpu.HBM(a.shape, a.dtype) for a in list(srcs) + list(lands)),
        in_specs=[HBM_SPEC] * (ns + nl) + [SEM_SPEC, SEM_SPEC, pl.BlockSpec(memory_space=pl.ANY)],
        out_specs=(HBM_SPEC,) * (ns + nl),
        input_output_aliases={i: i for i in range(ns + nl)},
        compiler_params=pltpu.CompilerParams(has_side_effects=DATAFLOW),
    )(*srcs, *lands, send_sems, recv_sems, after)
    return list(outs[:ns]), list(outs[ns:])


def _row_tile(rows, cap=512):
    t = min(rows, cap)
    while rows % t or (t % 16 and t != rows):
        t -= 1
    return t


def _as2d(a):
    return a.reshape(-1, a.shape[-1])


def _cast_layer(w, layer, slot, name):
    _, rows, cols = w.shape
    tr = _row_tile(rows)

    def body(*refs):
        refs[-1][...] = refs[-2][...].astype(BF16)

    if slot is None:
        return _call(body, name=name, grid=(rows // tr,),
                     in_specs=[pl.BlockSpec((None, tr, cols), lambda i: (layer, i, 0))],
                     out_specs=pl.BlockSpec((tr, cols), lambda i: (i, 0)), out_shape=_sds((rows, cols), BF16))(w)
    return _call(body, name=name, grid=(rows // tr,), prefetch=True,
                 in_specs=[pl.BlockSpec((None, tr, cols), lambda i, s: (layer, i, 0))],
                 out_specs=pl.BlockSpec((None, tr, cols), lambda i, s: (s[0], i, 0)),
                 out_shape=_sds((N_DEV, rows, cols), BF16))(slot, w)


def _sum_adamw(slot, parts, lands, w, m, v, name):
    n_l = len(parts)
    _, rows, cols = lands[0].shape
    tr = _row_tile(rows, 128)

    def body(s_ref, *refs):
        p_refs, l_refs = refs[:n_l], refs[n_l:2 * n_l]
        w_ref, m_ref, v_ref, g_ref, d_ref, nm_ref, nv_ref = refs[2 * n_l:]
        for k in range(n_l):
            @pl.when(pl.program_id(0) == k)
            def _(k=k):
                own = p_refs[k][...]
                g = jnp.zeros((tr, cols), F32)
                for j in range(N_DEV):
                    g = g + jnp.where(s_ref[0] == j, own, l_refs[k][j]).astype(F32)
                delta, nm, nv = _adamw_math(w_ref[...], g, m_ref[...], v_ref[...])
                g_ref[...] = g
                d_ref[...] = delta
                nm_ref[...] = nm
                nv_ref[...] = nv

    def own_block(k):
        return pl.BlockSpec((None, tr, cols), lambda l, i, s: (s[0], jnp.where(l == k, i, 0), 0))

    def zone_block(k):
        return pl.BlockSpec((N_DEV, tr, cols), lambda l, i, s: (0, jnp.where(l == k, i, 0), 0))

    lay = pl.BlockSpec((None, tr, cols), lambda l, i, s: (l, i, 0))
    return _call(body, name=name, grid=(n_l, rows // tr), prefetch=True,
                 in_specs=[own_block(k) for k in range(n_l)] + [zone_block(k) for k in range(n_l)] + [lay, lay, lay],
                 out_specs=[lay] * 4, out_shape=[_sds((n_l, rows, cols), F32)] * 4)(slot, *parts, *lands, w, m, v)


def _adamw_math(w, g, m, v):
    m = ADAM_B1 * m + (1.0 - ADAM_B1) * g
    v = ADAM_B2 * v + (1.0 - ADAM_B2) * (g * g)
    m_hat = m / (1.0 - ADAM_B1 ** ADAM_STEP)
    v_hat = v / (1.0 - ADAM_B2 ** ADAM_STEP)
    delta = -ADAM_LR * (m_hat / (jnp.sqrt(v_hat) + ADAM_EPS) + ADAM_WD * w)
    return delta, m, v


def _small_adamw(gathered, w, m, v, name):
    def body(a_ref, w_ref, m_ref, v_ref, g_ref, d_ref, nm_ref, nv_ref):
        g = a_ref[0]
        for k in range(1, N_DEV):
            g = g + a_ref[k]
        delta, nm, nv = _adamw_math(w_ref[...], g, m_ref[...], v_ref[...])
        g_ref[...] = g
        d_ref[...] = delta
        nm_ref[...] = nm
        nv_ref[...] = nv

    return _call(body, name=name, out_shape=[_sds(w.shape, F32)] * 4)(gathered, w, m, v)


def _norm_matmul_cols(x, g, wg, layer, mode, name):
    t, d = x.shape
    nb = wg.shape[-1]
    tm = BIG_ROW_TILE
    per = 2
    pieces = per * nb // LANES

    def body(x_ref, g_ref, w_ref, y_ref, xnt_ref, xn_ref):
        @pl.when(pl.program_id(1) == 0)
        def _():
            xn = _rms(x_ref[...], g_ref[...])
            xn_ref[...] = xn.astype(BF16)
            xnt_ref[...] = xn.T.astype(BF16)

        y = _dot(xn_ref[...], jnp.concatenate([w_ref[k] for k in range(per)], axis=1))
        if mode == "heads":
            for p in range(pieces):
                y_ref[p] = y[:, p * LANES:(p + 1) * LANES]
        else:
            y_ref[...] = y.astype(BF16)

    if mode == "cols":
        y_shape, y_spec = _sds((t, N_DEV * nb), BF16), pl.BlockSpec((tm, per * nb), lambda i, j: (i, j))
    else:
        y_shape = _sds((N_DEV // per * pieces, t, LANES), F32)
        y_spec = pl.BlockSpec((pieces, tm, LANES), lambda i, j: (j, i, 0))
    return _call(
        body, name=name, grid=(t // tm, N_DEV // per),
        in_specs=[pl.BlockSpec((tm, d), lambda i, j: (i, 0)), pl.BlockSpec((1, d), lambda i, j: (0, 0)),
                  pl.BlockSpec((per, None, d, nb), lambda i, j: (j, layer, 0, 0))],
        out_specs=[y_spec, pl.BlockSpec((d, tm), lambda i, j: (0, i))],
        out_shape=[y_shape, _sds((d, t), BF16)], scratch_shapes=[pltpu.VMEM((tm, d), BF16)])(x, g, wg)


def _ffn_in_swiglu(x, g, wg, layer, name):
    t, d = x.shape
    fc = wg.shape[-2]
    tm = BIG_ROW_TILE

    def body(x_ref, g_ref, wg_ref, wu_ref, gate_ref, up_ref, a_ref, xn_ref):
        @pl.when(pl.program_id(1) == 0)
        def _():
            xn_ref[...] = _rms(x_ref[...], g_ref[...]).astype(BF16)

        xn = xn_ref[...]
        gate, up = _dot_nt(xn, wg_ref[...]), _dot_nt(xn, wu_ref[...])
        gate_ref[...] = gate.astype(BF16)
        up_ref[...] = up.astype(BF16)
        a_ref[...] = (gate * jax.nn.sigmoid(gate) * up).astype(BF16)

    chunk = pl.BlockSpec((None, tm, fc), lambda i, c: (c, i, 0))
    return _call(
        body, name=name, grid=(t // tm, 4),
        in_specs=[pl.BlockSpec((tm, d), lambda i, c: (i, 0)), pl.BlockSpec((1, d), lambda i, c: (0, 0)),
                  pl.BlockSpec((None, None, fc, d), lambda i, c: (c, layer, 0, 0)),
                  pl.BlockSpec((None, None, fc, d), lambda i, c: (c + 4, layer, 0, 0))],
        out_specs=[chunk, chunk, chunk, pl.BlockSpec((tm, d), lambda i, c: (i, 0))],
        out_shape=[_sds((4, t, fc), BF16)] * 3 + [_sds((t, d), BF16)])(x, g, wg, wg)


def _norm_matmul_heads(x, g, wg, layer, scale, name):
    t, d = x.shape
    tm = ROW_TILE
    hp = d // LANES

    def body(x_ref, g_ref, w_ref, y_ref, xn_ref):
        xn = _rms(x_ref[...], g_ref[...]).astype(BF16)
        xn_ref[...] = xn
        y = _dot(xn, w_ref[...].reshape(d, d)) * scale
        for p in range(hp):
            y_ref[p] = y[:, p * LANES:(p + 1) * LANES]

    return _call(
        body, name=name, grid=(t // tm,),
        in_specs=[pl.BlockSpec((tm, d), lambda i: (i, 0)), pl.BlockSpec((1, d), lambda i: (0, 0)),
                  pl.BlockSpec((N_DEV, None, d // N_DEV, d), lambda i: (0, layer, 0, 0))],
        out_specs=[pl.BlockSpec((hp, tm, LANES), lambda i: (0, i, 0)), pl.BlockSpec((tm, d), lambda i: (i, 0))],
        out_shape=[_sds((hp, t, LANES), F32), _sds((t, d), BF16)])(x, g, wg)


def _shift_down(u, halo, k, tm):
    row = lax.broadcasted_iota(jnp.int32, u.shape, 0)
    out = pltpu.roll(u, k, 0)
    for j in range(k):
        out = jnp.where(row == j, halo[halo.shape[0] - k + j:halo.shape[0] - k + j + 1, :], out)
    return out


def _shift_up(u, halo, k, tm):
    row = lax.broadcasted_iota(jnp.int32, u.shape, 0)
    out = pltpu.roll(u, tm - k, 0)
    for j in range(k):
        out = jnp.where(row == tm - k + j, halo[j:j + 1, :], out)
    return out


HALO = 16


def _conv_fwd(p, cw, name):
    t, d3 = p.shape
    d = d3 // 3
    tm = ROW_TILE
    hb = tm // HALO

    def body(p_ref, prev_ref, cw_ref, z_ref):
        i = pl.program_id(0)
        b = p_ref[:, 0:d].astype(F32)
        u = p_ref[:, d:2 * d].astype(F32) * p_ref[:, 2 * d:3 * d].astype(F32)
        keep = (i > 0).astype(F32)
        hu = prev_ref[:, d:2 * d].astype(F32) * prev_ref[:, 2 * d:3 * d].astype(F32) * keep
        uc = cw_ref[2:3, :] * u + cw_ref[1:2, :] * _shift_down(u, hu, 1, tm) + cw_ref[0:1, :] * _shift_down(u, hu, 2, tm)
        z_ref[...] = (b * uc).astype(BF16)

    return _call(
        body, name=name, grid=(t // tm,),
        in_specs=[pl.BlockSpec((tm, d3), lambda i: (i, 0)),
                  pl.BlockSpec((HALO, d3), lambda i: (jnp.maximum(i * hb - 1, 0), 0)),
                  pl.BlockSpec((3, d), lambda i: (0, 0))],
        out_specs=pl.BlockSpec((tm, d), lambda i: (i, 0)), out_shape=_sds((t, d), BF16))(p, p, cw)


def _matmul_norm_residual(a3, wg, layer, g, x_res, name):
    kc_n, t, kc = a3.shape
    d = wg.shape[-1]
    per = N_DEV // kc_n
    rows = wg.shape[2]
    tm = ROW_TILE

    def body(a_ref, w_ref, g_ref, x_ref, raw_ref, xo_ref):
        raw = None
        for c in range(kc_n):
            term = _dot(a_ref[c], w_ref[c * per:(c + 1) * per].reshape(per * rows, d))
            raw = term if raw is None else raw + term
        raw_ref[...] = raw
        xo_ref[...] = x_ref[...] + _rms(raw, g_ref[...])

    row_spec = pl.BlockSpec((tm, d), lambda i: (i, 0))
    return _call(
        body, name=name, grid=(t // tm,),
        in_specs=[pl.BlockSpec((kc_n, tm, kc), lambda i: (0, i, 0)),
                  pl.BlockSpec((N_DEV, None, rows, d), lambda i: (0, layer, 0, 0)),
                  pl.BlockSpec((1, d), lambda i: (0, 0)), row_spec],
        out_specs=[row_spec, row_spec], out_shape=[_sds((t, d), F32)] * 2)(a3, wg, g, x_res)


def _alibi_slopes(n_heads):
    hh = np.arange(n_heads, dtype=np.float32) + 1.0
    s = np.power(2.0, -8.0 * hh / n_heads).astype(np.float32)
    return jnp.asarray(np.repeat(s.reshape(n_heads // 2, 2, 1), 2 * ATT_BLOCK, axis=2))


def _band_bias(sl_ref, dil):
    u = lax.broadcasted_iota(jnp.int32, (ATT_BLOCK, 2 * ATT_BLOCK), 0)
    kk = lax.broadcasted_iota(jnp.int32, (ATT_BLOCK, 2 * ATT_BLOCK), 1)
    delta = u + ATT_BLOCK - kk
    valid = (delta >= 0) & (delta <= ATT_BLOCK)
    dist = (delta * dil).astype(F32)
    rows = [jnp.where(valid, -sl_ref[hd:hd + 1, :] * dist, NEG) for hd in range(2)]
    return jnp.concatenate(rows, axis=0)


def _stack_heads(a):
    lane = lax.broadcasted_iota(jnp.int32, a.shape, 1)
    return jnp.concatenate([jnp.where(lane < HEAD_DIM, a, 0.0), jnp.where(lane >= HEAD_DIM, a, 0.0)], axis=0).astype(BF16)


def _unstack_heads(a2):
    top, bot = a2[:ATT_BLOCK], a2[ATT_BLOCK:]
    lane = lax.broadcasted_iota(jnp.int32, top.shape, 1)
    return jnp.where(lane < HEAD_DIM, top, bot)


def _rows_to_lanes(a0, a1):
    eye = lax.broadcasted_iota(jnp.int32, a0.shape, 0) == lax.broadcasted_iota(jnp.int32, a0.shape, 1)
    return jnp.concatenate([jnp.sum(jnp.where(eye, a, 0.0), axis=0, keepdims=True) for a in (a0, a1)], axis=1)


def _fill_bias_t(sl_ref, bias_ref):
    kk = lax.broadcasted_iota(jnp.int32, (2 * ATT_BLOCK, 2 * ATT_BLOCK), 0)
    lane = lax.broadcasted_iota(jnp.int32, (2 * ATT_BLOCK, 2 * ATT_BLOCK), 1)
    delta = lane % ATT_BLOCK + ATT_BLOCK - kk
    valid = (delta >= 0) & (delta <= ATT_BLOCK)
    slope = jnp.concatenate([sl_ref[0:1, :ATT_BLOCK], sl_ref[1:2, :ATT_BLOCK]], axis=1)
    for gi, dil in enumerate(DILATIONS):
        bias = jnp.where(valid, -slope * (delta * dil).astype(F32), NEG)
        bias_ref[2 * gi] = bias
        bias_ref[2 * gi + 1] = jnp.where(kk < ATT_BLOCK, NEG, bias)


def _fill_bias(sl_ref, bias_ref):
    kk = lax.broadcasted_iota(jnp.int32, (2 * ATT_BLOCK, 2 * ATT_BLOCK), 1)
    for gi, dil in enumerate(DILATIONS):
        bias = _band_bias(sl_ref, dil)
        bias_ref[2 * gi] = bias
        bias_ref[2 * gi + 1] = jnp.where(kk < ATT_BLOCK, NEG, bias)


def _strided_keys(dil, r, b, kc_ref, kp_ref, vc_ref, vp_ref, kcar_ref, vcar_ref):
    if b > 0:
        keys = pl.ds((b - 1) * (ATT_BLOCK * dil) + r, 2 * ATT_BLOCK, stride=dil)
        return kc_ref[keys, :].astype(BF16), vc_ref[keys, :].astype(BF16)
    own = pl.ds(r, ATT_BLOCK, stride=dil)
    k_own, v_own = kc_ref[own, :].astype(BF16), vc_ref[own, :].astype(BF16)
    if dil * ATT_BLOCK != SUPER:
        before = pl.ds(SUPER - ATT_BLOCK * dil + r, ATT_BLOCK, stride=dil)
        k_before, v_before = kp_ref[before, :].astype(BF16), vp_ref[before, :].astype(BF16)
    else:
        k_before, v_before = kcar_ref[r], vcar_ref[r]
        kcar_ref[r] = k_own
        vcar_ref[r] = v_own
    return jnp.concatenate([k_before, k_own], axis=0), jnp.concatenate([v_before, v_own], axis=0)


def _attention_fwd(q, kv, slopes, name):
    hp, t, _ = q.shape
    ns = t // SUPER
    nd = len(DILATIONS)

    def body(sl_ref, q_ref, kc_ref, kp_ref, vc_ref, vp_ref, o_ref, lse_ref, og_ref, lg_ref, bias_ref, kcar_ref, vcar_ref):
        n = pl.program_id(1)

        @pl.when(n == 0)
        def _():
            _fill_bias(sl_ref, bias_ref)
            kcar_ref[...] = jnp.zeros_like(kcar_ref)
            vcar_ref[...] = jnp.zeros_like(vcar_ref)

        for gi, dil in enumerate(DILATIONS):
            for idx in range(SUPER // ATT_BLOCK):
                r, b = idx % dil, idx // dil
                qs = b * (ATT_BLOCK * dil) + r
                first = (n == 0).astype(jnp.int32) if b == 0 else 0
                q2 = _stack_heads(q_ref[pl.ds(qs, ATT_BLOCK, stride=dil), :])
                kb, vb = _strided_keys(dil, r, b, kc_ref, kp_ref, vc_ref, vp_ref, kcar_ref, vcar_ref)
                s = _dot_nt(q2, kb) + bias_ref[2 * gi + first]
                m = jnp.max(s, axis=-1, keepdims=True)
                p = jnp.exp(s - m).astype(BF16)
                ol = _dot(p, jnp.concatenate([vb, jnp.ones_like(vb)], axis=1))
                l = ol[:, LANES:]
                o2 = ol[:, :LANES] / l
                lse2 = m + jnp.log(l)
                og_ref[gi, pl.ds(qs, ATT_BLOCK, stride=dil), :] = _unstack_heads(o2)
                lg_ref[gi, pl.ds(qs, ATT_BLOCK, stride=dil), :] = _unstack_heads(lse2)
        lg = [lg_ref[gi] for gi in range(nd)]
        top = functools.reduce(jnp.maximum, lg)
        ws = [jnp.exp(x - top) for x in lg]
        tot = functools.reduce(jnp.add, ws)
        lse_ref[...] = top + jnp.log(tot)
        acc = ws[0] * og_ref[0]
        for gi in range(1, nd):
            acc = acc + ws[gi] * og_ref[gi]
        o_ref[...] = (acc / tot).astype(BF16)

    cur = lambda off: pl.BlockSpec((None, SUPER, LANES), lambda h, n: (h + off, n, 0))
    prev = lambda off: pl.BlockSpec((None, SUPER, LANES), lambda h, n: (h + off, jnp.maximum(n - 1, 0), 0))
    return _call(
        body, name=name, grid=(hp, ns),
        in_specs=[pl.BlockSpec((None, 2, 2 * ATT_BLOCK), lambda h, n: (h, 0, 0)), cur(0), cur(0), prev(0), cur(hp), prev(hp)],
        out_specs=[pl.BlockSpec((SUPER, LANES), lambda h, n: (n, h)), cur(0)],
        out_shape=[_sds((t, hp * LANES), BF16), _sds((hp, t, LANES), F32)],
        scratch_shapes=[pltpu.VMEM((nd, SUPER, LANES), F32), pltpu.VMEM((nd, SUPER, LANES), F32),
                        pltpu.VMEM((2 * nd, 2 * ATT_BLOCK, 2 * ATT_BLOCK), F32)] + [
                            pltpu.VMEM((DILATIONS[-1], ATT_BLOCK, LANES), BF16)] * 2,
    )(slopes, q, kv, kv, kv, kv)


def _attention_bwd(q, kv, o, lse, d_o, dk_in, dv_in, slopes, q_scale, name):
    hp, t, _ = q.shape
    ns = t // SUPER
    shared = dk_in is not None

    def body(sl_ref, q_ref, kc_ref, kp_ref, vc_ref, vp_ref, o_ref, lse_ref, do_ref, *rest):
        dki_ref, dvi_ref = rest[:2] if shared else (None, None)
        dq_ref, dk_ref, dv_ref, dkw_ref, dvw_ref, st_ref, bias_ref, kcar_ref, vcar_ref = rest[2 if shared else 0:]
        n = pl.program_id(1)

        @pl.when(n == 0)
        def _():
            dkw_ref[...] = jnp.zeros_like(dkw_ref)
            dvw_ref[...] = jnp.zeros_like(dvw_ref)

        @pl.when(n > 0)
        def _():
            dkw_ref[0:SUPER, :] = dkw_ref[SUPER:, :]
            dvw_ref[0:SUPER, :] = dvw_ref[SUPER:, :]
            dkw_ref[SUPER:, :] = jnp.zeros((SUPER, LANES), F32)
            dvw_ref[SUPER:, :] = jnp.zeros((SUPER, LANES), F32)

        @pl.when(n < ns)
        def _():
            prod = do_ref[...] * o_ref[...].astype(F32)
            lane = lax.broadcasted_iota(jnp.int32, prod.shape, 1)
            zero = jnp.zeros((SUPER, LANES), F32)
            st_ref[0] = zero + jnp.sum(jnp.where(lane < HEAD_DIM, prod, 0.0), axis=-1, keepdims=True)
            st_ref[1] = zero + jnp.sum(jnp.where(lane >= HEAD_DIM, prod, 0.0), axis=-1, keepdims=True)
            lse = lse_ref[...]
            swapped = pltpu.roll(lse, HEAD_DIM, 1)
            st_ref[2] = jnp.where(lane < HEAD_DIM, lse, swapped)
            st_ref[3] = jnp.where(lane >= HEAD_DIM, lse, swapped)
            dq_ref[...] = jnp.zeros_like(dq_ref)

            @pl.when(n == 0)
            def _():
                _fill_bias_t(sl_ref, bias_ref)
                kcar_ref[...] = jnp.zeros_like(kcar_ref)
                vcar_ref[...] = jnp.zeros_like(vcar_ref)

            for gi, dil in enumerate(DILATIONS):
                for idx in range(SUPER // ATT_BLOCK):
                    r, b = idx % dil, idx // dil
                    qs = b * (ATT_BLOCK * dil) + r
                    ks = SUPER + (b - 1) * (ATT_BLOCK * dil) + r
                    first = (n == 0).astype(jnp.int32) if b == 0 else 0
                    rows = pl.ds(qs, ATT_BLOCK, stride=dil)
                    keys = pl.ds(ks, 2 * ATT_BLOCK, stride=dil)
                    q2 = _stack_heads(q_ref[rows, :])
                    do2 = _stack_heads(do_ref[rows, :])
                    kb, vb = _strided_keys(dil, r, b, kc_ref, kp_ref, vc_ref, vp_ref, kcar_ref, vcar_ref)
                    dd = _rows_to_lanes(st_ref[0, rows, :], st_ref[1, rows, :])
                    lse_b = _rows_to_lanes(st_ref[2, rows, :], st_ref[3, rows, :])
                    ps, dss = [], []
                    for half in range(2):
                        hk = slice(half * ATT_BLOCK, (half + 1) * ATT_BLOCK)
                        p = jnp.exp(_dot_nt(kb[hk], q2) + bias_ref[2 * gi + first, hk, :] - lse_b)
                        dss.append((p * (_dot_nt(vb[hk], do2) - dd)).astype(BF16))
                        ps.append(p.astype(BF16))
                    p, ds = jnp.concatenate(ps, axis=0), jnp.concatenate(dss, axis=0)
                    dvw_ref[keys, :] += _dot(p, do2)
                    dkw_ref[keys, :] += _dot(ds, q2)
                    dq_ref[rows, :] += _unstack_heads(_dot_tn(ds, kb)) * q_scale

        dk_ref[...] =dkw_ref[0:SUPER, :] + dki_ref[...] if shared else dkw_ref[0:SUPER, :]
        dv_ref[...] = dvw_ref[0:SUPER, :] + dvi_ref[...] if shared else dvw_ref[0:SUPER, :]

    last = ns - 1
    cur = lambda off: pl.BlockSpec((None, SUPER, LANES), lambda h, n: (h + off, jnp.minimum(n, last), 0))
    prev = lambda off: pl.BlockSpec((None, SUPER, LANES), lambda h, n: (h + off, jnp.clip(n - 1, 0, last), 0))
    nat = pl.BlockSpec((SUPER, LANES), lambda h, n: (jnp.minimum(n, last), h))
    late = pl.BlockSpec((None, SUPER, LANES), lambda h, n: (h, jnp.maximum(n - 1, 0), 0))
    dq, dk, dv = _call(
        body, name=name, grid=(hp, ns + 1),
        in_specs=[pl.BlockSpec((None, 2, 2 * ATT_BLOCK), lambda h, n: (h, 0, 0)), cur(0), cur(0), prev(0), cur(hp), prev(hp),
                  nat, cur(0), nat] + ([late, late] if shared else []),
        out_specs=[cur(0), late, late],
        out_shape=[_sds((hp, t, LANES), F32)] * 3,
        scratch_shapes=[pltpu.VMEM((2 * SUPER, LANES), F32)] * 2 + [
            pltpu.VMEM((4, SUPER, LANES), F32), pltpu.VMEM((2 * len(DILATIONS), 2 * ATT_BLOCK, 2 * ATT_BLOCK), F32)] + [
                pltpu.VMEM((DILATIONS[-1], ATT_BLOCK, LANES), BF16)] * 2,
    )(slopes, q, kv, kv, kv, kv, o, lse, d_o, *((dk_in, dv_in) if shared else ()))
    return dq, dk, dv


def _loss_head(y, target, raw, g, name):
    t, d = y.shape
    tm = ROW_TILE

    def body(y_ref, t_ref, raw_ref, g_ref, sq_ref, dy_ref, draw_ref, dg_ref):
        i = pl.program_id(0)
        err = y_ref[...] - t_ref[...]
        dy = err * (1.0 / d)
        dy_ref[...] = dy
        draw, dg = _rms_bwd(raw_ref[...], g_ref[...], dy)
        draw_ref[...] = draw.astype(BF16)
        sq = jnp.zeros((8, LANES), F32) + jnp.sum(err * err)

        @pl.when(i == 0)
        def _():
            sq_ref[...] = sq
            dg_ref[...] = dg

        @pl.when(i > 0)
        def _():
            sq_ref[...] += sq
            dg_ref[...] += dg

    row = pl.BlockSpec((tm, d), lambda i: (i, 0))
    vec = pl.BlockSpec((1, d), lambda i: (0, 0))
    return _call(
        body, name=name, grid=(t // tm,), in_specs=[row, row, row, vec],
        out_specs=[pl.BlockSpec((8, LANES), lambda i: (0, 0)), row, row, vec],
        out_shape=[_sds((8, LANES), F32), _sds((t, d), F32), _sds((t, d), BF16), _sds((1, d), F32)])(y, target, raw, g)


def _bwd_matmul_norms(a_specs, a_args, a_tile, n_steps, w_spec, w_arg, w_mat, xa, ga, resid, xb, gb, name,
                      w_transposed=False):
    t, d = xa.shape
    tm = BWD_ROW_TILE
    na = len(a_specs)
    second = xb is not None
    per = 4 if n_steps % 4 == 0 else 1
    n_steps //= per

    def blocks_of(spec, k):
        return pl.BlockSpec(spec.block_shape, lambda i, j: spec.index_map(i, per * j + k))

    def body(*refs):
        a_refs, w_refs = refs[:per * na], refs[per * na:per * na + per]
        xa_ref, ga_ref, res_ref = refs[per * na + per:per * na + per + 3]
        rest = refs[per * na + per + 3:]
        if second:
            xb_ref, gb_ref, dx_ref, d2_ref, dga_ref, dgb_ref, acc_ref = rest
        else:
            dx_ref, dga_ref, acc_ref = rest
        i, j = pl.program_id(0), pl.program_id(1)
        part = None
        for k in range(per):
            term = (_dot if w_transposed else _dot_nt)(a_tile(per * j + k, *a_refs[k * na:(k + 1) * na]), w_mat(w_refs[k]))
            part = term if part is None else part + term

        @pl.when(j == 0)
        def _():
            acc_ref[...] = part

        @pl.when(j > 0)
        def _():
            acc_ref[...] += part

        @pl.when(j == n_steps - 1)
        def _():
            da, dga = _rms_bwd(xa_ref[...], ga_ref[...], acc_ref[...])
            dx = res_ref[...] + da
            dx_ref[...] = dx
            if second:
                d2, dgb = _rms_bwd(xb_ref[...], gb_ref[...], dx)
                d2_ref[...] = d2.astype(BF16)

            @pl.when(i == 0)
            def _():
                dga_ref[...] = dga
                if second:
                    dgb_ref[...] = dgb

            @pl.when(i > 0)
            def _():
                dga_ref[...] += dga
                if second:
                    dgb_ref[...] += dgb

    row = pl.BlockSpec((tm, d), lambda i, j: (i, 0))
    vec = pl.BlockSpec((1, d), lambda i, j: (0, 0))
    in_specs = [blocks_of(sp, k) for k in range(per) for sp in a_specs] + [blocks_of(w_spec, k) for k in range(per)]
    in_specs += [row, vec, row]
    args = list(a_args) * per + [w_arg] * per + [xa, ga, resid]
    if second:
        in_specs += [row, vec]
        args += [xb, gb]
        out_specs = [row, row, vec, vec]
        out_shape = [_sds((t, d), F32), _sds((t, d), BF16), _sds((1, d), F32), _sds((1, d), F32)]
    else:
        out_specs = [row, vec]
        out_shape = [_sds((t, d), F32), _sds((1, d), F32)]
    return _call(body, name=name, grid=(t // tm, n_steps), in_specs=in_specs, out_specs=out_specs,
                 out_shape=out_shape, scratch_shapes=[pltpu.VMEM((tm, d), F32)])(*args)


def _heads_to_rows(*refs):
    hp = refs[0].shape[0]
    cols = []
    for p in range(hp):
        v = refs[0][p]
        for r in refs[1:]:
            v = v + r[p]
        cols.append(v)
    return jnp.concatenate(cols, axis=-1).astype(BF16)


def _matmul_nt_rows(a, wg, layer, out_dtype, name):
    t, d = a.shape
    tm = ROW_TILE

    def body(a_ref, w_ref, o_ref):
        o_ref[...] = _dot_nt(a_ref[...], w_ref[...].reshape(d, d)).astype(out_dtype)

    row = pl.BlockSpec((tm, d), lambda i: (i, 0))
    return _call(body, name=name, grid=(t // tm,),
                 in_specs=[row, pl.BlockSpec((N_DEV, None, d // N_DEV, d), lambda i: (0, layer, 0, 0))],
                 out_specs=row, out_shape=_sds((t, d), out_dtype))(a, wg)


def _swiglu_bwd(d_ff, wg, layer, gate, up, name):
    t, d = d_ff.shape
    fc = gate.shape[-1]
    rows = wg.shape[2]
    tm = BIG_ROW_TILE

    def body(df_ref, w_ref, g_ref, u_ref, dh_ref):
        w = w_ref[...].reshape(2 * rows, d)
        for r0 in range(0, tm, SWIGLU_ROWS):
            rs = slice(r0, r0 + SWIGLU_ROWS)
            da = _dot_nt(df_ref[rs, :], w)
            gate, up = g_ref[rs, :].astype(F32), u_ref[rs, :].astype(F32)
            sig = jax.nn.sigmoid(gate)
            dh_ref[0, rs, :] = (da * up * (sig * (1.0 + gate * (1.0 - sig)))).astype(BF16)
            dh_ref[1, rs, :] = (da * (gate * sig)).astype(BF16)

    return _call(
        body, name=name, grid=(t // tm, 4),
        in_specs=[pl.BlockSpec((tm, d), lambda i, c: (i, 0)),
                  pl.BlockSpec((2, None, rows, d), lambda i, c: (c, layer, 0, 0)),
                  pl.BlockSpec((None, tm, fc), lambda i, c: (c, i, 0)),
                  pl.BlockSpec((None, tm, fc), lambda i, c: (c, i, 0))],
        out_specs=pl.BlockSpec((None, 2, tm, fc), lambda i, c: (c, 0, i, 0)),
        out_shape=_sds((4, 2, t, fc), BF16))(d_ff, wg, gate, up)


def _conv_bwd(p, d_z, cw, name):
    t, d3 = p.shape
    d = d3 // 3
    tm = ROW_TILE
    hb = tm // HALO
    nt = t // tm

    def body(p_ref, prev_ref, next_ref, dz_ref, dzn_ref, cw_ref, dp_ref, dcw_ref):
        i = pl.program_id(0)
        b = p_ref[:, 0:d].astype(F32)
        c = p_ref[:, d:2 * d].astype(F32)
        h = p_ref[:, 2 * d:3 * d].astype(F32)
        u = c * h
        hu = prev_ref[:, d:2 * d].astype(F32) * prev_ref[:, 2 * d:3 * d].astype(F32) * (i > 0).astype(F32)
        u1, u2 = _shift_down(u, hu, 1, tm), _shift_down(u, hu, 2, tm)
        uc = cw_ref[2:3, :] * u + cw_ref[1:2, :] * u1 + cw_ref[0:1, :] * u2
        dz = dz_ref[...].astype(F32)
        duc = dz * b
        dn = dzn_ref[...].astype(F32) * next_ref[:, 0:d].astype(F32) * (i < nt - 1).astype(F32)
        du = cw_ref[2:3, :] * duc + cw_ref[1:2, :] * _shift_up(duc, dn, 1, tm) + cw_ref[0:1, :] * _shift_up(duc, dn, 2, tm)
        dp_ref[:, 0:d] = (dz * uc).astype(BF16)
        dp_ref[:, d:2 * d] = (du * h).astype(BF16)
        dp_ref[:, 2 * d:3 * d] = (du * c).astype(BF16)
        dcw = jnp.concatenate([jnp.sum(duc * u2, axis=0, keepdims=True), jnp.sum(duc * u1, axis=0, keepdims=True),
                               jnp.sum(duc * u, axis=0, keepdims=True)], axis=0)

        @pl.when(i == 0)
        def _():
            dcw_ref[...] = dcw

        @pl.when(i > 0)
        def _():
            dcw_ref[...] += dcw

    last_halo = t // HALO - 1
    return _call(
        body, name=name, grid=(nt,),
        in_specs=[pl.BlockSpec((tm, d3), lambda i: (i, 0)),
                  pl.BlockSpec((HALO, d3), lambda i: (jnp.maximum(i * hb - 1, 0), 0)),
                  pl.BlockSpec((HALO, d3), lambda i: (jnp.minimum((i + 1) * hb, last_halo), 0)),
                  pl.BlockSpec((tm, d), lambda i: (i, 0)),
                  pl.BlockSpec((HALO, d), lambda i: (jnp.minimum((i + 1) * hb, last_halo), 0)),
                  pl.BlockSpec((3, d), lambda i: (0, 0))],
        out_specs=[pl.BlockSpec((tm, d3), lambda i: (i, 0)), pl.BlockSpec((3, d), lambda i: (0, 0))],
        out_shape=[_sds((t, d3), BF16), _sds((3, d), F32)])(p, p, p, d_z, d_z, cw)


def _grad_weight(a_specs, a_args, a_tile, b_specs, b_args, b_tile, n_out, acc_shape, out_spec, out_shape, t, name,
                 a_transposed=False):
    tt = GRAD_ROW_TILE
    na, nb = len(a_specs), len(b_specs)

    def body(*refs):
        a_refs, b_refs = refs[:na], refs[na:na + nb]
        o_ref, acc_ref = refs[na + nb:]
        s = pl.program_id(1)
        a, b = a_tile(pl.program_id(0), *a_refs), b_tile(pl.program_id(0), *b_refs)
        part = _dot(a, b) if a_transposed else _dot_tn(a, b)

        @pl.when(s == 0)
        def _():
            acc_ref[...] = part

        @pl.when(s > 0)
        def _():
            acc_ref[...] += part

        @pl.when(s == t // tt - 1)
        def _():
            acc = acc_ref[...].astype(BF16)
            if o_ref.shape[-1] == acc.shape[-1]:
                o_ref[...] = acc.reshape(o_ref.shape)
            else:
                for k in range(o_ref.shape[0]):
                    o_ref[k] = acc[:, k * o_ref.shape[-1]:(k + 1) * o_ref.shape[-1]]

    return _call(body, name=name, grid=(n_out, t // tt), in_specs=list(a_specs) + list(b_specs), out_specs=out_spec,
                 out_shape=out_shape, scratch_shapes=[pltpu.VMEM(acc_shape, F32)])(*a_args, *b_args)


def _ident(*args):
    return args[-1][...]


def _heads_tile(j, *refs):
    return _heads_to_rows(*refs)


def kernel(x, norm_g, conv_in_w, conv_w, conv_out_w, kv_norm_g, kv_w, q_w, o_w, ffn_in_w, ffn_out_w, loss_target, m_norm_g, m_conv_in_w, m_conv_w, m_conv_out_w, m_kv_norm_g, m_kv_w, m_q_w, m_o_w, m_ffn_in_w, m_ffn_out_w, v_norm_g, v_conv_in_w, v_conv_w, v_conv_out_w, v_kv_norm_g, v_kv_w, v_q_w, v_o_w, v_ffn_in_w, v_ffn_out_w):
    x0 = x[0]
    target = loss_target[0]
    t, d = x0.shape
    depth = norm_g.shape[0]
    n_a = conv_in_w.shape[0]
    n_b = q_w.shape[0]
    hp = d // LANES
    tm, tg = BWD_ROW_TILE, GRAD_ROW_TILE
    assert t % SUPER == 0 and d % LANES == 0 and depth == n_a + n_b
    dev = 4 * lax.axis_index("x") + 2 * lax.axis_index("y") + lax.axis_index("c")

    n_small = 4 * depth + 3 * n_a
    small_rows = -(-(n_small + 1) // 8) * 8
    small_local = jnp.concatenate([norm_g.reshape(4 * depth, -1), conv_w.reshape(3 * n_a, -1),
                                   jnp.zeros((small_rows - n_small, norm_g.shape[-1]), F32)], axis=0)
    swap = lambda a: jnp.swapaxes(a, 1, 2)
    big = {"conv_in_w": conv_in_w, "conv_out_w": conv_out_w, "kv_w": kv_w[None], "q_w": q_w, "o_w": o_w,
           "ffn_in_w": swap(ffn_in_w), "ffn_out_w": ffn_out_w}
    names = list(big)

    def group(layer):
        if layer < n_a:
            return [("conv_in_w", layer), ("conv_out_w", layer), ("ffn_in_w", layer), ("ffn_out_w", layer)]
        j = layer - n_a
        return ([("kv_w", 0)] if j == 0 else []) + [("q_w", j), ("o_w", j), ("ffn_in_w", layer), ("ffn_out_w", layer)]

    slot = dev.astype(jnp.int32).reshape(1)
    is_ffn = lambda key: key[0].startswith("ffn")
    first_keys = [key for key in group(0) if not is_ffn(key)]
    first = _all_gather([small_local] + [_cast_layer(big[k], i, None, f"cast_{k}_{i}") for k, i in first_keys], "gather_weights")
    small_all = first[0].transpose(1, 0, 2).reshape(small_rows, d)
    wl = {key: a[:, None] for key, a in zip(first_keys, first[1:])}

    def gather_start(keys, after, tag):
        lands = [_cast_layer(big[k], i, slot, f"cast_{k}_{i}") for k, i in keys]
        send_sems, recv_sems, _, lands, tok = _send_start([], lands, after, "gather_start" + tag)
        return (keys, tag, send_sems, recv_sems, lands), tok[0, 0]

    def gather_wait(flight, after):
        keys, tag, send_sems, recv_sems, lands = flight
        _, lands = _send_wait(send_sems, recv_sems, [], lands, after, "gather_wait" + tag)
        wl.update({key: a[:, None] for key, a in zip(keys, lands)})

    in_flight, token = gather_start([key for key in group(0) if is_ffn(key)], small_all, "_l0")
    W = lambda k, i: (wl[(k, i)], 0)
    gain = lambda layer, k: small_all[4 * layer + k][None]
    taps = lambda layer: small_all[4 * depth + 3 * layer: 4 * depth + 3 * layer + 3]
    g_kv = kv_norm_g[None]
    slopes = _alibi_slopes(d // HEAD_DIM)
    fc = big["ffn_in_w"].shape[-2]
    cb = big["conv_in_w"].shape[-1]
    kvb = big["kv_w"].shape[-1]
    q_scale = HEAD_DIM ** -0.5

    saved = []
    kv = kvn_t = None
    xs = x0
    for layer in range(depth):
        tag = f"_l{layer}"
        g0 = g2 = 0.0
        if layer == 0:
            g0 = token
        else:
            gather_wait(in_flight, xs)
            if layer + 1 < depth:
                in_flight, g0 = gather_start(group(layer + 1), xs, f"_l{layer + 1}")
        s = {"x_in": xs}
        g0 = gain(layer, 0) + g0
        if layer < n_a:
            s["p"], s["xn_t"] = _norm_matmul_cols(xs, g0, *W("conv_in_w", layer), "cols", "conv_in" + tag)
            s["z"] = _conv_fwd(s["p"], taps(layer), "conv" + tag)
            s["mix"], x_mid = _matmul_norm_residual(s["z"][None], *W("conv_out_w", layer), gain(layer, 1), xs, "conv_out" + tag)
        else:
            j = layer - n_a
            if kv is None:
                kv, kvn_t = _norm_matmul_cols(xs, g_kv, *W("kv_w", 0), "heads", "kv_proj")
            s["q"], s["xn"] = _norm_matmul_heads(xs, g0, *W("q_w", j), q_scale, "q_proj" + tag)
            s["o"], s["lse"] = _attention_fwd(s["q"], kv, slopes, "attention" + tag)
            s["mix"], x_mid = _matmul_norm_residual(s["o"][None], *W("o_w", j), gain(layer, 1), xs, "o_proj" + tag)
        s["x_mid"] = x_mid
        if layer == 0:
            gather_wait(in_flight, x_mid)
            in_flight, g2 = gather_start(group(1), x_mid, "_l1")
        s["gate"], s["up"], s["a"], s["fn"] = _ffn_in_swiglu(x_mid, gain(layer, 2) + g2, *W("ffn_in_w", layer), "ffn_in" + tag)
        s["ff"], xs = _matmul_norm_residual(s["a"], *W("ffn_out_w", layer), gain(layer, 3), x_mid, "ffn_out" + tag)
        saved.append(s)

    last = saved[-1]
    sq, dx_out, d_ff, dg3 = _loss_head(xs, target, last["ff"], gain(depth - 1, 3), "loss_head")
    loss = lax.psum(sq[0, 0] * (0.5 / d), ("x", "y", "c"))

    dgain = {(depth - 1, 3): dg3}
    dtaps = {}
    grads = {k: [None] * big[k].shape[0] for k in names}
    dkv_parts = []
    scattering = []

    def scatter_start(keys, tag):
        parts = [grads[k][i] for k, i in keys]
        zones = [lax.empty(p.shape, p.dtype) for p in parts]
        send_sems, recv_sems, parts, zones, tok = _send_start(parts, zones, small_all, "scatter_start" + tag)
        scattering.append((keys, tag, send_sems, recv_sems, parts, zones))
        return tok[0, 0]

    for layer in reversed(range(depth)):
        tag = f"_l{layer}"
        s = saved[layer]
        dh = _swiglu_bwd(d_ff, *W("ffn_out_w", layer), s["gate"], s["up"], "swiglu_bwd" + tag)
        rows_out = big["ffn_out_w"].shape[1]
        grads["ffn_out_w"][layer] = _grad_weight(
            [pl.BlockSpec((None, tg, fc), lambda c, i: (c, i, 0))], [s["a"]], _ident,
            [pl.BlockSpec((tg, d), lambda c, i: (i, 0))], [d_ff], _ident,
            4, (fc, d), pl.BlockSpec((2, rows_out, d), lambda c, i: (c, 0, 0)), _sds((N_DEV, rows_out, d), BF16), t,
            "grad_ffn_out" + tag)
        grads["ffn_in_w"][layer] = _grad_weight(
            [pl.BlockSpec((None, None, tg, fc), lambda j, i: (j % 4, j // 4, i, 0))], [dh], _ident,
            [pl.BlockSpec((tg, d), lambda j, i: (i, 0))], [s["fn"]], _ident,
            N_DEV, (fc, d), pl.BlockSpec((None, fc, d), lambda j, i: (j, 0, 0)), _sds((N_DEV, fc, d), BF16), t,
            "grad_ffn_in" + tag)
        tok = scatter_start([("ffn_in_w", layer), ("ffn_out_w", layer)], "_ffn" + tag)
        dx_mid, d_mix, dg2, dg1 = _bwd_matmul_norms(
            [pl.BlockSpec((None, None, tm, fc), lambda i, j: (j % 4, j // 4, i, 0))], [dh], _ident, N_DEV,
            pl.BlockSpec((None, None, fc, d), lambda i, j: (j, 0, 0, 0)), W("ffn_in_w", layer)[0], _ident,
            s["x_mid"], gain(layer, 2) + tok, dx_out, s["mix"], gain(layer, 1), "ffn_in_bwd" + tag, w_transposed=True)
        dgain[(layer, 2)], dgain[(layer, 1)] = dg2, dg1
        full_rows = pl.BlockSpec((N_DEV, d // N_DEV, d), lambda j, i: (0, 0, 0))
        rows_w = lambda wname, idx: (pl.BlockSpec((N_DEV, None, d // N_DEV, d), lambda i, j: (0, 0, 0, 0)), W(wname, idx)[0],
                                     lambda w_ref: w_ref[...].reshape(d, d))
        if layer < n_a:
            d_z = _matmul_nt_rows(d_mix, *W("conv_out_w", layer), BF16, "conv_out_bwd" + tag)
            grads["conv_out_w"][layer] = _grad_weight(
                [pl.BlockSpec((tg, d), lambda j, i: (i, 0))], [s["z"]], _ident,
                [pl.BlockSpec((tg, d), lambda j, i: (i, 0))], [d_mix], _ident,
                1, (d, d), full_rows, _sds((N_DEV, d // N_DEV, d), BF16), t, "grad_conv_out" + tag)
            d_p, dtaps[layer] = _conv_bwd(s["p"], d_z, taps(layer), "conv_bwd" + tag)
            grads["conv_in_w"][layer] = _grad_weight(
                [pl.BlockSpec((d, tg), lambda j, i: (0, i))], [s["xn_t"]], _ident,
                [pl.BlockSpec((tg, 2 * cb), lambda j, i: (i, j))], [d_p], _ident,
                N_DEV // 2, (d, 2 * cb), pl.BlockSpec((2, d, cb), lambda j, i: (j, 0, 0)), _sds((N_DEV, d, cb), BF16), t,
                "grad_conv_in" + tag, a_transposed=True)
            a_specs, a_args, a_tile, n_steps = [pl.BlockSpec((tm, 4 * cb), lambda i, j: (i, j))], [d_p], _ident, N_DEV // 4
            w_spec = pl.BlockSpec((4, None, d, cb), lambda i, j: (j, 0, 0, 0))
            w_arg = W("conv_in_w", layer)[0]
            w_mat = lambda w_ref: jnp.concatenate([w_ref[k] for k in range(4)], axis=1)
            resid = dx_mid
        else:
            j_b = layer - n_a
            d_o = _matmul_nt_rows(d_mix, *W("o_w", j_b), F32, "o_proj_bwd" + tag)
            grads["o_w"][j_b] = _grad_weight(
                [pl.BlockSpec((tg, d), lambda j, i: (i, 0))], [s["o"]], _ident,
                [pl.BlockSpec((tg, d), lambda j, i: (i, 0))], [d_mix], _ident,
                1, (d, d), full_rows, _sds((N_DEV, d // N_DEV, d), BF16), t, "grad_o" + tag)
            dk_in, dv_in = dkv_parts[0] if dkv_parts else (None, None)
            dq, dk, dv = _attention_bwd(s["q"], kv, s["o"], s["lse"], d_o, dk_in, dv_in, slopes, q_scale, "attention_bwd" + tag)
            dkv_parts = [(dk, dv)]
            heads_spec = pl.BlockSpec((hp, tg, LANES), lambda j, i: (0, i, 0))
            grads["q_w"][j_b] = _grad_weight(
                [pl.BlockSpec((tg, d), lambda j, i: (i, 0))], [s["xn"]], _ident,
                [heads_spec], [dq], _heads_tile,
                1, (d, d), full_rows, _sds((N_DEV, d // N_DEV, d), BF16), t, "grad_q" + tag)
            a_specs, a_args, a_tile, n_steps = [pl.BlockSpec((hp, tm, LANES), lambda i, j: (0, i, 0))], [dq], _heads_tile, 1
            w_spec, w_arg, w_mat = rows_w("q_w", j_b)
            resid = dx_mid
            if layer == n_a:
                pieces = kvb // LANES
                halves = []
                for src in (0, 1):
                    halves.append([part[src] for part in dkv_parts])
                n_half = len(dkv_parts)
                kv_args = [arr for src in (0, 1) for arr in halves[src]]

                def kv_block(src, j):
                    return jnp.where((j // 4) == src, j % 4, 0)

                def kv_tile(j, *refs):
                    keys = _heads_to_rows(*refs[:n_half])
                    vals = _heads_to_rows(*refs[n_half:])
                    return jnp.where(j < 4, keys, vals)

                kv_specs = [pl.BlockSpec((pieces, tm, LANES), functools.partial(lambda i, j, src: (kv_block(src, j), i, 0), src=src))
                            for src in (0, 1) for _ in range(n_half)]
                resid, dgain["kv"] = _bwd_matmul_norms(
                    kv_specs, kv_args, kv_tile, N_DEV,
                    pl.BlockSpec((None, None, d, kvb), lambda i, j: (j, 0, 0, 0)), W("kv_w", 0)[0], _ident,
                    s["x_in"], g_kv, dx_mid, None, None, "kv_proj_bwd")
                kv_b_specs = [pl.BlockSpec((pieces, tg, LANES), functools.partial(lambda j, i, src: (kv_block(src, j), i, 0), src=src))
                              for src in (0, 1) for _ in range(n_half)]
                grads["kv_w"][0] = _grad_weight(
                    [pl.BlockSpec((d, tg), lambda j, i: (0, i))], [kvn_t], _ident,
                    kv_b_specs, kv_args, kv_tile,
                    N_DEV, (d, kvb), pl.BlockSpec((None, d, kvb), lambda j, i: (j, 0, 0)), _sds((N_DEV, d, kvb), BF16), t,
                    "grad_kv", a_transposed=True)
        tok = scatter_start([key for key in group(layer) if not key[0].startswith("ffn")], "_mix" + tag)
        if layer > 0:
            prev = saved[layer - 1]
            dx_out, d_ff, dg0, dg3p = _bwd_matmul_norms(
                a_specs, a_args, a_tile, n_steps, w_spec, w_arg, w_mat,
                s["x_in"], gain(layer, 0) + tok, resid, prev["ff"], gain(layer - 1, 3), "mixer_in_bwd" + tag)
            dgain[(layer, 0)], dgain[(layer - 1, 3)] = dg0, dg3p
        else:
            grad_x, dg0 = _bwd_matmul_norms(
                a_specs, a_args, a_tile, n_steps, w_spec, w_arg, w_mat,
                s["x_in"], gain(layer, 0), resid, None, None, "mixer_in_bwd" + tag)
            dgain[(layer, 0)] = dg0

    small_grad = jnp.concatenate(
        [dgain[(layer, k)] for layer in range(depth) for k in range(4)] + [dtaps[layer] for layer in range(n_a)]
        + [dgain["kv"]] + [jnp.zeros((small_rows - n_small - 1, d), F32)], axis=0)
    small_grads_all = _all_gather([small_grad], "gather_small_grads")[0]
    lo = dev * (d // N_DEV)

    def pack(ng, cwp, kvg):
        rows = jnp.concatenate([ng.reshape(4 * depth, -1), cwp.reshape(3 * n_a, -1)], axis=0)
        z = lax.dynamic_update_slice(jnp.zeros((small_rows, d), F32), rows, (0, lo))
        return lax.dynamic_update_slice(z, kvg[None], (n_small, 0))

    w_small = lax.dynamic_update_slice(small_all, g_kv, (n_small, 0))
    m_small, v_small = pack(m_norm_g, m_conv_w, m_kv_norm_g), pack(v_norm_g, v_conv_w, v_kv_norm_g)
    sm = _small_adamw(small_grads_all, w_small, m_small, v_small, "adamw_small")

    def unpack(a):
        mine = lax.dynamic_slice(a, (0, lo), (small_rows, d // N_DEV))
        return (mine[:4 * depth].reshape(norm_g.shape), mine[4 * depth:n_small].reshape(conv_w.shape), a[n_small])

    small_out = [unpack(a) for a in sm]

    moments = {"conv_in_w": (m_conv_in_w, v_conv_in_w), "conv_out_w": (m_conv_out_w, v_conv_out_w),
               "kv_w": (m_kv_w[None], v_kv_w[None]), "q_w": (m_q_w, v_q_w), "o_w": (m_o_w, v_o_w),
               "ffn_in_w": (swap(m_ffn_in_w), swap(v_ffn_in_w)), "ffn_out_w": (m_ffn_out_w, v_ffn_out_w)}
    landed = {k: [None] * big[k].shape[0] for k in names}
    for keys, tag, send_sems, recv_sems, parts, zones in scattering:
        parts, zones = _send_wait(send_sems, recv_sems, parts, zones, grad_x, "scatter_wait" + tag)
        for (k, i), part, zone in zip(keys, parts, zones):
            landed[k][i] = (part, zone)
    res = {k: _sum_adamw(slot, [p for p, _ in landed[k]], [z for _, z in landed[k]], big[k], moments[k][0], moments[k][1],
                         "adamw_" + k) for k in names}

    def big_out(k, which):
        out = res[k][which]
        return out[0] if k == "kv_w" else swap(out) if k == "ffn_in_w" else out

    out_names = ["norm_g", "conv_in_w", "conv_w", "conv_out_w", "kv_norm_g", "kv_w", "q_w", "o_w", "ffn_in_w", "ffn_out_w"]
    small_pos = {"norm_g": 0, "conv_w": 1, "kv_norm_g": 2}
    outs = [loss, grad_x[None]]
    for which in range(4):
        for k in out_names:
            outs.append(small_out[which][small_pos[k]] if k in small_pos else big_out(k, which))
    return tuple(outs)
```

```python
import functools
import math

import numpy as np
import jax
import jax.numpy as jnp
from jax import lax
from jax.experimental import pallas as pl
from jax.experimental.pallas import tpu as pltpu

F32 = jnp.float32
BF16 = jnp.bfloat16

N_DEV = 8
RMS_EPS = 1e-6
HEAD_DIM = 64
LANES = 128
ATT_BLOCK = 128
DILATIONS = (1, 4, 16)
SUPER = ATT_BLOCK * DILATIONS[-1]
NEG = -1e30

ADAM_LR, ADAM_B1, ADAM_B2, ADAM_EPS, ADAM_WD, ADAM_STEP = 0.001, 0.9, 0.999, 1e-08, 0.01, 10

ROW_TILE = 512
BIG_ROW_TILE = 1024
SWIGLU_ROWS = 256
GRAD_ROW_TILE = 2048
BWD_ROW_TILE = 512
MESH = pl.DeviceIdType.MESH


def _call(body, *, name, grid=None, in_specs=None, out_specs=None, out_shape=None, scratch_shapes=(), prefetch=False,
          **params):
    cp = pltpu.CompilerParams(**params) if params else None
    if prefetch:
        spec = pltpu.PrefetchScalarGridSpec(num_scalar_prefetch=1, grid=grid, in_specs=in_specs, out_specs=out_specs,
                                            scratch_shapes=list(scratch_shapes))
        return pl.pallas_call(body, name=name, grid_spec=spec, out_shape=out_shape, compiler_params=cp)
    kwargs = {k: v for k, v in (("grid", grid), ("in_specs", in_specs), ("out_specs", out_specs)) if v is not None}
    return pl.pallas_call(body, name=name, out_shape=out_shape, scratch_shapes=list(scratch_shapes),
                          compiler_params=cp, **kwargs)


def _sds(shape, dtype):
    return jax.ShapeDtypeStruct(tuple(shape), dtype)


def _rms(x, g):
    r = lax.rsqrt(jnp.mean(x * x, axis=-1, keepdims=True) + RMS_EPS)
    return x * r * g


def _rms_bwd(x, g, dy):
    r = lax.rsqrt(jnp.mean(x * x, axis=-1, keepdims=True) + RMS_EPS)
    xh = x * r
    dxh = dy * g
    dx = r * (dxh - xh * jnp.mean(dxh * xh, axis=-1, keepdims=True))
    return dx, jnp.sum(dy * xh, axis=0, keepdims=True)


def _dot(a, b):
    return jnp.dot(a, b, preferred_element_type=F32)


def _dot_nt(a, b):
    return lax.dot_general(a, b, (((1,), (1,)), ((), ())), preferred_element_type=F32)


def _dot_tn(a, b):
    return lax.dot_general(a, b, (((0,), (0,)), ((), ())), preferred_element_type=F32)


def _mesh_pos():
    return lax.axis_index("x"), lax.axis_index("y"), lax.axis_index("c")


def _all_gather(arrs, name):
    n = len(arrs)

    def body(*refs):
        ins, outs = refs[:n], refs[n:2 * n]
        send_sems, recv_sems, local_sems = refs[2 * n:]
        x, y, c = _mesh_pos()
        me, sibling = (x, y, c), (x, y, 1 - c)
        chips = [(1 - x, y), (x, 1 - y), (1 - x, 1 - y)]

        def copy(a, k, block, to, src=None):
            dst = outs[a].at[4 * block[0] + 2 * block[1] + block[2]]
            return pltpu.make_async_remote_copy(
                src_ref=dst if src is None else src, dst_ref=dst, send_sem=send_sems.at[a, k],
                recv_sem=recv_sems.at[a, k], device_id=to, device_id_type=MESH)

        started = []
        for a in range(n):
            mine = pltpu.make_async_copy(ins[a], outs[a].at[4 * x + 2 * y + c], local_sems.at[a])
            mine.start()
            started.append(mine)
        first = []
        for a in range(n):
            first.append(copy(a, 0, me, sibling, src=ins[a]))
            first += [copy(a, 1 + j, me, (*chip, c), src=ins[a]) for j, chip in enumerate(chips)]
        for cp in first:
            cp.start()
        passed = []
        for a in range(n):
            for j, chip in enumerate(chips):
                copy(a, 1 + j, (*chip, c), me).wait_recv()
                fwd = copy(a, 4 + j, (*chip, c), sibling)
                fwd.start()
                passed.append(fwd)
        for a in range(n):
            copy(a, 0, sibling, me).wait_recv()
            for j, chip in enumerate(chips):
                copy(a, 4 + j, (*chip, 1 - c), me).wait_recv()
        for cp in first + passed:
            cp.wait_send()
        for cp in started:
            cp.wait()

    any_spec = pl.BlockSpec(memory_space=pl.ANY)
    outs = _call(
        body, name=name, in_specs=[any_spec] * n, out_specs=[any_spec] * n,
        out_shape=[_sds((N_DEV,) + a.shape, a.dtype) for a in arrs],
        scratch_shapes=[pltpu.SemaphoreType.DMA((n, 7)), pltpu.SemaphoreType.DMA((n, 7)), pltpu.SemaphoreType.DMA((n,))],
        has_side_effects=True,
    )(*arrs)
    return list(outs)


HBM_SPEC = pl.BlockSpec(memory_space=pltpu.HBM)
SEM_SPEC = pl.BlockSpec(memory_space=pltpu.SEMAPHORE)
DATAFLOW = pltpu.SideEffectType.DATAFLOW_SIDE_EFFECTING
PEERS = [(dx, dy, dc) for dx in (0, 1) for dy in (0, 1) for dc in (0, 1)][1:]


def _peer(flip):
    x, y, c = _mesh_pos()
    return tuple(1 - v if f else v for v, f in zip((x, y, c), flip))


def _slot(pos):
    return 4 * pos[0] + 2 * pos[1] + pos[2]


def _in_hbm(a):
    return pltpu.with_memory_space_constraint(a, pltpu.HBM)


def _direct_copies(srcs, lands, send_sems, recv_sems, scatter):
    me = _slot(_mesh_pos())
    copies = []
    for a in range(len(lands)):
        for k, flip in enumerate(PEERS):
            peer = _peer(flip)
            src = srcs[a].at[_slot(peer)] if scatter else lands[a].at[me]
            idx = a * len(PEERS) + k
            copies.append(pltpu.make_async_remote_copy(
                src_ref=src, dst_ref=lands[a].at[me], send_sem=send_sems.at[idx], recv_sem=recv_sems.at[idx],
                device_id=peer, device_id_type=MESH))
    return copies


def _send_start(srcs, lands, after, name):
    ns, nl = len(srcs), len(lands)
    scatter = ns > 0

    def body(*refs):
        src_refs, land_refs = refs[:ns], refs[ns:ns + nl]
        send_sems, recv_sems = refs[ns + nl + 1:ns + nl + 3]
        token = refs[-1]
        for cp in _direct_copies(src_refs, land_refs, send_sems, recv_sems, scatter):
            cp.start()
        token[...] = jnp.zeros_like(token)

    sem = pltpu.SemaphoreType.DMA((nl * len(PEERS),))
    outs = pl.pallas_call(
        body, name=name,
        out_shape=(sem, sem) + tuple(pltpu.HBM(a.shape, a.dtype) for a in list(srcs) + list(lands))
        + (_sds((8, LANES), F32),),
        in_specs=[HBM_SPEC] * (ns + nl) + [pl.BlockSpec(memory_space=pl.ANY)],
        out_specs=(SEM_SPEC, SEM_SPEC) + (HBM_SPEC,) * (ns + nl) + (pl.BlockSpec(memory_space=pltpu.VMEM),),
        input_output_aliases={i: 2 + i for i in range(ns + nl)},
        compiler_params=pltpu.CompilerParams(has_side_effects=DATAFLOW),
    )(*[_in_hbm(a) for a in list(srcs) + list(lands)], after)
    send_sems, recv_sems = outs[0], outs[1]
    return send_sems, recv_sems, list(outs[2:2 + ns]), list(outs[2 + ns:2 + ns + nl]), outs[-1]


def _send_wait(send_sems, recv_sems, srcs, lands, after, name):
    ns, nl = len(srcs), len(lands)
    scatter = ns > 0

    def body(*refs):
        src_refs, land_refs = refs[:ns], refs[ns:ns + nl]
        send_sems, recv_sems = refs[ns + nl:ns + nl + 2]
        copies = _direct_copies(src_refs, land_refs, send_sems, recv_sems, scatter)
        for cp in copies:
            cp.wait_send()
        for cp in copies:
            cp.wait_recv()

    outs = pl.pallas_call(
        body, name=name,
        out_shape=tuple(pltpu.HBM(a.shape, a.dtype) for a in list(srcs) + list(lands)),
        in_specs=[HBM_SPEC] * (ns + nl) + [SEM_SPEC, SEM_SPEC, pl.BlockSpec(memory_space=pl.ANY)],
        out_specs=(HBM_SPEC,) * (ns + nl),
        input_output_aliases={i: i for i in range(ns + nl)},
        compiler_params=pltpu.CompilerParams(has_side_effects=DATAFLOW),
    )(*srcs, *lands, send_sems, recv_sems, after)
    return list(outs[:ns]), list(outs[ns:])


def _row_tile(rows, cap=512):
    t = min(rows, cap)
    while rows % t or (t % 16 and t != rows):
        t -= 1
    return t


def _as2d(a):
    return a.reshape(-1, a.shape[-1])


def _cast_layer(w, layer, slot, name):
    _, rows, cols = w.shape
    tr = _row_tile(rows)

    def body(*refs):
        refs[-1][...] = refs[-2][...].astype(BF16)

    if slot is None:
        return _call(body, name=name, grid=(rows // tr,),
                     in_specs=[pl.BlockSpec((None, tr, cols), lambda i: (layer, i, 0))],
                     out_specs=pl.BlockSpec((tr, cols), lambda i: (i, 0)), out_shape=_sds((rows, cols), BF16))(w)
    return _call(body, name=name, grid=(rows // tr,), prefetch=True,
                 in_specs=[pl.BlockSpec((None, tr, cols), lambda i, s: (layer, i, 0))],
                 out_specs=pl.BlockSpec((None, tr, cols), lambda i, s: (s[0], i, 0)),
                 out_shape=_sds((N_DEV, rows, cols), BF16))(slot, w)


def _sum_adamw(slot, parts, lands, w, m, v, name):
    n_l = len(parts)
    _, rows, cols = lands[0].shape
    tr = _row_tile(rows, 128)

    def body(s_ref, *refs):
        p_refs, l_refs = refs[:n_l], refs[n_l:2 * n_l]
        w_ref, m_ref, v_ref, g_ref, d_ref, nm_ref, nv_ref = refs[2 * n_l:]
        for k in range(n_l):
            @pl.when(pl.program_id(0) == k)
            def _(k=k):
                own = p_refs[k][...]
                g = jnp.zeros((tr, cols), F32)
                for j in range(N_DEV):
                    g = g + jnp.where(s_ref[0] == j, own, l_refs[k][j]).astype(F32)
                delta, nm, nv = _adamw_math(w_ref[...], g, m_ref[...], v_ref[...])
                g_ref[...] = g
                d_ref[...] = delta
                nm_ref[...] = nm
                nv_ref[...] = nv

    def own_block(k):
        return pl.BlockSpec((None, tr, cols), lambda l, i, s: (s[0], jnp.where(l == k, i, 0), 0))

    def zone_block(k):
        return pl.BlockSpec((N_DEV, tr, cols), lambda l, i, s: (0, jnp.where(l == k, i, 0), 0))

    lay = pl.BlockSpec((None, tr, cols), lambda l, i, s: (l, i, 0))
    return _call(body, name=name, grid=(n_l, rows // tr), prefetch=True,
                 in_specs=[own_block(k) for k in range(n_l)] + [zone_block(k) for k in range(n_l)] + [lay, lay, lay],
                 out_specs=[lay] * 4, out_shape=[_sds((n_l, rows, cols), F32)] * 4)(slot, *parts, *lands, w, m, v)


def _adamw_math(w, g, m, v):
    m = ADAM_B1 * m + (1.0 - ADAM_B1) * g
    v = ADAM_B2 * v + (1.0 - ADAM_B2) * (g * g)
    m_hat = m / (1.0 - ADAM_B1 ** ADAM_STEP)
    v_hat = v / (1.0 - ADAM_B2 ** ADAM_STEP)
    delta = -ADAM_LR * (m_hat / (jnp.sqrt(v_hat) + ADAM_EPS) + ADAM_WD * w)
    return delta, m, v


def _small_adamw(gathered, w, m, v, name):
    def body(a_ref, w_ref, m_ref, v_ref, g_ref, d_ref, nm_ref, nv_ref):
        g = a_ref[0]
        for k in range(1, N_DEV):
            g = g + a_ref[k]
        delta, nm, nv = _adamw_math(w_ref[...], g, m_ref[...], v_ref[...])
        g_ref[...] = g
        d_ref[...] = delta
        nm_ref[...] = nm
        nv_ref[...] = nv

    return _call(body, name=name, out_shape=[_sds(w.shape, F32)] * 4)(gathered, w, m, v)


def _norm_matmul_cols(x, g, wg, layer, mode, name):
    t, d = x.shape
    nb = wg.shape[-1]
    tm = BIG_ROW_TILE
    per = 2
    pieces = per * nb // LANES

    def body(x_ref, g_ref, w_ref, y_ref, xnt_ref, xn_ref):
        @pl.when(pl.program_id(1) == 0)
        def _():
            xn = _rms(x_ref[...], g_ref[...])
            xn_ref[...] = xn.astype(BF16)
            xnt_ref[...] = xn.T.astype(BF16)

        y = _dot(xn_ref[...], jnp.concatenate([w_ref[k] for k in range(per)], axis=1))
        if mode == "heads":
            for p in range(pieces):
                y_ref[p] = y[:, p * LANES:(p + 1) * LANES]
        else:
            y_ref[...] = y.astype(BF16)

    if mode == "cols":
        y_shape, y_spec = _sds((t, N_DEV * nb), BF16), pl.BlockSpec((tm, per * nb), lambda i, j: (i, j))
    else:
        y_shape = _sds((N_DEV // per * pieces, t, LANES), F32)
        y_spec = pl.BlockSpec((pieces, tm, LANES), lambda i, j: (j, i, 0))
    return _call(
        body, name=name, grid=(t // tm, N_DEV // per),
        in_specs=[pl.BlockSpec((tm, d), lambda i, j: (i, 0)), pl.BlockSpec((1, d), lambda i, j: (0, 0)),
                  pl.BlockSpec((per, None, d, nb), lambda i, j: (j, layer, 0, 0))],
        out_specs=[y_spec, pl.BlockSpec((d, tm), lambda i, j: (0, i))],
        out_shape=[y_shape, _sds((d, t), BF16)], scratch_shapes=[pltpu.VMEM((tm, d), BF16)])(x, g, wg)


def _ffn_in_swiglu(x, g, wg, layer, name):
    t, d = x.shape
    fc = wg.shape[-2]
    tm = BIG_ROW_TILE

    def body(x_ref, g_ref, wg_ref, wu_ref, gate_ref, up_ref, a_ref, xn_ref):
        @pl.when(pl.program_id(1) == 0)
        def _():
            xn_ref[...] = _rms(x_ref[...], g_ref[...]).astype(BF16)

        xn = xn_ref[...]
        gate, up = _dot_nt(xn, wg_ref[...]), _dot_nt(xn, wu_ref[...])
        gate_ref[...] = gate.astype(BF16)
        up_ref[...] = up.astype(BF16)
        a_ref[...] = (gate * jax.nn.sigmoid(gate) * up).astype(BF16)

    chunk = pl.BlockSpec((None, tm, fc), lambda i, c: (c, i, 0))
    return _call(
        body, name=name, grid=(t // tm, 4),
        in_specs=[pl.BlockSpec((tm, d), lambda i, c: (i, 0)), pl.BlockSpec((1, d), lambda i, c: (0, 0)),
                  pl.BlockSpec((None, None, fc, d), lambda i, c: (c, layer, 0, 0)),
                  pl.BlockSpec((None, None, fc, d), lambda i, c: (c + 4, layer, 0, 0))],
        out_specs=[chunk, chunk, chunk, pl.BlockSpec((tm, d), lambda i, c: (i, 0))],
        out_shape=[_sds((4, t, fc), BF16)] * 3 + [_sds((t, d), BF16)])(x, g, wg, wg)


def _norm_matmul_heads(x, g, wg, layer, scale, name):
    t, d = x.shape
    tm = ROW_TILE
    hp = d // LANES

    def body(x_ref, g_ref, w_ref, y_ref, xn_ref):
        xn = _rms(x_ref[...], g_ref[...]).astype(BF16)
        xn_ref[...] = xn
        y = _dot(xn, w_ref[...].reshape(d, d)) * scale
        for p in range(hp):
            y_ref[p] = y[:, p * LANES:(p + 1) * LANES]

    return _call(
        body, name=name, grid=(t // tm,),
        in_specs=[pl.BlockSpec((tm, d), lambda i: (i, 0)), pl.BlockSpec((1, d), lambda i: (0, 0)),
                  pl.BlockSpec((N_DEV, None, d // N_DEV, d), lambda i: (0, layer, 0, 0))],
        out_specs=[pl.BlockSpec((hp, tm, LANES), lambda i: (0, i, 0)), pl.BlockSpec((tm, d), lambda i: (i, 0))],
        out_shape=[_sds((hp, t, LANES), F32), _sds((t, d), BF16)])(x, g, wg)


def _shift_down(u, halo, k, tm):
    row = lax.broadcasted_iota(jnp.int32, u.shape, 0)
    out = pltpu.roll(u, k, 0)
    for j in range(k):
        out = jnp.where(row == j, halo[halo.shape[0] - k + j:halo.shape[0] - k + j + 1, :], out)
    return out


def _shift_up(u, halo, k, tm):
    row = lax.broadcasted_iota(jnp.int32, u.shape, 0)
    out = pltpu.roll(u, tm - k, 0)
    for j in range(k):
        out = jnp.where(row == tm - k + j, halo[j:j + 1, :], out)
    return out


HALO = 16


def _conv_fwd(p, cw, name):
    t, d3 = p.shape
    d = d3 // 3
    tm = ROW_TILE
    hb = tm // HALO

    def body(p_ref, prev_ref, cw_ref, z_ref):
        i = pl.program_id(0)
        b = p_ref[:, 0:d].astype(F32)
        u = p_ref[:, d:2 * d].astype(F32) * p_ref[:, 2 * d:3 * d].astype(F32)
        keep = (i > 0).astype(F32)
        hu = prev_ref[:, d:2 * d].astype(F32) * prev_ref[:, 2 * d:3 * d].astype(F32) * keep
        uc = cw_ref[2:3, :] * u + cw_ref[1:2, :] * _shift_down(u, hu, 1, tm) + cw_ref[0:1, :] * _shift_down(u, hu, 2, tm)
        z_ref[...] = (b * uc).astype(BF16)

    return _call(
        body, name=name, grid=(t // tm,),
        in_specs=[pl.BlockSpec((tm, d3), lambda i: (i, 0)),
                  pl.BlockSpec((HALO, d3), lambda i: (jnp.maximum(i * hb - 1, 0), 0)),
                  pl.BlockSpec((3, d), lambda i: (0, 0))],
        out_specs=pl.BlockSpec((tm, d), lambda i: (i, 0)), out_shape=_sds((t, d), BF16))(p, p, cw)


def _matmul_norm_residual(a3, wg, layer, g, x_res, name):
    kc_n, t, kc = a3.shape
    d = wg.shape[-1]
    per = N_DEV // kc_n
    rows = wg.shape[2]
    tm = ROW_TILE

    def body(a_ref, w_ref, g_ref, x_ref, raw_ref, xo_ref):
        raw = None
        for c in range(kc_n):
            term = _dot(a_ref[c], w_ref[c * per:(c + 1) * per].reshape(per * rows, d))
            raw = term if raw is None else raw + term
        raw_ref[...] = raw
        xo_ref[...] = x_ref[...] + _rms(raw, g_ref[...])

    row_spec = pl.BlockSpec((tm, d), lambda i: (i, 0))
    return _call(
        body, name=name, grid=(t // tm,),
        in_specs=[pl.BlockSpec((kc_n, tm, kc), lambda i: (0, i, 0)),
                  pl.BlockSpec((N_DEV, None, rows, d), lambda i: (0, layer, 0, 0)),
                  pl.BlockSpec((1, d), lambda i: (0, 0)), row_spec],
        out_specs=[row_spec, row_spec], out_shape=[_sds((t, d), F32)] * 2)(a3, wg, g, x_res)


def _alibi_slopes(n_heads):
    hh = np.arange(n_heads, dtype=np.float32) + 1.0
    s = np.power(2.0, -8.0 * hh / n_heads).astype(np.float32)
    return jnp.asarray(np.repeat(s.reshape(n_heads // 2, 2, 1), 2 * ATT_BLOCK, axis=2))


def _band_bias(sl_ref, dil):
    u = lax.broadcasted_iota(jnp.int32, (ATT_BLOCK, 2 * ATT_BLOCK), 0)
    kk = lax.broadcasted_iota(jnp.int32, (ATT_BLOCK, 2 * ATT_BLOCK), 1)
    delta = u + ATT_BLOCK - kk
    valid = (delta >= 0) & (delta <= ATT_BLOCK)
    dist = (delta * dil).astype(F32)
    rows = [jnp.where(valid, -sl_ref[hd:hd + 1, :] * dist, NEG) for hd in range(2)]
    return jnp.concatenate(rows, axis=0)


def _stack_heads(a):
    lane = lax.broadcasted_iota(jnp.int32, a.shape, 1)
    return jnp.concatenate([jnp.where(lane < HEAD_DIM, a, 0.0), jnp.where(lane >= HEAD_DIM, a, 0.0)], axis=0).astype(BF16)


def _unstack_heads(a2):
    top, bot = a2[:ATT_BLOCK], a2[ATT_BLOCK:]
    lane = lax.broadcasted_iota(jnp.int32, top.shape, 1)
    return jnp.where(lane < HEAD_DIM, top, bot)


def _rows_to_lanes(a0, a1):
    eye = lax.broadcasted_iota(jnp.int32, a0.shape, 0) == lax.broadcasted_iota(jnp.int32, a0.shape, 1)
    return jnp.concatenate([jnp.sum(jnp.where(eye, a, 0.0), axis=0, keepdims=True) for a in (a0, a1)], axis=1)


def _fill_bias_t(sl_ref, bias_ref):
    kk = lax.broadcasted_iota(jnp.int32, (2 * ATT_BLOCK, 2 * ATT_BLOCK), 0)
    lane = lax.broadcasted_iota(jnp.int32, (2 * ATT_BLOCK, 2 * ATT_BLOCK), 1)
    delta = lane % ATT_BLOCK + ATT_BLOCK - kk
    valid = (delta >= 0) & (delta <= ATT_BLOCK)
    slope = jnp.concatenate([sl_ref[0:1, :ATT_BLOCK], sl_ref[1:2, :ATT_BLOCK]], axis=1)
    for gi, dil in enumerate(DILATIONS):
        bias = jnp.where(valid, -slope * (delta * dil).astype(F32), NEG)
        bias_ref[2 * gi] = bias
        bias_ref[2 * gi + 1] = jnp.where(kk < ATT_BLOCK, NEG, bias)


def _fill_bias(sl_ref, bias_ref):
    kk = lax.broadcasted_iota(jnp.int32, (2 * ATT_BLOCK, 2 * ATT_BLOCK), 1)
    for gi, dil in enumerate(DILATIONS):
        bias = _band_bias(sl_ref, dil)
        bias_ref[2 * gi] = bias
        bias_ref[2 * gi + 1] = jnp.where(kk < ATT_BLOCK, NEG, bias)


def _strided_keys(dil, r, b, kc_ref, kp_ref, vc_ref, vp_ref, kcar_ref, vcar_ref):
    if b > 0:
        keys = pl.ds((b - 1) * (ATT_BLOCK * dil) + r, 2 * ATT_BLOCK, stride=dil)
        return kc_ref[keys, :].astype(BF16), vc_ref[keys, :].astype(BF16)
    own = pl.ds(r, ATT_BLOCK, stride=dil)
    k_own, v_own = kc_ref[own, :].astype(BF16), vc_ref[own, :].astype(BF16)
    if dil * ATT_BLOCK != SUPER:
        before = pl.ds(SUPER - ATT_BLOCK * dil + r, ATT_BLOCK, stride=dil)
        k_before, v_before = kp_ref[before, :].astype(BF16), vp_ref[before, :].astype(BF16)
    else:
        k_before, v_before = kcar_ref[r], vcar_ref[r]
        kcar_ref[r] = k_own
        vcar_ref[r] = v_own
    return jnp.concatenate([k_before, k_own], axis=0), jnp.concatenate([v_before, v_own], axis=0)


def _attention_fwd(q, kv, slopes, name):
    hp, t, _ = q.shape
    ns = t // SUPER
    nd = len(DILATIONS)

    def body(sl_ref, q_ref, kc_ref, kp_ref, vc_ref, vp_ref, o_ref, lse_ref, og_ref, lg_ref, bias_ref, kcar_ref, vcar_ref):
        n = pl.program_id(1)

        @pl.when(n == 0)
        def _():
            _fill_bias(sl_ref, bias_ref)
            kcar_ref[...] = jnp.zeros_like(kcar_ref)
            vcar_ref[...] = jnp.zeros_like(vcar_ref)

        for gi, dil in enumerate(DILATIONS):
            for idx in range(SUPER // ATT_BLOCK):
                r, b = idx % dil, idx // dil
                qs = b * (ATT_BLOCK * dil) + r
                first = (n == 0).astype(jnp.int32) if b == 0 else 0
                q2 = _stack_heads(q_ref[pl.ds(qs, ATT_BLOCK, stride=dil), :])
                kb, vb = _strided_keys(dil, r, b, kc_ref, kp_ref, vc_ref, vp_ref, kcar_ref, vcar_ref)
                s = _dot_nt(q2, kb) + bias_ref[2 * gi + first]
                m = jnp.max(s, axis=-1, keepdims=True)
                p = jnp.exp(s - m).astype(BF16)
                ol = _dot(p, jnp.concatenate([vb, jnp.ones_like(vb)], axis=1))
                l = ol[:, LANES:]
                o2 = ol[:, :LANES] / l
                lse2 = m + jnp.log(l)
                og_ref[gi, pl.ds(qs, ATT_BLOCK, stride=dil), :] = _unstack_heads(o2)
                lg_ref[gi, pl.ds(qs, ATT_BLOCK, stride=dil), :] = _unstack_heads(lse2)
        lg = [lg_ref[gi] for gi in range(nd)]
        top = functools.reduce(jnp.maximum, lg)
        ws = [jnp.exp(x - top) for x in lg]
        tot = functools.reduce(jnp.add, ws)
        lse_ref[...] = top + jnp.log(tot)
        acc = ws[0] * og_ref[0]
        for gi in range(1, nd):
            acc = acc + ws[gi] * og_ref[gi]
        o_ref[...] = (acc / tot).astype(BF16)

    cur = lambda off: pl.BlockSpec((None, SUPER, LANES), lambda h, n: (h + off, n, 0))
    prev = lambda off: pl.BlockSpec((None, SUPER, LANES), lambda h, n: (h + off, jnp.maximum(n - 1, 0), 0))
    return _call(
        body, name=name, grid=(hp, ns),
        in_specs=[pl.BlockSpec((None, 2, 2 * ATT_BLOCK), lambda h, n: (h, 0, 0)), cur(0), cur(0), prev(0), cur(hp), prev(hp)],
        out_specs=[pl.BlockSpec((SUPER, LANES), lambda h, n: (n, h)), cur(0)],
        out_shape=[_sds((t, hp * LANES), BF16), _sds((hp, t, LANES), F32)],
        scratch_shapes=[pltpu.VMEM((nd, SUPER, LANES), F32), pltpu.VMEM((nd, SUPER, LANES), F32),
                        pltpu.VMEM((2 * nd, 2 * ATT_BLOCK, 2 * ATT_BLOCK), F32)] + [
                            pltpu.VMEM((DILATIONS[-1], ATT_BLOCK, LANES), BF16)] * 2,
    )(slopes, q, kv, kv, kv, kv)


def _attention_bwd(q, kv, o, lse, d_o, dk_in, dv_in, slopes, q_scale, name):
    hp, t, _ = q.shape
    ns = t // SUPER
    shared = dk_in is not None

    def body(sl_ref, q_ref, kc_ref, kp_ref, vc_ref, vp_ref, o_ref, lse_ref, do_ref, *rest):
        dki_ref, dvi_ref = rest[:2] if shared else (None, None)
        dq_ref, dk_ref, dv_ref, dkw_ref, dvw_ref, st_ref, bias_ref, kcar_ref, vcar_ref = rest[2 if shared else 0:]
        n = pl.program_id(1)

        @pl.when(n == 0)
        def _():
            dkw_ref[...] = jnp.zeros_like(dkw_ref)
            dvw_ref[...] = jnp.zeros_like(dvw_ref)

        @pl.when(n > 0)
        def _():
            dkw_ref[0:SUPER, :] = dkw_ref[SUPER:, :]
            dvw_ref[0:SUPER, :] = dvw_ref[SUPER:, :]
            dkw_ref[SUPER:, :] = jnp.zeros((SUPER, LANES), F32)
            dvw_ref[SUPER:, :] = jnp.zeros((SUPER, LANES), F32)

        @pl.when(n < ns)
        def _():
            prod = do_ref[...] * o_ref[...].astype(F32)
            lane = lax.broadcasted_iota(jnp.int32, prod.shape, 1)
            zero = jnp.zeros((SUPER, LANES), F32)
            st_ref[0] = zero + jnp.sum(jnp.where(lane < HEAD_DIM, prod, 0.0), axis=-1, keepdims=True)
            st_ref[1] = zero + jnp.sum(jnp.where(lane >= HEAD_DIM, prod, 0.0), axis=-1, keepdims=True)
            lse = lse_ref[...]
            swapped = pltpu.roll(lse, HEAD_DIM, 1)
            st_ref[2] = jnp.where(lane < HEAD_DIM, lse, swapped)
            st_ref[3] = jnp.where(lane >= HEAD_DIM, lse, swapped)
            dq_ref[...] = jnp.zeros_like(dq_ref)

            @pl.when(n == 0)
            def _():
                _fill_bias_t(sl_ref, bias_ref)
                kcar_ref[...] = jnp.zeros_like(kcar_ref)
                vcar_ref[...] = jnp.zeros_like(vcar_ref)

            for gi, dil in enumerate(DILATIONS):
                for idx in range(SUPER // ATT_BLOCK):
                    r, b = idx % dil, idx // dil
                    qs = b * (ATT_BLOCK * dil) + r
                    ks = SUPER + (b - 1) * (ATT_BLOCK * dil) + r
                    first = (n == 0).astype(jnp.int32) if b == 0 else 0
                    rows = pl.ds(qs, ATT_BLOCK, stride=dil)
                    keys = pl.ds(ks, 2 * ATT_BLOCK, stride=dil)
                    q2 = _stack_heads(q_ref[rows, :])
                    do2 = _stack_heads(do_ref[rows, :])
                    kb, vb = _strided_keys(dil, r, b, kc_ref, kp_ref, vc_ref, vp_ref, kcar_ref, vcar_ref)
                    dd = _rows_to_lanes(st_ref[0, rows, :], st_ref[1, rows, :])
                    lse_b = _rows_to_lanes(st_ref[2, rows, :], st_ref[3, rows, :])
                    ps, dss = [], []
                    for half in range(2):
                        hk = slice(half * ATT_BLOCK, (half + 1) * ATT_BLOCK)
                        p = jnp.exp(_dot_nt(kb[hk], q2) + bias_ref[2 * gi + first, hk, :] - lse_b)
                        dss.append((p * (_dot_nt(vb[hk], do2) - dd)).astype(BF16))
                        ps.append(p.astype(BF16))
                    p, ds = jnp.concatenate(ps, axis=0), jnp.concatenate(dss, axis=0)
                    dvw_ref[keys, :] += _dot(p, do2)
                    dkw_ref[keys, :] += _dot(ds, q2)
                    dq_ref[rows, :] += _unstack_heads(_dot_tn(ds, kb)) * q_scale

        dk_ref[...] =dkw_ref[0:SUPER, :] + dki_ref[...] if shared else dkw_ref[0:SUPER, :]
        dv_ref[...] = dvw_ref[0:SUPER, :] + dvi_ref[...] if shared else dvw_ref[0:SUPER, :]

    last = ns - 1
    cur = lambda off: pl.BlockSpec((None, SUPER, LANES), lambda h, n: (h + off, jnp.minimum(n, last), 0))
    prev = lambda off: pl.BlockSpec((None, SUPER, LANES), lambda h, n: (h + off, jnp.clip(n - 1, 0, last), 0))
    nat = pl.BlockSpec((SUPER, LANES), lambda h, n: (jnp.minimum(n, last), h))
    late = pl.BlockSpec((None, SUPER, LANES), lambda h, n: (h, jnp.maximum(n - 1, 0), 0))
    dq, dk, dv = _call(
        body, name=name, grid=(hp, ns + 1),
        in_specs=[pl.BlockSpec((None, 2, 2 * ATT_BLOCK), lambda h, n: (h, 0, 0)), cur(0), cur(0), prev(0), cur(hp), prev(hp),
                  nat, cur(0), nat] + ([late, late] if shared else []),
        out_specs=[cur(0), late, late],
        out_shape=[_sds((hp, t, LANES), F32)] * 3,
        scratch_shapes=[pltpu.VMEM((2 * SUPER, LANES), F32)] * 2 + [
            pltpu.VMEM((4, SUPER, LANES), F32), pltpu.VMEM((2 * len(DILATIONS), 2 * ATT_BLOCK, 2 * ATT_BLOCK), F32)] + [
                pltpu.VMEM((DILATIONS[-1], ATT_BLOCK, LANES), BF16)] * 2,
    )(slopes, q, kv, kv, kv, kv, o, lse, d_o, *((dk_in, dv_in) if shared else ()))
    return dq, dk, dv


def _loss_head(y, target, raw, g, name):
    t, d = y.shape
    tm = ROW_TILE

    def body(y_ref, t_ref, raw_ref, g_ref, sq_ref, dy_ref, draw_ref, dg_ref):
        i = pl.program_id(0)
        err = y_ref[...] - t_ref[...]
        dy = err * (1.0 / d)
        dy_ref[...] = dy
        draw, dg = _rms_bwd(raw_ref[...], g_ref[...], dy)
        draw_ref[...] = draw.astype(BF16)
        sq = jnp.zeros((8, LANES), F32) + jnp.sum(err * err)

        @pl.when(i == 0)
        def _():
            sq_ref[...] = sq
            dg_ref[...] = dg

        @pl.when(i > 0)
        def _():
            sq_ref[...] += sq
            dg_ref[...] += dg

    row = pl.BlockSpec((tm, d), lambda i: (i, 0))
    vec = pl.BlockSpec((1, d), lambda i: (0, 0))
    return _call(
        body, name=name, grid=(t // tm,), in_specs=[row, row, row, vec],
        out_specs=[pl.BlockSpec((8, LANES), lambda i: (0, 0)), row, row, vec],
        out_shape=[_sds((8, LANES), F32), _sds((t, d), F32), _sds((t, d), BF16), _sds((1, d), F32)])(y, target, raw, g)


def _bwd_matmul_norms(a_specs, a_args, a_tile, n_steps, w_spec, w_arg, w_mat, xa, ga, resid, xb, gb, name,
                      w_transposed=False):
    t, d = xa.shape
    tm = BWD_ROW_TILE
    na = len(a_specs)
    second = xb is not None
    per = 8 if n_steps % 8 == 0 else 1
    n_steps //= per

    def blocks_of(spec, k):
        return pl.BlockSpec(spec.block_shape, lambda i, j: spec.index_map(i, per * j + k))

    def body(*refs):
        a_refs, w_refs = refs[:per * na], refs[per * na:per * na + per]
        xa_ref, ga_ref, res_ref = refs[per * na + per:per * na + per + 3]
        rest = refs[per * na + per + 3:]
        if second:
            xb_ref, gb_ref, dx_ref, d2_ref, dga_ref, dgb_ref, acc_ref = rest
        else:
            dx_ref, dga_ref, acc_ref = rest
        i, j = pl.program_id(0), pl.program_id(1)
        part = None
        for k in range(per):
            term = (_dot if w_transposed else _dot_nt)(a_tile(per * j + k, *a_refs[k * na:(k + 1) * na]), w_mat(w_refs[k]))
            part = term if part is None else part + term

        @pl.when(j == 0)
        def _():
            acc_ref[...] = part

        @pl.when(j > 0)
        def _():
            acc_ref[...] += part

        @pl.when(j == n_steps - 1)
        def _():
            da, dga = _rms_bwd(xa_ref[...], ga_ref[...], acc_ref[...])
            dx = res_ref[...] + da
            dx_ref[...] = dx
            if second:
                d2, dgb = _rms_bwd(xb_ref[...], gb_ref[...], dx)
                d2_ref[...] = d2.astype(BF16)

            @pl.when(i == 0)
            def _():
                dga_ref[...] = dga
                if second:
                    dgb_ref[...] = dgb

            @pl.when(i > 0)
            def _():
                dga_ref[...] += dga
                if second:
                    dgb_ref[...] += dgb

    row = pl.BlockSpec((tm, d), lambda i, j: (i, 0))
    vec = pl.BlockSpec((1, d), lambda i, j: (0, 0))
    in_specs = [blocks_of(sp, k) for k in range(per) for sp in a_specs] + [blocks_of(w_spec, k) for k in range(per)]
    in_specs += [row, vec, row]
    args = list(a_args) * per + [w_arg] * per + [xa, ga, resid]
    if second:
        in_specs += [row, vec]
        args += [xb, gb]
        out_specs = [row, row, vec, vec]
        out_shape = [_sds((t, d), F32), _sds((t, d), BF16), _sds((1, d), F32), _sds((1, d), F32)]
    else:
        out_specs = [row, vec]
        out_shape = [_sds((t, d), F32), _sds((1, d), F32)]
    return _call(body, name=name, grid=(t // tm, n_steps), in_specs=in_specs, out_specs=out_specs,
                 out_shape=out_shape, scratch_shapes=[pltpu.VMEM((tm, d), F32)])(*args)


def _heads_to_rows(*refs):
    hp = refs[0].shape[0]
    cols = []
    for p in range(hp):
        v = refs[0][p]
        for r in refs[1:]:
            v = v + r[p]
        cols.append(v)
    return jnp.concatenate(cols, axis=-1).astype(BF16)


def _matmul_nt_rows(a, wg, layer, out_dtype, name):
    t, d = a.shape
    tm = ROW_TILE

    def body(a_ref, w_ref, o_ref):
        o_ref[...] = _dot_nt(a_ref[...], w_ref[...].reshape(d, d)).astype(out_dtype)

    row = pl.BlockSpec((tm, d), lambda i: (i, 0))
    return _call(body, name=name, grid=(t // tm,),
                 in_specs=[row, pl.BlockSpec((N_DEV, None, d // N_DEV, d), lambda i: (0, layer, 0, 0))],
                 out_specs=row, out_shape=_sds((t, d), out_dtype))(a, wg)


def _swiglu_bwd(d_ff, wg, layer, gate, up, name):
    t, d = d_ff.shape
    fc = gate.shape[-1]
    rows = wg.shape[2]
    tm = BIG_ROW_TILE

    def body(df_ref, w_ref, g_ref, u_ref, dh_ref):
        w = w_ref[...].reshape(2 * rows, d)
        for r0 in range(0, tm, SWIGLU_ROWS):
            rs = slice(r0, r0 + SWIGLU_ROWS)
            da = _dot_nt(df_ref[rs, :], w)
            gate, up = g_ref[rs, :].astype(F32), u_ref[rs, :].astype(F32)
            sig = jax.nn.sigmoid(gate)
            dh_ref[0, rs, :] = (da * up * (sig * (1.0 + gate * (1.0 - sig)))).astype(BF16)
            dh_ref[1, rs, :] = (da * (gate * sig)).astype(BF16)

    return _call(
        body, name=name, grid=(t // tm, 4),
        in_specs=[pl.BlockSpec((tm, d), lambda i, c: (i, 0)),
                  pl.BlockSpec((2, None, rows, d), lambda i, c: (c, layer, 0, 0)),
                  pl.BlockSpec((None, tm, fc), lambda i, c: (c, i, 0)),
                  pl.BlockSpec((None, tm, fc), lambda i, c: (c, i, 0))],
        out_specs=pl.BlockSpec((None, 2, tm, fc), lambda i, c: (c, 0, i, 0)),
        out_shape=_sds((4, 2, t, fc), BF16))(d_ff, wg, gate, up)


def _conv_bwd(p, d_z, cw, name):
    t, d3 = p.shape
    d = d3 // 3
    tm = ROW_TILE
    hb = tm // HALO
    nt = t // tm

    def body(p_ref, prev_ref, next_ref, dz_ref, dzn_ref, cw_ref, dp_ref, dcw_ref):
        i = pl.program_id(0)
        b = p_ref[:, 0:d].astype(F32)
        c = p_ref[:, d:2 * d].astype(F32)
        h = p_ref[:, 2 * d:3 * d].astype(F32)
        u = c * h
        hu = prev_ref[:, d:2 * d].astype(F32) * prev_ref[:, 2 * d:3 * d].astype(F32) * (i > 0).astype(F32)
        u1, u2 = _shift_down(u, hu, 1, tm), _shift_down(u, hu, 2, tm)
        uc = cw_ref[2:3, :] * u + cw_ref[1:2, :] * u1 + cw_ref[0:1, :] * u2
        dz = dz_ref[...].astype(F32)
        duc = dz * b
        dn = dzn_ref[...].astype(F32) * next_ref[:, 0:d].astype(F32) * (i < nt - 1).astype(F32)
        du = cw_ref[2:3, :] * duc + cw_ref[1:2, :] * _shift_up(duc, dn, 1, tm) + cw_ref[0:1, :] * _shift_up(duc, dn, 2, tm)
        dp_ref[:, 0:d] = (dz * uc).astype(BF16)
        dp_ref[:, d:2 * d] = (du * h).astype(BF16)
        dp_ref[:, 2 * d:3 * d] = (du * c).astype(BF16)
        dcw = jnp.concatenate([jnp.sum(duc * u2, axis=0, keepdims=True), jnp.sum(duc * u1, axis=0, keepdims=True),
                               jnp.sum(duc * u, axis=0, keepdims=True)], axis=0)

        @pl.when(i == 0)
        def _():
            dcw_ref[...] = dcw

        @pl.when(i > 0)
        def _():
            dcw_ref[...] += dcw

    last_halo = t // HALO - 1
    return _call(
        body, name=name, grid=(nt,),
        in_specs=[pl.BlockSpec((tm, d3), lambda i: (i, 0)),
                  pl.BlockSpec((HALO, d3), lambda i: (jnp.maximum(i * hb - 1, 0), 0)),
                  pl.BlockSpec((HALO, d3), lambda i: (jnp.minimum((i + 1) * hb, last_halo), 0)),
                  pl.BlockSpec((tm, d), lambda i: (i, 0)),
                  pl.BlockSpec((HALO, d), lambda i: (jnp.minimum((i + 1) * hb, last_halo), 0)),
                  pl.BlockSpec((3, d), lambda i: (0, 0))],
        out_specs=[pl.BlockSpec((tm, d3), lambda i: (i, 0)), pl.BlockSpec((3, d), lambda i: (0, 0))],
        out_shape=[_sds((t, d3), BF16), _sds((3, d), F32)])(p, p, p, d_z, d_z, cw)


def _grad_weight(a_specs, a_args, a_tile, b_specs, b_args, b_tile, n_out, acc_shape, out_spec, out_shape, t, name,
                 a_transposed=False):
    tt = GRAD_ROW_TILE
    na, nb = len(a_specs), len(b_specs)

    def body(*refs):
        a_refs, b_refs = refs[:na], refs[na:na + nb]
        o_ref, acc_ref = refs[na + nb:]
        s = pl.program_id(1)
        a, b = a_tile(pl.program_id(0), *a_refs), b_tile(pl.program_id(0), *b_refs)
        part = _dot(a, b) if a_transposed else _dot_tn(a, b)

        @pl.when(s == 0)
        def _():
            acc_ref[...] = part

        @pl.when(s > 0)
        def _():
            acc_ref[...] += part

        @pl.when(s == t // tt - 1)
        def _():
            acc = acc_ref[...].astype(BF16)
            if o_ref.shape[-1] == acc.shape[-1]:
                o_ref[...] = acc.reshape(o_ref.shape)
            else:
                for k in range(o_ref.shape[0]):
                    o_ref[k] = acc[:, k * o_ref.shape[-1]:(k + 1) * o_ref.shape[-1]]

    return _call(body, name=name, grid=(n_out, t // tt), in_specs=list(a_specs) + list(b_specs), out_specs=out_spec,
                 out_shape=out_shape, scratch_shapes=[pltpu.VMEM(acc_shape, F32)])(*a_args, *b_args)


def _ident(*args):
    return args[-1][...]


def _heads_tile(j, *refs):
    return _heads_to_rows(*refs)


def kernel(x, norm_g, conv_in_w, conv_w, conv_out_w, kv_norm_g, kv_w, q_w, o_w, ffn_in_w, ffn_out_w, loss_target, m_norm_g, m_conv_in_w, m_conv_w, m_conv_out_w, m_kv_norm_g, m_kv_w, m_q_w, m_o_w, m_ffn_in_w, m_ffn_out_w, v_norm_g, v_conv_in_w, v_conv_w, v_conv_out_w, v_kv_norm_g, v_kv_w, v_q_w, v_o_w, v_ffn_in_w, v_ffn_out_w):
    x0 = x[0]
    target = loss_target[0]
    t, d = x0.shape
    depth = norm_g.shape[0]
    n_a = conv_in_w.shape[0]
    n_b = q_w.shape[0]
    hp = d // LANES
    tm, tg = BWD_ROW_TILE, GRAD_ROW_TILE
    assert t % SUPER == 0 and d % LANES == 0 and depth == n_a + n_b
    dev = 4 * lax.axis_index("x") + 2 * lax.axis_index("y") + lax.axis_index("c")

    n_small = 4 * depth + 3 * n_a
    small_rows = -(-(n_small + 1) // 8) * 8
    small_local = jnp.concatenate([norm_g.reshape(4 * depth, -1), conv_w.reshape(3 * n_a, -1),
                                   jnp.zeros((small_rows - n_small, norm_g.shape[-1]), F32)], axis=0)
    swap = lambda a: jnp.swapaxes(a, 1, 2)
    big = {"conv_in_w": conv_in_w, "conv_out_w": conv_out_w, "kv_w": kv_w[None], "q_w": q_w, "o_w": o_w,
           "ffn_in_w": swap(ffn_in_w), "ffn_out_w": ffn_out_w}
    names = list(big)

    def group(layer):
        if layer < n_a:
            return [("conv_in_w", layer), ("conv_out_w", layer), ("ffn_in_w", layer), ("ffn_out_w", layer)]
        j = layer - n_a
        return ([("kv_w", 0)] if j == 0 else []) + [("q_w", j), ("o_w", j), ("ffn_in_w", layer), ("ffn_out_w", layer)]

    slot = dev.astype(jnp.int32).reshape(1)
    is_ffn = lambda key: key[0].startswith("ffn")
    first_keys = [key for key in group(0) if not is_ffn(key)]
    first = _all_gather([small_local] + [_cast_layer(big[k], i, None, f"cast_{k}_{i}") for k, i in first_keys], "gather_weights")
    small_all = first[0].transpose(1, 0, 2).reshape(small_rows, d)
    wl = {key: a[:, None] for key, a in zip(first_keys, first[1:])}

    def gather_start(keys, after, tag):
        lands = [_cast_layer(big[k], i, slot, f"cast_{k}_{i}") for k, i in keys]
        send_sems, recv_sems, _, lands, tok = _send_start([], lands, after, "gather_start" + tag)
        return (keys, tag, send_sems, recv_sems, lands), tok[0, 0]

    def gather_wait(flight, after):
        keys, tag, send_sems, recv_sems, lands = flight
        _, lands = _send_wait(send_sems, recv_sems, [], lands, after, "gather_wait" + tag)
        wl.update({key: a[:, None] for key, a in zip(keys, lands)})

    in_flight, token = gather_start([key for key in group(0) if is_ffn(key)], small_all, "_l0")
    W = lambda k, i: (wl[(k, i)], 0)
    gain = lambda layer, k: small_all[4 * layer + k][None]
    taps = lambda layer: small_all[4 * depth + 3 * layer: 4 * depth + 3 * layer + 3]
    g_kv = kv_norm_g[None]
    slopes = _alibi_slopes(d // HEAD_DIM)
    fc = big["ffn_in_w"].shape[-2]
    cb = big["conv_in_w"].shape[-1]
    kvb = big["kv_w"].shape[-1]
    q_scale = HEAD_DIM ** -0.5

    saved = []
    kv = kvn_t = None
    xs = x0
    for layer in range(depth):
        tag = f"_l{layer}"
        g0 = g2 = 0.0
        if layer == 0:
            g0 = token
        else:
            gather_wait(in_flight, xs)
            if layer + 1 < depth:
                in_flight, g0 = gather_start(group(layer + 1), xs, f"_l{layer + 1}")
        s = {"x_in": xs}
        g0 = gain(layer, 0) + g0
        if layer < n_a:
            s["p"], s["xn_t"] = _norm_matmul_cols(xs, g0, *W("conv_in_w", layer), "cols", "conv_in" + tag)
            s["z"] = _conv_fwd(s["p"], taps(layer), "conv" + tag)
            s["mix"], x_mid = _matmul_norm_residual(s["z"][None], *W("conv_out_w", layer), gain(layer, 1), xs, "conv_out" + tag)
        else:
            j = layer - n_a
            if kv is None:
                kv, kvn_t = _norm_matmul_cols(xs, g_kv, *W("kv_w", 0), "heads", "kv_proj")
            s["q"], s["xn"] = _norm_matmul_heads(xs, g0, *W("q_w", j), q_scale, "q_proj" + tag)
            s["o"], s["lse"] = _attention_fwd(s["q"], kv, slopes, "attention" + tag)
            s["mix"], x_mid = _matmul_norm_residual(s["o"][None], *W("o_w", j), gain(layer, 1), xs, "o_proj" + tag)
        s["x_mid"] = x_mid
        if layer == 0:
            gather_wait(in_flight, x_mid)
            in_flight, g2 = gather_start(group(1), x_mid, "_l1")
        s["gate"], s["up"], s["a"], s["fn"] = _ffn_in_swiglu(x_mid, gain(layer, 2) + g2, *W("ffn_in_w", layer), "ffn_in" + tag)
        s["ff"], xs = _matmul_norm_residual(s["a"], *W("ffn_out_w", layer), gain(layer, 3), x_mid, "ffn_out" + tag)
        saved.append(s)

    last = saved[-1]
    sq, dx_out, d_ff, dg3 = _loss_head(xs, target, last["ff"], gain(depth - 1, 3), "loss_head")
    loss = lax.psum(sq[0, 0] * (0.5 / d), ("x", "y", "c"))

    dgain = {(depth - 1, 3): dg3}
    dtaps = {}
    grads = {k: [None] * big[k].shape[0] for k in names}
    dkv_parts = []
    scattering = []

    def scatter_start(keys, tag):
        parts = [grads[k][i] for k, i in keys]
        zones = [lax.empty(p.shape, p.dtype) for p in parts]
        send_sems, recv_sems, parts, zones, tok = _send_start(parts, zones, small_all, "scatter_start" + tag)
        scattering.append((keys, tag, send_sems, recv_sems, parts, zones))
        return tok[0, 0]

    for layer in reversed(range(depth)):
        tag = f"_l{layer}"
        s = saved[layer]
        dh = _swiglu_bwd(d_ff, *W("ffn_out_w", layer), s["gate"], s["up"], "swiglu_bwd" + tag)
        rows_out = big["ffn_out_w"].shape[1]
        grads["ffn_out_w"][layer] = _grad_weight(
            [pl.BlockSpec((None, tg, fc), lambda c, i: (c, i, 0))], [s["a"]], _ident,
            [pl.BlockSpec((tg, d), lambda c, i: (i, 0))], [d_ff], _ident,
            4, (fc, d), pl.BlockSpec((2, rows_out, d), lambda c, i: (c, 0, 0)), _sds((N_DEV, rows_out, d), BF16), t,
            "grad_ffn_out" + tag)
        grads["ffn_in_w"][layer] = _grad_weight(
            [pl.BlockSpec((None, None, tg, fc), lambda j, i: (j % 4, j // 4, i, 0))], [dh], _ident,
            [pl.BlockSpec((tg, d), lambda j, i: (i, 0))], [s["fn"]], _ident,
            N_DEV, (fc, d), pl.BlockSpec((None, fc, d), lambda j, i: (j, 0, 0)), _sds((N_DEV, fc, d), BF16), t,
            "grad_ffn_in" + tag)
        tok = scatter_start([("ffn_in_w", layer), ("ffn_out_w", layer)], "_ffn" + tag)
        dx_mid, d_mix, dg2, dg1 = _bwd_matmul_norms(
            [pl.BlockSpec((None, None, tm, fc), lambda i, j: (j % 4, j // 4, i, 0))], [dh], _ident, N_DEV,
            pl.BlockSpec((None, None, fc, d), lambda i, j: (j, 0, 0, 0)), W("ffn_in_w", layer)[0], _ident,
            s["x_mid"], gain(layer, 2) + tok, dx_out, s["mix"], gain(layer, 1), "ffn_in_bwd" + tag, w_transposed=True)
        dgain[(layer, 2)], dgain[(layer, 1)] = dg2, dg1
        full_rows = pl.BlockSpec((N_DEV, d // N_DEV, d), lambda j, i: (0, 0, 0))
        rows_w = lambda wname, idx: (pl.BlockSpec((N_DEV, None, d // N_DEV, d), lambda i, j: (0, 0, 0, 0)), W(wname, idx)[0],
                                     lambda w_ref: w_ref[...].reshape(d, d))
        if layer < n_a:
            d_z = _matmul_nt_rows(d_mix, *W("conv_out_w", layer), BF16, "conv_out_bwd" + tag)
            grads["conv_out_w"][layer] = _grad_weight(
                [pl.BlockSpec((tg, d), lambda j, i: (i, 0))], [s["z"]], _ident,
                [pl.BlockSpec((tg, d), lambda j, i: (i, 0))], [d_mix], _ident,
                1, (d, d), full_rows, _sds((N_DEV, d // N_DEV, d), BF16), t, "grad_conv_out" + tag)
            d_p, dtaps[layer] = _conv_bwd(s["p"], d_z, taps(layer), "conv_bwd" + tag)
            grads["conv_in_w"][layer] = _grad_weight(
                [pl.BlockSpec((d, tg), lambda j, i: (0, i))], [s["xn_t"]], _ident,
                [pl.BlockSpec((tg, 2 * cb), lambda j, i: (i, j))], [d_p], _ident,
                N_DEV // 2, (d, 2 * cb), pl.BlockSpec((2, d, cb), lambda j, i: (j, 0, 0)), _sds((N_DEV, d, cb), BF16), t,
                "grad_conv_in" + tag, a_transposed=True)
            a_specs, a_args, a_tile, n_steps = [pl.BlockSpec((tm, N_DEV * cb), lambda i, j: (i, 0))], [d_p], _ident, 1
            w_spec = pl.BlockSpec((N_DEV, None, d, cb), lambda i, j: (0, 0, 0, 0))
            w_arg = W("conv_in_w", layer)[0]
            w_mat = lambda w_ref: jnp.concatenate([w_ref[k] for k in range(N_DEV)], axis=1)
            resid = dx_mid
        else:
            j_b = layer - n_a
            d_o = _matmul_nt_rows(d_mix, *W("o_w", j_b), F32, "o_proj_bwd" + tag)
            grads["o_w"][j_b] = _grad_weight(
                [pl.BlockSpec((tg, d), lambda j, i: (i, 0))], [s["o"]], _ident,
                [pl.BlockSpec((tg, d), lambda j, i: (i, 0))], [d_mix], _ident,
                1, (d, d), full_rows, _sds((N_DEV, d // N_DEV, d), BF16), t, "grad_o" + tag)
            dk_in, dv_in = dkv_parts[0] if dkv_parts else (None, None)
            dq, dk, dv = _attention_bwd(s["q"], kv, s["o"], s["lse"], d_o, dk_in, dv_in, slopes, q_scale, "attention_bwd" + tag)
            dkv_parts = [(dk, dv)]
            heads_spec = pl.BlockSpec((hp, tg, LANES), lambda j, i: (0, i, 0))
            grads["q_w"][j_b] = _grad_weight(
                [pl.BlockSpec((tg, d), lambda j, i: (i, 0))], [s["xn"]], _ident,
                [heads_spec], [dq], _heads_tile,
                1, (d, d), full_rows, _sds((N_DEV, d // N_DEV, d), BF16), t, "grad_q" + tag)
            a_specs, a_args, a_tile, n_steps = [pl.BlockSpec((hp, tm, LANES), lambda i, j: (0, i, 0))], [dq], _heads_tile, 1
            w_spec, w_arg, w_mat = rows_w("q_w", j_b)
            resid = dx_mid
            if layer == n_a:
                pieces = kvb // LANES
                halves = []
                for src in (0, 1):
                    halves.append([part[src] for part in dkv_parts])
                n_half = len(dkv_parts)
                kv_args = [arr for src in (0, 1) for arr in halves[src]]

                def kv_block(src, j):
                    return jnp.where((j // 4) == src, j % 4, 0)

                def kv_tile(j, *refs):
                    keys = _heads_to_rows(*refs[:n_half])
                    vals = _heads_to_rows(*refs[n_half:])
                    return jnp.where(j < 4, keys, vals)

                kv_specs = [pl.BlockSpec((pieces, tm, LANES), functools.partial(lambda i, j, src: (kv_block(src, j), i, 0), src=src))
                            for src in (0, 1) for _ in range(n_half)]
                resid, dgain["kv"] = _bwd_matmul_norms(
                    kv_specs, kv_args, kv_tile, N_DEV,
                    pl.BlockSpec((None, None, d, kvb), lambda i, j: (j, 0, 0, 0)), W("kv_w", 0)[0], _ident,
                    s["x_in"], g_kv, dx_mid, None, None, "kv_proj_bwd")
                kv_b_specs = [pl.BlockSpec((pieces, tg, LANES), functools.partial(lambda j, i, src: (kv_block(src, j), i, 0), src=src))
                              for src in (0, 1) for _ in range(n_half)]
                grads["kv_w"][0] = _grad_weight(
                    [pl.BlockSpec((d, tg), lambda j, i: (0, i))], [kvn_t], _ident,
                    kv_b_specs, kv_args, kv_tile,
                    N_DEV, (d, kvb), pl.BlockSpec((None, d, kvb), lambda j, i: (j, 0, 0)), _sds((N_DEV, d, kvb), BF16), t,
                    "grad_kv", a_transposed=True)
        tok = scatter_start([key for key in group(layer) if not key[0].startswith("ffn")], "_mix" + tag)
        if layer > 0:
            prev = saved[layer - 1]
            dx_out, d_ff, dg0, dg3p = _bwd_matmul_norms(
                a_specs, a_args, a_tile, n_steps, w_spec, w_arg, w_mat,
                s["x_in"], gain(layer, 0) + tok, resid, prev["ff"], gain(layer - 1, 3), "mixer_in_bwd" + tag)
            dgain[(layer, 0)], dgain[(layer - 1, 3)] = dg0, dg3p
        else:
            grad_x, dg0 = _bwd_matmul_norms(
                a_specs, a_args, a_tile, n_steps, w_spec, w_arg, w_mat,
                s["x_in"], gain(layer, 0), resid, None, None, "mixer_in_bwd" + tag)
            dgain[(layer, 0)] = dg0

    small_grad = jnp.concatenate(
        [dgain[(layer, k)] for layer in range(depth) for k in range(4)] + [dtaps[layer] for layer in range(n_a)]
        + [dgain["kv"]] + [jnp.zeros((small_rows - n_small - 1, d), F32)], axis=0)
    small_grads_all = _all_gather([small_grad], "gather_small_grads")[0]
    lo = dev * (d // N_DEV)

    def pack(ng, cwp, kvg):
        rows = jnp.concatenate([ng.reshape(4 * depth, -1), cwp.reshape(3 * n_a, -1)], axis=0)
        z = lax.dynamic_update_slice(jnp.zeros((small_rows, d), F32), rows, (0, lo))
        return lax.dynamic_update_slice(z, kvg[None], (n_small, 0))

    w_small = lax.dynamic_update_slice(small_all, g_kv, (n_small, 0))
    m_small, v_small = pack(m_norm_g, m_conv_w, m_kv_norm_g), pack(v_norm_g, v_conv_w, v_kv_norm_g)
    sm = _small_adamw(small_grads_all, w_small, m_small, v_small, "adamw_small")

    def unpack(a):
        mine = lax.dynamic_slice(a, (0, lo), (small_rows, d // N_DEV))
        return (mine[:4 * depth].reshape(norm_g.shape), mine[4 * depth:n_small].reshape(conv_w.shape), a[n_small])

    small_out = [unpack(a) for a in sm]

    moments = {"conv_in_w": (m_conv_in_w, v_conv_in_w), "conv_out_w": (m_conv_out_w, v_conv_out_w),
               "kv_w": (m_kv_w[None], v_kv_w[None]), "q_w": (m_q_w, v_q_w), "o_w": (m_o_w, v_o_w),
               "ffn_in_w": (swap(m_ffn_in_w), swap(v_ffn_in_w)), "ffn_out_w": (m_ffn_out_w, v_ffn_out_w)}
    landed = {k: [None] * big[k].shape[0] for k in names}
    for keys, tag, send_sems, recv_sems, parts, zones in scattering:
        parts, zones = _send_wait(send_sems, recv_sems, parts, zones, grad_x, "scatter_wait" + tag)
        for (k, i), part, zone in zip(keys, parts, zones):
            landed[k][i] = (part, zone)
    res = {k: _sum_adamw(slot, [p for p, _ in landed[k]], [z for _, z in landed[k]], big[k], moments[k][0], moments[k][1],
                         "adamw_" + k) for k in names}

    def big_out(k, which):
        out = res[k][which]
        return out[0] if k == "kv_w" else swap(out) if k == "ffn_in_w" else out

    out_names = ["norm_g", "conv_in_w", "conv_w", "conv_out_w", "kv_norm_g", "kv_w", "q_w", "o_w", "ffn_in_w", "ffn_out_w"]
    small_pos = {"norm_g": 0, "conv_w": 1, "kv_norm_g": 2}
    outs = [loss, grad_x[None]]
    for which in range(4):
        for k in out_names:
            outs.append(small_out[which][small_pos[k]] if k in small_pos else big_out(k, which))
    return tuple(outs)
```

```python
import functools
import math

import numpy as np
import jax
import jax.numpy as jnp
from jax import lax
from jax.experimental import pallas as pl
from jax.experimental.pallas import tpu as pltpu

F32 = jnp.float32
BF16 = jnp.bfloat16

N_DEV = 8
RMS_EPS = 1e-6
HEAD_DIM = 64
LANES = 128
ATT_BLOCK = 128
DILATIONS = (1, 4, 16)
SUPER = ATT_BLOCK * DILATIONS[-1]
NEG = -1e30

ADAM_LR, ADAM_B1, ADAM_B2, ADAM_EPS, ADAM_WD, ADAM_STEP = 0.001, 0.9, 0.999, 1e-08, 0.01, 10

ROW_TILE = 512
BIG_ROW_TILE = 1024
SWIGLU_ROWS = 256
GRAD_ROW_TILE = 2048
BWD_ROW_TILE = 512
MESH = pl.DeviceIdType.MESH


def _call(body, *, name, grid=None, in_specs=None, out_specs=None, out_shape=None, scratch_shapes=(), prefetch=False,
          **params):
    cp = pltpu.CompilerParams(**params) if params else None
    if prefetch:
        spec = pltpu.PrefetchScalarGridSpec(num_scalar_prefetch=1, grid=grid, in_specs=in_specs, out_specs=out_specs,
                                            scratch_shapes=list(scratch_shapes))
        return pl.pallas_call(body, name=name, grid_spec=spec, out_shape=out_shape, compiler_params=cp)
    kwargs = {k: v for k, v in (("grid", grid), ("in_specs", in_specs), ("out_specs", out_specs)) if v is not None}
    return pl.pallas_call(body, name=name, out_shape=out_shape, scratch_shapes=list(scratch_shapes),
                          compiler_params=cp, **kwargs)


def _sds(shape, dtype):
    return jax.ShapeDtypeStruct(tuple(shape), dtype)


def _rms(x, g):
    r = lax.rsqrt(jnp.mean(x * x, axis=-1, keepdims=True) + RMS_EPS)
    return x * r * g


def _rms_bwd(x, g, dy):
    r = lax.rsqrt(jnp.mean(x * x, axis=-1, keepdims=True) + RMS_EPS)
    xh = x * r
    dxh = dy * g
    dx = r * (dxh - xh * jnp.mean(dxh * xh, axis=-1, keepdims=True))
    return dx, jnp.sum(dy * xh, axis=0, keepdims=True)


def _dot(a, b):
    return jnp.dot(a, b, preferred_element_type=F32)


def _dot_nt(a, b):
    return lax.dot_general(a, b, (((1,), (1,)), ((), ())), preferred_element_type=F32)


def _dot_tn(a, b):
    return lax.dot_general(a, b, (((0,), (0,)), ((), ())), preferred_element_type=F32)


def _mesh_pos():
    return lax.axis_index("x"), lax.axis_index("y"), lax.axis_index("c")


def _all_gather(arrs, name):
    n = len(arrs)

    def body(*refs):
        ins, outs = refs[:n], refs[n:2 * n]
        send_sems, recv_sems, local_sems = refs[2 * n:]
        x, y, c = _mesh_pos()
        me, sibling = (x, y, c), (x, y, 1 - c)
        chips = [(1 - x, y), (x, 1 - y), (1 - x, 1 - y)]

        def copy(a, k, block, to, src=None):
            dst = outs[a].at[4 * block[0] + 2 * block[1] + block[2]]
            return pltpu.make_async_remote_copy(
                src_ref=dst if src is None else src, dst_ref=dst, send_sem=send_sems.at[a, k],
                recv_sem=recv_sems.at[a, k], device_id=to, device_id_type=MESH)

        started = []
        for a in range(n):
            mine = pltpu.make_async_copy(ins[a], outs[a].at[4 * x + 2 * y + c], local_sems.at[a])
            mine.start()
            started.append(mine)
        first = []
        for a in range(n):
            first.append(copy(a, 0, me, sibling, src=ins[a]))
            first += [copy(a, 1 + j, me, (*chip, c), src=ins[a]) for j, chip in enumerate(chips)]
        for cp in first:
            cp.start()
        passed = []
        for a in range(n):
            for j, chip in enumerate(chips):
                copy(a, 1 + j, (*chip, c), me).wait_recv()
                fwd = copy(a, 4 + j, (*chip, c), sibling)
                fwd.start()
                passed.append(fwd)
        for a in range(n):
            copy(a, 0, sibling, me).wait_recv()
            for j, chip in enumerate(chips):
                copy(a, 4 + j, (*chip, 1 - c), me).wait_recv()
        for cp in first + passed:
            cp.wait_send()
        for cp in started:
            cp.wait()

    any_spec = pl.BlockSpec(memory_space=pl.ANY)
    outs = _call(
        body, name=name, in_specs=[any_spec] * n, out_specs=[any_spec] * n,
        out_shape=[_sds((N_DEV,) + a.shape, a.dtype) for a in arrs],
        scratch_shapes=[pltpu.SemaphoreType.DMA((n, 7)), pltpu.SemaphoreType.DMA((n, 7)), pltpu.SemaphoreType.DMA((n,))],
        has_side_effects=True,
    )(*arrs)
    return list(outs)


HBM_SPEC = pl.BlockSpec(memory_space=pltpu.HBM)
SEM_SPEC = pl.BlockSpec(memory_space=pltpu.SEMAPHORE)
DATAFLOW = pltpu.SideEffectType.DATAFLOW_SIDE_EFFECTING
PEERS = [(dx, dy, dc) for dx in (0, 1) for dy in (0, 1) for dc in (0, 1)][1:]


def _peer(flip):
    x, y, c = _mesh_pos()
    return tuple(1 - v if f else v for v, f in zip((x, y, c), flip))


def _slot(pos):
    return 4 * pos[0] + 2 * pos[1] + pos[2]


def _in_hbm(a):
    return pltpu.with_memory_space_constraint(a, pltpu.HBM)


def _direct_copies(srcs, lands, send_sems, recv_sems, scatter):
    me = _slot(_mesh_pos())
    copies = []
    for a in range(len(lands)):
        for k, flip in enumerate(PEERS):
            peer = _peer(flip)
            src = srcs[a].at[_slot(peer)] if scatter else lands[a].at[me]
            idx = a * len(PEERS) + k
            copies.append(pltpu.make_async_remote_copy(
                src_ref=src, dst_ref=lands[a].at[me], send_sem=send_sems.at[idx], recv_sem=recv_sems.at[idx],
                device_id=peer, device_id_type=MESH))
    return copies


def _send_start(srcs, lands, after, name):
    ns, nl = len(srcs), len(lands)
    scatter = ns > 0

    def body(*refs):
        src_refs, land_refs = refs[:ns], refs[ns:ns + nl]
        send_sems, recv_sems = refs[ns + nl + 1:ns + nl + 3]
        token = refs[-1]
        for cp in _direct_copies(src_refs, land_refs, send_sems, recv_sems, scatter):
            cp.start()
        token[...] = jnp.zeros_like(token)

    sem = pltpu.SemaphoreType.DMA((nl * len(PEERS),))
    outs = pl.pallas_call(
        body, name=name,
        out_shape=(sem, sem) + tuple(pltpu.HBM(a.shape, a.dtype) for a in list(srcs) + list(lands))
        + (_sds((8, LANES), F32),),
        in_specs=[HBM_SPEC] * (ns + nl) + [pl.BlockSpec(memory_space=pl.ANY)],
        out_specs=(SEM_SPEC, SEM_SPEC) + (HBM_SPEC,) * (ns + nl) + (pl.BlockSpec(memory_space=pltpu.VMEM),),
        input_output_aliases={i: 2 + i for i in range(ns + nl)},
        compiler_params=pltpu.CompilerParams(has_side_effects=DATAFLOW),
    )(*[_in_hbm(a) for a in list(srcs) + list(lands)], after)
    send_sems, recv_sems = outs[0], outs[1]
    return send_sems, recv_sems, list(outs[2:2 + ns]), list(outs[2 + ns:2 + ns + nl]), outs[-1]


def _send_wait(send_sems, recv_sems, srcs, lands, after, name):
    ns, nl = len(srcs), len(lands)
    scatter = ns > 0

    def body(*refs):
        src_refs, land_refs = refs[:ns], refs[ns:ns + nl]
        send_sems, recv_sems = refs[ns + nl:ns + nl + 2]
        copies = _direct_copies(src_refs, land_refs, send_sems, recv_sems, scatter)
        for cp in copies:
            cp.wait_send()
        for cp in copies:
            cp.wait_recv()

    outs = pl.pallas_call(
        body, name=name,
        out_shape=tuple(pltpu.HBM(a.shape, a.dtype) for a in list(srcs) + list(lands)),
        in_specs=[HBM_SPEC] * (ns + nl) + [SEM_SPEC, SEM_SPEC, pl.BlockSpec(memory_space=pl.ANY)],
        out_specs=(HBM_SPEC,) * (ns + nl),
        input_output_aliases={i: i for i in range(ns + nl)},
        compiler_params=pltpu.CompilerParams(has_side_effects=DATAFLOW),
    )(*srcs, *lands, send_sems, recv_sems, after)
    return list(outs[:ns]), list(outs[ns:])


def _row_tile(rows, cap=512):
    t = min(rows, cap)
    while rows % t or (t % 16 and t != rows):
        t -= 1
    return t


def _as2d(a):
    return a.reshape(-1, a.shape[-1])


def _cast_layer(w, layer, slot, name):
    _, rows, cols = w.shape
    tr = _row_tile(rows)

    def body(*refs):
        refs[-1][...] = refs[-2][...].astype(BF16)

    if slot is None:
        return _call(body, name=name, grid=(rows // tr,),
                     in_specs=[pl.BlockSpec((None, tr, cols), lambda i: (layer, i, 0))],
                     out_specs=pl.BlockSpec((tr, cols), lambda i: (i, 0)), out_shape=_sds((rows, cols), BF16))(w)
    return _call(body, name=name, grid=(rows // tr,), prefetch=True,
                 in_specs=[pl.BlockSpec((None, tr, cols), lambda i, s: (layer, i, 0))],
                 out_specs=pl.BlockSpec((None, tr, cols), lambda i, s: (s[0], i, 0)),
                 out_shape=_sds((N_DEV, rows, cols), BF16))(slot, w)


def _sum_adamw(slot, parts, lands, w, m, v, name):
    n_l = len(parts)
    _, rows, cols = lands[0].shape
    tr = _row_tile(rows, 128)

    def body(s_ref, *refs):
        p_refs, l_refs = refs[:n_l], refs[n_l:2 * n_l]
        w_ref, m_ref, v_ref, g_ref, d_ref, nm_ref, nv_ref = refs[2 * n_l:]
        for k in range(n_l):
            @pl.when(pl.program_id(0) == k)
            def _(k=k):
                own = p_refs[k][...]
                g = jnp.zeros((tr, cols), F32)
                for j in range(N_DEV):
                    g = g + jnp.where(s_ref[0] == j, own, l_refs[k][j]).astype(F32)
                delta, nm, nv = _adamw_math(w_ref[...], g, m_ref[...], v_ref[...])
                g_ref[...] = g
                d_ref[...] = delta
                nm_ref[...] = nm
                nv_ref[...] = nv

    def own_block(k):
        return pl.BlockSpec((None, tr, cols), lambda l, i, s: (s[0], jnp.where(l == k, i, 0), 0))

    def zone_block(k):
        return pl.BlockSpec((N_DEV, tr, cols), lambda l, i, s: (0, jnp.where(l == k, i, 0), 0))

    lay = pl.BlockSpec((None, tr, cols), lambda l, i, s: (l, i, 0))
    return _call(body, name=name, grid=(n_l, rows // tr), prefetch=True,
                 in_specs=[own_block(k) for k in range(n_l)] + [zone_block(k) for k in range(n_l)] + [lay, lay, lay],
                 out_specs=[lay] * 4, out_shape=[_sds((n_l, rows, cols), F32)] * 4)(slot, *parts, *lands, w, m, v)


def _adamw_math(w, g, m, v):
    m = ADAM_B1 * m + (1.0 - ADAM_B1) * g
    v = ADAM_B2 * v + (1.0 - ADAM_B2) * (g * g)
    m_hat = m / (1.0 - ADAM_B1 ** ADAM_STEP)
    v_hat = v / (1.0 - ADAM_B2 ** ADAM_STEP)
    delta = -ADAM_LR * (m_hat / (jnp.sqrt(v_hat) + ADAM_EPS) + ADAM_WD * w)
    return delta, m, v


def _small_adamw(gathered, w, m, v, name):
    def body(a_ref, w_ref, m_ref, v_ref, g_ref, d_ref, nm_ref, nv_ref):
        g = a_ref[0]
        for k in range(1, N_DEV):
            g = g + a_ref[k]
        delta, nm, nv = _adamw_math(w_ref[...], g, m_ref[...], v_ref[...])
        g_ref[...] = g
        d_ref[...] = delta
        nm_ref[...] = nm
        nv_ref[...] = nv

    return _call(body, name=name, out_shape=[_sds(w.shape, F32)] * 4)(gathered, w, m, v)


def _norm_matmul_cols(x, g, wg, layer, mode, name):
    t, d = x.shape
    nb = wg.shape[-1]
    tm = BIG_ROW_TILE
    per = 2
    pieces = per * nb // LANES

    def body(x_ref, g_ref, w_ref, y_ref, xnt_ref):
        xn = _rms(x_ref[...], g_ref[...])
        xnt_ref[...] = xn.T.astype(BF16)
        xn = xn.astype(BF16)
        for j in range(N_DEV // per):
            y = _dot(xn, jnp.concatenate([w_ref[per * j + k] for k in range(per)], axis=1))
            if mode == "heads":
                for p in range(pieces):
                    y_ref[pieces * j + p] = y[:, p * LANES:(p + 1) * LANES]
            else:
                y_ref[:, j * per * nb:(j + 1) * per * nb] = y.astype(BF16)

    if mode == "cols":
        y_shape, y_spec = _sds((t, N_DEV * nb), BF16), pl.BlockSpec((tm, N_DEV * nb), lambda i: (i, 0))
    else:
        y_shape = _sds((N_DEV // per * pieces, t, LANES), F32)
        y_spec = pl.BlockSpec((N_DEV // per * pieces, tm, LANES), lambda i: (0, i, 0))
    return _call(
        body, name=name, grid=(t // tm,),
        in_specs=[pl.BlockSpec((tm, d), lambda i: (i, 0)), pl.BlockSpec((1, d), lambda i: (0, 0)),
                  pl.BlockSpec((N_DEV, None, d, nb), lambda i: (0, layer, 0, 0))],
        out_specs=[y_spec, pl.BlockSpec((d, tm), lambda i: (0, i))],
        out_shape=[y_shape, _sds((d, t), BF16)])(x, g, wg)


def _ffn_in_swiglu(x, g, wg, layer, name):
    t, d = x.shape
    fc = wg.shape[-2]
    tm = ROW_TILE

    def body(x_ref, g_ref, w_ref, gate_ref, up_ref, a_ref, xn_ref):
        xn = _rms(x_ref[...], g_ref[...]).astype(BF16)
        xn_ref[...] = xn
        for c in range(4):
            gate, up = _dot_nt(xn, w_ref[c]), _dot_nt(xn, w_ref[c + 4])
            gate_ref[c] = gate.astype(BF16)
            up_ref[c] = up.astype(BF16)
            a_ref[c] = (gate * jax.nn.sigmoid(gate) * up).astype(BF16)

    chunks = pl.BlockSpec((4, tm, fc), lambda i: (0, i, 0))
    return _call(
        body, name=name, grid=(t // tm,),
        in_specs=[pl.BlockSpec((tm, d), lambda i: (i, 0)), pl.BlockSpec((1, d), lambda i: (0, 0)),
                  pl.BlockSpec((N_DEV, None, fc, d), lambda i: (0, layer, 0, 0))],
        out_specs=[chunks, chunks, chunks, pl.BlockSpec((tm, d), lambda i: (i, 0))],
        out_shape=[_sds((4, t, fc), BF16)] * 3 + [_sds((t, d), BF16)])(x, g, wg)


def _norm_matmul_heads(x, g, wg, layer, scale, name):
    t, d = x.shape
    tm = ROW_TILE
    hp = d // LANES

    def body(x_ref, g_ref, w_ref, y_ref, xn_ref):
        xn = _rms(x_ref[...], g_ref[...]).astype(BF16)
        xn_ref[...] = xn
        y = _dot(xn, w_ref[...].reshape(d, d)) * scale
        for p in range(hp):
            y_ref[p] = y[:, p * LANES:(p + 1) * LANES]

    return _call(
        body, name=name, grid=(t // tm,),
        in_specs=[pl.BlockSpec((tm, d), lambda i: (i, 0)), pl.BlockSpec((1, d), lambda i: (0, 0)),
                  pl.BlockSpec((N_DEV, None, d // N_DEV, d), lambda i: (0, layer, 0, 0))],
        out_specs=[pl.BlockSpec((hp, tm, LANES), lambda i: (0, i, 0)), pl.BlockSpec((tm, d), lambda i: (i, 0))],
        out_shape=[_sds((hp, t, LANES), F32), _sds((t, d), BF16)])(x, g, wg)


def _shift_down(u, halo, k, tm):
    row = lax.broadcasted_iota(jnp.int32, u.shape, 0)
    out = pltpu.roll(u, k, 0)
    for j in range(k):
        out = jnp.where(row == j, halo[halo.shape[0] - k + j:halo.shape[0] - k + j + 1, :], out)
    return out


def _shift_up(u, halo, k, tm):
    row = lax.broadcasted_iota(jnp.int32, u.shape, 0)
    out = pltpu.roll(u, tm - k, 0)
    for j in range(k):
        out = jnp.where(row == tm - k + j, halo[j:j + 1, :], out)
    return out


HALO = 16


def _conv_fwd(p, cw, name):
    t, d3 = p.shape
    d = d3 // 3
    tm = ROW_TILE
    hb = tm // HALO

    def body(p_ref, prev_ref, cw_ref, z_ref):
        i = pl.program_id(0)
        b = p_ref[:, 0:d].astype(F32)
        u = p_ref[:, d:2 * d].astype(F32) * p_ref[:, 2 * d:3 * d].astype(F32)
        keep = (i > 0).astype(F32)
        hu = prev_ref[:, d:2 * d].astype(F32) * prev_ref[:, 2 * d:3 * d].astype(F32) * keep
        uc = cw_ref[2:3, :] * u + cw_ref[1:2, :] * _shift_down(u, hu, 1, tm) + cw_ref[0:1, :] * _shift_down(u, hu, 2, tm)
        z_ref[...] = (b * uc).astype(BF16)

    return _call(
        body, name=name, grid=(t // tm,),
        in_specs=[pl.BlockSpec((tm, d3), lambda i: (i, 0)),
                  pl.BlockSpec((HALO, d3), lambda i: (jnp.maximum(i * hb - 1, 0), 0)),
                  pl.BlockSpec((3, d), lambda i: (0, 0))],
        out_specs=pl.BlockSpec((tm, d), lambda i: (i, 0)), out_shape=_sds((t, d), BF16))(p, p, cw)


def _matmul_norm_residual(a3, wg, layer, g, x_res, name):
    kc_n, t, kc = a3.shape
    d = wg.shape[-1]
    per = N_DEV // kc_n
    rows = wg.shape[2]
    tm = ROW_TILE

    def body(a_ref, w_ref, g_ref, x_ref, raw_ref, xo_ref):
        raw = None
        for c in range(kc_n):
            term = _dot(a_ref[c], w_ref[c * per:(c + 1) * per].reshape(per * rows, d))
            raw = term if raw is None else raw + term
        raw_ref[...] = raw
        xo_ref[...] = x_ref[...] + _rms(raw, g_ref[...])

    row_spec = pl.BlockSpec((tm, d), lambda i: (i, 0))
    return _call(
        body, name=name, grid=(t // tm,),
        in_specs=[pl.BlockSpec((kc_n, tm, kc), lambda i: (0, i, 0)),
                  pl.BlockSpec((N_DEV, None, rows, d), lambda i: (0, layer, 0, 0)),
                  pl.BlockSpec((1, d), lambda i: (0, 0)), row_spec],
        out_specs=[row_spec, row_spec], out_shape=[_sds((t, d), F32)] * 2)(a3, wg, g, x_res)


def _alibi_slopes(n_heads):
    hh = np.arange(n_heads, dtype=np.float32) + 1.0
    s = np.power(2.0, -8.0 * hh / n_heads).astype(np.float32)
    return jnp.asarray(np.repeat(s.reshape(n_heads // 2, 2, 1), 2 * ATT_BLOCK, axis=2))


def _band_bias(sl_ref, dil):
    u = lax.broadcasted_iota(jnp.int32, (ATT_BLOCK, 2 * ATT_BLOCK), 0)
    kk = lax.broadcasted_iota(jnp.int32, (ATT_BLOCK, 2 * ATT_BLOCK), 1)
    delta = u + ATT_BLOCK - kk
    valid = (delta >= 0) & (delta <= ATT_BLOCK)
    dist = (delta * dil).astype(F32)
    rows = [jnp.where(valid, -sl_ref[hd:hd + 1, :] * dist, NEG) for hd in range(2)]
    return jnp.concatenate(rows, axis=0)


def _stack_heads(a):
    lane = lax.broadcasted_iota(jnp.int32, a.shape, 1)
    return jnp.concatenate([jnp.where(lane < HEAD_DIM, a, 0.0), jnp.where(lane >= HEAD_DIM, a, 0.0)], axis=0).astype(BF16)


def _unstack_heads(a2):
    top, bot = a2[:ATT_BLOCK], a2[ATT_BLOCK:]
    lane = lax.broadcasted_iota(jnp.int32, top.shape, 1)
    return jnp.where(lane < HEAD_DIM, top, bot)


def _rows_to_lanes(a0, a1):
    eye = lax.broadcasted_iota(jnp.int32, a0.shape, 0) == lax.broadcasted_iota(jnp.int32, a0.shape, 1)
    return jnp.concatenate([jnp.sum(jnp.where(eye, a, 0.0), axis=0, keepdims=True) for a in (a0, a1)], axis=1)


def _fill_bias_t(sl_ref, bias_ref):
    kk = lax.broadcasted_iota(jnp.int32, (2 * ATT_BLOCK, 2 * ATT_BLOCK), 0)
    lane = lax.broadcasted_iota(jnp.int32, (2 * ATT_BLOCK, 2 * ATT_BLOCK), 1)
    delta = lane % ATT_BLOCK + ATT_BLOCK - kk
    valid = (delta >= 0) & (delta <= ATT_BLOCK)
    slope = jnp.concatenate([sl_ref[0:1, :ATT_BLOCK], sl_ref[1:2, :ATT_BLOCK]], axis=1)
    for gi, dil in enumerate(DILATIONS):
        bias = jnp.where(valid, -slope * (delta * dil).astype(F32), NEG)
        bias_ref[2 * gi] = bias
        bias_ref[2 * gi + 1] = jnp.where(kk < ATT_BLOCK, NEG, bias)


def _fill_bias(sl_ref, bias_ref):
    kk = lax.broadcasted_iota(jnp.int32, (2 * ATT_BLOCK, 2 * ATT_BLOCK), 1)
    for gi, dil in enumerate(DILATIONS):
        bias = _band_bias(sl_ref, dil)
        bias_ref[2 * gi] = bias
        bias_ref[2 * gi + 1] = jnp.where(kk < ATT_BLOCK, NEG, bias)


def _strided_keys(dil, r, b, kc_ref, kp_ref, vc_ref, vp_ref, kcar_ref, vcar_ref):
    if b > 0:
        keys = pl.ds((b - 1) * (ATT_BLOCK * dil) + r, 2 * ATT_BLOCK, stride=dil)
        return kc_ref[keys, :].astype(BF16), vc_ref[keys, :].astype(BF16)
    own = pl.ds(r, ATT_BLOCK, stride=dil)
    k_own, v_own = kc_ref[own, :].astype(BF16), vc_ref[own, :].astype(BF16)
    if dil * ATT_BLOCK != SUPER:
        before = pl.ds(SUPER - ATT_BLOCK * dil + r, ATT_BLOCK, stride=dil)
        k_before, v_before = kp_ref[before, :].astype(BF16), vp_ref[before, :].astype(BF16)
    else:
        k_before, v_before = kcar_ref[r], vcar_ref[r]
        kcar_ref[r] = k_own
        vcar_ref[r] = v_own
    return jnp.concatenate([k_before, k_own], axis=0), jnp.concatenate([v_before, v_own], axis=0)


def _attention_fwd(q, kv, slopes, name):
    hp, t, _ = q.shape
    ns = t // SUPER
    nd = len(DILATIONS)

    def body(sl_ref, q_ref, kc_ref, kp_ref, vc_ref, vp_ref, o_ref, lse_ref, og_ref, lg_ref, bias_ref, kcar_ref, vcar_ref):
        n = pl.program_id(1)

        @pl.when(n == 0)
        def _():
            _fill_bias(sl_ref, bias_ref)
            kcar_ref[...] = jnp.zeros_like(kcar_ref)
            vcar_ref[...] = jnp.zeros_like(vcar_ref)

        for gi, dil in enumerate(DILATIONS):
            for idx in range(SUPER // ATT_BLOCK):
                r, b = idx % dil, idx // dil
                qs = b * (ATT_BLOCK * dil) + r
                first = (n == 0).astype(jnp.int32) if b == 0 else 0
                q2 = _stack_heads(q_ref[pl.ds(qs, ATT_BLOCK, stride=dil), :])
                kb, vb = _strided_keys(dil, r, b, kc_ref, kp_ref, vc_ref, vp_ref, kcar_ref, vcar_ref)
                s = _dot_nt(q2, kb) + bias_ref[2 * gi + first]
                m = jnp.max(s, axis=-1, keepdims=True)
                p = jnp.exp(s - m).astype(BF16)
                ol = _dot(p, jnp.concatenate([vb, jnp.ones_like(vb)], axis=1))
                l = ol[:, LANES:]
                o2 = ol[:, :LANES] / l
                lse2 = m + jnp.log(l)
                og_ref[gi, pl.ds(qs, ATT_BLOCK, stride=dil), :] = _unstack_heads(o2)
                lg_ref[gi, pl.ds(qs, ATT_BLOCK, stride=dil), :] = _unstack_heads(lse2)
        lg = [lg_ref[gi] for gi in range(nd)]
        top = functools.reduce(jnp.maximum, lg)
        ws = [jnp.exp(x - top) for x in lg]
        tot = functools.reduce(jnp.add, ws)
        lse_ref[...] = top + jnp.log(tot)
        acc = ws[0] * og_ref[0]
        for gi in range(1, nd):
            acc = acc + ws[gi] * og_ref[gi]
        o_ref[...] = (acc / tot).astype(BF16)

    cur = lambda off: pl.BlockSpec((None, SUPER, LANES), lambda h, n: (h + off, n, 0))
    prev = lambda off: pl.BlockSpec((None, SUPER, LANES), lambda h, n: (h + off, jnp.maximum(n - 1, 0), 0))
    return _call(
        body, name=name, grid=(hp, ns),
        in_specs=[pl.BlockSpec((None, 2, 2 * ATT_BLOCK), lambda h, n: (h, 0, 0)), cur(0), cur(0), prev(0), cur(hp), prev(hp)],
        out_specs=[pl.BlockSpec((SUPER, LANES), lambda h, n: (n, h)), cur(0)],
        out_shape=[_sds((t, hp * LANES), BF16), _sds((hp, t, LANES), F32)],
        scratch_shapes=[pltpu.VMEM((nd, SUPER, LANES), F32), pltpu.VMEM((nd, SUPER, LANES), F32),
                        pltpu.VMEM((2 * nd, 2 * ATT_BLOCK, 2 * ATT_BLOCK), F32)] + [
                            pltpu.VMEM((DILATIONS[-1], ATT_BLOCK, LANES), BF16)] * 2,
    )(slopes, q, kv, kv, kv, kv)


def _attention_bwd(q, kv, o, lse, d_o, dk_in, dv_in, slopes, q_scale, name):
    hp, t, _ = q.shape
    ns = t // SUPER
    shared = dk_in is not None

    def body(sl_ref, q_ref, kc_ref, kp_ref, vc_ref, vp_ref, o_ref, lse_ref, do_ref, *rest):
        dki_ref, dvi_ref = rest[:2] if shared else (None, None)
        dq_ref, dk_ref, dv_ref, dkw_ref, dvw_ref, st_ref, bias_ref, kcar_ref, vcar_ref = rest[2 if shared else 0:]
        n = pl.program_id(1)

        @pl.when(n == 0)
        def _():
            dkw_ref[...] = jnp.zeros_like(dkw_ref)
            dvw_ref[...] = jnp.zeros_like(dvw_ref)

        @pl.when(n > 0)
        def _():
            dkw_ref[0:SUPER, :] = dkw_ref[SUPER:, :]
            dvw_ref[0:SUPER, :] = dvw_ref[SUPER:, :]
            dkw_ref[SUPER:, :] = jnp.zeros((SUPER, LANES), F32)
            dvw_ref[SUPER:, :] = jnp.zeros((SUPER, LANES), F32)

        @pl.when(n < ns)
        def _():
            prod = do_ref[...] * o_ref[...].astype(F32)
            lane = lax.broadcasted_iota(jnp.int32, prod.shape, 1)
            zero = jnp.zeros((SUPER, LANES), F32)
            st_ref[0] = zero + jnp.sum(jnp.where(lane < HEAD_DIM, prod, 0.0), axis=-1, keepdims=True)
            st_ref[1] = zero + jnp.sum(jnp.where(lane >= HEAD_DIM, prod, 0.0), axis=-1, keepdims=True)
            lse = lse_ref[...]
            swapped = pltpu.roll(lse, HEAD_DIM, 1)
            st_ref[2] = jnp.where(lane < HEAD_DIM, lse, swapped)
            st_ref[3] = jnp.where(lane >= HEAD_DIM, lse, swapped)
            dq_ref[...] = jnp.zeros_like(dq_ref)

            @pl.when(n == 0)
            def _():
                _fill_bias_t(sl_ref, bias_ref)
                kcar_ref[...] = jnp.zeros_like(kcar_ref)
                vcar_ref[...] = jnp.zeros_like(vcar_ref)

            for gi, dil in enumerate(DILATIONS):
                for idx in range(SUPER // ATT_BLOCK):
                    r, b = idx % dil, idx // dil
                    qs = b * (ATT_BLOCK * dil) + r
                    ks = SUPER + (b - 1) * (ATT_BLOCK * dil) + r
                    first = (n == 0).astype(jnp.int32) if b == 0 else 0
                    rows = pl.ds(qs, ATT_BLOCK, stride=dil)
                    keys = pl.ds(ks, 2 * ATT_BLOCK, stride=dil)
                    q2 = _stack_heads(q_ref[rows, :])
                    do2 = _stack_heads(do_ref[rows, :])
                    kb, vb = _strided_keys(dil, r, b, kc_ref, kp_ref, vc_ref, vp_ref, kcar_ref, vcar_ref)
                    dd = _rows_to_lanes(st_ref[0, rows, :], st_ref[1, rows, :])
                    lse_b = _rows_to_lanes(st_ref[2, rows, :], st_ref[3, rows, :])
                    ps, dss = [], []
                    for half in range(2):
                        hk = slice(half * ATT_BLOCK, (half + 1) * ATT_BLOCK)
                        p = jnp.exp(_dot_nt(kb[hk], q2) + bias_ref[2 * gi + first, hk, :] - lse_b)
                        dss.append((p * (_dot_nt(vb[hk], do2) - dd)).astype(BF16))
                        ps.append(p.astype(BF16))
                    p, ds = jnp.concatenate(ps, axis=0), jnp.concatenate(dss, axis=0)
                    dvw_ref[keys, :] += _dot(p, do2)
                    dkw_ref[keys, :] += _dot(ds, q2)
                    dq_ref[rows, :] += _unstack_heads(_dot_tn(ds, kb)) * q_scale

        dk_ref[...] =dkw_ref[0:SUPER, :] + dki_ref[...] if shared else dkw_ref[0:SUPER, :]
        dv_ref[...] = dvw_ref[0:SUPER, :] + dvi_ref[...] if shared else dvw_ref[0:SUPER, :]

    last = ns - 1
    cur = lambda off: pl.BlockSpec((None, SUPER, LANES), lambda h, n: (h + off, jnp.minimum(n, last), 0))
    prev = lambda off: pl.BlockSpec((None, SUPER, LANES), lambda h, n: (h + off, jnp.clip(n - 1, 0, last), 0))
    nat = pl.BlockSpec((SUPER, LANES), lambda h, n: (jnp.minimum(n, last), h))
    late = pl.BlockSpec((None, SUPER, LANES), lambda h, n: (h, jnp.maximum(n - 1, 0), 0))
    dq, dk, dv = _call(
        body, name=name, grid=(hp, ns + 1),
        in_specs=[pl.BlockSpec((None, 2, 2 * ATT_BLOCK), lambda h, n: (h, 0, 0)), cur(0), cur(0), prev(0), cur(hp), prev(hp),
                  nat, cur(0), nat] + ([late, late] if shared else []),
        out_specs=[cur(0), late, late],
        out_shape=[_sds((hp, t, LANES), F32)] * 3,
        scratch_shapes=[pltpu.VMEM((2 * SUPER, LANES), F32)] * 2 + [
            pltpu.VMEM((4, SUPER, LANES), F32), pltpu.VMEM((2 * len(DILATIONS), 2 * ATT_BLOCK, 2 * ATT_BLOCK), F32)] + [
                pltpu.VMEM((DILATIONS[-1], ATT_BLOCK, LANES), BF16)] * 2,
    )(slopes, q, kv, kv, kv, kv, o, lse, d_o, *((dk_in, dv_in) if shared else ()))
    return dq, dk, dv


def _loss_head(y, target, raw, g, name):
    t, d = y.shape
    tm = ROW_TILE

    def body(y_ref, t_ref, raw_ref, g_ref, sq_ref, dy_ref, draw_ref, dg_ref):
        i = pl.program_id(0)
        err = y_ref[...] - t_ref[...]
        dy = err * (1.0 / d)
        dy_ref[...] = dy
        draw, dg = _rms_bwd(raw_ref[...], g_ref[...], dy)
        draw_ref[...] = draw.astype(BF16)
        sq = jnp.zeros((8, LANES), F32) + jnp.sum(err * err)

        @pl.when(i == 0)
        def _():
            sq_ref[...] = sq
            dg_ref[...] = dg

        @pl.when(i > 0)
        def _():
            sq_ref[...] += sq
            dg_ref[...] += dg

    row = pl.BlockSpec((tm, d), lambda i: (i, 0))
    vec = pl.BlockSpec((1, d), lambda i: (0, 0))
    return _call(
        body, name=name, grid=(t // tm,), in_specs=[row, row, row, vec],
        out_specs=[pl.BlockSpec((8, LANES), lambda i: (0, 0)), row, row, vec],
        out_shape=[_sds((8, LANES), F32), _sds((t, d), F32), _sds((t, d), BF16), _sds((1, d), F32)])(y, target, raw, g)


def _bwd_matmul_norms(a_specs, a_args, a_tile, n_steps, w_spec, w_arg, w_mat, xa, ga, resid, xb, gb, name,
                      w_transposed=False):
    t, d = xa.shape
    tm = BWD_ROW_TILE
    na = len(a_specs)
    second = xb is not None
    per = 8 if n_steps % 8 == 0 else 1
    n_steps //= per

    def blocks_of(spec, k):
        return pl.BlockSpec(spec.block_shape, lambda i, j: spec.index_map(i, per * j + k))

    def body(*refs):
        a_refs, w_refs = refs[:per * na], refs[per * na:per * na + per]
        xa_ref, ga_ref, res_ref = refs[per * na + per:per * na + per + 3]
        rest = refs[per * na + per + 3:]
        if second:
            xb_ref, gb_ref, dx_ref, d2_ref, dga_ref, dgb_ref, acc_ref = rest
        else:
            dx_ref, dga_ref, acc_ref = rest
        i, j = pl.program_id(0), pl.program_id(1)
        part = None
        for k in range(per):
            term = (_dot if w_transposed else _dot_nt)(a_tile(per * j + k, *a_refs[k * na:(k + 1) * na]), w_mat(w_refs[k]))
            part = term if part is None else part + term

        @pl.when(j == 0)
        def _():
            acc_ref[...] = part

        @pl.when(j > 0)
        def _():
            acc_ref[...] += part

        @pl.when(j == n_steps - 1)
        def _():
            da, dga = _rms_bwd(xa_ref[...], ga_ref[...], acc_ref[...])
            dx = res_ref[...] + da
            dx_ref[...] = dx
            if second:
                d2, dgb = _rms_bwd(xb_ref[...], gb_ref[...], dx)
                d2_ref[...] = d2.astype(BF16)

            @pl.when(i == 0)
            def _():
                dga_ref[...] = dga
                if second:
                    dgb_ref[...] = dgb

            @pl.when(i > 0)
            def _():
                dga_ref[...] += dga
                if second:
                    dgb_ref[...] += dgb

    row = pl.BlockSpec((tm, d), lambda i, j: (i, 0))
    vec = pl.BlockSpec((1, d), lambda i, j: (0, 0))
    in_specs = [blocks_of(sp, k) for k in range(per) for sp in a_specs] + [blocks_of(w_spec, k) for k in range(per)]
    in_specs += [row, vec, row]
    args = list(a_args) * per + [w_arg] * per + [xa, ga, resid]
    if second:
        in_specs += [row, vec]
        args += [xb, gb]
        out_specs = [row, row, vec, vec]
        out_shape = [_sds((t, d), F32), _sds((t, d), BF16), _sds((1, d), F32), _sds((1, d), F32)]
    else:
        out_specs = [row, vec]
        out_shape = [_sds((t, d), F32), _sds((1, d), F32)]
    return _call(body, name=name, grid=(t // tm, n_steps), in_specs=in_specs, out_specs=out_specs,
                 out_shape=out_shape, scratch_shapes=[pltpu.VMEM((tm, d), F32)])(*args)


def _heads_to_rows(*refs):
    hp = refs[0].shape[0]
    cols = []
    for p in range(hp):
        v = refs[0][p]
        for r in refs[1:]:
            v = v + r[p]
        cols.append(v)
    return jnp.concatenate(cols, axis=-1).astype(BF16)


def _matmul_nt_rows(a, wg, layer, out_dtype, name):
    t, d = a.shape
    tm = ROW_TILE

    def body(a_ref, w_ref, o_ref):
        o_ref[...] = _dot_nt(a_ref[...], w_ref[...].reshape(d, d)).astype(out_dtype)

    row = pl.BlockSpec((tm, d), lambda i: (i, 0))
    return _call(body, name=name, grid=(t // tm,),
                 in_specs=[row, pl.BlockSpec((N_DEV, None, d // N_DEV, d), lambda i: (0, layer, 0, 0))],
                 out_specs=row, out_shape=_sds((t, d), out_dtype))(a, wg)


def _swiglu_bwd(d_ff, wg, layer, gate, up, name):
    t, d = d_ff.shape
    fc = gate.shape[-1]
    rows = wg.shape[2]
    tm = BIG_ROW_TILE

    def body(df_ref, w_ref, g_ref, u_ref, dh_ref):
        w = w_ref[...].reshape(2 * rows, d)
        for r0 in range(0, tm, SWIGLU_ROWS):
            rs = slice(r0, r0 + SWIGLU_ROWS)
            da = _dot_nt(df_ref[rs, :], w)
            gate, up = g_ref[rs, :].astype(F32), u_ref[rs, :].astype(F32)
            sig = jax.nn.sigmoid(gate)
            dh_ref[0, rs, :] = (da * up * (sig * (1.0 + gate * (1.0 - sig)))).astype(BF16)
            dh_ref[1, rs, :] = (da * (gate * sig)).astype(BF16)

    return _call(
        body, name=name, grid=(t // tm, 4),
        in_specs=[pl.BlockSpec((tm, d), lambda i, c: (i, 0)),
                  pl.BlockSpec((2, None, rows, d), lambda i, c: (c, layer, 0, 0)),
                  pl.BlockSpec((None, tm, fc), lambda i, c: (c, i, 0)),
                  pl.BlockSpec((None, tm, fc), lambda i, c: (c, i, 0))],
        out_specs=pl.BlockSpec((None, 2, tm, fc), lambda i, c: (c, 0, i, 0)),
        out_shape=_sds((4, 2, t, fc), BF16))(d_ff, wg, gate, up)


def _conv_bwd(p, d_z, cw, name):
    t, d3 = p.shape
    d = d3 // 3
    tm = ROW_TILE
    hb = tm // HALO
    nt = t // tm

    def body(p_ref, prev_ref, next_ref, dz_ref, dzn_ref, cw_ref, dp_ref, dcw_ref):
        i = pl.program_id(0)
        b = p_ref[:, 0:d].astype(F32)
        c = p_ref[:, d:2 * d].astype(F32)
        h = p_ref[:, 2 * d:3 * d].astype(F32)
        u = c * h
        hu = prev_ref[:, d:2 * d].astype(F32) * prev_ref[:, 2 * d:3 * d].astype(F32) * (i > 0).astype(F32)
        u1, u2 = _shift_down(u, hu, 1, tm), _shift_down(u, hu, 2, tm)
        uc = cw_ref[2:3, :] * u + cw_ref[1:2, :] * u1 + cw_ref[0:1, :] * u2
        dz = dz_ref[...].astype(F32)
        duc = dz * b
        dn = dzn_ref[...].astype(F32) * next_ref[:, 0:d].astype(F32) * (i < nt - 1).astype(F32)
        du = cw_ref[2:3, :] * duc + cw_ref[1:2, :] * _shift_up(duc, dn, 1, tm) + cw_ref[0:1, :] * _shift_up(duc, dn, 2, tm)
        dp_ref[:, 0:d] = (dz * uc).astype(BF16)
        dp_ref[:, d:2 * d] = (du * h).astype(BF16)
        dp_ref[:, 2 * d:3 * d] = (du * c).astype(BF16)
        dcw = jnp.concatenate([jnp.sum(duc * u2, axis=0, keepdims=True), jnp.sum(duc * u1, axis=0, keepdims=True),
                               jnp.sum(duc * u, axis=0, keepdims=True)], axis=0)

        @pl.when(i == 0)
        def _():
            dcw_ref[...] = dcw

        @pl.when(i > 0)
        def _():
            dcw_ref[...] += dcw

    last_halo = t // HALO - 1
    return _call(
        body, name=name, grid=(nt,),
        in_specs=[pl.BlockSpec((tm, d3), lambda i: (i, 0)),
                  pl.BlockSpec((HALO, d3), lambda i: (jnp.maximum(i * hb - 1, 0), 0)),
                  pl.BlockSpec((HALO, d3), lambda i: (jnp.minimum((i + 1) * hb, last_halo), 0)),
                  pl.BlockSpec((tm, d), lambda i: (i, 0)),
                  pl.BlockSpec((HALO, d), lambda i: (jnp.minimum((i + 1) * hb, last_halo), 0)),
                  pl.BlockSpec((3, d), lambda i: (0, 0))],
        out_specs=[pl.BlockSpec((tm, d3), lambda i: (i, 0)), pl.BlockSpec((3, d), lambda i: (0, 0))],
        out_shape=[_sds((t, d3), BF16), _sds((3, d), F32)])(p, p, p, d_z, d_z, cw)


def _grad_weight(a_specs, a_args, a_tile, b_specs, b_args, b_tile, n_out, acc_shape, out_spec, out_shape, t, name,
                 a_transposed=False):
    tt = GRAD_ROW_TILE
    na, nb = len(a_specs), len(b_specs)

    def body(*refs):
        a_refs, b_refs = refs[:na], refs[na:na + nb]
        o_ref, acc_ref = refs[na + nb:]
        s = pl.program_id(1)
        a, b = a_tile(pl.program_id(0), *a_refs), b_tile(pl.program_id(0), *b_refs)
        part = _dot(a, b) if a_transposed else _dot_tn(a, b)

        @pl.when(s == 0)
        def _():
            acc_ref[...] = part

        @pl.when(s > 0)
        def _():
            acc_ref[...] += part

        @pl.when(s == t // tt - 1)
        def _():
            acc = acc_ref[...].astype(BF16)
            if o_ref.shape[-1] == acc.shape[-1]:
                o_ref[...] = acc.reshape(o_ref.shape)
            else:
                for k in range(o_ref.shape[0]):
                    o_ref[k] = acc[:, k * o_ref.shape[-1]:(k + 1) * o_ref.shape[-1]]

    return _call(body, name=name, grid=(n_out, t // tt), in_specs=list(a_specs) + list(b_specs), out_specs=out_spec,
                 out_shape=out_shape, scratch_shapes=[pltpu.VMEM(acc_shape, F32)])(*a_args, *b_args)


def _ident(*args):
    return args[-1][...]


def _heads_tile(j, *refs):
    return _heads_to_rows(*refs)


def kernel(x, norm_g, conv_in_w, conv_w, conv_out_w, kv_norm_g, kv_w, q_w, o_w, ffn_in_w, ffn_out_w, loss_target, m_norm_g, m_conv_in_w, m_conv_w, m_conv_out_w, m_kv_norm_g, m_kv_w, m_q_w, m_o_w, m_ffn_in_w, m_ffn_out_w, v_norm_g, v_conv_in_w, v_conv_w, v_conv_out_w, v_kv_norm_g, v_kv_w, v_q_w, v_o_w, v_ffn_in_w, v_ffn_out_w):
    x0 = x[0]
    target = loss_target[0]
    t, d = x0.shape
    depth = norm_g.shape[0]
    n_a = conv_in_w.shape[0]
    n_b = q_w.shape[0]
    hp = d // LANES
    tm, tg = BWD_ROW_TILE, GRAD_ROW_TILE
    assert t % SUPER == 0 and d % LANES == 0 and depth == n_a + n_b
    dev = 4 * lax.axis_index("x") + 2 * lax.axis_index("y") + lax.axis_index("c")

    n_small = 4 * depth + 3 * n_a
    small_rows = -(-(n_small + 1) // 8) * 8
    small_local = jnp.concatenate([norm_g.reshape(4 * depth, -1), conv_w.reshape(3 * n_a, -1),
                                   jnp.zeros((small_rows - n_small, norm_g.shape[-1]), F32)], axis=0)
    swap = lambda a: jnp.swapaxes(a, 1, 2)
    big = {"conv_in_w": conv_in_w, "conv_out_w": conv_out_w, "kv_w": kv_w[None], "q_w": q_w, "o_w": o_w,
           "ffn_in_w": swap(ffn_in_w), "ffn_out_w": ffn_out_w}
    names = list(big)

    def group(layer):
        if layer < n_a:
            return [("conv_in_w", layer), ("conv_out_w", layer), ("ffn_in_w", layer), ("ffn_out_w", layer)]
        j = layer - n_a
        return ([("kv_w", 0)] if j == 0 else []) + [("q_w", j), ("o_w", j), ("ffn_in_w", layer), ("ffn_out_w", layer)]

    slot = dev.astype(jnp.int32).reshape(1)
    is_ffn = lambda key: key[0].startswith("ffn")
    first_keys = [key for key in group(0) if not is_ffn(key)]
    first = _all_gather([small_local] + [_cast_layer(big[k], i, None, f"cast_{k}_{i}") for k, i in first_keys], "gather_weights")
    small_all = first[0].transpose(1, 0, 2).reshape(small_rows, d)
    wl = {key: a[:, None] for key, a in zip(first_keys, first[1:])}

    def gather_start(keys, after, tag):
        lands = [_cast_layer(big[k], i, slot, f"cast_{k}_{i}") for k, i in keys]
        send_sems, recv_sems, _, lands, tok = _send_start([], lands, after, "gather_start" + tag)
        return (keys, tag, send_sems, recv_sems, lands), tok[0, 0]

    def gather_wait(flight, after):
        keys, tag, send_sems, recv_sems, lands = flight
        _, lands = _send_wait(send_sems, recv_sems, [], lands, after, "gather_wait" + tag)
        wl.update({key: a[:, None] for key, a in zip(keys, lands)})

    in_flight, token = gather_start([key for key in group(0) if is_ffn(key)], small_all, "_l0")
    W = lambda k, i: (wl[(k, i)], 0)
    gain = lambda layer, k: small_all[4 * layer + k][None]
    taps = lambda layer: small_all[4 * depth + 3 * layer: 4 * depth + 3 * layer + 3]
    g_kv = kv_norm_g[None]
    slopes = _alibi_slopes(d // HEAD_DIM)
    fc = big["ffn_in_w"].shape[-2]
    cb = big["conv_in_w"].shape[-1]
    kvb = big["kv_w"].shape[-1]
    q_scale = HEAD_DIM ** -0.5

    saved = []
    kv = kvn_t = None
    xs = x0
    for layer in range(depth):
        tag = f"_l{layer}"
        g0 = g2 = 0.0
        if layer == 0:
            g0 = token
        else:
            gather_wait(in_flight, xs)
            if layer + 1 < depth:
                in_flight, g0 = gather_start(group(layer + 1), xs, f"_l{layer + 1}")
        s = {"x_in": xs}
        g0 = gain(layer, 0) + g0
        if layer < n_a:
            s["p"], s["xn_t"] = _norm_matmul_cols(xs, g0, *W("conv_in_w", layer), "cols", "conv_in" + tag)
            s["z"] = _conv_fwd(s["p"], taps(layer), "conv" + tag)
            s["mix"], x_mid = _matmul_norm_residual(s["z"][None], *W("conv_out_w", layer), gain(layer, 1), xs, "conv_out" + tag)
        else:
            j = layer - n_a
            if kv is None:
                kv, kvn_t = _norm_matmul_cols(xs, g_kv, *W("kv_w", 0), "heads", "kv_proj")
            s["q"], s["xn"] = _norm_matmul_heads(xs, g0, *W("q_w", j), q_scale, "q_proj" + tag)
            s["o"], s["lse"] = _attention_fwd(s["q"], kv, slopes, "attention" + tag)
            s["mix"], x_mid = _matmul_norm_residual(s["o"][None], *W("o_w", j), gain(layer, 1), xs, "o_proj" + tag)
        s["x_mid"] = x_mid
        if layer == 0:
            gather_wait(in_flight, x_mid)
            in_flight, g2 = gather_start(group(1), x_mid, "_l1")
        s["gate"], s["up"], s["a"], s["fn"] = _ffn_in_swiglu(x_mid, gain(layer, 2) + g2, *W("ffn_in_w", layer), "ffn_in" + tag)
        s["ff"], xs = _matmul_norm_residual(s["a"], *W("ffn_out_w", layer), gain(layer, 3), x_mid, "ffn_out" + tag)
        saved.append(s)

    last = saved[-1]
    sq, dx_out, d_ff, dg3 = _loss_head(xs, target, last["ff"], gain(depth - 1, 3), "loss_head")
    loss = lax.psum(sq[0, 0] * (0.5 / d), ("x", "y", "c"))

    dgain = {(depth - 1, 3): dg3}
    dtaps = {}
    grads = {k: [None] * big[k].shape[0] for k in names}
    dkv_parts = []
    scattering = []

    def scatter_start(keys, tag):
        parts = [grads[k][i] for k, i in keys]
        zones = [lax.empty(p.shape, p.dtype) for p in parts]
        send_sems, recv_sems, parts, zones, tok = _send_start(parts, zones, small_all, "scatter_start" + tag)
        scattering.append((keys, tag, send_sems, recv_sems, parts, zones))
        return tok[0, 0]

    for layer in reversed(range(depth)):
        tag = f"_l{layer}"
        s = saved[layer]
        dh = _swiglu_bwd(d_ff, *W("ffn_out_w", layer), s["gate"], s["up"], "swiglu_bwd" + tag)
        rows_out = big["ffn_out_w"].shape[1]
        grads["ffn_out_w"][layer] = _grad_weight(
            [pl.BlockSpec((None, tg, fc), lambda c, i: (c, i, 0))], [s["a"]], _ident,
            [pl.BlockSpec((tg, d), lambda c, i: (i, 0))], [d_ff], _ident,
            4, (fc, d), pl.BlockSpec((2, rows_out, d), lambda c, i: (c, 0, 0)), _sds((N_DEV, rows_out, d), BF16), t,
            "grad_ffn_out" + tag)
        grads["ffn_in_w"][layer] = _grad_weight(
            [pl.BlockSpec((None, None, tg, fc), lambda j, i: (j % 4, j // 4, i, 0))], [dh], _ident,
            [pl.BlockSpec((tg, d), lambda j, i: (i, 0))], [s["fn"]], _ident,
            N_DEV, (fc, d), pl.BlockSpec((None, fc, d), lambda j, i: (j, 0, 0)), _sds((N_DEV, fc, d), BF16), t,
            "grad_ffn_in" + tag)
        tok = scatter_start([("ffn_in_w", layer), ("ffn_out_w", layer)], "_ffn" + tag)
        dx_mid, d_mix, dg2, dg1 = _bwd_matmul_norms(
            [pl.BlockSpec((None, None, tm, fc), lambda i, j: (j % 4, j // 4, i, 0))], [dh], _ident, N_DEV,
            pl.BlockSpec((None, None, fc, d), lambda i, j: (j, 0, 0, 0)), W("ffn_in_w", layer)[0], _ident,
            s["x_mid"], gain(layer, 2) + tok, dx_out, s["mix"], gain(layer, 1), "ffn_in_bwd" + tag, w_transposed=True)
        dgain[(layer, 2)], dgain[(layer, 1)] = dg2, dg1
        full_rows = pl.BlockSpec((N_DEV, d // N_DEV, d), lambda j, i: (0, 0, 0))
        rows_w = lambda wname, idx: (pl.BlockSpec((N_DEV, None, d // N_DEV, d), lambda i, j: (0, 0, 0, 0)), W(wname, idx)[0],
                                     lambda w_ref: w_ref[...].reshape(d, d))
        if layer < n_a:
            d_z = _matmul_nt_rows(d_mix, *W("conv_out_w", layer), BF16, "conv_out_bwd" + tag)
            grads["conv_out_w"][layer] = _grad_weight(
                [pl.BlockSpec((tg, d), lambda j, i: (i, 0))], [s["z"]], _ident,
                [pl.BlockSpec((tg, d), lambda j, i: (i, 0))], [d_mix], _ident,
                1, (d, d), full_rows, _sds((N_DEV, d // N_DEV, d), BF16), t, "grad_conv_out" + tag)
            d_p, dtaps[layer] = _conv_bwd(s["p"], d_z, taps(layer), "conv_bwd" + tag)
            grads["conv_in_w"][layer] = _grad_weight(
                [pl.BlockSpec((d, tg), lambda j, i: (0, i))], [s["xn_t"]], _ident,
                [pl.BlockSpec((tg, 2 * cb), lambda j, i: (i, j))], [d_p], _ident,
                N_DEV // 2, (d, 2 * cb), pl.BlockSpec((2, d, cb), lambda j, i: (j, 0, 0)), _sds((N_DEV, d, cb), BF16), t,
                "grad_conv_in" + tag, a_transposed=True)
            a_specs, a_args, a_tile, n_steps = [pl.BlockSpec((tm, N_DEV * cb), lambda i, j: (i, 0))], [d_p], _ident, 1
            w_spec = pl.BlockSpec((N_DEV, None, d, cb), lambda i, j: (0, 0, 0, 0))
            w_arg = W("conv_in_w", layer)[0]
            w_mat = lambda w_ref: jnp.concatenate([w_ref[k] for k in range(N_DEV)], axis=1)
            resid = dx_mid
        else:
            j_b = layer - n_a
            d_o = _matmul_nt_rows(d_mix, *W("o_w", j_b), F32, "o_proj_bwd" + tag)
            grads["o_w"][j_b] = _grad_weight(
                [pl.BlockSpec((tg, d), lambda j, i: (i, 0))], [s["o"]], _ident,
                [pl.BlockSpec((tg, d), lambda j, i: (i, 0))], [d_mix], _ident,
                1, (d, d), full_rows, _sds((N_DEV, d // N_DEV, d), BF16), t, "grad_o" + tag)
            dk_in, dv_in = dkv_parts[0] if dkv_parts else (None, None)
            dq, dk, dv = _attention_bwd(s["q"], kv, s["o"], s["lse"], d_o, dk_in, dv_in, slopes, q_scale, "attention_bwd" + tag)
            dkv_parts = [(dk, dv)]
            heads_spec = pl.BlockSpec((hp, tg, LANES), lambda j, i: (0, i, 0))
            grads["q_w"][j_b] = _grad_weight(
                [pl.BlockSpec((tg, d), lambda j, i: (i, 0))], [s["xn"]], _ident,
                [heads_spec], [dq], _heads_tile,
                1, (d, d), full_rows, _sds((N_DEV, d // N_DEV, d), BF16), t, "grad_q" + tag)
            a_specs, a_args, a_tile, n_steps = [pl.BlockSpec((hp, tm, LANES), lambda i, j: (0, i, 0))], [dq], _heads_tile, 1
            w_spec, w_arg, w_mat = rows_w("q_w", j_b)
            resid = dx_mid
            if layer == n_a:
                pieces = kvb // LANES
                halves = []
                for src in (0, 1):
                    halves.append([part[src] for part in dkv_parts])
                n_half = len(dkv_parts)
                kv_args = [arr for src in (0, 1) for arr in halves[src]]

                def kv_block(src, j):
                    return jnp.where((j // 4) == src, j % 4, 0)

                def kv_tile(j, *refs):
                    keys = _heads_to_rows(*refs[:n_half])
                    vals = _heads_to_rows(*refs[n_half:])
                    return jnp.where(j < 4, keys, vals)

                kv_specs = [pl.BlockSpec((pieces, tm, LANES), functools.partial(lambda i, j, src: (kv_block(src, j), i, 0), src=src))
                            for src in (0, 1) for _ in range(n_half)]
                resid, dgain["kv"] = _bwd_matmul_norms(
                    kv_specs, kv_args, kv_tile, N_DEV,
                    pl.BlockSpec((None, None, d, kvb), lambda i, j: (j, 0, 0, 0)), W("kv_w", 0)[0], _ident,
                    s["x_in"], g_kv, dx_mid, None, None, "kv_proj_bwd")
                kv_b_specs = [pl.BlockSpec((pieces, tg, LANES), functools.partial(lambda j, i, src: (kv_block(src, j), i, 0), src=src))
                              for src in (0, 1) for _ in range(n_half)]
                grads["kv_w"][0] = _grad_weight(
                    [pl.BlockSpec((d, tg), lambda j, i: (0, i))], [kvn_t], _ident,
                    kv_b_specs, kv_args, kv_tile,
                    N_DEV, (d, kvb), pl.BlockSpec((None, d, kvb), lambda j, i: (j, 0, 0)), _sds((N_DEV, d, kvb), BF16), t,
                    "grad_kv", a_transposed=True)
        tok = scatter_start([key for key in group(layer) if not key[0].startswith("ffn")], "_mix" + tag)
        if layer > 0:
            prev = saved[layer - 1]
            dx_out, d_ff, dg0, dg3p = _bwd_matmul_norms(
                a_specs, a_args, a_tile, n_steps, w_spec, w_arg, w_mat,
                s["x_in"], gain(layer, 0) + tok, resid, prev["ff"], gain(layer - 1, 3), "mixer_in_bwd" + tag)
            dgain[(layer, 0)], dgain[(layer - 1, 3)] = dg0, dg3p
        else:
            grad_x, dg0 = _bwd_matmul_norms(
                a_specs, a_args, a_tile, n_steps, w_spec, w_arg, w_mat,
                s["x_in"], gain(layer, 0), resid, None, None, "mixer_in_bwd" + tag)
            dgain[(layer, 0)] = dg0

    small_grad = jnp.concatenate(
        [dgain[(layer, k)] for layer in range(depth) for k in range(4)] + [dtaps[layer] for layer in range(n_a)]
        + [dgain["kv"]] + [jnp.zeros((small_rows - n_small - 1, d), F32)], axis=0)
    small_grads_all = _all_gather([small_grad], "gather_small_grads")[0]
    lo = dev * (d // N_DEV)

    def pack(ng, cwp, kvg):
        rows = jnp.concatenate([ng.reshape(4 * depth, -1), cwp.reshape(3 * n_a, -1)], axis=0)
        z = lax.dynamic_update_slice(jnp.zeros((small_rows, d), F32), rows, (0, lo))
        return lax.dynamic_update_slice(z, kvg[None], (n_small, 0))

    w_small = lax.dynamic_update_slice(small_all, g_kv, (n_small, 0))
    m_small, v_small = pack(m_norm_g, m_conv_w, m_kv_norm_g), pack(v_norm_g, v_conv_w, v_kv_norm_g)
    sm = _small_adamw(small_grads_all, w_small, m_small, v_small, "adamw_small")

    def unpack(a):
        mine = lax.dynamic_slice(a, (0, lo), (small_rows, d // N_DEV))
        return (mine[:4 * depth].reshape(norm_g.shape), mine[4 * depth:n_small].reshape(conv_w.shape), a[n_small])

    small_out = [unpack(a) for a in sm]

    moments = {"conv_in_w": (m_conv_in_w, v_conv_in_w), "conv_out_w": (m_conv_out_w, v_conv_out_w),
               "kv_w": (m_kv_w[None], v_kv_w[None]), "q_w": (m_q_w, v_q_w), "o_w": (m_o_w, v_o_w),
               "ffn_in_w": (swap(m_ffn_in_w), swap(v_ffn_in_w)), "ffn_out_w": (m_ffn_out_w, v_ffn_out_w)}
    landed = {k: [None] * big[k].shape[0] for k in names}
    for keys, tag, send_sems, recv_sems, parts, zones in scattering:
        parts, zones = _send_wait(send_sems, recv_sems, parts, zones, grad_x, "scatter_wait" + tag)
        for (k, i), part, zone in zip(keys, parts, zones):
            landed[k][i] = (part, zone)
    res = {k: _sum_adamw(slot, [p for p, _ in landed[k]], [z for _, z in landed[k]], big[k], moments[k][0], moments[k][1],
                         "adamw_" + k) for k in names}

    def big_out(k, which):
        out = res[k][which]
        return out[0] if k == "kv_w" else swap(out) if k == "ffn_in_w" else out

    out_names = ["norm_g", "conv_in_w", "conv_w", "conv_out_w", "kv_norm_g", "kv_w", "q_w", "o_w", "ffn_in_w", "ffn_out_w"]
    small_pos = {"norm_g": 0, "conv_w": 1, "kv_norm_g": 2}
    outs = [loss, grad_x[None]]
    for which in range(4):
        for k in out_names:
            outs.append(small_out[which][small_pos[k]] if k in small_pos else big_out(k, which))
    return tuple(outs)
```

```python
import functools
import math

import numpy as np
import jax
import jax.numpy as jnp
from jax import lax
from jax.experimental import pallas as pl
from jax.experimental.pallas import tpu as pltpu

F32 = jnp.float32
BF16 = jnp.bfloat16

N_DEV = 8
RMS_EPS = 1e-6
HEAD_DIM = 64
LANES = 128
ATT_BLOCK = 128
DILATIONS = (1, 4, 16)
SUPER = ATT_BLOCK * DILATIONS[-1]
NEG = -1e30

ADAM_LR, ADAM_B1, ADAM_B2, ADAM_EPS, ADAM_WD, ADAM_STEP = 0.001, 0.9, 0.999, 1e-08, 0.01, 10

ROW_TILE = 512
BIG_ROW_TILE = 1024
SWIGLU_ROWS = 256
GRAD_ROW_TILE = 2048
BWD_ROW_TILE = 512
MESH = pl.DeviceIdType.MESH


def _call(body, *, name, grid=None, in_specs=None, out_specs=None, out_shape=None, scratch_shapes=(), prefetch=False,
          **params):
    cp = pltpu.CompilerParams(**params) if params else None
    if prefetch:
        spec = pltpu.PrefetchScalarGridSpec(num_scalar_prefetch=1, grid=grid, in_specs=in_specs, out_specs=out_specs,
                                            scratch_shapes=list(scratch_shapes))
        return pl.pallas_call(body, name=name, grid_spec=spec, out_shape=out_shape, compiler_params=cp)
    kwargs = {k: v for k, v in (("grid", grid), ("in_specs", in_specs), ("out_specs", out_specs)) if v is not None}
    return pl.pallas_call(body, name=name, out_shape=out_shape, scratch_shapes=list(scratch_shapes),
                          compiler_params=cp, **kwargs)


def _sds(shape, dtype):
    return jax.ShapeDtypeStruct(tuple(shape), dtype)


def _rms(x, g):
    r = lax.rsqrt(jnp.mean(x * x, axis=-1, keepdims=True) + RMS_EPS)
    return x * r * g


def _rms_bwd(x, g, dy):
    r = lax.rsqrt(jnp.mean(x * x, axis=-1, keepdims=True) + RMS_EPS)
    xh = x * r
    dxh = dy * g
    dx = r * (dxh - xh * jnp.mean(dxh * xh, axis=-1, keepdims=True))
    return dx, jnp.sum(dy * xh, axis=0, keepdims=True)


def _dot(a, b):
    return jnp.dot(a, b, preferred_element_type=F32)


def _dot_nt(a, b):
    return lax.dot_general(a, b, (((1,), (1,)), ((), ())), preferred_element_type=F32)


def _dot_tn(a, b):
    return lax.dot_general(a, b, (((0,), (0,)), ((), ())), preferred_element_type=F32)


def _mesh_pos():
    return lax.axis_index("x"), lax.axis_index("y"), lax.axis_index("c")


def _all_gather(arrs, name):
    n = len(arrs)

    def body(*refs):
        ins, outs = refs[:n], refs[n:2 * n]
        send_sems, recv_sems, local_sems = refs[2 * n:]
        x, y, c = _mesh_pos()
        me, sibling = (x, y, c), (x, y, 1 - c)
        chips = [(1 - x, y), (x, 1 - y), (1 - x, 1 - y)]

        def copy(a, k, block, to, src=None):
            dst = outs[a].at[4 * block[0] + 2 * block[1] + block[2]]
            return pltpu.make_async_remote_copy(
                src_ref=dst if src is None else src, dst_ref=dst, send_sem=send_sems.at[a, k],
                recv_sem=recv_sems.at[a, k], device_id=to, device_id_type=MESH)

        started = []
        for a in range(n):
            mine = pltpu.make_async_copy(ins[a], outs[a].at[4 * x + 2 * y + c], local_sems.at[a])
            mine.start()
            started.append(mine)
        first = []
        for a in range(n):
            first.append(copy(a, 0, me, sibling, src=ins[a]))
            first += [copy(a, 1 + j, me, (*chip, c), src=ins[a]) for j, chip in enumerate(chips)]
        for cp in first:
            cp.start()
        passed = []
        for a in range(n):
            for j, chip in enumerate(chips):
                copy(a, 1 + j, (*chip, c), me).wait_recv()
                fwd = copy(a, 4 + j, (*chip, c), sibling)
                fwd.start()
                passed.append(fwd)
        for a in range(n):
            copy(a, 0, sibling, me).wait_recv()
            for j, chip in enumerate(chips):
                copy(a, 4 + j, (*chip, 1 - c), me).wait_recv()
        for cp in first + passed:
            cp.wait_send()
        for cp in started:
            cp.wait()

    any_spec = pl.BlockSpec(memory_space=pl.ANY)
    outs = _call(
        body, name=name, in_specs=[any_spec] * n, out_specs=[any_spec] * n,
        out_shape=[_sds((N_DEV,) + a.shape, a.dtype) for a in arrs],
        scratch_shapes=[pltpu.SemaphoreType.DMA((n, 7)), pltpu.SemaphoreType.DMA((n, 7)), pltpu.SemaphoreType.DMA((n,))],
        has_side_effects=True,
    )(*arrs)
    return list(outs)


HBM_SPEC = pl.BlockSpec(memory_space=pltpu.HBM)
SEM_SPEC = pl.BlockSpec(memory_space=pltpu.SEMAPHORE)
DATAFLOW = pltpu.SideEffectType.DATAFLOW_SIDE_EFFECTING
PEERS = [(dx, dy, dc) for dx in (0, 1) for dy in (0, 1) for dc in (0, 1)][1:]


def _peer(flip):
    x, y, c = _mesh_pos()
    return tuple(1 - v if f else v for v, f in zip((x, y, c), flip))


def _slot(pos):
    return 4 * pos[0] + 2 * pos[1] + pos[2]


def _in_hbm(a):
    return pltpu.with_memory_space_constraint(a, pltpu.HBM)


NEAR = [(0, 0, 1), (1, 0, 0), (0, 1, 0), (1, 1, 0)]


def _direct_copies(srcs, lands, send_sems, recv_sems, mode):
    me = _slot(_mesh_pos())
    copies = []
    for a in range(len(lands)):
        if mode == "scatter":
            plan = [(srcs[a].at[_slot(_peer(flip))], lands[a].at[me], _peer(flip)) for flip in PEERS]
        elif mode == "gather_near":
            plan = [(lands[a].at[me], lands[a].at[me], _peer(flip)) for flip in NEAR]
        else:
            origins = [_slot(_peer(flip)) for flip in NEAR[1:]]
            plan = [(lands[a].at[o], lands[a].at[o], _peer(NEAR[0])) for o in origins]
        for k, (src, dst, to) in enumerate(plan):
            idx = a * len(PEERS) + k
            copies.append(pltpu.make_async_remote_copy(
                src_ref=src, dst_ref=dst, send_sem=send_sems.at[idx], recv_sem=recv_sems.at[idx],
                device_id=to, device_id_type=MESH))
    return copies


def _send_start(srcs, lands, after, name, mode):
    ns, nl = len(srcs), len(lands)

    def body(*refs):
        src_refs, land_refs = refs[:ns], refs[ns:ns + nl]
        send_sems, recv_sems = refs[ns + nl + 1:ns + nl + 3]
        token = refs[-1]
        for cp in _direct_copies(src_refs, land_refs, send_sems, recv_sems, mode):
            cp.start()
        token[...] = jnp.zeros_like(token)

    sem = pltpu.SemaphoreType.DMA((nl * len(PEERS),))
    outs = pl.pallas_call(
        body, name=name,
        out_shape=(sem, sem) + tuple(pltpu.HBM(a.shape, a.dtype) for a in list(srcs) + list(lands))
        + (_sds((8, LANES), F32),),
        in_specs=[HBM_SPEC] * (ns + nl) + [pl.BlockSpec(memory_space=pl.ANY)],
        out_specs=(SEM_SPEC, SEM_SPEC) + (HBM_SPEC,) * (ns + nl) + (pl.BlockSpec(memory_space=pltpu.VMEM),),
        input_output_aliases={i: 2 + i for i in range(ns + nl)},
        compiler_params=pltpu.CompilerParams(has_side_effects=DATAFLOW),
    )(*[_in_hbm(a) for a in list(srcs) + list(lands)], after)
    send_sems, recv_sems = outs[0], outs[1]
    return send_sems, recv_sems, list(outs[2:2 + ns]), list(outs[2 + ns:2 + ns + nl]), outs[-1]


def _send_wait(send_sems, recv_sems, srcs, lands, after, name, mode):
    ns, nl = len(srcs), len(lands)

    def body(*refs):
        src_refs, land_refs = refs[:ns], refs[ns:ns + nl]
        send_sems, recv_sems = refs[ns + nl:ns + nl + 2]
        copies = _direct_copies(src_refs, land_refs, send_sems, recv_sems, mode)
        for cp in copies:
            cp.wait_send()
        for cp in copies:
            cp.wait_recv()

    outs = pl.pallas_call(
        body, name=name,
        out_shape=tuple(pltpu.HBM(a.shape, a.dtype) for a in list(srcs) + list(lands)),
        in_specs=[HBM_SPEC] * (ns + nl) + [SEM_SPEC, SEM_SPEC, pl.BlockSpec(memory_space=pl.ANY)],
        out_specs=(HBM_SPEC,) * (ns + nl),
        input_output_aliases={i: i for i in range(ns + nl)},
        compiler_params=pltpu.CompilerParams(has_side_effects=DATAFLOW),
    )(*srcs, *lands, send_sems, recv_sems, after)
    return list(outs[:ns]), list(outs[ns:])


def _row_tile(rows, cap=512):
    t = min(rows, cap)
    while rows % t or (t % 16 and t != rows):
        t -= 1
    return t


def _as2d(a):
    return a.reshape(-1, a.shape[-1])


def _cast_layer(w, layer, slot, name):
    _, rows, cols = w.shape
    tr = _row_tile(rows)

    def body(*refs):
        refs[-1][...] = refs[-2][...].astype(BF16)

    if slot is None:
        return _call(body, name=name, grid=(rows // tr,),
                     in_specs=[pl.BlockSpec((None, tr, cols), lambda i: (layer, i, 0))],
                     out_specs=pl.BlockSpec((tr, cols), lambda i: (i, 0)), out_shape=_sds((rows, cols), BF16))(w)
    return _call(body, name=name, grid=(rows // tr,), prefetch=True,
                 in_specs=[pl.BlockSpec((None, tr, cols), lambda i, s: (layer, i, 0))],
                 out_specs=pl.BlockSpec((None, tr, cols), lambda i, s: (s[0], i, 0)),
                 out_shape=_sds((N_DEV, rows, cols), BF16))(slot, w)


def _sum_adamw(slot, parts, lands, w, m, v, name):
    n_l = len(parts)
    _, rows, cols = lands[0].shape
    tr = _row_tile(rows, 128)

    def body(s_ref, *refs):
        p_refs, l_refs = refs[:n_l], refs[n_l:2 * n_l]
        w_ref, m_ref, v_ref, g_ref, d_ref, nm_ref, nv_ref = refs[2 * n_l:]
        for k in range(n_l):
            @pl.when(pl.program_id(0) == k)
            def _(k=k):
                own = p_refs[k][...]
                g = jnp.zeros((tr, cols), F32)
                for j in range(N_DEV):
                    g = g + jnp.where(s_ref[0] == j, own, l_refs[k][j]).astype(F32)
                delta, nm, nv = _adamw_math(w_ref[...], g, m_ref[...], v_ref[...])
                g_ref[...] = g
                d_ref[...] = delta
                nm_ref[...] = nm
                nv_ref[...] = nv

    def own_block(k):
        return pl.BlockSpec((None, tr, cols), lambda l, i, s: (s[0], jnp.where(l == k, i, 0), 0))

    def zone_block(k):
        return pl.BlockSpec((N_DEV, tr, cols), lambda l, i, s: (0, jnp.where(l == k, i, 0), 0))

    lay = pl.BlockSpec((None, tr, cols), lambda l, i, s: (l, i, 0))
    return _call(body, name=name, grid=(n_l, rows // tr), prefetch=True,
                 in_specs=[own_block(k) for k in range(n_l)] + [zone_block(k) for k in range(n_l)] + [lay, lay, lay],
                 out_specs=[lay] * 4, out_shape=[_sds((n_l, rows, cols), F32)] * 4)(slot, *parts, *lands, w, m, v)


def _adamw_math(w, g, m, v):
    m = ADAM_B1 * m + (1.0 - ADAM_B1) * g
    v = ADAM_B2 * v + (1.0 - ADAM_B2) * (g * g)
    m_hat = m / (1.0 - ADAM_B1 ** ADAM_STEP)
    v_hat = v / (1.0 - ADAM_B2 ** ADAM_STEP)
    delta = -ADAM_LR * (m_hat / (jnp.sqrt(v_hat) + ADAM_EPS) + ADAM_WD * w)
    return delta, m, v


def _small_adamw(gathered, w, m, v, name):
    def body(a_ref, w_ref, m_ref, v_ref, g_ref, d_ref, nm_ref, nv_ref):
        g = a_ref[0]
        for k in range(1, N_DEV):
            g = g + a_ref[k]
        delta, nm, nv = _adamw_math(w_ref[...], g, m_ref[...], v_ref[...])
        g_ref[...] = g
        d_ref[...] = delta
        nm_ref[...] = nm
        nv_ref[...] = nv

    return _call(body, name=name, out_shape=[_sds(w.shape, F32)] * 4)(gathered, w, m, v)


def _norm_matmul_cols(x, g, wg, layer, mode, name):
    t, d = x.shape
    nb = wg.shape[-1]
    tm = BIG_ROW_TILE
    per = 2
    pieces = per * nb // LANES

    def body(x_ref, g_ref, w_ref, y_ref, xnt_ref):
        xn = _rms(x_ref[...], g_ref[...])
        xnt_ref[...] = xn.T.astype(BF16)
        xn = xn.astype(BF16)
        for j in range(N_DEV // per):
            y = _dot(xn, jnp.concatenate([w_ref[per * j + k] for k in range(per)], axis=1))
            if mode == "heads":
                for p in range(pieces):
                    y_ref[pieces * j + p] = y[:, p * LANES:(p + 1) * LANES]
            else:
                y_ref[:, j * per * nb:(j + 1) * per * nb] = y.astype(BF16)

    if mode == "cols":
        y_shape, y_spec = _sds((t, N_DEV * nb), BF16), pl.BlockSpec((tm, N_DEV * nb), lambda i: (i, 0))
    else:
        y_shape = _sds((N_DEV // per * pieces, t, LANES), F32)
        y_spec = pl.BlockSpec((N_DEV // per * pieces, tm, LANES), lambda i: (0, i, 0))
    return _call(
        body, name=name, grid=(t // tm,),
        in_specs=[pl.BlockSpec((tm, d), lambda i: (i, 0)), pl.BlockSpec((1, d), lambda i: (0, 0)),
                  pl.BlockSpec((N_DEV, None, d, nb), lambda i: (0, layer, 0, 0))],
        out_specs=[y_spec, pl.BlockSpec((d, tm), lambda i: (0, i))],
        out_shape=[y_shape, _sds((d, t), BF16)])(x, g, wg)


def _ffn_in_swiglu(x, g, wg, layer, name):
    t, d = x.shape
    fc = wg.shape[-2]
    tm = ROW_TILE

    def body(x_ref, g_ref, w_ref, gate_ref, up_ref, a_ref, xn_ref):
        xn = _rms(x_ref[...], g_ref[...]).astype(BF16)
        xn_ref[...] = xn
        for c in range(4):
            gate, up = _dot_nt(xn, w_ref[c]), _dot_nt(xn, w_ref[c + 4])
            gate_ref[c] = gate.astype(BF16)
            up_ref[c] = up.astype(BF16)
            a_ref[c] = (gate * jax.nn.sigmoid(gate) * up).astype(BF16)

    chunks = pl.BlockSpec((4, tm, fc), lambda i: (0, i, 0))
    return _call(
        body, name=name, grid=(t // tm,),
        in_specs=[pl.BlockSpec((tm, d), lambda i: (i, 0)), pl.BlockSpec((1, d), lambda i: (0, 0)),
                  pl.BlockSpec((N_DEV, None, fc, d), lambda i: (0, layer, 0, 0))],
        out_specs=[chunks, chunks, chunks, pl.BlockSpec((tm, d), lambda i: (i, 0))],
        out_shape=[_sds((4, t, fc), BF16)] * 3 + [_sds((t, d), BF16)])(x, g, wg)


def _norm_matmul_heads(x, g, wg, layer, scale, name):
    t, d = x.shape
    tm = ROW_TILE
    hp = d // LANES

    def body(x_ref, g_ref, w_ref, y_ref, xn_ref):
        xn = _rms(x_ref[...], g_ref[...]).astype(BF16)
        xn_ref[...] = xn
        y = _dot(xn, w_ref[...].reshape(d, d)) * scale
        for p in range(hp):
            y_ref[p] = y[:, p * LANES:(p + 1) * LANES]

    return _call(
        body, name=name, grid=(t // tm,),
        in_specs=[pl.BlockSpec((tm, d), lambda i: (i, 0)), pl.BlockSpec((1, d), lambda i: (0, 0)),
                  pl.BlockSpec((N_DEV, None, d // N_DEV, d), lambda i: (0, layer, 0, 0))],
        out_specs=[pl.BlockSpec((hp, tm, LANES), lambda i: (0, i, 0)), pl.BlockSpec((tm, d), lambda i: (i, 0))],
        out_shape=[_sds((hp, t, LANES), F32), _sds((t, d), BF16)])(x, g, wg)


def _shift_down(u, halo, k, tm):
    row = lax.broadcasted_iota(jnp.int32, u.shape, 0)
    out = pltpu.roll(u, k, 0)
    for j in range(k):
        out = jnp.where(row == j, halo[halo.shape[0] - k + j:halo.shape[0] - k + j + 1, :], out)
    return out


def _shift_up(u, halo, k, tm):
    row = lax.broadcasted_iota(jnp.int32, u.shape, 0)
    out = pltpu.roll(u, tm - k, 0)
    for j in range(k):
        out = jnp.where(row == tm - k + j, halo[j:j + 1, :], out)
    return out


HALO = 16


def _conv_fwd(p, cw, name):
    t, d3 = p.shape
    d = d3 // 3
    tm = ROW_TILE
    hb = tm // HALO

    def body(p_ref, prev_ref, cw_ref, z_ref):
        i = pl.program_id(0)
        b = p_ref[:, 0:d].astype(F32)
        u = p_ref[:, d:2 * d].astype(F32) * p_ref[:, 2 * d:3 * d].astype(F32)
        keep = (i > 0).astype(F32)
        hu = prev_ref[:, d:2 * d].astype(F32) * prev_ref[:, 2 * d:3 * d].astype(F32) * keep
        uc = cw_ref[2:3, :] * u + cw_ref[1:2, :] * _shift_down(u, hu, 1, tm) + cw_ref[0:1, :] * _shift_down(u, hu, 2, tm)
        z_ref[...] = (b * uc).astype(BF16)

    return _call(
        body, name=name, grid=(t // tm,),
        in_specs=[pl.BlockSpec((tm, d3), lambda i: (i, 0)),
                  pl.BlockSpec((HALO, d3), lambda i: (jnp.maximum(i * hb - 1, 0), 0)),
                  pl.BlockSpec((3, d), lambda i: (0, 0))],
        out_specs=pl.BlockSpec((tm, d), lambda i: (i, 0)), out_shape=_sds((t, d), BF16))(p, p, cw)


def _matmul_norm_residual(a3, wg, layer, g, x_res, name):
    kc_n, t, kc = a3.shape
    d = wg.shape[-1]
    per = N_DEV // kc_n
    rows = wg.shape[2]
    tm = ROW_TILE

    def body(a_ref, w_ref, g_ref, x_ref, raw_ref, xo_ref):
        raw = None
        for c in range(kc_n):
            term = _dot(a_ref[c], w_ref[c * per:(c + 1) * per].reshape(per * rows, d))
            raw = term if raw is None else raw + term
        raw_ref[...] = raw
        xo_ref[...] = x_ref[...] + _rms(raw, g_ref[...])

    row_spec = pl.BlockSpec((tm, d), lambda i: (i, 0))
    return _call(
        body, name=name, grid=(t // tm,),
        in_specs=[pl.BlockSpec((kc_n, tm, kc), lambda i: (0, i, 0)),
                  pl.BlockSpec((N_DEV, None, rows, d), lambda i: (0, layer, 0, 0)),
                  pl.BlockSpec((1, d), lambda i: (0, 0)), row_spec],
        out_specs=[row_spec, row_spec], out_shape=[_sds((t, d), F32)] * 2)(a3, wg, g, x_res)


def _alibi_slopes(n_heads):
    hh = np.arange(n_heads, dtype=np.float32) + 1.0
    s = np.power(2.0, -8.0 * hh / n_heads).astype(np.float32)
    return jnp.asarray(np.repeat(s.reshape(n_heads // 2, 2, 1), 2 * ATT_BLOCK, axis=2))


def _band_bias(sl_ref, dil):
    u = lax.broadcasted_iota(jnp.int32, (ATT_BLOCK, 2 * ATT_BLOCK), 0)
    kk = lax.broadcasted_iota(jnp.int32, (ATT_BLOCK, 2 * ATT_BLOCK), 1)
    delta = u + ATT_BLOCK - kk
    valid = (delta >= 0) & (delta <= ATT_BLOCK)
    dist = (delta * dil).astype(F32)
    rows = [jnp.where(valid, -sl_ref[hd:hd + 1, :] * dist, NEG) for hd in range(2)]
    return jnp.concatenate(rows, axis=0)


def _stack_heads(a):
    lane = lax.broadcasted_iota(jnp.int32, a.shape, 1)
    return jnp.concatenate([jnp.where(lane < HEAD_DIM, a, 0.0), jnp.where(lane >= HEAD_DIM, a, 0.0)], axis=0).astype(BF16)


def _unstack_heads(a2):
    top, bot = a2[:ATT_BLOCK], a2[ATT_BLOCK:]
    lane = lax.broadcasted_iota(jnp.int32, top.shape, 1)
    return jnp.where(lane < HEAD_DIM, top, bot)


def _rows_to_lanes(a0, a1):
    eye = lax.broadcasted_iota(jnp.int32, a0.shape, 0) == lax.broadcasted_iota(jnp.int32, a0.shape, 1)
    return jnp.concatenate([jnp.sum(jnp.where(eye, a, 0.0), axis=0, keepdims=True) for a in (a0, a1)], axis=1)


def _fill_bias_t(sl_ref, bias_ref):
    kk = lax.broadcasted_iota(jnp.int32, (2 * ATT_BLOCK, 2 * ATT_BLOCK), 0)
    lane = lax.broadcasted_iota(jnp.int32, (2 * ATT_BLOCK, 2 * ATT_BLOCK), 1)
    delta = lane % ATT_BLOCK + ATT_BLOCK - kk
    valid = (delta >= 0) & (delta <= ATT_BLOCK)
    slope = jnp.concatenate([sl_ref[0:1, :ATT_BLOCK], sl_ref[1:2, :ATT_BLOCK]], axis=1)
    for gi, dil in enumerate(DILATIONS):
        bias = jnp.where(valid, -slope * (delta * dil).astype(F32), NEG)
        bias_ref[2 * gi] = bias
        bias_ref[2 * gi + 1] = jnp.where(kk < ATT_BLOCK, NEG, bias)


def _fill_bias(sl_ref, bias_ref):
    kk = lax.broadcasted_iota(jnp.int32, (2 * ATT_BLOCK, 2 * ATT_BLOCK), 1)
    for gi, dil in enumerate(DILATIONS):
        bias = _band_bias(sl_ref, dil)
        bias_ref[2 * gi] = bias
        bias_ref[2 * gi + 1] = jnp.where(kk < ATT_BLOCK, NEG, bias)


def _strided_keys(dil, r, b, kc_ref, kp_ref, vc_ref, vp_ref, kcar_ref, vcar_ref):
    if b > 0:
        keys = pl.ds((b - 1) * (ATT_BLOCK * dil) + r, 2 * ATT_BLOCK, stride=dil)
        return kc_ref[keys, :].astype(BF16), vc_ref[keys, :].astype(BF16)
    own = pl.ds(r, ATT_BLOCK, stride=dil)
    k_own, v_own = kc_ref[own, :].astype(BF16), vc_ref[own, :].astype(BF16)
    if dil * ATT_BLOCK != SUPER:
        before = pl.ds(SUPER - ATT_BLOCK * dil + r, ATT_BLOCK, stride=dil)
        k_before, v_before = kp_ref[before, :].astype(BF16), vp_ref[before, :].astype(BF16)
    else:
        k_before, v_before = kcar_ref[r], vcar_ref[r]
        kcar_ref[r] = k_own
        vcar_ref[r] = v_own
    return jnp.concatenate([k_before, k_own], axis=0), jnp.concatenate([v_before, v_own], axis=0)


def _attention_fwd(q, kv, slopes, name):
    hp, t, _ = q.shape
    ns = t // SUPER
    nd = len(DILATIONS)

    def body(sl_ref, q_ref, kc_ref, kp_ref, vc_ref, vp_ref, o_ref, lse_ref, og_ref, lg_ref, bias_ref, kcar_ref, vcar_ref):
        n = pl.program_id(1)

        @pl.when(n == 0)
        def _():
            _fill_bias(sl_ref, bias_ref)
            kcar_ref[...] = jnp.zeros_like(kcar_ref)
            vcar_ref[...] = jnp.zeros_like(vcar_ref)

        for gi, dil in enumerate(DILATIONS):
            for idx in range(SUPER // ATT_BLOCK):
                r, b = idx % dil, idx // dil
                qs = b * (ATT_BLOCK * dil) + r
                first = (n == 0).astype(jnp.int32) if b == 0 else 0
                q2 = _stack_heads(q_ref[pl.ds(qs, ATT_BLOCK, stride=dil), :])
                kb, vb = _strided_keys(dil, r, b, kc_ref, kp_ref, vc_ref, vp_ref, kcar_ref, vcar_ref)
                s = _dot_nt(q2, kb) + bias_ref[2 * gi + first]
                m = jnp.max(s, axis=-1, keepdims=True)
                p = jnp.exp(s - m).astype(BF16)
                ol = _dot(p, jnp.concatenate([vb, jnp.ones_like(vb)], axis=1))
                l = ol[:, LANES:]
                o2 = ol[:, :LANES] / l
                lse2 = m + jnp.log(l)
                og_ref[gi, pl.ds(qs, ATT_BLOCK, stride=dil), :] = _unstack_heads(o2)
                lg_ref[gi, pl.ds(qs, ATT_BLOCK, stride=dil), :] = _unstack_heads(lse2)
        lg = [lg_ref[gi] for gi in range(nd)]
        top = functools.reduce(jnp.maximum, lg)
        ws = [jnp.exp(x - top) for x in lg]
        tot = functools.reduce(jnp.add, ws)
        lse_ref[...] = top + jnp.log(tot)
        acc = ws[0] * og_ref[0]
        for gi in range(1, nd):
            acc = acc + ws[gi] * og_ref[gi]
        o_ref[...] = (acc / tot).astype(BF16)

    cur = lambda off: pl.BlockSpec((None, SUPER, LANES), lambda h, n: (h + off, n, 0))
    prev = lambda off: pl.BlockSpec((None, SUPER, LANES), lambda h, n: (h + off, jnp.maximum(n - 1, 0), 0))
    return _call(
        body, name=name, grid=(hp, ns),
        in_specs=[pl.BlockSpec((None, 2, 2 * ATT_BLOCK), lambda h, n: (h, 0, 0)), cur(0), cur(0), prev(0), cur(hp), prev(hp)],
        out_specs=[pl.BlockSpec((SUPER, LANES), lambda h, n: (n, h)), cur(0)],
        out_shape=[_sds((t, hp * LANES), BF16), _sds((hp, t, LANES), F32)],
        scratch_shapes=[pltpu.VMEM((nd, SUPER, LANES), F32), pltpu.VMEM((nd, SUPER, LANES), F32),
                        pltpu.VMEM((2 * nd, 2 * ATT_BLOCK, 2 * ATT_BLOCK), F32)] + [
                            pltpu.VMEM((DILATIONS[-1], ATT_BLOCK, LANES), BF16)] * 2,
    )(slopes, q, kv, kv, kv, kv)


def _attention_bwd(q, kv, o, lse, d_o, dk_in, dv_in, slopes, q_scale, name):
    hp, t, _ = q.shape
    ns = t // SUPER
    shared = dk_in is not None

    def body(sl_ref, q_ref, kc_ref, kp_ref, vc_ref, vp_ref, o_ref, lse_ref, do_ref, *rest):
        dki_ref, dvi_ref = rest[:2] if shared else (None, None)
        dq_ref, dk_ref, dv_ref, dkw_ref, dvw_ref, st_ref, bias_ref, kcar_ref, vcar_ref = rest[2 if shared else 0:]
        n = pl.program_id(1)

        @pl.when(n == 0)
        def _():
            dkw_ref[...] = jnp.zeros_like(dkw_ref)
            dvw_ref[...] = jnp.zeros_like(dvw_ref)

        @pl.when(n > 0)
        def _():
            dkw_ref[0:SUPER, :] = dkw_ref[SUPER:, :]
            dvw_ref[0:SUPER, :] = dvw_ref[SUPER:, :]
            dkw_ref[SUPER:, :] = jnp.zeros((SUPER, LANES), F32)
            dvw_ref[SUPER:, :] = jnp.zeros((SUPER, LANES), F32)

        @pl.when(n < ns)
        def _():
            prod = do_ref[...] * o_ref[...].astype(F32)
            lane = lax.broadcasted_iota(jnp.int32, prod.shape, 1)
            zero = jnp.zeros((SUPER, LANES), F32)
            st_ref[0] = zero + jnp.sum(jnp.where(lane < HEAD_DIM, prod, 0.0), axis=-1, keepdims=True)
            st_ref[1] = zero + jnp.sum(jnp.where(lane >= HEAD_DIM, prod, 0.0), axis=-1, keepdims=True)
            lse = lse_ref[...]
            swapped = pltpu.roll(lse, HEAD_DIM, 1)
            st_ref[2] = jnp.where(lane < HEAD_DIM, lse, swapped)
            st_ref[3] = jnp.where(lane >= HEAD_DIM, lse, swapped)
            dq_ref[...] = jnp.zeros_like(dq_ref)

            @pl.when(n == 0)
            def _():
                _fill_bias_t(sl_ref, bias_ref)
                kcar_ref[...] = jnp.zeros_like(kcar_ref)
                vcar_ref[...] = jnp.zeros_like(vcar_ref)

            for gi, dil in enumerate(DILATIONS):
                for idx in range(SUPER // ATT_BLOCK):
                    r, b = idx % dil, idx // dil
                    qs = b * (ATT_BLOCK * dil) + r
                    ks = SUPER + (b - 1) * (ATT_BLOCK * dil) + r
                    first = (n == 0).astype(jnp.int32) if b == 0 else 0
                    rows = pl.ds(qs, ATT_BLOCK, stride=dil)
                    keys = pl.ds(ks, 2 * ATT_BLOCK, stride=dil)
                    q2 = _stack_heads(q_ref[rows, :])
                    do2 = _stack_heads(do_ref[rows, :])
                    kb, vb = _strided_keys(dil, r, b, kc_ref, kp_ref, vc_ref, vp_ref, kcar_ref, vcar_ref)
                    dd = _rows_to_lanes(st_ref[0, rows, :], st_ref[1, rows, :])
                    lse_b = _rows_to_lanes(st_ref[2, rows, :], st_ref[3, rows, :])
                    ps, dss = [], []
                    for half in range(2):
                        hk = slice(half * ATT_BLOCK, (half + 1) * ATT_BLOCK)
                        p = jnp.exp(_dot_nt(kb[hk], q2) + bias_ref[2 * gi + first, hk, :] - lse_b)
                        dss.append((p * (_dot_nt(vb[hk], do2) - dd)).astype(BF16))
                        ps.append(p.astype(BF16))
                    p, ds = jnp.concatenate(ps, axis=0), jnp.concatenate(dss, axis=0)
                    dvw_ref[keys, :] += _dot(p, do2)
                    dkw_ref[keys, :] += _dot(ds, q2)
                    dq_ref[rows, :] += _unstack_heads(_dot_tn(ds, kb)) * q_scale

        dk_ref[...] =dkw_ref[0:SUPER, :] + dki_ref[...] if shared else dkw_ref[0:SUPER, :]
        dv_ref[...] = dvw_ref[0:SUPER, :] + dvi_ref[...] if shared else dvw_ref[0:SUPER, :]

    last = ns - 1
    cur = lambda off: pl.BlockSpec((None, SUPER, LANES), lambda h, n: (h + off, jnp.minimum(n, last), 0))
    prev = lambda off: pl.BlockSpec((None, SUPER, LANES), lambda h, n: (h + off, jnp.clip(n - 1, 0, last), 0))
    nat = pl.BlockSpec((SUPER, LANES), lambda h, n: (jnp.minimum(n, last), h))
    late = pl.BlockSpec((None, SUPER, LANES), lambda h, n: (h, jnp.maximum(n - 1, 0), 0))
    dq, dk, dv = _call(
        body, name=name, grid=(hp, ns + 1),
        in_specs=[pl.BlockSpec((None, 2, 2 * ATT_BLOCK), lambda h, n: (h, 0, 0)), cur(0), cur(0), prev(0), cur(hp), prev(hp),
                  nat, cur(0), nat] + ([late, late] if shared else []),
        out_specs=[cur(0), late, late],
        out_shape=[_sds((hp, t, LANES), F32)] * 3,
        scratch_shapes=[pltpu.VMEM((2 * SUPER, LANES), F32)] * 2 + [
            pltpu.VMEM((4, SUPER, LANES), F32), pltpu.VMEM((2 * len(DILATIONS), 2 * ATT_BLOCK, 2 * ATT_BLOCK), F32)] + [
                pltpu.VMEM((DILATIONS[-1], ATT_BLOCK, LANES), BF16)] * 2,
    )(slopes, q, kv, kv, kv, kv, o, lse, d_o, *((dk_in, dv_in) if shared else ()))
    return dq, dk, dv


def _loss_head(y, target, raw, g, name):
    t, d = y.shape
    tm = ROW_TILE

    def body(y_ref, t_ref, raw_ref, g_ref, sq_ref, dy_ref, draw_ref, dg_ref):
        i = pl.program_id(0)
        err = y_ref[...] - t_ref[...]
        dy = err * (1.0 / d)
        dy_ref[...] = dy
        draw, dg = _rms_bwd(raw_ref[...], g_ref[...], dy)
        draw_ref[...] = draw.astype(BF16)
        sq = jnp.zeros((8, LANES), F32) + jnp.sum(err * err)

        @pl.when(i == 0)
        def _():
            sq_ref[...] = sq
            dg_ref[...] = dg

        @pl.when(i > 0)
        def _():
            sq_ref[...] += sq
            dg_ref[...] += dg

    row = pl.BlockSpec((tm, d), lambda i: (i, 0))
    vec = pl.BlockSpec((1, d), lambda i: (0, 0))
    return _call(
        body, name=name, grid=(t // tm,), in_specs=[row, row, row, vec],
        out_specs=[pl.BlockSpec((8, LANES), lambda i: (0, 0)), row, row, vec],
        out_shape=[_sds((8, LANES), F32), _sds((t, d), F32), _sds((t, d), BF16), _sds((1, d), F32)])(y, target, raw, g)


def _bwd_matmul_norms(a_specs, a_args, a_tile, n_steps, w_spec, w_arg, w_mat, xa, ga, resid, xb, gb, name,
                      w_transposed=False):
    t, d = xa.shape
    tm = BWD_ROW_TILE
    na = len(a_specs)
    second = xb is not None
    per = 8 if n_steps % 8 == 0 else 1
    n_steps //= per

    def blocks_of(spec, k):
        return pl.BlockSpec(spec.block_shape, lambda i, j: spec.index_map(i, per * j + k))

    def body(*refs):
        a_refs, w_refs = refs[:per * na], refs[per * na:per * na + per]
        xa_ref, ga_ref, res_ref = refs[per * na + per:per * na + per + 3]
        rest = refs[per * na + per + 3:]
        if second:
            xb_ref, gb_ref, dx_ref, d2_ref, dga_ref, dgb_ref, acc_ref = rest
        else:
            dx_ref, dga_ref, acc_ref = rest
        i, j = pl.program_id(0), pl.program_id(1)
        part = None
        for k in range(per):
            term = (_dot if w_transposed else _dot_nt)(a_tile(per * j + k, *a_refs[k * na:(k + 1) * na]), w_mat(w_refs[k]))
            part = term if part is None else part + term

        @pl.when(j == 0)
        def _():
            acc_ref[...] = part

        @pl.when(j > 0)
        def _():
            acc_ref[...] += part

        @pl.when(j == n_steps - 1)
        def _():
            da, dga = _rms_bwd(xa_ref[...], ga_ref[...], acc_ref[...])
            dx = res_ref[...] + da
            dx_ref[...] = dx
            if second:
                d2, dgb = _rms_bwd(xb_ref[...], gb_ref[...], dx)
                d2_ref[...] = d2.astype(BF16)

            @pl.when(i == 0)
            def _():
                dga_ref[...] = dga
                if second:
                    dgb_ref[...] = dgb

            @pl.when(i > 0)
            def _():
                dga_ref[...] += dga
                if second:
                    dgb_ref[...] += dgb

    row = pl.BlockSpec((tm, d), lambda i, j: (i, 0))
    vec = pl.BlockSpec((1, d), lambda i, j: (0, 0))
    in_specs = [blocks_of(sp, k) for k in range(per) for sp in a_specs] + [blocks_of(w_spec, k) for k in range(per)]
    in_specs += [row, vec, row]
    args = list(a_args) * per + [w_arg] * per + [xa, ga, resid]
    if second:
        in_specs += [row, vec]
        args += [xb, gb]
        out_specs = [row, row, vec, vec]
        out_shape = [_sds((t, d), F32), _sds((t, d), BF16), _sds((1, d), F32), _sds((1, d), F32)]
    else:
        out_specs = [row, vec]
        out_shape = [_sds((t, d), F32), _sds((1, d), F32)]
    return _call(body, name=name, grid=(t // tm, n_steps), in_specs=in_specs, out_specs=out_specs,
                 out_shape=out_shape, scratch_shapes=[pltpu.VMEM((tm, d), F32)])(*args)


def _heads_to_rows(*refs):
    hp = refs[0].shape[0]
    cols = []
    for p in range(hp):
        v = refs[0][p]
        for r in refs[1:]:
            v = v + r[p]
        cols.append(v)
    return jnp.concatenate(cols, axis=-1).astype(BF16)


def _matmul_nt_rows(a, wg, layer, out_dtype, name):
    t, d = a.shape
    tm = ROW_TILE

    def body(a_ref, w_ref, o_ref):
        o_ref[...] = _dot_nt(a_ref[...], w_ref[...].reshape(d, d)).astype(out_dtype)

    row = pl.BlockSpec((tm, d), lambda i: (i, 0))
    return _call(body, name=name, grid=(t // tm,),
                 in_specs=[row, pl.BlockSpec((N_DEV, None, d // N_DEV, d), lambda i: (0, layer, 0, 0))],
                 out_specs=row, out_shape=_sds((t, d), out_dtype))(a, wg)


def _swiglu_bwd(d_ff, wg, layer, gate, up, name):
    t, d = d_ff.shape
    fc = gate.shape[-1]
    rows = wg.shape[2]
    tm = BIG_ROW_TILE

    def body(df_ref, w_ref, g_ref, u_ref, dh_ref):
        w = w_ref[...].reshape(2 * rows, d)
        for r0 in range(0, tm, SWIGLU_ROWS):
            rs = slice(r0, r0 + SWIGLU_ROWS)
            da = _dot_nt(df_ref[rs, :], w)
            gate, up = g_ref[rs, :].astype(F32), u_ref[rs, :].astype(F32)
            sig = jax.nn.sigmoid(gate)
            dh_ref[0, rs, :] = (da * up * (sig * (1.0 + gate * (1.0 - sig)))).astype(BF16)
            dh_ref[1, rs, :] = (da * (gate * sig)).astype(BF16)

    return _call(
        body, name=name, grid=(t // tm, 4),
        in_specs=[pl.BlockSpec((tm, d), lambda i, c: (i, 0)),
                  pl.BlockSpec((2, None, rows, d), lambda i, c: (c, layer, 0, 0)),
                  pl.BlockSpec((None, tm, fc), lambda i, c: (c, i, 0)),
                  pl.BlockSpec((None, tm, fc), lambda i, c: (c, i, 0))],
        out_specs=pl.BlockSpec((None, 2, tm, fc), lambda i, c: (c, 0, i, 0)),
        out_shape=_sds((4, 2, t, fc), BF16))(d_ff, wg, gate, up)


def _conv_bwd(p, d_z, cw, name):
    t, d3 = p.shape
    d = d3 // 3
    tm = ROW_TILE
    hb = tm // HALO
    nt = t // tm

    def body(p_ref, prev_ref, next_ref, dz_ref, dzn_ref, cw_ref, dp_ref, dcw_ref):
        i = pl.program_id(0)
        b = p_ref[:, 0:d].astype(F32)
        c = p_ref[:, d:2 * d].astype(F32)
        h = p_ref[:, 2 * d:3 * d].astype(F32)
        u = c * h
        hu = prev_ref[:, d:2 * d].astype(F32) * prev_ref[:, 2 * d:3 * d].astype(F32) * (i > 0).astype(F32)
        u1, u2 = _shift_down(u, hu, 1, tm), _shift_down(u, hu, 2, tm)
        uc = cw_ref[2:3, :] * u + cw_ref[1:2, :] * u1 + cw_ref[0:1, :] * u2
        dz = dz_ref[...].astype(F32)
        duc = dz * b
        dn = dzn_ref[...].astype(F32) * next_ref[:, 0:d].astype(F32) * (i < nt - 1).astype(F32)
        du = cw_ref[2:3, :] * duc + cw_ref[1:2, :] * _shift_up(duc, dn, 1, tm) + cw_ref[0:1, :] * _shift_up(duc, dn, 2, tm)
        dp_ref[:, 0:d] = (dz * uc).astype(BF16)
        dp_ref[:, d:2 * d] = (du * h).astype(BF16)
        dp_ref[:, 2 * d:3 * d] = (du * c).astype(BF16)
        dcw = jnp.concatenate([jnp.sum(duc * u2, axis=0, keepdims=True), jnp.sum(duc * u1, axis=0, keepdims=True),
                               jnp.sum(duc * u, axis=0, keepdims=True)], axis=0)

        @pl.when(i == 0)
        def _():
            dcw_ref[...] = dcw

        @pl.when(i > 0)
        def _():
            dcw_ref[...] += dcw

    last_halo = t // HALO - 1
    return _call(
        body, name=name, grid=(nt,),
        in_specs=[pl.BlockSpec((tm, d3), lambda i: (i, 0)),
                  pl.BlockSpec((HALO, d3), lambda i: (jnp.maximum(i * hb - 1, 0), 0)),
                  pl.BlockSpec((HALO, d3), lambda i: (jnp.minimum((i + 1) * hb, last_halo), 0)),
                  pl.BlockSpec((tm, d), lambda i: (i, 0)),
                  pl.BlockSpec((HALO, d), lambda i: (jnp.minimum((i + 1) * hb, last_halo), 0)),
                  pl.BlockSpec((3, d), lambda i: (0, 0))],
        out_specs=[pl.BlockSpec((tm, d3), lambda i: (i, 0)), pl.BlockSpec((3, d), lambda i: (0, 0))],
        out_shape=[_sds((t, d3), BF16), _sds((3, d), F32)])(p, p, p, d_z, d_z, cw)


def _grad_weight(a_specs, a_args, a_tile, b_specs, b_args, b_tile, n_out, acc_shape, out_spec, out_shape, t, name,
                 a_transposed=False):
    tt = GRAD_ROW_TILE
    na, nb = len(a_specs), len(b_specs)

    def body(*refs):
        a_refs, b_refs = refs[:na], refs[na:na + nb]
        o_ref, acc_ref = refs[na + nb:]
        s = pl.program_id(1)
        a, b = a_tile(pl.program_id(0), *a_refs), b_tile(pl.program_id(0), *b_refs)
        part = _dot(a, b) if a_transposed else _dot_tn(a, b)

        @pl.when(s == 0)
        def _():
            acc_ref[...] = part

        @pl.when(s > 0)
        def _():
            acc_ref[...] += part

        @pl.when(s == t // tt - 1)
        def _():
            acc = acc_ref[...].astype(BF16)
            if o_ref.shape[-1] == acc.shape[-1]:
                o_ref[...] = acc.reshape(o_ref.shape)
            else:
                for k in range(o_ref.shape[0]):
                    o_ref[k] = acc[:, k * o_ref.shape[-1]:(k + 1) * o_ref.shape[-1]]

    return _call(body, name=name, grid=(n_out, t // tt), in_specs=list(a_specs) + list(b_specs), out_specs=out_spec,
                 out_shape=out_shape, scratch_shapes=[pltpu.VMEM(acc_shape, F32)])(*a_args, *b_args)


def _ident(*args):
    return args[-1][...]


def _heads_tile(j, *refs):
    return _heads_to_rows(*refs)


def kernel(x, norm_g, conv_in_w, conv_w, conv_out_w, kv_norm_g, kv_w, q_w, o_w, ffn_in_w, ffn_out_w, loss_target, m_norm_g, m_conv_in_w, m_conv_w, m_conv_out_w, m_kv_norm_g, m_kv_w, m_q_w, m_o_w, m_ffn_in_w, m_ffn_out_w, v_norm_g, v_conv_in_w, v_conv_w, v_conv_out_w, v_kv_norm_g, v_kv_w, v_q_w, v_o_w, v_ffn_in_w, v_ffn_out_w):
    x0 = x[0]
    target = loss_target[0]
    t, d = x0.shape
    depth = norm_g.shape[0]
    n_a = conv_in_w.shape[0]
    n_b = q_w.shape[0]
    hp = d // LANES
    tm, tg = BWD_ROW_TILE, GRAD_ROW_TILE
    assert t % SUPER == 0 and d % LANES == 0 and depth == n_a + n_b
    dev = 4 * lax.axis_index("x") + 2 * lax.axis_index("y") + lax.axis_index("c")

    n_small = 4 * depth + 3 * n_a
    small_rows = -(-(n_small + 1) // 8) * 8
    small_local = jnp.concatenate([norm_g.reshape(4 * depth, -1), conv_w.reshape(3 * n_a, -1),
                                   jnp.zeros((small_rows - n_small, norm_g.shape[-1]), F32)], axis=0)
    swap = lambda a: jnp.swapaxes(a, 1, 2)
    big = {"conv_in_w": conv_in_w, "conv_out_w": conv_out_w, "kv_w": kv_w[None], "q_w": q_w, "o_w": o_w,
           "ffn_in_w": swap(ffn_in_w), "ffn_out_w": ffn_out_w}
    names = list(big)

    def group(layer):
        if layer < n_a:
            return [("conv_in_w", layer), ("conv_out_w", layer), ("ffn_in_w", layer), ("ffn_out_w", layer)]
        j = layer - n_a
        return ([("kv_w", 0)] if j == 0 else []) + [("q_w", j), ("o_w", j), ("ffn_in_w", layer), ("ffn_out_w", layer)]

    slot = dev.astype(jnp.int32).reshape(1)
    is_ffn = lambda key: key[0].startswith("ffn")
    first_keys = [key for key in group(0) if not is_ffn(key)]
    first = _all_gather([small_local] + [_cast_layer(big[k], i, None, f"cast_{k}_{i}") for k, i in first_keys], "gather_weights")
    small_all = first[0].transpose(1, 0, 2).reshape(small_rows, d)
    wl = {key: a[:, None] for key, a in zip(first_keys, first[1:])}

    def gather_start(keys, after, tag):
        lands = [_cast_layer(big[k], i, slot, f"cast_{k}_{i}") for k, i in keys]
        send_sems, recv_sems, _, lands, tok = _send_start([], lands, after, "gather_near_start" + tag, "gather_near")
        return (keys, tag, send_sems, recv_sems, lands), tok[0, 0]

    def gather_mid(flight, after):
        keys, tag, send_sems, recv_sems, lands = flight
        _, lands = _send_wait(send_sems, recv_sems, [], lands, after, "gather_near_wait" + tag, "gather_near")
        send_sems, recv_sems, _, lands, tok = _send_start([], lands, after, "gather_far_start" + tag, "gather_far")
        return (keys, tag, send_sems, recv_sems, lands), tok[0, 0]

    def gather_wait(flight, after):
        keys, tag, send_sems, recv_sems, lands = flight
        _, lands = _send_wait(send_sems, recv_sems, [], lands, after, "gather_far_wait" + tag, "gather_far")
        wl.update({key: a[:, None] for key, a in zip(keys, lands)})

    in_flight, token = gather_start([key for key in group(0) if is_ffn(key)], small_all, "_l0")
    W = lambda k, i: (wl[(k, i)], 0)
    gain = lambda layer, k: small_all[4 * layer + k][None]
    taps = lambda layer: small_all[4 * depth + 3 * layer: 4 * depth + 3 * layer + 3]
    g_kv = kv_norm_g[None]
    slopes = _alibi_slopes(d // HEAD_DIM)
    fc = big["ffn_in_w"].shape[-2]
    cb = big["conv_in_w"].shape[-1]
    kvb = big["kv_w"].shape[-1]
    q_scale = HEAD_DIM ** -0.5

    saved = []
    kv = kvn_t = None
    xs = x0
    for layer in range(depth):
        tag = f"_l{layer}"
        g0 = g1 = g2 = g3 = 0.0
        if layer == 0:
            g0 = token
        else:
            gather_wait(in_flight, xs)
            if layer + 1 < depth:
                in_flight, g0 = gather_start(group(layer + 1), xs, f"_l{layer + 1}")
        s = {"x_in": xs}
        g0 = gain(layer, 0) + g0
        if layer < n_a:
            s["p"], s["xn_t"] = _norm_matmul_cols(xs, g0, *W("conv_in_w", layer), "cols", "conv_in" + tag)
            if layer == 0:
                in_flight, g1 = gather_mid(in_flight, s["p"])
            s["z"] = _conv_fwd(s["p"], taps(layer) + g1, "conv" + tag)
            s["mix"], x_mid = _matmul_norm_residual(s["z"][None], *W("conv_out_w", layer), gain(layer, 1), xs, "conv_out" + tag)
        else:
            j = layer - n_a
            if kv is None:
                kv, kvn_t = _norm_matmul_cols(xs, g_kv, *W("kv_w", 0), "heads", "kv_proj")
            s["q"], s["xn"] = _norm_matmul_heads(xs, g0, *W("q_w", j), q_scale, "q_proj" + tag)
            s["o"], s["lse"] = _attention_fwd(s["q"], kv, slopes, "attention" + tag)
            s["mix"], x_mid = _matmul_norm_residual(s["o"][None], *W("o_w", j), gain(layer, 1), xs, "o_proj" + tag)
        s["x_mid"] = x_mid
        if layer == 0:
            gather_wait(in_flight, x_mid)
            in_flight, g2 = gather_start(group(1), x_mid, "_l1")
        elif layer + 1 < depth:
            in_flight, g2 = gather_mid(in_flight, x_mid)
        s["gate"], s["up"], s["a"], s["fn"] = _ffn_in_swiglu(x_mid, gain(layer, 2) + g2, *W("ffn_in_w", layer), "ffn_in" + tag)
        if layer == 0:
            in_flight, g3 = gather_mid(in_flight, s["a"])
        s["ff"], xs = _matmul_norm_residual(s["a"], *W("ffn_out_w", layer), gain(layer, 3) + g3, x_mid, "ffn_out" + tag)
        saved.append(s)

    last = saved[-1]
    sq, dx_out, d_ff, dg3 = _loss_head(xs, target, last["ff"], gain(depth - 1, 3), "loss_head")
    loss = lax.psum(sq[0, 0] * (0.5 / d), ("x", "y", "c"))

    dgain = {(depth - 1, 3): dg3}
    dtaps = {}
    grads = {k: [None] * big[k].shape[0] for k in names}
    dkv_parts = []
    scattering = []

    def scatter_start(keys, tag):
        parts = [grads[k][i] for k, i in keys]
        zones = [lax.empty(p.shape, p.dtype) for p in parts]
        send_sems, recv_sems, parts, zones, tok = _send_start(parts, zones, small_all, "scatter_start" + tag, "scatter")
        scattering.append((keys, tag, send_sems, recv_sems, parts, zones))
        return tok[0, 0]

    for layer in reversed(range(depth)):
        tag = f"_l{layer}"
        s = saved[layer]
        dh = _swiglu_bwd(d_ff, *W("ffn_out_w", layer), s["gate"], s["up"], "swiglu_bwd" + tag)
        rows_out = big["ffn_out_w"].shape[1]
        grads["ffn_out_w"][layer] = _grad_weight(
            [pl.BlockSpec((None, tg, fc), lambda c, i: (c, i, 0))], [s["a"]], _ident,
            [pl.BlockSpec((tg, d), lambda c, i: (i, 0))], [d_ff], _ident,
            4, (fc, d), pl.BlockSpec((2, rows_out, d), lambda c, i: (c, 0, 0)), _sds((N_DEV, rows_out, d), BF16), t,
            "grad_ffn_out" + tag)
        grads["ffn_in_w"][layer] = _grad_weight(
            [pl.BlockSpec((None, None, tg, fc), lambda j, i: (j % 4, j // 4, i, 0))], [dh], _ident,
            [pl.BlockSpec((tg, d), lambda j, i: (i, 0))], [s["fn"]], _ident,
            N_DEV, (fc, d), pl.BlockSpec((None, fc, d), lambda j, i: (j, 0, 0)), _sds((N_DEV, fc, d), BF16), t,
            "grad_ffn_in" + tag)
        tok = scatter_start([("ffn_in_w", layer), ("ffn_out_w", layer)], "_ffn" + tag)
        dx_mid, d_mix, dg2, dg1 = _bwd_matmul_norms(
            [pl.BlockSpec((None, None, tm, fc), lambda i, j: (j % 4, j // 4, i, 0))], [dh], _ident, N_DEV,
            pl.BlockSpec((None, None, fc, d), lambda i, j: (j, 0, 0, 0)), W("ffn_in_w", layer)[0], _ident,
            s["x_mid"], gain(layer, 2) + tok, dx_out, s["mix"], gain(layer, 1), "ffn_in_bwd" + tag, w_transposed=True)
        dgain[(layer, 2)], dgain[(layer, 1)] = dg2, dg1
        full_rows = pl.BlockSpec((N_DEV, d // N_DEV, d), lambda j, i: (0, 0, 0))
        rows_w = lambda wname, idx: (pl.BlockSpec((N_DEV, None, d // N_DEV, d), lambda i, j: (0, 0, 0, 0)), W(wname, idx)[0],
                                     lambda w_ref: w_ref[...].reshape(d, d))
        if layer < n_a:
            d_z = _matmul_nt_rows(d_mix, *W("conv_out_w", layer), BF16, "conv_out_bwd" + tag)
            grads["conv_out_w"][layer] = _grad_weight(
                [pl.BlockSpec((tg, d), lambda j, i: (i, 0))], [s["z"]], _ident,
                [pl.BlockSpec((tg, d), lambda j, i: (i, 0))], [d_mix], _ident,
                1, (d, d), full_rows, _sds((N_DEV, d // N_DEV, d), BF16), t, "grad_conv_out" + tag)
            d_p, dtaps[layer] = _conv_bwd(s["p"], d_z, taps(layer), "conv_bwd" + tag)
            grads["conv_in_w"][layer] = _grad_weight(
                [pl.BlockSpec((d, tg), lambda j, i: (0, i))], [s["xn_t"]], _ident,
                [pl.BlockSpec((tg, 2 * cb), lambda j, i: (i, j))], [d_p], _ident,
                N_DEV // 2, (d, 2 * cb), pl.BlockSpec((2, d, cb), lambda j, i: (j, 0, 0)), _sds((N_DEV, d, cb), BF16), t,
                "grad_conv_in" + tag, a_transposed=True)
            a_specs, a_args, a_tile, n_steps = [pl.BlockSpec((tm, N_DEV * cb), lambda i, j: (i, 0))], [d_p], _ident, 1
            w_spec = pl.BlockSpec((N_DEV, None, d, cb), lambda i, j: (0, 0, 0, 0))
            w_arg = W("conv_in_w", layer)[0]
            w_mat = lambda w_ref: jnp.concatenate([w_ref[k] for k in range(N_DEV)], axis=1)
            resid = dx_mid
        else:
            j_b = layer - n_a
            d_o = _matmul_nt_rows(d_mix, *W("o_w", j_b), F32, "o_proj_bwd" + tag)
            grads["o_w"][j_b] = _grad_weight(
                [pl.BlockSpec((tg, d), lambda j, i: (i, 0))], [s["o"]], _ident,
                [pl.BlockSpec((tg, d), lambda j, i: (i, 0))], [d_mix], _ident,
                1, (d, d), full_rows, _sds((N_DEV, d // N_DEV, d), BF16), t, "grad_o" + tag)
            dk_in, dv_in = dkv_parts[0] if dkv_parts else (None, None)
            dq, dk, dv = _attention_bwd(s["q"], kv, s["o"], s["lse"], d_o, dk_in, dv_in, slopes, q_scale, "attention_bwd" + tag)
            dkv_parts = [(dk, dv)]
            heads_spec = pl.BlockSpec((hp, tg, LANES), lambda j, i: (0, i, 0))
            grads["q_w"][j_b] = _grad_weight(
                [pl.BlockSpec((tg, d), lambda j, i: (i, 0))], [s["xn"]], _ident,
                [heads_spec], [dq], _heads_tile,
                1, (d, d), full_rows, _sds((N_DEV, d // N_DEV, d), BF16), t, "grad_q" + tag)
            a_specs, a_args, a_tile, n_steps = [pl.BlockSpec((hp, tm, LANES), lambda i, j: (0, i, 0))], [dq], _heads_tile, 1
            w_spec, w_arg, w_mat = rows_w("q_w", j_b)
            resid = dx_mid
            if layer == n_a:
                pieces = kvb // LANES
                halves = []
                for src in (0, 1):
                    halves.append([part[src] for part in dkv_parts])
                n_half = len(dkv_parts)
                kv_args = [arr for src in (0, 1) for arr in halves[src]]

                def kv_block(src, j):
                    return jnp.where((j // 4) == src, j % 4, 0)

                def kv_tile(j, *refs):
                    keys = _heads_to_rows(*refs[:n_half])
                    vals = _heads_to_rows(*refs[n_half:])
                    return jnp.where(j < 4, keys, vals)

                kv_specs = [pl.BlockSpec((pieces, tm, LANES), functools.partial(lambda i, j, src: (kv_block(src, j), i, 0), src=src))
                            for src in (0, 1) for _ in range(n_half)]
                resid, dgain["kv"] = _bwd_matmul_norms(
                    kv_specs, kv_args, kv_tile, N_DEV,
                    pl.BlockSpec((None, None, d, kvb), lambda i, j: (j, 0, 0, 0)), W("kv_w", 0)[0], _ident,
                    s["x_in"], g_kv, dx_mid, None, None, "kv_proj_bwd")
                kv_b_specs = [pl.BlockSpec((pieces, tg, LANES), functools.partial(lambda j, i, src: (kv_block(src, j), i, 0), src=src))
                              for src in (0, 1) for _ in range(n_half)]
                grads["kv_w"][0] = _grad_weight(
                    [pl.BlockSpec((d, tg), lambda j, i: (0, i))], [kvn_t], _ident,
                    kv_b_specs, kv_args, kv_tile,
                    N_DEV, (d, kvb), pl.BlockSpec((None, d, kvb), lambda j, i: (j, 0, 0)), _sds((N_DEV, d, kvb), BF16), t,
                    "grad_kv", a_transposed=True)
        tok = scatter_start([key for key in group(layer) if not key[0].startswith("ffn")], "_mix" + tag)
        if layer > 0:
            prev = saved[layer - 1]
            dx_out, d_ff, dg0, dg3p = _bwd_matmul_norms(
                a_specs, a_args, a_tile, n_steps, w_spec, w_arg, w_mat,
                s["x_in"], gain(layer, 0) + tok, resid, prev["ff"], gain(layer - 1, 3), "mixer_in_bwd" + tag)
            dgain[(layer, 0)], dgain[(layer - 1, 3)] = dg0, dg3p
        else:
            grad_x, dg0 = _bwd_matmul_norms(
                a_specs, a_args, a_tile, n_steps, w_spec, w_arg, w_mat,
                s["x_in"], gain(layer, 0), resid, None, None, "mixer_in_bwd" + tag)
            dgain[(layer, 0)] = dg0

    small_grad = jnp.concatenate(
        [dgain[(layer, k)] for layer in range(depth) for k in range(4)] + [dtaps[layer] for layer in range(n_a)]
        + [dgain["kv"]] + [jnp.zeros((small_rows - n_small - 1, d), F32)], axis=0)
    small_grads_all = _all_gather([small_grad], "gather_small_grads")[0]
    lo = dev * (d // N_DEV)

    def pack(ng, cwp, kvg):
        rows = jnp.concatenate([ng.reshape(4 * depth, -1), cwp.reshape(3 * n_a, -1)], axis=0)
        z = lax.dynamic_update_slice(jnp.zeros((small_rows, d), F32), rows, (0, lo))
        return lax.dynamic_update_slice(z, kvg[None], (n_small, 0))

    w_small = lax.dynamic_update_slice(small_all, g_kv, (n_small, 0))
    m_small, v_small = pack(m_norm_g, m_conv_w, m_kv_norm_g), pack(v_norm_g, v_conv_w, v_kv_norm_g)
    sm = _small_adamw(small_grads_all, w_small, m_small, v_small, "adamw_small")

    def unpack(a):
        mine = lax.dynamic_slice(a, (0, lo), (small_rows, d // N_DEV))
        return (mine[:4 * depth].reshape(norm_g.shape), mine[4 * depth:n_small].reshape(conv_w.shape), a[n_small])

    small_out = [unpack(a) for a in sm]

    moments = {"conv_in_w": (m_conv_in_w, v_conv_in_w), "conv_out_w": (m_conv_out_w, v_conv_out_w),
               "kv_w": (m_kv_w[None], v_kv_w[None]), "q_w": (m_q_w, v_q_w), "o_w": (m_o_w, v_o_w),
               "ffn_in_w": (swap(m_ffn_in_w), swap(v_ffn_in_w)), "ffn_out_w": (m_ffn_out_w, v_ffn_out_w)}
    landed = {k: [None] * big[k].shape[0] for k in names}
    for keys, tag, send_sems, recv_sems, parts, zones in scattering:
        parts, zones = _send_wait(send_sems, recv_sems, parts, zones, grad_x, "scatter_wait" + tag, "scatter")
        for (k, i), part, zone in zip(keys, parts, zones):
            landed[k][i] = (part, zone)
    res = {k: _sum_adamw(slot, [p for p, _ in landed[k]], [z for _, z in landed[k]], big[k], moments[k][0], moments[k][1],
                         "adamw_" + k) for k in names}

    def big_out(k, which):
        out = res[k][which]
        return out[0] if k == "kv_w" else swap(out) if k == "ffn_in_w" else out

    out_names = ["norm_g", "conv_in_w", "conv_w", "conv_out_w", "kv_norm_g", "kv_w", "q_w", "o_w", "ffn_in_w", "ffn_out_w"]
    small_pos = {"norm_g": 0, "conv_w": 1, "kv_norm_g": 2}
    outs = [loss, grad_x[None]]
    for which in range(4):
        for k in out_names:
            outs.append(small_out[which][small_pos[k]] if k in small_pos else big_out(k, which))
    return tuple(outs)
```

```python
import functools
import math

import numpy as np
import jax
import jax.numpy as jnp
from jax import lax
from jax.experimental import pallas as pl
from jax.experimental.pallas import tpu as pltpu

F32 = jnp.float32
BF16 = jnp.bfloat16

N_DEV = 8
RMS_EPS = 1e-6
HEAD_DIM = 64
LANES = 128
ATT_BLOCK = 128
DILATIONS = (1, 4, 16)
SUPER = ATT_BLOCK * DILATIONS[-1]
NEG = -1e30

ADAM_LR, ADAM_B1, ADAM_B2, ADAM_EPS, ADAM_WD, ADAM_STEP = 0.001, 0.9, 0.999, 1e-08, 0.01, 10

ROW_TILE = 512
BIG_ROW_TILE = 1024
SWIGLU_ROWS = 256
GRAD_ROW_TILE = 2048
BWD_ROW_TILE = 512
MESH = pl.DeviceIdType.MESH


def _call(body, *, name, grid=None, in_specs=None, out_specs=None, out_shape=None, scratch_shapes=(), prefetch=False,
          **params):
    cp = pltpu.CompilerParams(**params) if params else None
    if prefetch:
        spec = pltpu.PrefetchScalarGridSpec(num_scalar_prefetch=1, grid=grid, in_specs=in_specs, out_specs=out_specs,
                                            scratch_shapes=list(scratch_shapes))
        return pl.pallas_call(body, name=name, grid_spec=spec, out_shape=out_shape, compiler_params=cp)
    kwargs = {k: v for k, v in (("grid", grid), ("in_specs", in_specs), ("out_specs", out_specs)) if v is not None}
    return pl.pallas_call(body, name=name, out_shape=out_shape, scratch_shapes=list(scratch_shapes),
                          compiler_params=cp, **kwargs)


def _sds(shape, dtype):
    return jax.ShapeDtypeStruct(tuple(shape), dtype)


def _rms(x, g):
    r = lax.rsqrt(jnp.mean(x * x, axis=-1, keepdims=True) + RMS_EPS)
    return x * r * g


def _rms_bwd(x, g, dy):
    r = lax.rsqrt(jnp.mean(x * x, axis=-1, keepdims=True) + RMS_EPS)
    xh = x * r
    dxh = dy * g
    dx = r * (dxh - xh * jnp.mean(dxh * xh, axis=-1, keepdims=True))
    return dx, jnp.sum(dy * xh, axis=0, keepdims=True)


def _dot(a, b):
    return jnp.dot(a, b, preferred_element_type=F32)


def _dot_nt(a, b):
    return lax.dot_general(a, b, (((1,), (1,)), ((), ())), preferred_element_type=F32)


def _dot_tn(a, b):
    return lax.dot_general(a, b, (((0,), (0,)), ((), ())), preferred_element_type=F32)


def _mesh_pos():
    return lax.axis_index("x"), lax.axis_index("y"), lax.axis_index("c")


def _all_gather(arrs, name):
    n = len(arrs)

    def body(*refs):
        ins, outs = refs[:n], refs[n:2 * n]
        send_sems, recv_sems, local_sems = refs[2 * n:]
        x, y, c = _mesh_pos()
        me, sibling = (x, y, c), (x, y, 1 - c)
        chips = [(1 - x, y), (x, 1 - y), (1 - x, 1 - y)]

        def copy(a, k, block, to, src=None):
            dst = outs[a].at[4 * block[0] + 2 * block[1] + block[2]]
            return pltpu.make_async_remote_copy(
                src_ref=dst if src is None else src, dst_ref=dst, send_sem=send_sems.at[a, k],
                recv_sem=recv_sems.at[a, k], device_id=to, device_id_type=MESH)

        started = []
        for a in range(n):
            mine = pltpu.make_async_copy(ins[a], outs[a].at[4 * x + 2 * y + c], local_sems.at[a])
            mine.start()
            started.append(mine)
        first = []
        for a in range(n):
            first.append(copy(a, 0, me, sibling, src=ins[a]))
            first += [copy(a, 1 + j, me, (*chip, c), src=ins[a]) for j, chip in enumerate(chips)]
        for cp in first:
            cp.start()
        passed = []
        for a in range(n):
            for j, chip in enumerate(chips):
                copy(a, 1 + j, (*chip, c), me).wait_recv()
                fwd = copy(a, 4 + j, (*chip, c), sibling)
                fwd.start()
                passed.append(fwd)
        for a in range(n):
            copy(a, 0, sibling, me).wait_recv()
            for j, chip in enumerate(chips):
                copy(a, 4 + j, (*chip, 1 - c), me).wait_recv()
        for cp in first + passed:
            cp.wait_send()
        for cp in started:
            cp.wait()

    any_spec = pl.BlockSpec(memory_space=pl.ANY)
    outs = _call(
        body, name=name, in_specs=[any_spec] * n, out_specs=[any_spec] * n,
        out_shape=[_sds((N_DEV,) + a.shape, a.dtype) for a in arrs],
        scratch_shapes=[pltpu.SemaphoreType.DMA((n, 7)), pltpu.SemaphoreType.DMA((n, 7)), pltpu.SemaphoreType.DMA((n,))],
        has_side_effects=True,
    )(*arrs)
    return list(outs)


HBM_SPEC = pl.BlockSpec(memory_space=pltpu.HBM)
SEM_SPEC = pl.BlockSpec(memory_space=pltpu.SEMAPHORE)
DATAFLOW = pltpu.SideEffectType.DATAFLOW_SIDE_EFFECTING
PEERS = [(dx, dy, dc) for dx in (0, 1) for dy in (0, 1) for dc in (0, 1)][1:]


def _peer(flip):
    x, y, c = _mesh_pos()
    return tuple(1 - v if f else v for v, f in zip((x, y, c), flip))


def _slot(pos):
    return 4 * pos[0] + 2 * pos[1] + pos[2]


def _in_hbm(a):
    return pltpu.with_memory_space_constraint(a, pltpu.HBM)


NEAR = [(0, 0, 1), (1, 0, 0), (0, 1, 0), (1, 1, 0)]


def _direct_copies(srcs, lands, send_sems, recv_sems, mode):
    me = _slot(_mesh_pos())
    copies = []
    for a in range(len(lands)):
        if mode == "scatter":
            plan = [(srcs[a].at[_slot(_peer(flip))], lands[a].at[me], _peer(flip)) for flip in PEERS]
        elif mode == "gather_near":
            plan = [(lands[a].at[me], lands[a].at[me], _peer(flip)) for flip in NEAR]
        else:
            origins = [_slot(_peer(flip)) for flip in NEAR[1:]]
            plan = [(lands[a].at[o], lands[a].at[o], _peer(NEAR[0])) for o in origins]
        for k, (src, dst, to) in enumerate(plan):
            idx = a * len(PEERS) + k
            copies.append(pltpu.make_async_remote_copy(
                src_ref=src, dst_ref=dst, send_sem=send_sems.at[idx], recv_sem=recv_sems.at[idx],
                device_id=to, device_id_type=MESH))
    return copies


def _send_start(srcs, lands, after, name, mode):
    ns, nl = len(srcs), len(lands)

    def body(*refs):
        src_refs, land_refs = refs[:ns], refs[ns:ns + nl]
        send_sems, recv_sems = refs[ns + nl + 1:ns + nl + 3]
        token = refs[-1]
        for cp in _direct_copies(src_refs, land_refs, send_sems, recv_sems, mode):
            cp.start()
        token[...] = jnp.zeros_like(token)

    sem = pltpu.SemaphoreType.DMA((nl * len(PEERS),))
    outs = pl.pallas_call(
        body, name=name,
        out_shape=(sem, sem) + tuple(pltpu.HBM(a.shape, a.dtype) for a in list(srcs) + list(lands))
        + (_sds((8, LANES), F32),),
        in_specs=[HBM_SPEC] * (ns + nl) + [pl.BlockSpec(memory_space=pl.ANY)],
        out_specs=(SEM_SPEC, SEM_SPEC) + (HBM_SPEC,) * (ns + nl) + (pl.BlockSpec(memory_space=pltpu.VMEM),),
        input_output_aliases={i: 2 + i for i in range(ns + nl)},
        compiler_params=pltpu.CompilerParams(has_side_effects=DATAFLOW),
    )(*[_in_hbm(a) for a in list(srcs) + list(lands)], after)
    send_sems, recv_sems = outs[0], outs[1]
    return send_sems, recv_sems, list(outs[2:2 + ns]), list(outs[2 + ns:2 + ns + nl]), outs[-1]


def _send_wait(send_sems, recv_sems, srcs, lands, after, name, mode):
    ns, nl = len(srcs), len(lands)

    def body(*refs):
        src_refs, land_refs = refs[:ns], refs[ns:ns + nl]
        send_sems, recv_sems = refs[ns + nl:ns + nl + 2]
        copies = _direct_copies(src_refs, land_refs, send_sems, recv_sems, mode)
        for cp in copies:
            cp.wait_send()
        for cp in copies:
            cp.wait_recv()

    outs = pl.pallas_call(
        body, name=name,
        out_shape=tuple(pltpu.HBM(a.shape, a.dtype) for a in list(srcs) + list(lands)),
        in_specs=[HBM_SPEC] * (ns + nl) + [SEM_SPEC, SEM_SPEC, pl.BlockSpec(memory_space=pl.ANY)],
        out_specs=(HBM_SPEC,) * (ns + nl),
        input_output_aliases={i: i for i in range(ns + nl)},
        compiler_params=pltpu.CompilerParams(has_side_effects=DATAFLOW),
    )(*srcs, *lands, send_sems, recv_sems, after)
    return list(outs[:ns]), list(outs[ns:])


def _row_tile(rows, cap=512):
    t = min(rows, cap)
    while rows % t or (t % 16 and t != rows):
        t -= 1
    return t


def _as2d(a):
    return a.reshape(-1, a.shape[-1])


def _cast_layer(w, layer, slot, name):
    _, rows, cols = w.shape
    tr = _row_tile(rows)

    def body(*refs):
        refs[-1][...] = refs[-2][...].astype(BF16)

    if slot is None:
        return _call(body, name=name, grid=(rows // tr,),
                     in_specs=[pl.BlockSpec((None, tr, cols), lambda i: (layer, i, 0))],
                     out_specs=pl.BlockSpec((tr, cols), lambda i: (i, 0)), out_shape=_sds((rows, cols), BF16))(w)
    return _call(body, name=name, grid=(rows // tr,), prefetch=True,
                 in_specs=[pl.BlockSpec((None, tr, cols), lambda i, s: (layer, i, 0))],
                 out_specs=pl.BlockSpec((None, tr, cols), lambda i, s: (s[0], i, 0)),
                 out_shape=_sds((N_DEV, rows, cols), BF16))(slot, w)


def _sum_adamw(slot, parts, lands, w, m, v, name):
    n_l = len(parts)
    _, rows, cols = lands[0].shape
    tr = _row_tile(rows, 128)

    def body(s_ref, *refs):
        p_refs, l_refs = refs[:n_l], refs[n_l:2 * n_l]
        w_ref, m_ref, v_ref, g_ref, d_ref, nm_ref, nv_ref = refs[2 * n_l:]
        for k in range(n_l):
            @pl.when(pl.program_id(0) == k)
            def _(k=k):
                own = p_refs[k][...]
                g = jnp.zeros((tr, cols), F32)
                for j in range(N_DEV):
                    g = g + jnp.where(s_ref[0] == j, own, l_refs[k][j]).astype(F32)
                delta, nm, nv = _adamw_math(w_ref[...], g, m_ref[...], v_ref[...])
                g_ref[...] = g
                d_ref[...] = delta
                nm_ref[...] = nm
                nv_ref[...] = nv

    def own_block(k):
        return pl.BlockSpec((None, tr, cols), lambda l, i, s: (s[0], jnp.where(l == k, i, 0), 0))

    def zone_block(k):
        return pl.BlockSpec((N_DEV, tr, cols), lambda l, i, s: (0, jnp.where(l == k, i, 0), 0))

    lay = pl.BlockSpec((None, tr, cols), lambda l, i, s: (l, i, 0))
    return _call(body, name=name, grid=(n_l, rows // tr), prefetch=True,
                 in_specs=[own_block(k) for k in range(n_l)] + [zone_block(k) for k in range(n_l)] + [lay, lay, lay],
                 out_specs=[lay] * 4, out_shape=[_sds((n_l, rows, cols), F32)] * 4)(slot, *parts, *lands, w, m, v)


def _adamw_math(w, g, m, v):
    m = ADAM_B1 * m + (1.0 - ADAM_B1) * g
    v = ADAM_B2 * v + (1.0 - ADAM_B2) * (g * g)
    m_hat = m / (1.0 - ADAM_B1 ** ADAM_STEP)
    v_hat = v / (1.0 - ADAM_B2 ** ADAM_STEP)
    delta = -ADAM_LR * (m_hat / (jnp.sqrt(v_hat) + ADAM_EPS) + ADAM_WD * w)
    return delta, m, v


def _small_adamw(gathered, w, m, v, name):
    def body(a_ref, w_ref, m_ref, v_ref, g_ref, d_ref, nm_ref, nv_ref):
        g = a_ref[0]
        for k in range(1, N_DEV):
            g = g + a_ref[k]
        delta, nm, nv = _adamw_math(w_ref[...], g, m_ref[...], v_ref[...])
        g_ref[...] = g
        d_ref[...] = delta
        nm_ref[...] = nm
        nv_ref[...] = nv

    return _call(body, name=name, out_shape=[_sds(w.shape, F32)] * 4)(gathered, w, m, v)


def _norm_matmul_cols(x, g, wg, layer, mode, name):
    t, d = x.shape
    nb = wg.shape[-1]
    tm = BIG_ROW_TILE
    per = 2
    pieces = per * nb // LANES

    def body(x_ref, g_ref, w_ref, y_ref, xnt_ref):
        xn = _rms(x_ref[...], g_ref[...])
        xnt_ref[...] = xn.T.astype(BF16)
        xn = xn.astype(BF16)
        for j in range(N_DEV // per):
            y = _dot(xn, jnp.concatenate([w_ref[per * j + k] for k in range(per)], axis=1))
            if mode == "heads":
                for p in range(pieces):
                    y_ref[pieces * j + p] = y[:, p * LANES:(p + 1) * LANES]
            else:
                y_ref[:, j * per * nb:(j + 1) * per * nb] = y.astype(BF16)

    if mode == "cols":
        y_shape, y_spec = _sds((t, N_DEV * nb), BF16), pl.BlockSpec((tm, N_DEV * nb), lambda i: (i, 0))
    else:
        y_shape = _sds((N_DEV // per * pieces, t, LANES), F32)
        y_spec = pl.BlockSpec((N_DEV // per * pieces, tm, LANES), lambda i: (0, i, 0))
    return _call(
        body, name=name, grid=(t // tm,),
        in_specs=[pl.BlockSpec((tm, d), lambda i: (i, 0)), pl.BlockSpec((1, d), lambda i: (0, 0)),
                  pl.BlockSpec((N_DEV, None, d, nb), lambda i: (0, layer, 0, 0))],
        out_specs=[y_spec, pl.BlockSpec((d, tm), lambda i: (0, i))],
        out_shape=[y_shape, _sds((d, t), BF16)])(x, g, wg)


def _ffn_in_swiglu(x, g, wg, layer, name):
    t, d = x.shape
    fc = wg.shape[-2]
    tm = ROW_TILE

    def body(x_ref, g_ref, w_ref, gate_ref, up_ref, a_ref, xn_ref):
        xn = _rms(x_ref[...], g_ref[...]).astype(BF16)
        xn_ref[...] = xn
        for c in range(4):
            gate, up = _dot_nt(xn, w_ref[c]), _dot_nt(xn, w_ref[c + 4])
            gate_ref[c] = gate.astype(BF16)
            up_ref[c] = up.astype(BF16)
            a_ref[c] = (gate * jax.nn.sigmoid(gate) * up).astype(BF16)

    chunks = pl.BlockSpec((4, tm, fc), lambda i: (0, i, 0))
    return _call(
        body, name=name, grid=(t // tm,),
        in_specs=[pl.BlockSpec((tm, d), lambda i: (i, 0)), pl.BlockSpec((1, d), lambda i: (0, 0)),
                  pl.BlockSpec((N_DEV, None, fc, d), lambda i: (0, layer, 0, 0))],
        out_specs=[chunks, chunks, chunks, pl.BlockSpec((tm, d), lambda i: (i, 0))],
        out_shape=[_sds((4, t, fc), BF16)] * 3 + [_sds((t, d), BF16)])(x, g, wg)


def _norm_matmul_heads(x, g, wg, layer, scale, name):
    t, d = x.shape
    tm = ROW_TILE
    hp = d // LANES

    def body(x_ref, g_ref, w_ref, y_ref, xn_ref):
        xn = _rms(x_ref[...], g_ref[...]).astype(BF16)
        xn_ref[...] = xn
        y = _dot(xn, w_ref[...].reshape(d, d)) * scale
        for p in range(hp):
            y_ref[p] = y[:, p * LANES:(p + 1) * LANES]

    return _call(
        body, name=name, grid=(t // tm,),
        in_specs=[pl.BlockSpec((tm, d), lambda i: (i, 0)), pl.BlockSpec((1, d), lambda i: (0, 0)),
                  pl.BlockSpec((N_DEV, None, d // N_DEV, d), lambda i: (0, layer, 0, 0))],
        out_specs=[pl.BlockSpec((hp, tm, LANES), lambda i: (0, i, 0)), pl.BlockSpec((tm, d), lambda i: (i, 0))],
        out_shape=[_sds((hp, t, LANES), F32), _sds((t, d), BF16)])(x, g, wg)


def _shift_down(u, halo, k, tm):
    row = lax.broadcasted_iota(jnp.int32, u.shape, 0)
    out = pltpu.roll(u, k, 0)
    for j in range(k):
        out = jnp.where(row == j, halo[halo.shape[0] - k + j:halo.shape[0] - k + j + 1, :], out)
    return out


def _shift_up(u, halo, k, tm):
    row = lax.broadcasted_iota(jnp.int32, u.shape, 0)
    out = pltpu.roll(u, tm - k, 0)
    for j in range(k):
        out = jnp.where(row == tm - k + j, halo[j:j + 1, :], out)
    return out


HALO = 16


def _conv_fwd(p, cw, name):
    t, d3 = p.shape
    d = d3 // 3
    tm = ROW_TILE
    hb = tm // HALO

    def body(p_ref, prev_ref, cw_ref, z_ref):
        i = pl.program_id(0)
        b = p_ref[:, 0:d].astype(F32)
        u = p_ref[:, d:2 * d].astype(F32) * p_ref[:, 2 * d:3 * d].astype(F32)
        keep = (i > 0).astype(F32)
        hu = prev_ref[:, d:2 * d].astype(F32) * prev_ref[:, 2 * d:3 * d].astype(F32) * keep
        uc = cw_ref[2:3, :] * u + cw_ref[1:2, :] * _shift_down(u, hu, 1, tm) + cw_ref[0:1, :] * _shift_down(u, hu, 2, tm)
        z_ref[...] = (b * uc).astype(BF16)

    return _call(
        body, name=name, grid=(t // tm,),
        in_specs=[pl.BlockSpec((tm, d3), lambda i: (i, 0)),
                  pl.BlockSpec((HALO, d3), lambda i: (jnp.maximum(i * hb - 1, 0), 0)),
                  pl.BlockSpec((3, d), lambda i: (0, 0))],
        out_specs=pl.BlockSpec((tm, d), lambda i: (i, 0)), out_shape=_sds((t, d), BF16))(p, p, cw)


def _matmul_norm_residual(a3, wg, layer, g, x_res, name):
    kc_n, t, kc = a3.shape
    d = wg.shape[-1]
    per = N_DEV // kc_n
    rows = wg.shape[2]
    tm = ROW_TILE

    def body(a_ref, w_ref, g_ref, x_ref, raw_ref, xo_ref):
        raw = None
        for c in range(kc_n):
            term = _dot(a_ref[c], w_ref[c * per:(c + 1) * per].reshape(per * rows, d))
            raw = term if raw is None else raw + term
        raw_ref[...] = raw
        xo_ref[...] = x_ref[...] + _rms(raw, g_ref[...])

    row_spec = pl.BlockSpec((tm, d), lambda i: (i, 0))
    return _call(
        body, name=name, grid=(t // tm,),
        in_specs=[pl.BlockSpec((kc_n, tm, kc), lambda i: (0, i, 0)),
                  pl.BlockSpec((N_DEV, None, rows, d), lambda i: (0, layer, 0, 0)),
                  pl.BlockSpec((1, d), lambda i: (0, 0)), row_spec],
        out_specs=[row_spec, row_spec], out_shape=[_sds((t, d), F32)] * 2)(a3, wg, g, x_res)


def _alibi_slopes(n_heads):
    hh = np.arange(n_heads, dtype=np.float32) + 1.0
    s = np.power(2.0, -8.0 * hh / n_heads).astype(np.float32)
    return jnp.asarray(np.repeat(s.reshape(n_heads // 2, 2, 1), 2 * ATT_BLOCK, axis=2))


def _band_bias(sl_ref, dil):
    u = lax.broadcasted_iota(jnp.int32, (ATT_BLOCK, 2 * ATT_BLOCK), 0)
    kk = lax.broadcasted_iota(jnp.int32, (ATT_BLOCK, 2 * ATT_BLOCK), 1)
    delta = u + ATT_BLOCK - kk
    valid = (delta >= 0) & (delta <= ATT_BLOCK)
    dist = (delta * dil).astype(F32)
    rows = [jnp.where(valid, -sl_ref[hd:hd + 1, :] * dist, NEG) for hd in range(2)]
    return jnp.concatenate(rows, axis=0)


def _stack_heads(a):
    lane = lax.broadcasted_iota(jnp.int32, a.shape, 1)
    return jnp.concatenate([jnp.where(lane < HEAD_DIM, a, 0.0), jnp.where(lane >= HEAD_DIM, a, 0.0)], axis=0).astype(BF16)


def _unstack_heads(a2):
    top, bot = a2[:ATT_BLOCK], a2[ATT_BLOCK:]
    lane = lax.broadcasted_iota(jnp.int32, top.shape, 1)
    return jnp.where(lane < HEAD_DIM, top, bot)


def _rows_to_lanes(a0, a1):
    eye = lax.broadcasted_iota(jnp.int32, a0.shape, 0) == lax.broadcasted_iota(jnp.int32, a0.shape, 1)
    return jnp.concatenate([jnp.sum(jnp.where(eye, a, 0.0), axis=0, keepdims=True) for a in (a0, a1)], axis=1)


def _fill_bias_t(sl_ref, bias_ref):
    kk = lax.broadcasted_iota(jnp.int32, (2 * ATT_BLOCK, 2 * ATT_BLOCK), 0)
    lane = lax.broadcasted_iota(jnp.int32, (2 * ATT_BLOCK, 2 * ATT_BLOCK), 1)
    delta = lane % ATT_BLOCK + ATT_BLOCK - kk
    valid = (delta >= 0) & (delta <= ATT_BLOCK)
    slope = jnp.concatenate([sl_ref[0:1, :ATT_BLOCK], sl_ref[1:2, :ATT_BLOCK]], axis=1)
    for gi, dil in enumerate(DILATIONS):
        bias = jnp.where(valid, -slope * (delta * dil).astype(F32), NEG)
        bias_ref[2 * gi] = bias
        bias_ref[2 * gi + 1] = jnp.where(kk < ATT_BLOCK, NEG, bias)


def _fill_bias(sl_ref, bias_ref):
    kk = lax.broadcasted_iota(jnp.int32, (2 * ATT_BLOCK, 2 * ATT_BLOCK), 1)
    for gi, dil in enumerate(DILATIONS):
        bias = _band_bias(sl_ref, dil)
        bias_ref[2 * gi] = bias
        bias_ref[2 * gi + 1] = jnp.where(kk < ATT_BLOCK, NEG, bias)


def _strided_keys(dil, r, b, kc_ref, kp_ref, vc_ref, vp_ref, kcar_ref, vcar_ref):
    if b > 0:
        keys = pl.ds((b - 1) * (ATT_BLOCK * dil) + r, 2 * ATT_BLOCK, stride=dil)
        return kc_ref[keys, :].astype(BF16), vc_ref[keys, :].astype(BF16)
    own = pl.ds(r, ATT_BLOCK, stride=dil)
    k_own, v_own = kc_ref[own, :].astype(BF16), vc_ref[own, :].astype(BF16)
    if dil * ATT_BLOCK != SUPER:
        before = pl.ds(SUPER - ATT_BLOCK * dil + r, ATT_BLOCK, stride=dil)
        k_before, v_before = kp_ref[before, :].astype(BF16), vp_ref[before, :].astype(BF16)
    else:
        k_before, v_before = kcar_ref[r], vcar_ref[r]
        kcar_ref[r] = k_own
        vcar_ref[r] = v_own
    return jnp.concatenate([k_before, k_own], axis=0), jnp.concatenate([v_before, v_own], axis=0)


def _attention_fwd(q, kv, slopes, name):
    hp, t, _ = q.shape
    ns = t // SUPER
    nd = len(DILATIONS)

    def body(sl_ref, q_ref, kc_ref, kp_ref, vc_ref, vp_ref, o_ref, lse_ref, og_ref, lg_ref, bias_ref, kcar_ref, vcar_ref):
        n = pl.program_id(1)

        @pl.when(n == 0)
        def _():
            _fill_bias(sl_ref, bias_ref)
            kcar_ref[...] = jnp.zeros_like(kcar_ref)
            vcar_ref[...] = jnp.zeros_like(vcar_ref)

        for gi, dil in enumerate(DILATIONS):
            for idx in range(SUPER // ATT_BLOCK):
                r, b = idx % dil, idx // dil
                qs = b * (ATT_BLOCK * dil) + r
                first = (n == 0).astype(jnp.int32) if b == 0 else 0
                q2 = _stack_heads(q_ref[pl.ds(qs, ATT_BLOCK, stride=dil), :])
                kb, vb = _strided_keys(dil, r, b, kc_ref, kp_ref, vc_ref, vp_ref, kcar_ref, vcar_ref)
                s = _dot_nt(q2, kb) + bias_ref[2 * gi + first]
                m = jnp.max(s, axis=-1, keepdims=True)
                p = jnp.exp(s - m).astype(BF16)
                ol = _dot(p, jnp.concatenate([vb, jnp.ones_like(vb)], axis=1))
                l = ol[:, LANES:]
                o2 = ol[:, :LANES] / l
                lse2 = m + jnp.log(l)
                og_ref[gi, pl.ds(qs, ATT_BLOCK, stride=dil), :] = _unstack_heads(o2)
                lg_ref[gi, pl.ds(qs, ATT_BLOCK, stride=dil), :] = _unstack_heads(lse2)
        lg = [lg_ref[gi] for gi in range(nd)]
        top = functools.reduce(jnp.maximum, lg)
        ws = [jnp.exp(x - top) for x in lg]
        tot = functools.reduce(jnp.add, ws)
        lse_ref[...] = top + jnp.log(tot)
        acc = ws[0] * og_ref[0]
        for gi in range(1, nd):
            acc = acc + ws[gi] * og_ref[gi]
        o_ref[...] = (acc / tot).astype(BF16)

    cur = lambda off: pl.BlockSpec((None, SUPER, LANES), lambda h, n: (h + off, n, 0))
    prev = lambda off: pl.BlockSpec((None, SUPER, LANES), lambda h, n: (h + off, jnp.maximum(n - 1, 0), 0))
    return _call(
        body, name=name, grid=(hp, ns),
        in_specs=[pl.BlockSpec((None, 2, 2 * ATT_BLOCK), lambda h, n: (h, 0, 0)), cur(0), cur(0), prev(0), cur(hp), prev(hp)],
        out_specs=[pl.BlockSpec((SUPER, LANES), lambda h, n: (n, h)), cur(0)],
        out_shape=[_sds((t, hp * LANES), BF16), _sds((hp, t, LANES), F32)],
        scratch_shapes=[pltpu.VMEM((nd, SUPER, LANES), F32), pltpu.VMEM((nd, SUPER, LANES), F32),
                        pltpu.VMEM((2 * nd, 2 * ATT_BLOCK, 2 * ATT_BLOCK), F32)] + [
                            pltpu.VMEM((DILATIONS[-1], ATT_BLOCK, LANES), BF16)] * 2,
    )(slopes, q, kv, kv, kv, kv)


def _attention_bwd(q, kv, o, lse, d_o, dk_in, dv_in, slopes, q_scale, name):
    hp, t, _ = q.shape
    ns = t // SUPER
    shared = dk_in is not None

    def body(sl_ref, q_ref, kc_ref, kp_ref, vc_ref, vp_ref, o_ref, lse_ref, do_ref, *rest):
        dki_ref, dvi_ref = rest[:2] if shared else (None, None)
        dq_ref, dk_ref, dv_ref, dkw_ref, dvw_ref, st_ref, bias_ref, kcar_ref, vcar_ref = rest[2 if shared else 0:]
        n = pl.program_id(1)

        @pl.when(n == 0)
        def _():
            dkw_ref[...] = jnp.zeros_like(dkw_ref)
            dvw_ref[...] = jnp.zeros_like(dvw_ref)

        @pl.when(n > 0)
        def _():
            dkw_ref[0:SUPER, :] = dkw_ref[SUPER:, :]
            dvw_ref[0:SUPER, :] = dvw_ref[SUPER:, :]
            dkw_ref[SUPER:, :] = jnp.zeros((SUPER, LANES), F32)
            dvw_ref[SUPER:, :] = jnp.zeros((SUPER, LANES), F32)

        @pl.when(n < ns)
        def _():
            prod = do_ref[...] * o_ref[...].astype(F32)
            lane = lax.broadcasted_iota(jnp.int32, prod.shape, 1)
            zero = jnp.zeros((SUPER, LANES), F32)
            st_ref[0] = zero + jnp.sum(jnp.where(lane < HEAD_DIM, prod, 0.0), axis=-1, keepdims=True)
            st_ref[1] = zero + jnp.sum(jnp.where(lane >= HEAD_DIM, prod, 0.0), axis=-1, keepdims=True)
            lse = lse_ref[...]
            swapped = pltpu.roll(lse, HEAD_DIM, 1)
            st_ref[2] = jnp.where(lane < HEAD_DIM, lse, swapped)
            st_ref[3] = jnp.where(lane >= HEAD_DIM, lse, swapped)
            dq_ref[...] = jnp.zeros_like(dq_ref)

            @pl.when(n == 0)
            def _():
                _fill_bias_t(sl_ref, bias_ref)
                kcar_ref[...] = jnp.zeros_like(kcar_ref)
                vcar_ref[...] = jnp.zeros_like(vcar_ref)

            for gi, dil in enumerate(DILATIONS):
                for idx in range(SUPER // ATT_BLOCK):
                    r, b = idx % dil, idx // dil
                    qs = b * (ATT_BLOCK * dil) + r
                    ks = SUPER + (b - 1) * (ATT_BLOCK * dil) + r
                    first = (n == 0).astype(jnp.int32) if b == 0 else 0
                    rows = pl.ds(qs, ATT_BLOCK, stride=dil)
                    keys = pl.ds(ks, 2 * ATT_BLOCK, stride=dil)
                    q2 = _stack_heads(q_ref[rows, :])
                    do2 = _stack_heads(do_ref[rows, :])
                    kb, vb = _strided_keys(dil, r, b, kc_ref, kp_ref, vc_ref, vp_ref, kcar_ref, vcar_ref)
                    dd = _rows_to_lanes(st_ref[0, rows, :], st_ref[1, rows, :])
                    lse_b = _rows_to_lanes(st_ref[2, rows, :], st_ref[3, rows, :])
                    ps, dss = [], []
                    for half in range(2):
                        hk = slice(half * ATT_BLOCK, (half + 1) * ATT_BLOCK)
                        p = jnp.exp(_dot_nt(kb[hk], q2) + bias_ref[2 * gi + first, hk, :] - lse_b)
                        dss.append((p * (_dot_nt(vb[hk], do2) - dd)).astype(BF16))
                        ps.append(p.astype(BF16))
                    p, ds = jnp.concatenate(ps, axis=0), jnp.concatenate(dss, axis=0)
                    dvw_ref[keys, :] += _dot(p, do2)
                    dkw_ref[keys, :] += _dot(ds, q2)
                    dq_ref[rows, :] += _unstack_heads(_dot_tn(ds, kb)) * q_scale

        dk_ref[...] =dkw_ref[0:SUPER, :] + dki_ref[...] if shared else dkw_ref[0:SUPER, :]
        dv_ref[...] = dvw_ref[0:SUPER, :] + dvi_ref[...] if shared else dvw_ref[0:SUPER, :]

    last = ns - 1
    cur = lambda off: pl.BlockSpec((None, SUPER, LANES), lambda h, n: (h + off, jnp.minimum(n, last), 0))
    prev = lambda off: pl.BlockSpec((None, SUPER, LANES), lambda h, n: (h + off, jnp.clip(n - 1, 0, last), 0))
    nat = pl.BlockSpec((SUPER, LANES), lambda h, n: (jnp.minimum(n, last), h))
    late = pl.BlockSpec((None, SUPER, LANES), lambda h, n: (h, jnp.maximum(n - 1, 0), 0))
    dq, dk, dv = _call(
        body, name=name, grid=(hp, ns + 1),
        in_specs=[pl.BlockSpec((None, 2, 2 * ATT_BLOCK), lambda h, n: (h, 0, 0)), cur(0), cur(0), prev(0), cur(hp), prev(hp),
                  nat, cur(0), nat] + ([late, late] if shared else []),
        out_specs=[cur(0), late, late],
        out_shape=[_sds((hp, t, LANES), F32)] * 3,
        scratch_shapes=[pltpu.VMEM((2 * SUPER, LANES), F32)] * 2 + [
            pltpu.VMEM((4, SUPER, LANES), F32), pltpu.VMEM((2 * len(DILATIONS), 2 * ATT_BLOCK, 2 * ATT_BLOCK), F32)] + [
                pltpu.VMEM((DILATIONS[-1], ATT_BLOCK, LANES), BF16)] * 2,
    )(slopes, q, kv, kv, kv, kv, o, lse, d_o, *((dk_in, dv_in) if shared else ()))
    return dq, dk, dv


def _loss_head(y, target, raw, g, name):
    t, d = y.shape
    tm = ROW_TILE

    def body(y_ref, t_ref, raw_ref, g_ref, sq_ref, dy_ref, draw_ref, dg_ref):
        i = pl.program_id(0)
        err = y_ref[...] - t_ref[...]
        dy = err * (1.0 / d)
        dy_ref[...] = dy
        draw, dg = _rms_bwd(raw_ref[...], g_ref[...], dy)
        draw_ref[...] = draw.astype(BF16)
        sq = jnp.zeros((8, LANES), F32) + jnp.sum(err * err)

        @pl.when(i == 0)
        def _():
            sq_ref[...] = sq
            dg_ref[...] = dg

        @pl.when(i > 0)
        def _():
            sq_ref[...] += sq
            dg_ref[...] += dg

    row = pl.BlockSpec((tm, d), lambda i: (i, 0))
    vec = pl.BlockSpec((1, d), lambda i: (0, 0))
    return _call(
        body, name=name, grid=(t // tm,), in_specs=[row, row, row, vec],
        out_specs=[pl.BlockSpec((8, LANES), lambda i: (0, 0)), row, row, vec],
        out_shape=[_sds((8, LANES), F32), _sds((t, d), F32), _sds((t, d), BF16), _sds((1, d), F32)])(y, target, raw, g)


def _bwd_matmul_norms(a_specs, a_args, a_tile, n_steps, w_spec, w_arg, w_mat, xa, ga, resid, xb, gb, name,
                      w_transposed=False):
    t, d = xa.shape
    tm = BWD_ROW_TILE
    na = len(a_specs)
    second = xb is not None
    per = 8 if n_steps % 8 == 0 else 1
    n_steps //= per

    def blocks_of(spec, k):
        return pl.BlockSpec(spec.block_shape, lambda i, j: spec.index_map(i, per * j + k))

    def body(*refs):
        a_refs, w_refs = refs[:per * na], refs[per * na:per * na + per]
        xa_ref, ga_ref, res_ref = refs[per * na + per:per * na + per + 3]
        rest = refs[per * na + per + 3:]
        if second:
            xb_ref, gb_ref, dx_ref, d2_ref, dga_ref, dgb_ref, acc_ref = rest
        else:
            dx_ref, dga_ref, acc_ref = rest
        i, j = pl.program_id(0), pl.program_id(1)
        part = None
        for k in range(per):
            term = (_dot if w_transposed else _dot_nt)(a_tile(per * j + k, *a_refs[k * na:(k + 1) * na]), w_mat(w_refs[k]))
            part = term if part is None else part + term

        @pl.when(j == 0)
        def _():
            acc_ref[...] = part

        @pl.when(j > 0)
        def _():
            acc_ref[...] += part

        @pl.when(j == n_steps - 1)
        def _():
            da, dga = _rms_bwd(xa_ref[...], ga_ref[...], acc_ref[...])
            dx = res_ref[...] + da
            dx_ref[...] = dx
            if second:
                d2, dgb = _rms_bwd(xb_ref[...], gb_ref[...], dx)
                d2_ref[...] = d2.astype(BF16)

            @pl.when(i == 0)
            def _():
                dga_ref[...] = dga
                if second:
                    dgb_ref[...] = dgb

            @pl.when(i > 0)
            def _():
                dga_ref[...] += dga
                if second:
                    dgb_ref[...] += dgb

    row = pl.BlockSpec((tm, d), lambda i, j: (i, 0))
    vec = pl.BlockSpec((1, d), lambda i, j: (0, 0))
    in_specs = [blocks_of(sp, k) for k in range(per) for sp in a_specs] + [blocks_of(w_spec, k) for k in range(per)]
    in_specs += [row, vec, row]
    args = list(a_args) * per + [w_arg] * per + [xa, ga, resid]
    if second:
        in_specs += [row, vec]
        args += [xb, gb]
        out_specs = [row, row, vec, vec]
        out_shape = [_sds((t, d), F32), _sds((t, d), BF16), _sds((1, d), F32), _sds((1, d), F32)]
    else:
        out_specs = [row, vec]
        out_shape = [_sds((t, d), F32), _sds((1, d), F32)]
    return _call(body, name=name, grid=(t // tm, n_steps), in_specs=in_specs, out_specs=out_specs,
                 out_shape=out_shape, scratch_shapes=[pltpu.VMEM((tm, d), F32)])(*args)


def _heads_to_rows(*refs):
    hp = refs[0].shape[0]
    cols = []
    for p in range(hp):
        v = refs[0][p]
        for r in refs[1:]:
            v = v + r[p]
        cols.append(v)
    return jnp.concatenate(cols, axis=-1).astype(BF16)


def _matmul_nt_rows(a, wg, layer, out_dtype, name):
    t, d = a.shape
    tm = ROW_TILE

    def body(a_ref, w_ref, o_ref):
        o_ref[...] = _dot_nt(a_ref[...], w_ref[...].reshape(d, d)).astype(out_dtype)

    row = pl.BlockSpec((tm, d), lambda i: (i, 0))
    return _call(body, name=name, grid=(t // tm,),
                 in_specs=[row, pl.BlockSpec((N_DEV, None, d // N_DEV, d), lambda i: (0, layer, 0, 0))],
                 out_specs=row, out_shape=_sds((t, d), out_dtype))(a, wg)


def _swiglu_bwd(d_ff, wg, layer, gate, up, name):
    t, d = d_ff.shape
    fc = gate.shape[-1]
    rows = wg.shape[2]
    tm = BIG_ROW_TILE

    def body(df_ref, w_ref, g_ref, u_ref, dh_ref):
        w = w_ref[...].reshape(2 * rows, d)
        for r0 in range(0, tm, SWIGLU_ROWS):
            rs = slice(r0, r0 + SWIGLU_ROWS)
            da = _dot_nt(df_ref[rs, :], w)
            gate, up = g_ref[rs, :].astype(F32), u_ref[rs, :].astype(F32)
            sig = jax.nn.sigmoid(gate)
            dh_ref[0, rs, :] = (da * up * (sig * (1.0 + gate * (1.0 - sig)))).astype(BF16)
            dh_ref[1, rs, :] = (da * (gate * sig)).astype(BF16)

    return _call(
        body, name=name, grid=(t // tm, 4),
        in_specs=[pl.BlockSpec((tm, d), lambda i, c: (i, 0)),
                  pl.BlockSpec((2, None, rows, d), lambda i, c: (c, layer, 0, 0)),
                  pl.BlockSpec((None, tm, fc), lambda i, c: (c, i, 0)),
                  pl.BlockSpec((None, tm, fc), lambda i, c: (c, i, 0))],
        out_specs=pl.BlockSpec((None, 2, tm, fc), lambda i, c: (c, 0, i, 0)),
        out_shape=_sds((4, 2, t, fc), BF16))(d_ff, wg, gate, up)


def _conv_bwd(p, d_z, cw, name):
    t, d3 = p.shape
    d = d3 // 3
    tm = ROW_TILE
    hb = tm // HALO
    nt = t // tm

    def body(p_ref, prev_ref, next_ref, dz_ref, dzn_ref, cw_ref, dp_ref, dcw_ref):
        i = pl.program_id(0)
        b = p_ref[:, 0:d].astype(F32)
        c = p_ref[:, d:2 * d].astype(F32)
        h = p_ref[:, 2 * d:3 * d].astype(F32)
        u = c * h
        hu = prev_ref[:, d:2 * d].astype(F32) * prev_ref[:, 2 * d:3 * d].astype(F32) * (i > 0).astype(F32)
        u1, u2 = _shift_down(u, hu, 1, tm), _shift_down(u, hu, 2, tm)
        uc = cw_ref[2:3, :] * u + cw_ref[1:2, :] * u1 + cw_ref[0:1, :] * u2
        dz = dz_ref[...].astype(F32)
        duc = dz * b
        dn = dzn_ref[...].astype(F32) * next_ref[:, 0:d].astype(F32) * (i < nt - 1).astype(F32)
        du = cw_ref[2:3, :] * duc + cw_ref[1:2, :] * _shift_up(duc, dn, 1, tm) + cw_ref[0:1, :] * _shift_up(duc, dn, 2, tm)
        dp_ref[:, 0:d] = (dz * uc).astype(BF16)
        dp_ref[:, d:2 * d] = (du * h).astype(BF16)
        dp_ref[:, 2 * d:3 * d] = (du * c).astype(BF16)
        dcw = jnp.concatenate([jnp.sum(duc * u2, axis=0, keepdims=True), jnp.sum(duc * u1, axis=0, keepdims=True),
                               jnp.sum(duc * u, axis=0, keepdims=True)], axis=0)

        @pl.when(i == 0)
        def _():
            dcw_ref[...] = dcw

        @pl.when(i > 0)
        def _():
            dcw_ref[...] += dcw

    last_halo = t // HALO - 1
    return _call(
        body, name=name, grid=(nt,),
        in_specs=[pl.BlockSpec((tm, d3), lambda i: (i, 0)),
                  pl.BlockSpec((HALO, d3), lambda i: (jnp.maximum(i * hb - 1, 0), 0)),
                  pl.BlockSpec((HALO, d3), lambda i: (jnp.minimum((i + 1) * hb, last_halo), 0)),
                  pl.BlockSpec((tm, d), lambda i: (i, 0)),
                  pl.BlockSpec((HALO, d), lambda i: (jnp.minimum((i + 1) * hb, last_halo), 0)),
                  pl.BlockSpec((3, d), lambda i: (0, 0))],
        out_specs=[pl.BlockSpec((tm, d3), lambda i: (i, 0)), pl.BlockSpec((3, d), lambda i: (0, 0))],
        out_shape=[_sds((t, d3), BF16), _sds((3, d), F32)])(p, p, p, d_z, d_z, cw)


def _grad_weight(a_specs, a_args, a_tile, b_specs, b_args, b_tile, n_out, acc_shape, out_spec, out_shape, t, name,
                 a_transposed=False, tt=GRAD_ROW_TILE):
    na, nb = len(a_specs), len(b_specs)

    def body(*refs):
        a_refs, b_refs = refs[:na], refs[na:na + nb]
        o_ref, acc_ref = refs[na + nb:]
        s = pl.program_id(1)
        a, b = a_tile(pl.program_id(0), *a_refs), b_tile(pl.program_id(0), *b_refs)
        part = _dot(a, b) if a_transposed else _dot_tn(a, b)

        @pl.when(s == 0)
        def _():
            acc_ref[...] = part

        @pl.when(s > 0)
        def _():
            acc_ref[...] += part

        @pl.when(s == t // tt - 1)
        def _():
            acc = acc_ref[...].astype(BF16)
            if o_ref.shape[-1] == acc.shape[-1]:
                o_ref[...] = acc.reshape(o_ref.shape)
            else:
                for k in range(o_ref.shape[0]):
                    o_ref[k] = acc[:, k * o_ref.shape[-1]:(k + 1) * o_ref.shape[-1]]

    return _call(body, name=name, grid=(n_out, t // tt), in_specs=list(a_specs) + list(b_specs), out_specs=out_spec,
                 out_shape=out_shape, scratch_shapes=[pltpu.VMEM(acc_shape, F32)])(*a_args, *b_args)


def _ident(*args):
    return args[-1][...]


def _heads_tile(j, *refs):
    return _heads_to_rows(*refs)


def kernel(x, norm_g, conv_in_w, conv_w, conv_out_w, kv_norm_g, kv_w, q_w, o_w, ffn_in_w, ffn_out_w, loss_target, m_norm_g, m_conv_in_w, m_conv_w, m_conv_out_w, m_kv_norm_g, m_kv_w, m_q_w, m_o_w, m_ffn_in_w, m_ffn_out_w, v_norm_g, v_conv_in_w, v_conv_w, v_conv_out_w, v_kv_norm_g, v_kv_w, v_q_w, v_o_w, v_ffn_in_w, v_ffn_out_w):
    x0 = x[0]
    target = loss_target[0]
    t, d = x0.shape
    depth = norm_g.shape[0]
    n_a = conv_in_w.shape[0]
    n_b = q_w.shape[0]
    hp = d // LANES
    tm, tg, tgb = BWD_ROW_TILE, GRAD_ROW_TILE, min(t, 2 * GRAD_ROW_TILE)
    assert t % SUPER == 0 and d % LANES == 0 and depth == n_a + n_b
    dev = 4 * lax.axis_index("x") + 2 * lax.axis_index("y") + lax.axis_index("c")

    n_small = 4 * depth + 3 * n_a
    small_rows = -(-(n_small + 1) // 8) * 8
    small_local = jnp.concatenate([norm_g.reshape(4 * depth, -1), conv_w.reshape(3 * n_a, -1),
                                   jnp.zeros((small_rows - n_small, norm_g.shape[-1]), F32)], axis=0)
    swap = lambda a: jnp.swapaxes(a, 1, 2)
    big = {"conv_in_w": conv_in_w, "conv_out_w": conv_out_w, "kv_w": kv_w[None], "q_w": q_w, "o_w": o_w,
           "ffn_in_w": swap(ffn_in_w), "ffn_out_w": ffn_out_w}
    names = list(big)

    def group(layer):
        if layer < n_a:
            return [("conv_in_w", layer), ("conv_out_w", layer), ("ffn_in_w", layer), ("ffn_out_w", layer)]
        j = layer - n_a
        return ([("kv_w", 0)] if j == 0 else []) + [("q_w", j), ("o_w", j), ("ffn_in_w", layer), ("ffn_out_w", layer)]

    slot = dev.astype(jnp.int32).reshape(1)
    is_ffn = lambda key: key[0].startswith("ffn")
    first_keys = [key for key in group(0) if not is_ffn(key)]
    first = _all_gather([small_local] + [_cast_layer(big[k], i, None, f"cast_{k}_{i}") for k, i in first_keys], "gather_weights")
    small_all = first[0].transpose(1, 0, 2).reshape(small_rows, d)
    wl = {key: a[:, None] for key, a in zip(first_keys, first[1:])}

    def gather_start(keys, after, tag):
        lands = [_cast_layer(big[k], i, slot, f"cast_{k}_{i}") for k, i in keys]
        send_sems, recv_sems, _, lands, tok = _send_start([], lands, after, "gather_near_start" + tag, "gather_near")
        return (keys, tag, send_sems, recv_sems, lands), tok[0, 0]

    def gather_mid(flight, after):
        keys, tag, send_sems, recv_sems, lands = flight
        _, lands = _send_wait(send_sems, recv_sems, [], lands, after, "gather_near_wait" + tag, "gather_near")
        send_sems, recv_sems, _, lands, tok = _send_start([], lands, after, "gather_far_start" + tag, "gather_far")
        return (keys, tag, send_sems, recv_sems, lands), tok[0, 0]

    def gather_wait(flight, after):
        keys, tag, send_sems, recv_sems, lands = flight
        _, lands = _send_wait(send_sems, recv_sems, [], lands, after, "gather_far_wait" + tag, "gather_far")
        wl.update({key: a[:, None] for key, a in zip(keys, lands)})

    in_flight, token = gather_start([key for key in group(0) if is_ffn(key)], small_all, "_l0")
    W = lambda k, i: (wl[(k, i)], 0)
    gain = lambda layer, k: small_all[4 * layer + k][None]
    taps = lambda layer: small_all[4 * depth + 3 * layer: 4 * depth + 3 * layer + 3]
    g_kv = kv_norm_g[None]
    slopes = _alibi_slopes(d // HEAD_DIM)
    fc = big["ffn_in_w"].shape[-2]
    cb = big["conv_in_w"].shape[-1]
    kvb = big["kv_w"].shape[-1]
    q_scale = HEAD_DIM ** -0.5

    saved = []
    kv = kvn_t = None
    xs = x0
    for layer in range(depth):
        tag = f"_l{layer}"
        g0 = g1 = g2 = g3 = 0.0
        if layer == 0:
            g0 = token
        else:
            gather_wait(in_flight, xs)
            if layer + 1 < depth:
                in_flight, g0 = gather_start(group(layer + 1), xs, f"_l{layer + 1}")
        s = {"x_in": xs}
        g0 = gain(layer, 0) + g0
        if layer < n_a:
            s["p"], s["xn_t"] = _norm_matmul_cols(xs, g0, *W("conv_in_w", layer), "cols", "conv_in" + tag)
            if layer == 0:
                in_flight, g1 = gather_mid(in_flight, s["p"])
            s["z"] = _conv_fwd(s["p"], taps(layer) + g1, "conv" + tag)
            s["mix"], x_mid = _matmul_norm_residual(s["z"][None], *W("conv_out_w", layer), gain(layer, 1), xs, "conv_out" + tag)
        else:
            j = layer - n_a
            if kv is None:
                kv, kvn_t = _norm_matmul_cols(xs, g_kv, *W("kv_w", 0), "heads", "kv_proj")
            s["q"], s["xn"] = _norm_matmul_heads(xs, g0, *W("q_w", j), q_scale, "q_proj" + tag)
            s["o"], s["lse"] = _attention_fwd(s["q"], kv, slopes, "attention" + tag)
            s["mix"], x_mid = _matmul_norm_residual(s["o"][None], *W("o_w", j), gain(layer, 1), xs, "o_proj" + tag)
        s["x_mid"] = x_mid
        if layer == 0:
            gather_wait(in_flight, x_mid)
            in_flight, g2 = gather_start(group(1), x_mid, "_l1")
        elif layer + 1 < depth:
            in_flight, g2 = gather_mid(in_flight, x_mid)
        s["gate"], s["up"], s["a"], s["fn"] = _ffn_in_swiglu(x_mid, gain(layer, 2) + g2, *W("ffn_in_w", layer), "ffn_in" + tag)
        if layer == 0:
            in_flight, g3 = gather_mid(in_flight, s["a"])
        s["ff"], xs = _matmul_norm_residual(s["a"], *W("ffn_out_w", layer), gain(layer, 3) + g3, x_mid, "ffn_out" + tag)
        saved.append(s)

    last = saved[-1]
    sq, dx_out, d_ff, dg3 = _loss_head(xs, target, last["ff"], gain(depth - 1, 3), "loss_head")
    loss = lax.psum(sq[0, 0] * (0.5 / d), ("x", "y", "c"))

    dgain = {(depth - 1, 3): dg3}
    dtaps = {}
    grads = {k: [None] * big[k].shape[0] for k in names}
    dkv_parts = []
    scattering = []

    def scatter_start(keys, tag):
        parts = [grads[k][i] for k, i in keys]
        zones = [lax.empty(p.shape, p.dtype) for p in parts]
        send_sems, recv_sems, parts, zones, tok = _send_start(parts, zones, small_all, "scatter_start" + tag, "scatter")
        scattering.append((keys, tag, send_sems, recv_sems, parts, zones))
        return tok[0, 0]

    for layer in reversed(range(depth)):
        tag = f"_l{layer}"
        s = saved[layer]
        dh = _swiglu_bwd(d_ff, *W("ffn_out_w", layer), s["gate"], s["up"], "swiglu_bwd" + tag)
        rows_out = big["ffn_out_w"].shape[1]
        grads["ffn_out_w"][layer] = _grad_weight(
            [pl.BlockSpec((None, tgb, fc), lambda c, i: (c, i, 0))], [s["a"]], _ident,
            [pl.BlockSpec((tgb, d), lambda c, i: (i, 0))], [d_ff], _ident,
            4, (fc, d), pl.BlockSpec((2, rows_out, d), lambda c, i: (c, 0, 0)), _sds((N_DEV, rows_out, d), BF16), t,
            "grad_ffn_out" + tag, tt=tgb)
        grads["ffn_in_w"][layer] = _grad_weight(
            [pl.BlockSpec((None, None, tgb, fc), lambda j, i: (j % 4, j // 4, i, 0))], [dh], _ident,
            [pl.BlockSpec((tgb, d), lambda j, i: (i, 0))], [s["fn"]], _ident,
            N_DEV, (fc, d), pl.BlockSpec((None, fc, d), lambda j, i: (j, 0, 0)), _sds((N_DEV, fc, d), BF16), t,
            "grad_ffn_in" + tag, tt=tgb)
        tok = scatter_start([("ffn_in_w", layer), ("ffn_out_w", layer)], "_ffn" + tag)
        dx_mid, d_mix, dg2, dg1 = _bwd_matmul_norms(
            [pl.BlockSpec((None, None, tm, fc), lambda i, j: (j % 4, j // 4, i, 0))], [dh], _ident, N_DEV,
            pl.BlockSpec((None, None, fc, d), lambda i, j: (j, 0, 0, 0)), W("ffn_in_w", layer)[0], _ident,
            s["x_mid"], gain(layer, 2) + tok, dx_out, s["mix"], gain(layer, 1), "ffn_in_bwd" + tag, w_transposed=True)
        dgain[(layer, 2)], dgain[(layer, 1)] = dg2, dg1
        full_rows = pl.BlockSpec((N_DEV, d // N_DEV, d), lambda j, i: (0, 0, 0))
        rows_w = lambda wname, idx: (pl.BlockSpec((N_DEV, None, d // N_DEV, d), lambda i, j: (0, 0, 0, 0)), W(wname, idx)[0],
                                     lambda w_ref: w_ref[...].reshape(d, d))
        if layer < n_a:
            d_z = _matmul_nt_rows(d_mix, *W("conv_out_w", layer), BF16, "conv_out_bwd" + tag)
            grads["conv_out_w"][layer] = _grad_weight(
                [pl.BlockSpec((tg, d), lambda j, i: (i, 0))], [s["z"]], _ident,
                [pl.BlockSpec((tg, d), lambda j, i: (i, 0))], [d_mix], _ident,
                1, (d, d), full_rows, _sds((N_DEV, d // N_DEV, d), BF16), t, "grad_conv_out" + tag)
            d_p, dtaps[layer] = _conv_bwd(s["p"], d_z, taps(layer), "conv_bwd" + tag)
            grads["conv_in_w"][layer] = _grad_weight(
                [pl.BlockSpec((d, tgb), lambda j, i: (0, i))], [s["xn_t"]], _ident,
                [pl.BlockSpec((tgb, 2 * cb), lambda j, i: (i, j))], [d_p], _ident,
                N_DEV // 2, (d, 2 * cb), pl.BlockSpec((2, d, cb), lambda j, i: (j, 0, 0)), _sds((N_DEV, d, cb), BF16), t,
                "grad_conv_in" + tag, a_transposed=True, tt=tgb)
            a_specs, a_args, a_tile, n_steps = [pl.BlockSpec((tm, N_DEV * cb), lambda i, j: (i, 0))], [d_p], _ident, 1
            w_spec = pl.BlockSpec((N_DEV, None, d, cb), lambda i, j: (0, 0, 0, 0))
            w_arg = W("conv_in_w", layer)[0]
            w_mat = lambda w_ref: jnp.concatenate([w_ref[k] for k in range(N_DEV)], axis=1)
            resid = dx_mid
        else:
            j_b = layer - n_a
            d_o = _matmul_nt_rows(d_mix, *W("o_w", j_b), F32, "o_proj_bwd" + tag)
            grads["o_w"][j_b] = _grad_weight(
                [pl.BlockSpec((tg, d), lambda j, i: (i, 0))], [s["o"]], _ident,
                [pl.BlockSpec((tg, d), lambda j, i: (i, 0))], [d_mix], _ident,
                1, (d, d), full_rows, _sds((N_DEV, d // N_DEV, d), BF16), t, "grad_o" + tag)
            dk_in, dv_in = dkv_parts[0] if dkv_parts else (None, None)
            dq, dk, dv = _attention_bwd(s["q"], kv, s["o"], s["lse"], d_o, dk_in, dv_in, slopes, q_scale, "attention_bwd" + tag)
            dkv_parts = [(dk, dv)]
            heads_spec = pl.BlockSpec((hp, tg, LANES), lambda j, i: (0, i, 0))
            grads["q_w"][j_b] = _grad_weight(
                [pl.BlockSpec((tg, d), lambda j, i: (i, 0))], [s["xn"]], _ident,
                [heads_spec], [dq], _heads_tile,
                1, (d, d), full_rows, _sds((N_DEV, d // N_DEV, d), BF16), t, "grad_q" + tag)
            a_specs, a_args, a_tile, n_steps = [pl.BlockSpec((hp, tm, LANES), lambda i, j: (0, i, 0))], [dq], _heads_tile, 1
            w_spec, w_arg, w_mat = rows_w("q_w", j_b)
            resid = dx_mid
            if layer == n_a:
                pieces = kvb // LANES
                halves = []
                for src in (0, 1):
                    halves.append([part[src] for part in dkv_parts])
                n_half = len(dkv_parts)
                kv_args = [arr for src in (0, 1) for arr in halves[src]]

                def kv_block(src, j):
                    return jnp.where((j // 4) == src, j % 4, 0)

                def kv_tile(j, *refs):
                    keys = _heads_to_rows(*refs[:n_half])
                    vals = _heads_to_rows(*refs[n_half:])
                    return jnp.where(j < 4, keys, vals)

                kv_specs = [pl.BlockSpec((pieces, tm, LANES), functools.partial(lambda i, j, src: (kv_block(src, j), i, 0), src=src))
                            for src in (0, 1) for _ in range(n_half)]
                resid, dgain["kv"] = _bwd_matmul_norms(
                    kv_specs, kv_args, kv_tile, N_DEV,
                    pl.BlockSpec((None, None, d, kvb), lambda i, j: (j, 0, 0, 0)), W("kv_w", 0)[0], _ident,
                    s["x_in"], g_kv, dx_mid, None, None, "kv_proj_bwd")
                kv_b_specs = [pl.BlockSpec((pieces, tg, LANES), functools.partial(lambda j, i, src: (kv_block(src, j), i, 0), src=src))
                              for src in (0, 1) for _ in range(n_half)]
                grads["kv_w"][0] = _grad_weight(
                    [pl.BlockSpec((d, tg), lambda j, i: (0, i))], [kvn_t], _ident,
                    kv_b_specs, kv_args, kv_tile,
                    N_DEV, (d, kvb), pl.BlockSpec((None, d, kvb), lambda j, i: (j, 0, 0)), _sds((N_DEV, d, kvb), BF16), t,
                    "grad_kv", a_transposed=True)
        tok = scatter_start([key for key in group(layer) if not key[0].startswith("ffn")], "_mix" + tag)
        if layer > 0:
            prev = saved[layer - 1]
            dx_out, d_ff, dg0, dg3p = _bwd_matmul_norms(
                a_specs, a_args, a_tile, n_steps, w_spec, w_arg, w_mat,
                s["x_in"], gain(layer, 0) + tok, resid, prev["ff"], gain(layer - 1, 3), "mixer_in_bwd" + tag)
            dgain[(layer, 0)], dgain[(layer - 1, 3)] = dg0, dg3p
        else:
            grad_x, dg0 = _bwd_matmul_norms(
                a_specs, a_args, a_tile, n_steps, w_spec, w_arg, w_mat,
                s["x_in"], gain(layer, 0), resid, None, None, "mixer_in_bwd" + tag)
            dgain[(layer, 0)] = dg0

    small_grad = jnp.concatenate(
        [dgain[(layer, k)] for layer in range(depth) for k in range(4)] + [dtaps[layer] for layer in range(n_a)]
        + [dgain["kv"]] + [jnp.zeros((small_rows - n_small - 1, d), F32)], axis=0)
    small_grads_all = _all_gather([small_grad], "gather_small_grads")[0]
    lo = dev * (d // N_DEV)

    def pack(ng, cwp, kvg):
        rows = jnp.concatenate([ng.reshape(4 * depth, -1), cwp.reshape(3 * n_a, -1)], axis=0)
        z = lax.dynamic_update_slice(jnp.zeros((small_rows, d), F32), rows, (0, lo))
        return lax.dynamic_update_slice(z, kvg[None], (n_small, 0))

    w_small = lax.dynamic_update_slice(small_all, g_kv, (n_small, 0))
    m_small, v_small = pack(m_norm_g, m_conv_w, m_kv_norm_g), pack(v_norm_g, v_conv_w, v_kv_norm_g)
    sm = _small_adamw(small_grads_all, w_small, m_small, v_small, "adamw_small")

    def unpack(a):
        mine = lax.dynamic_slice(a, (0, lo), (small_rows, d // N_DEV))
        return (mine[:4 * depth].reshape(norm_g.shape), mine[4 * depth:n_small].reshape(conv_w.shape), a[n_small])

    small_out = [unpack(a) for a in sm]

    moments = {"conv_in_w": (m_conv_in_w, v_conv_in_w), "conv_out_w": (m_conv_out_w, v_conv_out_w),
               "kv_w": (m_kv_w[None], v_kv_w[None]), "q_w": (m_q_w, v_q_w), "o_w": (m_o_w, v_o_w),
               "ffn_in_w": (swap(m_ffn_in_w), swap(v_ffn_in_w)), "ffn_out_w": (m_ffn_out_w, v_ffn_out_w)}
    landed = {k: [None] * big[k].shape[0] for k in names}
    for keys, tag, send_sems, recv_sems, parts, zones in scattering:
        parts, zones = _send_wait(send_sems, recv_sems, parts, zones, grad_x, "scatter_wait" + tag, "scatter")
        for (k, i), part, zone in zip(keys, parts, zones):
            landed[k][i] = (part, zone)
    res = {k: _sum_adamw(slot, [p for p, _ in landed[k]], [z for _, z in landed[k]], big[k], moments[k][0], moments[k][1],
                         "adamw_" + k) for k in names}

    def big_out(k, which):
        out = res[k][which]
        return out[0] if k == "kv_w" else swap(out) if k == "ffn_in_w" else out

    out_names = ["norm_g", "conv_in_w", "conv_w", "conv_out_w", "kv_norm_g", "kv_w", "q_w", "o_w", "ffn_in_w", "ffn_out_w"]
    small_pos = {"norm_g": 0, "conv_w": 1, "kv_norm_g": 2}
    outs = [loss, grad_x[None]]
    for which in range(4):
        for k in out_names:
            outs.append(small_out[which][small_pos[k]] if k in small_pos else big_out(k, which))
    return tuple(outs)
```

```python
import functools
import math

import numpy as np
import jax
import jax.numpy as jnp
from jax import lax
from jax.experimental import pallas as pl
from jax.experimental.pallas import tpu as pltpu

F32 = jnp.float32
BF16 = jnp.bfloat16

N_DEV = 8
RMS_EPS = 1e-6
HEAD_DIM = 64
LANES = 128
ATT_BLOCK = 128
DILATIONS = (1, 4, 16)
SUPER = ATT_BLOCK * DILATIONS[-1]
NEG = -1e30

ADAM_LR, ADAM_B1, ADAM_B2, ADAM_EPS, ADAM_WD, ADAM_STEP = 0.001, 0.9, 0.999, 1e-08, 0.01, 10

ROW_TILE = 512
BIG_ROW_TILE = 1024
SWIGLU_ROWS = 256
GRAD_ROW_TILE = 2048
BWD_ROW_TILE = 512
MESH = pl.DeviceIdType.MESH


def _call(body, *, name, grid=None, in_specs=None, out_specs=None, out_shape=None, scratch_shapes=(), prefetch=False,
          **params):
    cp = pltpu.CompilerParams(**params) if params else None
    if prefetch:
        spec = pltpu.PrefetchScalarGridSpec(num_scalar_prefetch=1, grid=grid, in_specs=in_specs, out_specs=out_specs,
                                            scratch_shapes=list(scratch_shapes))
        return pl.pallas_call(body, name=name, grid_spec=spec, out_shape=out_shape, compiler_params=cp)
    kwargs = {k: v for k, v in (("grid", grid), ("in_specs", in_specs), ("out_specs", out_specs)) if v is not None}
    return pl.pallas_call(body, name=name, out_shape=out_shape, scratch_shapes=list(scratch_shapes),
                          compiler_params=cp, **kwargs)


def _sds(shape, dtype):
    return jax.ShapeDtypeStruct(tuple(shape), dtype)


def _rms(x, g):
    r = lax.rsqrt(jnp.mean(x * x, axis=-1, keepdims=True) + RMS_EPS)
    return x * r * g


def _rms_bwd(x, g, dy):
    r = lax.rsqrt(jnp.mean(x * x, axis=-1, keepdims=True) + RMS_EPS)
    xh = x * r
    dxh = dy * g
    dx = r * (dxh - xh * jnp.mean(dxh * xh, axis=-1, keepdims=True))
    return dx, jnp.sum(dy * xh, axis=0, keepdims=True)


def _dot(a, b):
    return jnp.dot(a, b, preferred_element_type=F32)


def _dot_nt(a, b):
    return lax.dot_general(a, b, (((1,), (1,)), ((), ())), preferred_element_type=F32)


def _dot_tn(a, b):
    return lax.dot_general(a, b, (((0,), (0,)), ((), ())), preferred_element_type=F32)


def _mesh_pos():
    return lax.axis_index("x"), lax.axis_index("y"), lax.axis_index("c")


def _all_gather(arrs, name):
    n = len(arrs)

    def body(*refs):
        ins, outs = refs[:n], refs[n:2 * n]
        send_sems, recv_sems, local_sems = refs[2 * n:]
        x, y, c = _mesh_pos()
        me, sibling = (x, y, c), (x, y, 1 - c)
        chips = [(1 - x, y), (x, 1 - y), (1 - x, 1 - y)]

        def copy(a, k, block, to, src=None):
            dst = outs[a].at[4 * block[0] + 2 * block[1] + block[2]]
            return pltpu.make_async_remote_copy(
                src_ref=dst if src is None else src, dst_ref=dst, send_sem=send_sems.at[a, k],
                recv_sem=recv_sems.at[a, k], device_id=to, device_id_type=MESH)

        started = []
        for a in range(n):
            mine = pltpu.make_async_copy(ins[a], outs[a].at[4 * x + 2 * y + c], local_sems.at[a])
            mine.start()
            started.append(mine)
        first = []
        for a in range(n):
            first.append(copy(a, 0, me, sibling, src=ins[a]))
            first += [copy(a, 1 + j, me, (*chip, c), src=ins[a]) for j, chip in enumerate(chips)]
        for cp in first:
            cp.start()
        passed = []
        for a in range(n):
            for j, chip in enumerate(chips):
                copy(a, 1 + j, (*chip, c), me).wait_recv()
                fwd = copy(a, 4 + j, (*chip, c), sibling)
                fwd.start()
                passed.append(fwd)
        for a in range(n):
            copy(a, 0, sibling, me).wait_recv()
            for j, chip in enumerate(chips):
                copy(a, 4 + j, (*chip, 1 - c), me).wait_recv()
        for cp in first + passed:
            cp.wait_send()
        for cp in started:
            cp.wait()

    any_spec = pl.BlockSpec(memory_space=pl.ANY)
    outs = _call(
        body, name=name, in_specs=[any_spec] * n, out_specs=[any_spec] * n,
        out_shape=[_sds((N_DEV,) + a.shape, a.dtype) for a in arrs],
        scratch_shapes=[pltpu.SemaphoreType.DMA((n, 7)), pltpu.SemaphoreType.DMA((n, 7)), pltpu.SemaphoreType.DMA((n,))],
        has_side_effects=True,
    )(*arrs)
    return list(outs)


HBM_SPEC = pl.BlockSpec(memory_space=pltpu.HBM)
SEM_SPEC = pl.BlockSpec(memory_space=pltpu.SEMAPHORE)
DATAFLOW = pltpu.SideEffectType.DATAFLOW_SIDE_EFFECTING
PEERS = [(dx, dy, dc) for dx in (0, 1) for dy in (0, 1) for dc in (0, 1)][1:]


def _peer(flip):
    x, y, c = _mesh_pos()
    return tuple(1 - v if f else v for v, f in zip((x, y, c), flip))


def _slot(pos):
    return 4 * pos[0] + 2 * pos[1] + pos[2]


def _in_hbm(a):
    return pltpu.with_memory_space_constraint(a, pltpu.HBM)


NEAR = [(0, 0, 1), (1, 0, 0), (0, 1, 0), (1, 1, 0)]


def _direct_copies(srcs, lands, send_sems, recv_sems, mode):
    me = _slot(_mesh_pos())
    copies = []
    for a in range(len(lands)):
        if mode == "scatter":
            plan = [(srcs[a].at[_slot(_peer(flip))], lands[a].at[me], _peer(flip)) for flip in PEERS]
        elif mode == "gather_near":
            plan = [(lands[a].at[me], lands[a].at[me], _peer(flip)) for flip in NEAR]
        else:
            origins = [_slot(_peer(flip)) for flip in NEAR[1:]]
            plan = [(lands[a].at[o], lands[a].at[o], _peer(NEAR[0])) for o in origins]
        for k, (src, dst, to) in enumerate(plan):
            idx = a * len(PEERS) + k
            copies.append(pltpu.make_async_remote_copy(
                src_ref=src, dst_ref=dst, send_sem=send_sems.at[idx], recv_sem=recv_sems.at[idx],
                device_id=to, device_id_type=MESH))
    return copies


def _send_start(srcs, lands, after, name, mode):
    ns, nl = len(srcs), len(lands)

    def body(*refs):
        src_refs, land_refs = refs[:ns], refs[ns:ns + nl]
        send_sems, recv_sems = refs[ns + nl + 1:ns + nl + 3]
        token = refs[-1]
        for cp in _direct_copies(src_refs, land_refs, send_sems, recv_sems, mode):
            cp.start()
        token[...] = jnp.zeros_like(token)

    sem = pltpu.SemaphoreType.DMA((nl * len(PEERS),))
    outs = pl.pallas_call(
        body, name=name,
        out_shape=(sem, sem) + tuple(pltpu.HBM(a.shape, a.dtype) for a in list(srcs) + list(lands))
        + (_sds((8, LANES), F32),),
        in_specs=[HBM_SPEC] * (ns + nl) + [pl.BlockSpec(memory_space=pl.ANY)],
        out_specs=(SEM_SPEC, SEM_SPEC) + (HBM_SPEC,) * (ns + nl) + (pl.BlockSpec(memory_space=pltpu.VMEM),),
        input_output_aliases={i: 2 + i for i in range(ns + nl)},
        compiler_params=pltpu.CompilerParams(has_side_effects=DATAFLOW),
    )(*[_in_hbm(a) for a in list(srcs) + list(lands)], after)
    send_sems, recv_sems = outs[0], outs[1]
    return send_sems, recv_sems, list(outs[2:2 + ns]), list(outs[2 + ns:2 + ns + nl]), outs[-1]


def _send_wait(send_sems, recv_sems, srcs, lands, after, name, mode):
    ns, nl = len(srcs), len(lands)

    def body(*refs):
        src_refs, land_refs = refs[:ns], refs[ns:ns + nl]
        send_sems, recv_sems = refs[ns + nl:ns + nl + 2]
        copies = _direct_copies(src_refs, land_refs, send_sems, recv_sems, mode)
        for cp in copies:
            cp.wait_send()
        for cp in copies:
            cp.wait_recv()

    outs = pl.pallas_call(
        body, name=name,
        out_shape=tuple(pltpu.HBM(a.shape, a.dtype) for a in list(srcs) + list(lands)),
        in_specs=[HBM_SPEC] * (ns + nl) + [SEM_SPEC, SEM_SPEC, pl.BlockSpec(memory_space=pl.ANY)],
        out_specs=(HBM_SPEC,) * (ns + nl),
        input_output_aliases={i: i for i in range(ns + nl)},
        compiler_params=pltpu.CompilerParams(has_side_effects=DATAFLOW),
    )(*srcs, *lands, send_sems, recv_sems, after)
    return list(outs[:ns]), list(outs[ns:])


def _row_tile(rows, cap=512):
    t = min(rows, cap)
    while rows % t or (t % 16 and t != rows):
        t -= 1
    return t


def _as2d(a):
    return a.reshape(-1, a.shape[-1])


def _cast_layer(w, layer, slot, name):
    _, rows, cols = w.shape
    tr = _row_tile(rows)

    def body(*refs):
        refs[-1][...] = refs[-2][...].astype(BF16)

    if slot is None:
        return _call(body, name=name, grid=(rows // tr,),
                     in_specs=[pl.BlockSpec((None, tr, cols), lambda i: (layer, i, 0))],
                     out_specs=pl.BlockSpec((tr, cols), lambda i: (i, 0)), out_shape=_sds((rows, cols), BF16))(w)
    return _call(body, name=name, grid=(rows // tr,), prefetch=True,
                 in_specs=[pl.BlockSpec((None, tr, cols), lambda i, s: (layer, i, 0))],
                 out_specs=pl.BlockSpec((None, tr, cols), lambda i, s: (s[0], i, 0)),
                 out_shape=_sds((N_DEV, rows, cols), BF16))(slot, w)


def _sum_adamw(slot, parts, lands, w, m, v, name):
    n_l = len(parts)
    _, rows, cols = lands[0].shape
    tr = _row_tile(rows, 128)

    def body(s_ref, *refs):
        p_refs, l_refs = refs[:n_l], refs[n_l:2 * n_l]
        w_ref, m_ref, v_ref, g_ref, d_ref, nm_ref, nv_ref = refs[2 * n_l:]
        for k in range(n_l):
            @pl.when(pl.program_id(0) == k)
            def _(k=k):
                own = p_refs[k][...]
                g = jnp.zeros((tr, cols), F32)
                for j in range(N_DEV):
                    g = g + jnp.where(s_ref[0] == j, own, l_refs[k][j]).astype(F32)
                delta, nm, nv = _adamw_math(w_ref[...], g, m_ref[...], v_ref[...])
                g_ref[...] = g
                d_ref[...] = delta
                nm_ref[...] = nm
                nv_ref[...] = nv

    def own_block(k):
        return pl.BlockSpec((None, tr, cols), lambda l, i, s: (s[0], jnp.where(l == k, i, 0), 0))

    def zone_block(k):
        return pl.BlockSpec((N_DEV, tr, cols), lambda l, i, s: (0, jnp.where(l == k, i, 0), 0))

    lay = pl.BlockSpec((None, tr, cols), lambda l, i, s: (l, i, 0))
    return _call(body, name=name, grid=(n_l, rows // tr), prefetch=True,
                 in_specs=[own_block(k) for k in range(n_l)] + [zone_block(k) for k in range(n_l)] + [lay, lay, lay],
                 out_specs=[lay] * 4, out_shape=[_sds((n_l, rows, cols), F32)] * 4)(slot, *parts, *lands, w, m, v)


def _adamw_math(w, g, m, v):
    m = ADAM_B1 * m + (1.0 - ADAM_B1) * g
    v = ADAM_B2 * v + (1.0 - ADAM_B2) * (g * g)
    m_hat = m / (1.0 - ADAM_B1 ** ADAM_STEP)
    v_hat = v / (1.0 - ADAM_B2 ** ADAM_STEP)
    delta = -ADAM_LR * (m_hat / (jnp.sqrt(v_hat) + ADAM_EPS) + ADAM_WD * w)
    return delta, m, v


def _small_adamw(gathered, w, m, v, name):
    def body(a_ref, w_ref, m_ref, v_ref, g_ref, d_ref, nm_ref, nv_ref):
        g = a_ref[0]
        for k in range(1, N_DEV):
            g = g + a_ref[k]
        delta, nm, nv = _adamw_math(w_ref[...], g, m_ref[...], v_ref[...])
        g_ref[...] = g
        d_ref[...] = delta
        nm_ref[...] = nm
        nv_ref[...] = nv

    return _call(body, name=name, out_shape=[_sds(w.shape, F32)] * 4)(gathered, w, m, v)


def _norm_matmul_cols(x, g, wg, layer, mode, name):
    t, d = x.shape
    nb = wg.shape[-1]
    tm = BIG_ROW_TILE
    per = 2
    pieces = per * nb // LANES

    def body(x_ref, g_ref, w_ref, y_ref, xnt_ref):
        xn = _rms(x_ref[...], g_ref[...])
        xnt_ref[...] = xn.T.astype(BF16)
        xn = xn.astype(BF16)
        for j in range(N_DEV // per):
            y = _dot(xn, jnp.concatenate([w_ref[per * j + k] for k in range(per)], axis=1))
            if mode == "heads":
                for p in range(pieces):
                    y_ref[pieces * j + p] = y[:, p * LANES:(p + 1) * LANES]
            else:
                y_ref[:, j * per * nb:(j + 1) * per * nb] = y.astype(BF16)

    if mode == "cols":
        y_shape, y_spec = _sds((t, N_DEV * nb), BF16), pl.BlockSpec((tm, N_DEV * nb), lambda i: (i, 0))
    else:
        y_shape = _sds((N_DEV // per * pieces, t, LANES), F32)
        y_spec = pl.BlockSpec((N_DEV // per * pieces, tm, LANES), lambda i: (0, i, 0))
    return _call(
        body, name=name, grid=(t // tm,),
        in_specs=[pl.BlockSpec((tm, d), lambda i: (i, 0)), pl.BlockSpec((1, d), lambda i: (0, 0)),
                  pl.BlockSpec((N_DEV, None, d, nb), lambda i: (0, layer, 0, 0))],
        out_specs=[y_spec, pl.BlockSpec((d, tm), lambda i: (0, i))],
        out_shape=[y_shape, _sds((d, t), BF16)])(x, g, wg)


def _ffn_in_swiglu(x, g, wg, layer, name):
    t, d = x.shape
    fc = wg.shape[-2]
    tm = ROW_TILE

    def body(x_ref, g_ref, w_ref, gate_ref, up_ref, a_ref, xn_ref):
        xn = _rms(x_ref[...], g_ref[...]).astype(BF16)
        xn_ref[...] = xn
        for c in range(4):
            gate, up = _dot_nt(xn, w_ref[c]), _dot_nt(xn, w_ref[c + 4])
            gate_ref[c] = gate.astype(BF16)
            up_ref[c] = up.astype(BF16)
            a_ref[c] = (gate * jax.nn.sigmoid(gate) * up).astype(BF16)

    chunks = pl.BlockSpec((4, tm, fc), lambda i: (0, i, 0))
    return _call(
        body, name=name, grid=(t // tm,),
        in_specs=[pl.BlockSpec((tm, d), lambda i: (i, 0)), pl.BlockSpec((1, d), lambda i: (0, 0)),
                  pl.BlockSpec((N_DEV, None, fc, d), lambda i: (0, layer, 0, 0))],
        out_specs=[chunks, chunks, chunks, pl.BlockSpec((tm, d), lambda i: (i, 0))],
        out_shape=[_sds((4, t, fc), BF16)] * 3 + [_sds((t, d), BF16)])(x, g, wg)


def _norm_matmul_heads(x, g, wg, layer, scale, name):
    t, d = x.shape
    tm = ROW_TILE
    hp = d // LANES

    def body(x_ref, g_ref, w_ref, y_ref, xn_ref):
        xn = _rms(x_ref[...], g_ref[...]).astype(BF16)
        xn_ref[...] = xn
        y = _dot(xn, w_ref[...].reshape(d, d)) * scale
        for p in range(hp):
            y_ref[p] = y[:, p * LANES:(p + 1) * LANES]

    return _call(
        body, name=name, grid=(t // tm,),
        in_specs=[pl.BlockSpec((tm, d), lambda i: (i, 0)), pl.BlockSpec((1, d), lambda i: (0, 0)),
                  pl.BlockSpec((N_DEV, None, d // N_DEV, d), lambda i: (0, layer, 0, 0))],
        out_specs=[pl.BlockSpec((hp, tm, LANES), lambda i: (0, i, 0)), pl.BlockSpec((tm, d), lambda i: (i, 0))],
        out_shape=[_sds((hp, t, LANES), F32), _sds((t, d), BF16)])(x, g, wg)


def _shift_down(u, halo, k, tm):
    row = lax.broadcasted_iota(jnp.int32, u.shape, 0)
    out = pltpu.roll(u, k, 0)
    for j in range(k):
        out = jnp.where(row == j, halo[halo.shape[0] - k + j:halo.shape[0] - k + j + 1, :], out)
    return out


def _shift_up(u, halo, k, tm):
    row = lax.broadcasted_iota(jnp.int32, u.shape, 0)
    out = pltpu.roll(u, tm - k, 0)
    for j in range(k):
        out = jnp.where(row == tm - k + j, halo[j:j + 1, :], out)
    return out


HALO = 16


def _conv_fwd(p, cw, name):
    t, d3 = p.shape
    d = d3 // 3
    tm = ROW_TILE
    hb = tm // HALO

    def body(p_ref, prev_ref, cw_ref, z_ref):
        i = pl.program_id(0)
        b = p_ref[:, 0:d].astype(F32)
        u = p_ref[:, d:2 * d].astype(F32) * p_ref[:, 2 * d:3 * d].astype(F32)
        keep = (i > 0).astype(F32)
        hu = prev_ref[:, d:2 * d].astype(F32) * prev_ref[:, 2 * d:3 * d].astype(F32) * keep
        uc = cw_ref[2:3, :] * u + cw_ref[1:2, :] * _shift_down(u, hu, 1, tm) + cw_ref[0:1, :] * _shift_down(u, hu, 2, tm)
        z_ref[...] = (b * uc).astype(BF16)

    return _call(
        body, name=name, grid=(t // tm,),
        in_specs=[pl.BlockSpec((tm, d3), lambda i: (i, 0)),
                  pl.BlockSpec((HALO, d3), lambda i: (jnp.maximum(i * hb - 1, 0), 0)),
                  pl.BlockSpec((3, d), lambda i: (0, 0))],
        out_specs=pl.BlockSpec((tm, d), lambda i: (i, 0)), out_shape=_sds((t, d), BF16))(p, p, cw)


def _matmul_norm_residual(a3, wg, layer, g, x_res, name):
    kc_n, t, kc = a3.shape
    d = wg.shape[-1]
    per = N_DEV // kc_n
    rows = wg.shape[2]
    tm = ROW_TILE

    def body(a_ref, w_ref, g_ref, x_ref, raw_ref, xo_ref):
        raw = None
        for c in range(kc_n):
            term = _dot(a_ref[c], w_ref[c * per:(c + 1) * per].reshape(per * rows, d))
            raw = term if raw is None else raw + term
        raw_ref[...] = raw
        xo_ref[...] = x_ref[...] + _rms(raw, g_ref[...])

    row_spec = pl.BlockSpec((tm, d), lambda i: (i, 0))
    return _call(
        body, name=name, grid=(t // tm,),
        in_specs=[pl.BlockSpec((kc_n, tm, kc), lambda i: (0, i, 0)),
                  pl.BlockSpec((N_DEV, None, rows, d), lambda i: (0, layer, 0, 0)),
                  pl.BlockSpec((1, d), lambda i: (0, 0)), row_spec],
        out_specs=[row_spec, row_spec], out_shape=[_sds((t, d), F32)] * 2)(a3, wg, g, x_res)


def _alibi_slopes(n_heads):
    hh = np.arange(n_heads, dtype=np.float32) + 1.0
    s = np.power(2.0, -8.0 * hh / n_heads).astype(np.float32)
    return jnp.asarray(np.repeat(s.reshape(n_heads // 2, 2, 1), 2 * ATT_BLOCK, axis=2))


def _band_bias(sl_ref, dil):
    u = lax.broadcasted_iota(jnp.int32, (ATT_BLOCK, 2 * ATT_BLOCK), 0)
    kk = lax.broadcasted_iota(jnp.int32, (ATT_BLOCK, 2 * ATT_BLOCK), 1)
    delta = u + ATT_BLOCK - kk
    valid = (delta >= 0) & (delta <= ATT_BLOCK)
    dist = (delta * dil).astype(F32)
    rows = [jnp.where(valid, -sl_ref[hd:hd + 1, :] * dist, NEG) for hd in range(2)]
    return jnp.concatenate(rows, axis=0)


def _stack_heads(a):
    lane = lax.broadcasted_iota(jnp.int32, a.shape, 1)
    return jnp.concatenate([jnp.where(lane < HEAD_DIM, a, 0.0), jnp.where(lane >= HEAD_DIM, a, 0.0)], axis=0).astype(BF16)


def _unstack_heads(a2):
    top, bot = a2[:ATT_BLOCK], a2[ATT_BLOCK:]
    lane = lax.broadcasted_iota(jnp.int32, top.shape, 1)
    return jnp.where(lane < HEAD_DIM, top, bot)


def _rows_to_lanes(a0, a1):
    eye = lax.broadcasted_iota(jnp.int32, a0.shape, 0) == lax.broadcasted_iota(jnp.int32, a0.shape, 1)
    return jnp.concatenate([jnp.sum(jnp.where(eye, a, 0.0), axis=0, keepdims=True) for a in (a0, a1)], axis=1)


def _fill_bias_t(sl_ref, bias_ref):
    kk = lax.broadcasted_iota(jnp.int32, (2 * ATT_BLOCK, 2 * ATT_BLOCK), 0)
    lane = lax.broadcasted_iota(jnp.int32, (2 * ATT_BLOCK, 2 * ATT_BLOCK), 1)
    delta = lane % ATT_BLOCK + ATT_BLOCK - kk
    valid = (delta >= 0) & (delta <= ATT_BLOCK)
    slope = jnp.concatenate([sl_ref[0:1, :ATT_BLOCK], sl_ref[1:2, :ATT_BLOCK]], axis=1)
    for gi, dil in enumerate(DILATIONS):
        bias = jnp.where(valid, -slope * (delta * dil).astype(F32), NEG)
        bias_ref[2 * gi] = bias
        bias_ref[2 * gi + 1] = jnp.where(kk < ATT_BLOCK, NEG, bias)


def _fill_bias(sl_ref, bias_ref):
    kk = lax.broadcasted_iota(jnp.int32, (2 * ATT_BLOCK, 2 * ATT_BLOCK), 1)
    for gi, dil in enumerate(DILATIONS):
        bias = _band_bias(sl_ref, dil)
        bias_ref[2 * gi] = bias
        bias_ref[2 * gi + 1] = jnp.where(kk < ATT_BLOCK, NEG, bias)


def _strided_keys(dil, r, b, kc_ref, kp_ref, vc_ref, vp_ref, kcar_ref, vcar_ref):
    if b > 0:
        keys = pl.ds((b - 1) * (ATT_BLOCK * dil) + r, 2 * ATT_BLOCK, stride=dil)
        return kc_ref[keys, :].astype(BF16), vc_ref[keys, :].astype(BF16)
    own = pl.ds(r, ATT_BLOCK, stride=dil)
    k_own, v_own = kc_ref[own, :].astype(BF16), vc_ref[own, :].astype(BF16)
    if dil * ATT_BLOCK != SUPER:
        before = pl.ds(SUPER - ATT_BLOCK * dil + r, ATT_BLOCK, stride=dil)
        k_before, v_before = kp_ref[before, :].astype(BF16), vp_ref[before, :].astype(BF16)
    else:
        k_before, v_before = kcar_ref[r], vcar_ref[r]
        kcar_ref[r] = k_own
        vcar_ref[r] = v_own
    return jnp.concatenate([k_before, k_own], axis=0), jnp.concatenate([v_before, v_own], axis=0)


def _attention_fwd(q, kv, slopes, name):
    hp, t, _ = q.shape
    ns = t // SUPER
    nd = len(DILATIONS)

    def body(sl_ref, q_ref, kc_ref, kp_ref, vc_ref, vp_ref, o_ref, lse_ref, og_ref, lg_ref, bias_ref, kcar_ref, vcar_ref):
        n = pl.program_id(1)

        @pl.when(n == 0)
        def _():
            _fill_bias(sl_ref, bias_ref)
            kcar_ref[...] = jnp.zeros_like(kcar_ref)
            vcar_ref[...] = jnp.zeros_like(vcar_ref)

        for gi, dil in enumerate(DILATIONS):
            for idx in range(SUPER // ATT_BLOCK):
                r, b = idx % dil, idx // dil
                qs = b * (ATT_BLOCK * dil) + r
                first = (n == 0).astype(jnp.int32) if b == 0 else 0
                q2 = _stack_heads(q_ref[pl.ds(qs, ATT_BLOCK, stride=dil), :])
                kb, vb = _strided_keys(dil, r, b, kc_ref, kp_ref, vc_ref, vp_ref, kcar_ref, vcar_ref)
                s = _dot_nt(q2, kb) + bias_ref[2 * gi + first]
                m = jnp.max(s, axis=-1, keepdims=True)
                p = jnp.exp(s - m).astype(BF16)
                ol = _dot(p, jnp.concatenate([vb, jnp.ones_like(vb)], axis=1))
                l = ol[:, LANES:]
                o2 = ol[:, :LANES] / l
                lse2 = m + jnp.log(l)
                og_ref[gi, pl.ds(qs, ATT_BLOCK, stride=dil), :] = _unstack_heads(o2)
                lg_ref[gi, pl.ds(qs, ATT_BLOCK, stride=dil), :] = _unstack_heads(lse2)
        lg = [lg_ref[gi] for gi in range(nd)]
        top = functools.reduce(jnp.maximum, lg)
        ws = [jnp.exp(x - top) for x in lg]
        tot = functools.reduce(jnp.add, ws)
        lse_ref[...] = top + jnp.log(tot)
        acc = ws[0] * og_ref[0]
        for gi in range(1, nd):
            acc = acc + ws[gi] * og_ref[gi]
        o_ref[...] = (acc / tot).astype(BF16)

    cur = lambda off: pl.BlockSpec((None, SUPER, LANES), lambda h, n: (h + off, n, 0))
    prev = lambda off: pl.BlockSpec((None, SUPER, LANES), lambda h, n: (h + off, jnp.maximum(n - 1, 0), 0))
    return _call(
        body, name=name, grid=(hp, ns),
        in_specs=[pl.BlockSpec((None, 2, 2 * ATT_BLOCK), lambda h, n: (h, 0, 0)), cur(0), cur(0), prev(0), cur(hp), prev(hp)],
        out_specs=[pl.BlockSpec((SUPER, LANES), lambda h, n: (n, h)), cur(0)],
        out_shape=[_sds((t, hp * LANES), BF16), _sds((hp, t, LANES), F32)],
        scratch_shapes=[pltpu.VMEM((nd, SUPER, LANES), F32), pltpu.VMEM((nd, SUPER, LANES), F32),
                        pltpu.VMEM((2 * nd, 2 * ATT_BLOCK, 2 * ATT_BLOCK), F32)] + [
                            pltpu.VMEM((DILATIONS[-1], ATT_BLOCK, LANES), BF16)] * 2,
    )(slopes, q, kv, kv, kv, kv)


def _attention_bwd(q, kv, o, lse, d_o, dk_in, dv_in, slopes, q_scale, name):
    hp, t, _ = q.shape
    ns = t // SUPER
    shared = dk_in is not None

    def body(sl_ref, q_ref, kc_ref, kp_ref, vc_ref, vp_ref, o_ref, lse_ref, do_ref, *rest):
        dki_ref, dvi_ref = rest[:2] if shared else (None, None)
        dq_ref, dk_ref, dv_ref, dkw_ref, dvw_ref, st_ref, bias_ref, kcar_ref, vcar_ref = rest[2 if shared else 0:]
        n = pl.program_id(1)

        @pl.when(n == 0)
        def _():
            dkw_ref[...] = jnp.zeros_like(dkw_ref)
            dvw_ref[...] = jnp.zeros_like(dvw_ref)

        @pl.when(n > 0)
        def _():
            dkw_ref[0:SUPER, :] = dkw_ref[SUPER:, :]
            dvw_ref[0:SUPER, :] = dvw_ref[SUPER:, :]
            dkw_ref[SUPER:, :] = jnp.zeros((SUPER, LANES), F32)
            dvw_ref[SUPER:, :] = jnp.zeros((SUPER, LANES), F32)

        @pl.when(n < ns)
        def _():
            prod = do_ref[...] * o_ref[...].astype(F32)
            lane = lax.broadcasted_iota(jnp.int32, prod.shape, 1)
            zero = jnp.zeros((SUPER, LANES), F32)
            st_ref[0] = zero + jnp.sum(jnp.where(lane < HEAD_DIM, prod, 0.0), axis=-1, keepdims=True)
            st_ref[1] = zero + jnp.sum(jnp.where(lane >= HEAD_DIM, prod, 0.0), axis=-1, keepdims=True)
            lse = lse_ref[...]
            swapped = pltpu.roll(lse, HEAD_DIM, 1)
            st_ref[2] = jnp.where(lane < HEAD_DIM, lse, swapped)
            st_ref[3] = jnp.where(lane >= HEAD_DIM, lse, swapped)
            dq_ref[...] = jnp.zeros_like(dq_ref)

            @pl.when(n == 0)
            def _():
                _fill_bias_t(sl_ref, bias_ref)
                kcar_ref[...] = jnp.zeros_like(kcar_ref)
                vcar_ref[...] = jnp.zeros_like(vcar_ref)

            for gi, dil in enumerate(DILATIONS):
                for idx in range(SUPER // ATT_BLOCK):
                    r, b = idx % dil, idx // dil
                    qs = b * (ATT_BLOCK * dil) + r
                    ks = SUPER + (b - 1) * (ATT_BLOCK * dil) + r
                    first = (n == 0).astype(jnp.int32) if b == 0 else 0
                    rows = pl.ds(qs, ATT_BLOCK, stride=dil)
                    keys = pl.ds(ks, 2 * ATT_BLOCK, stride=dil)
                    q2 = _stack_heads(q_ref[rows, :])
                    do2 = _stack_heads(do_ref[rows, :])
                    kb, vb = _strided_keys(dil, r, b, kc_ref, kp_ref, vc_ref, vp_ref, kcar_ref, vcar_ref)
                    dd = _rows_to_lanes(st_ref[0, rows, :], st_ref[1, rows, :])
                    lse_b = _rows_to_lanes(st_ref[2, rows, :], st_ref[3, rows, :])
                    ps, dss = [], []
                    for half in range(2):
                        hk = slice(half * ATT_BLOCK, (half + 1) * ATT_BLOCK)
                        p = jnp.exp(_dot_nt(kb[hk], q2) + bias_ref[2 * gi + first, hk, :] - lse_b)
                        dss.append((p * (_dot_nt(vb[hk], do2) - dd)).astype(BF16))
                        ps.append(p.astype(BF16))
                    p, ds = jnp.concatenate(ps, axis=0), jnp.concatenate(dss, axis=0)
                    dvw_ref[keys, :] += _dot(p, do2)
                    dkw_ref[keys, :] += _dot(ds, q2)
                    dq_ref[rows, :] += _unstack_heads(_dot_tn(ds, kb)) * q_scale

        dk_ref[...] =dkw_ref[0:SUPER, :] + dki_ref[...] if shared else dkw_ref[0:SUPER, :]
        dv_ref[...] = dvw_ref[0:SUPER, :] + dvi_ref[...] if shared else dvw_ref[0:SUPER, :]

    last = ns - 1
    cur = lambda off: pl.BlockSpec((None, SUPER, LANES), lambda h, n: (h + off, jnp.minimum(n, last), 0))
    prev = lambda off: pl.BlockSpec((None, SUPER, LANES), lambda h, n: (h + off, jnp.clip(n - 1, 0, last), 0))
    nat = pl.BlockSpec((SUPER, LANES), lambda h, n: (jnp.minimum(n, last), h))
    late = pl.BlockSpec((None, SUPER, LANES), lambda h, n: (h, jnp.maximum(n - 1, 0), 0))
    dq, dk, dv = _call(
        body, name=name, grid=(hp, ns + 1),
        in_specs=[pl.BlockSpec((None, 2, 2 * ATT_BLOCK), lambda h, n: (h, 0, 0)), cur(0), cur(0), prev(0), cur(hp), prev(hp),
                  nat, cur(0), nat] + ([late, late] if shared else []),
        out_specs=[cur(0), late, late],
        out_shape=[_sds((hp, t, LANES), F32)] * 3,
        scratch_shapes=[pltpu.VMEM((2 * SUPER, LANES), F32)] * 2 + [
            pltpu.VMEM((4, SUPER, LANES), F32), pltpu.VMEM((2 * len(DILATIONS), 2 * ATT_BLOCK, 2 * ATT_BLOCK), F32)] + [
                pltpu.VMEM((DILATIONS[-1], ATT_BLOCK, LANES), BF16)] * 2,
    )(slopes, q, kv, kv, kv, kv, o, lse, d_o, *((dk_in, dv_in) if shared else ()))
    return dq, dk, dv


def _loss_head(y, target, raw, g, name):
    t, d = y.shape
    tm = ROW_TILE

    def body(y_ref, t_ref, raw_ref, g_ref, sq_ref, dy_ref, draw_ref, dg_ref):
        i = pl.program_id(0)
        err = y_ref[...] - t_ref[...]
        dy = err * (1.0 / d)
        dy_ref[...] = dy
        draw, dg = _rms_bwd(raw_ref[...], g_ref[...], dy)
        draw_ref[...] = draw.astype(BF16)
        sq = jnp.zeros((8, LANES), F32) + jnp.sum(err * err)

        @pl.when(i == 0)
        def _():
            sq_ref[...] = sq
            dg_ref[...] = dg

        @pl.when(i > 0)
        def _():
            sq_ref[...] += sq
            dg_ref[...] += dg

    row = pl.BlockSpec((tm, d), lambda i: (i, 0))
    vec = pl.BlockSpec((1, d), lambda i: (0, 0))
    return _call(
        body, name=name, grid=(t // tm,), in_specs=[row, row, row, vec],
        out_specs=[pl.BlockSpec((8, LANES), lambda i: (0, 0)), row, row, vec],
        out_shape=[_sds((8, LANES), F32), _sds((t, d), F32), _sds((t, d), BF16), _sds((1, d), F32)])(y, target, raw, g)


def _bwd_matmul_norms(a_specs, a_args, a_tile, n_steps, w_spec, w_arg, w_mat, xa, ga, resid, xb, gb, name,
                      w_transposed=False):
    t, d = xa.shape
    tm = BWD_ROW_TILE
    na = len(a_specs)
    second = xb is not None
    per = 8 if n_steps % 8 == 0 else 1
    n_steps //= per

    def blocks_of(spec, k):
        return pl.BlockSpec(spec.block_shape, lambda i, j: spec.index_map(i, per * j + k))

    def body(*refs):
        a_refs, w_refs = refs[:per * na], refs[per * na:per * na + per]
        xa_ref, ga_ref, res_ref = refs[per * na + per:per * na + per + 3]
        rest = refs[per * na + per + 3:]
        if second:
            xb_ref, gb_ref, dx_ref, d2_ref, dga_ref, dgb_ref, acc_ref = rest
        else:
            dx_ref, dga_ref, acc_ref = rest
        i, j = pl.program_id(0), pl.program_id(1)
        part = None
        for k in range(per):
            term = (_dot if w_transposed else _dot_nt)(a_tile(per * j + k, *a_refs[k * na:(k + 1) * na]), w_mat(w_refs[k]))
            part = term if part is None else part + term

        @pl.when(j == 0)
        def _():
            acc_ref[...] = part

        @pl.when(j > 0)
        def _():
            acc_ref[...] += part

        @pl.when(j == n_steps - 1)
        def _():
            da, dga = _rms_bwd(xa_ref[...], ga_ref[...], acc_ref[...])
            dx = res_ref[...] + da
            dx_ref[...] = dx
            if second:
                d2, dgb = _rms_bwd(xb_ref[...], gb_ref[...], dx)
                d2_ref[...] = d2.astype(BF16)

            @pl.when(i == 0)
            def _():
                dga_ref[...] = dga
                if second:
                    dgb_ref[...] = dgb

            @pl.when(i > 0)
            def _():
                dga_ref[...] += dga
                if second:
                    dgb_ref[...] += dgb

    row = pl.BlockSpec((tm, d), lambda i, j: (i, 0))
    vec = pl.BlockSpec((1, d), lambda i, j: (0, 0))
    in_specs = [blocks_of(sp, k) for k in range(per) for sp in a_specs] + [blocks_of(w_spec, k) for k in range(per)]
    in_specs += [row, vec, row]
    args = list(a_args) * per + [w_arg] * per + [xa, ga, resid]
    if second:
        in_specs += [row, vec]
        args += [xb, gb]
        out_specs = [row, row, vec, vec]
        out_shape = [_sds((t, d), F32), _sds((t, d), BF16), _sds((1, d), F32), _sds((1, d), F32)]
    else:
        out_specs = [row, vec]
        out_shape = [_sds((t, d), F32), _sds((1, d), F32)]
    return _call(body, name=name, grid=(t // tm, n_steps), in_specs=in_specs, out_specs=out_specs,
                 out_shape=out_shape, scratch_shapes=[pltpu.VMEM((tm, d), F32)])(*args)


def _heads_to_rows(*refs):
    hp = refs[0].shape[0]
    cols = []
    for p in range(hp):
        v = refs[0][p]
        for r in refs[1:]:
            v = v + r[p]
        cols.append(v)
    return jnp.concatenate(cols, axis=-1).astype(BF16)


def _matmul_nt_rows(a, wg, layer, out_dtype, name):
    t, d = a.shape
    tm = ROW_TILE

    def body(a_ref, w_ref, o_ref):
        o_ref[...] = _dot_nt(a_ref[...], w_ref[...].reshape(d, d)).astype(out_dtype)

    row = pl.BlockSpec((tm, d), lambda i: (i, 0))
    return _call(body, name=name, grid=(t // tm,),
                 in_specs=[row, pl.BlockSpec((N_DEV, None, d // N_DEV, d), lambda i: (0, layer, 0, 0))],
                 out_specs=row, out_shape=_sds((t, d), out_dtype))(a, wg)


def _swiglu_bwd(d_ff, wg, layer, gate, up, name):
    t, d = d_ff.shape
    fc = gate.shape[-1]
    rows = wg.shape[2]
    tm = BIG_ROW_TILE

    def body(df_ref, w_ref, g_ref, u_ref, dh_ref):
        w = w_ref[...].reshape(2 * rows, d)
        for r0 in range(0, tm, SWIGLU_ROWS):
            rs = slice(r0, r0 + SWIGLU_ROWS)
            da = _dot_nt(df_ref[rs, :], w)
            gate, up = g_ref[rs, :].astype(F32), u_ref[rs, :].astype(F32)
            sig = jax.nn.sigmoid(gate)
            dh_ref[0, rs, :] = (da * up * (sig * (1.0 + gate * (1.0 - sig)))).astype(BF16)
            dh_ref[1, rs, :] = (da * (gate * sig)).astype(BF16)

    return _call(
        body, name=name, grid=(t // tm, 4),
        in_specs=[pl.BlockSpec((tm, d), lambda i, c: (i, 0)),
                  pl.BlockSpec((2, None, rows, d), lambda i, c: (c, layer, 0, 0)),
                  pl.BlockSpec((None, tm, fc), lambda i, c: (c, i, 0)),
                  pl.BlockSpec((None, tm, fc), lambda i, c: (c, i, 0))],
        out_specs=pl.BlockSpec((None, 2, tm, fc), lambda i, c: (c, 0, i, 0)),
        out_shape=_sds((4, 2, t, fc), BF16))(d_ff, wg, gate, up)


def _conv_bwd(p, d_z, cw, name):
    t, d3 = p.shape
    d = d3 // 3
    tm = ROW_TILE
    hb = tm // HALO
    nt = t // tm

    def body(p_ref, prev_ref, next_ref, dz_ref, dzn_ref, cw_ref, dp_ref, dcw_ref):
        i = pl.program_id(0)
        b = p_ref[:, 0:d].astype(F32)
        c = p_ref[:, d:2 * d].astype(F32)
        h = p_ref[:, 2 * d:3 * d].astype(F32)
        u = c * h
        hu = prev_ref[:, d:2 * d].astype(F32) * prev_ref[:, 2 * d:3 * d].astype(F32) * (i > 0).astype(F32)
        u1, u2 = _shift_down(u, hu, 1, tm), _shift_down(u, hu, 2, tm)
        uc = cw_ref[2:3, :] * u + cw_ref[1:2, :] * u1 + cw_ref[0:1, :] * u2
        dz = dz_ref[...].astype(F32)
        duc = dz * b
        dn = dzn_ref[...].astype(F32) * next_ref[:, 0:d].astype(F32) * (i < nt - 1).astype(F32)
        du = cw_ref[2:3, :] * duc + cw_ref[1:2, :] * _shift_up(duc, dn, 1, tm) + cw_ref[0:1, :] * _shift_up(duc, dn, 2, tm)
        dp_ref[:, 0:d] = (dz * uc).astype(BF16)
        dp_ref[:, d:2 * d] = (du * h).astype(BF16)
        dp_ref[:, 2 * d:3 * d] = (du * c).astype(BF16)
        dcw = jnp.concatenate([jnp.sum(duc * u2, axis=0, keepdims=True), jnp.sum(duc * u1, axis=0, keepdims=True),
                               jnp.sum(duc * u, axis=0, keepdims=True)], axis=0)

        @pl.when(i == 0)
        def _():
            dcw_ref[...] = dcw

        @pl.when(i > 0)
        def _():
            dcw_ref[...] += dcw

    last_halo = t // HALO - 1
    return _call(
        body, name=name, grid=(nt,),
        in_specs=[pl.BlockSpec((tm, d3), lambda i: (i, 0)),
                  pl.BlockSpec((HALO, d3), lambda i: (jnp.maximum(i * hb - 1, 0), 0)),
                  pl.BlockSpec((HALO, d3), lambda i: (jnp.minimum((i + 1) * hb, last_halo), 0)),
                  pl.BlockSpec((tm, d), lambda i: (i, 0)),
                  pl.BlockSpec((HALO, d), lambda i: (jnp.minimum((i + 1) * hb, last_halo), 0)),
                  pl.BlockSpec((3, d), lambda i: (0, 0))],
        out_specs=[pl.BlockSpec((tm, d3), lambda i: (i, 0)), pl.BlockSpec((3, d), lambda i: (0, 0))],
        out_shape=[_sds((t, d3), BF16), _sds((3, d), F32)])(p, p, p, d_z, d_z, cw)


def _grad_weight(a_specs, a_args, a_tile, b_specs, b_args, b_tile, n_out, acc_shape, out_spec, out_shape, t, name,
                 a_transposed=False, tt=GRAD_ROW_TILE):
    na, nb = len(a_specs), len(b_specs)

    def body(*refs):
        a_refs, b_refs = refs[:na], refs[na:na + nb]
        o_ref, acc_ref = refs[na + nb:]
        s = pl.program_id(1)
        a, b = a_tile(pl.program_id(0), *a_refs), b_tile(pl.program_id(0), *b_refs)
        part = _dot(a, b) if a_transposed else _dot_tn(a, b)

        @pl.when(s == 0)
        def _():
            acc_ref[...] = part

        @pl.when(s > 0)
        def _():
            acc_ref[...] += part

        @pl.when(s == t // tt - 1)
        def _():
            acc = acc_ref[...].astype(BF16)
            if o_ref.shape[-1] == acc.shape[-1]:
                o_ref[...] = acc.reshape(o_ref.shape)
            else:
                for k in range(o_ref.shape[0]):
                    o_ref[k] = acc[:, k * o_ref.shape[-1]:(k + 1) * o_ref.shape[-1]]

    return _call(body, name=name, grid=(n_out, t // tt), in_specs=list(a_specs) + list(b_specs), out_specs=out_spec,
                 out_shape=out_shape, scratch_shapes=[pltpu.VMEM(acc_shape, F32)])(*a_args, *b_args)


def _ident(*args):
    return args[-1][...]


def _heads_tile(j, *refs):
    return _heads_to_rows(*refs)


def kernel(x, norm_g, conv_in_w, conv_w, conv_out_w, kv_norm_g, kv_w, q_w, o_w, ffn_in_w, ffn_out_w, loss_target, m_norm_g, m_conv_in_w, m_conv_w, m_conv_out_w, m_kv_norm_g, m_kv_w, m_q_w, m_o_w, m_ffn_in_w, m_ffn_out_w, v_norm_g, v_conv_in_w, v_conv_w, v_conv_out_w, v_kv_norm_g, v_kv_w, v_q_w, v_o_w, v_ffn_in_w, v_ffn_out_w):
    x0 = x[0]
    target = loss_target[0]
    t, d = x0.shape
    depth = norm_g.shape[0]
    n_a = conv_in_w.shape[0]
    n_b = q_w.shape[0]
    hp = d // LANES
    tm, tg, tgb = BWD_ROW_TILE, GRAD_ROW_TILE, min(t, 2 * GRAD_ROW_TILE)
    assert t % SUPER == 0 and d % LANES == 0 and depth == n_a + n_b
    dev = 4 * lax.axis_index("x") + 2 * lax.axis_index("y") + lax.axis_index("c")

    n_small = 4 * depth + 3 * n_a
    small_rows = -(-(n_small + 1) // 8) * 8
    small_local = jnp.concatenate([norm_g.reshape(4 * depth, -1), conv_w.reshape(3 * n_a, -1),
                                   jnp.zeros((small_rows - n_small, norm_g.shape[-1]), F32)], axis=0)
    swap = lambda a: jnp.swapaxes(a, 1, 2)
    big = {"conv_in_w": conv_in_w, "conv_out_w": conv_out_w, "kv_w": kv_w[None], "q_w": q_w, "o_w": o_w,
           "ffn_in_w": swap(ffn_in_w), "ffn_out_w": ffn_out_w}
    names = list(big)

    def group(layer):
        if layer < n_a:
            return [("conv_in_w", layer), ("conv_out_w", layer), ("ffn_in_w", layer), ("ffn_out_w", layer)]
        j = layer - n_a
        return ([("kv_w", 0)] if j == 0 else []) + [("q_w", j), ("o_w", j), ("ffn_in_w", layer), ("ffn_out_w", layer)]

    slot = dev.astype(jnp.int32).reshape(1)
    is_ffn = lambda key: key[0].startswith("ffn")
    first_keys = [key for key in group(0) if not is_ffn(key)]
    first = _all_gather([small_local] + [_cast_layer(big[k], i, None, f"cast_{k}_{i}") for k, i in first_keys], "gather_weights")
    small_all = first[0].transpose(1, 0, 2).reshape(small_rows, d)
    wl = {key: a[:, None] for key, a in zip(first_keys, first[1:])}

    def gather_start(keys, after, tag):
        lands = [_cast_layer(big[k], i, slot, f"cast_{k}_{i}") for k, i in keys]
        send_sems, recv_sems, _, lands, tok = _send_start([], lands, after, "gather_near_start" + tag, "gather_near")
        return (keys, tag, send_sems, recv_sems, lands), tok[0, 0]

    def gather_mid(flight, after):
        keys, tag, send_sems, recv_sems, lands = flight
        _, lands = _send_wait(send_sems, recv_sems, [], lands, after, "gather_near_wait" + tag, "gather_near")
        send_sems, recv_sems, _, lands, tok = _send_start([], lands, after, "gather_far_start" + tag, "gather_far")
        return (keys, tag, send_sems, recv_sems, lands), tok[0, 0]

    def gather_wait(flight, after):
        keys, tag, send_sems, recv_sems, lands = flight
        _, lands = _send_wait(send_sems, recv_sems, [], lands, after, "gather_far_wait" + tag, "gather_far")
        wl.update({key: a[:, None] for key, a in zip(keys, lands)})

    in_flight, token = gather_start([key for key in group(0) if is_ffn(key)], small_all, "_l0")
    W = lambda k, i: (wl[(k, i)], 0)
    gain = lambda layer, k: small_all[4 * layer + k][None]
    taps = lambda layer: small_all[4 * depth + 3 * layer: 4 * depth + 3 * layer + 3]
    g_kv = kv_norm_g[None]
    slopes = _alibi_slopes(d // HEAD_DIM)
    fc = big["ffn_in_w"].shape[-2]
    cb = big["conv_in_w"].shape[-1]
    kvb = big["kv_w"].shape[-1]
    q_scale = HEAD_DIM ** -0.5

    saved = []
    kv = kvn_t = None
    xs = x0
    for layer in range(depth):
        tag = f"_l{layer}"
        g0 = g1 = g2 = g3 = 0.0
        if layer == 0:
            g0 = token
        else:
            gather_wait(in_flight, xs)
            if layer + 1 < depth:
                in_flight, g0 = gather_start(group(layer + 1), xs, f"_l{layer + 1}")
        s = {"x_in": xs}
        g0 = gain(layer, 0) + g0
        if layer < n_a:
            s["p"], s["xn_t"] = _norm_matmul_cols(xs, g0, *W("conv_in_w", layer), "cols", "conv_in" + tag)
            if layer == 0:
                in_flight, g1 = gather_mid(in_flight, s["p"])
            s["z"] = _conv_fwd(s["p"], taps(layer) + g1, "conv" + tag)
            s["mix"], x_mid = _matmul_norm_residual(s["z"][None], *W("conv_out_w", layer), gain(layer, 1), xs, "conv_out" + tag)
        else:
            j = layer - n_a
            if kv is None:
                kv, kvn_t = _norm_matmul_cols(xs, g_kv, *W("kv_w", 0), "heads", "kv_proj")
            s["q"], s["xn"] = _norm_matmul_heads(xs, g0, *W("q_w", j), q_scale, "q_proj" + tag)
            s["o"], s["lse"] = _attention_fwd(s["q"], kv, slopes, "attention" + tag)
            s["mix"], x_mid = _matmul_norm_residual(s["o"][None], *W("o_w", j), gain(layer, 1), xs, "o_proj" + tag)
        s["x_mid"] = x_mid
        if layer == 0:
            gather_wait(in_flight, x_mid)
            in_flight, g2 = gather_start(group(1), x_mid, "_l1")
        elif layer + 1 < depth:
            in_flight, g2 = gather_mid(in_flight, x_mid)
        s["gate"], s["up"], s["a"], s["fn"] = _ffn_in_swiglu(x_mid, gain(layer, 2) + g2, *W("ffn_in_w", layer), "ffn_in" + tag)
        if layer == 0:
            in_flight, g3 = gather_mid(in_flight, s["a"])
        s["ff"], xs = _matmul_norm_residual(s["a"], *W("ffn_out_w", layer), gain(layer, 3) + g3, x_mid, "ffn_out" + tag)
        saved.append(s)

    last = saved[-1]
    sq, dx_out, d_ff, dg3 = _loss_head(xs, target, last["ff"], gain(depth - 1, 3), "loss_head")
    loss = lax.psum(sq[0, 0] * (0.5 / d), ("x", "y", "c"))

    dgain = {(depth - 1, 3): dg3}
    dtaps = {}
    grads = {k: [None] * big[k].shape[0] for k in names}
    dkv_parts = []
    scattering = []

    def scatter_start(keys, tag):
        parts = [grads[k][i] for k, i in keys]
        zones = [lax.empty(p.shape, p.dtype) for p in parts]
        send_sems, recv_sems, parts, zones, tok = _send_start(parts, zones, small_all, "scatter_start" + tag, "scatter")
        scattering.append((keys, tag, send_sems, recv_sems, parts, zones))
        return tok[0, 0]

    for layer in reversed(range(depth)):
        tag = f"_l{layer}"
        s = saved[layer]
        dh = _swiglu_bwd(d_ff, *W("ffn_out_w", layer), s["gate"], s["up"], "swiglu_bwd" + tag)
        rows_out = big["ffn_out_w"].shape[1]
        grads["ffn_out_w"][layer] = _grad_weight(
            [pl.BlockSpec((None, tgb, fc), lambda c, i: (c, i, 0))], [s["a"]], _ident,
            [pl.BlockSpec((tgb, d), lambda c, i: (i, 0))], [d_ff], _ident,
            4, (fc, d), pl.BlockSpec((2, rows_out, d), lambda c, i: (c, 0, 0)), _sds((N_DEV, rows_out, d), BF16), t,
            "grad_ffn_out" + tag, tt=tgb)
        grads["ffn_in_w"][layer] = _grad_weight(
            [pl.BlockSpec((None, None, tgb, fc), lambda j, i: (j % 4, j // 4, i, 0))], [dh], _ident,
            [pl.BlockSpec((tgb, d), lambda j, i: (i, 0))], [s["fn"]], _ident,
            N_DEV, (fc, d), pl.BlockSpec((None, fc, d), lambda j, i: (j, 0, 0)), _sds((N_DEV, fc, d), BF16), t,
            "grad_ffn_in" + tag, tt=tgb)
        tok = scatter_start([("ffn_in_w", layer), ("ffn_out_w", layer)], "_ffn" + tag)
        dx_mid, d_mix, dg2, dg1 = _bwd_matmul_norms(
            [pl.BlockSpec((None, None, tm, fc), lambda i, j: (j % 4, j // 4, i, 0))], [dh], _ident, N_DEV,
            pl.BlockSpec((None, None, fc, d), lambda i, j: (j, 0, 0, 0)), W("ffn_in_w", layer)[0], _ident,
            s["x_mid"], gain(layer, 2) + tok, dx_out, s["mix"], gain(layer, 1), "ffn_in_bwd" + tag, w_transposed=True)
        dgain[(layer, 2)], dgain[(layer, 1)] = dg2, dg1
        full_rows = pl.BlockSpec((N_DEV, d // N_DEV, d), lambda j, i: (0, 0, 0))
        rows_w = lambda wname, idx: (pl.BlockSpec((N_DEV, None, d // N_DEV, d), lambda i, j: (0, 0, 0, 0)), W(wname, idx)[0],
                                     lambda w_ref: w_ref[...].reshape(d, d))
        if layer < n_a:
            d_z = _matmul_nt_rows(d_mix, *W("conv_out_w", layer), BF16, "conv_out_bwd" + tag)
            grads["conv_out_w"][layer] = _grad_weight(
                [pl.BlockSpec((tgb, d), lambda j, i: (i, 0))], [s["z"]], _ident,
                [pl.BlockSpec((tgb, d), lambda j, i: (i, 0))], [d_mix], _ident,
                1, (d, d), full_rows, _sds((N_DEV, d // N_DEV, d), BF16), t, "grad_conv_out" + tag, tt=tgb)
            d_p, dtaps[layer] = _conv_bwd(s["p"], d_z, taps(layer), "conv_bwd" + tag)
            grads["conv_in_w"][layer] = _grad_weight(
                [pl.BlockSpec((d, tgb), lambda j, i: (0, i))], [s["xn_t"]], _ident,
                [pl.BlockSpec((tgb, 2 * cb), lambda j, i: (i, j))], [d_p], _ident,
                N_DEV // 2, (d, 2 * cb), pl.BlockSpec((2, d, cb), lambda j, i: (j, 0, 0)), _sds((N_DEV, d, cb), BF16), t,
                "grad_conv_in" + tag, a_transposed=True, tt=tgb)
            a_specs, a_args, a_tile, n_steps = [pl.BlockSpec((tm, N_DEV * cb), lambda i, j: (i, 0))], [d_p], _ident, 1
            w_spec = pl.BlockSpec((N_DEV, None, d, cb), lambda i, j: (0, 0, 0, 0))
            w_arg = W("conv_in_w", layer)[0]
            w_mat = lambda w_ref: jnp.concatenate([w_ref[k] for k in range(N_DEV)], axis=1)
            resid = dx_mid
        else:
            j_b = layer - n_a
            d_o = _matmul_nt_rows(d_mix, *W("o_w", j_b), F32, "o_proj_bwd" + tag)
            grads["o_w"][j_b] = _grad_weight(
                [pl.BlockSpec((tgb, d), lambda j, i: (i, 0))], [s["o"]], _ident,
                [pl.BlockSpec((tgb, d), lambda j, i: (i, 0))], [d_mix], _ident,
                1, (d, d), full_rows, _sds((N_DEV, d // N_DEV, d), BF16), t, "grad_o" + tag, tt=tgb)
            dk_in, dv_in = dkv_parts[0] if dkv_parts else (None, None)
            dq, dk, dv = _attention_bwd(s["q"], kv, s["o"], s["lse"], d_o, dk_in, dv_in, slopes, q_scale, "attention_bwd" + tag)
            dkv_parts = [(dk, dv)]
            heads_spec = pl.BlockSpec((hp, tg, LANES), lambda j, i: (0, i, 0))
            grads["q_w"][j_b] = _grad_weight(
                [pl.BlockSpec((tg, d), lambda j, i: (i, 0))], [s["xn"]], _ident,
                [heads_spec], [dq], _heads_tile,
                1, (d, d), full_rows, _sds((N_DEV, d // N_DEV, d), BF16), t, "grad_q" + tag)
            a_specs, a_args, a_tile, n_steps = [pl.BlockSpec((hp, tm, LANES), lambda i, j: (0, i, 0))], [dq], _heads_tile, 1
            w_spec, w_arg, w_mat = rows_w("q_w", j_b)
            resid = dx_mid
            if layer == n_a:
                pieces = kvb // LANES
                halves = []
                for src in (0, 1):
                    halves.append([part[src] for part in dkv_parts])
                n_half = len(dkv_parts)
                kv_args = [arr for src in (0, 1) for arr in halves[src]]

                def kv_block(src, j):
                    return jnp.where((j // 4) == src, j % 4, 0)

                def kv_tile(j, *refs):
                    keys = _heads_to_rows(*refs[:n_half])
                    vals = _heads_to_rows(*refs[n_half:])
                    return jnp.where(j < 4, keys, vals)

                kv_specs = [pl.BlockSpec((pieces, tm, LANES), functools.partial(lambda i, j, src: (kv_block(src, j), i, 0), src=src))
                            for src in (0, 1) for _ in range(n_half)]
                resid, dgain["kv"] = _bwd_matmul_norms(
                    kv_specs, kv_args, kv_tile, N_DEV,
                    pl.BlockSpec((None, None, d, kvb), lambda i, j: (j, 0, 0, 0)), W("kv_w", 0)[0], _ident,
                    s["x_in"], g_kv, dx_mid, None, None, "kv_proj_bwd")
                kv_b_specs = [pl.BlockSpec((pieces, tgb, LANES), functools.partial(lambda j, i, src: (kv_block(src, j), i, 0), src=src))
                              for src in (0, 1) for _ in range(n_half)]
                grads["kv_w"][0] = _grad_weight(
                    [pl.BlockSpec((d, tgb), lambda j, i: (0, i))], [kvn_t], _ident,
                    kv_b_specs, kv_args, kv_tile,
                    N_DEV, (d, kvb), pl.BlockSpec((None, d, kvb), lambda j, i: (j, 0, 0)), _sds((N_DEV, d, kvb), BF16), t,
                    "grad_kv", a_transposed=True, tt=tgb)
        tok = scatter_start([key for key in group(layer) if not key[0].startswith("ffn")], "_mix" + tag)
        if layer > 0:
            prev = saved[layer - 1]
            dx_out, d_ff, dg0, dg3p = _bwd_matmul_norms(
                a_specs, a_args, a_tile, n_steps, w_spec, w_arg, w_mat,
                s["x_in"], gain(layer, 0) + tok, resid, prev["ff"], gain(layer - 1, 3), "mixer_in_bwd" + tag)
            dgain[(layer, 0)], dgain[(layer - 1, 3)] = dg0, dg3p
        else:
            grad_x, dg0 = _bwd_matmul_norms(
                a_specs, a_args, a_tile, n_steps, w_spec, w_arg, w_mat,
                s["x_in"], gain(layer, 0), resid, None, None, "mixer_in_bwd" + tag)
            dgain[(layer, 0)] = dg0

    small_grad = jnp.concatenate(
        [dgain[(layer, k)] for layer in range(depth) for k in range(4)] + [dtaps[layer] for layer in range(n_a)]
        + [dgain["kv"]] + [jnp.zeros((small_rows - n_small - 1, d), F32)], axis=0)
    small_grads_all = _all_gather([small_grad], "gather_small_grads")[0]
    lo = dev * (d // N_DEV)

    def pack(ng, cwp, kvg):
        rows = jnp.concatenate([ng.reshape(4 * depth, -1), cwp.reshape(3 * n_a, -1)], axis=0)
        z = lax.dynamic_update_slice(jnp.zeros((small_rows, d), F32), rows, (0, lo))
        return lax.dynamic_update_slice(z, kvg[None], (n_small, 0))

    w_small = lax.dynamic_update_slice(small_all, g_kv, (n_small, 0))
    m_small, v_small = pack(m_norm_g, m_conv_w, m_kv_norm_g), pack(v_norm_g, v_conv_w, v_kv_norm_g)
    sm = _small_adamw(small_grads_all, w_small, m_small, v_small, "adamw_small")

    def unpack(a):
        mine = lax.dynamic_slice(a, (0, lo), (small_rows, d // N_DEV))
        return (mine[:4 * depth].reshape(norm_g.shape), mine[4 * depth:n_small].reshape(conv_w.shape), a[n_small])

    small_out = [unpack(a) for a in sm]

    moments = {"conv_in_w": (m_conv_in_w, v_conv_in_w), "conv_out_w": (m_conv_out_w, v_conv_out_w),
               "kv_w": (m_kv_w[None], v_kv_w[None]), "q_w": (m_q_w, v_q_w), "o_w": (m_o_w, v_o_w),
               "ffn_in_w": (swap(m_ffn_in_w), swap(v_ffn_in_w)), "ffn_out_w": (m_ffn_out_w, v_ffn_out_w)}
    landed = {k: [None] * big[k].shape[0] for k in names}
    for keys, tag, send_sems, recv_sems, parts, zones in scattering:
        parts, zones = _send_wait(send_sems, recv_sems, parts, zones, grad_x, "scatter_wait" + tag, "scatter")
        for (k, i), part, zone in zip(keys, parts, zones):
            landed[k][i] = (part, zone)
    res = {k: _sum_adamw(slot, [p for p, _ in landed[k]], [z for _, z in landed[k]], big[k], moments[k][0], moments[k][1],
                         "adamw_" + k) for k in names}

    def big_out(k, which):
        out = res[k][which]
        return out[0] if k == "kv_w" else swap(out) if k == "ffn_in_w" else out

    out_names = ["norm_g", "conv_in_w", "conv_w", "conv_out_w", "kv_norm_g", "kv_w", "q_w", "o_w", "ffn_in_w", "ffn_out_w"]
    small_pos = {"norm_g": 0, "conv_w": 1, "kv_norm_g": 2}
    outs = [loss, grad_x[None]]
    for which in range(4):
        for k in out_names:
            outs.append(small_out[which][small_pos[k]] if k in small_pos else big_out(k, which))
    return tuple(outs)
```

```python
import functools
import math

import numpy as np
import jax
import jax.numpy as jnp
from jax import lax
from jax.experimental import pallas as pl
from jax.experimental.pallas import tpu as pltpu

F32 = jnp.float32
BF16 = jnp.bfloat16

N_DEV = 8
RMS_EPS = 1e-6
HEAD_DIM = 64
LANES = 128
ATT_BLOCK = 128
DILATIONS = (1, 4, 16)
SUPER = ATT_BLOCK * DILATIONS[-1]
NEG = -1e30

ADAM_LR, ADAM_B1, ADAM_B2, ADAM_EPS, ADAM_WD, ADAM_STEP = 0.001, 0.9, 0.999, 1e-08, 0.01, 10

ROW_TILE = 512
BIG_ROW_TILE = 1024
SWIGLU_ROWS = 256
GRAD_ROW_TILE = 2048
BWD_ROW_TILE = 512
MESH = pl.DeviceIdType.MESH


def _call(body, *, name, grid=None, in_specs=None, out_specs=None, out_shape=None, scratch_shapes=(), prefetch=False,
          **params):
    cp = pltpu.CompilerParams(**params) if params else None
    if prefetch:
        spec = pltpu.PrefetchScalarGridSpec(num_scalar_prefetch=1, grid=grid, in_specs=in_specs, out_specs=out_specs,
                                            scratch_shapes=list(scratch_shapes))
        return pl.pallas_call(body, name=name, grid_spec=spec, out_shape=out_shape, compiler_params=cp)
    kwargs = {k: v for k, v in (("grid", grid), ("in_specs", in_specs), ("out_specs", out_specs)) if v is not None}
    return pl.pallas_call(body, name=name, out_shape=out_shape, scratch_shapes=list(scratch_shapes),
                          compiler_params=cp, **kwargs)


def _sds(shape, dtype):
    return jax.ShapeDtypeStruct(tuple(shape), dtype)


def _rms(x, g):
    r = lax.rsqrt(jnp.mean(x * x, axis=-1, keepdims=True) + RMS_EPS)
    return x * r * g


def _rms_bwd(x, g, dy):
    r = lax.rsqrt(jnp.mean(x * x, axis=-1, keepdims=True) + RMS_EPS)
    xh = x * r
    dxh = dy * g
    dx = r * (dxh - xh * jnp.mean(dxh * xh, axis=-1, keepdims=True))
    return dx, jnp.sum(dy * xh, axis=0, keepdims=True)


def _dot(a, b):
    return jnp.dot(a, b, preferred_element_type=F32)


def _dot_nt(a, b):
    return lax.dot_general(a, b, (((1,), (1,)), ((), ())), preferred_element_type=F32)


def _dot_tn(a, b):
    return lax.dot_general(a, b, (((0,), (0,)), ((), ())), preferred_element_type=F32)


def _mesh_pos():
    return lax.axis_index("x"), lax.axis_index("y"), lax.axis_index("c")


def _all_gather(arrs, name):
    n = len(arrs)

    def body(*refs):
        ins, outs = refs[:n], refs[n:2 * n]
        send_sems, recv_sems, local_sems = refs[2 * n:]
        x, y, c = _mesh_pos()
        me, sibling = (x, y, c), (x, y, 1 - c)
        chips = [(1 - x, y), (x, 1 - y), (1 - x, 1 - y)]

        def copy(a, k, block, to, src=None):
            dst = outs[a].at[4 * block[0] + 2 * block[1] + block[2]]
            return pltpu.make_async_remote_copy(
                src_ref=dst if src is None else src, dst_ref=dst, send_sem=send_sems.at[a, k],
                recv_sem=recv_sems.at[a, k], device_id=to, device_id_type=MESH)

        started = []
        for a in range(n):
            mine = pltpu.make_async_copy(ins[a], outs[a].at[4 * x + 2 * y + c], local_sems.at[a])
            mine.start()
            started.append(mine)
        first = []
        for a in range(n):
            first.append(copy(a, 0, me, sibling, src=ins[a]))
            first += [copy(a, 1 + j, me, (*chip, c), src=ins[a]) for j, chip in enumerate(chips)]
        for cp in first:
            cp.start()
        passed = []
        for a in range(n):
            for j, chip in enumerate(chips):
                copy(a, 1 + j, (*chip, c), me).wait_recv()
                fwd = copy(a, 4 + j, (*chip, c), sibling)
                fwd.start()
                passed.append(fwd)
        for a in range(n):
            copy(a, 0, sibling, me).wait_recv()
            for j, chip in enumerate(chips):
                copy(a, 4 + j, (*chip, 1 - c), me).wait_recv()
        for cp in first + passed:
            cp.wait_send()
        for cp in started:
            cp.wait()

    any_spec = pl.BlockSpec(memory_space=pl.ANY)
    outs = _call(
        body, name=name, in_specs=[any_spec] * n, out_specs=[any_spec] * n,
        out_shape=[_sds((N_DEV,) + a.shape, a.dtype) for a in arrs],
        scratch_shapes=[pltpu.SemaphoreType.DMA((n, 7)), pltpu.SemaphoreType.DMA((n, 7)), pltpu.SemaphoreType.DMA((n,))],
        has_side_effects=True,
    )(*arrs)
    return list(outs)


HBM_SPEC = pl.BlockSpec(memory_space=pltpu.HBM)
SEM_SPEC = pl.BlockSpec(memory_space=pltpu.SEMAPHORE)
DATAFLOW = pltpu.SideEffectType.DATAFLOW_SIDE_EFFECTING
PEERS = [(dx, dy, dc) for dx in (0, 1) for dy in (0, 1) for dc in (0, 1)][1:]


def _peer(flip):
    x, y, c = _mesh_pos()
    return tuple(1 - v if f else v for v, f in zip((x, y, c), flip))


def _slot(pos):
    return 4 * pos[0] + 2 * pos[1] + pos[2]


def _in_hbm(a):
    return pltpu.with_memory_space_constraint(a, pltpu.HBM)


NEAR = [(0, 0, 1), (1, 0, 0), (0, 1, 0), (1, 1, 0)]


def _direct_copies(srcs, lands, send_sems, recv_sems, mode):
    me = _slot(_mesh_pos())
    copies = []
    for a in range(len(lands)):
        if mode == "scatter":
            plan = [(srcs[a].at[_slot(_peer(flip))], lands[a].at[me], _peer(flip)) for flip in PEERS]
        elif mode == "gather_near":
            plan = [(lands[a].at[me], lands[a].at[me], _peer(flip)) for flip in NEAR]
        else:
            origins = [_slot(_peer(flip)) for flip in NEAR[1:]]
            plan = [(lands[a].at[o], lands[a].at[o], _peer(NEAR[0])) for o in origins]
        for k, (src, dst, to) in enumerate(plan):
            idx = a * len(PEERS) + k
            copies.append(pltpu.make_async_remote_copy(
                src_ref=src, dst_ref=dst, send_sem=send_sems.at[idx], recv_sem=recv_sems.at[idx],
                device_id=to, device_id_type=MESH))
    return copies


def _send_start(srcs, lands, after, name, mode):
    ns, nl = len(srcs), len(lands)

    def body(*refs):
        src_refs, land_refs = refs[:ns], refs[ns:ns + nl]
        send_sems, recv_sems = refs[ns + nl + 1:ns + nl + 3]
        token = refs[-1]
        for cp in _direct_copies(src_refs, land_refs, send_sems, recv_sems, mode):
            cp.start()
        token[...] = jnp.zeros_like(token)

    sem = pltpu.SemaphoreType.DMA((nl * len(PEERS),))
    outs = pl.pallas_call(
        body, name=name,
        out_shape=(sem, sem) + tuple(pltpu.HBM(a.shape, a.dtype) for a in list(srcs) + list(lands))
        + (_sds((8, LANES), F32),),
        in_specs=[HBM_SPEC] * (ns + nl) + [pl.BlockSpec(memory_space=pl.ANY)],
        out_specs=(SEM_SPEC, SEM_SPEC) + (HBM_SPEC,) * (ns + nl) + (pl.BlockSpec(memory_space=pltpu.VMEM),),
        input_output_aliases={i: 2 + i for i in range(ns + nl)},
        compiler_params=pltpu.CompilerParams(has_side_effects=DATAFLOW),
    )(*[_in_hbm(a) for a in list(srcs) + list(lands)], after)
    send_sems, recv_sems = outs[0], outs[1]
    return send_sems, recv_sems, list(outs[2:2 + ns]), list(outs[2 + ns:2 + ns + nl]), outs[-1]


def _send_wait(send_sems, recv_sems, srcs, lands, after, name, mode):
    ns, nl = len(srcs), len(lands)

    def body(*refs):
        src_refs, land_refs = refs[:ns], refs[ns:ns + nl]
        send_sems, recv_sems = refs[ns + nl:ns + nl + 2]
        copies = _direct_copies(src_refs, land_refs, send_sems, recv_sems, mode)
        for cp in copies:
            cp.wait_send()
        for cp in copies:
            cp.wait_recv()

    outs = pl.pallas_call(
        body, name=name,
        out_shape=tuple(pltpu.HBM(a.shape, a.dtype) for a in list(srcs) + list(lands)),
        in_specs=[HBM_SPEC] * (ns + nl) + [SEM_SPEC, SEM_SPEC, pl.BlockSpec(memory_space=pl.ANY)],
        out_specs=(HBM_SPEC,) * (ns + nl),
        input_output_aliases={i: i for i in range(ns + nl)},
        compiler_params=pltpu.CompilerParams(has_side_effects=DATAFLOW),
    )(*srcs, *lands, send_sems, recv_sems, after)
    return list(outs[:ns]), list(outs[ns:])


def _row_tile(rows, cap=512):
    t = min(rows, cap)
    while rows % t or (t % 16 and t != rows):
        t -= 1
    return t


def _as2d(a):
    return a.reshape(-1, a.shape[-1])


def _cast_layer(w, layer, slot, name):
    _, rows, cols = w.shape
    tr = _row_tile(rows)

    def body(*refs):
        refs[-1][...] = refs[-2][...].astype(BF16)

    if slot is None:
        return _call(body, name=name, grid=(rows // tr,),
                     in_specs=[pl.BlockSpec((None, tr, cols), lambda i: (layer, i, 0))],
                     out_specs=pl.BlockSpec((tr, cols), lambda i: (i, 0)), out_shape=_sds((rows, cols), BF16))(w)
    return _call(body, name=name, grid=(rows // tr,), prefetch=True,
                 in_specs=[pl.BlockSpec((None, tr, cols), lambda i, s: (layer, i, 0))],
                 out_specs=pl.BlockSpec((None, tr, cols), lambda i, s: (s[0], i, 0)),
                 out_shape=_sds((N_DEV, rows, cols), BF16))(slot, w)


def _sum_adamw(slot, parts, lands, w, m, v, name):
    n_l = len(parts)
    _, rows, cols = lands[0].shape
    tr = _row_tile(rows, 128)

    def body(s_ref, *refs):
        p_refs, l_refs = refs[:n_l], refs[n_l:2 * n_l]
        w_ref, m_ref, v_ref, g_ref, d_ref, nm_ref, nv_ref = refs[2 * n_l:]
        for k in range(n_l):
            @pl.when(pl.program_id(0) == k)
            def _(k=k):
                own = p_refs[k][...]
                g = jnp.zeros((tr, cols), F32)
                for j in range(N_DEV):
                    g = g + jnp.where(s_ref[0] == j, own, l_refs[k][j]).astype(F32)
                delta, nm, nv = _adamw_math(w_ref[...], g, m_ref[...], v_ref[...])
                g_ref[...] = g
                d_ref[...] = delta
                nm_ref[...] = nm
                nv_ref[...] = nv

    def own_block(k):
        return pl.BlockSpec((None, tr, cols), lambda l, i, s: (s[0], jnp.where(l == k, i, 0), 0))

    def zone_block(k):
        return pl.BlockSpec((N_DEV, tr, cols), lambda l, i, s: (0, jnp.where(l == k, i, 0), 0))

    lay = pl.BlockSpec((None, tr, cols), lambda l, i, s: (l, i, 0))
    return _call(body, name=name, grid=(n_l, rows // tr), prefetch=True,
                 in_specs=[own_block(k) for k in range(n_l)] + [zone_block(k) for k in range(n_l)] + [lay, lay, lay],
                 out_specs=[lay] * 4, out_shape=[_sds((n_l, rows, cols), F32)] * 4)(slot, *parts, *lands, w, m, v)


def _adamw_math(w, g, m, v):
    m = ADAM_B1 * m + (1.0 - ADAM_B1) * g
    v = ADAM_B2 * v + (1.0 - ADAM_B2) * (g * g)
    m_hat = m / (1.0 - ADAM_B1 ** ADAM_STEP)
    v_hat = v / (1.0 - ADAM_B2 ** ADAM_STEP)
    delta = -ADAM_LR * (m_hat / (jnp.sqrt(v_hat) + ADAM_EPS) + ADAM_WD * w)
    return delta, m, v


def _small_adamw(gathered, w, m, v, name):
    def body(a_ref, w_ref, m_ref, v_ref, g_ref, d_ref, nm_ref, nv_ref):
        g = a_ref[0]
        for k in range(1, N_DEV):
            g = g + a_ref[k]
        delta, nm, nv = _adamw_math(w_ref[...], g, m_ref[...], v_ref[...])
        g_ref[...] = g
        d_ref[...] = delta
        nm_ref[...] = nm
        nv_ref[...] = nv

    return _call(body, name=name, out_shape=[_sds(w.shape, F32)] * 4)(gathered, w, m, v)


def _norm_matmul_cols(x, g, wg, layer, mode, name):
    t, d = x.shape
    nb = wg.shape[-1]
    tm = BIG_ROW_TILE
    per = 2
    pieces = per * nb // LANES

    def body(x_ref, g_ref, w_ref, y_ref, xnt_ref):
        xn = _rms(x_ref[...], g_ref[...])
        xnt_ref[...] = xn.T.astype(BF16)
        xn = xn.astype(BF16)
        for j in range(N_DEV // per):
            y = _dot(xn, jnp.concatenate([w_ref[per * j + k] for k in range(per)], axis=1))
            if mode == "heads":
                for p in range(pieces):
                    y_ref[pieces * j + p] = y[:, p * LANES:(p + 1) * LANES]
            else:
                y_ref[:, j * per * nb:(j + 1) * per * nb] = y.astype(BF16)

    if mode == "cols":
        y_shape, y_spec = _sds((t, N_DEV * nb), BF16), pl.BlockSpec((tm, N_DEV * nb), lambda i: (i, 0))
    else:
        y_shape = _sds((N_DEV // per * pieces, t, LANES), F32)
        y_spec = pl.BlockSpec((N_DEV // per * pieces, tm, LANES), lambda i: (0, i, 0))
    return _call(
        body, name=name, grid=(t // tm,),
        in_specs=[pl.BlockSpec((tm, d), lambda i: (i, 0)), pl.BlockSpec((1, d), lambda i: (0, 0)),
                  pl.BlockSpec((N_DEV, None, d, nb), lambda i: (0, layer, 0, 0))],
        out_specs=[y_spec, pl.BlockSpec((d, tm), lambda i: (0, i))],
        out_shape=[y_shape, _sds((d, t), BF16)])(x, g, wg)


def _ffn_in_swiglu(x, g, wg, layer, name):
    t, d = x.shape
    fc = wg.shape[-2]
    tm = ROW_TILE

    def body(x_ref, g_ref, w_ref, gate_ref, up_ref, a_ref, xn_ref):
        xn = _rms(x_ref[...], g_ref[...]).astype(BF16)
        xn_ref[...] = xn
        for c in range(4):
            gate, up = _dot_nt(xn, w_ref[c]), _dot_nt(xn, w_ref[c + 4])
            gate_ref[c] = gate.astype(BF16)
            up_ref[c] = up.astype(BF16)
            a_ref[c] = (gate * jax.nn.sigmoid(gate) * up).astype(BF16)

    chunks = pl.BlockSpec((4, tm, fc), lambda i: (0, i, 0))
    return _call(
        body, name=name, grid=(t // tm,),
        in_specs=[pl.BlockSpec((tm, d), lambda i: (i, 0)), pl.BlockSpec((1, d), lambda i: (0, 0)),
                  pl.BlockSpec((N_DEV, None, fc, d), lambda i: (0, layer, 0, 0))],
        out_specs=[chunks, chunks, chunks, pl.BlockSpec((tm, d), lambda i: (i, 0))],
        out_shape=[_sds((4, t, fc), BF16)] * 3 + [_sds((t, d), BF16)])(x, g, wg)


def _norm_matmul_heads(x, g, wg, layer, scale, name):
    t, d = x.shape
    tm = ROW_TILE
    hp = d // LANES

    def body(x_ref, g_ref, w_ref, y_ref, xn_ref):
        xn = _rms(x_ref[...], g_ref[...]).astype(BF16)
        xn_ref[...] = xn
        y = _dot(xn, w_ref[...].reshape(d, d)) * scale
        for p in range(hp):
            y_ref[p] = y[:, p * LANES:(p + 1) * LANES]

    return _call(
        body, name=name, grid=(t // tm,),
        in_specs=[pl.BlockSpec((tm, d), lambda i: (i, 0)), pl.BlockSpec((1, d), lambda i: (0, 0)),
                  pl.BlockSpec((N_DEV, None, d // N_DEV, d), lambda i: (0, layer, 0, 0))],
        out_specs=[pl.BlockSpec((hp, tm, LANES), lambda i: (0, i, 0)), pl.BlockSpec((tm, d), lambda i: (i, 0))],
        out_shape=[_sds((hp, t, LANES), F32), _sds((t, d), BF16)])(x, g, wg)


def _shift_down(u, halo, k, tm):
    row = lax.broadcasted_iota(jnp.int32, u.shape, 0)
    out = pltpu.roll(u, k, 0)
    for j in range(k):
        out = jnp.where(row == j, halo[halo.shape[0] - k + j:halo.shape[0] - k + j + 1, :], out)
    return out


def _shift_up(u, halo, k, tm):
    row = lax.broadcasted_iota(jnp.int32, u.shape, 0)
    out = pltpu.roll(u, tm - k, 0)
    for j in range(k):
        out = jnp.where(row == tm - k + j, halo[j:j + 1, :], out)
    return out


HALO = 16


def _conv_fwd(p, cw, name):
    t, d3 = p.shape
    d = d3 // 3
    tm = ROW_TILE
    hb = tm // HALO

    def body(p_ref, prev_ref, cw_ref, z_ref):
        i = pl.program_id(0)
        b = p_ref[:, 0:d].astype(F32)
        u = p_ref[:, d:2 * d].astype(F32) * p_ref[:, 2 * d:3 * d].astype(F32)
        keep = (i > 0).astype(F32)
        hu = prev_ref[:, d:2 * d].astype(F32) * prev_ref[:, 2 * d:3 * d].astype(F32) * keep
        uc = cw_ref[2:3, :] * u + cw_ref[1:2, :] * _shift_down(u, hu, 1, tm) + cw_ref[0:1, :] * _shift_down(u, hu, 2, tm)
        z_ref[...] = (b * uc).astype(BF16)

    return _call(
        body, name=name, grid=(t // tm,),
        in_specs=[pl.BlockSpec((tm, d3), lambda i: (i, 0)),
                  pl.BlockSpec((HALO, d3), lambda i: (jnp.maximum(i * hb - 1, 0), 0)),
                  pl.BlockSpec((3, d), lambda i: (0, 0))],
        out_specs=pl.BlockSpec((tm, d), lambda i: (i, 0)), out_shape=_sds((t, d), BF16))(p, p, cw)


def _matmul_norm_residual(a3, wg, layer, g, x_res, name):
    kc_n, t, kc = a3.shape
    d = wg.shape[-1]
    per = N_DEV // kc_n
    rows = wg.shape[2]
    tm = ROW_TILE

    def body(a_ref, w_ref, g_ref, x_ref, raw_ref, xo_ref):
        raw = None
        for c in range(kc_n):
            term = _dot(a_ref[c], w_ref[c * per:(c + 1) * per].reshape(per * rows, d))
            raw = term if raw is None else raw + term
        raw_ref[...] = raw.astype(BF16)
        xo_ref[...] = x_ref[...] + _rms(raw, g_ref[...])

    row_spec = pl.BlockSpec((tm, d), lambda i: (i, 0))
    return _call(
        body, name=name, grid=(t // tm,),
        in_specs=[pl.BlockSpec((kc_n, tm, kc), lambda i: (0, i, 0)),
                  pl.BlockSpec((N_DEV, None, rows, d), lambda i: (0, layer, 0, 0)),
                  pl.BlockSpec((1, d), lambda i: (0, 0)), row_spec],
        out_specs=[row_spec, row_spec], out_shape=[_sds((t, d), BF16), _sds((t, d), F32)])(a3, wg, g, x_res)


def _alibi_slopes(n_heads):
    hh = np.arange(n_heads, dtype=np.float32) + 1.0
    s = np.power(2.0, -8.0 * hh / n_heads).astype(np.float32)
    return jnp.asarray(np.repeat(s.reshape(n_heads // 2, 2, 1), 2 * ATT_BLOCK, axis=2))


def _band_bias(sl_ref, dil):
    u = lax.broadcasted_iota(jnp.int32, (ATT_BLOCK, 2 * ATT_BLOCK), 0)
    kk = lax.broadcasted_iota(jnp.int32, (ATT_BLOCK, 2 * ATT_BLOCK), 1)
    delta = u + ATT_BLOCK - kk
    valid = (delta >= 0) & (delta <= ATT_BLOCK)
    dist = (delta * dil).astype(F32)
    rows = [jnp.where(valid, -sl_ref[hd:hd + 1, :] * dist, NEG) for hd in range(2)]
    return jnp.concatenate(rows, axis=0)


def _stack_heads(a):
    lane = lax.broadcasted_iota(jnp.int32, a.shape, 1)
    return jnp.concatenate([jnp.where(lane < HEAD_DIM, a, 0.0), jnp.where(lane >= HEAD_DIM, a, 0.0)], axis=0).astype(BF16)


def _unstack_heads(a2):
    top, bot = a2[:ATT_BLOCK], a2[ATT_BLOCK:]
    lane = lax.broadcasted_iota(jnp.int32, top.shape, 1)
    return jnp.where(lane < HEAD_DIM, top, bot)


def _rows_to_lanes(a0, a1):
    eye = lax.broadcasted_iota(jnp.int32, a0.shape, 0) == lax.broadcasted_iota(jnp.int32, a0.shape, 1)
    return jnp.concatenate([jnp.sum(jnp.where(eye, a, 0.0), axis=0, keepdims=True) for a in (a0, a1)], axis=1)


def _fill_bias_t(sl_ref, bias_ref):
    kk = lax.broadcasted_iota(jnp.int32, (2 * ATT_BLOCK, 2 * ATT_BLOCK), 0)
    lane = lax.broadcasted_iota(jnp.int32, (2 * ATT_BLOCK, 2 * ATT_BLOCK), 1)
    delta = lane % ATT_BLOCK + ATT_BLOCK - kk
    valid = (delta >= 0) & (delta <= ATT_BLOCK)
    slope = jnp.concatenate([sl_ref[0:1, :ATT_BLOCK], sl_ref[1:2, :ATT_BLOCK]], axis=1)
    for gi, dil in enumerate(DILATIONS):
        bias = jnp.where(valid, -slope * (delta * dil).astype(F32), NEG)
        bias_ref[2 * gi] = bias
        bias_ref[2 * gi + 1] = jnp.where(kk < ATT_BLOCK, NEG, bias)


def _fill_bias(sl_ref, bias_ref):
    kk = lax.broadcasted_iota(jnp.int32, (2 * ATT_BLOCK, 2 * ATT_BLOCK), 1)
    for gi, dil in enumerate(DILATIONS):
        bias = _band_bias(sl_ref, dil)
        bias_ref[2 * gi] = bias
        bias_ref[2 * gi + 1] = jnp.where(kk < ATT_BLOCK, NEG, bias)


def _strided_keys(dil, r, b, kc_ref, kp_ref, vc_ref, vp_ref, kcar_ref, vcar_ref):
    if b > 0:
        keys = pl.ds((b - 1) * (ATT_BLOCK * dil) + r, 2 * ATT_BLOCK, stride=dil)
        return kc_ref[keys, :].astype(BF16), vc_ref[keys, :].astype(BF16)
    own = pl.ds(r, ATT_BLOCK, stride=dil)
    k_own, v_own = kc_ref[own, :].astype(BF16), vc_ref[own, :].astype(BF16)
    if dil * ATT_BLOCK != SUPER:
        before = pl.ds(SUPER - ATT_BLOCK * dil + r, ATT_BLOCK, stride=dil)
        k_before, v_before = kp_ref[before, :].astype(BF16), vp_ref[before, :].astype(BF16)
    else:
        k_before, v_before = kcar_ref[r], vcar_ref[r]
        kcar_ref[r] = k_own
        vcar_ref[r] = v_own
    return jnp.concatenate([k_before, k_own], axis=0), jnp.concatenate([v_before, v_own], axis=0)


def _attention_fwd(q, kv, slopes, name):
    hp, t, _ = q.shape
    ns = t // SUPER
    nd = len(DILATIONS)

    def body(sl_ref, q_ref, kc_ref, kp_ref, vc_ref, vp_ref, o_ref, lse_ref, og_ref, lg_ref, bias_ref, kcar_ref, vcar_ref):
        n = pl.program_id(1)

        @pl.when(n == 0)
        def _():
            _fill_bias(sl_ref, bias_ref)
            kcar_ref[...] = jnp.zeros_like(kcar_ref)
            vcar_ref[...] = jnp.zeros_like(vcar_ref)

        for gi, dil in enumerate(DILATIONS):
            for idx in range(SUPER // ATT_BLOCK):
                r, b = idx % dil, idx // dil
                qs = b * (ATT_BLOCK * dil) + r
                first = (n == 0).astype(jnp.int32) if b == 0 else 0
                q2 = _stack_heads(q_ref[pl.ds(qs, ATT_BLOCK, stride=dil), :])
                kb, vb = _strided_keys(dil, r, b, kc_ref, kp_ref, vc_ref, vp_ref, kcar_ref, vcar_ref)
                s = _dot_nt(q2, kb) + bias_ref[2 * gi + first]
                m = jnp.max(s, axis=-1, keepdims=True)
                p = jnp.exp(s - m).astype(BF16)
                ol = _dot(p, jnp.concatenate([vb, jnp.ones_like(vb)], axis=1))
                l = ol[:, LANES:]
                o2 = ol[:, :LANES] / l
                lse2 = m + jnp.log(l)
                og_ref[gi, pl.ds(qs, ATT_BLOCK, stride=dil), :] = _unstack_heads(o2)
                lg_ref[gi, pl.ds(qs, ATT_BLOCK, stride=dil), :] = _unstack_heads(lse2)
        lg = [lg_ref[gi] for gi in range(nd)]
        top = functools.reduce(jnp.maximum, lg)
        ws = [jnp.exp(x - top) for x in lg]
        tot = functools.reduce(jnp.add, ws)
        lse_ref[...] = top + jnp.log(tot)
        acc = ws[0] * og_ref[0]
        for gi in range(1, nd):
            acc = acc + ws[gi] * og_ref[gi]
        o_ref[...] = (acc / tot).astype(BF16)

    cur = lambda off: pl.BlockSpec((None, SUPER, LANES), lambda h, n: (h + off, n, 0))
    prev = lambda off: pl.BlockSpec((None, SUPER, LANES), lambda h, n: (h + off, jnp.maximum(n - 1, 0), 0))
    return _call(
        body, name=name, grid=(hp, ns),
        in_specs=[pl.BlockSpec((None, 2, 2 * ATT_BLOCK), lambda h, n: (h, 0, 0)), cur(0), cur(0), prev(0), cur(hp), prev(hp)],
        out_specs=[pl.BlockSpec((SUPER, LANES), lambda h, n: (n, h)), cur(0)],
        out_shape=[_sds((t, hp * LANES), BF16), _sds((hp, t, LANES), F32)],
        scratch_shapes=[pltpu.VMEM((nd, SUPER, LANES), F32), pltpu.VMEM((nd, SUPER, LANES), F32),
                        pltpu.VMEM((2 * nd, 2 * ATT_BLOCK, 2 * ATT_BLOCK), F32)] + [
                            pltpu.VMEM((DILATIONS[-1], ATT_BLOCK, LANES), BF16)] * 2,
    )(slopes, q, kv, kv, kv, kv)


def _attention_bwd(q, kv, o, lse, d_o, dk_in, dv_in, slopes, q_scale, name):
    hp, t, _ = q.shape
    ns = t // SUPER
    shared = dk_in is not None

    def body(sl_ref, q_ref, kc_ref, kp_ref, vc_ref, vp_ref, o_ref, lse_ref, do_ref, *rest):
        dki_ref, dvi_ref = rest[:2] if shared else (None, None)
        dq_ref, dk_ref, dv_ref, dkw_ref, dvw_ref, st_ref, bias_ref, kcar_ref, vcar_ref = rest[2 if shared else 0:]
        n = pl.program_id(1)

        @pl.when(n == 0)
        def _():
            dkw_ref[...] = jnp.zeros_like(dkw_ref)
            dvw_ref[...] = jnp.zeros_like(dvw_ref)

        @pl.when(n > 0)
        def _():
            dkw_ref[0:SUPER, :] = dkw_ref[SUPER:, :]
            dvw_ref[0:SUPER, :] = dvw_ref[SUPER:, :]
            dkw_ref[SUPER:, :] = jnp.zeros((SUPER, LANES), F32)
            dvw_ref[SUPER:, :] = jnp.zeros((SUPER, LANES), F32)

        @pl.when(n < ns)
        def _():
            prod = do_ref[...] * o_ref[...].astype(F32)
            lane = lax.broadcasted_iota(jnp.int32, prod.shape, 1)
            zero = jnp.zeros((SUPER, LANES), F32)
            st_ref[0] = zero + jnp.sum(jnp.where(lane < HEAD_DIM, prod, 0.0), axis=-1, keepdims=True)
            st_ref[1] = zero + jnp.sum(jnp.where(lane >= HEAD_DIM, prod, 0.0), axis=-1, keepdims=True)
            lse = lse_ref[...]
            swapped = pltpu.roll(lse, HEAD_DIM, 1)
            st_ref[2] = jnp.where(lane < HEAD_DIM, lse, swapped)
            st_ref[3] = jnp.where(lane >= HEAD_DIM, lse, swapped)
            dq_ref[...] = jnp.zeros_like(dq_ref)

            @pl.when(n == 0)
            def _():
                _fill_bias_t(sl_ref, bias_ref)
                kcar_ref[...] = jnp.zeros_like(kcar_ref)
                vcar_ref[...] = jnp.zeros_like(vcar_ref)

            for gi, dil in enumerate(DILATIONS):
                for idx in range(SUPER // ATT_BLOCK):
                    r, b = idx % dil, idx // dil
                    qs = b * (ATT_BLOCK * dil) + r
                    ks = SUPER + (b - 1) * (ATT_BLOCK * dil) + r
                    first = (n == 0).astype(jnp.int32) if b == 0 else 0
                    rows = pl.ds(qs, ATT_BLOCK, stride=dil)
                    keys = pl.ds(ks, 2 * ATT_BLOCK, stride=dil)
                    q2 = _stack_heads(q_ref[rows, :])
                    do2 = _stack_heads(do_ref[rows, :])
                    kb, vb = _strided_keys(dil, r, b, kc_ref, kp_ref, vc_ref, vp_ref, kcar_ref, vcar_ref)
                    dd = _rows_to_lanes(st_ref[0, rows, :], st_ref[1, rows, :])
                    lse_b = _rows_to_lanes(st_ref[2, rows, :], st_ref[3, rows, :])
                    ps, dss = [], []
                    for half in range(2):
                        hk = slice(half * ATT_BLOCK, (half + 1) * ATT_BLOCK)
                        p = jnp.exp(_dot_nt(kb[hk], q2) + bias_ref[2 * gi + first, hk, :] - lse_b)
                        dss.append((p * (_dot_nt(vb[hk], do2) - dd)).astype(BF16))
                        ps.append(p.astype(BF16))
                    p, ds = jnp.concatenate(ps, axis=0), jnp.concatenate(dss, axis=0)
                    dvw_ref[keys, :] += _dot(p, do2)
                    dkw_ref[keys, :] += _dot(ds, q2)
                    dq_ref[rows, :] += _unstack_heads(_dot_tn(ds, kb)) * q_scale

        dk_ref[...] =dkw_ref[0:SUPER, :] + dki_ref[...] if shared else dkw_ref[0:SUPER, :]
        dv_ref[...] = dvw_ref[0:SUPER, :] + dvi_ref[...] if shared else dvw_ref[0:SUPER, :]

    last = ns - 1
    cur = lambda off: pl.BlockSpec((None, SUPER, LANES), lambda h, n: (h + off, jnp.minimum(n, last), 0))
    prev = lambda off: pl.BlockSpec((None, SUPER, LANES), lambda h, n: (h + off, jnp.clip(n - 1, 0, last), 0))
    nat = pl.BlockSpec((SUPER, LANES), lambda h, n: (jnp.minimum(n, last), h))
    late = pl.BlockSpec((None, SUPER, LANES), lambda h, n: (h, jnp.maximum(n - 1, 0), 0))
    dq, dk, dv = _call(
        body, name=name, grid=(hp, ns + 1),
        in_specs=[pl.BlockSpec((None, 2, 2 * ATT_BLOCK), lambda h, n: (h, 0, 0)), cur(0), cur(0), prev(0), cur(hp), prev(hp),
                  nat, cur(0), nat] + ([late, late] if shared else []),
        out_specs=[cur(0), late, late],
        out_shape=[_sds((hp, t, LANES), F32)] * 3,
        scratch_shapes=[pltpu.VMEM((2 * SUPER, LANES), F32)] * 2 + [
            pltpu.VMEM((4, SUPER, LANES), F32), pltpu.VMEM((2 * len(DILATIONS), 2 * ATT_BLOCK, 2 * ATT_BLOCK), F32)] + [
                pltpu.VMEM((DILATIONS[-1], ATT_BLOCK, LANES), BF16)] * 2,
    )(slopes, q, kv, kv, kv, kv, o, lse, d_o, *((dk_in, dv_in) if shared else ()))
    return dq, dk, dv


def _loss_head(y, target, raw, g, name):
    t, d = y.shape
    tm = ROW_TILE

    def body(y_ref, t_ref, raw_ref, g_ref, sq_ref, dy_ref, draw_ref, dg_ref):
        i = pl.program_id(0)
        err = y_ref[...] - t_ref[...]
        dy = err * (1.0 / d)
        dy_ref[...] = dy
        draw, dg = _rms_bwd(raw_ref[...].astype(F32), g_ref[...], dy)
        draw_ref[...] = draw.astype(BF16)
        sq = jnp.zeros((8, LANES), F32) + jnp.sum(err * err)

        @pl.when(i == 0)
        def _():
            sq_ref[...] = sq
            dg_ref[...] = dg

        @pl.when(i > 0)
        def _():
            sq_ref[...] += sq
            dg_ref[...] += dg

    row = pl.BlockSpec((tm, d), lambda i: (i, 0))
    vec = pl.BlockSpec((1, d), lambda i: (0, 0))
    return _call(
        body, name=name, grid=(t // tm,), in_specs=[row, row, row, vec],
        out_specs=[pl.BlockSpec((8, LANES), lambda i: (0, 0)), row, row, vec],
        out_shape=[_sds((8, LANES), F32), _sds((t, d), F32), _sds((t, d), BF16), _sds((1, d), F32)])(y, target, raw, g)


def _bwd_matmul_norms(a_specs, a_args, a_tile, n_steps, w_spec, w_arg, w_mat, xa, ga, resid, xb, gb, name,
                      w_transposed=False):
    t, d = xa.shape
    tm = BWD_ROW_TILE
    na = len(a_specs)
    second = xb is not None
    per = 8 if n_steps % 8 == 0 else 1
    n_steps //= per

    def blocks_of(spec, k):
        return pl.BlockSpec(spec.block_shape, lambda i, j: spec.index_map(i, per * j + k))

    def body(*refs):
        a_refs, w_refs = refs[:per * na], refs[per * na:per * na + per]
        xa_ref, ga_ref, res_ref = refs[per * na + per:per * na + per + 3]
        rest = refs[per * na + per + 3:]
        if second:
            xb_ref, gb_ref, dx_ref, d2_ref, dga_ref, dgb_ref, acc_ref = rest
        else:
            dx_ref, dga_ref, acc_ref = rest
        i, j = pl.program_id(0), pl.program_id(1)
        part = None
        for k in range(per):
            term = (_dot if w_transposed else _dot_nt)(a_tile(per * j + k, *a_refs[k * na:(k + 1) * na]), w_mat(w_refs[k]))
            part = term if part is None else part + term

        @pl.when(j == 0)
        def _():
            acc_ref[...] = part

        @pl.when(j > 0)
        def _():
            acc_ref[...] += part

        @pl.when(j == n_steps - 1)
        def _():
            da, dga = _rms_bwd(xa_ref[...], ga_ref[...], acc_ref[...])
            dx = res_ref[...] + da
            dx_ref[...] = dx
            if second:
                d2, dgb = _rms_bwd(xb_ref[...].astype(F32), gb_ref[...], dx)
                d2_ref[...] = d2.astype(BF16)

            @pl.when(i == 0)
            def _():
                dga_ref[...] = dga
                if second:
                    dgb_ref[...] = dgb

            @pl.when(i > 0)
            def _():
                dga_ref[...] += dga
                if second:
                    dgb_ref[...] += dgb

    row = pl.BlockSpec((tm, d), lambda i, j: (i, 0))
    vec = pl.BlockSpec((1, d), lambda i, j: (0, 0))
    in_specs = [blocks_of(sp, k) for k in range(per) for sp in a_specs] + [blocks_of(w_spec, k) for k in range(per)]
    in_specs += [row, vec, row]
    args = list(a_args) * per + [w_arg] * per + [xa, ga, resid]
    if second:
        in_specs += [row, vec]
        args += [xb, gb]
        out_specs = [row, row, vec, vec]
        out_shape = [_sds((t, d), F32), _sds((t, d), BF16), _sds((1, d), F32), _sds((1, d), F32)]
    else:
        out_specs = [row, vec]
        out_shape = [_sds((t, d), F32), _sds((1, d), F32)]
    return _call(body, name=name, grid=(t // tm, n_steps), in_specs=in_specs, out_specs=out_specs,
                 out_shape=out_shape, scratch_shapes=[pltpu.VMEM((tm, d), F32)])(*args)


def _heads_to_rows(*refs):
    hp = refs[0].shape[0]
    cols = []
    for p in range(hp):
        v = refs[0][p]
        for r in refs[1:]:
            v = v + r[p]
        cols.append(v)
    return jnp.concatenate(cols, axis=-1).astype(BF16)


def _matmul_nt_rows(a, wg, layer, out_dtype, name):
    t, d = a.shape
    tm = ROW_TILE

    def body(a_ref, w_ref, o_ref):
        o_ref[...] = _dot_nt(a_ref[...], w_ref[...].reshape(d, d)).astype(out_dtype)

    row = pl.BlockSpec((tm, d), lambda i: (i, 0))
    return _call(body, name=name, grid=(t // tm,),
                 in_specs=[row, pl.BlockSpec((N_DEV, None, d // N_DEV, d), lambda i: (0, layer, 0, 0))],
                 out_specs=row, out_shape=_sds((t, d), out_dtype))(a, wg)


def _swiglu_bwd(d_ff, wg, layer, gate, up, name):
    t, d = d_ff.shape
    fc = gate.shape[-1]
    rows = wg.shape[2]
    tm = BIG_ROW_TILE

    def body(df_ref, w_ref, g_ref, u_ref, dh_ref):
        w = w_ref[...].reshape(2 * rows, d)
        for r0 in range(0, tm, SWIGLU_ROWS):
            rs = slice(r0, r0 + SWIGLU_ROWS)
            da = _dot_nt(df_ref[rs, :], w)
            gate, up = g_ref[rs, :].astype(F32), u_ref[rs, :].astype(F32)
            sig = jax.nn.sigmoid(gate)
            dh_ref[0, rs, :] = (da * up * (sig * (1.0 + gate * (1.0 - sig)))).astype(BF16)
            dh_ref[1, rs, :] = (da * (gate * sig)).astype(BF16)

    return _call(
        body, name=name, grid=(t // tm, 4),
        in_specs=[pl.BlockSpec((tm, d), lambda i, c: (i, 0)),
                  pl.BlockSpec((2, None, rows, d), lambda i, c: (c, layer, 0, 0)),
                  pl.BlockSpec((None, tm, fc), lambda i, c: (c, i, 0)),
                  pl.BlockSpec((None, tm, fc), lambda i, c: (c, i, 0))],
        out_specs=pl.BlockSpec((None, 2, tm, fc), lambda i, c: (c, 0, i, 0)),
        out_shape=_sds((4, 2, t, fc), BF16))(d_ff, wg, gate, up)


def _conv_bwd(p, d_z, cw, name):
    t, d3 = p.shape
    d = d3 // 3
    tm = ROW_TILE
    hb = tm // HALO
    nt = t // tm

    def body(p_ref, prev_ref, next_ref, dz_ref, dzn_ref, cw_ref, dp_ref, dcw_ref):
        i = pl.program_id(0)
        b = p_ref[:, 0:d].astype(F32)
        c = p_ref[:, d:2 * d].astype(F32)
        h = p_ref[:, 2 * d:3 * d].astype(F32)
        u = c * h
        hu = prev_ref[:, d:2 * d].astype(F32) * prev_ref[:, 2 * d:3 * d].astype(F32) * (i > 0).astype(F32)
        u1, u2 = _shift_down(u, hu, 1, tm), _shift_down(u, hu, 2, tm)
        uc = cw_ref[2:3, :] * u + cw_ref[1:2, :] * u1 + cw_ref[0:1, :] * u2
        dz = dz_ref[...].astype(F32)
        duc = dz * b
        dn = dzn_ref[...].astype(F32) * next_ref[:, 0:d].astype(F32) * (i < nt - 1).astype(F32)
        du = cw_ref[2:3, :] * duc + cw_ref[1:2, :] * _shift_up(duc, dn, 1, tm) + cw_ref[0:1, :] * _shift_up(duc, dn, 2, tm)
        dp_ref[:, 0:d] = (dz * uc).astype(BF16)
        dp_ref[:, d:2 * d] = (du * h).astype(BF16)
        dp_ref[:, 2 * d:3 * d] = (du * c).astype(BF16)
        dcw = jnp.concatenate([jnp.sum(duc * u2, axis=0, keepdims=True), jnp.sum(duc * u1, axis=0, keepdims=True),
                               jnp.sum(duc * u, axis=0, keepdims=True)], axis=0)

        @pl.when(i == 0)
        def _():
            dcw_ref[...] = dcw

        @pl.when(i > 0)
        def _():
            dcw_ref[...] += dcw

    last_halo = t // HALO - 1
    return _call(
        body, name=name, grid=(nt,),
        in_specs=[pl.BlockSpec((tm, d3), lambda i: (i, 0)),
                  pl.BlockSpec((HALO, d3), lambda i: (jnp.maximum(i * hb - 1, 0), 0)),
                  pl.BlockSpec((HALO, d3), lambda i: (jnp.minimum((i + 1) * hb, last_halo), 0)),
                  pl.BlockSpec((tm, d), lambda i: (i, 0)),
                  pl.BlockSpec((HALO, d), lambda i: (jnp.minimum((i + 1) * hb, last_halo), 0)),
                  pl.BlockSpec((3, d), lambda i: (0, 0))],
        out_specs=[pl.BlockSpec((tm, d3), lambda i: (i, 0)), pl.BlockSpec((3, d), lambda i: (0, 0))],
        out_shape=[_sds((t, d3), BF16), _sds((3, d), F32)])(p, p, p, d_z, d_z, cw)


def _grad_weight(a_specs, a_args, a_tile, b_specs, b_args, b_tile, n_out, acc_shape, out_spec, out_shape, t, name,
                 a_transposed=False, tt=GRAD_ROW_TILE):
    na, nb = len(a_specs), len(b_specs)

    def body(*refs):
        a_refs, b_refs = refs[:na], refs[na:na + nb]
        o_ref, acc_ref = refs[na + nb:]
        s = pl.program_id(1)
        a, b = a_tile(pl.program_id(0), *a_refs), b_tile(pl.program_id(0), *b_refs)
        part = _dot(a, b) if a_transposed else _dot_tn(a, b)

        @pl.when(s == 0)
        def _():
            acc_ref[...] = part

        @pl.when(s > 0)
        def _():
            acc_ref[...] += part

        @pl.when(s == t // tt - 1)
        def _():
            acc = acc_ref[...].astype(BF16)
            if o_ref.shape[-1] == acc.shape[-1]:
                o_ref[...] = acc.reshape(o_ref.shape)
            else:
                for k in range(o_ref.shape[0]):
                    o_ref[k] = acc[:, k * o_ref.shape[-1]:(k + 1) * o_ref.shape[-1]]

    return _call(body, name=name, grid=(n_out, t // tt), in_specs=list(a_specs) + list(b_specs), out_specs=out_spec,
                 out_shape=out_shape, scratch_shapes=[pltpu.VMEM(acc_shape, F32)])(*a_args, *b_args)


def _ident(*args):
    return args[-1][...]


def _heads_tile(j, *refs):
    return _heads_to_rows(*refs)


def kernel(x, norm_g, conv_in_w, conv_w, conv_out_w, kv_norm_g, kv_w, q_w, o_w, ffn_in_w, ffn_out_w, loss_target, m_norm_g, m_conv_in_w, m_conv_w, m_conv_out_w, m_kv_norm_g, m_kv_w, m_q_w, m_o_w, m_ffn_in_w, m_ffn_out_w, v_norm_g, v_conv_in_w, v_conv_w, v_conv_out_w, v_kv_norm_g, v_kv_w, v_q_w, v_o_w, v_ffn_in_w, v_ffn_out_w):
    x0 = x[0]
    target = loss_target[0]
    t, d = x0.shape
    depth = norm_g.shape[0]
    n_a = conv_in_w.shape[0]
    n_b = q_w.shape[0]
    hp = d // LANES
    tm, tg, tgb = BWD_ROW_TILE, GRAD_ROW_TILE, min(t, 2 * GRAD_ROW_TILE)
    assert t % SUPER == 0 and d % LANES == 0 and depth == n_a + n_b
    dev = 4 * lax.axis_index("x") + 2 * lax.axis_index("y") + lax.axis_index("c")

    n_small = 4 * depth + 3 * n_a
    small_rows = -(-(n_small + 1) // 8) * 8
    small_local = jnp.concatenate([norm_g.reshape(4 * depth, -1), conv_w.reshape(3 * n_a, -1),
                                   jnp.zeros((small_rows - n_small, norm_g.shape[-1]), F32)], axis=0)
    swap = lambda a: jnp.swapaxes(a, 1, 2)
    big = {"conv_in_w": conv_in_w, "conv_out_w": conv_out_w, "kv_w": kv_w[None], "q_w": q_w, "o_w": o_w,
           "ffn_in_w": swap(ffn_in_w), "ffn_out_w": ffn_out_w}
    names = list(big)

    def group(layer):
        if layer < n_a:
            return [("conv_in_w", layer), ("conv_out_w", layer), ("ffn_in_w", layer), ("ffn_out_w", layer)]
        j = layer - n_a
        return ([("kv_w", 0)] if j == 0 else []) + [("q_w", j), ("o_w", j), ("ffn_in_w", layer), ("ffn_out_w", layer)]

    slot = dev.astype(jnp.int32).reshape(1)
    is_ffn = lambda key: key[0].startswith("ffn")
    first_keys = [key for key in group(0) if not is_ffn(key)]
    first = _all_gather([small_local] + [_cast_layer(big[k], i, None, f"cast_{k}_{i}") for k, i in first_keys], "gather_weights")
    small_all = first[0].transpose(1, 0, 2).reshape(small_rows, d)
    wl = {key: a[:, None] for key, a in zip(first_keys, first[1:])}

    def gather_start(keys, after, tag):
        lands = [_cast_layer(big[k], i, slot, f"cast_{k}_{i}") for k, i in keys]
        send_sems, recv_sems, _, lands, tok = _send_start([], lands, after, "gather_near_start" + tag, "gather_near")
        return (keys, tag, send_sems, recv_sems, lands), tok[0, 0]

    def gather_mid(flight, after):
        keys, tag, send_sems, recv_sems, lands = flight
        _, lands = _send_wait(send_sems, recv_sems, [], lands, after, "gather_near_wait" + tag, "gather_near")
        send_sems, recv_sems, _, lands, tok = _send_start([], lands, after, "gather_far_start" + tag, "gather_far")
        return (keys, tag, send_sems, recv_sems, lands), tok[0, 0]

    def gather_wait(flight, after):
        keys, tag, send_sems, recv_sems, lands = flight
        _, lands = _send_wait(send_sems, recv_sems, [], lands, after, "gather_far_wait" + tag, "gather_far")
        wl.update({key: a[:, None] for key, a in zip(keys, lands)})

    in_flight, token = gather_start([key for key in group(0) if is_ffn(key)], small_all, "_l0")
    W = lambda k, i: (wl[(k, i)], 0)
    gain = lambda layer, k: small_all[4 * layer + k][None]
    taps = lambda layer: small_all[4 * depth + 3 * layer: 4 * depth + 3 * layer + 3]
    g_kv = kv_norm_g[None]
    slopes = _alibi_slopes(d // HEAD_DIM)
    fc = big["ffn_in_w"].shape[-2]
    cb = big["conv_in_w"].shape[-1]
    kvb = big["kv_w"].shape[-1]
    q_scale = HEAD_DIM ** -0.5

    saved = []
    kv = kvn_t = None
    xs = x0
    for layer in range(depth):
        tag = f"_l{layer}"
        g0 = g1 = g2 = g3 = 0.0
        if layer == 0:
            g0 = token
        else:
            gather_wait(in_flight, xs)
            if layer + 1 < depth:
                in_flight, g0 = gather_start(group(layer + 1), xs, f"_l{layer + 1}")
        s = {"x_in": xs}
        g0 = gain(layer, 0) + g0
        if layer < n_a:
            s["p"], s["xn_t"] = _norm_matmul_cols(xs, g0, *W("conv_in_w", layer), "cols", "conv_in" + tag)
            if layer == 0:
                in_flight, g1 = gather_mid(in_flight, s["p"])
            s["z"] = _conv_fwd(s["p"], taps(layer) + g1, "conv" + tag)
            s["mix"], x_mid = _matmul_norm_residual(s["z"][None], *W("conv_out_w", layer), gain(layer, 1), xs, "conv_out" + tag)
        else:
            j = layer - n_a
            if kv is None:
                kv, kvn_t = _norm_matmul_cols(xs, g_kv, *W("kv_w", 0), "heads", "kv_proj")
            s["q"], s["xn"] = _norm_matmul_heads(xs, g0, *W("q_w", j), q_scale, "q_proj" + tag)
            s["o"], s["lse"] = _attention_fwd(s["q"], kv, slopes, "attention" + tag)
            s["mix"], x_mid = _matmul_norm_residual(s["o"][None], *W("o_w", j), gain(layer, 1), xs, "o_proj" + tag)
        s["x_mid"] = x_mid
        if layer == 0:
            gather_wait(in_flight, x_mid)
            in_flight, g2 = gather_start(group(1), x_mid, "_l1")
        elif layer + 1 < depth:
            in_flight, g2 = gather_mid(in_flight, x_mid)
        s["gate"], s["up"], s["a"], s["fn"] = _ffn_in_swiglu(x_mid, gain(layer, 2) + g2, *W("ffn_in_w", layer), "ffn_in" + tag)
        if layer == 0:
            in_flight, g3 = gather_mid(in_flight, s["a"])
        s["ff"], xs = _matmul_norm_residual(s["a"], *W("ffn_out_w", layer), gain(layer, 3) + g3, x_mid, "ffn_out" + tag)
        saved.append(s)

    last = saved[-1]
    sq, dx_out, d_ff, dg3 = _loss_head(xs, target, last["ff"], gain(depth - 1, 3), "loss_head")
    loss = lax.psum(sq[0, 0] * (0.5 / d), ("x", "y", "c"))

    dgain = {(depth - 1, 3): dg3}
    dtaps = {}
    grads = {k: [None] * big[k].shape[0] for k in names}
    dkv_parts = []
    scattering = []

    def scatter_start(keys, tag):
        parts = [grads[k][i] for k, i in keys]
        zones = [lax.empty(p.shape, p.dtype) for p in parts]
        send_sems, recv_sems, parts, zones, tok = _send_start(parts, zones, small_all, "scatter_start" + tag, "scatter")
        scattering.append((keys, tag, send_sems, recv_sems, parts, zones))
        return tok[0, 0]

    for layer in reversed(range(depth)):
        tag = f"_l{layer}"
        s = saved[layer]
        dh = _swiglu_bwd(d_ff, *W("ffn_out_w", layer), s["gate"], s["up"], "swiglu_bwd" + tag)
        rows_out = big["ffn_out_w"].shape[1]
        grads["ffn_out_w"][layer] = _grad_weight(
            [pl.BlockSpec((None, tgb, fc), lambda c, i: (c, i, 0))], [s["a"]], _ident,
            [pl.BlockSpec((tgb, d), lambda c, i: (i, 0))], [d_ff], _ident,
            4, (fc, d), pl.BlockSpec((2, rows_out, d), lambda c, i: (c, 0, 0)), _sds((N_DEV, rows_out, d), BF16), t,
            "grad_ffn_out" + tag, tt=tgb)
        grads["ffn_in_w"][layer] = _grad_weight(
            [pl.BlockSpec((None, None, tgb, fc), lambda j, i: (j % 4, j // 4, i, 0))], [dh], _ident,
            [pl.BlockSpec((tgb, d), lambda j, i: (i, 0))], [s["fn"]], _ident,
            N_DEV, (fc, d), pl.BlockSpec((None, fc, d), lambda j, i: (j, 0, 0)), _sds((N_DEV, fc, d), BF16), t,
            "grad_ffn_in" + tag, tt=tgb)
        tok = scatter_start([("ffn_in_w", layer), ("ffn_out_w", layer)], "_ffn" + tag)
        dx_mid, d_mix, dg2, dg1 = _bwd_matmul_norms(
            [pl.BlockSpec((None, None, tm, fc), lambda i, j: (j % 4, j // 4, i, 0))], [dh], _ident, N_DEV,
            pl.BlockSpec((None, None, fc, d), lambda i, j: (j, 0, 0, 0)), W("ffn_in_w", layer)[0], _ident,
            s["x_mid"], gain(layer, 2) + tok, dx_out, s["mix"], gain(layer, 1), "ffn_in_bwd" + tag, w_transposed=True)
        dgain[(layer, 2)], dgain[(layer, 1)] = dg2, dg1
        full_rows = pl.BlockSpec((N_DEV, d // N_DEV, d), lambda j, i: (0, 0, 0))
        rows_w = lambda wname, idx: (pl.BlockSpec((N_DEV, None, d // N_DEV, d), lambda i, j: (0, 0, 0, 0)), W(wname, idx)[0],
                                     lambda w_ref: w_ref[...].reshape(d, d))
        if layer < n_a:
            d_z = _matmul_nt_rows(d_mix, *W("conv_out_w", layer), BF16, "conv_out_bwd" + tag)
            grads["conv_out_w"][layer] = _grad_weight(
                [pl.BlockSpec((tg, d), lambda j, i: (i, 0))], [s["z"]], _ident,
                [pl.BlockSpec((tg, d), lambda j, i: (i, 0))], [d_mix], _ident,
                1, (d, d), full_rows, _sds((N_DEV, d // N_DEV, d), BF16), t, "grad_conv_out" + tag)
            d_p, dtaps[layer] = _conv_bwd(s["p"], d_z, taps(layer), "conv_bwd" + tag)
            grads["conv_in_w"][layer] = _grad_weight(
                [pl.BlockSpec((d, tgb), lambda j, i: (0, i))], [s["xn_t"]], _ident,
                [pl.BlockSpec((tgb, 2 * cb), lambda j, i: (i, j))], [d_p], _ident,
                N_DEV // 2, (d, 2 * cb), pl.BlockSpec((2, d, cb), lambda j, i: (j, 0, 0)), _sds((N_DEV, d, cb), BF16), t,
                "grad_conv_in" + tag, a_transposed=True, tt=tgb)
            a_specs, a_args, a_tile, n_steps = [pl.BlockSpec((tm, N_DEV * cb), lambda i, j: (i, 0))], [d_p], _ident, 1
            w_spec = pl.BlockSpec((N_DEV, None, d, cb), lambda i, j: (0, 0, 0, 0))
            w_arg = W("conv_in_w", layer)[0]
            w_mat = lambda w_ref: jnp.concatenate([w_ref[k] for k in range(N_DEV)], axis=1)
            resid = dx_mid
        else:
            j_b = layer - n_a
            d_o = _matmul_nt_rows(d_mix, *W("o_w", j_b), F32, "o_proj_bwd" + tag)
            grads["o_w"][j_b] = _grad_weight(
                [pl.BlockSpec((tg, d), lambda j, i: (i, 0))], [s["o"]], _ident,
                [pl.BlockSpec((tg, d), lambda j, i: (i, 0))], [d_mix], _ident,
                1, (d, d), full_rows, _sds((N_DEV, d // N_DEV, d), BF16), t, "grad_o" + tag)
            dk_in, dv_in = dkv_parts[0] if dkv_parts else (None, None)
            dq, dk, dv = _attention_bwd(s["q"], kv, s["o"], s["lse"], d_o, dk_in, dv_in, slopes, q_scale, "attention_bwd" + tag)
            dkv_parts = [(dk, dv)]
            heads_spec = pl.BlockSpec((hp, tg, LANES), lambda j, i: (0, i, 0))
            grads["q_w"][j_b] = _grad_weight(
                [pl.BlockSpec((tg, d), lambda j, i: (i, 0))], [s["xn"]], _ident,
                [heads_spec], [dq], _heads_tile,
                1, (d, d), full_rows, _sds((N_DEV, d // N_DEV, d), BF16), t, "grad_q" + tag)
            a_specs, a_args, a_tile, n_steps = [pl.BlockSpec((hp, tm, LANES), lambda i, j: (0, i, 0))], [dq], _heads_tile, 1
            w_spec, w_arg, w_mat = rows_w("q_w", j_b)
            resid = dx_mid
            if layer == n_a:
                pieces = kvb // LANES
                halves = []
                for src in (0, 1):
                    halves.append([part[src] for part in dkv_parts])
                n_half = len(dkv_parts)
                kv_args = [arr for src in (0, 1) for arr in halves[src]]

                def kv_block(src, j):
                    return jnp.where((j // 4) == src, j % 4, 0)

                def kv_tile(j, *refs):
                    keys = _heads_to_rows(*refs[:n_half])
                    vals = _heads_to_rows(*refs[n_half:])
                    return jnp.where(j < 4, keys, vals)

                kv_specs = [pl.BlockSpec((pieces, tm, LANES), functools.partial(lambda i, j, src: (kv_block(src, j), i, 0), src=src))
                            for src in (0, 1) for _ in range(n_half)]
                resid, dgain["kv"] = _bwd_matmul_norms(
                    kv_specs, kv_args, kv_tile, N_DEV,
                    pl.BlockSpec((None, None, d, kvb), lambda i, j: (j, 0, 0, 0)), W("kv_w", 0)[0], _ident,
                    s["x_in"], g_kv, dx_mid, None, None, "kv_proj_bwd")
                kv_b_specs = [pl.BlockSpec((pieces, tg, LANES), functools.partial(lambda j, i, src: (kv_block(src, j), i, 0), src=src))
                              for src in (0, 1) for _ in range(n_half)]
                grads["kv_w"][0] = _grad_weight(
                    [pl.BlockSpec((d, tg), lambda j, i: (0, i))], [kvn_t], _ident,
                    kv_b_specs, kv_args, kv_tile,
                    N_DEV, (d, kvb), pl.BlockSpec((None, d, kvb), lambda j, i: (j, 0, 0)), _sds((N_DEV, d, kvb), BF16), t,
                    "grad_kv", a_transposed=True)
        tok = scatter_start([key for key in group(layer) if not key[0].startswith("ffn")], "_mix" + tag)
        if layer > 0:
            prev = saved[layer - 1]
            dx_out, d_ff, dg0, dg3p = _bwd_matmul_norms(
                a_specs, a_args, a_tile, n_steps, w_spec, w_arg, w_mat,
                s["x_in"], gain(layer, 0) + tok, resid, prev["ff"], gain(layer - 1, 3), "mixer_in_bwd" + tag)
            dgain[(layer, 0)], dgain[(layer - 1, 3)] = dg0, dg3p
        else:
            grad_x, dg0 = _bwd_matmul_norms(
                a_specs, a_args, a_tile, n_steps, w_spec, w_arg, w_mat,
                s["x_in"], gain(layer, 0), resid, None, None, "mixer_in_bwd" + tag)
            dgain[(layer, 0)] = dg0

    small_grad = jnp.concatenate(
        [dgain[(layer, k)] for layer in range(depth) for k in range(4)] + [dtaps[layer] for layer in range(n_a)]
        + [dgain["kv"]] + [jnp.zeros((small_rows - n_small - 1, d), F32)], axis=0)
    small_grads_all = _all_gather([small_grad], "gather_small_grads")[0]
    lo = dev * (d // N_DEV)

    def pack(ng, cwp, kvg):
        rows = jnp.concatenate([ng.reshape(4 * depth, -1), cwp.reshape(3 * n_a, -1)], axis=0)
        z = lax.dynamic_update_slice(jnp.zeros((small_rows, d), F32), rows, (0, lo))
        return lax.dynamic_update_slice(z, kvg[None], (n_small, 0))

    w_small = lax.dynamic_update_slice(small_all, g_kv, (n_small, 0))
    m_small, v_small = pack(m_norm_g, m_conv_w, m_kv_norm_g), pack(v_norm_g, v_conv_w, v_kv_norm_g)
    sm = _small_adamw(small_grads_all, w_small, m_small, v_small, "adamw_small")

    def unpack(a):
        mine = lax.dynamic_slice(a, (0, lo), (small_rows, d // N_DEV))
        return (mine[:4 * depth].reshape(norm_g.shape), mine[4 * depth:n_small].reshape(conv_w.shape), a[n_small])

    small_out = [unpack(a) for a in sm]

    moments = {"conv_in_w": (m_conv_in_w, v_conv_in_w), "conv_out_w": (m_conv_out_w, v_conv_out_w),
               "kv_w": (m_kv_w[None], v_kv_w[None]), "q_w": (m_q_w, v_q_w), "o_w": (m_o_w, v_o_w),
               "ffn_in_w": (swap(m_ffn_in_w), swap(v_ffn_in_w)), "ffn_out_w": (m_ffn_out_w, v_ffn_out_w)}
    landed = {k: [None] * big[k].shape[0] for k in names}
    for keys, tag, send_sems, recv_sems, parts, zones in scattering:
        parts, zones = _send_wait(send_sems, recv_sems, parts, zones, grad_x, "scatter_wait" + tag, "scatter")
        for (k, i), part, zone in zip(keys, parts, zones):
            landed[k][i] = (part, zone)
    res = {k: _sum_adamw(slot, [p for p, _ in landed[k]], [z for _, z in landed[k]], big[k], moments[k][0], moments[k][1],
                         "adamw_" + k) for k in names}

    def big_out(k, which):
        out = res[k][which]
        return out[0] if k == "kv_w" else swap(out) if k == "ffn_in_w" else out

    out_names = ["norm_g", "conv_in_w", "conv_w", "conv_out_w", "kv_norm_g", "kv_w", "q_w", "o_w", "ffn_in_w", "ffn_out_w"]
    small_pos = {"norm_g": 0, "conv_w": 1, "kv_norm_g": 2}
    outs = [loss, grad_x[None]]
    for which in range(4):
        for k in out_names:
            outs.append(small_out[which][small_pos[k]] if k in small_pos else big_out(k, which))
    return tuple(outs)
```

```python
import functools
import math

import numpy as np
import jax
import jax.numpy as jnp
from jax import lax
from jax.experimental import pallas as pl
from jax.experimental.pallas import tpu as pltpu

F32 = jnp.float32
BF16 = jnp.bfloat16

N_DEV = 8
RMS_EPS = 1e-6
HEAD_DIM = 64
LANES = 128
ATT_BLOCK = 128
DILATIONS = (1, 4, 16)
SUPER = ATT_BLOCK * DILATIONS[-1]
NEG = -1e30

ADAM_LR, ADAM_B1, ADAM_B2, ADAM_EPS, ADAM_WD, ADAM_STEP = 0.001, 0.9, 0.999, 1e-08, 0.01, 10

ROW_TILE = 512
BIG_ROW_TILE = 1024
SWIGLU_ROWS = 256
GRAD_ROW_TILE = 2048
BWD_ROW_TILE = 512
MESH = pl.DeviceIdType.MESH


def _call(body, *, name, grid=None, in_specs=None, out_specs=None, out_shape=None, scratch_shapes=(), prefetch=False,
          **params):
    cp = pltpu.CompilerParams(**params) if params else None
    if prefetch:
        spec = pltpu.PrefetchScalarGridSpec(num_scalar_prefetch=1, grid=grid, in_specs=in_specs, out_specs=out_specs,
                                            scratch_shapes=list(scratch_shapes))
        return pl.pallas_call(body, name=name, grid_spec=spec, out_shape=out_shape, compiler_params=cp)
    kwargs = {k: v for k, v in (("grid", grid), ("in_specs", in_specs), ("out_specs", out_specs)) if v is not None}
    return pl.pallas_call(body, name=name, out_shape=out_shape, scratch_shapes=list(scratch_shapes),
                          compiler_params=cp, **kwargs)


def _sds(shape, dtype):
    return jax.ShapeDtypeStruct(tuple(shape), dtype)


def _rms(x, g):
    r = lax.rsqrt(jnp.mean(x * x, axis=-1, keepdims=True) + RMS_EPS)
    return x * r * g


def _rms_bwd(x, g, dy):
    r = lax.rsqrt(jnp.mean(x * x, axis=-1, keepdims=True) + RMS_EPS)
    xh = x * r
    dxh = dy * g
    dx = r * (dxh - xh * jnp.mean(dxh * xh, axis=-1, keepdims=True))
    return dx, jnp.sum(dy * xh, axis=0, keepdims=True)


def _dot(a, b):
    return jnp.dot(a, b, preferred_element_type=F32)


def _dot_nt(a, b):
    return lax.dot_general(a, b, (((1,), (1,)), ((), ())), preferred_element_type=F32)


def _dot_tn(a, b):
    return lax.dot_general(a, b, (((0,), (0,)), ((), ())), preferred_element_type=F32)


def _mesh_pos():
    return lax.axis_index("x"), lax.axis_index("y"), lax.axis_index("c")


def _all_gather(arrs, name):
    n = len(arrs)

    def body(*refs):
        ins, outs = refs[:n], refs[n:2 * n]
        send_sems, recv_sems, local_sems = refs[2 * n:]
        x, y, c = _mesh_pos()
        me, sibling = (x, y, c), (x, y, 1 - c)
        chips = [(1 - x, y), (x, 1 - y), (1 - x, 1 - y)]

        def copy(a, k, block, to, src=None):
            dst = outs[a].at[4 * block[0] + 2 * block[1] + block[2]]
            return pltpu.make_async_remote_copy(
                src_ref=dst if src is None else src, dst_ref=dst, send_sem=send_sems.at[a, k],
                recv_sem=recv_sems.at[a, k], device_id=to, device_id_type=MESH)

        started = []
        for a in range(n):
            mine = pltpu.make_async_copy(ins[a], outs[a].at[4 * x + 2 * y + c], local_sems.at[a])
            mine.start()
            started.append(mine)
        first = []
        for a in range(n):
            first.append(copy(a, 0, me, sibling, src=ins[a]))
            first += [copy(a, 1 + j, me, (*chip, c), src=ins[a]) for j, chip in enumerate(chips)]
        for cp in first:
            cp.start()
        passed = []
        for a in range(n):
            for j, chip in enumerate(chips):
                copy(a, 1 + j, (*chip, c), me).wait_recv()
                fwd = copy(a, 4 + j, (*chip, c), sibling)
                fwd.start()
                passed.append(fwd)
        for a in range(n):
            copy(a, 0, sibling, me).wait_recv()
            for j, chip in enumerate(chips):
                copy(a, 4 + j, (*chip, 1 - c), me).wait_recv()
        for cp in first + passed:
            cp.wait_send()
        for cp in started:
            cp.wait()

    any_spec = pl.BlockSpec(memory_space=pl.ANY)
    outs = _call(
        body, name=name, in_specs=[any_spec] * n, out_specs=[any_spec] * n,
        out_shape=[_sds((N_DEV,) + a.shape, a.dtype) for a in arrs],
        scratch_shapes=[pltpu.SemaphoreType.DMA((n, 7)), pltpu.SemaphoreType.DMA((n, 7)), pltpu.SemaphoreType.DMA((n,))],
        has_side_effects=True,
    )(*arrs)
    return list(outs)


HBM_SPEC = pl.BlockSpec(memory_space=pltpu.HBM)
SEM_SPEC = pl.BlockSpec(memory_space=pltpu.SEMAPHORE)
DATAFLOW = pltpu.SideEffectType.DATAFLOW_SIDE_EFFECTING
PEERS = [(dx, dy, dc) for dx in (0, 1) for dy in (0, 1) for dc in (0, 1)][1:]


def _peer(flip):
    x, y, c = _mesh_pos()
    return tuple(1 - v if f else v for v, f in zip((x, y, c), flip))


def _slot(pos):
    return 4 * pos[0] + 2 * pos[1] + pos[2]


def _in_hbm(a):
    return pltpu.with_memory_space_constraint(a, pltpu.HBM)


NEAR = [(0, 0, 1), (1, 0, 0), (0, 1, 0), (1, 1, 0)]


def _direct_copies(srcs, lands, send_sems, recv_sems, mode):
    me = _slot(_mesh_pos())
    copies = []
    for a in range(len(lands)):
        if mode == "scatter":
            plan = [(srcs[a].at[_slot(_peer(flip))], lands[a].at[me], _peer(flip)) for flip in PEERS]
        elif mode == "gather_near":
            plan = [(lands[a].at[me], lands[a].at[me], _peer(flip)) for flip in NEAR]
        else:
            origins = [_slot(_peer(flip)) for flip in NEAR[1:]]
            plan = [(lands[a].at[o], lands[a].at[o], _peer(NEAR[0])) for o in origins]
        for k, (src, dst, to) in enumerate(plan):
            idx = a * len(PEERS) + k
            copies.append(pltpu.make_async_remote_copy(
                src_ref=src, dst_ref=dst, send_sem=send_sems.at[idx], recv_sem=recv_sems.at[idx],
                device_id=to, device_id_type=MESH))
    return copies


def _send_start(srcs, lands, after, name, mode):
    ns, nl = len(srcs), len(lands)

    def body(*refs):
        src_refs, land_refs = refs[:ns], refs[ns:ns + nl]
        send_sems, recv_sems = refs[ns + nl + 1:ns + nl + 3]
        token = refs[-1]
        for cp in _direct_copies(src_refs, land_refs, send_sems, recv_sems, mode):
            cp.start()
        token[...] = jnp.zeros_like(token)

    sem = pltpu.SemaphoreType.DMA((nl * len(PEERS),))
    outs = pl.pallas_call(
        body, name=name,
        out_shape=(sem, sem) + tuple(pltpu.HBM(a.shape, a.dtype) for a in list(srcs) + list(lands))
        + (_sds((8, LANES), F32),),
        in_specs=[HBM_SPEC] * (ns + nl) + [pl.BlockSpec(memory_space=pl.ANY)],
        out_specs=(SEM_SPEC, SEM_SPEC) + (HBM_SPEC,) * (ns + nl) + (pl.BlockSpec(memory_space=pltpu.VMEM),),
        input_output_aliases={i: 2 + i for i in range(ns + nl)},
        compiler_params=pltpu.CompilerParams(has_side_effects=DATAFLOW),
    )(*[_in_hbm(a) for a in list(srcs) + list(lands)], after)
    send_sems, recv_sems = outs[0], outs[1]
    return send_sems, recv_sems, list(outs[2:2 + ns]), list(outs[2 + ns:2 + ns + nl]), outs[-1]


def _send_wait(send_sems, recv_sems, srcs, lands, after, name, mode):
    ns, nl = len(srcs), len(lands)

    def body(*refs):
        src_refs, land_refs = refs[:ns], refs[ns:ns + nl]
        send_sems, recv_sems = refs[ns + nl:ns + nl + 2]
        copies = _direct_copies(src_refs, land_refs, send_sems, recv_sems, mode)
        for cp in copies:
            cp.wait_send()
        for cp in copies:
            cp.wait_recv()

    outs = pl.pallas_call(
        body, name=name,
        out_shape=tuple(pltpu.HBM(a.shape, a.dtype) for a in list(srcs) + list(lands)),
        in_specs=[HBM_SPEC] * (ns + nl) + [SEM_SPEC, SEM_SPEC, pl.BlockSpec(memory_space=pl.ANY)],
        out_specs=(HBM_SPEC,) * (ns + nl),
        input_output_aliases={i: i for i in range(ns + nl)},
        compiler_params=pltpu.CompilerParams(has_side_effects=DATAFLOW),
    )(*srcs, *lands, send_sems, recv_sems, after)
    return list(outs[:ns]), list(outs[ns:])


def _row_tile(rows, cap=512):
    t = min(rows, cap)
    while rows % t or (t % 16 and t != rows):
        t -= 1
    return t


def _as2d(a):
    return a.reshape(-1, a.shape[-1])


def _cast_layer(w, layer, slot, name):
    _, rows, cols = w.shape
    tr = _row_tile(rows)

    def body(*refs):
        refs[-1][...] = refs[-2][...].astype(BF16)

    if slot is None:
        return _call(body, name=name, grid=(rows // tr,),
                     in_specs=[pl.BlockSpec((None, tr, cols), lambda i: (layer, i, 0))],
                     out_specs=pl.BlockSpec((tr, cols), lambda i: (i, 0)), out_shape=_sds((rows, cols), BF16))(w)
    return _call(body, name=name, grid=(rows // tr,), prefetch=True,
                 in_specs=[pl.BlockSpec((None, tr, cols), lambda i, s: (layer, i, 0))],
                 out_specs=pl.BlockSpec((None, tr, cols), lambda i, s: (s[0], i, 0)),
                 out_shape=_sds((N_DEV, rows, cols), BF16))(slot, w)


def _sum_adamw(slot, parts, lands, w, m, v, name):
    n_l = len(parts)
    _, rows, cols = lands[0].shape
    tr = _row_tile(rows, 128)

    def body(s_ref, *refs):
        p_refs, l_refs = refs[:n_l], refs[n_l:2 * n_l]
        w_ref, m_ref, v_ref, g_ref, d_ref, nm_ref, nv_ref = refs[2 * n_l:]
        for k in range(n_l):
            @pl.when(pl.program_id(0) == k)
            def _(k=k):
                own = p_refs[k][...]
                g = jnp.zeros((tr, cols), F32)
                for j in range(N_DEV):
                    g = g + jnp.where(s_ref[0] == j, own, l_refs[k][j]).astype(F32)
                delta, nm, nv = _adamw_math(w_ref[...], g, m_ref[...], v_ref[...])
                g_ref[...] = g
                d_ref[...] = delta
                nm_ref[...] = nm
                nv_ref[...] = nv

    def own_block(k):
        return pl.BlockSpec((None, tr, cols), lambda l, i, s: (s[0], jnp.where(l == k, i, 0), 0))

    def zone_block(k):
        return pl.BlockSpec((N_DEV, tr, cols), lambda l, i, s: (0, jnp.where(l == k, i, 0), 0))

    lay = pl.BlockSpec((None, tr, cols), lambda l, i, s: (l, i, 0))
    return _call(body, name=name, grid=(n_l, rows // tr), prefetch=True,
                 in_specs=[own_block(k) for k in range(n_l)] + [zone_block(k) for k in range(n_l)] + [lay, lay, lay],
                 out_specs=[lay] * 4, out_shape=[_sds((n_l, rows, cols), F32)] * 4)(slot, *parts, *lands, w, m, v)


def _adamw_math(w, g, m, v):
    m = ADAM_B1 * m + (1.0 - ADAM_B1) * g
    v = ADAM_B2 * v + (1.0 - ADAM_B2) * (g * g)
    m_hat = m / (1.0 - ADAM_B1 ** ADAM_STEP)
    v_hat = v / (1.0 - ADAM_B2 ** ADAM_STEP)
    delta = -ADAM_LR * (m_hat / (jnp.sqrt(v_hat) + ADAM_EPS) + ADAM_WD * w)
    return delta, m, v


def _small_adamw(gathered, w, m, v, name):
    def body(a_ref, w_ref, m_ref, v_ref, g_ref, d_ref, nm_ref, nv_ref):
        g = a_ref[0]
        for k in range(1, N_DEV):
            g = g + a_ref[k]
        delta, nm, nv = _adamw_math(w_ref[...], g, m_ref[...], v_ref[...])
        g_ref[...] = g
        d_ref[...] = delta
        nm_ref[...] = nm
        nv_ref[...] = nv

    return _call(body, name=name, out_shape=[_sds(w.shape, F32)] * 4)(gathered, w, m, v)


def _norm_matmul_cols(x, g, wg, layer, mode, name):
    t, d = x.shape
    nb = wg.shape[-1]
    tm = BIG_ROW_TILE
    per = 2
    pieces = per * nb // LANES

    def body(x_ref, g_ref, w_ref, y_ref, xnt_ref):
        xn = _rms(x_ref[...], g_ref[...])
        xnt_ref[...] = xn.T.astype(BF16)
        xn = xn.astype(BF16)
        for j in range(N_DEV // per):
            y = _dot(xn, jnp.concatenate([w_ref[per * j + k] for k in range(per)], axis=1))
            if mode == "heads":
                for p in range(pieces):
                    y_ref[pieces * j + p] = y[:, p * LANES:(p + 1) * LANES]
            else:
                y_ref[:, j * per * nb:(j + 1) * per * nb] = y.astype(BF16)

    if mode == "cols":
        y_shape, y_spec = _sds((t, N_DEV * nb), BF16), pl.BlockSpec((tm, N_DEV * nb), lambda i: (i, 0))
    else:
        y_shape = _sds((N_DEV // per * pieces, t, LANES), F32)
        y_spec = pl.BlockSpec((N_DEV // per * pieces, tm, LANES), lambda i: (0, i, 0))
    return _call(
        body, name=name, grid=(t // tm,),
        in_specs=[pl.BlockSpec((tm, d), lambda i: (i, 0)), pl.BlockSpec((1, d), lambda i: (0, 0)),
                  pl.BlockSpec((N_DEV, None, d, nb), lambda i: (0, layer, 0, 0))],
        out_specs=[y_spec, pl.BlockSpec((d, tm), lambda i: (0, i))],
        out_shape=[y_shape, _sds((d, t), BF16)])(x, g, wg)


def _ffn_in_swiglu(x, g, wg, layer, name):
    t, d = x.shape
    fc = wg.shape[-2]
    tm = ROW_TILE

    def body(x_ref, g_ref, w_ref, gate_ref, up_ref, a_ref, xn_ref):
        xn = _rms(x_ref[...], g_ref[...]).astype(BF16)
        xn_ref[...] = xn
        for c in range(4):
            gate, up = _dot_nt(xn, w_ref[c]), _dot_nt(xn, w_ref[c + 4])
            gate_ref[c] = gate.astype(BF16)
            up_ref[c] = up.astype(BF16)
            a_ref[c] = (gate * jax.nn.sigmoid(gate) * up).astype(BF16)

    chunks = pl.BlockSpec((4, tm, fc), lambda i: (0, i, 0))
    return _call(
        body, name=name, grid=(t // tm,),
        in_specs=[pl.BlockSpec((tm, d), lambda i: (i, 0)), pl.BlockSpec((1, d), lambda i: (0, 0)),
                  pl.BlockSpec((N_DEV, None, fc, d), lambda i: (0, layer, 0, 0))],
        out_specs=[chunks, chunks, chunks, pl.BlockSpec((tm, d), lambda i: (i, 0))],
        out_shape=[_sds((4, t, fc), BF16)] * 3 + [_sds((t, d), BF16)])(x, g, wg)


def _norm_matmul_heads(x, g, wg, layer, scale, name):
    t, d = x.shape
    tm = ROW_TILE
    hp = d // LANES

    def body(x_ref, g_ref, w_ref, y_ref, xn_ref):
        xn = _rms(x_ref[...], g_ref[...]).astype(BF16)
        xn_ref[...] = xn
        y = _dot(xn, w_ref[...].reshape(d, d)) * scale
        for p in range(hp):
            y_ref[p] = y[:, p * LANES:(p + 1) * LANES]

    return _call(
        body, name=name, grid=(t // tm,),
        in_specs=[pl.BlockSpec((tm, d), lambda i: (i, 0)), pl.BlockSpec((1, d), lambda i: (0, 0)),
                  pl.BlockSpec((N_DEV, None, d // N_DEV, d), lambda i: (0, layer, 0, 0))],
        out_specs=[pl.BlockSpec((hp, tm, LANES), lambda i: (0, i, 0)), pl.BlockSpec((tm, d), lambda i: (i, 0))],
        out_shape=[_sds((hp, t, LANES), F32), _sds((t, d), BF16)])(x, g, wg)


def _shift_down(u, halo, k, tm):
    row = lax.broadcasted_iota(jnp.int32, u.shape, 0)
    out = pltpu.roll(u, k, 0)
    for j in range(k):
        out = jnp.where(row == j, halo[halo.shape[0] - k + j:halo.shape[0] - k + j + 1, :], out)
    return out


def _shift_up(u, halo, k, tm):
    row = lax.broadcasted_iota(jnp.int32, u.shape, 0)
    out = pltpu.roll(u, tm - k, 0)
    for j in range(k):
        out = jnp.where(row == tm - k + j, halo[j:j + 1, :], out)
    return out


HALO = 16


def _conv_fwd(p, cw, name):
    t, d3 = p.shape
    d = d3 // 3
    tm = ROW_TILE
    hb = tm // HALO

    def body(p_ref, prev_ref, cw_ref, z_ref):
        i = pl.program_id(0)
        b = p_ref[:, 0:d].astype(F32)
        u = p_ref[:, d:2 * d].astype(F32) * p_ref[:, 2 * d:3 * d].astype(F32)
        keep = (i > 0).astype(F32)
        hu = prev_ref[:, d:2 * d].astype(F32) * prev_ref[:, 2 * d:3 * d].astype(F32) * keep
        uc = cw_ref[2:3, :] * u + cw_ref[1:2, :] * _shift_down(u, hu, 1, tm) + cw_ref[0:1, :] * _shift_down(u, hu, 2, tm)
        z_ref[...] = (b * uc).astype(BF16)

    return _call(
        body, name=name, grid=(t // tm,),
        in_specs=[pl.BlockSpec((tm, d3), lambda i: (i, 0)),
                  pl.BlockSpec((HALO, d3), lambda i: (jnp.maximum(i * hb - 1, 0), 0)),
                  pl.BlockSpec((3, d), lambda i: (0, 0))],
        out_specs=pl.BlockSpec((tm, d), lambda i: (i, 0)), out_shape=_sds((t, d), BF16))(p, p, cw)


def _matmul_norm_residual(a3, wg, layer, g, x_res, name):
    kc_n, t, kc = a3.shape
    d = wg.shape[-1]
    per = N_DEV // kc_n
    rows = wg.shape[2]
    tm = ROW_TILE

    def body(a_ref, w_ref, g_ref, x_ref, raw_ref, xo_ref):
        raw = None
        for c in range(kc_n):
            term = _dot(a_ref[c], w_ref[c * per:(c + 1) * per].reshape(per * rows, d))
            raw = term if raw is None else raw + term
        raw_ref[...] = raw.astype(BF16)
        xo_ref[...] = x_ref[...] + _rms(raw, g_ref[...])

    row_spec = pl.BlockSpec((tm, d), lambda i: (i, 0))
    return _call(
        body, name=name, grid=(t // tm,),
        in_specs=[pl.BlockSpec((kc_n, tm, kc), lambda i: (0, i, 0)),
                  pl.BlockSpec((N_DEV, None, rows, d), lambda i: (0, layer, 0, 0)),
                  pl.BlockSpec((1, d), lambda i: (0, 0)), row_spec],
        out_specs=[row_spec, row_spec], out_shape=[_sds((t, d), BF16), _sds((t, d), F32)])(a3, wg, g, x_res)


def _alibi_slopes(n_heads):
    hh = np.arange(n_heads, dtype=np.float32) + 1.0
    s = np.power(2.0, -8.0 * hh / n_heads).astype(np.float32)
    return jnp.asarray(np.repeat(s.reshape(n_heads // 2, 2, 1), 2 * ATT_BLOCK, axis=2))


def _band_bias(sl_ref, dil):
    u = lax.broadcasted_iota(jnp.int32, (ATT_BLOCK, 2 * ATT_BLOCK), 0)
    kk = lax.broadcasted_iota(jnp.int32, (ATT_BLOCK, 2 * ATT_BLOCK), 1)
    delta = u + ATT_BLOCK - kk
    valid = (delta >= 0) & (delta <= ATT_BLOCK)
    dist = (delta * dil).astype(F32)
    rows = [jnp.where(valid, -sl_ref[hd:hd + 1, :] * dist, NEG) for hd in range(2)]
    return jnp.concatenate(rows, axis=0)


def _stack_heads(a):
    lane = lax.broadcasted_iota(jnp.int32, a.shape, 1)
    return jnp.concatenate([jnp.where(lane < HEAD_DIM, a, 0.0), jnp.where(lane >= HEAD_DIM, a, 0.0)], axis=0).astype(BF16)


def _unstack_heads(a2):
    top, bot = a2[:ATT_BLOCK], a2[ATT_BLOCK:]
    lane = lax.broadcasted_iota(jnp.int32, top.shape, 1)
    return jnp.where(lane < HEAD_DIM, top, bot)


def _rows_to_lanes(a0, a1):
    eye = lax.broadcasted_iota(jnp.int32, a0.shape, 0) == lax.broadcasted_iota(jnp.int32, a0.shape, 1)
    return jnp.concatenate([jnp.sum(jnp.where(eye, a, 0.0), axis=0, keepdims=True) for a in (a0, a1)], axis=1)


def _fill_bias_t(sl_ref, bias_ref):
    kk = lax.broadcasted_iota(jnp.int32, (2 * ATT_BLOCK, 2 * ATT_BLOCK), 0)
    lane = lax.broadcasted_iota(jnp.int32, (2 * ATT_BLOCK, 2 * ATT_BLOCK), 1)
    delta = lane % ATT_BLOCK + ATT_BLOCK - kk
    valid = (delta >= 0) & (delta <= ATT_BLOCK)
    slope = jnp.concatenate([sl_ref[0:1, :ATT_BLOCK], sl_ref[1:2, :ATT_BLOCK]], axis=1)
    for gi, dil in enumerate(DILATIONS):
        bias = jnp.where(valid, -slope * (delta * dil).astype(F32), NEG)
        bias_ref[2 * gi] = bias
        bias_ref[2 * gi + 1] = jnp.where(kk < ATT_BLOCK, NEG, bias)


def _fill_bias(sl_ref, bias_ref):
    kk = lax.broadcasted_iota(jnp.int32, (2 * ATT_BLOCK, 2 * ATT_BLOCK), 1)
    for gi, dil in enumerate(DILATIONS):
        bias = _band_bias(sl_ref, dil)
        bias_ref[2 * gi] = bias
        bias_ref[2 * gi + 1] = jnp.where(kk < ATT_BLOCK, NEG, bias)


def _strided_keys(dil, r, b, kc_ref, kp_ref, vc_ref, vp_ref, kcar_ref, vcar_ref):
    if b > 0:
        keys = pl.ds((b - 1) * (ATT_BLOCK * dil) + r, 2 * ATT_BLOCK, stride=dil)
        return kc_ref[keys, :].astype(BF16), vc_ref[keys, :].astype(BF16)
    own = pl.ds(r, ATT_BLOCK, stride=dil)
    k_own, v_own = kc_ref[own, :].astype(BF16), vc_ref[own, :].astype(BF16)
    if dil * ATT_BLOCK != SUPER:
        before = pl.ds(SUPER - ATT_BLOCK * dil + r, ATT_BLOCK, stride=dil)
        k_before, v_before = kp_ref[before, :].astype(BF16), vp_ref[before, :].astype(BF16)
    else:
        k_before, v_before = kcar_ref[r], vcar_ref[r]
        kcar_ref[r] = k_own
        vcar_ref[r] = v_own
    return jnp.concatenate([k_before, k_own], axis=0), jnp.concatenate([v_before, v_own], axis=0)


def _attention_fwd(q, kv, slopes, name):
    hp, t, _ = q.shape
    ns = t // SUPER
    nd = len(DILATIONS)

    def body(sl_ref, q_ref, kc_ref, kp_ref, vc_ref, vp_ref, o_ref, lse_ref, og_ref, lg_ref, bias_ref, kcar_ref, vcar_ref):
        n = pl.program_id(1)

        @pl.when(n == 0)
        def _():
            _fill_bias(sl_ref, bias_ref)
            kcar_ref[...] = jnp.zeros_like(kcar_ref)
            vcar_ref[...] = jnp.zeros_like(vcar_ref)

        for gi, dil in enumerate(DILATIONS):
            for idx in range(SUPER // ATT_BLOCK):
                r, b = idx % dil, idx // dil
                qs = b * (ATT_BLOCK * dil) + r
                first = (n == 0).astype(jnp.int32) if b == 0 else 0
                q2 = _stack_heads(q_ref[pl.ds(qs, ATT_BLOCK, stride=dil), :])
                kb, vb = _strided_keys(dil, r, b, kc_ref, kp_ref, vc_ref, vp_ref, kcar_ref, vcar_ref)
                s = _dot_nt(q2, kb) + bias_ref[2 * gi + first]
                m = jnp.max(s, axis=-1, keepdims=True)
                p = jnp.exp(s - m).astype(BF16)
                ol = _dot(p, jnp.concatenate([vb, jnp.ones_like(vb)], axis=1))
                rows = pl.ds(qs, ATT_BLOCK, stride=dil)
                og_ref[gi, rows, :] = _unstack_heads(ol[:, :LANES])
                lg_ref[gi, rows, :] = _unstack_heads(ol[:, LANES:])
                lg_ref[nd + gi, rows, :] = _unstack_heads(m + jnp.zeros((2 * ATT_BLOCK, LANES), F32))
        ms = [lg_ref[nd + gi] for gi in range(nd)]
        top = functools.reduce(jnp.maximum, ms)
        ws = [jnp.exp(x - top) for x in ms]
        tot = ws[0] * lg_ref[0]
        acc = ws[0] * og_ref[0]
        for gi in range(1, nd):
            tot = tot + ws[gi] * lg_ref[gi]
            acc = acc + ws[gi] * og_ref[gi]
        lse_ref[...] = top + jnp.log(tot)
        o_ref[...] = (acc / tot).astype(BF16)

    cur = lambda off: pl.BlockSpec((None, SUPER, LANES), lambda h, n: (h + off, n, 0))
    prev = lambda off: pl.BlockSpec((None, SUPER, LANES), lambda h, n: (h + off, jnp.maximum(n - 1, 0), 0))
    return _call(
        body, name=name, grid=(hp, ns),
        in_specs=[pl.BlockSpec((None, 2, 2 * ATT_BLOCK), lambda h, n: (h, 0, 0)), cur(0), cur(0), prev(0), cur(hp), prev(hp)],
        out_specs=[pl.BlockSpec((SUPER, LANES), lambda h, n: (n, h)), cur(0)],
        out_shape=[_sds((t, hp * LANES), BF16), _sds((hp, t, LANES), F32)],
        scratch_shapes=[pltpu.VMEM((nd, SUPER, LANES), F32), pltpu.VMEM((2 * nd, SUPER, LANES), F32),
                        pltpu.VMEM((2 * nd, 2 * ATT_BLOCK, 2 * ATT_BLOCK), F32)] + [
                            pltpu.VMEM((DILATIONS[-1], ATT_BLOCK, LANES), BF16)] * 2,
    )(slopes, q, kv, kv, kv, kv)


def _attention_bwd(q, kv, o, lse, d_o, dk_in, dv_in, slopes, q_scale, name):
    hp, t, _ = q.shape
    ns = t // SUPER
    shared = dk_in is not None

    def body(sl_ref, q_ref, kc_ref, kp_ref, vc_ref, vp_ref, o_ref, lse_ref, do_ref, *rest):
        dki_ref, dvi_ref = rest[:2] if shared else (None, None)
        dq_ref, dk_ref, dv_ref, dkw_ref, dvw_ref, st_ref, bias_ref, kcar_ref, vcar_ref = rest[2 if shared else 0:]
        n = pl.program_id(1)

        @pl.when(n == 0)
        def _():
            dkw_ref[...] = jnp.zeros_like(dkw_ref)
            dvw_ref[...] = jnp.zeros_like(dvw_ref)

        @pl.when(n > 0)
        def _():
            dkw_ref[0:SUPER, :] = dkw_ref[SUPER:, :]
            dvw_ref[0:SUPER, :] = dvw_ref[SUPER:, :]
            dkw_ref[SUPER:, :] = jnp.zeros((SUPER, LANES), F32)
            dvw_ref[SUPER:, :] = jnp.zeros((SUPER, LANES), F32)

        @pl.when(n < ns)
        def _():
            prod = do_ref[...] * o_ref[...].astype(F32)
            lane = lax.broadcasted_iota(jnp.int32, prod.shape, 1)
            zero = jnp.zeros((SUPER, LANES), F32)
            st_ref[0] = zero + jnp.sum(jnp.where(lane < HEAD_DIM, prod, 0.0), axis=-1, keepdims=True)
            st_ref[1] = zero + jnp.sum(jnp.where(lane >= HEAD_DIM, prod, 0.0), axis=-1, keepdims=True)
            lse = lse_ref[...]
            swapped = pltpu.roll(lse, HEAD_DIM, 1)
            st_ref[2] = jnp.where(lane < HEAD_DIM, lse, swapped)
            st_ref[3] = jnp.where(lane >= HEAD_DIM, lse, swapped)
            dq_ref[...] = jnp.zeros_like(dq_ref)

            @pl.when(n == 0)
            def _():
                _fill_bias_t(sl_ref, bias_ref)
                kcar_ref[...] = jnp.zeros_like(kcar_ref)
                vcar_ref[...] = jnp.zeros_like(vcar_ref)

            for gi, dil in enumerate(DILATIONS):
                for idx in range(SUPER // ATT_BLOCK):
                    r, b = idx % dil, idx // dil
                    qs = b * (ATT_BLOCK * dil) + r
                    ks = SUPER + (b - 1) * (ATT_BLOCK * dil) + r
                    first = (n == 0).astype(jnp.int32) if b == 0 else 0
                    rows = pl.ds(qs, ATT_BLOCK, stride=dil)
                    keys = pl.ds(ks, 2 * ATT_BLOCK, stride=dil)
                    q2 = _stack_heads(q_ref[rows, :])
                    do2 = _stack_heads(do_ref[rows, :])
                    kb, vb = _strided_keys(dil, r, b, kc_ref, kp_ref, vc_ref, vp_ref, kcar_ref, vcar_ref)
                    dd = _rows_to_lanes(st_ref[0, rows, :], st_ref[1, rows, :])
                    lse_b = _rows_to_lanes(st_ref[2, rows, :], st_ref[3, rows, :])
                    ps, dss = [], []
                    for half in range(2):
                        hk = slice(half * ATT_BLOCK, (half + 1) * ATT_BLOCK)
                        p = jnp.exp(_dot_nt(kb[hk], q2) + bias_ref[2 * gi + first, hk, :] - lse_b)
                        dss.append((p * (_dot_nt(vb[hk], do2) - dd)).astype(BF16))
                        ps.append(p.astype(BF16))
                    p, ds = jnp.concatenate(ps, axis=0), jnp.concatenate(dss, axis=0)
                    dvw_ref[keys, :] += _dot(p, do2)
                    dkw_ref[keys, :] += _dot(ds, q2)
                    dq_ref[rows, :] += _unstack_heads(_dot_tn(ds, kb)) * q_scale

        dk_ref[...] =dkw_ref[0:SUPER, :] + dki_ref[...] if shared else dkw_ref[0:SUPER, :]
        dv_ref[...] = dvw_ref[0:SUPER, :] + dvi_ref[...] if shared else dvw_ref[0:SUPER, :]

    last = ns - 1
    cur = lambda off: pl.BlockSpec((None, SUPER, LANES), lambda h, n: (h + off, jnp.minimum(n, last), 0))
    prev = lambda off: pl.BlockSpec((None, SUPER, LANES), lambda h, n: (h + off, jnp.clip(n - 1, 0, last), 0))
    nat = pl.BlockSpec((SUPER, LANES), lambda h, n: (jnp.minimum(n, last), h))
    late = pl.BlockSpec((None, SUPER, LANES), lambda h, n: (h, jnp.maximum(n - 1, 0), 0))
    dq, dk, dv = _call(
        body, name=name, grid=(hp, ns + 1),
        in_specs=[pl.BlockSpec((None, 2, 2 * ATT_BLOCK), lambda h, n: (h, 0, 0)), cur(0), cur(0), prev(0), cur(hp), prev(hp),
                  nat, cur(0), nat] + ([late, late] if shared else []),
        out_specs=[cur(0), late, late],
        out_shape=[_sds((hp, t, LANES), F32)] * 3,
        scratch_shapes=[pltpu.VMEM((2 * SUPER, LANES), F32)] * 2 + [
            pltpu.VMEM((4, SUPER, LANES), F32), pltpu.VMEM((2 * len(DILATIONS), 2 * ATT_BLOCK, 2 * ATT_BLOCK), F32)] + [
                pltpu.VMEM((DILATIONS[-1], ATT_BLOCK, LANES), BF16)] * 2,
    )(slopes, q, kv, kv, kv, kv, o, lse, d_o, *((dk_in, dv_in) if shared else ()))
    return dq, dk, dv


def _loss_head(y, target, raw, g, name):
    t, d = y.shape
    tm = ROW_TILE

    def body(y_ref, t_ref, raw_ref, g_ref, sq_ref, dy_ref, draw_ref, dg_ref):
        i = pl.program_id(0)
        err = y_ref[...] - t_ref[...]
        dy = err * (1.0 / d)
        dy_ref[...] = dy
        draw, dg = _rms_bwd(raw_ref[...].astype(F32), g_ref[...], dy)
        draw_ref[...] = draw.astype(BF16)
        sq = jnp.zeros((8, LANES), F32) + jnp.sum(err * err)

        @pl.when(i == 0)
        def _():
            sq_ref[...] = sq
            dg_ref[...] = dg

        @pl.when(i > 0)
        def _():
            sq_ref[...] += sq
            dg_ref[...] += dg

    row = pl.BlockSpec((tm, d), lambda i: (i, 0))
    vec = pl.BlockSpec((1, d), lambda i: (0, 0))
    return _call(
        body, name=name, grid=(t // tm,), in_specs=[row, row, row, vec],
        out_specs=[pl.BlockSpec((8, LANES), lambda i: (0, 0)), row, row, vec],
        out_shape=[_sds((8, LANES), F32), _sds((t, d), F32), _sds((t, d), BF16), _sds((1, d), F32)])(y, target, raw, g)


def _bwd_matmul_norms(a_specs, a_args, a_tile, n_steps, w_spec, w_arg, w_mat, xa, ga, resid, xb, gb, name,
                      w_transposed=False):
    t, d = xa.shape
    tm = BWD_ROW_TILE
    na = len(a_specs)
    second = xb is not None
    per = 8 if n_steps % 8 == 0 else 1
    n_steps //= per

    def blocks_of(spec, k):
        return pl.BlockSpec(spec.block_shape, lambda i, j: spec.index_map(i, per * j + k))

    def body(*refs):
        a_refs, w_refs = refs[:per * na], refs[per * na:per * na + per]
        xa_ref, ga_ref, res_ref = refs[per * na + per:per * na + per + 3]
        rest = refs[per * na + per + 3:]
        if second:
            xb_ref, gb_ref, dx_ref, d2_ref, dga_ref, dgb_ref, acc_ref = rest
        else:
            dx_ref, dga_ref, acc_ref = rest
        i, j = pl.program_id(0), pl.program_id(1)
        part = None
        for k in range(per):
            term = (_dot if w_transposed else _dot_nt)(a_tile(per * j + k, *a_refs[k * na:(k + 1) * na]), w_mat(w_refs[k]))
            part = term if part is None else part + term

        @pl.when(j == 0)
        def _():
            acc_ref[...] = part

        @pl.when(j > 0)
        def _():
            acc_ref[...] += part

        @pl.when(j == n_steps - 1)
        def _():
            da, dga = _rms_bwd(xa_ref[...], ga_ref[...], acc_ref[...])
            dx = res_ref[...] + da
            dx_ref[...] = dx
            if second:
                d2, dgb = _rms_bwd(xb_ref[...].astype(F32), gb_ref[...], dx)
                d2_ref[...] = d2.astype(BF16)

            @pl.when(i == 0)
            def _():
                dga_ref[...] = dga
                if second:
                    dgb_ref[...] = dgb

            @pl.when(i > 0)
            def _():
                dga_ref[...] += dga
                if second:
                    dgb_ref[...] += dgb

    row = pl.BlockSpec((tm, d), lambda i, j: (i, 0))
    vec = pl.BlockSpec((1, d), lambda i, j: (0, 0))
    in_specs = [blocks_of(sp, k) for k in range(per) for sp in a_specs] + [blocks_of(w_spec, k) for k in range(per)]
    in_specs += [row, vec, row]
    args = list(a_args) * per + [w_arg] * per + [xa, ga, resid]
    if second:
        in_specs += [row, vec]
        args += [xb, gb]
        out_specs = [row, row, vec, vec]
        out_shape = [_sds((t, d), F32), _sds((t, d), BF16), _sds((1, d), F32), _sds((1, d), F32)]
    else:
        out_specs = [row, vec]
        out_shape = [_sds((t, d), F32), _sds((1, d), F32)]
    return _call(body, name=name, grid=(t // tm, n_steps), in_specs=in_specs, out_specs=out_specs,
                 out_shape=out_shape, scratch_shapes=[pltpu.VMEM((tm, d), F32)])(*args)


def _heads_to_rows(*refs):
    hp = refs[0].shape[0]
    cols = []
    for p in range(hp):
        v = refs[0][p]
        for r in refs[1:]:
            v = v + r[p]
        cols.append(v)
    return jnp.concatenate(cols, axis=-1).astype(BF16)


def _matmul_nt_rows(a, wg, layer, out_dtype, name):
    t, d = a.shape
    tm = ROW_TILE

    def body(a_ref, w_ref, o_ref):
        o_ref[...] = _dot_nt(a_ref[...], w_ref[...].reshape(d, d)).astype(out_dtype)

    row = pl.BlockSpec((tm, d), lambda i: (i, 0))
    return _call(body, name=name, grid=(t // tm,),
                 in_specs=[row, pl.BlockSpec((N_DEV, None, d // N_DEV, d), lambda i: (0, layer, 0, 0))],
                 out_specs=row, out_shape=_sds((t, d), out_dtype))(a, wg)


def _swiglu_bwd(d_ff, wg, layer, gate, up, name):
    t, d = d_ff.shape
    fc = gate.shape[-1]
    rows = wg.shape[2]
    tm = BIG_ROW_TILE

    def body(df_ref, w_ref, g_ref, u_ref, dh_ref):
        w = w_ref[...].reshape(2 * rows, d)
        for r0 in range(0, tm, SWIGLU_ROWS):
            rs = slice(r0, r0 + SWIGLU_ROWS)
            da = _dot_nt(df_ref[rs, :], w)
            gate, up = g_ref[rs, :].astype(F32), u_ref[rs, :].astype(F32)
            sig = jax.nn.sigmoid(gate)
            dh_ref[0, rs, :] = (da * up * (sig * (1.0 + gate * (1.0 - sig)))).astype(BF16)
            dh_ref[1, rs, :] = (da * (gate * sig)).astype(BF16)

    return _call(
        body, name=name, grid=(t // tm, 4),
        in_specs=[pl.BlockSpec((tm, d), lambda i, c: (i, 0)),
                  pl.BlockSpec((2, None, rows, d), lambda i, c: (c, layer, 0, 0)),
                  pl.BlockSpec((None, tm, fc), lambda i, c: (c, i, 0)),
                  pl.BlockSpec((None, tm, fc), lambda i, c: (c, i, 0))],
        out_specs=pl.BlockSpec((None, 2, tm, fc), lambda i, c: (c, 0, i, 0)),
        out_shape=_sds((4, 2, t, fc), BF16))(d_ff, wg, gate, up)


def _conv_bwd(p, d_z, cw, name):
    t, d3 = p.shape
    d = d3 // 3
    tm = ROW_TILE
    hb = tm // HALO
    nt = t // tm

    def body(p_ref, prev_ref, next_ref, dz_ref, dzn_ref, cw_ref, dp_ref, dcw_ref):
        i = pl.program_id(0)
        b = p_ref[:, 0:d].astype(F32)
        c = p_ref[:, d:2 * d].astype(F32)
        h = p_ref[:, 2 * d:3 * d].astype(F32)
        u = c * h
        hu = prev_ref[:, d:2 * d].astype(F32) * prev_ref[:, 2 * d:3 * d].astype(F32) * (i > 0).astype(F32)
        u1, u2 = _shift_down(u, hu, 1, tm), _shift_down(u, hu, 2, tm)
        uc = cw_ref[2:3, :] * u + cw_ref[1:2, :] * u1 + cw_ref[0:1, :] * u2
        dz = dz_ref[...].astype(F32)
        duc = dz * b
        dn = dzn_ref[...].astype(F32) * next_ref[:, 0:d].astype(F32) * (i < nt - 1).astype(F32)
        du = cw_ref[2:3, :] * duc + cw_ref[1:2, :] * _shift_up(duc, dn, 1, tm) + cw_ref[0:1, :] * _shift_up(duc, dn, 2, tm)
        dp_ref[:, 0:d] = (dz * uc).astype(BF16)
        dp_ref[:, d:2 * d] = (du * h).astype(BF16)
        dp_ref[:, 2 * d:3 * d] = (du * c).astype(BF16)
        dcw = jnp.concatenate([jnp.sum(duc * u2, axis=0, keepdims=True), jnp.sum(duc * u1, axis=0, keepdims=True),
                               jnp.sum(duc * u, axis=0, keepdims=True)], axis=0)

        @pl.when(i == 0)
        def _():
            dcw_ref[...] = dcw

        @pl.when(i > 0)
        def _():
            dcw_ref[...] += dcw

    last_halo = t // HALO - 1
    return _call(
        body, name=name, grid=(nt,),
        in_specs=[pl.BlockSpec((tm, d3), lambda i: (i, 0)),
                  pl.BlockSpec((HALO, d3), lambda i: (jnp.maximum(i * hb - 1, 0), 0)),
                  pl.BlockSpec((HALO, d3), lambda i: (jnp.minimum((i + 1) * hb, last_halo), 0)),
                  pl.BlockSpec((tm, d), lambda i: (i, 0)),
                  pl.BlockSpec((HALO, d), lambda i: (jnp.minimum((i + 1) * hb, last_halo), 0)),
                  pl.BlockSpec((3, d), lambda i: (0, 0))],
        out_specs=[pl.BlockSpec((tm, d3), lambda i: (i, 0)), pl.BlockSpec((3, d), lambda i: (0, 0))],
        out_shape=[_sds((t, d3), BF16), _sds((3, d), F32)])(p, p, p, d_z, d_z, cw)


def _grad_weight(a_specs, a_args, a_tile, b_specs, b_args, b_tile, n_out, acc_shape, out_spec, out_shape, t, name,
                 a_transposed=False, tt=GRAD_ROW_TILE):
    na, nb = len(a_specs), len(b_specs)

    def body(*refs):
        a_refs, b_refs = refs[:na], refs[na:na + nb]
        o_ref, acc_ref = refs[na + nb:]
        s = pl.program_id(1)
        a, b = a_tile(pl.program_id(0), *a_refs), b_tile(pl.program_id(0), *b_refs)
        part = _dot(a, b) if a_transposed else _dot_tn(a, b)

        @pl.when(s == 0)
        def _():
            acc_ref[...] = part

        @pl.when(s > 0)
        def _():
            acc_ref[...] += part

        @pl.when(s == t // tt - 1)
        def _():
            acc = acc_ref[...].astype(BF16)
            if o_ref.shape[-1] == acc.shape[-1]:
                o_ref[...] = acc.reshape(o_ref.shape)
            else:
                for k in range(o_ref.shape[0]):
                    o_ref[k] = acc[:, k * o_ref.shape[-1]:(k + 1) * o_ref.shape[-1]]

    return _call(body, name=name, grid=(n_out, t // tt), in_specs=list(a_specs) + list(b_specs), out_specs=out_spec,
                 out_shape=out_shape, scratch_shapes=[pltpu.VMEM(acc_shape, F32)])(*a_args, *b_args)


def _ident(*args):
    return args[-1][...]


def _heads_tile(j, *refs):
    return _heads_to_rows(*refs)


def kernel(x, norm_g, conv_in_w, conv_w, conv_out_w, kv_norm_g, kv_w, q_w, o_w, ffn_in_w, ffn_out_w, loss_target, m_norm_g, m_conv_in_w, m_conv_w, m_conv_out_w, m_kv_norm_g, m_kv_w, m_q_w, m_o_w, m_ffn_in_w, m_ffn_out_w, v_norm_g, v_conv_in_w, v_conv_w, v_conv_out_w, v_kv_norm_g, v_kv_w, v_q_w, v_o_w, v_ffn_in_w, v_ffn_out_w):
    x0 = x[0]
    target = loss_target[0]
    t, d = x0.shape
    depth = norm_g.shape[0]
    n_a = conv_in_w.shape[0]
    n_b = q_w.shape[0]
    hp = d // LANES
    tm, tg, tgb = BWD_ROW_TILE, GRAD_ROW_TILE, min(t, 2 * GRAD_ROW_TILE)
    assert t % SUPER == 0 and d % LANES == 0 and depth == n_a + n_b
    dev = 4 * lax.axis_index("x") + 2 * lax.axis_index("y") + lax.axis_index("c")

    n_small = 4 * depth + 3 * n_a
    small_rows = -(-(n_small + 1) // 8) * 8
    small_local = jnp.concatenate([norm_g.reshape(4 * depth, -1), conv_w.reshape(3 * n_a, -1),
                                   jnp.zeros((small_rows - n_small, norm_g.shape[-1]), F32)], axis=0)
    swap = lambda a: jnp.swapaxes(a, 1, 2)
    big = {"conv_in_w": conv_in_w, "conv_out_w": conv_out_w, "kv_w": kv_w[None], "q_w": q_w, "o_w": o_w,
           "ffn_in_w": swap(ffn_in_w), "ffn_out_w": ffn_out_w}
    names = list(big)

    def group(layer):
        if layer < n_a:
            return [("conv_in_w", layer), ("conv_out_w", layer), ("ffn_in_w", layer), ("ffn_out_w", layer)]
        j = layer - n_a
        return ([("kv_w", 0)] if j == 0 else []) + [("q_w", j), ("o_w", j), ("ffn_in_w", layer), ("ffn_out_w", layer)]

    slot = dev.astype(jnp.int32).reshape(1)
    is_ffn = lambda key: key[0].startswith("ffn")
    first_keys = [key for key in group(0) if not is_ffn(key)]
    first = _all_gather([small_local] + [_cast_layer(big[k], i, None, f"cast_{k}_{i}") for k, i in first_keys], "gather_weights")
    small_all = first[0].transpose(1, 0, 2).reshape(small_rows, d)
    wl = {key: a[:, None] for key, a in zip(first_keys, first[1:])}

    def gather_start(keys, after, tag):
        lands = [_cast_layer(big[k], i, slot, f"cast_{k}_{i}") for k, i in keys]
        send_sems, recv_sems, _, lands, tok = _send_start([], lands, after, "gather_near_start" + tag, "gather_near")
        return (keys, tag, send_sems, recv_sems, lands), tok[0, 0]

    def gather_mid(flight, after):
        keys, tag, send_sems, recv_sems, lands = flight
        _, lands = _send_wait(send_sems, recv_sems, [], lands, after, "gather_near_wait" + tag, "gather_near")
        send_sems, recv_sems, _, lands, tok = _send_start([], lands, after, "gather_far_start" + tag, "gather_far")
        return (keys, tag, send_sems, recv_sems, lands), tok[0, 0]

    def gather_wait(flight, after):
        keys, tag, send_sems, recv_sems, lands = flight
        _, lands = _send_wait(send_sems, recv_sems, [], lands, after, "gather_far_wait" + tag, "gather_far")
        wl.update({key: a[:, None] for key, a in zip(keys, lands)})

    in_flight, token = gather_start([key for key in group(0) if is_ffn(key)], small_all, "_l0")
    W = lambda k, i: (wl[(k, i)], 0)
    gain = lambda layer, k: small_all[4 * layer + k][None]
    taps = lambda layer: small_all[4 * depth + 3 * layer: 4 * depth + 3 * layer + 3]
    g_kv = kv_norm_g[None]
    slopes = _alibi_slopes(d // HEAD_DIM)
    fc = big["ffn_in_w"].shape[-2]
    cb = big["conv_in_w"].shape[-1]
    kvb = big["kv_w"].shape[-1]
    q_scale = HEAD_DIM ** -0.5

    saved = []
    kv = kvn_t = None
    xs = x0
    for layer in range(depth):
        tag = f"_l{layer}"
        g0 = g1 = g2 = g3 = 0.0
        if layer == 0:
            g0 = token
        else:
            gather_wait(in_flight, xs)
            if layer + 1 < depth:
                in_flight, g0 = gather_start(group(layer + 1), xs, f"_l{layer + 1}")
        s = {"x_in": xs}
        g0 = gain(layer, 0) + g0
        if layer < n_a:
            s["p"], s["xn_t"] = _norm_matmul_cols(xs, g0, *W("conv_in_w", layer), "cols", "conv_in" + tag)
            if layer == 0:
                in_flight, g1 = gather_mid(in_flight, s["p"])
            s["z"] = _conv_fwd(s["p"], taps(layer) + g1, "conv" + tag)
            s["mix"], x_mid = _matmul_norm_residual(s["z"][None], *W("conv_out_w", layer), gain(layer, 1), xs, "conv_out" + tag)
        else:
            j = layer - n_a
            if kv is None:
                kv, kvn_t = _norm_matmul_cols(xs, g_kv, *W("kv_w", 0), "heads", "kv_proj")
            s["q"], s["xn"] = _norm_matmul_heads(xs, g0, *W("q_w", j), q_scale, "q_proj" + tag)
            s["o"], s["lse"] = _attention_fwd(s["q"], kv, slopes, "attention" + tag)
            s["mix"], x_mid = _matmul_norm_residual(s["o"][None], *W("o_w", j), gain(layer, 1), xs, "o_proj" + tag)
        s["x_mid"] = x_mid
        if layer == 0:
            gather_wait(in_flight, x_mid)
            in_flight, g2 = gather_start(group(1), x_mid, "_l1")
        elif layer + 1 < depth:
            in_flight, g2 = gather_mid(in_flight, x_mid)
        s["gate"], s["up"], s["a"], s["fn"] = _ffn_in_swiglu(x_mid, gain(layer, 2) + g2, *W("ffn_in_w", layer), "ffn_in" + tag)
        if layer == 0:
            in_flight, g3 = gather_mid(in_flight, s["a"])
        s["ff"], xs = _matmul_norm_residual(s["a"], *W("ffn_out_w", layer), gain(layer, 3) + g3, x_mid, "ffn_out" + tag)
        saved.append(s)

    last = saved[-1]
    sq, dx_out, d_ff, dg3 = _loss_head(xs, target, last["ff"], gain(depth - 1, 3), "loss_head")
    loss = lax.psum(sq[0, 0] * (0.5 / d), ("x", "y", "c"))

    dgain = {(depth - 1, 3): dg3}
    dtaps = {}
    grads = {k: [None] * big[k].shape[0] for k in names}
    dkv_parts = []
    scattering = []

    def scatter_start(keys, tag):
        parts = [grads[k][i] for k, i in keys]
        zones = [lax.empty(p.shape, p.dtype) for p in parts]
        send_sems, recv_sems, parts, zones, tok = _send_start(parts, zones, small_all, "scatter_start" + tag, "scatter")
        scattering.append((keys, tag, send_sems, recv_sems, parts, zones))
        return tok[0, 0]

    for layer in reversed(range(depth)):
        tag = f"_l{layer}"
        s = saved[layer]
        dh = _swiglu_bwd(d_ff, *W("ffn_out_w", layer), s["gate"], s["up"], "swiglu_bwd" + tag)
        rows_out = big["ffn_out_w"].shape[1]
        grads["ffn_out_w"][layer] = _grad_weight(
            [pl.BlockSpec((None, tgb, fc), lambda c, i: (c, i, 0))], [s["a"]], _ident,
            [pl.BlockSpec((tgb, d), lambda c, i: (i, 0))], [d_ff], _ident,
            4, (fc, d), pl.BlockSpec((2, rows_out, d), lambda c, i: (c, 0, 0)), _sds((N_DEV, rows_out, d), BF16), t,
            "grad_ffn_out" + tag, tt=tgb)
        grads["ffn_in_w"][layer] = _grad_weight(
            [pl.BlockSpec((None, None, tgb, fc), lambda j, i: (j % 4, j // 4, i, 0))], [dh], _ident,
            [pl.BlockSpec((tgb, d), lambda j, i: (i, 0))], [s["fn"]], _ident,
            N_DEV, (fc, d), pl.BlockSpec((None, fc, d), lambda j, i: (j, 0, 0)), _sds((N_DEV, fc, d), BF16), t,
            "grad_ffn_in" + tag, tt=tgb)
        tok = scatter_start([("ffn_in_w", layer), ("ffn_out_w", layer)], "_ffn" + tag)
        dx_mid, d_mix, dg2, dg1 = _bwd_matmul_norms(
            [pl.BlockSpec((None, None, tm, fc), lambda i, j: (j % 4, j // 4, i, 0))], [dh], _ident, N_DEV,
            pl.BlockSpec((None, None, fc, d), lambda i, j: (j, 0, 0, 0)), W("ffn_in_w", layer)[0], _ident,
            s["x_mid"], gain(layer, 2) + tok, dx_out, s["mix"], gain(layer, 1), "ffn_in_bwd" + tag, w_transposed=True)
        dgain[(layer, 2)], dgain[(layer, 1)] = dg2, dg1
        full_rows = pl.BlockSpec((N_DEV, d // N_DEV, d), lambda j, i: (0, 0, 0))
        rows_w = lambda wname, idx: (pl.BlockSpec((N_DEV, None, d // N_DEV, d), lambda i, j: (0, 0, 0, 0)), W(wname, idx)[0],
                                     lambda w_ref: w_ref[...].reshape(d, d))
        if layer < n_a:
            d_z = _matmul_nt_rows(d_mix, *W("conv_out_w", layer), BF16, "conv_out_bwd" + tag)
            grads["conv_out_w"][layer] = _grad_weight(
                [pl.BlockSpec((tg, d), lambda j, i: (i, 0))], [s["z"]], _ident,
                [pl.BlockSpec((tg, d), lambda j, i: (i, 0))], [d_mix], _ident,
                1, (d, d), full_rows, _sds((N_DEV, d // N_DEV, d), BF16), t, "grad_conv_out" + tag)
            d_p, dtaps[layer] = _conv_bwd(s["p"], d_z, taps(layer), "conv_bwd" + tag)
            grads["conv_in_w"][layer] = _grad_weight(
                [pl.BlockSpec((d, tgb), lambda j, i: (0, i))], [s["xn_t"]], _ident,
                [pl.BlockSpec((tgb, 2 * cb), lambda j, i: (i, j))], [d_p], _ident,
                N_DEV // 2, (d, 2 * cb), pl.BlockSpec((2, d, cb), lambda j, i: (j, 0, 0)), _sds((N_DEV, d, cb), BF16), t,
                "grad_conv_in" + tag, a_transposed=True, tt=tgb)
            a_specs, a_args, a_tile, n_steps = [pl.BlockSpec((tm, N_DEV * cb), lambda i, j: (i, 0))], [d_p], _ident, 1
            w_spec = pl.BlockSpec((N_DEV, None, d, cb), lambda i, j: (0, 0, 0, 0))
            w_arg = W("conv_in_w", layer)[0]
            w_mat = lambda w_ref: jnp.concatenate([w_ref[k] for k in range(N_DEV)], axis=1)
            resid = dx_mid
        else:
            j_b = layer - n_a
            d_o = _matmul_nt_rows(d_mix, *W("o_w", j_b), F32, "o_proj_bwd" + tag)
            grads["o_w"][j_b] = _grad_weight(
                [pl.BlockSpec((tg, d), lambda j, i: (i, 0))], [s["o"]], _ident,
                [pl.BlockSpec((tg, d), lambda j, i: (i, 0))], [d_mix], _ident,
                1, (d, d), full_rows, _sds((N_DEV, d // N_DEV, d), BF16), t, "grad_o" + tag)
            dk_in, dv_in = dkv_parts[0] if dkv_parts else (None, None)
            dq, dk, dv = _attention_bwd(s["q"], kv, s["o"], s["lse"], d_o, dk_in, dv_in, slopes, q_scale, "attention_bwd" + tag)
            dkv_parts = [(dk, dv)]
            heads_spec = pl.BlockSpec((hp, tg, LANES), lambda j, i: (0, i, 0))
            grads["q_w"][j_b] = _grad_weight(
                [pl.BlockSpec((tg, d), lambda j, i: (i, 0))], [s["xn"]], _ident,
                [heads_spec], [dq], _heads_tile,
                1, (d, d), full_rows, _sds((N_DEV, d // N_DEV, d), BF16), t, "grad_q" + tag)
            a_specs, a_args, a_tile, n_steps = [pl.BlockSpec((hp, tm, LANES), lambda i, j: (0, i, 0))], [dq], _heads_tile, 1
            w_spec, w_arg, w_mat = rows_w("q_w", j_b)
            resid = dx_mid
            if layer == n_a:
                pieces = kvb // LANES
                halves = []
                for src in (0, 1):
                    halves.append([part[src] for part in dkv_parts])
                n_half = len(dkv_parts)
                kv_args = [arr for src in (0, 1) for arr in halves[src]]

                def kv_block(src, j):
                    return jnp.where((j // 4) == src, j % 4, 0)

                def kv_tile(j, *refs):
                    keys = _heads_to_rows(*refs[:n_half])
                    vals = _heads_to_rows(*refs[n_half:])
                    return jnp.where(j < 4, keys, vals)

                kv_specs = [pl.BlockSpec((pieces, tm, LANES), functools.partial(lambda i, j, src: (kv_block(src, j), i, 0), src=src))
                            for src in (0, 1) for _ in range(n_half)]
                resid, dgain["kv"] = _bwd_matmul_norms(
                    kv_specs, kv_args, kv_tile, N_DEV,
                    pl.BlockSpec((None, None, d, kvb), lambda i, j: (j, 0, 0, 0)), W("kv_w", 0)[0], _ident,
                    s["x_in"], g_kv, dx_mid, None, None, "kv_proj_bwd")
                kv_b_specs = [pl.BlockSpec((pieces, tg, LANES), functools.partial(lambda j, i, src: (kv_block(src, j), i, 0), src=src))
                              for src in (0, 1) for _ in range(n_half)]
                grads["kv_w"][0] = _grad_weight(
                    [pl.BlockSpec((d, tg), lambda j, i: (0, i))], [kvn_t], _ident,
                    kv_b_specs, kv_args, kv_tile,
                    N_DEV, (d, kvb), pl.BlockSpec((None, d, kvb), lambda j, i: (j, 0, 0)), _sds((N_DEV, d, kvb), BF16), t,
                    "grad_kv", a_transposed=True)
        tok = scatter_start([key for key in group(layer) if not key[0].startswith("ffn")], "_mix" + tag)
        if layer > 0:
            prev = saved[layer - 1]
            dx_out, d_ff, dg0, dg3p = _bwd_matmul_norms(
                a_specs, a_args, a_tile, n_steps, w_spec, w_arg, w_mat,
                s["x_in"], gain(layer, 0) + tok, resid, prev["ff"], gain(layer - 1, 3), "mixer_in_bwd" + tag)
            dgain[(layer, 0)], dgain[(layer - 1, 3)] = dg0, dg3p
        else:
            grad_x, dg0 = _bwd_matmul_norms(
                a_specs, a_args, a_tile, n_steps, w_spec, w_arg, w_mat,
                s["x_in"], gain(layer, 0), resid, None, None, "mixer_in_bwd" + tag)
            dgain[(layer, 0)] = dg0

    small_grad = jnp.concatenate(
        [dgain[(layer, k)] for layer in range(depth) for k in range(4)] + [dtaps[layer] for layer in range(n_a)]
        + [dgain["kv"]] + [jnp.zeros((small_rows - n_small - 1, d), F32)], axis=0)
    small_grads_all = _all_gather([small_grad], "gather_small_grads")[0]
    lo = dev * (d // N_DEV)

    def pack(ng, cwp, kvg):
        rows = jnp.concatenate([ng.reshape(4 * depth, -1), cwp.reshape(3 * n_a, -1)], axis=0)
        z = lax.dynamic_update_slice(jnp.zeros((small_rows, d), F32), rows, (0, lo))
        return lax.dynamic_update_slice(z, kvg[None], (n_small, 0))

    w_small = lax.dynamic_update_slice(small_all, g_kv, (n_small, 0))
    m_small, v_small = pack(m_norm_g, m_conv_w, m_kv_norm_g), pack(v_norm_g, v_conv_w, v_kv_norm_g)
    sm = _small_adamw(small_grads_all, w_small, m_small, v_small, "adamw_small")

    def unpack(a):
        mine = lax.dynamic_slice(a, (0, lo), (small_rows, d // N_DEV))
        return (mine[:4 * depth].reshape(norm_g.shape), mine[4 * depth:n_small].reshape(conv_w.shape), a[n_small])

    small_out = [unpack(a) for a in sm]

    moments = {"conv_in_w": (m_conv_in_w, v_conv_in_w), "conv_out_w": (m_conv_out_w, v_conv_out_w),
               "kv_w": (m_kv_w[None], v_kv_w[None]), "q_w": (m_q_w, v_q_w), "o_w": (m_o_w, v_o_w),
               "ffn_in_w": (swap(m_ffn_in_w), swap(v_ffn_in_w)), "ffn_out_w": (m_ffn_out_w, v_ffn_out_w)}
    landed = {k: [None] * big[k].shape[0] for k in names}
    for keys, tag, send_sems, recv_sems, parts, zones in scattering:
        parts, zones = _send_wait(send_sems, recv_sems, parts, zones, grad_x, "scatter_wait" + tag, "scatter")
        for (k, i), part, zone in zip(keys, parts, zones):
            landed[k][i] = (part, zone)
    res = {k: _sum_adamw(slot, [p for p, _ in landed[k]], [z for _, z in landed[k]], big[k], moments[k][0], moments[k][1],
                         "adamw_" + k) for k in names}

    def big_out(k, which):
        out = res[k][which]
        return out[0] if k == "kv_w" else swap(out) if k == "ffn_in_w" else out

    out_names = ["norm_g", "conv_in_w", "conv_w", "conv_out_w", "kv_norm_g", "kv_w", "q_w", "o_w", "ffn_in_w", "ffn_out_w"]
    small_pos = {"norm_g": 0, "conv_w": 1, "kv_norm_g": 2}
    outs = [loss, grad_x[None]]
    for which in range(4):
        for k in out_names:
            outs.append(small_out[which][small_pos[k]] if k in small_pos else big_out(k, which))
    return tuple(outs)
```
